```python
import math
import jax, jax.numpy as jnp
from jax import lax
import numpy as np

D_MODEL = 1024
BATCH = 4
SEQ = 8192
DEPTH = 2

GRID_W = 64
CTX_LEN = 256
HEAD_DIM = 64
N_Q_HEADS = 8
N_KV_HEADS = 4
GROUP = N_Q_HEADS // N_KV_HEADS
Q_W = N_Q_HEADS * HEAD_DIM
KV_W = N_KV_HEADS * HEAD_DIM
QKV_W = Q_W + 2 * KV_W
CONV_W = D_MODEL // 2
MIX_IN_W = QKV_W + 3 * CONV_W
MIX_OUT_W = Q_W + CONV_W
Q_BLOCK = 128
WINDOW = 128
ROPE_THETA = 10000.0
FILT_EMB = 33
FILT_WIDTH = 64
DECAY_TARGET = 1e-2
FAST_DECAY_PCT = 0.3
SLOW_DECAY_PCT = 1.5
DECAY_SHIFT = 0.05
D_FF = 2816
N_EVEN = (DEPTH + 1) // 2
N_ODD = DEPTH // 2
NEG_INF = -1e30
RMS_EPS = 1e-6

kernel_name = 'hybrid_flow_backbone'


def rmsnorm(x, g):
    xf = x.astype(jnp.float32)
    xf = xf * lax.rsqrt(jnp.mean(xf * xf, axis=-1, keepdims=True) + RMS_EPS)
    return xf.astype(x.dtype) * g


def modulate(x, g, shift, scale):
    return rmsnorm(x, g) * (1 + scale) + shift


def dwconv3(x, w):
    xp = jnp.pad(x, ((0, 0), (1, 1), (0, 0)))
    return xp[:, :-2] * w[0] + xp[:, 1:-1] * w[1] + xp[:, 2:] * w[2]


def axial_angles(rows):
    row = jnp.repeat(jnp.arange(rows), GRID_W).astype(jnp.float32)
    col = jnp.tile(jnp.arange(GRID_W), rows).astype(jnp.float32)
    n_freq = HEAD_DIM // 4
    inv = ROPE_THETA ** (-jnp.arange(n_freq, dtype=jnp.float32) / n_freq)
    return row[:, None] * inv, col[:, None] * inv


def rope_half(x, ang):
    x1, x2 = jnp.split(x, 2, axis=-1)
    cos = jnp.cos(ang).astype(x.dtype)
    sin = jnp.sin(ang).astype(x.dtype)
    return jnp.concatenate([x1 * cos - x2 * sin, x1 * sin + x2 * cos], axis=-1)


def rope_2d(x, ang_row, ang_col):
    xr, xc = jnp.split(x, 2, axis=-1)
    return jnp.concatenate([rope_half(xr, ang_row), rope_half(xc, ang_col)], axis=-1)


def split_kv(z):
    b, s, _ = z.shape
    k = z[..., :KV_W].reshape(b, s, N_KV_HEADS, HEAD_DIM).transpose(0, 2, 1, 3)
    v = z[..., KV_W:].reshape(b, s, N_KV_HEADS, HEAD_DIM).transpose(0, 2, 1, 3)
    return k, v


def split_qkv(z):
    b, s, _ = z.shape
    q = z[..., :Q_W].reshape(b, s, N_KV_HEADS, GROUP, HEAD_DIM).transpose(0, 2, 3, 1, 4)
    k, v = split_kv(z[..., Q_W:QKV_W])
    return q, k, v


def merge_heads(o):
    b, h, g, s, d = o.shape
    return o.transpose(0, 3, 1, 2, 4).reshape(b, s, h * g * d)


def attend(q, k, v, sink=None):
    s = jnp.einsum('bhgqd,bhkd->bhgqk', q, k).astype(jnp.float32) * HEAD_DIM ** -0.5
    if sink is None:
        p = jax.nn.softmax(s, axis=-1)
    else:
        sk = jnp.broadcast_to(sink.astype(jnp.float32)[None, :, :, None, None], s.shape[:-1] + (1,))
        p = jax.nn.softmax(jnp.concatenate([s, sk], axis=-1), axis=-1)[..., :-1]
    return jnp.einsum('bhgqk,bhkd->bhgqd', p.astype(v.dtype), v)


def dense_block_attention(q, k, v, kc, vc):
    b, h, g, s, d = q.shape
    nb = s // Q_BLOCK
    kk = jnp.concatenate([k, kc], axis=2)
    vv = jnp.concatenate([v, vc], axis=2)
    qb = q.reshape(b, h, g, nb, Q_BLOCK, d).transpose(3, 0, 1, 2, 4, 5)
    out = lax.map(lambda qblk: attend(qblk, kk, vv), qb)
    return out.transpose(1, 2, 3, 0, 4, 5).reshape(b, h, g, s, d)


def banded_attention(q, k, v, kc, vc, sink):
    b, h, g, s, d = q.shape
    nb = s // Q_BLOCK
    qb = q.reshape(b, h, g, nb, Q_BLOCK, d)

    def band(t):
        tp = jnp.pad(t.reshape(b, h, nb, Q_BLOCK, d), ((0, 0), (0, 0), (1, 1), (0, 0), (0, 0)))
        return jnp.concatenate([tp[:, :, :-2], tp[:, :, 1:-1], tp[:, :, 2:]], axis=3)

    kb, vb = band(k), band(v)
    scale = HEAD_DIM ** -0.5
    s_loc = jnp.einsum('bhgnqd,bhnkd->bhgnqk', qb, kb).astype(jnp.float32) * scale
    s_ctx = jnp.einsum('bhgnqd,bhkd->bhgnqk', qb, kc).astype(jnp.float32) * scale
    blk = jnp.arange(nb)[:, None, None]
    qi = blk * Q_BLOCK + jnp.arange(Q_BLOCK)[None, :, None]
    kj = (blk - 1) * Q_BLOCK + jnp.arange(3 * Q_BLOCK)[None, None, :]
    ok = (jnp.abs(kj - qi) <= WINDOW) & (kj >= 0) & (kj < s)
    s_loc = jnp.where(ok, s_loc, NEG_INF)
    sk = jnp.broadcast_to(sink.astype(jnp.float32)[None, :, :, None, None, None], s_loc.shape[:-1] + (1,))
    p = jax.nn.softmax(jnp.concatenate([s_loc, s_ctx, sk], axis=-1), axis=-1).astype(v.dtype)
    n_loc = 3 * Q_BLOCK
    o = (jnp.einsum('bhgnqk,bhnkd->bhgnqd', p[..., :n_loc], vb)
         + jnp.einsum('bhgnqk,bhkd->bhgnqd', p[..., n_loc:-1], vc))
    return o.reshape(b, h, g, s, d)


def implicit_filter(n, w1, b1, w2, b2, w3, b3, w4, freq):
    t01 = jnp.linspace(0.0, 1.0, n, dtype=jnp.float32)[:, None]
    bands = (FILT_EMB - 1) // 2
    w = 2.0 * math.pi * jnp.arange(n, dtype=jnp.float32)[:, None] / n
    f = jnp.linspace(1e-4, bands - 1, bands, dtype=jnp.float32)[None, :]
    feats = jnp.concatenate([t01, jnp.cos(f * w), -jnp.sin(f * w)], axis=-1)
    hid = jnp.sin(freq * (feats @ w1 + b1))
    hid = jnp.sin(freq * (hid @ w2 + b2))
    hid = jnp.sin(freq * (hid @ w3 + b3))
    hf = (hid @ w4).astype(jnp.float32)
    deltas = jnp.abs(jnp.linspace(math.log(DECAY_TARGET) / SLOW_DECAY_PCT,
                                  math.log(DECAY_TARGET) / FAST_DECAY_PCT, CONV_W, dtype=jnp.float32))
    window = jnp.exp(-t01 * deltas) + DECAY_SHIFT
    h_fwd = hf[:, :CONV_W] * window
    h_bwd = hf[:, CONV_W:] * window
    k = jnp.concatenate([h_fwd, jnp.zeros((1, CONV_W), jnp.float32), h_bwd[:0:-1]], axis=0)
    return k / jnp.sum(jnp.abs(k), axis=0, keepdims=True)


def hyena(z, conv_w, conv_b, w1, b1, w2, b2, w3, b3, w4, freq, bias_d):
    n = z.shape[1]
    z = dwconv3(z, conv_w) + conv_b
    x0, x1, v = jnp.split(z, 3, axis=-1)
    k = implicit_filter(n, w1, b1, w2, b2, w3, b3, w4, freq)
    u = (v * x1).astype(jnp.float32)
    y = jnp.fft.irfft(jnp.fft.rfft(u, n=2 * n, axis=1) * jnp.fft.rfft(k, n=2 * n, axis=0)[None],
                      n=2 * n, axis=1)[:, :n]
    y = y + u * bias_d
    return (y * x0.astype(jnp.float32)).astype(z.dtype)


def short_conv_mixer(z, conv_w):
    bg, cg, xv = jnp.split(z, 3, axis=-1)
    return bg * dwconv3(cg * xv, conv_w)


def conv_ffn(h, w_up, conv_w, conv_b, w_down):
    a, v = jnp.split(h @ w_up, 2, axis=-1)
    a = dwconv3(a, conv_w) + conv_b
    return (jax.nn.gelu(a) * v) @ w_down


def setup_inputs(seed: int = 0) -> dict:
    key = jax.random.key(seed)
    keys = iter(jax.random.split(key, 40))

    def nrm(shape, scale):
        return jax.random.normal(next(keys), shape, jnp.float32) * scale

    def gain(shape):
        return 1.0 + nrm(shape, 0.02)

    return {
        'x': nrm((BATCH, SEQ, D_MODEL), 1.0),
        'c': nrm((BATCH, D_MODEL), 1.0),
        'ctx': nrm((BATCH, CTX_LEN, D_MODEL), 1.0),
        'c_ctx': nrm((D_MODEL,), 1.0),
        'ada_w': nrm((DEPTH, D_MODEL, 6 * D_MODEL), 0.5 * D_MODEL ** -0.5),
        'ada_b': nrm((DEPTH, 6 * D_MODEL), 0.01),
        'norm_mix': gain((DEPTH, D_MODEL)),
        'norm_ffn': gain((DEPTH, D_MODEL)),
        'mix_w_in': nrm((DEPTH, D_MODEL, MIX_IN_W), D_MODEL ** -0.5),
        'mix_w_out': nrm((DEPTH, MIX_OUT_W, D_MODEL), MIX_OUT_W ** -0.5),
        'attn_q_norm': gain((DEPTH, HEAD_DIM)),
        'attn_k_norm': gain((DEPTH, HEAD_DIM)),
        'swa_sink': nrm((N_ODD, N_Q_HEADS), 0.5),
        'hy_conv_w': nrm((N_EVEN, 3, 3 * CONV_W), 3 ** -0.5),
        'hy_conv_b': nrm((N_EVEN, 3 * CONV_W), 0.01),
        'hy_w1': nrm((N_EVEN, FILT_EMB, FILT_WIDTH), FILT_EMB ** -0.5),
        'hy_b1': nrm((N_EVEN, FILT_WIDTH), 0.1),
        'hy_w2': nrm((N_EVEN, FILT_WIDTH, FILT_WIDTH), FILT_WIDTH ** -0.5),
        'hy_b2': nrm((N_EVEN, FILT_WIDTH), 0.1),
        'hy_w3': nrm((N_EVEN, FILT_WIDTH, FILT_WIDTH), FILT_WIDTH ** -0.5),
        'hy_b3': nrm((N_EVEN, FILT_WIDTH), 0.1),
        'hy_w4': nrm((N_EVEN, FILT_WIDTH, 2 * CONV_W), FILT_WIDTH ** -0.5),
        'hy_freq': gain((N_EVEN, FILT_WIDTH)),
        'hy_bias_d': nrm((N_EVEN, CONV_W), 1.0),
        'sc_conv_w': nrm((N_ODD, 3, CONV_W), 3 ** -0.5),
        'ffn_w_up': nrm((DEPTH, D_MODEL, 2 * D_FF), D_MODEL ** -0.5),
        'ffn_conv_w': nrm((DEPTH, 3, D_FF), 3 ** -0.5),
        'ffn_conv_b': nrm((DEPTH, D_FF), 0.01),
        'ffn_w_down': nrm((DEPTH, D_FF, D_MODEL), D_FF ** -0.5),
    }


def reference(x, c, ctx, c_ctx, ada_w, ada_b, norm_mix, norm_ffn, mix_w_in, mix_w_out,
              attn_q_norm, attn_k_norm, swa_sink, hy_conv_w, hy_conv_b, hy_w1, hy_b1, hy_w2, hy_b2,
              hy_w3, hy_b3, hy_w4, hy_freq, hy_bias_d, sc_conv_w, ffn_w_up, ffn_conv_w, ffn_conv_b,
              ffn_w_down):
    rows = x.shape[1] // GRID_W
    ang_r, ang_c = axial_angles(rows)
    xc = ctx
    for i in range(DEPTH):
        last = i == DEPTH - 1
        j = i // 2
        mod = jax.nn.silu(c) @ ada_w[i] + ada_b[i]
        mod_c = jax.nn.silu(c_ctx) @ ada_w[i] + ada_b[i]
        sh1, sc1, g1, sh2, sc2, g2 = [m[:, None, :] for m in jnp.split(mod, 6, axis=-1)]
        sh1c, sc1c, g1c, sh2c, sc2c, g2c = jnp.split(mod_c, 6, axis=-1)
        w_in = mix_w_in[i]

        h = modulate(x, norm_mix[i], sh1, sc1)
        z = h @ w_in
        q, k, v = split_qkv(z)
        q = rope_2d(rmsnorm(q, attn_q_norm[i]), ang_r, ang_c)
        k = rope_2d(rmsnorm(k, attn_k_norm[i]), ang_r, ang_c)

        hc = modulate(xc, norm_mix[i], sh1c, sc1c)
        if last:
            kc, vc = split_kv(hc @ w_in[:, Q_W:QKV_W])
        else:
            zc = hc @ w_in
            qc, kc, vc = split_qkv(zc)
            qc = rmsnorm(qc, attn_q_norm[i])
        kc = rmsnorm(kc, attn_k_norm[i])

        if i % 2 == 0:
            hy_args = (hy_conv_w[j], hy_conv_b[j], hy_w1[j], hy_b1[j], hy_w2[j], hy_b2[j],
                       hy_w3[j], hy_b3[j], hy_w4[j], hy_freq[j], hy_bias_d[j])
            o_attn = dense_block_attention(q, k, v, kc, vc)
            o_conv = hyena(z[..., QKV_W:], *hy_args)
            if not last:
                oc_attn = attend(qc, kc, vc)
                oc_conv = hyena(zc[..., QKV_W:], *hy_args)
        else:
            sink = swa_sink[j].reshape(N_KV_HEADS, GROUP)
            o_attn = banded_attention(q, k, v, kc, vc, sink)
            o_conv = short_conv_mixer(z[..., QKV_W:], sc_conv_w[j])
            if not last:
                oc_attn = attend(qc, kc, vc, sink)
                oc_conv = short_conv_mixer(zc[..., QKV_W:], sc_conv_w[j])

        y = jnp.concatenate([merge_heads(o_attn), o_conv], axis=-1) @ mix_w_out[i]
        x = x + g1 * y
        x = x + g2 * conv_ffn(modulate(x, norm_ffn[i], sh2, sc2),
                              ffn_w_up[i], ffn_conv_w[i], ffn_conv_b[i], ffn_w_down[i])
        if not last:
            yc = jnp.concatenate([merge_heads(oc_attn), oc_conv], axis=-1) @ mix_w_out[i]
            xc = xc + g1c * yc
            xc = xc + g2c * conv_ffn(modulate(xc, norm_ffn[i], sh2c, sc2c),
                                     ffn_w_up[i], ffn_conv_w[i], ffn_conv_b[i], ffn_w_down[i])
    return x
```

```python
import functools
import math

import numpy as np
import jax
import jax.numpy as jnp
from jax import lax
from jax.experimental import pallas as pl
from jax.experimental.pallas import tpu as pltpu

F32 = jnp.float32
BF16 = jnp.bfloat16
HI = lax.Precision.HIGHEST

D_MODEL = 1024
GRID_W = 64
HEAD_DIM = 64
N_Q_HEADS = 8
N_KV_HEADS = 4
GROUP = N_Q_HEADS // N_KV_HEADS
Q_W = N_Q_HEADS * HEAD_DIM
KV_W = N_KV_HEADS * HEAD_DIM
QKV_W = Q_W + 2 * KV_W
CONV_W = D_MODEL // 2
MIX_IN_W = QKV_W + 3 * CONV_W
WINDOW = 128
ROPE_THETA = 10000.0
FILT_EMB = 33
FILT_WIDTH = 64
DECAY_TARGET = 1e-2
FAST_DECAY_PCT = 0.3
SLOW_DECAY_PCT = 1.5
DECAY_SHIFT = 0.05
D_FF = 2816
NEG_INF = -1e30
RMS_EPS = 1e-6
LOG2E = 1.4426950408889634

LANES = 128
SUBLANES = 8
DFT_N1 = 128
VMEM_LIMIT_MB = 56


def _params(sem, vmem_mb=VMEM_LIMIT_MB):
    return pltpu.CompilerParams(dimension_semantics=sem, vmem_limit_bytes=vmem_mb * 1024 * 1024)


def _full(shape):
    nd = len(shape)
    return pl.BlockSpec(shape, lambda *_: (0,) * nd)


def _mods_kernel(c_ref, w_ref, b_ref, o_ref):
    c = c_ref[...]
    s = c / (1.0 + jnp.exp(-c))
    o_ref[0] = jnp.dot(s, w_ref[0], preferred_element_type=F32, precision=HI) + b_ref[0]


def _mods(cvec, ada_w, ada_b):
    depth, d, n6 = ada_w.shape
    tn = 1536
    return pl.pallas_call(
        _mods_kernel,
        grid=(depth, n6 // tn),
        in_specs=[_full((SUBLANES, d)),
                  pl.BlockSpec((1, d, tn), lambda l, j: (l, 0, j)),
                  pl.BlockSpec((1, 1, tn), lambda l, j: (l, 0, j))],
        out_specs=pl.BlockSpec((1, SUBLANES, tn), lambda l, j: (l, 0, j)),
        out_shape=jax.ShapeDtypeStruct((depth, SUBLANES, n6), F32),
        compiler_params=_params(("arbitrary", "arbitrary")),
        name="ada_mods",
    )(cvec, ada_w, ada_b.reshape(depth, 1, n6))


def _modulated_norm(x, gain, shift, scale):
    ms = jnp.mean(x * x, axis=-1, keepdims=True)
    return (x * lax.rsqrt(ms + RMS_EPS)) * gain * (1.0 + scale) + shift


def _shifted_rows(xm, prev8, next8, i, nt):
    pz = jnp.where(i > 0, prev8, 0.0)
    nz = jnp.where(i < nt - 1, next8, 0.0)
    ext = jnp.concatenate([pz, xm, nz], axis=0)
    r = ext.shape[0]
    xp = pltpu.roll(ext, 1, 0)[SUBLANES:r - SUBLANES]
    xn = pltpu.roll(ext, r - 1, 0)[SUBLANES:r - SUBLANES]
    return xp, xn


def _halo_specs(tm, t, width):
    nb = tm // SUBLANES
    last = t // SUBLANES - 1
    return [pl.BlockSpec((1, tm, width), lambda b, i: (b, i, 0)),
            pl.BlockSpec((1, SUBLANES, width), lambda b, i: (b, jnp.maximum(i * nb - 1, 0), 0)),
            pl.BlockSpec((1, SUBLANES, width), lambda b, i: (b, jnp.minimum((i + 1) * nb, last), 0))]


def _inproj_kernel(x_ref, sh_ref, sc_ref, g_ref, w_ref, qg_ref, kg_ref, bd_ref, cos_ref, sa_ref, sb_ref,
                   q_ref, k_ref, v_ref, zc_ref):
    h = _modulated_norm(x_ref[0], g_ref[...], sh_ref[0], sc_ref[0])
    z = jnp.dot(h.astype(BF16), w_ref[...], preferred_element_type=F32)
    zc_ref[0] = z[:, QKV_W:]
    cos, sa, sb = cos_ref[...], sa_ref[...], sb_ref[...]

    def head_norm_rope(t, gain, bd):
        ssq = jnp.dot((t * t).astype(BF16), bd, preferred_element_type=F32)
        t = t * lax.rsqrt(ssq * (1.0 / HEAD_DIM) + RMS_EPS) * gain
        outs = []
        for j in range(t.shape[1] // LANES):
            tj = t[:, j * LANES:(j + 1) * LANES]
            outs.append(tj * cos + pltpu.roll(tj, LANES - 16, 1) * sa + pltpu.roll(tj, 16, 1) * sb)
        return outs

    qs = head_norm_rope(z[:, :Q_W], qg_ref[...], bd_ref[...])
    ks = head_norm_rope(z[:, Q_W:Q_W + KV_W], kg_ref[...], bd_ref[:KV_W, :KV_W])
    qscale = HEAD_DIM ** -0.5 * LOG2E
    for j, qj in enumerate(qs):
        qj = (qj * qscale).astype(BF16)
        q_ref[0, 2 * j] = qj[:, :HEAD_DIM]
        q_ref[0, 2 * j + 1] = qj[:, HEAD_DIM:]
    for j, kj in enumerate(ks):
        kj = kj.astype(BF16)
        k_ref[0, 2 * j] = kj[:, :HEAD_DIM]
        k_ref[0, 2 * j + 1] = kj[:, HEAD_DIM:]
    tm = z.shape[0]
    ones_col = (lax.broadcasted_iota(jnp.int32, (tm, HEAD_DIM), 1) == 0).astype(F32)
    for hh in range(N_KV_HEADS):
        vh = z[:, Q_W + KV_W + hh * HEAD_DIM:Q_W + KV_W + (hh + 1) * HEAD_DIM]
        v_ref[0, hh] = jnp.concatenate([vh, ones_col], axis=1).astype(BF16)


def _inproj(x, sh, sc, gain, w_bf, qg, kg, bd, cos, sa, sb):
    bx, t, d = x.shape
    tm = min(512, t)
    vec = pl.BlockSpec((1, 1, d), lambda b, i: (b, 0, 0))
    tab = pl.BlockSpec((tm, LANES), lambda b, i: (i, 0))
    return pl.pallas_call(
        _inproj_kernel,
        grid=(bx, t // tm),
        in_specs=[pl.BlockSpec((1, tm, d), lambda b, i: (b, i, 0)), vec, vec, _full((1, d)),
                  _full((d, MIX_IN_W)), _full((1, Q_W)), _full((1, KV_W)), _full((Q_W, Q_W)), tab, tab, tab],
        out_specs=[pl.BlockSpec((1, N_Q_HEADS, tm, HEAD_DIM), lambda b, i: (b, 0, i, 0)),
                   pl.BlockSpec((1, N_KV_HEADS, tm, HEAD_DIM), lambda b, i: (b, 0, i, 0)),
                   pl.BlockSpec((1, N_KV_HEADS, tm, LANES), lambda b, i: (b, 0, i, 0)),
                   pl.BlockSpec((1, tm, 3 * CONV_W), lambda b, i: (b, i, 0))],
        out_shape=[jax.ShapeDtypeStruct((bx, N_Q_HEADS, t, HEAD_DIM), BF16),
                   jax.ShapeDtypeStruct((bx, N_KV_HEADS, t, HEAD_DIM), BF16),
                   jax.ShapeDtypeStruct((bx, N_KV_HEADS, t, LANES), BF16),
                   jax.ShapeDtypeStruct((bx, t, 3 * CONV_W), F32)],
        compiler_params=_params(("arbitrary", "arbitrary")),
        name="in_proj",
    )(x, sh, sc, gain, w_bf, qg, kg, bd, cos, sa, sb)


def _softmax_step(q, ks, vs, m, acc, mask=None):
    s = lax.dot_general(q, ks, (((1,), (1,)), ((), ())), preferred_element_type=F32)
    if mask is not None:
        s = jnp.where(mask, s, NEG_INF)
    m_new = jnp.maximum(m, jnp.max(s, axis=1, keepdims=True))
    p = jnp.exp2(s - m_new).astype(BF16)
    acc = acc * jnp.exp2(m - m_new) + jnp.dot(p, vs, preferred_element_type=F32)
    return m_new, acc


def _attn_finish(acc, tq, o_ref, extra_den=None):
    den = acc[:, HEAD_DIM:HEAD_DIM + 1]
    if extra_den is not None:
        den = den + extra_den
    o = acc[:, :HEAD_DIM] / den
    o_ref[0] = jnp.concatenate([o[:tq], o[tq:]], axis=1).astype(o_ref.dtype)


def _sink_column(sink_ref, hh, m_rows, tq, m, acc):
    r1 = lax.broadcasted_iota(jnp.int32, (m_rows, 1), 0)
    sink = jnp.where(r1 < tq, sink_ref[hh, 0], sink_ref[hh, 1]) * LOG2E
    m_new = jnp.maximum(m, sink)
    return acc * jnp.exp2(m - m_new), jnp.exp2(sink - m_new)


def _flash_kernel(*refs, tq, tk, n_main, has_ctx, has_sink):
    o_ref = refs[-1]
    refs = list(refs[:-1])
    sink_ref = refs.pop(0) if has_sink else None
    q_ref, k_ref, v_ref = refs[:3]
    m_rows = GROUP * tq
    q = q_ref[0].reshape(m_rows, HEAD_DIM)

    def body(j, carry):
        off = pl.multiple_of(j * tk, tk)
        return _softmax_step(q, k_ref[0, 0, pl.ds(off, tk), :], v_ref[0, 0, pl.ds(off, tk), :], *carry)

    carry = (jnp.full((m_rows, 1), NEG_INF, F32), jnp.zeros((m_rows, LANES), F32))
    carry = lax.fori_loop(0, n_main, body, carry)
    if has_ctx:
        carry = _softmax_step(q, refs[3][0, 0], refs[4][0, 0], *carry)
    m, acc = carry
    extra = None
    if has_sink:
        acc, extra = _sink_column(sink_ref, pl.program_id(1), m_rows, tq, m, acc)
    _attn_finish(acc, tq, o_ref, extra_den=extra)


def _dense_attention(q, k, v, kc=None, vc=None, sink=None):
    bx, _, t, _ = q.shape
    s = k.shape[2]
    tq = min(256, t)
    tk = min(512, s)
    has_ctx = kc is not None
    has_sink = sink is not None
    kv = lambda n, w: pl.BlockSpec((1, 1, n, w), lambda b, h, i: (b, h, 0, 0))
    in_specs = [pl.BlockSpec((1, GROUP, tq, HEAD_DIM), lambda b, h, i: (b, h, i, 0)), kv(s, HEAD_DIM), kv(s, LANES)]
    args = [q, k, v]
    if has_ctx:
        in_specs += [kv(kc.shape[2], HEAD_DIM), kv(kc.shape[2], LANES)]
        args += [kc, vc]
    if has_sink:
        in_specs = [pl.BlockSpec(memory_space=pltpu.SMEM)] + in_specs
        args = [sink] + args
    return pl.pallas_call(
        functools.partial(_flash_kernel, tq=tq, tk=tk, n_main=s // tk, has_ctx=has_ctx, has_sink=has_sink),
        grid=(bx, N_KV_HEADS, t // tq),
        in_specs=in_specs,
        out_specs=pl.BlockSpec((1, tq, LANES), lambda b, h, i: (b, i, h)),
        out_shape=jax.ShapeDtypeStruct((bx, t, Q_W), BF16),
        compiler_params=_params(("arbitrary", "arbitrary", "arbitrary")),
        name="dense_attention",
    )(*args)


def _banded_kernel(sink_ref, q_ref, k_ref, v_ref, kc_ref, vc_ref, o_ref, *, tq, s_len):
    i = pl.program_id(2)
    hh = pl.program_id(1)
    m_rows = GROUP * tq
    span = tq + 2 * WINDOW
    q = q_ref[0].reshape(m_rows, HEAD_DIM)
    q0 = i * tq
    start = pl.multiple_of(jnp.clip(q0 - WINDOW, 0, s_len - span), WINDOW)
    row = lax.broadcasted_iota(jnp.int32, (m_rows, span), 0)
    qi = q0 + jnp.where(row >= tq, row - tq, row)
    kj = start + lax.broadcasted_iota(jnp.int32, (m_rows, span), 1)
    ok = jnp.abs(kj - qi) <= WINDOW
    carry = (jnp.full((m_rows, 1), NEG_INF, F32), jnp.zeros((m_rows, LANES), F32))
    carry = _softmax_step(q, k_ref[0, 0, pl.ds(start, span), :], v_ref[0, 0, pl.ds(start, span), :], *carry, mask=ok)
    m, acc = _softmax_step(q, kc_ref[0, 0], vc_ref[0, 0], *carry)
    acc, extra = _sink_column(sink_ref, hh, m_rows, tq, m, acc)
    _attn_finish(acc, tq, o_ref, extra_den=extra)


def _banded_attention(q, k, v, kc, vc, sink):
    bx, _, t, _ = q.shape
    sc = kc.shape[2]
    tq = 256
    kv = lambda n, w: pl.BlockSpec((1, 1, n, w), lambda b, h, i: (b, h, 0, 0))
    return pl.pallas_call(
        functools.partial(_banded_kernel, tq=tq, s_len=t),
        grid=(bx, N_KV_HEADS, t // tq),
        in_specs=[pl.BlockSpec(memory_space=pltpu.SMEM),
                  pl.BlockSpec((1, GROUP, tq, HEAD_DIM), lambda b, h, i: (b, h, i, 0)),
                  kv(t, HEAD_DIM), kv(t, LANES), kv(sc, HEAD_DIM), kv(sc, LANES)],
        out_specs=pl.BlockSpec((1, tq, LANES), lambda b, h, i: (b, i, h)),
        out_shape=jax.ShapeDtypeStruct((bx, t, Q_W), BF16),
        compiler_params=_params(("arbitrary", "arbitrary", "arbitrary")),
        name="banded_attention",
    )(sink, q, k, v, kc, vc)


def _hyena_pre_kernel(z_ref, zp_ref, zn_ref, w_ref, b_ref, u_ref, x0_ref):
    i = pl.program_id(1)
    zm = z_ref[0]
    zp, zn = _shifted_rows(zm, zp_ref[0], zn_ref[0], i, pl.num_programs(1))
    w = w_ref[...]
    c = zp * w[0:1] + zm * w[1:2] + zn * w[2:3] + b_ref[...]
    x0_ref[0] = c[:, :CONV_W]
    u_ref[0] = c[:, 2 * CONV_W:] * c[:, CONV_W:2 * CONV_W]


def _hyena_pre(zc, conv_w, conv_b):
    bx, t, w3 = zc.shape
    tm = min(512, t)
    out = jax.ShapeDtypeStruct((bx, t, CONV_W), F32)
    ospec = pl.BlockSpec((1, tm, CONV_W), lambda b, i: (b, i, 0))
    return pl.pallas_call(
        _hyena_pre_kernel,
        grid=(bx, t // tm),
        in_specs=_halo_specs(tm, t, w3) + [_full((3, w3)), _full((1, w3))],
        out_specs=[ospec, ospec],
        out_shape=[out, out],
        compiler_params=_params(("arbitrary", "arbitrary")),
        name="hyena_pre",
    )(zc, zc, zc, conv_w, conv_b.reshape(1, w3))


def _short_conv_kernel(z_ref, zp_ref, zn_ref, w_ref, o_ref):
    i = pl.program_id(1)
    prod = lambda z: z[:, CONV_W:2 * CONV_W] * z[:, 2 * CONV_W:]
    zm = z_ref[0]
    pm = prod(zm)
    pp, pn = _shifted_rows(pm, prod(zp_ref[0]), prod(zn_ref[0]), i, pl.num_programs(1))
    w = w_ref[...]
    o_ref[0] = (zm[:, :CONV_W] * (pp * w[0:1] + pm * w[1:2] + pn * w[2:3])).astype(o_ref.dtype)


def _short_conv(zc, conv_w):
    bx, t, w3 = zc.shape
    tm = min(512, t)
    return pl.pallas_call(
        _short_conv_kernel,
        grid=(bx, t // tm),
        in_specs=_halo_specs(tm, t, w3) + [_full((3, CONV_W))],
        out_specs=pl.BlockSpec((1, tm, CONV_W), lambda b, i: (b, i, 0)),
        out_shape=jax.ShapeDtypeStruct((bx, t, CONV_W), BF16),
        compiler_params=_params(("arbitrary", "arbitrary")),
        name="short_conv",
    )(zc, zc, zc, conv_w)


@functools.lru_cache(maxsize=None)
def _filter_features(n):
    j = np.arange(2 * n)
    d = np.where(j <= n, j, 2 * n - j)
    d = np.where(j == n, 0, d)
    bands = (FILT_EMB - 1) // 2
    t01 = np.linspace(0.0, 1.0, n)[d]
    w = 2.0 * np.pi * d.astype(np.float64) / n
    f = np.linspace(1e-4, bands - 1, bands)[None, :]
    feats = np.zeros((2 * n, LANES), np.float64)
    feats[:, 0] = t01
    feats[:, 1:1 + bands] = np.cos(f * w[:, None])
    feats[:, 1 + bands:FILT_EMB] = -np.sin(f * w[:, None])
    feats[:, 64] = t01
    feats[:, 65] = (j < n)
    feats[:, 66] = (j != n)
    return feats.astype(np.float32)


def _filter_kernel(f_ref, w1_ref, b1_ref, w2_ref, b2_ref, w3_ref, b3_ref, w4_ref, fr_ref, dl_ref, k_ref, s_ref):
    f = f_ref[...]
    fr = fr_ref[...]
    mm = lambda a, b: jnp.dot(a, b, preferred_element_type=F32, precision=HI)
    h = jnp.sin(fr * (mm(f, w1_ref[...]) + b1_ref[...]))
    h = jnp.sin(fr * (mm(h, w2_ref[...]) + b2_ref[...]))
    h = jnp.sin(fr * (mm(h, w3_ref[...]) + b3_ref[...]))
    hf = mm(h, w4_ref[...])
    win = jnp.exp(-f[:, 64:65] * dl_ref[...]) + DECAY_SHIFT
    k = jnp.where(f[:, 65:66] > 0.5, hf[:, :CONV_W], hf[:, CONV_W:]) * win * f[:, 66:67]
    k_ref[...] = k

    @pl.when(pl.program_id(0) == 0)
    def _():
        s_ref[...] = jnp.zeros_like(s_ref)

    s_ref[...] += jnp.sum(jnp.abs(k), axis=0, keepdims=True)


def _implicit_filter(n, w1, b1, w2, b2, w3, b3, w4, freq):
    feats = jnp.asarray(_filter_features(n))
    pad = LANES - FILT_WIDTH
    padc = lambda a: jnp.pad(a.reshape(1, -1), ((0, 0), (0, pad)))
    w1p = jnp.pad(w1, ((0, LANES - FILT_EMB), (0, pad)))
    w2p = jnp.pad(w2, ((0, pad), (0, pad)))
    w3p = jnp.pad(w3, ((0, pad), (0, pad)))
    w4p = jnp.pad(w4, ((0, pad), (0, 0)))
    deltas = np.abs(np.linspace(math.log(DECAY_TARGET) / SLOW_DECAY_PCT, math.log(DECAY_TARGET) / FAST_DECAY_PCT,
                                CONV_W)).astype(np.float32).reshape(1, CONV_W)
    tr = min(1024, 2 * n)
    sq = _full((LANES, LANES))
    vec = _full((1, LANES))
    return pl.pallas_call(
        _filter_kernel,
        grid=(2 * n // tr,),
        in_specs=[pl.BlockSpec((tr, LANES), lambda i: (i, 0)), sq, vec, sq, vec, sq, vec,
                  _full((LANES, 2 * CONV_W)), vec, _full((1, CONV_W))],
        out_specs=[pl.BlockSpec((tr, CONV_W), lambda i: (i, 0)), _full((1, CONV_W))],
        out_shape=[jax.ShapeDtypeStruct((2 * n, CONV_W), F32), jax.ShapeDtypeStruct((1, CONV_W), F32)],
        compiler_params=_params(("arbitrary",)),
        name="hyena_filter",
    )(feats, w1p, padc(b1), w2p, padc(b2), w3p, padc(b3), w4p, padc(freq), jnp.asarray(deltas))


def _twiddle(idx, mod):
    ang = 2.0 * np.pi * (idx % mod) / mod
    return np.cos(ang), -np.sin(ang)


def _real_form(mr, mi):
    return np.concatenate([np.concatenate([mr, -mi], -1), np.concatenate([mi, mr], -1)], -2)


@functools.lru_cache(maxsize=None)
def _dft_tables(n2):
    n1 = DFT_N1
    n = n1 * n2
    h = n2 // 2
    a2 = np.arange(n2)
    fr, fi = _twiddle(np.outer(a2, a2), n2)
    m_data = _real_form(fr[:, :h], fi[:, :h])
    m_filt = np.concatenate([fr, fi], 0)
    k2 = a2[:, None, None]
    k1 = np.arange(n1)[None, :, None]
    c1 = np.arange(n1)[None, None, :]
    g = _real_form(*_twiddle(c1 * (n2 * k1 + k2), n))
    a1 = np.arange(n1)
    f1 = _real_form(*_twiddle(np.outer(a1, a1), n1))
    t2 = a1[:, None, None]
    t1 = np.arange(h)[None, :, None]
    j1 = a2[None, None, :]
    hh = _real_form(*_twiddle(j1 * (n1 * t1 + t2), n))
    cast = lambda m: np.asarray(m, dtype=BF16)
    return cast(m_data), cast(m_filt), cast(g), cast(f1), cast(hh)


def _dft_rows_kernel(m_ref, *refs):
    o_ref = refs[-1]
    rhs = jnp.concatenate([r[0] for r in refs[:-1]], axis=0).astype(BF16)
    out = jnp.dot(m_ref[...], rhs, preferred_element_type=F32)
    half = out.shape[0] // 2
    o_ref[0, 0] = out[:half].astype(o_ref.dtype)
    o_ref[0, 1] = out[half:].astype(o_ref.dtype)


def _dft_stage1(mat, views, pairs, rows, n2, width, lc):
    nin = len(views)
    in_specs = [_full(mat.shape)] + [
        pl.BlockSpec((1, rows, lc), (lambda p, j, a=a: (nin * p + a, 0, j))) for a in range(nin)]
    return pl.pallas_call(
        _dft_rows_kernel,
        grid=(pairs, width // lc),
        in_specs=in_specs,
        out_specs=pl.BlockSpec((1, 2, n2, lc), lambda p, j: (p, 0, 0, j)),
        out_shape=jax.ShapeDtypeStruct((pairs, 2, n2, width), BF16),
        compiler_params=_params(("arbitrary", "arbitrary")),
        name="dft_stage1",
    )(mat, *views)


def _spectrum_kernel(a_ref, g_ref, sc_ref, o_ref, *, kb):
    for j in range(kb):
        rhs = jnp.concatenate([a_ref[0, 0, j], a_ref[0, 1, j]], axis=0)
        x = jnp.dot(g_ref[j], rhs, preferred_element_type=F32)
        o_ref[0, j] = x[:DFT_N1] * sc_ref[...]
        o_ref[1, j] = x[DFT_N1:] * sc_ref[...]


def _filter_spectrum(a, g, scale, n2, kb):
    c = a.shape[-1]
    return pl.pallas_call(
        functools.partial(_spectrum_kernel, kb=kb),
        grid=(n2 // kb,),
        in_specs=[pl.BlockSpec((1, 2, kb, DFT_N1, c), lambda k: (0, 0, k, 0, 0)),
                  pl.BlockSpec((kb, 2 * DFT_N1, 2 * DFT_N1), lambda k: (k, 0, 0)), _full((1, c))],
        out_specs=pl.BlockSpec((2, kb, DFT_N1, c), lambda k: (0, k, 0, 0)),
        out_shape=jax.ShapeDtypeStruct((2, n2, DFT_N1, c), F32),
        compiler_params=_params(("arbitrary",)),
        name="filter_spectrum",
    )(a, g, scale)


def _dft_mid_kernel(a_ref, g_ref, kh_ref, f_ref, o_ref, *, kb):
    for j in range(kb):
        rhs = jnp.concatenate([a_ref[0, 0, j], a_ref[0, 1, j]], axis=0)
        x = jnp.dot(g_ref[j], rhs, preferred_element_type=F32)
        xr, xi = x[:DFT_N1], x[DFT_N1:]
        kr, ki = kh_ref[0, j], kh_ref[1, j]
        yr = xr * kr - xi * ki
        yi = xr * ki + xi * kr
        v = jnp.concatenate([yr, -yi], axis=0).astype(BF16)
        b = jnp.dot(f_ref[...], v, preferred_element_type=F32)
        o_ref[0, 0, j] = b[:DFT_N1].astype(o_ref.dtype)
        o_ref[0, 1, j] = b[DFT_N1:].astype(o_ref.dtype)


def _dft_mid(a, g, khat, f1, pairs, n2, kb):
    c = a.shape[-1]
    blk = pl.BlockSpec((1, 2, kb, DFT_N1, c), lambda k, p: (p, 0, k, 0, 0))
    return pl.pallas_call(
        functools.partial(_dft_mid_kernel, kb=kb),
        grid=(n2 // kb, pairs),
        in_specs=[blk, pl.BlockSpec((kb, 2 * DFT_N1, 2 * DFT_N1), lambda k, p: (k, 0, 0)),
                  pl.BlockSpec((2, kb, DFT_N1, c), lambda k, p: (0, k, 0, 0)), _full(f1.shape)],
        out_specs=blk,
        out_shape=jax.ShapeDtypeStruct(a.shape, BF16),
        compiler_params=_params(("arbitrary", "arbitrary")),
        name="dft_mid",
    )(a, g, khat, f1)


def _dft_last_kernel(b_ref, h_ref, u_ref, x0_ref, bd_ref, o_ref, *, tc, c):
    half = h_ref.shape[1] // 2
    bd = bd_ref[...]
    for t in range(tc):
        cols = slice(t * c, (t + 1) * c)
        rhs = jnp.concatenate([b_ref[0, 0, :, cols], b_ref[0, 1, :, cols]], axis=0)
        v = jnp.dot(h_ref[t], rhs, preferred_element_type=F32)
        y0, y1 = v[:half], -v[half:]
        o_ref[0, :, cols] = ((y0 + u_ref[0, :, cols] * bd) * x0_ref[0, :, cols]).astype(o_ref.dtype)
        o_ref[1, :, cols] = ((y1 + u_ref[1, :, cols] * bd) * x0_ref[1, :, cols]).astype(o_ref.dtype)


def _dft_last(bm, hh, u_view, x0_view, bias_d, pairs, n2, c, tc):
    half = n2 // 2
    width = DFT_N1 * c
    io = pl.BlockSpec((2, half, tc * c), lambda p, j: (p, 0, j))
    return pl.pallas_call(
        functools.partial(_dft_last_kernel, tc=tc, c=c),
        grid=(pairs, DFT_N1 // tc),
        in_specs=[pl.BlockSpec((1, 2, n2, tc * c), lambda p, j: (p, 0, 0, j)),
                  pl.BlockSpec((tc, n2, 2 * n2), lambda p, j: (j, 0, 0)), io, io, _full((1, c))],
        out_specs=io,
        out_shape=jax.ShapeDtypeStruct((2 * pairs, half, width), BF16),
        compiler_params=_params(("arbitrary", "arbitrary")),
        name="dft_last",
    )(bm, hh, u_view, x0_view, bias_d)


def _long_conv_mixer(u, x0, kfilt, ksum, bias_d):
    b, n, c = u.shape
    n2 = 2 * n // DFT_N1
    half = n2 // 2
    pairs = b // 2
    width = DFT_N1 * c
    m_data, m_filt, g, f1, hh = (jnp.asarray(t) for t in _dft_tables(n2))
    lc = min(4096, width)
    kb = min(8, n2)
    scale = 1.0 / (ksum * float(DFT_N1 * n2))
    ka = _dft_stage1(m_filt, [kfilt.reshape(1, n2, width)], 1, n2, n2, width, lc)
    khat = _filter_spectrum(ka.reshape(1, 2, n2, DFT_N1, c), g, scale, n2, kb)
    u_view = u.reshape(b, half, width)
    a = _dft_stage1(m_data, [u_view, u_view], pairs, half, n2, width, lc)
    bm = _dft_mid(a.reshape(pairs, 2, n2, DFT_N1, c), g, khat, f1, pairs, n2, kb)
    out = _dft_last(bm.reshape(pairs, 2, n2, width), hh, u_view, x0.reshape(b, half, width),
                    bias_d.reshape(1, c), pairs, n2, c, min(8, DFT_N1))
    return out.reshape(b, n, c)


@functools.lru_cache(maxsize=None)
def _small_dft_tables(n):
    big = 2 * n
    a = np.arange(big)
    fr, fi = _twiddle(np.outer(a, a), big)
    cast = lambda m: np.asarray(m, dtype=BF16)
    return (cast(np.concatenate([fr, fi], 0)),
            cast(_real_form(fr[:, :n], fi[:, :n])),
            cast(_real_form(fr[:n], fi[:n])))


def _small_conv_kernel(k_ref, ks_ref, u_ref, x0_ref, bd_ref, mf_ref, md_ref, mi_ref, o_ref, *, n):
    big = 2 * n
    kh = jnp.dot(mf_ref[...], k_ref[...].astype(BF16), preferred_element_type=F32) * (1.0 / (ks_ref[...] * big))
    kr, ki = kh[:big], kh[big:]
    rhs = jnp.concatenate([u_ref[0], u_ref[1]], axis=0).astype(BF16)
    x = jnp.dot(md_ref[...], rhs, preferred_element_type=F32)
    xr, xi = x[:big], x[big:]
    v = jnp.concatenate([xr * kr - xi * ki, -(xr * ki + xi * kr)], axis=0).astype(BF16)
    y = jnp.dot(mi_ref[...], v, preferred_element_type=F32)
    bd = bd_ref[...]
    o_ref[0] = ((y[:n] + u_ref[0] * bd) * x0_ref[0]).astype(o_ref.dtype)
    o_ref[1] = ((-y[n:] + u_ref[1] * bd) * x0_ref[1]).astype(o_ref.dtype)


def _small_conv_mixer(u, x0, kfilt, ksum, bias_d):
    b, n, c = u.shape
    mf, md, mi = (jnp.asarray(t) for t in _small_dft_tables(n))
    io = pl.BlockSpec((2, n, c), lambda p: (p, 0, 0))
    return pl.pallas_call(
        functools.partial(_small_conv_kernel, n=n),
        grid=(b // 2,),
        in_specs=[_full((2 * n, c)), _full((1, c)), io, io, _full((1, c)),
                  _full(mf.shape), _full(md.shape), _full(mi.shape)],
        out_specs=io,
        out_shape=jax.ShapeDtypeStruct((b, n, c), BF16),
        compiler_params=_params(("arbitrary",)),
        name="small_conv",
    )(kfilt, ksum, u, x0, bias_d.reshape(1, c), mf, md, mi)


def _outproj_kernel(x_ref, oa_ref, oc_ref, wa_ref, wc_ref, g_ref, o_ref):
    y = (jnp.dot(oa_ref[0], wa_ref[...], preferred_element_type=F32)
         + jnp.dot(oc_ref[0], wc_ref[...], preferred_element_type=F32))
    o_ref[0] = x_ref[0] + g_ref[0] * y


def _outproj(x, oa, oc, w_bf, gate):
    bx, t, d = x.shape
    tm = min(512, t)
    row = lambda w: pl.BlockSpec((1, tm, w), lambda b, i: (b, i, 0))
    return pl.pallas_call(
        _outproj_kernel,
        grid=(bx, t // tm),
        in_specs=[row(d), row(Q_W), row(CONV_W), pl.BlockSpec((Q_W, d), lambda b, i: (0, 0)),
                  pl.BlockSpec((CONV_W, d), lambda b, i: (1, 0)), pl.BlockSpec((1, 1, d), lambda b, i: (b, 0, 0))],
        out_specs=row(d),
        out_shape=jax.ShapeDtypeStruct((bx, t, d), F32),
        compiler_params=_params(("arbitrary", "arbitrary")),
        name="out_proj",
    )(x, oa, oc, w_bf, w_bf, gate)


FFN_CHUNK = D_FF // 2


def _gelu_tanh(x):
    return 0.5 * x * (1.0 + jnp.tanh(math.sqrt(2.0 / math.pi) * (x + 0.044715 * (x * x * x))))


def _ffn_kernel(x_ref, xp_ref, xn_ref, sh_ref, sc_ref, gt_ref, g_ref, wu_ref, cw_ref, cb_ref, wd_ref, o_ref):
    i = pl.program_id(1)
    nt = pl.num_programs(1)
    xm = x_ref[0]
    tm = xm.shape[0]
    ext = jnp.concatenate([xp_ref[0], xm, xn_ref[0]], axis=0)
    r = tm + 2 * SUBLANES
    h = _modulated_norm(ext, g_ref[...], sh_ref[0], sc_ref[0])
    row = lax.broadcasted_iota(jnp.int32, (r, 1), 0)
    inside = jnp.logical_and(jnp.logical_or(i > 0, row >= SUBLANES), jnp.logical_or(i < nt - 1, row < tm + SUBLANES))
    h = jnp.where(inside, h, 0.0).astype(BF16)
    hm = h[SUBLANES:tm + SUBLANES]
    acc = jnp.zeros((tm, D_MODEL), F32)
    for c0 in range(0, D_FF, FFN_CHUNK):
        a = jnp.dot(h, wu_ref[:, c0:c0 + FFN_CHUNK], preferred_element_type=F32)
        v = jnp.dot(hm, wu_ref[:, D_FF + c0:D_FF + c0 + FFN_CHUNK], preferred_element_type=F32)
        ap = pltpu.roll(a, 1, 0)[SUBLANES:tm + SUBLANES]
        an = pltpu.roll(a, r - 1, 0)[SUBLANES:tm + SUBLANES]
        cw = cw_ref[:, c0:c0 + FFN_CHUNK]
        conv = ap * cw[0:1] + a[SUBLANES:tm + SUBLANES] * cw[1:2] + an * cw[2:3] + cb_ref[:, c0:c0 + FFN_CHUNK]
        act = (_gelu_tanh(conv) * v).astype(BF16)
        acc = acc + jnp.dot(act, wd_ref[c0:c0 + FFN_CHUNK, :], preferred_element_type=F32)
    o_ref[0] = xm + gt_ref[0] * acc


def _ffn(x, sh, sc, gate, gain, wu_bf, conv_w, conv_b, wd_bf):
    bx, t, d = x.shape
    tm = min(512, t)
    vec = pl.BlockSpec((1, 1, d), lambda b, i: (b, 0, 0))
    once = pl.Buffered(1)
    return pl.pallas_call(
        _ffn_kernel,
        grid=(bx, t // tm),
        in_specs=_halo_specs(tm, t, d) + [
            vec, vec, vec, _full((1, d)),
            pl.BlockSpec((d, 2 * D_FF), lambda b, i: (0, 0), pipeline_mode=once),
            _full((3, D_FF)), _full((1, D_FF)),
            pl.BlockSpec((D_FF, d), lambda b, i: (0, 0), pipeline_mode=once)],
        out_specs=pl.BlockSpec((1, tm, d), lambda b, i: (b, i, 0)),
        out_shape=jax.ShapeDtypeStruct((bx, t, d), F32),
        compiler_params=_params(("arbitrary", "arbitrary")),
        name="conv_ffn",
    )(x, x, x, sh, sc, gate, gain, wu_bf, conv_w, conv_b.reshape(1, D_FF), wd_bf)


@functools.lru_cache(maxsize=None)
def _rope_tables(t):
    pos = np.arange(t)
    n_freq = HEAD_DIM // 4
    inv = ROPE_THETA ** (-np.arange(n_freq, dtype=np.float64) / n_freq)
    ang_r = (pos // GRID_W)[:, None] * inv
    ang_c = (pos % GRID_W)[:, None] * inv
    zero = np.zeros_like(ang_r)
    cos = np.concatenate([np.cos(ang_r)] * 2 + [np.cos(ang_c)] * 2, 1)
    sa = np.concatenate([-np.sin(ang_r), zero, -np.sin(ang_c), zero], 1)
    sb = np.concatenate([zero, np.sin(ang_r), zero, np.sin(ang_c)], 1)
    tile = lambda m: np.tile(m, (1, LANES // HEAD_DIM)).astype(np.float32)
    return tile(cos), tile(sa), tile(sb)


@functools.lru_cache(maxsize=None)
def _identity_rope_tables(t):
    return np.ones((t, LANES), np.float32), np.zeros((t, LANES), np.float32), np.zeros((t, LANES), np.float32)


@functools.lru_cache(maxsize=None)
def _head_block_diag():
    hid = np.arange(Q_W) // HEAD_DIM
    return np.asarray(hid[:, None] == hid[None, :], dtype=BF16)


def kernel(x, c, ctx, c_ctx, ada_w, ada_b, norm_mix, norm_ffn, mix_w_in, mix_w_out, attn_q_norm, attn_k_norm,
           swa_sink, hy_conv_w, hy_conv_b, hy_w1, hy_b1, hy_w2, hy_b2, hy_w3, hy_b3, hy_w4, hy_freq, hy_bias_d,
           sc_conv_w, ffn_w_up, ffn_conv_w, ffn_conv_b, ffn_w_down):
    b, s, d = x.shape
    s_ctx = ctx.shape[1]
    depth = ada_w.shape[0]
    assert d == D_MODEL and b % 2 == 0 and b + 1 <= SUBLANES and s % 1024 == 0 and s_ctx % SUBLANES == 0

    cvec = jnp.concatenate([c, c_ctx[None, :], jnp.zeros((SUBLANES - b - 1, d), F32)], axis=0)
    mods = _mods(cvec, ada_w, ada_b)
    rope = [jnp.asarray(t) for t in _rope_tables(s)]
    rope_ctx = [jnp.asarray(t) for t in _identity_rope_tables(s_ctx)]
    bd = jnp.asarray(_head_block_diag())
    xc = ctx

    for i in range(depth):
        last = i == depth - 1
        j = i // 2
        lat = [mods[i, :b, k * d:(k + 1) * d][:, None, :] for k in range(6)]
        cx = [jnp.broadcast_to(mods[i, b, k * d:(k + 1) * d][None, None, :], (b, 1, d)) for k in range(6)]
        w_in = mix_w_in[i].astype(BF16)
        w_out = mix_w_out[i].astype(BF16)
        w_up = ffn_w_up[i].astype(BF16)
        w_down = ffn_w_down[i].astype(BF16)
        g_mix = norm_mix[i].reshape(1, d)
        g_ffn = norm_ffn[i].reshape(1, d)
        qg = jnp.tile(attn_q_norm[i], N_Q_HEADS).reshape(1, Q_W)
        kg = jnp.tile(attn_k_norm[i], N_KV_HEADS).reshape(1, KV_W)

        q, k, v, zc = _inproj(x, lat[0], lat[1], g_mix, w_in, qg, kg, bd, *rope)
        qc, kc, vc, zcc = _inproj(xc, cx[0], cx[1], g_mix, w_in, qg, kg, bd, *rope_ctx)

        if i % 2 == 0:
            fargs = (hy_w1[j], hy_b1[j], hy_w2[j], hy_b2[j], hy_w3[j], hy_b3[j], hy_w4[j], hy_freq[j])
            o_attn = _dense_attention(q, k, v, kc, vc)
            u, x0 = _hyena_pre(zc, hy_conv_w[j], hy_conv_b[j])
            kf, ks = _implicit_filter(s, *fargs)
            o_conv = _long_conv_mixer(u, x0, kf, ks, hy_bias_d[j])
            if not last:
                oc_attn = _dense_attention(qc, kc, vc)
                uc, x0c = _hyena_pre(zcc, hy_conv_w[j], hy_conv_b[j])
                kfc, ksc = _implicit_filter(s_ctx, *fargs)
                oc_conv = _small_conv_mixer(uc, x0c, kfc, ksc, hy_bias_d[j])
        else:
            sink = swa_sink[j].reshape(N_KV_HEADS, GROUP)
            o_attn = _banded_attention(q, k, v, kc, vc, sink)
            o_conv = _short_conv(zc, sc_conv_w[j])
            if not last:
                oc_attn = _dense_attention(qc, kc, vc, sink=sink)
                oc_conv = _short_conv(zcc, sc_conv_w[j])

        x = _outproj(x, o_attn, o_conv, w_out, lat[2])
        x = _ffn(x, lat[3], lat[4], lat[5], g_ffn, w_up, ffn_conv_w[i], ffn_conv_b[i], w_down)
        if not last:
            xc = _outproj(xc, oc_attn, oc_conv, w_out, cx[2])
            xc = _ffn(xc, cx[3], cx[4], cx[5], g_ffn, w_up, ffn_conv_w[i], ffn_conv_b[i], w_down)
    return x
```

```python
import functools
import math

import numpy as np
import jax
import jax.numpy as jnp
from jax import lax
from jax.experimental import pallas as pl
from jax.experimental.pallas import tpu as pltpu

F32 = jnp.float32
BF16 = jnp.bfloat16
HI = lax.Precision.HIGHEST

D_MODEL = 1024
GRID_W = 64
HEAD_DIM = 64
N_Q_HEADS = 8
N_KV_HEADS = 4
GROUP = N_Q_HEADS // N_KV_HEADS
Q_W = N_Q_HEADS * HEAD_DIM
KV_W = N_KV_HEADS * HEAD_DIM
QKV_W = Q_W + 2 * KV_W
CONV_W = D_MODEL // 2
MIX_IN_W = QKV_W + 3 * CONV_W
WINDOW = 128
ROPE_THETA = 10000.0
FILT_EMB = 33
FILT_WIDTH = 64
DECAY_TARGET = 1e-2
FAST_DECAY_PCT = 0.3
SLOW_DECAY_PCT = 1.5
DECAY_SHIFT = 0.05
D_FF = 2816
NEG_INF = -1e30
RMS_EPS = 1e-6
LOG2E = 1.4426950408889634

LANES = 128
SUBLANES = 8
DFT_N1 = 128
VMEM_LIMIT_MB = 56


def _params(sem, vmem_mb=VMEM_LIMIT_MB):
    return pltpu.CompilerParams(dimension_semantics=sem, vmem_limit_bytes=vmem_mb * 1024 * 1024)


def _full(shape):
    nd = len(shape)
    return pl.BlockSpec(shape, lambda *_: (0,) * nd)


def _mods_kernel(c_ref, w_ref, b_ref, o_ref):
    c = c_ref[...]
    s = c / (1.0 + jnp.exp(-c))
    o_ref[0] = jnp.dot(s, w_ref[0], preferred_element_type=F32, precision=HI) + b_ref[0]


def _mods(cvec, ada_w, ada_b):
    depth, d, n6 = ada_w.shape
    tn = 1536
    return pl.pallas_call(
        _mods_kernel,
        grid=(depth, n6 // tn),
        in_specs=[_full((SUBLANES, d)),
                  pl.BlockSpec((1, d, tn), lambda l, j: (l, 0, j)),
                  pl.BlockSpec((1, 1, tn), lambda l, j: (l, 0, j))],
        out_specs=pl.BlockSpec((1, SUBLANES, tn), lambda l, j: (l, 0, j)),
        out_shape=jax.ShapeDtypeStruct((depth, SUBLANES, n6), F32),
        compiler_params=_params(("arbitrary", "arbitrary")),
        name="ada_mods",
    )(cvec, ada_w, ada_b.reshape(depth, 1, n6))


def _modulated_norm(x, gain, shift, scale):
    ms = jnp.mean(x * x, axis=-1, keepdims=True)
    return (x * lax.rsqrt(ms + RMS_EPS)) * gain * (1.0 + scale) + shift


def _shifted_rows(xm, prev8, next8, i, nt):
    pz = jnp.where(i > 0, prev8, 0.0)
    nz = jnp.where(i < nt - 1, next8, 0.0)
    ext = jnp.concatenate([pz, xm, nz], axis=0)
    r = ext.shape[0]
    xp = pltpu.roll(ext, 1, 0)[SUBLANES:r - SUBLANES]
    xn = pltpu.roll(ext, r - 1, 0)[SUBLANES:r - SUBLANES]
    return xp, xn


def _halo_specs(tm, t, width):
    nb = tm // SUBLANES
    last = t // SUBLANES - 1
    return [pl.BlockSpec((1, tm, width), lambda b, i: (b, i, 0)),
            pl.BlockSpec((1, SUBLANES, width), lambda b, i: (b, jnp.maximum(i * nb - 1, 0), 0)),
            pl.BlockSpec((1, SUBLANES, width), lambda b, i: (b, jnp.minimum((i + 1) * nb, last), 0))]


def _inproj_kernel(x_ref, sh_ref, sc_ref, g_ref, w_ref, qg_ref, kg_ref, bd_ref, cos_ref, sa_ref, sb_ref,
                   q_ref, k_ref, vt_ref, zc_ref):
    h = _modulated_norm(x_ref[0], g_ref[...], sh_ref[0], sc_ref[0])
    z = jnp.dot(h.astype(BF16), w_ref[...], preferred_element_type=F32)
    zc_ref[0] = z[:, QKV_W:]
    cos, sa, sb = cos_ref[...], sa_ref[...], sb_ref[...]

    def head_norm_rope(t, gain, bd):
        ssq = jnp.dot((t * t).astype(BF16), bd, preferred_element_type=F32)
        t = t * lax.rsqrt(ssq * (1.0 / HEAD_DIM) + RMS_EPS) * gain
        outs = []
        for j in range(t.shape[1] // LANES):
            tj = t[:, j * LANES:(j + 1) * LANES]
            outs.append(tj * cos + pltpu.roll(tj, LANES - 16, 1) * sa + pltpu.roll(tj, 16, 1) * sb)
        return outs

    qs = head_norm_rope(z[:, :Q_W], qg_ref[...], bd_ref[...])
    ks = head_norm_rope(z[:, Q_W:Q_W + KV_W], kg_ref[...], bd_ref[:KV_W, :KV_W])
    qscale = HEAD_DIM ** -0.5 * LOG2E
    for j, qj in enumerate(qs):
        qj = (qj * qscale).astype(BF16)
        q_ref[0, 2 * j] = qj[:, :HEAD_DIM]
        q_ref[0, 2 * j + 1] = qj[:, HEAD_DIM:]
    for j, kj in enumerate(ks):
        kj = kj.astype(BF16)
        k_ref[0, 2 * j] = kj[:, :HEAD_DIM]
        k_ref[0, 2 * j + 1] = kj[:, HEAD_DIM:]
    tm = z.shape[0]
    vt = z[:, Q_W + KV_W:QKV_W].T
    ones_row = (lax.broadcasted_iota(jnp.int32, (LANES - HEAD_DIM, tm), 0) == 0).astype(F32)
    for hh in range(N_KV_HEADS):
        vt_ref[0, hh] = jnp.concatenate([vt[hh * HEAD_DIM:(hh + 1) * HEAD_DIM], ones_row], axis=0).astype(BF16)


def _inproj(x, sh, sc, gain, w_bf, qg, kg, bd, cos, sa, sb):
    bx, t, d = x.shape
    tm = min(512, t)
    vec = pl.BlockSpec((1, 1, d), lambda b, i: (b, 0, 0))
    tab = pl.BlockSpec((tm, LANES), lambda b, i: (i, 0))
    return pl.pallas_call(
        _inproj_kernel,
        grid=(bx, t // tm),
        in_specs=[pl.BlockSpec((1, tm, d), lambda b, i: (b, i, 0)), vec, vec, _full((1, d)),
                  _full((d, MIX_IN_W)), _full((1, Q_W)), _full((1, KV_W)), _full((Q_W, Q_W)), tab, tab, tab],
        out_specs=[pl.BlockSpec((1, N_Q_HEADS, tm, HEAD_DIM), lambda b, i: (b, 0, i, 0)),
                   pl.BlockSpec((1, N_KV_HEADS, tm, HEAD_DIM), lambda b, i: (b, 0, i, 0)),
                   pl.BlockSpec((1, N_KV_HEADS, LANES, tm), lambda b, i: (b, 0, 0, i)),
                   pl.BlockSpec((1, tm, 3 * CONV_W), lambda b, i: (b, i, 0))],
        out_shape=[jax.ShapeDtypeStruct((bx, N_Q_HEADS, t, HEAD_DIM), BF16),
                   jax.ShapeDtypeStruct((bx, N_KV_HEADS, t, HEAD_DIM), BF16),
                   jax.ShapeDtypeStruct((bx, N_KV_HEADS, LANES, t), BF16),
                   jax.ShapeDtypeStruct((bx, t, 3 * CONV_W), F32)],
        compiler_params=_params(("arbitrary", "arbitrary")),
        name="in_proj",
    )(x, sh, sc, gain, w_bf, qg, kg, bd, cos, sa, sb)


def _scores_t(ks, q):
    return lax.dot_general(ks, q, (((1,), (1,)), ((), ())), preferred_element_type=F32)


def _softmax_update(s_t, vt, m, acc, mask=None):
    if mask is not None:
        s_t = jnp.where(mask, s_t, NEG_INF)
    m_new = jnp.maximum(m, jnp.max(s_t, axis=0, keepdims=True))
    p_t = jnp.exp2(s_t - m_new).astype(BF16)
    acc = acc * jnp.exp2(m - m_new) + jnp.dot(vt, p_t, preferred_element_type=F32)
    return m_new, acc


def _sink_column(sink_ref, hh, tq, m, acc):
    col = lax.broadcasted_iota(jnp.int32, m.shape, 1)
    sink = jnp.where(col < tq, sink_ref[hh, 0], sink_ref[hh, 1]) * LOG2E
    m_new = jnp.maximum(m, sink)
    den_row = lax.broadcasted_iota(jnp.int32, acc.shape, 0) == HEAD_DIM
    return acc * jnp.exp2(m - m_new) + jnp.where(den_row, jnp.exp2(sink - m_new), 0.0)


def _attn_finish(acc, tq, o_ref):
    o = (acc / acc[HEAD_DIM:HEAD_DIM + 1]).T
    o_ref[0] = jnp.concatenate([o[:tq, :HEAD_DIM], o[tq:, :HEAD_DIM]], axis=1).astype(o_ref.dtype)


def _attn_init(m_cols):
    return jnp.full((1, m_cols), NEG_INF, F32), jnp.zeros((LANES, m_cols), F32)


EXP_ROWS = 64


def _score_stage(ks, q, s_ref):
    s = _scores_t(ks, q)
    s_ref[0:s.shape[0]] = s
    return jnp.max(s, axis=0, keepdims=True)


def _softmax_stage(s_ref, p_ref, n, vt, mx, m, acc_ref):
    m_new = jnp.maximum(m, mx)
    for c in range(0, n, EXP_ROWS):
        p_ref[c:c + EXP_ROWS] = jnp.exp2(s_ref[c:c + EXP_ROWS] - m_new).astype(BF16)
    acc_ref[...] = acc_ref[...] * jnp.exp2(m - m_new) + jnp.dot(vt, p_ref[0:n], preferred_element_type=F32)
    return m_new


def _flash_kernel(*refs, tq, tk, n_main, has_ctx, has_sink):
    sa_ref, sb_ref, pa_ref, pb_ref, acc_ref = refs[-5:]
    o_ref = refs[-6]
    refs = list(refs[:-6])
    sink_ref = refs.pop(0) if has_sink else None
    q_ref, k_ref, vt_ref = refs[:3]
    m_cols = GROUP * tq
    q = q_ref[0].reshape(m_cols, HEAD_DIM)

    def k_tile(j):
        return k_ref[0, 0, pl.ds(pl.multiple_of(j * tk, tk), tk), :]

    def vt_tile(j):
        return vt_ref[0, 0, :, pl.ds(pl.multiple_of(j * tk, tk), tk)]

    acc_ref[...] = jnp.zeros_like(acc_ref)
    m = jnp.full((1, m_cols), NEG_INF, F32)
    mx_a = _score_stage(k_tile(0), q, sa_ref)
    if n_main > 1:
        def body(i, carry):
            m, mx_a = carry
            mx_b = _score_stage(k_tile(2 * i + 1), q, sb_ref)
            m = _softmax_stage(sa_ref, pa_ref, tk, vt_tile(2 * i), mx_a, m, acc_ref)
            mx_a = _score_stage(k_tile(2 * i + 2), q, sa_ref)
            m = _softmax_stage(sb_ref, pb_ref, tk, vt_tile(2 * i + 1), mx_b, m, acc_ref)
            return m, mx_a

        m, mx_a = lax.fori_loop(0, n_main // 2 - 1, body, (m, mx_a))
        mx_b = _score_stage(k_tile(n_main - 1), q, sb_ref)
        m = _softmax_stage(sa_ref, pa_ref, tk, vt_tile(n_main - 2), mx_a, m, acc_ref)
        last = (sb_ref, pb_ref, mx_b)
        spare = (sa_ref, pa_ref)
    else:
        last = (sa_ref, pa_ref, mx_a)
        spare = (sb_ref, pb_ref)
    if has_ctx:
        n_ctx = refs[3].shape[2]
        mx_c = _score_stage(refs[3][0, 0], q, spare[0])
    m = _softmax_stage(last[0], last[1], tk, vt_tile(n_main - 1), last[2], m, acc_ref)
    if has_ctx:
        m = _softmax_stage(spare[0], spare[1], n_ctx, refs[4][0, 0], mx_c, m, acc_ref)
    acc = acc_ref[...]
    if has_sink:
        acc = _sink_column(sink_ref, pl.program_id(1), tq, m, acc)
    _attn_finish(acc, tq, o_ref)


def _kv_specs(n):
    return [pl.BlockSpec((1, 1, n, HEAD_DIM), lambda b, h, i: (b, h, 0, 0)),
            pl.BlockSpec((1, 1, LANES, n), lambda b, h, i: (b, h, 0, 0))]


def _dense_attention(q, k, vt, kc=None, vct=None, sink=None):
    bx, _, t, _ = q.shape
    s = k.shape[2]
    tq = min(256, t)
    tk = min(512, s)
    has_ctx = kc is not None
    has_sink = sink is not None
    in_specs = [pl.BlockSpec((1, GROUP, tq, HEAD_DIM), lambda b, h, i: (b, h, i, 0))] + _kv_specs(s)
    args = [q, k, vt]
    if has_ctx:
        in_specs += _kv_specs(kc.shape[2])
        args += [kc, vct]
    if has_sink:
        in_specs = [pl.BlockSpec(memory_space=pltpu.SMEM)] + in_specs
        args = [sink] + args
    n_main = s // tk
    assert n_main == 1 or n_main % 2 == 0
    rows = max(tk, kc.shape[2]) if has_ctx else tk
    m_cols = GROUP * tq
    return pl.pallas_call(
        functools.partial(_flash_kernel, tq=tq, tk=tk, n_main=n_main, has_ctx=has_ctx, has_sink=has_sink),
        grid=(bx, N_KV_HEADS, t // tq),
        in_specs=in_specs,
        out_specs=pl.BlockSpec((1, tq, LANES), lambda b, h, i: (b, i, h)),
        out_shape=jax.ShapeDtypeStruct((bx, t, Q_W), BF16),
        scratch_shapes=[pltpu.VMEM((rows, m_cols), F32), pltpu.VMEM((rows, m_cols), F32),
                        pltpu.VMEM((rows, m_cols), BF16), pltpu.VMEM((rows, m_cols), BF16),
                        pltpu.VMEM((LANES, m_cols), F32)],
        compiler_params=_params(("arbitrary", "arbitrary", "arbitrary")),
        name="dense_attention",
    )(*args)


def _banded_kernel(sink_ref, q_ref, k_ref, vt_ref, kc_ref, vct_ref, o_ref, *, tq, s_len):
    i = pl.program_id(2)
    m_cols = GROUP * tq
    span = tq + 2 * WINDOW
    q = q_ref[0].reshape(m_cols, HEAD_DIM)
    q0 = i * tq
    start = pl.multiple_of(jnp.clip(q0 - WINDOW, 0, s_len - span), WINDOW)
    s_loc = _scores_t(k_ref[0, 0, pl.ds(start, span), :], q)
    s_ctx = _scores_t(kc_ref[0, 0], q)
    col = lax.broadcasted_iota(jnp.int32, (span, m_cols), 1)
    qi = q0 + jnp.where(col >= tq, col - tq, col)
    kj = start + lax.broadcasted_iota(jnp.int32, (span, m_cols), 0)
    ok = jnp.abs(kj - qi) <= WINDOW
    m, acc = _softmax_update(s_loc, vt_ref[0, 0, :, pl.ds(start, span)], *_attn_init(m_cols), mask=ok)
    m, acc = _softmax_update(s_ctx, vct_ref[0, 0], m, acc)
    acc = _sink_column(sink_ref, pl.program_id(1), tq, m, acc)
    _attn_finish(acc, tq, o_ref)


def _banded_attention(q, k, vt, kc, vct, sink):
    bx, _, t, _ = q.shape
    tq = 256
    return pl.pallas_call(
        functools.partial(_banded_kernel, tq=tq, s_len=t),
        grid=(bx, N_KV_HEADS, t // tq),
        in_specs=[pl.BlockSpec(memory_space=pltpu.SMEM),
                  pl.BlockSpec((1, GROUP, tq, HEAD_DIM), lambda b, h, i: (b, h, i, 0))]
                 + _kv_specs(t) + _kv_specs(kc.shape[2]),
        out_specs=pl.BlockSpec((1, tq, LANES), lambda b, h, i: (b, i, h)),
        out_shape=jax.ShapeDtypeStruct((bx, t, Q_W), BF16),
        compiler_params=_params(("arbitrary", "arbitrary", "arbitrary")),
        name="banded_attention",
    )(sink, q, k, vt, kc, vct)


def _hyena_pre_kernel(z_ref, zp_ref, zn_ref, w_ref, b_ref, u_ref, x0_ref):
    i = pl.program_id(1)
    zm = z_ref[0]
    zp, zn = _shifted_rows(zm, zp_ref[0], zn_ref[0], i, pl.num_programs(1))
    w = w_ref[...]
    c = zp * w[0:1] + zm * w[1:2] + zn * w[2:3] + b_ref[...]
    x0_ref[0] = c[:, :CONV_W]
    u_ref[0] = c[:, 2 * CONV_W:] * c[:, CONV_W:2 * CONV_W]


def _hyena_pre(zc, conv_w, conv_b):
    bx, t, w3 = zc.shape
    tm = min(512, t)
    out = jax.ShapeDtypeStruct((bx, t, CONV_W), F32)
    ospec = pl.BlockSpec((1, tm, CONV_W), lambda b, i: (b, i, 0))
    return pl.pallas_call(
        _hyena_pre_kernel,
        grid=(bx, t // tm),
        in_specs=_halo_specs(tm, t, w3) + [_full((3, w3)), _full((1, w3))],
        out_specs=[ospec, ospec],
        out_shape=[out, out],
        compiler_params=_params(("arbitrary", "arbitrary")),
        name="hyena_pre",
    )(zc, zc, zc, conv_w, conv_b.reshape(1, w3))


def _short_conv_kernel(z_ref, zp_ref, zn_ref, w_ref, o_ref):
    i = pl.program_id(1)
    prod = lambda z: z[:, CONV_W:2 * CONV_W] * z[:, 2 * CONV_W:]
    zm = z_ref[0]
    pm = prod(zm)
    pp, pn = _shifted_rows(pm, prod(zp_ref[0]), prod(zn_ref[0]), i, pl.num_programs(1))
    w = w_ref[...]
    o_ref[0] = (zm[:, :CONV_W] * (pp * w[0:1] + pm * w[1:2] + pn * w[2:3])).astype(o_ref.dtype)


def _short_conv(zc, conv_w):
    bx, t, w3 = zc.shape
    tm = min(512, t)
    return pl.pallas_call(
        _short_conv_kernel,
        grid=(bx, t // tm),
        in_specs=_halo_specs(tm, t, w3) + [_full((3, CONV_W))],
        out_specs=pl.BlockSpec((1, tm, CONV_W), lambda b, i: (b, i, 0)),
        out_shape=jax.ShapeDtypeStruct((bx, t, CONV_W), BF16),
        compiler_params=_params(("arbitrary", "arbitrary")),
        name="short_conv",
    )(zc, zc, zc, conv_w)


@functools.lru_cache(maxsize=None)
def _filter_features(n):
    j = np.arange(2 * n)
    d = np.where(j <= n, j, 2 * n - j)
    d = np.where(j == n, 0, d)
    bands = (FILT_EMB - 1) // 2
    t01 = np.linspace(0.0, 1.0, n)[d]
    w = 2.0 * np.pi * d.astype(np.float64) / n
    f = np.linspace(1e-4, bands - 1, bands)[None, :]
    feats = np.zeros((2 * n, LANES), np.float64)
    feats[:, 0] = t01
    feats[:, 1:1 + bands] = np.cos(f * w[:, None])
    feats[:, 1 + bands:FILT_EMB] = -np.sin(f * w[:, None])
    feats[:, 64] = t01
    feats[:, 65] = (j < n)
    feats[:, 66] = (j != n)
    return feats.astype(np.float32)


def _filter_kernel(f_ref, w1_ref, b1_ref, w2_ref, b2_ref, w3_ref, b3_ref, w4_ref, fr_ref, dl_ref, k_ref, s_ref):
    f = f_ref[...]
    fr = fr_ref[...]
    mm = lambda a, b: jnp.dot(a, b, preferred_element_type=F32, precision=HI)
    h = jnp.sin(fr * (mm(f, w1_ref[...]) + b1_ref[...]))
    h = jnp.sin(fr * (mm(h, w2_ref[...]) + b2_ref[...]))
    h = jnp.sin(fr * (mm(h, w3_ref[...]) + b3_ref[...]))
    hf = mm(h, w4_ref[...])
    win = jnp.exp(-f[:, 64:65] * dl_ref[...]) + DECAY_SHIFT
    k = jnp.where(f[:, 65:66] > 0.5, hf[:, :CONV_W], hf[:, CONV_W:]) * win * f[:, 66:67]
    k_ref[...] = k

    @pl.when(pl.program_id(0) == 0)
    def _():
        s_ref[...] = jnp.zeros_like(s_ref)

    s_ref[...] += jnp.sum(jnp.abs(k), axis=0, keepdims=True)


def _implicit_filter(n, w1, b1, w2, b2, w3, b3, w4, freq):
    feats = jnp.asarray(_filter_features(n))
    pad = LANES - FILT_WIDTH
    padc = lambda a: jnp.pad(a.reshape(1, -1), ((0, 0), (0, pad)))
    w1p = jnp.pad(w1, ((0, LANES - FILT_EMB), (0, pad)))
    w2p = jnp.pad(w2, ((0, pad), (0, pad)))
    w3p = jnp.pad(w3, ((0, pad), (0, pad)))
    w4p = jnp.pad(w4, ((0, pad), (0, 0)))
    deltas = np.abs(np.linspace(math.log(DECAY_TARGET) / SLOW_DECAY_PCT, math.log(DECAY_TARGET) / FAST_DECAY_PCT,
                                CONV_W)).astype(np.float32).reshape(1, CONV_W)
    tr = min(1024, 2 * n)
    sq = _full((LANES, LANES))
    vec = _full((1, LANES))
    return pl.pallas_call(
        _filter_kernel,
        grid=(2 * n // tr,),
        in_specs=[pl.BlockSpec((tr, LANES), lambda i: (i, 0)), sq, vec, sq, vec, sq, vec,
                  _full((LANES, 2 * CONV_W)), vec, _full((1, CONV_W))],
        out_specs=[pl.BlockSpec((tr, CONV_W), lambda i: (i, 0)), _full((1, CONV_W))],
        out_shape=[jax.ShapeDtypeStruct((2 * n, CONV_W), F32), jax.ShapeDtypeStruct((1, CONV_W), F32)],
        compiler_params=_params(("arbitrary",)),
        name="hyena_filter",
    )(feats, w1p, padc(b1), w2p, padc(b2), w3p, padc(b3), w4p, padc(freq), jnp.asarray(deltas))


def _twiddle(idx, mod):
    ang = 2.0 * np.pi * (idx % mod) / mod
    return np.cos(ang), -np.sin(ang)


def _real_form(mr, mi):
    return np.concatenate([np.concatenate([mr, -mi], -1), np.concatenate([mi, mr], -1)], -2)


@functools.lru_cache(maxsize=None)
def _dft_tables(n2):
    n1 = DFT_N1
    n = n1 * n2
    h = n2 // 2
    a2 = np.arange(n2)
    fr, fi = _twiddle(np.outer(a2, a2), n2)
    m_data = _real_form(fr[:, :h], fi[:, :h])
    m_filt = np.concatenate([fr, fi], 0)
    k2 = a2[:, None, None]
    k1 = np.arange(n1)[None, :, None]
    c1 = np.arange(n1)[None, None, :]
    g = _real_form(*_twiddle(c1 * (n2 * k1 + k2), n))
    a1 = np.arange(n1)
    f1 = _real_form(*_twiddle(np.outer(a1, a1), n1))
    t2 = a1[:, None, None]
    t1 = np.arange(h)[None, :, None]
    j1 = a2[None, None, :]
    hh = _real_form(*_twiddle(j1 * (n1 * t1 + t2), n))
    cast = lambda m: np.asarray(m, dtype=BF16)
    return cast(m_data), cast(m_filt), cast(g), cast(f1), cast(hh)


def _dft_rows_kernel(m_ref, *refs):
    o_ref = refs[-1]
    rhs = jnp.concatenate([r[0] for r in refs[:-1]], axis=0).astype(BF16)
    out = jnp.dot(m_ref[...], rhs, preferred_element_type=F32)
    half = out.shape[0] // 2
    o_ref[0, 0] = out[:half].astype(o_ref.dtype)
    o_ref[0, 1] = out[half:].astype(o_ref.dtype)


def _dft_stage1(mat, views, pairs, rows, n2, width, lc):
    nin = len(views)
    in_specs = [_full(mat.shape)] + [
        pl.BlockSpec((1, rows, lc), (lambda p, j, a=a: (nin * p + a, 0, j))) for a in range(nin)]
    return pl.pallas_call(
        _dft_rows_kernel,
        grid=(pairs, width // lc),
        in_specs=in_specs,
        out_specs=pl.BlockSpec((1, 2, n2, lc), lambda p, j: (p, 0, 0, j)),
        out_shape=jax.ShapeDtypeStruct((pairs, 2, n2, width), BF16),
        compiler_params=_params(("arbitrary", "arbitrary")),
        name="dft_stage1",
    )(mat, *views)


def _spectrum_kernel(a_ref, g_ref, sc_ref, o_ref, *, kb):
    for j in range(kb):
        rhs = jnp.concatenate([a_ref[0, 0, j], a_ref[0, 1, j]], axis=0)
        x = jnp.dot(g_ref[j], rhs, preferred_element_type=F32)
        o_ref[0, j] = x[:DFT_N1] * sc_ref[...]
        o_ref[1, j] = x[DFT_N1:] * sc_ref[...]


def _filter_spectrum(a, g, scale, n2, kb):
    c = a.shape[-1]
    return pl.pallas_call(
        functools.partial(_spectrum_kernel, kb=kb),
        grid=(n2 // kb,),
        in_specs=[pl.BlockSpec((1, 2, kb, DFT_N1, c), lambda k: (0, 0, k, 0, 0)),
                  pl.BlockSpec((kb, 2 * DFT_N1, 2 * DFT_N1), lambda k: (k, 0, 0)), _full((1, c))],
        out_specs=pl.BlockSpec((2, kb, DFT_N1, c), lambda k: (0, k, 0, 0)),
        out_shape=jax.ShapeDtypeStruct((2, n2, DFT_N1, c), F32),
        compiler_params=_params(("arbitrary",)),
        name="filter_spectrum",
    )(a, g, scale)


def _dft_mid_kernel(a_ref, g_ref, kh_ref, f_ref, o_ref, *, kb):
    for j in range(kb):
        rhs = jnp.concatenate([a_ref[0, 0, j], a_ref[0, 1, j]], axis=0)
        x = jnp.dot(g_ref[j], rhs, preferred_element_type=F32)
        xr, xi = x[:DFT_N1], x[DFT_N1:]
        kr, ki = kh_ref[0, j], kh_ref[1, j]
        yr = xr * kr - xi * ki
        yi = xr * ki + xi * kr
        v = jnp.concatenate([yr, -yi], axis=0).astype(BF16)
        b = jnp.dot(f_ref[...], v, preferred_element_type=F32)
        o_ref[0, 0, j] = b[:DFT_N1].astype(o_ref.dtype)
        o_ref[0, 1, j] = b[DFT_N1:].astype(o_ref.dtype)


def _dft_mid(a, g, khat, f1, pairs, n2, kb):
    c = a.shape[-1]
    blk = pl.BlockSpec((1, 2, kb, DFT_N1, c), lambda k, p: (p, 0, k, 0, 0))
    return pl.pallas_call(
        functools.partial(_dft_mid_kernel, kb=kb),
        grid=(n2 // kb, pairs),
        in_specs=[blk, pl.BlockSpec((kb, 2 * DFT_N1, 2 * DFT_N1), lambda k, p: (k, 0, 0)),
                  pl.BlockSpec((2, kb, DFT_N1, c), lambda k, p: (0, k, 0, 0)), _full(f1.shape)],
        out_specs=blk,
        out_shape=jax.ShapeDtypeStruct(a.shape, BF16),
        compiler_params=_params(("arbitrary", "arbitrary")),
        name="dft_mid",
    )(a, g, khat, f1)


def _dft_last_kernel(b_ref, h_ref, u_ref, x0_ref, bd_ref, o_ref, *, tc, c):
    half = h_ref.shape[1] // 2
    bd = bd_ref[...]
    for t in range(tc):
        cols = slice(t * c, (t + 1) * c)
        rhs = jnp.concatenate([b_ref[0, 0, :, cols], b_ref[0, 1, :, cols]], axis=0)
        v = jnp.dot(h_ref[t], rhs, preferred_element_type=F32)
        y0, y1 = v[:half], -v[half:]
        o_ref[0, :, cols] = ((y0 + u_ref[0, :, cols] * bd) * x0_ref[0, :, cols]).astype(o_ref.dtype)
        o_ref[1, :, cols] = ((y1 + u_ref[1, :, cols] * bd) * x0_ref[1, :, cols]).astype(o_ref.dtype)


def _dft_last(bm, hh, u_view, x0_view, bias_d, pairs, n2, c, tc):
    half = n2 // 2
    width = DFT_N1 * c
    io = pl.BlockSpec((2, half, tc * c), lambda p, j: (p, 0, j))
    return pl.pallas_call(
        functools.partial(_dft_last_kernel, tc=tc, c=c),
        grid=(pairs, DFT_N1 // tc),
        in_specs=[pl.BlockSpec((1, 2, n2, tc * c), lambda p, j: (p, 0, 0, j)),
                  pl.BlockSpec((tc, n2, 2 * n2), lambda p, j: (j, 0, 0)), io, io, _full((1, c))],
        out_specs=io,
        out_shape=jax.ShapeDtypeStruct((2 * pairs, half, width), BF16),
        compiler_params=_params(("arbitrary", "arbitrary")),
        name="dft_last",
    )(bm, hh, u_view, x0_view, bias_d)


def _long_conv_mixer(u, x0, kfilt, ksum, bias_d):
    b, n, c = u.shape
    n2 = 2 * n // DFT_N1
    half = n2 // 2
    pairs = b // 2
    width = DFT_N1 * c
    m_data, m_filt, g, f1, hh = (jnp.asarray(t) for t in _dft_tables(n2))
    lc = min(4096, width)
    kb = min(8, n2)
    scale = 1.0 / (ksum * float(DFT_N1 * n2))
    ka = _dft_stage1(m_filt, [kfilt.reshape(1, n2, width)], 1, n2, n2, width, lc)
    khat = _filter_spectrum(ka.reshape(1, 2, n2, DFT_N1, c), g, scale, n2, kb)
    u_view = u.reshape(b, half, width)
    a = _dft_stage1(m_data, [u_view, u_view], pairs, half, n2, width, lc)
    bm = _dft_mid(a.reshape(pairs, 2, n2, DFT_N1, c), g, khat, f1, pairs, n2, kb)
    out = _dft_last(bm.reshape(pairs, 2, n2, width), hh, u_view, x0.reshape(b, half, width),
                    bias_d.reshape(1, c), pairs, n2, c, min(8, DFT_N1))
    return out.reshape(b, n, c)


@functools.lru_cache(maxsize=None)
def _small_dft_tables(n):
    big = 2 * n
    a = np.arange(big)
    fr, fi = _twiddle(np.outer(a, a), big)
    cast = lambda m: np.asarray(m, dtype=BF16)
    return (cast(np.concatenate([fr, fi], 0)),
            cast(_real_form(fr[:, :n], fi[:, :n])),
            cast(_real_form(fr[:n], fi[:n])))


def _small_conv_kernel(k_ref, ks_ref, u_ref, x0_ref, bd_ref, mf_ref, md_ref, mi_ref, o_ref, *, n):
    big = 2 * n
    kh = jnp.dot(mf_ref[...], k_ref[...].astype(BF16), preferred_element_type=F32) * (1.0 / (ks_ref[...] * big))
    kr, ki = kh[:big], kh[big:]
    rhs = jnp.concatenate([u_ref[0], u_ref[1]], axis=0).astype(BF16)
    x = jnp.dot(md_ref[...], rhs, preferred_element_type=F32)
    xr, xi = x[:big], x[big:]
    v = jnp.concatenate([xr * kr - xi * ki, -(xr * ki + xi * kr)], axis=0).astype(BF16)
    y = jnp.dot(mi_ref[...], v, preferred_element_type=F32)
    bd = bd_ref[...]
    o_ref[0] = ((y[:n] + u_ref[0] * bd) * x0_ref[0]).astype(o_ref.dtype)
    o_ref[1] = ((-y[n:] + u_ref[1] * bd) * x0_ref[1]).astype(o_ref.dtype)


def _small_conv_mixer(u, x0, kfilt, ksum, bias_d):
    b, n, c = u.shape
    mf, md, mi = (jnp.asarray(t) for t in _small_dft_tables(n))
    io = pl.BlockSpec((2, n, c), lambda p: (p, 0, 0))
    return pl.pallas_call(
        functools.partial(_small_conv_kernel, n=n),
        grid=(b // 2,),
        in_specs=[_full((2 * n, c)), _full((1, c)), io, io, _full((1, c)),
                  _full(mf.shape), _full(md.shape), _full(mi.shape)],
        out_specs=io,
        out_shape=jax.ShapeDtypeStruct((b, n, c), BF16),
        compiler_params=_params(("arbitrary",)),
        name="small_conv",
    )(kfilt, ksum, u, x0, bias_d.reshape(1, c), mf, md, mi)


def _outproj_kernel(x_ref, oa_ref, oc_ref, wa_ref, wc_ref, g_ref, o_ref):
    y = (jnp.dot(oa_ref[0], wa_ref[...], preferred_element_type=F32)
         + jnp.dot(oc_ref[0], wc_ref[...], preferred_element_type=F32))
    o_ref[0] = x_ref[0] + g_ref[0] * y


def _outproj(x, oa, oc, w_bf, gate):
    bx, t, d = x.shape
    tm = min(512, t)
    row = lambda w: pl.BlockSpec((1, tm, w), lambda b, i: (b, i, 0))
    return pl.pallas_call(
        _outproj_kernel,
        grid=(bx, t // tm),
        in_specs=[row(d), row(Q_W), row(CONV_W), pl.BlockSpec((Q_W, d), lambda b, i: (0, 0)),
                  pl.BlockSpec((CONV_W, d), lambda b, i: (1, 0)), pl.BlockSpec((1, 1, d), lambda b, i: (b, 0, 0))],
        out_specs=row(d),
        out_shape=jax.ShapeDtypeStruct((bx, t, d), F32),
        compiler_params=_params(("arbitrary", "arbitrary")),
        name="out_proj",
    )(x, oa, oc, w_bf, w_bf, gate)


FFN_CHUNK = D_FF // 2


def _gelu_tanh(x):
    return 0.5 * x * (1.0 + jnp.tanh(math.sqrt(2.0 / math.pi) * (x + 0.044715 * (x * x * x))))


def _ffn_kernel(x_ref, xp_ref, xn_ref, sh_ref, sc_ref, gt_ref, g_ref, wu_ref, cw_ref, cb_ref, wd_ref, o_ref):
    i = pl.program_id(1)
    nt = pl.num_programs(1)
    xm = x_ref[0]
    tm = xm.shape[0]
    ext = jnp.concatenate([xp_ref[0], xm, xn_ref[0]], axis=0)
    r = tm + 2 * SUBLANES
    h = _modulated_norm(ext, g_ref[...], sh_ref[0], sc_ref[0])
    row = lax.broadcasted_iota(jnp.int32, (r, 1), 0)
    inside = jnp.logical_and(jnp.logical_or(i > 0, row >= SUBLANES), jnp.logical_or(i < nt - 1, row < tm + SUBLANES))
    h = jnp.where(inside, h, 0.0).astype(BF16)
    hm = h[SUBLANES:tm + SUBLANES]
    acc = jnp.zeros((tm, D_MODEL), F32)
    for c0 in range(0, D_FF, FFN_CHUNK):
        a = jnp.dot(h, wu_ref[:, c0:c0 + FFN_CHUNK], preferred_element_type=F32)
        v = jnp.dot(hm, wu_ref[:, D_FF + c0:D_FF + c0 + FFN_CHUNK], preferred_element_type=F32)
        ap = pltpu.roll(a, 1, 0)[SUBLANES:tm + SUBLANES]
        an = pltpu.roll(a, r - 1, 0)[SUBLANES:tm + SUBLANES]
        cw = cw_ref[:, c0:c0 + FFN_CHUNK]
        conv = ap * cw[0:1] + a[SUBLANES:tm + SUBLANES] * cw[1:2] + an * cw[2:3] + cb_ref[:, c0:c0 + FFN_CHUNK]
        act = (_gelu_tanh(conv) * v).astype(BF16)
        acc = acc + jnp.dot(act, wd_ref[c0:c0 + FFN_CHUNK, :], preferred_element_type=F32)
    o_ref[0] = xm + gt_ref[0] * acc


def _ffn(x, sh, sc, gate, gain, wu_bf, conv_w, conv_b, wd_bf):
    bx, t, d = x.shape
    tm = min(512, t)
    vec = pl.BlockSpec((1, 1, d), lambda b, i: (b, 0, 0))
    once = pl.Buffered(1)
    return pl.pallas_call(
        _ffn_kernel,
        grid=(bx, t // tm),
        in_specs=_halo_specs(tm, t, d) + [
            vec, vec, vec, _full((1, d)),
            pl.BlockSpec((d, 2 * D_FF), lambda b, i: (0, 0), pipeline_mode=once),
            _full((3, D_FF)), _full((1, D_FF)),
            pl.BlockSpec((D_FF, d), lambda b, i: (0, 0), pipeline_mode=once)],
        out_specs=pl.BlockSpec((1, tm, d), lambda b, i: (b, i, 0)),
        out_shape=jax.ShapeDtypeStruct((bx, t, d), F32),
        compiler_params=_params(("arbitrary", "arbitrary")),
        name="conv_ffn",
    )(x, x, x, sh, sc, gate, gain, wu_bf, conv_w, conv_b.reshape(1, D_FF), wd_bf)


@functools.lru_cache(maxsize=None)
def _rope_tables(t):
    pos = np.arange(t)
    n_freq = HEAD_DIM // 4
    inv = ROPE_THETA ** (-np.arange(n_freq, dtype=np.float64) / n_freq)
    ang_r = (pos // GRID_W)[:, None] * inv
    ang_c = (pos % GRID_W)[:, None] * inv
    zero = np.zeros_like(ang_r)
    cos = np.concatenate([np.cos(ang_r)] * 2 + [np.cos(ang_c)] * 2, 1)
    sa = np.concatenate([-np.sin(ang_r), zero, -np.sin(ang_c), zero], 1)
    sb = np.concatenate([zero, np.sin(ang_r), zero, np.sin(ang_c)], 1)
    tile = lambda m: np.tile(m, (1, LANES // HEAD_DIM)).astype(np.float32)
    return tile(cos), tile(sa), tile(sb)


@functools.lru_cache(maxsize=None)
def _identity_rope_tables(t):
    return np.ones((t, LANES), np.float32), np.zeros((t, LANES), np.float32), np.zeros((t, LANES), np.float32)


@functools.lru_cache(maxsize=None)
def _head_block_diag():
    hid = np.arange(Q_W) // HEAD_DIM
    return np.asarray(hid[:, None] == hid[None, :], dtype=BF16)


def kernel(x, c, ctx, c_ctx, ada_w, ada_b, norm_mix, norm_ffn, mix_w_in, mix_w_out, attn_q_norm, attn_k_norm,
           swa_sink, hy_conv_w, hy_conv_b, hy_w1, hy_b1, hy_w2, hy_b2, hy_w3, hy_b3, hy_w4, hy_freq, hy_bias_d,
           sc_conv_w, ffn_w_up, ffn_conv_w, ffn_conv_b, ffn_w_down):
    b, s, d = x.shape
    s_ctx = ctx.shape[1]
    depth = ada_w.shape[0]
    assert d == D_MODEL and b % 2 == 0 and b + 1 <= SUBLANES and s % 1024 == 0 and s_ctx % SUBLANES == 0

    cvec = jnp.concatenate([c, c_ctx[None, :], jnp.zeros((SUBLANES - b - 1, d), F32)], axis=0)
    mods = _mods(cvec, ada_w, ada_b)
    rope = [jnp.asarray(t) for t in _rope_tables(s)]
    rope_ctx = [jnp.asarray(t) for t in _identity_rope_tables(s_ctx)]
    bd = jnp.asarray(_head_block_diag())
    xc = ctx

    for i in range(depth):
        last = i == depth - 1
        j = i // 2
        lat = [mods[i, :b, k * d:(k + 1) * d][:, None, :] for k in range(6)]
        cx = [jnp.broadcast_to(mods[i, b, k * d:(k + 1) * d][None, None, :], (b, 1, d)) for k in range(6)]
        w_in = mix_w_in[i].astype(BF16)
        w_out = mix_w_out[i].astype(BF16)
        w_up = ffn_w_up[i].astype(BF16)
        w_down = ffn_w_down[i].astype(BF16)
        g_mix = norm_mix[i].reshape(1, d)
        g_ffn = norm_ffn[i].reshape(1, d)
        qg = jnp.tile(attn_q_norm[i], N_Q_HEADS).reshape(1, Q_W)
        kg = jnp.tile(attn_k_norm[i], N_KV_HEADS).reshape(1, KV_W)

        q, k, v, zc = _inproj(x, lat[0], lat[1], g_mix, w_in, qg, kg, bd, *rope)
        qc, kc, vc, zcc = _inproj(xc, cx[0], cx[1], g_mix, w_in, qg, kg, bd, *rope_ctx)

        if i % 2 == 0:
            fargs = (hy_w1[j], hy_b1[j], hy_w2[j], hy_b2[j], hy_w3[j], hy_b3[j], hy_w4[j], hy_freq[j])
            o_attn = _dense_attention(q, k, v, kc, vc)
            u, x0 = _hyena_pre(zc, hy_conv_w[j], hy_conv_b[j])
            kf, ks = _implicit_filter(s, *fargs)
            o_conv = _long_conv_mixer(u, x0, kf, ks, hy_bias_d[j])
            if not last:
                oc_attn = _dense_attention(qc, kc, vc)
                uc, x0c = _hyena_pre(zcc, hy_conv_w[j], hy_conv_b[j])
                kfc, ksc = _implicit_filter(s_ctx, *fargs)
                oc_conv = _small_conv_mixer(uc, x0c, kfc, ksc, hy_bias_d[j])
        else:
            sink = swa_sink[j].reshape(N_KV_HEADS, GROUP)
            o_attn = _banded_attention(q, k, v, kc, vc, sink)
            o_conv = _short_conv(zc, sc_conv_w[j])
            if not last:
                oc_attn = _dense_attention(qc, kc, vc, sink=sink)
                oc_conv = _short_conv(zcc, sc_conv_w[j])

        x = _outproj(x, o_attn, o_conv, w_out, lat[2])
        x = _ffn(x, lat[3], lat[4], lat[5], g_ffn, w_up, ffn_conv_w[i], ffn_conv_b[i], w_down)
        if not last:
            xc = _outproj(xc, oc_attn, oc_conv, w_out, cx[2])
            xc = _ffn(xc, cx[3], cx[4], cx[5], g_ffn, w_up, ffn_conv_w[i], ffn_conv_b[i], w_down)
    return x
```

```python
import functools
import math

import numpy as np
import jax
import jax.numpy as jnp
from jax import lax
from jax.experimental import pallas as pl
from jax.experimental.pallas import tpu as pltpu

F32 = jnp.float32
BF16 = jnp.bfloat16
HI = lax.Precision.HIGHEST

D_MODEL = 1024
GRID_W = 64
HEAD_DIM = 64
N_Q_HEADS = 8
N_KV_HEADS = 4
GROUP = N_Q_HEADS // N_KV_HEADS
Q_W = N_Q_HEADS * HEAD_DIM
KV_W = N_KV_HEADS * HEAD_DIM
QKV_W = Q_W + 2 * KV_W
CONV_W = D_MODEL // 2
MIX_IN_W = QKV_W + 3 * CONV_W
WINDOW = 128
ROPE_THETA = 10000.0
FILT_EMB = 33
FILT_WIDTH = 64
DECAY_TARGET = 1e-2
FAST_DECAY_PCT = 0.3
SLOW_DECAY_PCT = 1.5
DECAY_SHIFT = 0.05
D_FF = 2816
NEG_INF = -1e30
RMS_EPS = 1e-6
LOG2E = 1.4426950408889634

LANES = 128
SUBLANES = 8
DFT_N1 = 128
VMEM_LIMIT_MB = 56


def _params(sem, vmem_mb=VMEM_LIMIT_MB):
    return pltpu.CompilerParams(dimension_semantics=sem, vmem_limit_bytes=vmem_mb * 1024 * 1024)


def _full(shape):
    nd = len(shape)
    return pl.BlockSpec(shape, lambda *_: (0,) * nd)


def _mods_kernel(c_ref, w_ref, b_ref, o_ref):
    c = c_ref[...]
    s = c / (1.0 + jnp.exp(-c))
    o_ref[0] = jnp.dot(s, w_ref[0], preferred_element_type=F32, precision=HI) + b_ref[0]


def _mods(cvec, ada_w, ada_b):
    depth, d, n6 = ada_w.shape
    tn = 1536
    return pl.pallas_call(
        _mods_kernel,
        grid=(depth, n6 // tn),
        in_specs=[_full((SUBLANES, d)),
                  pl.BlockSpec((1, d, tn), lambda l, j: (l, 0, j)),
                  pl.BlockSpec((1, 1, tn), lambda l, j: (l, 0, j))],
        out_specs=pl.BlockSpec((1, SUBLANES, tn), lambda l, j: (l, 0, j)),
        out_shape=jax.ShapeDtypeStruct((depth, SUBLANES, n6), F32),
        compiler_params=_params(("arbitrary", "arbitrary")),
        name="ada_mods",
    )(cvec, ada_w, ada_b.reshape(depth, 1, n6))


def _modulated_norm(x, gain, shift, scale):
    ms = jnp.mean(x * x, axis=-1, keepdims=True)
    return (x * lax.rsqrt(ms + RMS_EPS)) * gain * (1.0 + scale) + shift


def _shifted_rows(xm, prev8, next8, i, nt):
    pz = jnp.where(i > 0, prev8, 0.0)
    nz = jnp.where(i < nt - 1, next8, 0.0)
    ext = jnp.concatenate([pz, xm, nz], axis=0)
    r = ext.shape[0]
    xp = pltpu.roll(ext, 1, 0)[SUBLANES:r - SUBLANES]
    xn = pltpu.roll(ext, r - 1, 0)[SUBLANES:r - SUBLANES]
    return xp, xn


def _halo_specs(tm, t, width):
    nb = tm // SUBLANES
    last = t // SUBLANES - 1
    return [pl.BlockSpec((1, tm, width), lambda b, i: (b, i, 0)),
            pl.BlockSpec((1, SUBLANES, width), lambda b, i: (b, jnp.maximum(i * nb - 1, 0), 0)),
            pl.BlockSpec((1, SUBLANES, width), lambda b, i: (b, jnp.minimum((i + 1) * nb, last), 0))]


def _inproj_kernel(x_ref, sh_ref, sc_ref, g_ref, w_ref, qg_ref, kg_ref, bd_ref, cos_ref, sa_ref, sb_ref,
                   q_ref, k_ref, vt_ref, zc_ref):
    h = _modulated_norm(x_ref[0], g_ref[...], sh_ref[0], sc_ref[0])
    z = jnp.dot(h.astype(BF16), w_ref[...], preferred_element_type=F32)
    zc_ref[0] = z[:, QKV_W:]
    cos, sa, sb = cos_ref[...], sa_ref[...], sb_ref[...]

    def head_norm_rope(t, gain, bd):
        ssq = jnp.dot((t * t).astype(BF16), bd, preferred_element_type=F32)
        t = t * lax.rsqrt(ssq * (1.0 / HEAD_DIM) + RMS_EPS) * gain
        outs = []
        for j in range(t.shape[1] // LANES):
            tj = t[:, j * LANES:(j + 1) * LANES]
            outs.append(tj * cos + pltpu.roll(tj, LANES - 16, 1) * sa + pltpu.roll(tj, 16, 1) * sb)
        return outs

    qs = head_norm_rope(z[:, :Q_W], qg_ref[...], bd_ref[...])
    ks = head_norm_rope(z[:, Q_W:Q_W + KV_W], kg_ref[...], bd_ref[:KV_W, :KV_W])
    qscale = HEAD_DIM ** -0.5 * LOG2E
    for j, qj in enumerate(qs):
        qj = (qj * qscale).astype(BF16)
        q_ref[0, 2 * j] = qj[:, :HEAD_DIM]
        q_ref[0, 2 * j + 1] = qj[:, HEAD_DIM:]
    for j, kj in enumerate(ks):
        kj = kj.astype(BF16)
        k_ref[0, 2 * j] = kj[:, :HEAD_DIM]
        k_ref[0, 2 * j + 1] = kj[:, HEAD_DIM:]
    tm = z.shape[0]
    vt = z[:, Q_W + KV_W:QKV_W].T
    ones_row = (lax.broadcasted_iota(jnp.int32, (LANES - HEAD_DIM, tm), 0) == 0).astype(F32)
    for hh in range(N_KV_HEADS):
        vt_ref[0, hh] = jnp.concatenate([vt[hh * HEAD_DIM:(hh + 1) * HEAD_DIM], ones_row], axis=0).astype(BF16)


def _inproj(x, sh, sc, gain, w_bf, qg, kg, bd, cos, sa, sb):
    bx, t, d = x.shape
    tm = min(512, t)
    vec = pl.BlockSpec((1, 1, d), lambda b, i: (b, 0, 0))
    tab = pl.BlockSpec((tm, LANES), lambda b, i: (i, 0))
    return pl.pallas_call(
        _inproj_kernel,
        grid=(bx, t // tm),
        in_specs=[pl.BlockSpec((1, tm, d), lambda b, i: (b, i, 0)), vec, vec, _full((1, d)),
                  _full((d, MIX_IN_W)), _full((1, Q_W)), _full((1, KV_W)), _full((Q_W, Q_W)), tab, tab, tab],
        out_specs=[pl.BlockSpec((1, N_Q_HEADS, tm, HEAD_DIM), lambda b, i: (b, 0, i, 0)),
                   pl.BlockSpec((1, N_KV_HEADS, tm, HEAD_DIM), lambda b, i: (b, 0, i, 0)),
                   pl.BlockSpec((1, N_KV_HEADS, LANES, tm), lambda b, i: (b, 0, 0, i)),
                   pl.BlockSpec((1, tm, 3 * CONV_W), lambda b, i: (b, i, 0))],
        out_shape=[jax.ShapeDtypeStruct((bx, N_Q_HEADS, t, HEAD_DIM), BF16),
                   jax.ShapeDtypeStruct((bx, N_KV_HEADS, t, HEAD_DIM), BF16),
                   jax.ShapeDtypeStruct((bx, N_KV_HEADS, LANES, t), BF16),
                   jax.ShapeDtypeStruct((bx, t, 3 * CONV_W), F32)],
        compiler_params=_params(("arbitrary", "arbitrary")),
        name="in_proj",
    )(x, sh, sc, gain, w_bf, qg, kg, bd, cos, sa, sb)


def _scores_t(ks, q):
    return lax.dot_general(ks, q, (((1,), (1,)), ((), ())), preferred_element_type=F32)


def _sink_column(sink_ref, hh, tq, m, acc):
    col = lax.broadcasted_iota(jnp.int32, m.shape, 1)
    sink = jnp.where(col < tq, sink_ref[hh, 0], sink_ref[hh, 1]) * LOG2E
    m_new = jnp.maximum(m, sink)
    den_row = lax.broadcasted_iota(jnp.int32, acc.shape, 0) == HEAD_DIM
    return acc * jnp.exp2(m - m_new) + jnp.where(den_row, jnp.exp2(sink - m_new), 0.0)


def _attn_finish(acc, tq):
    o = (acc / acc[HEAD_DIM:HEAD_DIM + 1]).T
    return jnp.concatenate([o[:tq, :HEAD_DIM], o[tq:, :HEAD_DIM]], axis=1)


EXP_CHUNK_ELEMS = 32 * SUBLANES * LANES


def _score_stage(ks, q, s_ref, bias=None):
    s = _scores_t(ks, q)
    if bias is not None:
        s = s + bias
    s_ref[0:s.shape[0]] = s
    return jnp.max(s, axis=0, keepdims=True)


def _softmax_stage(s_ref, p_ref, n, vt, mx, m, acc_ref):
    m_new = jnp.maximum(m, mx)
    rows = EXP_CHUNK_ELEMS // m.shape[1]
    for c in range(0, n, rows):
        p_ref[c:c + rows] = jnp.exp2(s_ref[c:c + rows] - m_new).astype(BF16)
    acc_ref[...] = acc_ref[...] * jnp.exp2(m - m_new) + jnp.dot(vt, p_ref[0:n], preferred_element_type=F32)
    return m_new


def _flash_kernel(*refs, tq, tk, n_main, has_ctx, has_sink):
    sa_ref, sb_ref, pa_ref, pb_ref, acc_ref = refs[-5:]
    o_ref = refs[-6]
    refs = list(refs[:-6])
    sink_ref = refs.pop(0) if has_sink else None
    q_ref, k_ref, vt_ref = refs[:3]
    m_cols = GROUP * tq
    q = q_ref[0].reshape(m_cols, HEAD_DIM)

    def k_tile(j):
        return k_ref[0, 0, pl.ds(pl.multiple_of(j * tk, tk), tk), :]

    def vt_tile(j):
        return vt_ref[0, 0, :, pl.ds(pl.multiple_of(j * tk, tk), tk)]

    acc_ref[...] = jnp.zeros_like(acc_ref)
    m = jnp.full((1, m_cols), NEG_INF, F32)
    mx_a = _score_stage(k_tile(0), q, sa_ref)
    if n_main > 1:
        def body(i, carry):
            m, mx_a = carry
            mx_b = _score_stage(k_tile(2 * i + 1), q, sb_ref)
            m = _softmax_stage(sa_ref, pa_ref, tk, vt_tile(2 * i), mx_a, m, acc_ref)
            mx_a = _score_stage(k_tile(2 * i + 2), q, sa_ref)
            m = _softmax_stage(sb_ref, pb_ref, tk, vt_tile(2 * i + 1), mx_b, m, acc_ref)
            return m, mx_a

        m, mx_a = lax.fori_loop(0, n_main // 2 - 1, body, (m, mx_a))
        mx_b = _score_stage(k_tile(n_main - 1), q, sb_ref)
        m = _softmax_stage(sa_ref, pa_ref, tk, vt_tile(n_main - 2), mx_a, m, acc_ref)
        last = (sb_ref, pb_ref, mx_b)
        spare = (sa_ref, pa_ref)
    else:
        last = (sa_ref, pa_ref, mx_a)
        spare = (sb_ref, pb_ref)
    if has_ctx:
        n_ctx = refs[3].shape[2]
        mx_c = _score_stage(refs[3][0, 0], q, spare[0])
    m = _softmax_stage(last[0], last[1], tk, vt_tile(n_main - 1), last[2], m, acc_ref)
    if has_ctx:
        m = _softmax_stage(spare[0], spare[1], n_ctx, refs[4][0, 0], mx_c, m, acc_ref)
    acc = acc_ref[...]
    if has_sink:
        acc = _sink_column(sink_ref, pl.program_id(1), tq, m, acc)
    o_ref[0] = _attn_finish(acc, tq).astype(o_ref.dtype)


def _kv_specs(n):
    return [pl.BlockSpec((1, 1, n, HEAD_DIM), lambda b, h, i: (b, h, 0, 0)),
            pl.BlockSpec((1, 1, LANES, n), lambda b, h, i: (b, h, 0, 0))]


def _dense_attention(q, k, vt, kc=None, vct=None, sink=None):
    bx, _, t, _ = q.shape
    s = k.shape[2]
    tq = min(512, t)
    tk = min(512, s)
    has_ctx = kc is not None
    has_sink = sink is not None
    in_specs = [pl.BlockSpec((1, GROUP, tq, HEAD_DIM), lambda b, h, i: (b, h, i, 0))] + _kv_specs(s)
    args = [q, k, vt]
    if has_ctx:
        in_specs += _kv_specs(kc.shape[2])
        args += [kc, vct]
    if has_sink:
        in_specs = [pl.BlockSpec(memory_space=pltpu.SMEM)] + in_specs
        args = [sink] + args
    n_main = s // tk
    assert n_main == 1 or n_main % 2 == 0
    rows = max(tk, kc.shape[2]) if has_ctx else tk
    m_cols = GROUP * tq
    return pl.pallas_call(
        functools.partial(_flash_kernel, tq=tq, tk=tk, n_main=n_main, has_ctx=has_ctx, has_sink=has_sink),
        grid=(bx, N_KV_HEADS, t // tq),
        in_specs=in_specs,
        out_specs=pl.BlockSpec((1, tq, LANES), lambda b, h, i: (b, i, h)),
        out_shape=jax.ShapeDtypeStruct((bx, t, Q_W), BF16),
        scratch_shapes=[pltpu.VMEM((rows, m_cols), F32), pltpu.VMEM((rows, m_cols), F32),
                        pltpu.VMEM((rows, m_cols), BF16), pltpu.VMEM((rows, m_cols), BF16),
                        pltpu.VMEM((LANES, m_cols), F32)],
        compiler_params=_params(("arbitrary", "arbitrary", "arbitrary")),
        name="dense_attention",
    )(*args)


BAND_SUB = 256


@functools.lru_cache(maxsize=None)
def _band_bias():
    span = BAND_SUB + 2 * WINDOW
    kr = np.arange(span)[None, :, None]
    qc = (np.arange(GROUP * BAND_SUB) % BAND_SUB)[None, None, :]
    rel = np.arange(3)[:, None, None]
    return np.where(np.abs(kr - qc - rel * WINDOW) <= WINDOW, 0.0, NEG_INF).astype(np.float32)


def _banded_kernel(sink_ref, q_ref, k_ref, vt_ref, kc_ref, vct_ref, bias_ref, o_ref,
                   sl_ref, sc_ref, pl_ref, pc_ref, acc_ref, *, n_sub, s_len):
    i = pl.program_id(2)
    sub = BAND_SUB
    m_cols = GROUP * sub
    span = sub + 2 * WINDOW
    n_ctx = kc_ref.shape[2]
    stats = []
    for u in range(n_sub):
        q = q_ref[0, :, u * sub:(u + 1) * sub, :].reshape(m_cols, HEAD_DIM)
        q0 = (i * n_sub + u) * sub
        start = pl.multiple_of(jnp.clip(q0 - WINDOW, 0, s_len - span), WINDOW)
        mx_l = _score_stage(k_ref[0, 0, pl.ds(start, span), :], q, sl_ref.at[u], bias=bias_ref[(q0 - start) // WINDOW])
        mx_c = _score_stage(kc_ref[0, 0], q, sc_ref.at[u])
        stats.append((start, mx_l, mx_c))
    for u in range(n_sub):
        start, mx_l, mx_c = stats[u]
        acc_u = acc_ref.at[u]
        acc_u[...] = jnp.zeros((LANES, m_cols), F32)
        m = jnp.full((1, m_cols), NEG_INF, F32)
        m = _softmax_stage(sl_ref.at[u], pl_ref.at[u], span, vt_ref[0, 0, :, pl.ds(start, span)], mx_l, m, acc_u)
        m = _softmax_stage(sc_ref.at[u], pc_ref.at[u], n_ctx, vct_ref[0, 0], mx_c, m, acc_u)
        acc = _sink_column(sink_ref, pl.program_id(1), sub, m, acc_u[...])
        o_ref[0, u * sub:(u + 1) * sub] = _attn_finish(acc, sub).astype(o_ref.dtype)


def _banded_attention(q, k, vt, kc, vct, sink):
    bx, _, t, _ = q.shape
    n_ctx = kc.shape[2]
    n_sub = 2
    tq = n_sub * BAND_SUB
    m_cols = GROUP * BAND_SUB
    span = BAND_SUB + 2 * WINDOW
    bias = jnp.asarray(_band_bias())
    return pl.pallas_call(
        functools.partial(_banded_kernel, n_sub=n_sub, s_len=t),
        grid=(bx, N_KV_HEADS, t // tq),
        in_specs=[pl.BlockSpec(memory_space=pltpu.SMEM),
                  pl.BlockSpec((1, GROUP, tq, HEAD_DIM), lambda b, h, i: (b, h, i, 0))]
                 + _kv_specs(t) + _kv_specs(n_ctx) + [_full(bias.shape)],
        out_specs=pl.BlockSpec((1, tq, LANES), lambda b, h, i: (b, i, h)),
        out_shape=jax.ShapeDtypeStruct((bx, t, Q_W), BF16),
        scratch_shapes=[pltpu.VMEM((n_sub, span, m_cols), F32), pltpu.VMEM((n_sub, n_ctx, m_cols), F32),
                        pltpu.VMEM((n_sub, span, m_cols), BF16), pltpu.VMEM((n_sub, n_ctx, m_cols), BF16),
                        pltpu.VMEM((n_sub, LANES, m_cols), F32)],
        compiler_params=_params(("arbitrary", "arbitrary", "arbitrary")),
        name="banded_attention",
    )(sink, q, k, vt, kc, vct, bias)


def _hyena_pre_kernel(z_ref, zp_ref, zn_ref, w_ref, b_ref, u_ref, x0_ref):
    i = pl.program_id(1)
    zm = z_ref[0]
    zp, zn = _shifted_rows(zm, zp_ref[0], zn_ref[0], i, pl.num_programs(1))
    w = w_ref[...]
    c = zp * w[0:1] + zm * w[1:2] + zn * w[2:3] + b_ref[...]
    x0_ref[0] = c[:, :CONV_W]
    u_ref[0] = c[:, 2 * CONV_W:] * c[:, CONV_W:2 * CONV_W]


def _hyena_pre(zc, conv_w, conv_b):
    bx, t, w3 = zc.shape
    tm = min(512, t)
    out = jax.ShapeDtypeStruct((bx, t, CONV_W), F32)
    ospec = pl.BlockSpec((1, tm, CONV_W), lambda b, i: (b, i, 0))
    return pl.pallas_call(
        _hyena_pre_kernel,
        grid=(bx, t // tm),
        in_specs=_halo_specs(tm, t, w3) + [_full((3, w3)), _full((1, w3))],
        out_specs=[ospec, ospec],
        out_shape=[out, out],
        compiler_params=_params(("arbitrary", "arbitrary")),
        name="hyena_pre",
    )(zc, zc, zc, conv_w, conv_b.reshape(1, w3))


def _short_conv_kernel(z_ref, zp_ref, zn_ref, w_ref, o_ref):
    i = pl.program_id(1)
    prod = lambda z: z[:, CONV_W:2 * CONV_W] * z[:, 2 * CONV_W:]
    zm = z_ref[0]
    pm = prod(zm)
    pp, pn = _shifted_rows(pm, prod(zp_ref[0]), prod(zn_ref[0]), i, pl.num_programs(1))
    w = w_ref[...]
    o_ref[0] = (zm[:, :CONV_W] * (pp * w[0:1] + pm * w[1:2] + pn * w[2:3])).astype(o_ref.dtype)


def _short_conv(zc, conv_w):
    bx, t, w3 = zc.shape
    tm = min(512, t)
    return pl.pallas_call(
        _short_conv_kernel,
        grid=(bx, t // tm),
        in_specs=_halo_specs(tm, t, w3) + [_full((3, CONV_W))],
        out_specs=pl.BlockSpec((1, tm, CONV_W), lambda b, i: (b, i, 0)),
        out_shape=jax.ShapeDtypeStruct((bx, t, CONV_W), BF16),
        compiler_params=_params(("arbitrary", "arbitrary")),
        name="short_conv",
    )(zc, zc, zc, conv_w)


@functools.lru_cache(maxsize=None)
def _filter_features(n):
    j = np.arange(2 * n)
    d = np.where(j <= n, j, 2 * n - j)
    d = np.where(j == n, 0, d)
    bands = (FILT_EMB - 1) // 2
    t01 = np.linspace(0.0, 1.0, n)[d]
    w = 2.0 * np.pi * d.astype(np.float64) / n
    f = np.linspace(1e-4, bands - 1, bands)[None, :]
    feats = np.zeros((2 * n, LANES), np.float64)
    feats[:, 0] = t01
    feats[:, 1:1 + bands] = np.cos(f * w[:, None])
    feats[:, 1 + bands:FILT_EMB] = -np.sin(f * w[:, None])
    feats[:, 64] = t01
    feats[:, 65] = (j < n)
    feats[:, 66] = (j != n)
    return feats.astype(np.float32)


def _filter_kernel(f_ref, w1_ref, b1_ref, w2_ref, b2_ref, w3_ref, b3_ref, w4_ref, fr_ref, dl_ref, k_ref, s_ref):
    f = f_ref[...]
    fr = fr_ref[...]
    mm = lambda a, b: jnp.dot(a, b, preferred_element_type=F32, precision=HI)
    h = jnp.sin(fr * (mm(f, w1_ref[...]) + b1_ref[...]))
    h = jnp.sin(fr * (mm(h, w2_ref[...]) + b2_ref[...]))
    h = jnp.sin(fr * (mm(h, w3_ref[...]) + b3_ref[...]))
    hf = mm(h, w4_ref[...])
    win = jnp.exp(-f[:, 64:65] * dl_ref[...]) + DECAY_SHIFT
    k = jnp.where(f[:, 65:66] > 0.5, hf[:, :CONV_W], hf[:, CONV_W:]) * win * f[:, 66:67]
    k_ref[...] = k

    @pl.when(pl.program_id(0) == 0)
    def _():
        s_ref[...] = jnp.zeros_like(s_ref)

    s_ref[...] += jnp.sum(jnp.abs(k), axis=0, keepdims=True)


def _implicit_filter(n, w1, b1, w2, b2, w3, b3, w4, freq):
    feats = jnp.asarray(_filter_features(n))
    pad = LANES - FILT_WIDTH
    padc = lambda a: jnp.pad(a.reshape(1, -1), ((0, 0), (0, pad)))
    w1p = jnp.pad(w1, ((0, LANES - FILT_EMB), (0, pad)))
    w2p = jnp.pad(w2, ((0, pad), (0, pad)))
    w3p = jnp.pad(w3, ((0, pad), (0, pad)))
    w4p = jnp.pad(w4, ((0, pad), (0, 0)))
    deltas = np.abs(np.linspace(math.log(DECAY_TARGET) / SLOW_DECAY_PCT, math.log(DECAY_TARGET) / FAST_DECAY_PCT,
                                CONV_W)).astype(np.float32).reshape(1, CONV_W)
    tr = min(1024, 2 * n)
    sq = _full((LANES, LANES))
    vec = _full((1, LANES))
    return pl.pallas_call(
        _filter_kernel,
        grid=(2 * n // tr,),
        in_specs=[pl.BlockSpec((tr, LANES), lambda i: (i, 0)), sq, vec, sq, vec, sq, vec,
                  _full((LANES, 2 * CONV_W)), vec, _full((1, CONV_W))],
        out_specs=[pl.BlockSpec((tr, CONV_W), lambda i: (i, 0)), _full((1, CONV_W))],
        out_shape=[jax.ShapeDtypeStruct((2 * n, CONV_W), F32), jax.ShapeDtypeStruct((1, CONV_W), F32)],
        compiler_params=_params(("arbitrary",)),
        name="hyena_filter",
    )(feats, w1p, padc(b1), w2p, padc(b2), w3p, padc(b3), w4p, padc(freq), jnp.asarray(deltas))


def _twiddle(idx, mod):
    ang = 2.0 * np.pi * (idx % mod) / mod
    return np.cos(ang), -np.sin(ang)


def _real_form(mr, mi):
    return np.concatenate([np.concatenate([mr, -mi], -1), np.concatenate([mi, mr], -1)], -2)


@functools.lru_cache(maxsize=None)
def _dft_tables(n2):
    n1 = DFT_N1
    n = n1 * n2
    h = n2 // 2
    a2 = np.arange(n2)
    fr, fi = _twiddle(np.outer(a2, a2), n2)
    m_data = _real_form(fr[:, :h], fi[:, :h])
    m_filt = np.concatenate([fr, fi], 0)
    k2 = a2[:, None, None]
    k1 = np.arange(n1)[None, :, None]
    c1 = np.arange(n1)[None, None, :]
    g = _real_form(*_twiddle(c1 * (n2 * k1 + k2), n))
    a1 = np.arange(n1)
    f1 = _real_form(*_twiddle(np.outer(a1, a1), n1))
    t2 = a1[:, None, None]
    t1 = np.arange(h)[None, :, None]
    j1 = a2[None, None, :]
    hh = _real_form(*_twiddle(j1 * (n1 * t1 + t2), n))
    cast = lambda m: np.asarray(m, dtype=BF16)
    return cast(m_data), cast(m_filt), cast(g), cast(f1), cast(hh)


def _dft_rows_kernel(m_ref, *refs):
    o_ref = refs[-1]
    rhs = jnp.concatenate([r[0] for r in refs[:-1]], axis=0).astype(BF16)
    out = jnp.dot(m_ref[...], rhs, preferred_element_type=F32)
    half = out.shape[0] // 2
    o_ref[0, 0] = out[:half].astype(o_ref.dtype)
    o_ref[0, 1] = out[half:].astype(o_ref.dtype)


def _dft_stage1(mat, views, pairs, rows, n2, width, lc):
    nin = len(views)
    in_specs = [_full(mat.shape)] + [
        pl.BlockSpec((1, rows, lc), (lambda p, j, a=a: (nin * p + a, 0, j))) for a in range(nin)]
    return pl.pallas_call(
        _dft_rows_kernel,
        grid=(pairs, width // lc),
        in_specs=in_specs,
        out_specs=pl.BlockSpec((1, 2, n2, lc), lambda p, j: (p, 0, 0, j)),
        out_shape=jax.ShapeDtypeStruct((pairs, 2, n2, width), BF16),
        compiler_params=_params(("arbitrary", "arbitrary")),
        name="dft_stage1",
    )(mat, *views)


def _spectrum_kernel(a_ref, g_ref, sc_ref, o_ref, *, kb):
    for j in range(kb):
        rhs = jnp.concatenate([a_ref[0, 0, j], a_ref[0, 1, j]], axis=0)
        x = jnp.dot(g_ref[j], rhs, preferred_element_type=F32)
        o_ref[0, j] = x[:DFT_N1] * sc_ref[...]
        o_ref[1, j] = x[DFT_N1:] * sc_ref[...]


def _filter_spectrum(a, g, scale, n2, kb):
    c = a.shape[-1]
    return pl.pallas_call(
        functools.partial(_spectrum_kernel, kb=kb),
        grid=(n2 // kb,),
        in_specs=[pl.BlockSpec((1, 2, kb, DFT_N1, c), lambda k: (0, 0, k, 0, 0)),
                  pl.BlockSpec((kb, 2 * DFT_N1, 2 * DFT_N1), lambda k: (k, 0, 0)), _full((1, c))],
        out_specs=pl.BlockSpec((2, kb, DFT_N1, c), lambda k: (0, k, 0, 0)),
        out_shape=jax.ShapeDtypeStruct((2, n2, DFT_N1, c), F32),
        compiler_params=_params(("arbitrary",)),
        name="filter_spectrum",
    )(a, g, scale)


def _dft_mid_kernel(a_ref, g_ref, kh_ref, f_ref, o_ref, *, kb):
    for j in range(kb):
        rhs = jnp.concatenate([a_ref[0, 0, j], a_ref[0, 1, j]], axis=0)
        x = jnp.dot(g_ref[j], rhs, preferred_element_type=F32)
        xr, xi = x[:DFT_N1], x[DFT_N1:]
        kr, ki = kh_ref[0, j], kh_ref[1, j]
        yr = xr * kr - xi * ki
        yi = xr * ki + xi * kr
        v = jnp.concatenate([yr, -yi], axis=0).astype(BF16)
        b = jnp.dot(f_ref[...], v, preferred_element_type=F32)
        o_ref[0, 0, j] = b[:DFT_N1].astype(o_ref.dtype)
        o_ref[0, 1, j] = b[DFT_N1:].astype(o_ref.dtype)


def _dft_mid(a, g, khat, f1, pairs, n2, kb):
    c = a.shape[-1]
    blk = pl.BlockSpec((1, 2, kb, DFT_N1, c), lambda k, p: (p, 0, k, 0, 0))
    return pl.pallas_call(
        functools.partial(_dft_mid_kernel, kb=kb),
        grid=(n2 // kb, pairs),
        in_specs=[blk, pl.BlockSpec((kb, 2 * DFT_N1, 2 * DFT_N1), lambda k, p: (k, 0, 0)),
                  pl.BlockSpec((2, kb, DFT_N1, c), lambda k, p: (0, k, 0, 0)), _full(f1.shape)],
        out_specs=blk,
        out_shape=jax.ShapeDtypeStruct(a.shape, BF16),
        compiler_params=_params(("arbitrary", "arbitrary")),
        name="dft_mid",
    )(a, g, khat, f1)


def _dft_last_kernel(b_ref, h_ref, u_ref, x0_ref, bd_ref, o_ref, *, tc, c):
    half = h_ref.shape[1] // 2
    bd = bd_ref[...]
    for t in range(tc):
        cols = slice(t * c, (t + 1) * c)
        rhs = jnp.concatenate([b_ref[0, 0, :, cols], b_ref[0, 1, :, cols]], axis=0)
        v = jnp.dot(h_ref[t], rhs, preferred_element_type=F32)
        y0, y1 = v[:half], -v[half:]
        o_ref[0, :, cols] = ((y0 + u_ref[0, :, cols] * bd) * x0_ref[0, :, cols]).astype(o_ref.dtype)
        o_ref[1, :, cols] = ((y1 + u_ref[1, :, cols] * bd) * x0_ref[1, :, cols]).astype(o_ref.dtype)


def _dft_last(bm, hh, u_view, x0_view, bias_d, pairs, n2, c, tc):
    half = n2 // 2
    width = DFT_N1 * c
    io = pl.BlockSpec((2, half, tc * c), lambda p, j: (p, 0, j))
    return pl.pallas_call(
        functools.partial(_dft_last_kernel, tc=tc, c=c),
        grid=(pairs, DFT_N1 // tc),
        in_specs=[pl.BlockSpec((1, 2, n2, tc * c), lambda p, j: (p, 0, 0, j)),
                  pl.BlockSpec((tc, n2, 2 * n2), lambda p, j: (j, 0, 0)), io, io, _full((1, c))],
        out_specs=io,
        out_shape=jax.ShapeDtypeStruct((2 * pairs, half, width), BF16),
        compiler_params=_params(("arbitrary", "arbitrary")),
        name="dft_last",
    )(bm, hh, u_view, x0_view, bias_d)


def _long_conv_mixer(u, x0, kfilt, ksum, bias_d):
    b, n, c = u.shape
    n2 = 2 * n // DFT_N1
    half = n2 // 2
    pairs = b // 2
    width = DFT_N1 * c
    m_data, m_filt, g, f1, hh = (jnp.asarray(t) for t in _dft_tables(n2))
    lc = min(4096, width)
    kb = min(8, n2)
    scale = 1.0 / (ksum * float(DFT_N1 * n2))
    ka = _dft_stage1(m_filt, [kfilt.reshape(1, n2, width)], 1, n2, n2, width, lc)
    khat = _filter_spectrum(ka.reshape(1, 2, n2, DFT_N1, c), g, scale, n2, kb)
    u_view = u.reshape(b, half, width)
    a = _dft_stage1(m_data, [u_view, u_view], pairs, half, n2, width, lc)
    bm = _dft_mid(a.reshape(pairs, 2, n2, DFT_N1, c), g, khat, f1, pairs, n2, kb)
    out = _dft_last(bm.reshape(pairs, 2, n2, width), hh, u_view, x0.reshape(b, half, width),
                    bias_d.reshape(1, c), pairs, n2, c, min(8, DFT_N1))
    return out.reshape(b, n, c)


@functools.lru_cache(maxsize=None)
def _small_dft_tables(n):
    big = 2 * n
    a = np.arange(big)
    fr, fi = _twiddle(np.outer(a, a), big)
    cast = lambda m: np.asarray(m, dtype=BF16)
    return (cast(np.concatenate([fr, fi], 0)),
            cast(_real_form(fr[:, :n], fi[:, :n])),
            cast(_real_form(fr[:n], fi[:n])))


def _small_conv_kernel(k_ref, ks_ref, u_ref, x0_ref, bd_ref, mf_ref, md_ref, mi_ref, o_ref, *, n):
    big = 2 * n
    kh = jnp.dot(mf_ref[...], k_ref[...].astype(BF16), preferred_element_type=F32) * (1.0 / (ks_ref[...] * big))
    kr, ki = kh[:big], kh[big:]
    rhs = jnp.concatenate([u_ref[0], u_ref[1]], axis=0).astype(BF16)
    x = jnp.dot(md_ref[...], rhs, preferred_element_type=F32)
    xr, xi = x[:big], x[big:]
    v = jnp.concatenate([xr * kr - xi * ki, -(xr * ki + xi * kr)], axis=0).astype(BF16)
    y = jnp.dot(mi_ref[...], v, preferred_element_type=F32)
    bd = bd_ref[...]
    o_ref[0] = ((y[:n] + u_ref[0] * bd) * x0_ref[0]).astype(o_ref.dtype)
    o_ref[1] = ((-y[n:] + u_ref[1] * bd) * x0_ref[1]).astype(o_ref.dtype)


def _small_conv_mixer(u, x0, kfilt, ksum, bias_d):
    b, n, c = u.shape
    mf, md, mi = (jnp.asarray(t) for t in _small_dft_tables(n))
    io = pl.BlockSpec((2, n, c), lambda p: (p, 0, 0))
    return pl.pallas_call(
        functools.partial(_small_conv_kernel, n=n),
        grid=(b // 2,),
        in_specs=[_full((2 * n, c)), _full((1, c)), io, io, _full((1, c)),
                  _full(mf.shape), _full(md.shape), _full(mi.shape)],
        out_specs=io,
        out_shape=jax.ShapeDtypeStruct((b, n, c), BF16),
        compiler_params=_params(("arbitrary",)),
        name="small_conv",
    )(kfilt, ksum, u, x0, bias_d.reshape(1, c), mf, md, mi)


def _outproj_kernel(x_ref, oa_ref, oc_ref, wa_ref, wc_ref, g_ref, o_ref):
    y = (jnp.dot(oa_ref[0], wa_ref[...], preferred_element_type=F32)
         + jnp.dot(oc_ref[0], wc_ref[...], preferred_element_type=F32))
    o_ref[0] = x_ref[0] + g_ref[0] * y


def _outproj(x, oa, oc, w_bf, gate):
    bx, t, d = x.shape
    tm = min(512, t)
    row = lambda w: pl.BlockSpec((1, tm, w), lambda b, i: (b, i, 0))
    return pl.pallas_call(
        _outproj_kernel,
        grid=(bx, t // tm),
        in_specs=[row(d), row(Q_W), row(CONV_W), pl.BlockSpec((Q_W, d), lambda b, i: (0, 0)),
                  pl.BlockSpec((CONV_W, d), lambda b, i: (1, 0)), pl.BlockSpec((1, 1, d), lambda b, i: (b, 0, 0))],
        out_specs=row(d),
        out_shape=jax.ShapeDtypeStruct((bx, t, d), F32),
        compiler_params=_params(("arbitrary", "arbitrary")),
        name="out_proj",
    )(x, oa, oc, w_bf, w_bf, gate)


FFN_CHUNK = D_FF // 2


def _gelu_tanh(x):
    return 0.5 * x * (1.0 + jnp.tanh(math.sqrt(2.0 / math.pi) * (x + 0.044715 * (x * x * x))))


def _ffn_kernel(x_ref, xp_ref, xn_ref, sh_ref, sc_ref, gt_ref, g_ref, wu_ref, cw_ref, cb_ref, wd_ref, o_ref):
    i = pl.program_id(1)
    nt = pl.num_programs(1)
    xm = x_ref[0]
    tm = xm.shape[0]
    ext = jnp.concatenate([xp_ref[0], xm, xn_ref[0]], axis=0)
    r = tm + 2 * SUBLANES
    h = _modulated_norm(ext, g_ref[...], sh_ref[0], sc_ref[0])
    row = lax.broadcasted_iota(jnp.int32, (r, 1), 0)
    inside = jnp.logical_and(jnp.logical_or(i > 0, row >= SUBLANES), jnp.logical_or(i < nt - 1, row < tm + SUBLANES))
    h = jnp.where(inside, h, 0.0).astype(BF16)
    hm = h[SUBLANES:tm + SUBLANES]
    acc = jnp.zeros((tm, D_MODEL), F32)
    for c0 in range(0, D_FF, FFN_CHUNK):
        a = jnp.dot(h, wu_ref[:, c0:c0 + FFN_CHUNK], preferred_element_type=F32)
        v = jnp.dot(hm, wu_ref[:, D_FF + c0:D_FF + c0 + FFN_CHUNK], preferred_element_type=F32)
        ap = pltpu.roll(a, 1, 0)[SUBLANES:tm + SUBLANES]
        an = pltpu.roll(a, r - 1, 0)[SUBLANES:tm + SUBLANES]
        cw = cw_ref[:, c0:c0 + FFN_CHUNK]
        conv = ap * cw[0:1] + a[SUBLANES:tm + SUBLANES] * cw[1:2] + an * cw[2:3] + cb_ref[:, c0:c0 + FFN_CHUNK]
        act = (_gelu_tanh(conv) * v).astype(BF16)
        acc = acc + jnp.dot(act, wd_ref[c0:c0 + FFN_CHUNK, :], preferred_element_type=F32)
    o_ref[0] = xm + gt_ref[0] * acc


def _ffn(x, sh, sc, gate, gain, wu_bf, conv_w, conv_b, wd_bf):
    bx, t, d = x.shape
    tm = min(512, t)
    vec = pl.BlockSpec((1, 1, d), lambda b, i: (b, 0, 0))
    once = pl.Buffered(1)
    return pl.pallas_call(
        _ffn_kernel,
        grid=(bx, t // tm),
        in_specs=_halo_specs(tm, t, d) + [
            vec, vec, vec, _full((1, d)),
            pl.BlockSpec((d, 2 * D_FF), lambda b, i: (0, 0), pipeline_mode=once),
            _full((3, D_FF)), _full((1, D_FF)),
            pl.BlockSpec((D_FF, d), lambda b, i: (0, 0), pipeline_mode=once)],
        out_specs=pl.BlockSpec((1, tm, d), lambda b, i: (b, i, 0)),
        out_shape=jax.ShapeDtypeStruct((bx, t, d), F32),
        compiler_params=_params(("arbitrary", "arbitrary")),
        name="conv_ffn",
    )(x, x, x, sh, sc, gate, gain, wu_bf, conv_w, conv_b.reshape(1, D_FF), wd_bf)


@functools.lru_cache(maxsize=None)
def _rope_tables(t):
    pos = np.arange(t)
    n_freq = HEAD_DIM // 4
    inv = ROPE_THETA ** (-np.arange(n_freq, dtype=np.float64) / n_freq)
    ang_r = (pos // GRID_W)[:, None] * inv
    ang_c = (pos % GRID_W)[:, None] * inv
    zero = np.zeros_like(ang_r)
    cos = np.concatenate([np.cos(ang_r)] * 2 + [np.cos(ang_c)] * 2, 1)
    sa = np.concatenate([-np.sin(ang_r), zero, -np.sin(ang_c), zero], 1)
    sb = np.concatenate([zero, np.sin(ang_r), zero, np.sin(ang_c)], 1)
    tile = lambda m: np.tile(m, (1, LANES // HEAD_DIM)).astype(np.float32)
    return tile(cos), tile(sa), tile(sb)


@functools.lru_cache(maxsize=None)
def _identity_rope_tables(t):
    return np.ones((t, LANES), np.float32), np.zeros((t, LANES), np.float32), np.zeros((t, LANES), np.float32)


@functools.lru_cache(maxsize=None)
def _head_block_diag():
    hid = np.arange(Q_W) // HEAD_DIM
    return np.asarray(hid[:, None] == hid[None, :], dtype=BF16)


def kernel(x, c, ctx, c_ctx, ada_w, ada_b, norm_mix, norm_ffn, mix_w_in, mix_w_out, attn_q_norm, attn_k_norm,
           swa_sink, hy_conv_w, hy_conv_b, hy_w1, hy_b1, hy_w2, hy_b2, hy_w3, hy_b3, hy_w4, hy_freq, hy_bias_d,
           sc_conv_w, ffn_w_up, ffn_conv_w, ffn_conv_b, ffn_w_down):
    b, s, d = x.shape
    s_ctx = ctx.shape[1]
    depth = ada_w.shape[0]
    assert d == D_MODEL and b % 2 == 0 and b + 1 <= SUBLANES and s % 1024 == 0 and s_ctx % SUBLANES == 0

    cvec = jnp.concatenate([c, c_ctx[None, :], jnp.zeros((SUBLANES - b - 1, d), F32)], axis=0)
    mods = _mods(cvec, ada_w, ada_b)
    rope = [jnp.asarray(t) for t in _rope_tables(s)]
    rope_ctx = [jnp.asarray(t) for t in _identity_rope_tables(s_ctx)]
    bd = jnp.asarray(_head_block_diag())
    xc = ctx

    for i in range(depth):
        last = i == depth - 1
        j = i // 2
        lat = [mods[i, :b, k * d:(k + 1) * d][:, None, :] for k in range(6)]
        cx = [jnp.broadcast_to(mods[i, b, k * d:(k + 1) * d][None, None, :], (b, 1, d)) for k in range(6)]
        w_in = mix_w_in[i].astype(BF16)
        w_out = mix_w_out[i].astype(BF16)
        w_up = ffn_w_up[i].astype(BF16)
        w_down = ffn_w_down[i].astype(BF16)
        g_mix = norm_mix[i].reshape(1, d)
        g_ffn = norm_ffn[i].reshape(1, d)
        qg = jnp.tile(attn_q_norm[i], N_Q_HEADS).reshape(1, Q_W)
        kg = jnp.tile(attn_k_norm[i], N_KV_HEADS).reshape(1, KV_W)

        q, k, v, zc = _inproj(x, lat[0], lat[1], g_mix, w_in, qg, kg, bd, *rope)
        qc, kc, vc, zcc = _inproj(xc, cx[0], cx[1], g_mix, w_in, qg, kg, bd, *rope_ctx)

        if i % 2 == 0:
            fargs = (hy_w1[j], hy_b1[j], hy_w2[j], hy_b2[j], hy_w3[j], hy_b3[j], hy_w4[j], hy_freq[j])
            o_attn = _dense_attention(q, k, v, kc, vc)
            u, x0 = _hyena_pre(zc, hy_conv_w[j], hy_conv_b[j])
            kf, ks = _implicit_filter(s, *fargs)
            o_conv = _long_conv_mixer(u, x0, kf, ks, hy_bias_d[j])
            if not last:
                oc_attn = _dense_attention(qc, kc, vc)
                uc, x0c = _hyena_pre(zcc, hy_conv_w[j], hy_conv_b[j])
                kfc, ksc = _implicit_filter(s_ctx, *fargs)
                oc_conv = _small_conv_mixer(uc, x0c, kfc, ksc, hy_bias_d[j])
        else:
            sink = swa_sink[j].reshape(N_KV_HEADS, GROUP)
            o_attn = _banded_attention(q, k, v, kc, vc, sink)
            o_conv = _short_conv(zc, sc_conv_w[j])
            if not last:
                oc_attn = _dense_attention(qc, kc, vc, sink=sink)
                oc_conv = _short_conv(zcc, sc_conv_w[j])

        x = _outproj(x, o_attn, o_conv, w_out, lat[2])
        x = _ffn(x, lat[3], lat[4], lat[5], g_ffn, w_up, ffn_conv_w[i], ffn_conv_b[i], w_down)
        if not last:
            xc = _outproj(xc, oc_attn, oc_conv, w_out, cx[2])
            xc = _ffn(xc, cx[3], cx[4], cx[5], g_ffn, w_up, ffn_conv_w[i], ffn_conv_b[i], w_down)
    return x
```

```python
import functools
import math

import numpy as np
import jax
import jax.numpy as jnp
from jax import lax
from jax.experimental import pallas as pl
from jax.experimental.pallas import tpu as pltpu

F32 = jnp.float32
BF16 = jnp.bfloat16
HI = lax.Precision.HIGHEST

D_MODEL = 1024
GRID_W = 64
HEAD_DIM = 64
N_Q_HEADS = 8
N_KV_HEADS = 4
GROUP = N_Q_HEADS // N_KV_HEADS
Q_W = N_Q_HEADS * HEAD_DIM
KV_W = N_KV_HEADS * HEAD_DIM
QKV_W = Q_W + 2 * KV_W
CONV_W = D_MODEL // 2
MIX_IN_W = QKV_W + 3 * CONV_W
WINDOW = 128
ROPE_THETA = 10000.0
FILT_EMB = 33
FILT_WIDTH = 64
DECAY_TARGET = 1e-2
FAST_DECAY_PCT = 0.3
SLOW_DECAY_PCT = 1.5
DECAY_SHIFT = 0.05
D_FF = 2816
NEG_INF = -1e30
RMS_EPS = 1e-6
LOG2E = 1.4426950408889634

LANES = 128
SUBLANES = 8
DFT_N1 = 128
VMEM_LIMIT_MB = 56


def _params(sem, vmem_mb=VMEM_LIMIT_MB):
    return pltpu.CompilerParams(dimension_semantics=sem, vmem_limit_bytes=vmem_mb * 1024 * 1024)


def _full(shape):
    nd = len(shape)
    return pl.BlockSpec(shape, lambda *_: (0,) * nd)


def _mods_kernel(c_ref, w_ref, b_ref, o_ref):
    c = c_ref[...]
    s = c / (1.0 + jnp.exp(-c))
    o_ref[0] = jnp.dot(s, w_ref[0], preferred_element_type=F32, precision=HI) + b_ref[0]


def _mods(cvec, ada_w, ada_b):
    depth, d, n6 = ada_w.shape
    tn = 1536
    return pl.pallas_call(
        _mods_kernel,
        grid=(depth, n6 // tn),
        in_specs=[_full((SUBLANES, d)),
                  pl.BlockSpec((1, d, tn), lambda l, j: (l, 0, j)),
                  pl.BlockSpec((1, 1, tn), lambda l, j: (l, 0, j))],
        out_specs=pl.BlockSpec((1, SUBLANES, tn), lambda l, j: (l, 0, j)),
        out_shape=jax.ShapeDtypeStruct((depth, SUBLANES, n6), F32),
        compiler_params=_params(("arbitrary", "arbitrary")),
        name="ada_mods",
    )(cvec, ada_w, ada_b.reshape(depth, 1, n6))


def _modulated_norm(x, gain, shift, scale):
    ms = jnp.mean(x * x, axis=-1, keepdims=True)
    return (x * lax.rsqrt(ms + RMS_EPS)) * gain * (1.0 + scale) + shift


def _shifted_rows(xm, prev8, next8, i, nt):
    pz = jnp.where(i > 0, prev8, 0.0)
    nz = jnp.where(i < nt - 1, next8, 0.0)
    ext = jnp.concatenate([pz, xm, nz], axis=0)
    r = ext.shape[0]
    xp = pltpu.roll(ext, 1, 0)[SUBLANES:r - SUBLANES]
    xn = pltpu.roll(ext, r - 1, 0)[SUBLANES:r - SUBLANES]
    return xp, xn


def _halo_specs(tm, t, width):
    nb = tm // SUBLANES
    last = t // SUBLANES - 1
    return [pl.BlockSpec((1, tm, width), lambda b, i: (b, i, 0)),
            pl.BlockSpec((1, SUBLANES, width), lambda b, i: (b, jnp.maximum(i * nb - 1, 0), 0)),
            pl.BlockSpec((1, SUBLANES, width), lambda b, i: (b, jnp.minimum((i + 1) * nb, last), 0))]


def _inproj_kernel(x_ref, sh_ref, sc_ref, g_ref, w_ref, qg_ref, kg_ref, bd_ref, cos_ref, sa_ref, sb_ref,
                   q_ref, k_ref, vt_ref, zc_ref):
    h = _modulated_norm(x_ref[0], g_ref[...], sh_ref[0], sc_ref[0])
    z = jnp.dot(h.astype(BF16), w_ref[...], preferred_element_type=F32)
    zc_ref[0] = z[:, QKV_W:]
    cos, sa, sb = cos_ref[...], sa_ref[...], sb_ref[...]

    def head_norm_rope(t, gain, bd):
        ssq = jnp.dot((t * t).astype(BF16), bd, preferred_element_type=F32)
        t = t * lax.rsqrt(ssq * (1.0 / HEAD_DIM) + RMS_EPS) * gain
        outs = []
        for j in range(t.shape[1] // LANES):
            tj = t[:, j * LANES:(j + 1) * LANES]
            outs.append(tj * cos + pltpu.roll(tj, LANES - 16, 1) * sa + pltpu.roll(tj, 16, 1) * sb)
        return outs

    qs = head_norm_rope(z[:, :Q_W], qg_ref[...], bd_ref[...])
    ks = head_norm_rope(z[:, Q_W:Q_W + KV_W], kg_ref[...], bd_ref[:KV_W, :KV_W])
    qscale = HEAD_DIM ** -0.5 * LOG2E
    for j, qj in enumerate(qs):
        qj = (qj * qscale).astype(BF16)
        q_ref[0, 2 * j] = qj[:, :HEAD_DIM]
        q_ref[0, 2 * j + 1] = qj[:, HEAD_DIM:]
    for j, kj in enumerate(ks):
        kj = kj.astype(BF16)
        k_ref[0, 2 * j] = kj[:, :HEAD_DIM]
        k_ref[0, 2 * j + 1] = kj[:, HEAD_DIM:]
    tm = z.shape[0]
    vt = z[:, Q_W + KV_W:QKV_W].T
    ones_row = (lax.broadcasted_iota(jnp.int32, (LANES - HEAD_DIM, tm), 0) == 0).astype(F32)
    for hh in range(N_KV_HEADS):
        vt_ref[0, hh] = jnp.concatenate([vt[hh * HEAD_DIM:(hh + 1) * HEAD_DIM], ones_row], axis=0).astype(BF16)


def _inproj(x, sh, sc, gain, w_bf, qg, kg, bd, cos, sa, sb):
    bx, t, d = x.shape
    tm = min(512, t)
    vec = pl.BlockSpec((1, 1, d), lambda b, i: (b, 0, 0))
    tab = pl.BlockSpec((tm, LANES), lambda b, i: (i, 0))
    return pl.pallas_call(
        _inproj_kernel,
        grid=(bx, t // tm),
        in_specs=[pl.BlockSpec((1, tm, d), lambda b, i: (b, i, 0)), vec, vec, _full((1, d)),
                  _full((d, MIX_IN_W)), _full((1, Q_W)), _full((1, KV_W)), _full((Q_W, Q_W)), tab, tab, tab],
        out_specs=[pl.BlockSpec((1, N_Q_HEADS, tm, HEAD_DIM), lambda b, i: (b, 0, i, 0)),
                   pl.BlockSpec((1, N_KV_HEADS, tm, HEAD_DIM), lambda b, i: (b, 0, i, 0)),
                   pl.BlockSpec((1, N_KV_HEADS, LANES, tm), lambda b, i: (b, 0, 0, i)),
                   pl.BlockSpec((1, tm, 3 * CONV_W), lambda b, i: (b, i, 0))],
        out_shape=[jax.ShapeDtypeStruct((bx, N_Q_HEADS, t, HEAD_DIM), BF16),
                   jax.ShapeDtypeStruct((bx, N_KV_HEADS, t, HEAD_DIM), BF16),
                   jax.ShapeDtypeStruct((bx, N_KV_HEADS, LANES, t), BF16),
                   jax.ShapeDtypeStruct((bx, t, 3 * CONV_W), F32)],
        compiler_params=_params(("arbitrary", "arbitrary")),
        name="in_proj",
    )(x, sh, sc, gain, w_bf, qg, kg, bd, cos, sa, sb)


def _scores_t(ks, q):
    return lax.dot_general(ks, q, (((1,), (1,)), ((), ())), preferred_element_type=F32)


def _sink_column(sink_ref, hh, tq, m, acc):
    col = lax.broadcasted_iota(jnp.int32, m.shape, 1)
    sink = jnp.where(col < tq, sink_ref[hh, 0], sink_ref[hh, 1]) * LOG2E
    m_new = jnp.maximum(m, sink)
    den_row = lax.broadcasted_iota(jnp.int32, acc.shape, 0) == HEAD_DIM
    return acc * jnp.exp2(m - m_new) + jnp.where(den_row, jnp.exp2(sink - m_new), 0.0)


def _attn_finish(acc, tq):
    o = (acc / acc[HEAD_DIM:HEAD_DIM + 1]).T
    return jnp.concatenate([o[:tq, :HEAD_DIM], o[tq:, :HEAD_DIM]], axis=1)


EXP_CHUNK_ELEMS = 32 * SUBLANES * LANES


def _score_stage(ks, q, s_ref, bias=None):
    s = _scores_t(ks, q)
    if bias is not None:
        s = s + bias
    s_ref[0:s.shape[0]] = s
    return jnp.max(s, axis=0, keepdims=True)


def _softmax_stage(s_ref, p_ref, n, vt, mx, m, acc_ref):
    m_new = jnp.maximum(m, mx)
    rows = EXP_CHUNK_ELEMS // m.shape[1]
    for c in range(0, n, rows):
        p_ref[c:c + rows] = jnp.exp2(s_ref[c:c + rows] - m_new).astype(BF16)
    acc_ref[...] = acc_ref[...] * jnp.exp2(m - m_new) + jnp.dot(vt, p_ref[0:n], preferred_element_type=F32)
    return m_new


SAFE_SHIFT = 40.0


def _flash_exact(q, k_tile, vt_tile, ctx, sa_ref, sb_ref, p_ref, acc_ref, *, tk, n_main):
    pa_ref, pb_ref = p_ref.at[0:tk], p_ref.at[tk:2 * tk]
    acc_ref[...] = jnp.zeros_like(acc_ref)
    m = jnp.full((1, q.shape[0]), NEG_INF, F32)
    mx_a = _score_stage(k_tile(0), q, sa_ref)
    if n_main > 1:
        def body(i, carry):
            m, mx_a = carry
            mx_b = _score_stage(k_tile(2 * i + 1), q, sb_ref)
            m = _softmax_stage(sa_ref, pa_ref, tk, vt_tile(2 * i), mx_a, m, acc_ref)
            mx_a = _score_stage(k_tile(2 * i + 2), q, sa_ref)
            m = _softmax_stage(sb_ref, pb_ref, tk, vt_tile(2 * i + 1), mx_b, m, acc_ref)
            return m, mx_a

        m, mx_a = lax.fori_loop(0, n_main // 2 - 1, body, (m, mx_a))
        mx_b = _score_stage(k_tile(n_main - 1), q, sb_ref)
        m = _softmax_stage(sa_ref, pa_ref, tk, vt_tile(n_main - 2), mx_a, m, acc_ref)
        last = (sb_ref, pb_ref, mx_b)
        spare = (sa_ref, pa_ref)
    else:
        last = (sa_ref, pa_ref, mx_a)
        spare = (sb_ref, pb_ref)
    if ctx is not None:
        mx_c = _score_stage(ctx[0], q, spare[0])
    m = _softmax_stage(last[0], last[1], tk, vt_tile(n_main - 1), last[2], m, acc_ref)
    if ctx is not None:
        m = _softmax_stage(spare[0], spare[1], ctx[0].shape[0], ctx[1], mx_c, m, acc_ref)
    return m


def _flash_bounded(q, k_ref, vt_ref, ctx, bound, p_ref, acc_ref, *, tk, n_main):
    step = min(2, n_main) * tk

    def pair(j):
        off = pl.multiple_of(j * step, step)
        for c in range(0, step, tk):
            s_t = _scores_t(k_ref[0, 0, pl.ds(off + c, tk), :], q)
            p_ref[c:c + tk] = jnp.exp2(s_t - bound).astype(BF16)
        return jnp.dot(vt_ref[0, 0, :, pl.ds(off, step)], p_ref[0:step], preferred_element_type=F32)

    acc_ref[...] = pair(0)

    def body(j, carry):
        acc_ref[...] += pair(j)
        return carry

    lax.fori_loop(1, n_main * tk // step, body, 0)
    if ctx is not None:
        n_ctx = ctx[0].shape[0]
        p_ref[0:n_ctx] = jnp.exp2(_scores_t(ctx[0], q) - bound).astype(BF16)
        acc_ref[...] += jnp.dot(ctx[1], p_ref[0:n_ctx], preferred_element_type=F32)


def _sq_norm_max(rows):
    r = rows.astype(F32)
    return jnp.max(jnp.sum(r * r, axis=1, keepdims=True), axis=0, keepdims=True)


def _flash_kernel(*refs, tq, tk, n_main, has_ctx, has_sink):
    sa_ref, sb_ref, p_ref, acc_ref, kmax_ref = refs[-5:]
    o_ref = refs[-6]
    refs = list(refs[:-6])
    sink_ref = refs.pop(0) if has_sink else None
    q_ref, k_ref, vt_ref = refs[:3]
    m_cols = GROUP * tq
    q = q_ref[0].reshape(m_cols, HEAD_DIM)
    ctx = (refs[3][0, 0], refs[4][0, 0]) if has_ctx else None

    def k_tile(j):
        return k_ref[0, 0, pl.ds(pl.multiple_of(j * tk, tk), tk), :]

    def vt_tile(j):
        return vt_ref[0, 0, :, pl.ds(pl.multiple_of(j * tk, tk), tk)]

    exact = functools.partial(_flash_exact, q, k_tile, vt_tile, ctx, sa_ref, sb_ref, p_ref, acc_ref,
                              tk=tk, n_main=n_main)
    if has_sink:
        acc = _sink_column(sink_ref, pl.program_id(1), tq, exact(), acc_ref[...])
    else:
        @pl.when(pl.program_id(2) == 0)
        def _():
            def body(j, mx):
                return jnp.maximum(mx, _sq_norm_max(k_tile(j)))

            mx = lax.fori_loop(0, n_main, body, jnp.zeros((1, 1), F32))
            if has_ctx:
                mx = jnp.maximum(mx, _sq_norm_max(ctx[0]))
            kmax_ref[...] = jnp.broadcast_to(mx, kmax_ref.shape)

        qf = q.astype(F32)
        qn2 = lax.dot_general(jnp.ones((SUBLANES, HEAD_DIM), F32), qf * qf, (((1,), (1,)), ((), ())),
                              preferred_element_type=F32, precision=HI)[0:1]
        bound = jnp.sqrt(qn2 * kmax_ref[0:1, 0:1])
        safe = jnp.max(bound) <= SAFE_SHIFT

        @pl.when(safe)
        def _():
            _flash_bounded(q, k_ref, vt_ref, ctx, bound, p_ref, acc_ref, tk=tk, n_main=n_main)

        @pl.when(jnp.logical_not(safe))
        def _():
            exact()

        acc = acc_ref[...]
    o_ref[0] = _attn_finish(acc, tq).astype(o_ref.dtype)


def _kv_specs(n):
    return [pl.BlockSpec((1, 1, n, HEAD_DIM), lambda b, h, i: (b, h, 0, 0)),
            pl.BlockSpec((1, 1, LANES, n), lambda b, h, i: (b, h, 0, 0))]


def _dense_attention(q, k, vt, kc=None, vct=None, sink=None):
    bx, _, t, _ = q.shape
    s = k.shape[2]
    tq = min(512, t)
    tk = min(512, s)
    has_ctx = kc is not None
    has_sink = sink is not None
    in_specs = [pl.BlockSpec((1, GROUP, tq, HEAD_DIM), lambda b, h, i: (b, h, i, 0))] + _kv_specs(s)
    args = [q, k, vt]
    if has_ctx:
        in_specs += _kv_specs(kc.shape[2])
        args += [kc, vct]
    if has_sink:
        in_specs = [pl.BlockSpec(memory_space=pltpu.SMEM)] + in_specs
        args = [sink] + args
    n_main = s // tk
    assert n_main == 1 or n_main % 2 == 0
    n_ctx = kc.shape[2] if has_ctx else 0
    assert n_ctx <= tk
    m_cols = GROUP * tq
    return pl.pallas_call(
        functools.partial(_flash_kernel, tq=tq, tk=tk, n_main=n_main, has_ctx=has_ctx, has_sink=has_sink),
        grid=(bx, N_KV_HEADS, t // tq),
        in_specs=in_specs,
        out_specs=pl.BlockSpec((1, tq, LANES), lambda b, h, i: (b, i, h)),
        out_shape=jax.ShapeDtypeStruct((bx, t, Q_W), BF16),
        scratch_shapes=[pltpu.VMEM((tk, m_cols), F32), pltpu.VMEM((tk, m_cols), F32),
                        pltpu.VMEM((2 * tk, m_cols), BF16), pltpu.VMEM((LANES, m_cols), F32),
                        pltpu.VMEM((SUBLANES, LANES), F32)],
        compiler_params=_params(("arbitrary", "arbitrary", "arbitrary")),
        name="dense_attention",
    )(*args)


BAND_SUB = 256


@functools.lru_cache(maxsize=None)
def _band_bias():
    span = BAND_SUB + 2 * WINDOW
    kr = np.arange(span)[None, :, None]
    qc = (np.arange(GROUP * BAND_SUB) % BAND_SUB)[None, None, :]
    rel = np.arange(3)[:, None, None]
    return np.where(np.abs(kr - qc - rel * WINDOW) <= WINDOW, 0.0, NEG_INF).astype(np.float32)


def _banded_kernel(sink_ref, q_ref, k_ref, vt_ref, kc_ref, vct_ref, bias_ref, o_ref,
                   sl_ref, sc_ref, pl_ref, pc_ref, acc_ref, *, n_sub, s_len):
    i = pl.program_id(2)
    sub = BAND_SUB
    m_cols = GROUP * sub
    span = sub + 2 * WINDOW
    n_ctx = kc_ref.shape[2]
    stats = []
    for u in range(n_sub):
        q = q_ref[0, :, u * sub:(u + 1) * sub, :].reshape(m_cols, HEAD_DIM)
        q0 = (i * n_sub + u) * sub
        start = pl.multiple_of(jnp.clip(q0 - WINDOW, 0, s_len - span), WINDOW)
        mx_l = _score_stage(k_ref[0, 0, pl.ds(start, span), :], q, sl_ref.at[u], bias=bias_ref[(q0 - start) // WINDOW])
        mx_c = _score_stage(kc_ref[0, 0], q, sc_ref.at[u])
        stats.append((start, mx_l, mx_c))
    for u in range(n_sub):
        start, mx_l, mx_c = stats[u]
        acc_u = acc_ref.at[u]
        acc_u[...] = jnp.zeros((LANES, m_cols), F32)
        m = jnp.full((1, m_cols), NEG_INF, F32)
        m = _softmax_stage(sl_ref.at[u], pl_ref.at[u], span, vt_ref[0, 0, :, pl.ds(start, span)], mx_l, m, acc_u)
        m = _softmax_stage(sc_ref.at[u], pc_ref.at[u], n_ctx, vct_ref[0, 0], mx_c, m, acc_u)
        acc = _sink_column(sink_ref, pl.program_id(1), sub, m, acc_u[...])
        o_ref[0, u * sub:(u + 1) * sub] = _attn_finish(acc, sub).astype(o_ref.dtype)


def _banded_attention(q, k, vt, kc, vct, sink):
    bx, _, t, _ = q.shape
    n_ctx = kc.shape[2]
    n_sub = 2
    tq = n_sub * BAND_SUB
    m_cols = GROUP * BAND_SUB
    span = BAND_SUB + 2 * WINDOW
    bias = jnp.asarray(_band_bias())
    return pl.pallas_call(
        functools.partial(_banded_kernel, n_sub=n_sub, s_len=t),
        grid=(bx, N_KV_HEADS, t // tq),
        in_specs=[pl.BlockSpec(memory_space=pltpu.SMEM),
                  pl.BlockSpec((1, GROUP, tq, HEAD_DIM), lambda b, h, i: (b, h, i, 0))]
                 + _kv_specs(t) + _kv_specs(n_ctx) + [_full(bias.shape)],
        out_specs=pl.BlockSpec((1, tq, LANES), lambda b, h, i: (b, i, h)),
        out_shape=jax.ShapeDtypeStruct((bx, t, Q_W), BF16),
        scratch_shapes=[pltpu.VMEM((n_sub, span, m_cols), F32), pltpu.VMEM((n_sub, n_ctx, m_cols), F32),
                        pltpu.VMEM((n_sub, span, m_cols), BF16), pltpu.VMEM((n_sub, n_ctx, m_cols), BF16),
                        pltpu.VMEM((n_sub, LANES, m_cols), F32)],
        compiler_params=_params(("arbitrary", "arbitrary", "arbitrary")),
        name="banded_attention",
    )(sink, q, k, vt, kc, vct, bias)


def _hyena_pre_kernel(z_ref, zp_ref, zn_ref, w_ref, b_ref, u_ref, x0_ref):
    i = pl.program_id(1)
    zm = z_ref[0]
    zp, zn = _shifted_rows(zm, zp_ref[0], zn_ref[0], i, pl.num_programs(1))
    w = w_ref[...]
    c = zp * w[0:1] + zm * w[1:2] + zn * w[2:3] + b_ref[...]
    x0_ref[0] = c[:, :CONV_W]
    u_ref[0] = c[:, 2 * CONV_W:] * c[:, CONV_W:2 * CONV_W]


def _hyena_pre(zc, conv_w, conv_b):
    bx, t, w3 = zc.shape
    tm = min(512, t)
    out = jax.ShapeDtypeStruct((bx, t, CONV_W), F32)
    ospec = pl.BlockSpec((1, tm, CONV_W), lambda b, i: (b, i, 0))
    return pl.pallas_call(
        _hyena_pre_kernel,
        grid=(bx, t // tm),
        in_specs=_halo_specs(tm, t, w3) + [_full((3, w3)), _full((1, w3))],
        out_specs=[ospec, ospec],
        out_shape=[out, out],
        compiler_params=_params(("arbitrary", "arbitrary")),
        name="hyena_pre",
    )(zc, zc, zc, conv_w, conv_b.reshape(1, w3))


def _short_conv_kernel(z_ref, zp_ref, zn_ref, w_ref, o_ref):
    i = pl.program_id(1)
    prod = lambda z: z[:, CONV_W:2 * CONV_W] * z[:, 2 * CONV_W:]
    zm = z_ref[0]
    pm = prod(zm)
    pp, pn = _shifted_rows(pm, prod(zp_ref[0]), prod(zn_ref[0]), i, pl.num_programs(1))
    w = w_ref[...]
    o_ref[0] = (zm[:, :CONV_W] * (pp * w[0:1] + pm * w[1:2] + pn * w[2:3])).astype(o_ref.dtype)


def _short_conv(zc, conv_w):
    bx, t, w3 = zc.shape
    tm = min(512, t)
    return pl.pallas_call(
        _short_conv_kernel,
        grid=(bx, t // tm),
        in_specs=_halo_specs(tm, t, w3) + [_full((3, CONV_W))],
        out_specs=pl.BlockSpec((1, tm, CONV_W), lambda b, i: (b, i, 0)),
        out_shape=jax.ShapeDtypeStruct((bx, t, CONV_W), BF16),
        compiler_params=_params(("arbitrary", "arbitrary")),
        name="short_conv",
    )(zc, zc, zc, conv_w)


@functools.lru_cache(maxsize=None)
def _filter_features(n):
    j = np.arange(2 * n)
    d = np.where(j <= n, j, 2 * n - j)
    d = np.where(j == n, 0, d)
    bands = (FILT_EMB - 1) // 2
    t01 = np.linspace(0.0, 1.0, n)[d]
    w = 2.0 * np.pi * d.astype(np.float64) / n
    f = np.linspace(1e-4, bands - 1, bands)[None, :]
    feats = np.zeros((2 * n, LANES), np.float64)
    feats[:, 0] = t01
    feats[:, 1:1 + bands] = np.cos(f * w[:, None])
    feats[:, 1 + bands:FILT_EMB] = -np.sin(f * w[:, None])
    feats[:, 64] = t01
    feats[:, 65] = (j < n)
    feats[:, 66] = (j != n)
    return feats.astype(np.float32)


def _filter_kernel(f_ref, w1_ref, b1_ref, w2_ref, b2_ref, w3_ref, b3_ref, w4_ref, fr_ref, dl_ref, k_ref, s_ref):
    f = f_ref[...]
    fr = fr_ref[...]
    mm = lambda a, b: jnp.dot(a, b, preferred_element_type=F32, precision=HI)
    h = jnp.sin(fr * (mm(f, w1_ref[...]) + b1_ref[...]))
    h = jnp.sin(fr * (mm(h, w2_ref[...]) + b2_ref[...]))
    h = jnp.sin(fr * (mm(h, w3_ref[...]) + b3_ref[...]))
    hf = mm(h, w4_ref[...])
    win = jnp.exp(-f[:, 64:65] * dl_ref[...]) + DECAY_SHIFT
    k = jnp.where(f[:, 65:66] > 0.5, hf[:, :CONV_W], hf[:, CONV_W:]) * win * f[:, 66:67]
    k_ref[...] = k

    @pl.when(pl.program_id(0) == 0)
    def _():
        s_ref[...] = jnp.zeros_like(s_ref)

    s_ref[...] += jnp.sum(jnp.abs(k), axis=0, keepdims=True)


def _implicit_filter(n, w1, b1, w2, b2, w3, b3, w4, freq):
    feats = jnp.asarray(_filter_features(n))
    pad = LANES - FILT_WIDTH
    padc = lambda a: jnp.pad(a.reshape(1, -1), ((0, 0), (0, pad)))
    w1p = jnp.pad(w1, ((0, LANES - FILT_EMB), (0, pad)))
    w2p = jnp.pad(w2, ((0, pad), (0, pad)))
    w3p = jnp.pad(w3, ((0, pad), (0, pad)))
    w4p = jnp.pad(w4, ((0, pad), (0, 0)))
    deltas = np.abs(np.linspace(math.log(DECAY_TARGET) / SLOW_DECAY_PCT, math.log(DECAY_TARGET) / FAST_DECAY_PCT,
                                CONV_W)).astype(np.float32).reshape(1, CONV_W)
    tr = min(1024, 2 * n)
    sq = _full((LANES, LANES))
    vec = _full((1, LANES))
    return pl.pallas_call(
        _filter_kernel,
        grid=(2 * n // tr,),
        in_specs=[pl.BlockSpec((tr, LANES), lambda i: (i, 0)), sq, vec, sq, vec, sq, vec,
                  _full((LANES, 2 * CONV_W)), vec, _full((1, CONV_W))],
        out_specs=[pl.BlockSpec((tr, CONV_W), lambda i: (i, 0)), _full((1, CONV_W))],
        out_shape=[jax.ShapeDtypeStruct((2 * n, CONV_W), F32), jax.ShapeDtypeStruct((1, CONV_W), F32)],
        compiler_params=_params(("arbitrary",)),
        name="hyena_filter",
    )(feats, w1p, padc(b1), w2p, padc(b2), w3p, padc(b3), w4p, padc(freq), jnp.asarray(deltas))


def _twiddle(idx, mod):
    ang = 2.0 * np.pi * (idx % mod) / mod
    return np.cos(ang), -np.sin(ang)


def _real_form(mr, mi):
    return np.concatenate([np.concatenate([mr, -mi], -1), np.concatenate([mi, mr], -1)], -2)


@functools.lru_cache(maxsize=None)
def _dft_tables(n2):
    n1 = DFT_N1
    n = n1 * n2
    h = n2 // 2
    a2 = np.arange(n2)
    fr, fi = _twiddle(np.outer(a2, a2), n2)
    m_data = _real_form(fr[:, :h], fi[:, :h])
    m_filt = np.concatenate([fr, fi], 0)
    k2 = a2[:, None, None]
    k1 = np.arange(n1)[None, :, None]
    c1 = np.arange(n1)[None, None, :]
    g = _real_form(*_twiddle(c1 * (n2 * k1 + k2), n))
    a1 = np.arange(n1)
    f1 = _real_form(*_twiddle(np.outer(a1, a1), n1))
    t2 = a1[:, None, None]
    t1 = np.arange(h)[None, :, None]
    j1 = a2[None, None, :]
    hh = _real_form(*_twiddle(j1 * (n1 * t1 + t2), n))
    cast = lambda m: np.asarray(m, dtype=BF16)
    return cast(m_data), cast(m_filt), cast(g), cast(f1), cast(hh)


DFT_ROW_CHUNK = SUBLANES


def _dft_rows_kernel(m_ref, *refs):
    o_ref = refs[-1]
    cols = [jnp.concatenate([r[0, :, t, :] for r in refs[:-1]], axis=0) for t in range(DFT_ROW_CHUNK)]
    rhs = jnp.concatenate(cols, axis=1).astype(BF16)
    out = jnp.dot(m_ref[...], rhs, preferred_element_type=F32)
    half = out.shape[0] // 2
    o_ref[0, 0] = out[:half].astype(o_ref.dtype)
    o_ref[0, 1] = out[half:].astype(o_ref.dtype)


def _dft_stage1(mat, views, pairs, rows, n2, c):
    nin = len(views)
    lc = DFT_ROW_CHUNK * c
    in_specs = [_full(mat.shape)] + [
        pl.BlockSpec((1, rows, DFT_ROW_CHUNK, c), (lambda p, j, a=a: (nin * p + a, 0, j, 0))) for a in range(nin)]
    return pl.pallas_call(
        _dft_rows_kernel,
        grid=(pairs, DFT_N1 // DFT_ROW_CHUNK),
        in_specs=in_specs,
        out_specs=pl.BlockSpec((1, 2, n2, lc), lambda p, j: (p, 0, 0, j)),
        out_shape=jax.ShapeDtypeStruct((pairs, 2, n2, DFT_N1 * c), BF16),
        compiler_params=_params(("arbitrary", "arbitrary")),
        name="dft_stage1",
    )(mat, *views)


def _spectrum_kernel(a_ref, g_ref, sc_ref, o_ref, *, kb):
    for j in range(kb):
        rhs = jnp.concatenate([a_ref[0, 0, j], a_ref[0, 1, j]], axis=0)
        x = jnp.dot(g_ref[j], rhs, preferred_element_type=F32)
        o_ref[0, j] = x[:DFT_N1] * sc_ref[...]
        o_ref[1, j] = x[DFT_N1:] * sc_ref[...]


def _filter_spectrum(a, g, scale, n2, kb):
    c = a.shape[-1]
    return pl.pallas_call(
        functools.partial(_spectrum_kernel, kb=kb),
        grid=(n2 // kb,),
        in_specs=[pl.BlockSpec((1, 2, kb, DFT_N1, c), lambda k: (0, 0, k, 0, 0)),
                  pl.BlockSpec((kb, 2 * DFT_N1, 2 * DFT_N1), lambda k: (k, 0, 0)), _full((1, c))],
        out_specs=pl.BlockSpec((2, kb, DFT_N1, c), lambda k: (0, k, 0, 0)),
        out_shape=jax.ShapeDtypeStruct((2, n2, DFT_N1, c), F32),
        compiler_params=_params(("arbitrary",)),
        name="filter_spectrum",
    )(a, g, scale)


def _dft_mid_kernel(a_ref, g_ref, kh_ref, f_ref, o_ref, *, kb):
    for j in range(kb):
        rhs = jnp.concatenate([a_ref[0, 0, j], a_ref[0, 1, j]], axis=0)
        x = jnp.dot(g_ref[j], rhs, preferred_element_type=F32)
        xr, xi = x[:DFT_N1], x[DFT_N1:]
        kr, ki = kh_ref[0, j], kh_ref[1, j]
        yr = xr * kr - xi * ki
        yi = xr * ki + xi * kr
        v = jnp.concatenate([yr, -yi], axis=0).astype(BF16)
        b = jnp.dot(f_ref[...], v, preferred_element_type=F32)
        o_ref[0, 0, j] = b[:DFT_N1].astype(o_ref.dtype)
        o_ref[0, 1, j] = b[DFT_N1:].astype(o_ref.dtype)


def _dft_mid(a, g, khat, f1, pairs, n2, kb):
    c = a.shape[-1]
    blk = pl.BlockSpec((1, 2, kb, DFT_N1, c), lambda k, p: (p, 0, k, 0, 0))
    return pl.pallas_call(
        functools.partial(_dft_mid_kernel, kb=kb),
        grid=(n2 // kb, pairs),
        in_specs=[blk, pl.BlockSpec((kb, 2 * DFT_N1, 2 * DFT_N1), lambda k, p: (k, 0, 0)),
                  pl.BlockSpec((2, kb, DFT_N1, c), lambda k, p: (0, k, 0, 0)), _full(f1.shape)],
        out_specs=blk,
        out_shape=jax.ShapeDtypeStruct(a.shape, BF16),
        compiler_params=_params(("arbitrary", "arbitrary")),
        name="dft_mid",
    )(a, g, khat, f1)


def _dft_last_kernel(b_ref, h_ref, u_ref, x0_ref, bd_ref, o_ref, *, c):
    half = h_ref.shape[1] // 2
    bd = bd_ref[...]
    for t in range(DFT_ROW_CHUNK):
        cols = slice(t * c, (t + 1) * c)
        rhs = jnp.concatenate([b_ref[0, 0, :, cols], b_ref[0, 1, :, cols]], axis=0)
        v = jnp.dot(h_ref[t], rhs, preferred_element_type=F32)
        o_ref[0, :, t, :] = (v[:half] + u_ref[0, :, t, :] * bd) * x0_ref[0, :, t, :]
        o_ref[1, :, t, :] = (-v[half:] + u_ref[1, :, t, :] * bd) * x0_ref[1, :, t, :]


def _dft_last(bm, hh, u_view, x0_view, bias_d, pairs, n2, c):
    half = n2 // 2
    tc = DFT_ROW_CHUNK
    io = pl.BlockSpec((2, half, tc, c), lambda p, j: (p, 0, j, 0))
    return pl.pallas_call(
        functools.partial(_dft_last_kernel, c=c),
        grid=(pairs, DFT_N1 // tc),
        in_specs=[pl.BlockSpec((1, 2, n2, tc * c), lambda p, j: (p, 0, 0, j)),
                  pl.BlockSpec((tc, n2, 2 * n2), lambda p, j: (j, 0, 0)), io, io, _full((1, c))],
        out_specs=io,
        out_shape=jax.ShapeDtypeStruct((2 * pairs, half, DFT_N1, c), F32),
        compiler_params=_params(("arbitrary", "arbitrary")),
        name="dft_last",
    )(bm, hh, u_view, x0_view, bias_d)


def _long_conv_mixer(u, x0, kfilt, ksum, bias_d):
    b, n, c = u.shape
    n2 = 2 * n // DFT_N1
    half = n2 // 2
    pairs = b // 2
    m_data, m_filt, g, f1, hh = (jnp.asarray(t) for t in _dft_tables(n2))
    kb = min(8, n2)
    scale = 1.0 / (ksum * float(DFT_N1 * n2))
    ka = _dft_stage1(m_filt, [kfilt.reshape(1, n2, DFT_N1, c)], 1, n2, n2, c)
    khat = _filter_spectrum(ka.reshape(1, 2, n2, DFT_N1, c), g, scale, n2, kb)
    u_view = u.reshape(b, half, DFT_N1, c)
    a = _dft_stage1(m_data, [u_view, u_view], pairs, half, n2, c)
    bm = _dft_mid(a.reshape(pairs, 2, n2, DFT_N1, c), g, khat, f1, pairs, n2, kb)
    out = _dft_last(bm.reshape(pairs, 2, n2, DFT_N1 * c), hh, u_view, x0.reshape(b, half, DFT_N1, c),
                    bias_d.reshape(1, c), pairs, n2, c)
    return out.reshape(b, n, c)


@functools.lru_cache(maxsize=None)
def _small_dft_tables(n):
    big = 2 * n
    a = np.arange(big)
    fr, fi = _twiddle(np.outer(a, a), big)
    cast = lambda m: np.asarray(m, dtype=BF16)
    return (cast(np.concatenate([fr, fi], 0)),
            cast(_real_form(fr[:, :n], fi[:, :n])),
            cast(_real_form(fr[:n], fi[:n])))


def _small_conv_kernel(k_ref, ks_ref, u_ref, x0_ref, bd_ref, mf_ref, md_ref, mi_ref, o_ref, *, n):
    big = 2 * n
    kh = jnp.dot(mf_ref[...], k_ref[...].astype(BF16), preferred_element_type=F32) * (1.0 / (ks_ref[...] * big))
    kr, ki = kh[:big], kh[big:]
    rhs = jnp.concatenate([u_ref[0], u_ref[1]], axis=0).astype(BF16)
    x = jnp.dot(md_ref[...], rhs, preferred_element_type=F32)
    xr, xi = x[:big], x[big:]
    v = jnp.concatenate([xr * kr - xi * ki, -(xr * ki + xi * kr)], axis=0).astype(BF16)
    y = jnp.dot(mi_ref[...], v, preferred_element_type=F32)
    bd = bd_ref[...]
    o_ref[0] = ((y[:n] + u_ref[0] * bd) * x0_ref[0]).astype(o_ref.dtype)
    o_ref[1] = ((-y[n:] + u_ref[1] * bd) * x0_ref[1]).astype(o_ref.dtype)


def _small_conv_mixer(u, x0, kfilt, ksum, bias_d):
    b, n, c = u.shape
    mf, md, mi = (jnp.asarray(t) for t in _small_dft_tables(n))
    io = pl.BlockSpec((2, n, c), lambda p: (p, 0, 0))
    return pl.pallas_call(
        functools.partial(_small_conv_kernel, n=n),
        grid=(b // 2,),
        in_specs=[_full((2 * n, c)), _full((1, c)), io, io, _full((1, c)),
                  _full(mf.shape), _full(md.shape), _full(mi.shape)],
        out_specs=io,
        out_shape=jax.ShapeDtypeStruct((b, n, c), BF16),
        compiler_params=_params(("arbitrary",)),
        name="small_conv",
    )(kfilt, ksum, u, x0, bias_d.reshape(1, c), mf, md, mi)


def _outproj_kernel(x_ref, oa_ref, oc_ref, wa_ref, wc_ref, g_ref, o_ref):
    y = (jnp.dot(oa_ref[0], wa_ref[...], preferred_element_type=F32)
         + jnp.dot(oc_ref[0].astype(BF16), wc_ref[...], preferred_element_type=F32))
    o_ref[0] = x_ref[0] + g_ref[0] * y


def _outproj(x, oa, oc, w_bf, gate):
    bx, t, d = x.shape
    tm = min(512, t)
    row = lambda w: pl.BlockSpec((1, tm, w), lambda b, i: (b, i, 0))
    return pl.pallas_call(
        _outproj_kernel,
        grid=(bx, t // tm),
        in_specs=[row(d), row(Q_W), row(CONV_W), pl.BlockSpec((Q_W, d), lambda b, i: (0, 0)),
                  pl.BlockSpec((CONV_W, d), lambda b, i: (1, 0)), pl.BlockSpec((1, 1, d), lambda b, i: (b, 0, 0))],
        out_specs=row(d),
        out_shape=jax.ShapeDtypeStruct((bx, t, d), F32),
        compiler_params=_params(("arbitrary", "arbitrary")),
        name="out_proj",
    )(x, oa, oc, w_bf, w_bf, gate)


FFN_CHUNK = D_FF // 2


def _gelu_tanh(x):
    return 0.5 * x * (1.0 + jnp.tanh(math.sqrt(2.0 / math.pi) * (x + 0.044715 * (x * x * x))))


def _ffn_kernel(x_ref, xp_ref, xn_ref, sh_ref, sc_ref, gt_ref, g_ref, wu_ref, cw_ref, cb_ref, wd_ref, o_ref):
    i = pl.program_id(1)
    nt = pl.num_programs(1)
    xm = x_ref[0]
    tm = xm.shape[0]
    ext = jnp.concatenate([xp_ref[0], xm, xn_ref[0]], axis=0)
    r = tm + 2 * SUBLANES
    h = _modulated_norm(ext, g_ref[...], sh_ref[0], sc_ref[0])
    row = lax.broadcasted_iota(jnp.int32, (r, 1), 0)
    inside = jnp.logical_and(jnp.logical_or(i > 0, row >= SUBLANES), jnp.logical_or(i < nt - 1, row < tm + SUBLANES))
    h = jnp.where(inside, h, 0.0).astype(BF16)
    hm = h[SUBLANES:tm + SUBLANES]
    acc = jnp.zeros((tm, D_MODEL), F32)
    for c0 in range(0, D_FF, FFN_CHUNK):
        a = jnp.dot(h, wu_ref[:, c0:c0 + FFN_CHUNK], preferred_element_type=F32)
        v = jnp.dot(hm, wu_ref[:, D_FF + c0:D_FF + c0 + FFN_CHUNK], preferred_element_type=F32)
        ap = pltpu.roll(a, 1, 0)[SUBLANES:tm + SUBLANES]
        an = pltpu.roll(a, r - 1, 0)[SUBLANES:tm + SUBLANES]
        cw = cw_ref[:, c0:c0 + FFN_CHUNK]
        conv = ap * cw[0:1] + a[SUBLANES:tm + SUBLANES] * cw[1:2] + an * cw[2:3] + cb_ref[:, c0:c0 + FFN_CHUNK]
        act = (_gelu_tanh(conv) * v).astype(BF16)
        acc = acc + jnp.dot(act, wd_ref[c0:c0 + FFN_CHUNK, :], preferred_element_type=F32)
    o_ref[0] = xm + gt_ref[0] * acc


def _ffn(x, sh, sc, gate, gain, wu_bf, conv_w, conv_b, wd_bf):
    bx, t, d = x.shape
    tm = min(512, t)
    vec = pl.BlockSpec((1, 1, d), lambda b, i: (b, 0, 0))
    once = pl.Buffered(1)
    return pl.pallas_call(
        _ffn_kernel,
        grid=(bx, t // tm),
        in_specs=_halo_specs(tm, t, d) + [
            vec, vec, vec, _full((1, d)),
            pl.BlockSpec((d, 2 * D_FF), lambda b, i: (0, 0), pipeline_mode=once),
            _full((3, D_FF)), _full((1, D_FF)),
            pl.BlockSpec((D_FF, d), lambda b, i: (0, 0), pipeline_mode=once)],
        out_specs=pl.BlockSpec((1, tm, d), lambda b, i: (b, i, 0)),
        out_shape=jax.ShapeDtypeStruct((bx, t, d), F32),
        compiler_params=_params(("arbitrary", "arbitrary")),
        name="conv_ffn",
    )(x, x, x, sh, sc, gate, gain, wu_bf, conv_w, conv_b.reshape(1, D_FF), wd_bf)


@functools.lru_cache(maxsize=None)
def _rope_tables(t):
    pos = np.arange(t)
    n_freq = HEAD_DIM // 4
    inv = ROPE_THETA ** (-np.arange(n_freq, dtype=np.float64) / n_freq)
    ang_r = (pos // GRID_W)[:, None] * inv
    ang_c = (pos % GRID_W)[:, None] * inv
    zero = np.zeros_like(ang_r)
    cos = np.concatenate([np.cos(ang_r)] * 2 + [np.cos(ang_c)] * 2, 1)
    sa = np.concatenate([-np.sin(ang_r), zero, -np.sin(ang_c), zero], 1)
    sb = np.concatenate([zero, np.sin(ang_r), zero, np.sin(ang_c)], 1)
    tile = lambda m: np.tile(m, (1, LANES // HEAD_DIM)).astype(np.float32)
    return tile(cos), tile(sa), tile(sb)


@functools.lru_cache(maxsize=None)
def _identity_rope_tables(t):
    return np.ones((t, LANES), np.float32), np.zeros((t, LANES), np.float32), np.zeros((t, LANES), np.float32)


@functools.lru_cache(maxsize=None)
def _head_block_diag():
    hid = np.arange(Q_W) // HEAD_DIM
    return np.asarray(hid[:, None] == hid[None, :], dtype=BF16)


def kernel(x, c, ctx, c_ctx, ada_w, ada_b, norm_mix, norm_ffn, mix_w_in, mix_w_out, attn_q_norm, attn_k_norm,
           swa_sink, hy_conv_w, hy_conv_b, hy_w1, hy_b1, hy_w2, hy_b2, hy_w3, hy_b3, hy_w4, hy_freq, hy_bias_d,
           sc_conv_w, ffn_w_up, ffn_conv_w, ffn_conv_b, ffn_w_down):
    b, s, d = x.shape
    s_ctx = ctx.shape[1]
    depth = ada_w.shape[0]
    assert d == D_MODEL and b % 2 == 0 and b + 1 <= SUBLANES and s % 1024 == 0 and s_ctx % SUBLANES == 0

    cvec = jnp.concatenate([c, c_ctx[None, :], jnp.zeros((SUBLANES - b - 1, d), F32)], axis=0)
    mods = _mods(cvec, ada_w, ada_b)
    rope = [jnp.asarray(t) for t in _rope_tables(s)]
    rope_ctx = [jnp.asarray(t) for t in _identity_rope_tables(s_ctx)]
    bd = jnp.asarray(_head_block_diag())
    xc = ctx

    for i in range(depth):
        last = i == depth - 1
        j = i // 2
        lat = [mods[i, :b, k * d:(k + 1) * d][:, None, :] for k in range(6)]
        cx = [jnp.broadcast_to(mods[i, b, k * d:(k + 1) * d][None, None, :], (b, 1, d)) for k in range(6)]
        w_in = mix_w_in[i].astype(BF16)
        w_out = mix_w_out[i].astype(BF16)
        w_up = ffn_w_up[i].astype(BF16)
        w_down = ffn_w_down[i].astype(BF16)
        g_mix = norm_mix[i].reshape(1, d)
        g_ffn = norm_ffn[i].reshape(1, d)
        qg = jnp.tile(attn_q_norm[i], N_Q_HEADS).reshape(1, Q_W)
        kg = jnp.tile(attn_k_norm[i], N_KV_HEADS).reshape(1, KV_W)

        q, k, v, zc = _inproj(x, lat[0], lat[1], g_mix, w_in, qg, kg, bd, *rope)
        qc, kc, vc, zcc = _inproj(xc, cx[0], cx[1], g_mix, w_in, qg, kg, bd, *rope_ctx)

        if i % 2 == 0:
            fargs = (hy_w1[j], hy_b1[j], hy_w2[j], hy_b2[j], hy_w3[j], hy_b3[j], hy_w4[j], hy_freq[j])
            o_attn = _dense_attention(q, k, v, kc, vc)
            u, x0 = _hyena_pre(zc, hy_conv_w[j], hy_conv_b[j])
            kf, ks = _implicit_filter(s, *fargs)
            o_conv = _long_conv_mixer(u, x0, kf, ks, hy_bias_d[j])
            if not last:
                oc_attn = _dense_attention(qc, kc, vc)
                uc, x0c = _hyena_pre(zcc, hy_conv_w[j], hy_conv_b[j])
                kfc, ksc = _implicit_filter(s_ctx, *fargs)
                oc_conv = _small_conv_mixer(uc, x0c, kfc, ksc, hy_bias_d[j])
        else:
            sink = swa_sink[j].reshape(N_KV_HEADS, GROUP)
            o_attn = _banded_attention(q, k, v, kc, vc, sink)
            o_conv = _short_conv(zc, sc_conv_w[j])
            if not last:
                oc_attn = _dense_attention(qc, kc, vc, sink=sink)
                oc_conv = _short_conv(zcc, sc_conv_w[j])

        x = _outproj(x, o_attn, o_conv, w_out, lat[2])
        x = _ffn(x, lat[3], lat[4], lat[5], g_ffn, w_up, ffn_conv_w[i], ffn_conv_b[i], w_down)
        if not last:
            xc = _outproj(xc, oc_attn, oc_conv, w_out, cx[2])
            xc = _ffn(xc, cx[3], cx[4], cx[5], g_ffn, w_up, ffn_conv_w[i], ffn_conv_b[i], w_down)
    return x
```

```python
import functools
import math

import numpy as np
import jax
import jax.numpy as jnp
from jax import lax
from jax.experimental import pallas as pl
from jax.experimental.pallas import tpu as pltpu

F32 = jnp.float32
BF16 = jnp.bfloat16
HI = lax.Precision.HIGHEST

D_MODEL = 1024
GRID_W = 64
HEAD_DIM = 64
N_Q_HEADS = 8
N_KV_HEADS = 4
GROUP = N_Q_HEADS // N_KV_HEADS
Q_W = N_Q_HEADS * HEAD_DIM
KV_W = N_KV_HEADS * HEAD_DIM
QKV_W = Q_W + 2 * KV_W
CONV_W = D_MODEL // 2
MIX_IN_W = QKV_W + 3 * CONV_W
WINDOW = 128
ROPE_THETA = 10000.0
FILT_EMB = 33
FILT_WIDTH = 64
DECAY_TARGET = 1e-2
FAST_DECAY_PCT = 0.3
SLOW_DECAY_PCT = 1.5
DECAY_SHIFT = 0.05
D_FF = 2816
NEG_INF = -1e30
RMS_EPS = 1e-6
LOG2E = 1.4426950408889634

LANES = 128
SUBLANES = 8
DFT_N1 = 128
VMEM_LIMIT_MB = 56
PROJ_SUB_ROWS = 128


def _params(sem, vmem_mb=VMEM_LIMIT_MB):
    return pltpu.CompilerParams(dimension_semantics=sem, vmem_limit_bytes=vmem_mb * 1024 * 1024)


def _full(shape):
    nd = len(shape)
    return pl.BlockSpec(shape, lambda *_: (0,) * nd)


def _mods_kernel(c_ref, w_ref, b_ref, o_ref):
    c = c_ref[...]
    s = c / (1.0 + jnp.exp(-c))
    o_ref[0] = jnp.dot(s, w_ref[0], preferred_element_type=F32, precision=HI) + b_ref[0]


def _mods(cvec, ada_w, ada_b):
    depth, d, n6 = ada_w.shape
    tn = 1536
    return pl.pallas_call(
        _mods_kernel,
        grid=(depth, n6 // tn),
        in_specs=[_full((SUBLANES, d)),
                  pl.BlockSpec((1, d, tn), lambda l, j: (l, 0, j)),
                  pl.BlockSpec((1, 1, tn), lambda l, j: (l, 0, j))],
        out_specs=pl.BlockSpec((1, SUBLANES, tn), lambda l, j: (l, 0, j)),
        out_shape=jax.ShapeDtypeStruct((depth, SUBLANES, n6), F32),
        compiler_params=_params(("arbitrary", "arbitrary")),
        name="ada_mods",
    )(cvec, ada_w, ada_b.reshape(depth, 1, n6))


def _modulated_norm(x, gain, shift, scale):
    ms = jnp.mean(x * x, axis=-1, keepdims=True)
    return (x * lax.rsqrt(ms + RMS_EPS)) * gain * (1.0 + scale) + shift


def _shifted_rows(xm, prev8, next8, i, nt):
    pz = jnp.where(i > 0, prev8, 0.0)
    nz = jnp.where(i < nt - 1, next8, 0.0)
    ext = jnp.concatenate([pz, xm, nz], axis=0)
    r = ext.shape[0]
    xp = pltpu.roll(ext, 1, 0)[SUBLANES:r - SUBLANES]
    xn = pltpu.roll(ext, r - 1, 0)[SUBLANES:r - SUBLANES]
    return xp, xn


def _halo_specs(tm, t, width):
    nb = tm // SUBLANES
    last = t // SUBLANES - 1
    return [pl.BlockSpec((1, tm, width), lambda b, i: (b, i, 0)),
            pl.BlockSpec((1, SUBLANES, width), lambda b, i: (b, jnp.maximum(i * nb - 1, 0), 0)),
            pl.BlockSpec((1, SUBLANES, width), lambda b, i: (b, jnp.minimum((i + 1) * nb, last), 0))]


def _inproj_kernel(x_ref, sh_ref, sc_ref, g_ref, w_ref, qg_ref, kg_ref, bd_ref, cos_ref, sa_ref, sb_ref,
                   q_ref, k_ref, vt_ref, zc_ref):
    sub = min(PROJ_SUB_ROWS, x_ref.shape[1])
    for r0 in range(0, x_ref.shape[1], sub):
        rows = slice(r0, r0 + sub)
        h = _modulated_norm(x_ref[0, rows], g_ref[...], sh_ref[0], sc_ref[0])
        z = jnp.dot(h.astype(BF16), w_ref[...], preferred_element_type=F32)
        zc_ref[0, rows] = z[:, QKV_W:]
        cos, sa, sb = cos_ref[rows], sa_ref[rows], sb_ref[rows]

        def head_norm_rope(t, gain, bd):
            ssq = jnp.dot((t * t).astype(BF16), bd, preferred_element_type=F32)
            t = t * lax.rsqrt(ssq * (1.0 / HEAD_DIM) + RMS_EPS) * gain
            outs = []
            for j in range(t.shape[1] // LANES):
                tj = t[:, j * LANES:(j + 1) * LANES]
                outs.append(tj * cos + pltpu.roll(tj, LANES - 16, 1) * sa + pltpu.roll(tj, 16, 1) * sb)
            return outs

        qs = head_norm_rope(z[:, :Q_W], qg_ref[...], bd_ref[...])
        ks = head_norm_rope(z[:, Q_W:Q_W + KV_W], kg_ref[...], bd_ref[:KV_W, :KV_W])
        qscale = HEAD_DIM ** -0.5 * LOG2E
        for j, qj in enumerate(qs):
            qj = (qj * qscale).astype(BF16)
            q_ref[0, 2 * j, rows] = qj[:, :HEAD_DIM]
            q_ref[0, 2 * j + 1, rows] = qj[:, HEAD_DIM:]
        for j, kj in enumerate(ks):
            kj = kj.astype(BF16)
            k_ref[0, 2 * j, rows] = kj[:, :HEAD_DIM]
            k_ref[0, 2 * j + 1, rows] = kj[:, HEAD_DIM:]
        vt = z[:, Q_W + KV_W:QKV_W].T
        ones_row = (lax.broadcasted_iota(jnp.int32, (LANES - HEAD_DIM, sub), 0) == 0).astype(F32)
        for hh in range(N_KV_HEADS):
            vt_ref[0, hh, :, rows] = jnp.concatenate([vt[hh * HEAD_DIM:(hh + 1) * HEAD_DIM], ones_row],
                                                     axis=0).astype(BF16)


def _inproj(x, sh, sc, gain, w_bf, qg, kg, bd, cos, sa, sb):
    bx, t, d = x.shape
    tm = min(512, t)
    vec = pl.BlockSpec((1, 1, d), lambda b, i: (b, 0, 0))
    tab = pl.BlockSpec((tm, LANES), lambda b, i: (i, 0))
    return pl.pallas_call(
        _inproj_kernel,
        grid=(bx, t // tm),
        in_specs=[pl.BlockSpec((1, tm, d), lambda b, i: (b, i, 0)), vec, vec, _full((1, d)),
                  _full((d, MIX_IN_W)), _full((1, Q_W)), _full((1, KV_W)), _full((Q_W, Q_W)), tab, tab, tab],
        out_specs=[pl.BlockSpec((1, N_Q_HEADS, tm, HEAD_DIM), lambda b, i: (b, 0, i, 0)),
                   pl.BlockSpec((1, N_KV_HEADS, tm, HEAD_DIM), lambda b, i: (b, 0, i, 0)),
                   pl.BlockSpec((1, N_KV_HEADS, LANES, tm), lambda b, i: (b, 0, 0, i)),
                   pl.BlockSpec((1, tm, 3 * CONV_W), lambda b, i: (b, i, 0))],
        out_shape=[jax.ShapeDtypeStruct((bx, N_Q_HEADS, t, HEAD_DIM), BF16),
                   jax.ShapeDtypeStruct((bx, N_KV_HEADS, t, HEAD_DIM), BF16),
                   jax.ShapeDtypeStruct((bx, N_KV_HEADS, LANES, t), BF16),
                   jax.ShapeDtypeStruct((bx, t, 3 * CONV_W), F32)],
        compiler_params=_params(("arbitrary", "arbitrary")),
        name="in_proj",
    )(x, sh, sc, gain, w_bf, qg, kg, bd, cos, sa, sb)


def _scores_t(ks, q):
    return lax.dot_general(ks, q, (((1,), (1,)), ((), ())), preferred_element_type=F32)


def _sink_column(sink_ref, hh, tq, m, acc):
    col = lax.broadcasted_iota(jnp.int32, m.shape, 1)
    sink = jnp.where(col < tq, sink_ref[hh, 0], sink_ref[hh, 1]) * LOG2E
    m_new = jnp.maximum(m, sink)
    den_row = lax.broadcasted_iota(jnp.int32, acc.shape, 0) == HEAD_DIM
    return acc * jnp.exp2(m - m_new) + jnp.where(den_row, jnp.exp2(sink - m_new), 0.0)


def _attn_finish(acc, tq):
    o = (acc / acc[HEAD_DIM:HEAD_DIM + 1]).T
    return jnp.concatenate([o[:tq, :HEAD_DIM], o[tq:, :HEAD_DIM]], axis=1)


EXP_CHUNK_ELEMS = 32 * SUBLANES * LANES


def _score_stage(ks, q, s_ref, bias=None):
    s = _scores_t(ks, q)
    if bias is not None:
        s = s + bias
    s_ref[0:s.shape[0]] = s
    return jnp.max(s, axis=0, keepdims=True)


def _softmax_stage(s_ref, p_ref, n, vt, mx, m, acc_ref):
    m_new = jnp.maximum(m, mx)
    rows = EXP_CHUNK_ELEMS // m.shape[1]
    for c in range(0, n, rows):
        p_ref[c:c + rows] = jnp.exp2(s_ref[c:c + rows] - m_new).astype(BF16)
    acc_ref[...] = acc_ref[...] * jnp.exp2(m - m_new) + jnp.dot(vt, p_ref[0:n], preferred_element_type=F32)
    return m_new


BOUNDED_TILES = 4
SAFE_SHIFT = 40.0


def _flash_exact(q, k_tile, vt_tile, ctx, sa_ref, sb_ref, p_ref, acc_ref, *, tk, n_main):
    pa_ref, pb_ref = p_ref.at[0:tk], p_ref.at[tk:2 * tk]
    acc_ref[...] = jnp.zeros_like(acc_ref)
    m = jnp.full((1, q.shape[0]), NEG_INF, F32)
    mx_a = _score_stage(k_tile(0), q, sa_ref)
    if n_main > 1:
        def body(i, carry):
            m, mx_a = carry
            mx_b = _score_stage(k_tile(2 * i + 1), q, sb_ref)
            m = _softmax_stage(sa_ref, pa_ref, tk, vt_tile(2 * i), mx_a, m, acc_ref)
            mx_a = _score_stage(k_tile(2 * i + 2), q, sa_ref)
            m = _softmax_stage(sb_ref, pb_ref, tk, vt_tile(2 * i + 1), mx_b, m, acc_ref)
            return m, mx_a

        m, mx_a = lax.fori_loop(0, n_main // 2 - 1, body, (m, mx_a))
        mx_b = _score_stage(k_tile(n_main - 1), q, sb_ref)
        m = _softmax_stage(sa_ref, pa_ref, tk, vt_tile(n_main - 2), mx_a, m, acc_ref)
        last = (sb_ref, pb_ref, mx_b)
        spare = (sa_ref, pa_ref)
    else:
        last = (sa_ref, pa_ref, mx_a)
        spare = (sb_ref, pb_ref)
    if ctx is not None:
        mx_c = _score_stage(ctx[0], q, spare[0])
    m = _softmax_stage(last[0], last[1], tk, vt_tile(n_main - 1), last[2], m, acc_ref)
    if ctx is not None:
        m = _softmax_stage(spare[0], spare[1], ctx[0].shape[0], ctx[1], mx_c, m, acc_ref)
    return m


def _flash_bounded(q, k_ref, vt_ref, ctx, bound, p_ref, acc_ref, *, tk, n_main):
    step = p_ref.shape[0] if n_main * tk % p_ref.shape[0] == 0 else tk

    def pair(j):
        off = pl.multiple_of(j * step, step)
        for c in range(0, step, tk):
            s_t = _scores_t(k_ref[0, 0, pl.ds(off + c, tk), :], q)
            p_ref[c:c + tk] = jnp.exp2(s_t - bound).astype(BF16)
        return jnp.dot(vt_ref[0, 0, :, pl.ds(off, step)], p_ref[0:step], preferred_element_type=F32)

    acc_ref[...] = pair(0)

    def body(j, carry):
        acc_ref[...] += pair(j)
        return carry

    lax.fori_loop(1, n_main * tk // step, body, 0)
    if ctx is not None:
        n_ctx = ctx[0].shape[0]
        p_ref[0:n_ctx] = jnp.exp2(_scores_t(ctx[0], q) - bound).astype(BF16)
        acc_ref[...] += jnp.dot(ctx[1], p_ref[0:n_ctx], preferred_element_type=F32)


def _sq_norm_max(rows):
    r = rows.astype(F32)
    return jnp.max(jnp.sum(r * r, axis=1, keepdims=True), axis=0, keepdims=True)


def _flash_kernel(*refs, tq, tk, n_main, has_ctx, has_sink):
    sa_ref, sb_ref, p_ref, acc_ref, kmax_ref = refs[-5:]
    o_ref = refs[-6]
    refs = list(refs[:-6])
    sink_ref = refs.pop(0) if has_sink else None
    q_ref, k_ref, vt_ref = refs[:3]
    m_cols = GROUP * tq
    q = q_ref[0].reshape(m_cols, HEAD_DIM)
    ctx = (refs[3][0, 0], refs[4][0, 0]) if has_ctx else None

    def k_tile(j):
        return k_ref[0, 0, pl.ds(pl.multiple_of(j * tk, tk), tk), :]

    def vt_tile(j):
        return vt_ref[0, 0, :, pl.ds(pl.multiple_of(j * tk, tk), tk)]

    exact = functools.partial(_flash_exact, q, k_tile, vt_tile, ctx, sa_ref, sb_ref, p_ref, acc_ref,
                              tk=tk, n_main=n_main)
    if has_sink:
        acc = _sink_column(sink_ref, pl.program_id(1), tq, exact(), acc_ref[...])
    else:
        @pl.when(pl.program_id(2) == 0)
        def _():
            def body(j, mx):
                return jnp.maximum(mx, _sq_norm_max(k_tile(j)))

            mx = lax.fori_loop(0, n_main, body, jnp.zeros((1, 1), F32))
            if has_ctx:
                mx = jnp.maximum(mx, _sq_norm_max(ctx[0]))
            kmax_ref[...] = jnp.broadcast_to(mx, kmax_ref.shape)

        qf = q.astype(F32)
        qn2 = lax.dot_general(jnp.ones((SUBLANES, HEAD_DIM), F32), qf * qf, (((1,), (1,)), ((), ())),
                              preferred_element_type=F32, precision=HI)[0:1]
        bound = jnp.sqrt(qn2 * kmax_ref[0:1, 0:1])
        safe = jnp.max(bound) <= SAFE_SHIFT

        @pl.when(safe)
        def _():
            _flash_bounded(q, k_ref, vt_ref, ctx, bound, p_ref, acc_ref, tk=tk, n_main=n_main)

        @pl.when(jnp.logical_not(safe))
        def _():
            exact()

        acc = acc_ref[...]
    o_ref[0] = _attn_finish(acc, tq).astype(o_ref.dtype)


def _kv_specs(n):
    return [pl.BlockSpec((1, 1, n, HEAD_DIM), lambda b, h, i: (b, h, 0, 0)),
            pl.BlockSpec((1, 1, LANES, n), lambda b, h, i: (b, h, 0, 0))]


def _dense_attention(q, k, vt, kc=None, vct=None, sink=None):
    bx, _, t, _ = q.shape
    s = k.shape[2]
    tq = min(512, t)
    tk = min(512, s)
    has_ctx = kc is not None
    has_sink = sink is not None
    in_specs = [pl.BlockSpec((1, GROUP, tq, HEAD_DIM), lambda b, h, i: (b, h, i, 0))] + _kv_specs(s)
    args = [q, k, vt]
    if has_ctx:
        in_specs += _kv_specs(kc.shape[2])
        args += [kc, vct]
    if has_sink:
        in_specs = [pl.BlockSpec(memory_space=pltpu.SMEM)] + in_specs
        args = [sink] + args
    n_main = s // tk
    assert n_main == 1 or n_main % 2 == 0
    n_ctx = kc.shape[2] if has_ctx else 0
    assert n_ctx <= tk
    m_cols = GROUP * tq
    return pl.pallas_call(
        functools.partial(_flash_kernel, tq=tq, tk=tk, n_main=n_main, has_ctx=has_ctx, has_sink=has_sink),
        grid=(bx, N_KV_HEADS, t // tq),
        in_specs=in_specs,
        out_specs=pl.BlockSpec((1, tq, LANES), lambda b, h, i: (b, i, h)),
        out_shape=jax.ShapeDtypeStruct((bx, t, Q_W), BF16),
        scratch_shapes=[pltpu.VMEM((tk, m_cols), F32), pltpu.VMEM((tk, m_cols), F32),
                        pltpu.VMEM((BOUNDED_TILES * tk, m_cols), BF16), pltpu.VMEM((LANES, m_cols), F32),
                        pltpu.VMEM((SUBLANES, LANES), F32)],
        compiler_params=_params(("arbitrary", "arbitrary", "arbitrary")),
        name="dense_attention",
    )(*args)


BAND_SUB = 256


@functools.lru_cache(maxsize=None)
def _band_bias():
    span = BAND_SUB + 2 * WINDOW
    kr = np.arange(span)[None, :, None]
    qc = (np.arange(GROUP * BAND_SUB) % BAND_SUB)[None, None, :]
    rel = np.arange(3)[:, None, None]
    return np.where(np.abs(kr - qc - rel * WINDOW) <= WINDOW, 0.0, NEG_INF).astype(np.float32)


def _banded_kernel(sink_ref, q_ref, k_ref, vt_ref, kc_ref, vct_ref, bias_ref, o_ref,
                   sl_ref, sc_ref, pl_ref, pc_ref, acc_ref, *, n_sub, s_len):
    i = pl.program_id(2)
    sub = BAND_SUB
    m_cols = GROUP * sub
    span = sub + 2 * WINDOW
    n_ctx = kc_ref.shape[2]
    stats = []
    for u in range(n_sub):
        q = q_ref[0, :, u * sub:(u + 1) * sub, :].reshape(m_cols, HEAD_DIM)
        q0 = (i * n_sub + u) * sub
        start = pl.multiple_of(jnp.clip(q0 - WINDOW, 0, s_len - span), WINDOW)
        mx_l = _score_stage(k_ref[0, 0, pl.ds(start, span), :], q, sl_ref.at[u], bias=bias_ref[(q0 - start) // WINDOW])
        mx_c = _score_stage(kc_ref[0, 0], q, sc_ref.at[u])
        stats.append((start, mx_l, mx_c))
    for u in range(n_sub):
        start, mx_l, mx_c = stats[u]
        acc_u = acc_ref.at[u]
        acc_u[...] = jnp.zeros((LANES, m_cols), F32)
        m = jnp.full((1, m_cols), NEG_INF, F32)
        m = _softmax_stage(sl_ref.at[u], pl_ref.at[u], span, vt_ref[0, 0, :, pl.ds(start, span)], mx_l, m, acc_u)
        m = _softmax_stage(sc_ref.at[u], pc_ref.at[u], n_ctx, vct_ref[0, 0], mx_c, m, acc_u)
        acc = _sink_column(sink_ref, pl.program_id(1), sub, m, acc_u[...])
        o_ref[0, u * sub:(u + 1) * sub] = _attn_finish(acc, sub).astype(o_ref.dtype)


def _banded_attention(q, k, vt, kc, vct, sink):
    bx, _, t, _ = q.shape
    n_ctx = kc.shape[2]
    n_sub = 2
    tq = n_sub * BAND_SUB
    m_cols = GROUP * BAND_SUB
    span = BAND_SUB + 2 * WINDOW
    bias = jnp.asarray(_band_bias())
    return pl.pallas_call(
        functools.partial(_banded_kernel, n_sub=n_sub, s_len=t),
        grid=(bx, N_KV_HEADS, t // tq),
        in_specs=[pl.BlockSpec(memory_space=pltpu.SMEM),
                  pl.BlockSpec((1, GROUP, tq, HEAD_DIM), lambda b, h, i: (b, h, i, 0))]
                 + _kv_specs(t) + _kv_specs(n_ctx) + [_full(bias.shape)],
        out_specs=pl.BlockSpec((1, tq, LANES), lambda b, h, i: (b, i, h)),
        out_shape=jax.ShapeDtypeStruct((bx, t, Q_W), BF16),
        scratch_shapes=[pltpu.VMEM((n_sub, span, m_cols), F32), pltpu.VMEM((n_sub, n_ctx, m_cols), F32),
                        pltpu.VMEM((n_sub, span, m_cols), BF16), pltpu.VMEM((n_sub, n_ctx, m_cols), BF16),
                        pltpu.VMEM((n_sub, LANES, m_cols), F32)],
        compiler_params=_params(("arbitrary", "arbitrary", "arbitrary")),
        name="banded_attention",
    )(sink, q, k, vt, kc, vct, bias)


def _hyena_pre_kernel(z_ref, zp_ref, zn_ref, w_ref, b_ref, u_ref, x0_ref):
    i = pl.program_id(1)
    zm = z_ref[0]
    zp, zn = _shifted_rows(zm, zp_ref[0], zn_ref[0], i, pl.num_programs(1))
    w = w_ref[...]
    c = zp * w[0:1] + zm * w[1:2] + zn * w[2:3] + b_ref[...]
    x0_ref[0] = c[:, :CONV_W]
    u_ref[0] = c[:, 2 * CONV_W:] * c[:, CONV_W:2 * CONV_W]


def _hyena_pre(zc, conv_w, conv_b):
    bx, t, w3 = zc.shape
    tm = min(512, t)
    out = jax.ShapeDtypeStruct((bx, t, CONV_W), F32)
    ospec = pl.BlockSpec((1, tm, CONV_W), lambda b, i: (b, i, 0))
    return pl.pallas_call(
        _hyena_pre_kernel,
        grid=(bx, t // tm),
        in_specs=_halo_specs(tm, t, w3) + [_full((3, w3)), _full((1, w3))],
        out_specs=[ospec, ospec],
        out_shape=[out, out],
        compiler_params=_params(("arbitrary", "arbitrary")),
        name="hyena_pre",
    )(zc, zc, zc, conv_w, conv_b.reshape(1, w3))


def _short_conv_kernel(z_ref, zp_ref, zn_ref, w_ref, o_ref):
    i = pl.program_id(1)
    prod = lambda z: z[:, CONV_W:2 * CONV_W] * z[:, 2 * CONV_W:]
    zm = z_ref[0]
    pm = prod(zm)
    pp, pn = _shifted_rows(pm, prod(zp_ref[0]), prod(zn_ref[0]), i, pl.num_programs(1))
    w = w_ref[...]
    o_ref[0] = (zm[:, :CONV_W] * (pp * w[0:1] + pm * w[1:2] + pn * w[2:3])).astype(o_ref.dtype)


def _short_conv(zc, conv_w):
    bx, t, w3 = zc.shape
    tm = min(512, t)
    return pl.pallas_call(
        _short_conv_kernel,
        grid=(bx, t // tm),
        in_specs=_halo_specs(tm, t, w3) + [_full((3, CONV_W))],
        out_specs=pl.BlockSpec((1, tm, CONV_W), lambda b, i: (b, i, 0)),
        out_shape=jax.ShapeDtypeStruct((bx, t, CONV_W), BF16),
        compiler_params=_params(("arbitrary", "arbitrary")),
        name="short_conv",
    )(zc, zc, zc, conv_w)


@functools.lru_cache(maxsize=None)
def _filter_features(n):
    j = np.arange(2 * n)
    d = np.where(j <= n, j, 2 * n - j)
    d = np.where(j == n, 0, d)
    bands = (FILT_EMB - 1) // 2
    t01 = np.linspace(0.0, 1.0, n)[d]
    w = 2.0 * np.pi * d.astype(np.float64) / n
    f = np.linspace(1e-4, bands - 1, bands)[None, :]
    feats = np.zeros((2 * n, LANES), np.float64)
    feats[:, 0] = t01
    feats[:, 1:1 + bands] = np.cos(f * w[:, None])
    feats[:, 1 + bands:FILT_EMB] = -np.sin(f * w[:, None])
    feats[:, 64] = t01
    feats[:, 65] = (j < n)
    feats[:, 66] = (j != n)
    return feats.astype(np.float32)


def _filter_kernel(f_ref, w1_ref, b1_ref, w2_ref, b2_ref, w3_ref, b3_ref, w4_ref, fr_ref, dl_ref, k_ref, s_ref):
    f = f_ref[...]
    fr = fr_ref[...]
    mm = lambda a, b: jnp.dot(a, b, preferred_element_type=F32, precision=HI)
    h = jnp.sin(fr * (mm(f, w1_ref[...]) + b1_ref[...]))
    h = jnp.sin(fr * (mm(h, w2_ref[...]) + b2_ref[...]))
    h = jnp.sin(fr * (mm(h, w3_ref[...]) + b3_ref[...]))
    hf = mm(h, w4_ref[...])
    win = jnp.exp(-f[:, 64:65] * dl_ref[...]) + DECAY_SHIFT
    k = jnp.where(f[:, 65:66] > 0.5, hf[:, :CONV_W], hf[:, CONV_W:]) * win * f[:, 66:67]
    k_ref[...] = k

    @pl.when(pl.program_id(0) == 0)
    def _():
        s_ref[...] = jnp.zeros_like(s_ref)

    s_ref[...] += jnp.sum(jnp.abs(k), axis=0, keepdims=True)


def _implicit_filter(n, w1, b1, w2, b2, w3, b3, w4, freq):
    feats = jnp.asarray(_filter_features(n))
    pad = LANES - FILT_WIDTH
    padc = lambda a: jnp.pad(a.reshape(1, -1), ((0, 0), (0, pad)))
    w1p = jnp.pad(w1, ((0, LANES - FILT_EMB), (0, pad)))
    w2p = jnp.pad(w2, ((0, pad), (0, pad)))
    w3p = jnp.pad(w3, ((0, pad), (0, pad)))
    w4p = jnp.pad(w4, ((0, pad), (0, 0)))
    deltas = np.abs(np.linspace(math.log(DECAY_TARGET) / SLOW_DECAY_PCT, math.log(DECAY_TARGET) / FAST_DECAY_PCT,
                                CONV_W)).astype(np.float32).reshape(1, CONV_W)
    tr = min(1024, 2 * n)
    sq = _full((LANES, LANES))
    vec = _full((1, LANES))
    return pl.pallas_call(
        _filter_kernel,
        grid=(2 * n // tr,),
        in_specs=[pl.BlockSpec((tr, LANES), lambda i: (i, 0)), sq, vec, sq, vec, sq, vec,
                  _full((LANES, 2 * CONV_W)), vec, _full((1, CONV_W))],
        out_specs=[pl.BlockSpec((tr, CONV_W), lambda i: (i, 0)), _full((1, CONV_W))],
        out_shape=[jax.ShapeDtypeStruct((2 * n, CONV_W), F32), jax.ShapeDtypeStruct((1, CONV_W), F32)],
        compiler_params=_params(("arbitrary",)),
        name="hyena_filter",
    )(feats, w1p, padc(b1), w2p, padc(b2), w3p, padc(b3), w4p, padc(freq), jnp.asarray(deltas))


def _twiddle(idx, mod):
    ang = 2.0 * np.pi * (idx % mod) / mod
    return np.cos(ang), -np.sin(ang)


def _real_form(mr, mi):
    return np.concatenate([np.concatenate([mr, -mi], -1), np.concatenate([mi, mr], -1)], -2)


@functools.lru_cache(maxsize=None)
def _dft_tables(n2):
    n1 = DFT_N1
    n = n1 * n2
    h = n2 // 2
    a2 = np.arange(n2)
    fr, fi = _twiddle(np.outer(a2, a2), n2)
    m_data = _real_form(fr[:, :h], fi[:, :h])
    m_filt = np.concatenate([fr, fi], 0)
    k2 = a2[:, None, None]
    k1 = np.arange(n1)[None, :, None]
    c1 = np.arange(n1)[None, None, :]
    g = _real_form(*_twiddle(c1 * (n2 * k1 + k2), n))
    a1 = np.arange(n1)
    f1 = _real_form(*_twiddle(np.outer(a1, a1), n1))
    t2 = a1[:, None, None]
    t1 = np.arange(h)[None, :, None]
    j1 = a2[None, None, :]
    hh = _real_form(*_twiddle(j1 * (n1 * t1 + t2), n))
    cast = lambda m: np.asarray(m, dtype=BF16)
    return cast(m_data), cast(m_filt), cast(g), cast(f1), cast(hh)


DFT_ROW_CHUNK = SUBLANES


def _dft_rows_kernel(m_ref, *refs):
    o_ref = refs[-1]
    cols = [jnp.concatenate([r[0, :, t, :] for r in refs[:-1]], axis=0) for t in range(DFT_ROW_CHUNK)]
    rhs = jnp.concatenate(cols, axis=1).astype(BF16)
    out = jnp.dot(m_ref[...], rhs, preferred_element_type=F32)
    half = out.shape[0] // 2
    o_ref[0, 0] = out[:half].astype(o_ref.dtype)
    o_ref[0, 1] = out[half:].astype(o_ref.dtype)


def _dft_stage1(mat, views, pairs, rows, n2, c):
    nin = len(views)
    lc = DFT_ROW_CHUNK * c
    in_specs = [_full(mat.shape)] + [
        pl.BlockSpec((1, rows, DFT_ROW_CHUNK, c), (lambda p, j, a=a: (nin * p + a, 0, j, 0))) for a in range(nin)]
    return pl.pallas_call(
        _dft_rows_kernel,
        grid=(pairs, DFT_N1 // DFT_ROW_CHUNK),
        in_specs=in_specs,
        out_specs=pl.BlockSpec((1, 2, n2, lc), lambda p, j: (p, 0, 0, j)),
        out_shape=jax.ShapeDtypeStruct((pairs, 2, n2, DFT_N1 * c), BF16),
        compiler_params=_params(("arbitrary", "arbitrary")),
        name="dft_stage1",
    )(mat, *views)


def _spectrum_kernel(a_ref, g_ref, sc_ref, o_ref, *, kb):
    for j in range(kb):
        rhs = jnp.concatenate([a_ref[0, 0, j], a_ref[0, 1, j]], axis=0)
        x = jnp.dot(g_ref[j], rhs, preferred_element_type=F32)
        o_ref[0, j] = x[:DFT_N1] * sc_ref[...]
        o_ref[1, j] = x[DFT_N1:] * sc_ref[...]


def _filter_spectrum(a, g, scale, n2, kb):
    c = a.shape[-1]
    return pl.pallas_call(
        functools.partial(_spectrum_kernel, kb=kb),
        grid=(n2 // kb,),
        in_specs=[pl.BlockSpec((1, 2, kb, DFT_N1, c), lambda k: (0, 0, k, 0, 0)),
                  pl.BlockSpec((kb, 2 * DFT_N1, 2 * DFT_N1), lambda k: (k, 0, 0)), _full((1, c))],
        out_specs=pl.BlockSpec((2, kb, DFT_N1, c), lambda k: (0, k, 0, 0)),
        out_shape=jax.ShapeDtypeStruct((2, n2, DFT_N1, c), F32),
        compiler_params=_params(("arbitrary",)),
        name="filter_spectrum",
    )(a, g, scale)


def _dft_mid_kernel(a_ref, g_ref, kh_ref, f_ref, o_ref, *, kb):
    for j in range(kb):
        rhs = jnp.concatenate([a_ref[0, 0, j], a_ref[0, 1, j]], axis=0)
        x = jnp.dot(g_ref[j], rhs, preferred_element_type=F32)
        xr, xi = x[:DFT_N1], x[DFT_N1:]
        kr, ki = kh_ref[0, j], kh_ref[1, j]
        yr = xr * kr - xi * ki
        yi = xr * ki + xi * kr
        v = jnp.concatenate([yr, -yi], axis=0).astype(BF16)
        b = jnp.dot(f_ref[...], v, preferred_element_type=F32)
        o_ref[0, 0, j] = b[:DFT_N1].astype(o_ref.dtype)
        o_ref[0, 1, j] = b[DFT_N1:].astype(o_ref.dtype)


def _dft_mid(a, g, khat, f1, pairs, n2, kb):
    c = a.shape[-1]
    blk = pl.BlockSpec((1, 2, kb, DFT_N1, c), lambda k, p: (p, 0, k, 0, 0))
    return pl.pallas_call(
        functools.partial(_dft_mid_kernel, kb=kb),
        grid=(n2 // kb, pairs),
        in_specs=[blk, pl.BlockSpec((kb, 2 * DFT_N1, 2 * DFT_N1), lambda k, p: (k, 0, 0)),
                  pl.BlockSpec((2, kb, DFT_N1, c), lambda k, p: (0, k, 0, 0)), _full(f1.shape)],
        out_specs=blk,
        out_shape=jax.ShapeDtypeStruct(a.shape, BF16),
        compiler_params=_params(("arbitrary", "arbitrary")),
        name="dft_mid",
    )(a, g, khat, f1)


def _dft_last_kernel(b_ref, h_ref, u_ref, x0_ref, bd_ref, o_ref, *, c):
    half = h_ref.shape[1] // 2
    bd = bd_ref[...]
    for t in range(DFT_ROW_CHUNK):
        cols = slice(t * c, (t + 1) * c)
        rhs = jnp.concatenate([b_ref[0, 0, :, cols], b_ref[0, 1, :, cols]], axis=0)
        v = jnp.dot(h_ref[t], rhs, preferred_element_type=F32)
        o_ref[0, :, t, :] = (v[:half] + u_ref[0, :, t, :] * bd) * x0_ref[0, :, t, :]
        o_ref[1, :, t, :] = (-v[half:] + u_ref[1, :, t, :] * bd) * x0_ref[1, :, t, :]


def _dft_last(bm, hh, u_view, x0_view, bias_d, pairs, n2, c):
    half = n2 // 2
    tc = DFT_ROW_CHUNK
    io = pl.BlockSpec((2, half, tc, c), lambda p, j: (p, 0, j, 0))
    return pl.pallas_call(
        functools.partial(_dft_last_kernel, c=c),
        grid=(pairs, DFT_N1 // tc),
        in_specs=[pl.BlockSpec((1, 2, n2, tc * c), lambda p, j: (p, 0, 0, j)),
                  pl.BlockSpec((tc, n2, 2 * n2), lambda p, j: (j, 0, 0)), io, io, _full((1, c))],
        out_specs=io,
        out_shape=jax.ShapeDtypeStruct((2 * pairs, half, DFT_N1, c), F32),
        compiler_params=_params(("arbitrary", "arbitrary")),
        name="dft_last",
    )(bm, hh, u_view, x0_view, bias_d)


def _long_conv_mixer(u, x0, kfilt, ksum, bias_d):
    b, n, c = u.shape
    n2 = 2 * n // DFT_N1
    half = n2 // 2
    pairs = b // 2
    m_data, m_filt, g, f1, hh = (jnp.asarray(t) for t in _dft_tables(n2))
    kb = min(8, n2)
    scale = 1.0 / (ksum * float(DFT_N1 * n2))
    ka = _dft_stage1(m_filt, [kfilt.reshape(1, n2, DFT_N1, c)], 1, n2, n2, c)
    khat = _filter_spectrum(ka.reshape(1, 2, n2, DFT_N1, c), g, scale, n2, kb)
    u_view = u.reshape(b, half, DFT_N1, c)
    a = _dft_stage1(m_data, [u_view, u_view], pairs, half, n2, c)
    bm = _dft_mid(a.reshape(pairs, 2, n2, DFT_N1, c), g, khat, f1, pairs, n2, kb)
    out = _dft_last(bm.reshape(pairs, 2, n2, DFT_N1 * c), hh, u_view, x0.reshape(b, half, DFT_N1, c),
                    bias_d.reshape(1, c), pairs, n2, c)
    return out.reshape(b, n, c)


@functools.lru_cache(maxsize=None)
def _small_dft_tables(n):
    big = 2 * n
    a = np.arange(big)
    fr, fi = _twiddle(np.outer(a, a), big)
    cast = lambda m: np.asarray(m, dtype=BF16)
    return (cast(np.concatenate([fr, fi], 0)),
            cast(_real_form(fr[:, :n], fi[:, :n])),
            cast(_real_form(fr[:n], fi[:n])))


def _small_conv_kernel(k_ref, ks_ref, u_ref, x0_ref, bd_ref, mf_ref, md_ref, mi_ref, o_ref, *, n):
    big = 2 * n
    kh = jnp.dot(mf_ref[...], k_ref[...].astype(BF16), preferred_element_type=F32) * (1.0 / (ks_ref[...] * big))
    kr, ki = kh[:big], kh[big:]
    rhs = jnp.concatenate([u_ref[0], u_ref[1]], axis=0).astype(BF16)
    x = jnp.dot(md_ref[...], rhs, preferred_element_type=F32)
    xr, xi = x[:big], x[big:]
    v = jnp.concatenate([xr * kr - xi * ki, -(xr * ki + xi * kr)], axis=0).astype(BF16)
    y = jnp.dot(mi_ref[...], v, preferred_element_type=F32)
    bd = bd_ref[...]
    o_ref[0] = ((y[:n] + u_ref[0] * bd) * x0_ref[0]).astype(o_ref.dtype)
    o_ref[1] = ((-y[n:] + u_ref[1] * bd) * x0_ref[1]).astype(o_ref.dtype)


def _small_conv_mixer(u, x0, kfilt, ksum, bias_d):
    b, n, c = u.shape
    mf, md, mi = (jnp.asarray(t) for t in _small_dft_tables(n))
    io = pl.BlockSpec((2, n, c), lambda p: (p, 0, 0))
    return pl.pallas_call(
        functools.partial(_small_conv_kernel, n=n),
        grid=(b // 2,),
        in_specs=[_full((2 * n, c)), _full((1, c)), io, io, _full((1, c)),
                  _full(mf.shape), _full(md.shape), _full(mi.shape)],
        out_specs=io,
        out_shape=jax.ShapeDtypeStruct((b, n, c), BF16),
        compiler_params=_params(("arbitrary",)),
        name="small_conv",
    )(kfilt, ksum, u, x0, bias_d.reshape(1, c), mf, md, mi)


def _outproj_kernel(x_ref, oa_ref, oc_ref, wa_ref, wc_ref, g_ref, o_ref):
    sub = min(PROJ_SUB_ROWS, x_ref.shape[1])
    for r0 in range(0, x_ref.shape[1], sub):
        rows = slice(r0, r0 + sub)
        y = (jnp.dot(oa_ref[0, rows], wa_ref[...], preferred_element_type=F32)
             + jnp.dot(oc_ref[0, rows].astype(BF16), wc_ref[...], preferred_element_type=F32))
        o_ref[0, rows] = x_ref[0, rows] + g_ref[0] * y


def _outproj(x, oa, oc, w_bf, gate):
    bx, t, d = x.shape
    tm = min(512, t)
    row = lambda w: pl.BlockSpec((1, tm, w), lambda b, i: (b, i, 0))
    return pl.pallas_call(
        _outproj_kernel,
        grid=(bx, t // tm),
        in_specs=[row(d), row(Q_W), row(CONV_W), pl.BlockSpec((Q_W, d), lambda b, i: (0, 0)),
                  pl.BlockSpec((CONV_W, d), lambda b, i: (1, 0)), pl.BlockSpec((1, 1, d), lambda b, i: (b, 0, 0))],
        out_specs=row(d),
        out_shape=jax.ShapeDtypeStruct((bx, t, d), F32),
        compiler_params=_params(("arbitrary", "arbitrary")),
        name="out_proj",
    )(x, oa, oc, w_bf, w_bf, gate)


FFN_CHUNK = D_FF // 2


def _gelu_tanh(x):
    return 0.5 * x * (1.0 + jnp.tanh(math.sqrt(2.0 / math.pi) * (x + 0.044715 * (x * x * x))))


def _ffn_kernel(x_ref, xp_ref, xn_ref, sh_ref, sc_ref, gt_ref, g_ref, wu_ref, cw_ref, cb_ref, wd_ref, o_ref):
    i = pl.program_id(1)
    nt = pl.num_programs(1)
    xm = x_ref[0]
    tm = xm.shape[0]
    ext = jnp.concatenate([xp_ref[0], xm, xn_ref[0]], axis=0)
    r = tm + 2 * SUBLANES
    h = _modulated_norm(ext, g_ref[...], sh_ref[0], sc_ref[0])
    row = lax.broadcasted_iota(jnp.int32, (r, 1), 0)
    inside = jnp.logical_and(jnp.logical_or(i > 0, row >= SUBLANES), jnp.logical_or(i < nt - 1, row < tm + SUBLANES))
    h = jnp.where(inside, h, 0.0).astype(BF16)
    hm = h[SUBLANES:tm + SUBLANES]
    acc = jnp.zeros((tm, D_MODEL), F32)
    for c0 in range(0, D_FF, FFN_CHUNK):
        a = jnp.dot(h, wu_ref[:, c0:c0 + FFN_CHUNK], preferred_element_type=F32)
        v = jnp.dot(hm, wu_ref[:, D_FF + c0:D_FF + c0 + FFN_CHUNK], preferred_element_type=F32)
        ap = pltpu.roll(a, 1, 0)[SUBLANES:tm + SUBLANES]
        an = pltpu.roll(a, r - 1, 0)[SUBLANES:tm + SUBLANES]
        cw = cw_ref[:, c0:c0 + FFN_CHUNK]
        conv = ap * cw[0:1] + a[SUBLANES:tm + SUBLANES] * cw[1:2] + an * cw[2:3] + cb_ref[:, c0:c0 + FFN_CHUNK]
        act = (_gelu_tanh(conv) * v).astype(BF16)
        acc = acc + jnp.dot(act, wd_ref[c0:c0 + FFN_CHUNK, :], preferred_element_type=F32)
    o_ref[0] = xm + gt_ref[0] * acc


def _ffn(x, sh, sc, gate, gain, wu_bf, conv_w, conv_b, wd_bf):
    bx, t, d = x.shape
    tm = min(512, t)
    vec = pl.BlockSpec((1, 1, d), lambda b, i: (b, 0, 0))
    once = pl.Buffered(1)
    return pl.pallas_call(
        _ffn_kernel,
        grid=(bx, t // tm),
        in_specs=_halo_specs(tm, t, d) + [
            vec, vec, vec, _full((1, d)),
            pl.BlockSpec((d, 2 * D_FF), lambda b, i: (0, 0), pipeline_mode=once),
            _full((3, D_FF)), _full((1, D_FF)),
            pl.BlockSpec((D_FF, d), lambda b, i: (0, 0), pipeline_mode=once)],
        out_specs=pl.BlockSpec((1, tm, d), lambda b, i: (b, i, 0)),
        out_shape=jax.ShapeDtypeStruct((bx, t, d), F32),
        compiler_params=_params(("arbitrary", "arbitrary")),
        name="conv_ffn",
    )(x, x, x, sh, sc, gate, gain, wu_bf, conv_w, conv_b.reshape(1, D_FF), wd_bf)


@functools.lru_cache(maxsize=None)
def _rope_tables(t):
    pos = np.arange(t)
    n_freq = HEAD_DIM // 4
    inv = ROPE_THETA ** (-np.arange(n_freq, dtype=np.float64) / n_freq)
    ang_r = (pos // GRID_W)[:, None] * inv
    ang_c = (pos % GRID_W)[:, None] * inv
    zero = np.zeros_like(ang_r)
    cos = np.concatenate([np.cos(ang_r)] * 2 + [np.cos(ang_c)] * 2, 1)
    sa = np.concatenate([-np.sin(ang_r), zero, -np.sin(ang_c), zero], 1)
    sb = np.concatenate([zero, np.sin(ang_r), zero, np.sin(ang_c)], 1)
    tile = lambda m: np.tile(m, (1, LANES // HEAD_DIM)).astype(np.float32)
    return tile(cos), tile(sa), tile(sb)


@functools.lru_cache(maxsize=None)
def _identity_rope_tables(t):
    return np.ones((t, LANES), np.float32), np.zeros((t, LANES), np.float32), np.zeros((t, LANES), np.float32)


@functools.lru_cache(maxsize=None)
def _head_block_diag():
    hid = np.arange(Q_W) // HEAD_DIM
    return np.asarray(hid[:, None] == hid[None, :], dtype=BF16)


def kernel(x, c, ctx, c_ctx, ada_w, ada_b, norm_mix, norm_ffn, mix_w_in, mix_w_out, attn_q_norm, attn_k_norm,
           swa_sink, hy_conv_w, hy_conv_b, hy_w1, hy_b1, hy_w2, hy_b2, hy_w3, hy_b3, hy_w4, hy_freq, hy_bias_d,
           sc_conv_w, ffn_w_up, ffn_conv_w, ffn_conv_b, ffn_w_down):
    b, s, d = x.shape
    s_ctx = ctx.shape[1]
    depth = ada_w.shape[0]
    assert d == D_MODEL and b % 2 == 0 and b + 1 <= SUBLANES and s % 1024 == 0 and s_ctx % SUBLANES == 0

    cvec = jnp.concatenate([c, c_ctx[None, :], jnp.zeros((SUBLANES - b - 1, d), F32)], axis=0)
    mods = _mods(cvec, ada_w, ada_b)
    rope = [jnp.asarray(t) for t in _rope_tables(s)]
    rope_ctx = [jnp.asarray(t) for t in _identity_rope_tables(s_ctx)]
    bd = jnp.asarray(_head_block_diag())
    xc = ctx

    for i in range(depth):
        last = i == depth - 1
        j = i // 2
        lat = [mods[i, :b, k * d:(k + 1) * d][:, None, :] for k in range(6)]
        cx = [jnp.broadcast_to(mods[i, b, k * d:(k + 1) * d][None, None, :], (b, 1, d)) for k in range(6)]
        w_in = mix_w_in[i].astype(BF16)
        w_out = mix_w_out[i].astype(BF16)
        w_up = ffn_w_up[i].astype(BF16)
        w_down = ffn_w_down[i].astype(BF16)
        g_mix = norm_mix[i].reshape(1, d)
        g_ffn = norm_ffn[i].reshape(1, d)
        qg = jnp.tile(attn_q_norm[i], N_Q_HEADS).reshape(1, Q_W)
        kg = jnp.tile(attn_k_norm[i], N_KV_HEADS).reshape(1, KV_W)

        q, k, v, zc = _inproj(x, lat[0], lat[1], g_mix, w_in, qg, kg, bd, *rope)
        qc, kc, vc, zcc = _inproj(xc, cx[0], cx[1], g_mix, w_in, qg, kg, bd, *rope_ctx)

        if i % 2 == 0:
            fargs = (hy_w1[j], hy_b1[j], hy_w2[j], hy_b2[j], hy_w3[j], hy_b3[j], hy_w4[j], hy_freq[j])
            o_attn = _dense_attention(q, k, v, kc, vc)
            u, x0 = _hyena_pre(zc, hy_conv_w[j], hy_conv_b[j])
            kf, ks = _implicit_filter(s, *fargs)
            o_conv = _long_conv_mixer(u, x0, kf, ks, hy_bias_d[j])
            if not last:
                oc_attn = _dense_attention(qc, kc, vc)
                uc, x0c = _hyena_pre(zcc, hy_conv_w[j], hy_conv_b[j])
                kfc, ksc = _implicit_filter(s_ctx, *fargs)
                oc_conv = _small_conv_mixer(uc, x0c, kfc, ksc, hy_bias_d[j])
        else:
            sink = swa_sink[j].reshape(N_KV_HEADS, GROUP)
            o_attn = _banded_attention(q, k, v, kc, vc, sink)
            o_conv = _short_conv(zc, sc_conv_w[j])
            if not last:
                oc_attn = _dense_attention(qc, kc, vc, sink=sink)
                oc_conv = _short_conv(zcc, sc_conv_w[j])

        x = _outproj(x, o_attn, o_conv, w_out, lat[2])
        x = _ffn(x, lat[3], lat[4], lat[5], g_ffn, w_up, ffn_conv_w[i], ffn_conv_b[i], w_down)
        if not last:
            xc = _outproj(xc, oc_attn, oc_conv, w_out, cx[2])
            xc = _ffn(xc, cx[3], cx[4], cx[5], g_ffn, w_up, ffn_conv_w[i], ffn_conv_b[i], w_down)
    return x
```

```python
import functools
import math

import numpy as np
import jax
import jax.numpy as jnp
from jax import lax
from jax.experimental import pallas as pl
from jax.experimental.pallas import tpu as pltpu

F32 = jnp.float32
BF16 = jnp.bfloat16
HI = lax.Precision.HIGHEST

D_MODEL = 1024
GRID_W = 64
HEAD_DIM = 64
N_Q_HEADS = 8
N_KV_HEADS = 4
GROUP = N_Q_HEADS // N_KV_HEADS
Q_W = N_Q_HEADS * HEAD_DIM
KV_W = N_KV_HEADS * HEAD_DIM
QKV_W = Q_W + 2 * KV_W
CONV_W = D_MODEL // 2
MIX_IN_W = QKV_W + 3 * CONV_W
WINDOW = 128
ROPE_THETA = 10000.0
FILT_EMB = 33
FILT_WIDTH = 64
DECAY_TARGET = 1e-2
FAST_DECAY_PCT = 0.3
SLOW_DECAY_PCT = 1.5
DECAY_SHIFT = 0.05
D_FF = 2816
NEG_INF = -1e30
RMS_EPS = 1e-6
LOG2E = 1.4426950408889634
Q_SCALE = HEAD_DIM ** -0.5 * LOG2E

LANES = 128
SUBLANES = 8
DFT_N1 = 128
VMEM_LIMIT_MB = 56
PROJ_SUB_ROWS = 128


def _params(sem, vmem_mb=VMEM_LIMIT_MB):
    return pltpu.CompilerParams(dimension_semantics=sem, vmem_limit_bytes=vmem_mb * 1024 * 1024)


def _full(shape):
    nd = len(shape)
    return pl.BlockSpec(shape, lambda *_: (0,) * nd)


def _mods_kernel(c_ref, w_ref, b_ref, o_ref):
    c = c_ref[...]
    s = c / (1.0 + jnp.exp(-c))
    o_ref[0] = jnp.dot(s, w_ref[0], preferred_element_type=F32, precision=HI) + b_ref[0]


def _mods(cvec, ada_w, ada_b):
    depth, d, n6 = ada_w.shape
    tn = 1536
    return pl.pallas_call(
        _mods_kernel,
        grid=(depth, n6 // tn),
        in_specs=[_full((SUBLANES, d)),
                  pl.BlockSpec((1, d, tn), lambda l, j: (l, 0, j)),
                  pl.BlockSpec((1, 1, tn), lambda l, j: (l, 0, j))],
        out_specs=pl.BlockSpec((1, SUBLANES, tn), lambda l, j: (l, 0, j)),
        out_shape=jax.ShapeDtypeStruct((depth, SUBLANES, n6), F32),
        compiler_params=_params(("arbitrary", "arbitrary")),
        name="ada_mods",
    )(cvec, ada_w, ada_b.reshape(depth, 1, n6))


def _modulated_norm(x, gain, shift, scale):
    ms = jnp.mean(x * x, axis=-1, keepdims=True)
    return (x * lax.rsqrt(ms + RMS_EPS)) * gain * (1.0 + scale) + shift


def _halo_specs(tm, t, width):
    nb = tm // SUBLANES
    last = t // SUBLANES - 1
    return [pl.BlockSpec((1, tm, width), lambda b, i: (b, i, 0)),
            pl.BlockSpec((1, SUBLANES, width), lambda b, i: (b, jnp.maximum(i * nb - 1, 0), 0)),
            pl.BlockSpec((1, SUBLANES, width), lambda b, i: (b, jnp.minimum((i + 1) * nb, last), 0))]


def _inproj_kernel(*refs, mode):
    (x_ref, xp_ref, xn_ref, sh_ref, sc_ref, g_ref, w_ref, qg_ref, kg_ref, bd_ref,
     cos_ref, sa_ref, sb_ref, cw_ref) = refs[:14]
    if mode == "hyena":
        cb_ref, q_ref, k_ref, vt_ref, u_ref, x0_ref, zc_ref = refs[14:]
    else:
        q_ref, k_ref, vt_ref, oc_ref, zc_ref = refs[14:]
    i = pl.program_id(1)
    nt = pl.num_programs(1)
    tm = x_ref.shape[1]
    sub = min(PROJ_SUB_ROWS, tm)
    gain, shift, scale = g_ref[...], sh_ref[0], sc_ref[0]

    xh = jnp.concatenate([xp_ref[0], xn_ref[0]], axis=0)
    zh = jnp.dot(_modulated_norm(xh, gain, shift, scale).astype(BF16), w_ref[:, QKV_W:], preferred_element_type=F32)
    zc_ref[0:SUBLANES] = jnp.where(i > 0, zh[:SUBLANES], 0.0)
    zc_ref[tm + SUBLANES:tm + 2 * SUBLANES] = jnp.where(i < nt - 1, zh[SUBLANES:], 0.0)

    def head_norm_rope(t, hgain, bd, cos, sa, sb):
        ssq = jnp.dot((t * t).astype(BF16), bd, preferred_element_type=F32)
        t = t * lax.rsqrt(ssq * (1.0 / HEAD_DIM) + RMS_EPS) * hgain
        outs = []
        for j in range(t.shape[1] // LANES):
            tj = t[:, j * LANES:(j + 1) * LANES]
            outs.append(tj * cos + pltpu.roll(tj, LANES - 16, 1) * sa + pltpu.roll(tj, 16, 1) * sb)
        return outs

    for r0 in range(0, tm, sub):
        rows = slice(r0, r0 + sub)
        h = _modulated_norm(x_ref[0, rows], gain, shift, scale)
        z = jnp.dot(h.astype(BF16), w_ref[...], preferred_element_type=F32)
        zc_ref[SUBLANES + r0:SUBLANES + r0 + sub] = z[:, QKV_W:]
        tabs = (cos_ref[rows], sa_ref[rows], sb_ref[rows])
        qs = head_norm_rope(z[:, :Q_W], qg_ref[...], bd_ref[...], *tabs)
        ks = head_norm_rope(z[:, Q_W:Q_W + KV_W], kg_ref[...], bd_ref[:KV_W, :KV_W], *tabs)
        for j, qj in enumerate(qs):
            qj = (qj * Q_SCALE).astype(BF16)
            q_ref[0, 2 * j, rows] = qj[:, :HEAD_DIM]
            q_ref[0, 2 * j + 1, rows] = qj[:, HEAD_DIM:]
        for j, kj in enumerate(ks):
            kj = kj.astype(BF16)
            k_ref[0, 2 * j, rows] = kj[:, :HEAD_DIM]
            k_ref[0, 2 * j + 1, rows] = kj[:, HEAD_DIM:]
        vt = z[:, Q_W + KV_W:QKV_W].T
        ones_row = (lax.broadcasted_iota(jnp.int32, (LANES - HEAD_DIM, sub), 0) == 0).astype(F32)
        for hh in range(N_KV_HEADS):
            vt_ref[0, hh, :, rows] = jnp.concatenate([vt[hh * HEAD_DIM:(hh + 1) * HEAD_DIM], ones_row],
                                                     axis=0).astype(BF16)

    cw = cw_ref[...]
    for r0 in range(0, tm, sub):
        rows = slice(r0, r0 + sub)
        ext = zc_ref[r0:r0 + sub + 2 * SUBLANES]
        n = sub + 2 * SUBLANES
        mid = slice(SUBLANES, sub + SUBLANES)
        if mode == "hyena":
            c = (pltpu.roll(ext, 1, 0)[mid] * cw[0:1] + ext[mid] * cw[1:2] + pltpu.roll(ext, n - 1, 0)[mid] * cw[2:3]
                 + cb_ref[...])
            x0_ref[0, rows] = c[:, :CONV_W]
            u_ref[0, rows] = c[:, 2 * CONV_W:] * c[:, CONV_W:2 * CONV_W]
        else:
            pr = ext[:, CONV_W:2 * CONV_W] * ext[:, 2 * CONV_W:]
            conv = pltpu.roll(pr, 1, 0)[mid] * cw[0:1] + pr[mid] * cw[1:2] + pltpu.roll(pr, n - 1, 0)[mid] * cw[2:3]
            oc_ref[0, rows] = (ext[mid, :CONV_W] * conv).astype(oc_ref.dtype)


def _inproj(x, sh, sc, gain, w_bf, qg, kg, bd, cos, sa, sb, conv_w, conv_b=None):
    bx, t, d = x.shape
    tm = min(512, t)
    mode = "hyena" if conv_b is not None else "short"
    vec = pl.BlockSpec((1, 1, d), lambda b, i: (b, 0, 0))
    tab = pl.BlockSpec((tm, LANES), lambda b, i: (i, 0))
    row = lambda w: pl.BlockSpec((1, tm, w), lambda b, i: (b, i, 0))
    in_specs = _halo_specs(tm, t, d) + [vec, vec, _full((1, d)), _full((d, MIX_IN_W)), _full((1, Q_W)),
                                        _full((1, KV_W)), _full((Q_W, Q_W)), tab, tab, tab, _full(conv_w.shape)]
    args = [x, x, x, sh, sc, gain, w_bf, qg, kg, bd, cos, sa, sb, conv_w]
    out_specs = [pl.BlockSpec((1, N_Q_HEADS, tm, HEAD_DIM), lambda b, i: (b, 0, i, 0)),
                 pl.BlockSpec((1, N_KV_HEADS, tm, HEAD_DIM), lambda b, i: (b, 0, i, 0)),
                 pl.BlockSpec((1, N_KV_HEADS, LANES, tm), lambda b, i: (b, 0, 0, i))]
    out_shape = [jax.ShapeDtypeStruct((bx, N_Q_HEADS, t, HEAD_DIM), BF16),
                 jax.ShapeDtypeStruct((bx, N_KV_HEADS, t, HEAD_DIM), BF16),
                 jax.ShapeDtypeStruct((bx, N_KV_HEADS, LANES, t), BF16)]
    if mode == "hyena":
        in_specs.append(_full((1, 3 * CONV_W)))
        args.append(conv_b.reshape(1, 3 * CONV_W))
        out_specs += [row(CONV_W), row(CONV_W)]
        out_shape += [jax.ShapeDtypeStruct((bx, t, CONV_W), F32)] * 2
    else:
        out_specs.append(row(CONV_W))
        out_shape.append(jax.ShapeDtypeStruct((bx, t, CONV_W), BF16))
    return pl.pallas_call(
        functools.partial(_inproj_kernel, mode=mode),
        grid=(bx, t // tm),
        in_specs=in_specs,
        out_specs=out_specs,
        out_shape=out_shape,
        scratch_shapes=[pltpu.VMEM((tm + 2 * SUBLANES, 3 * CONV_W), F32)],
        compiler_params=_params(("arbitrary", "arbitrary")),
        name="in_proj",
    )(*args)


def _score_bound(q_gain, k_gain):
    b = HEAD_DIM * Q_SCALE * BOUND_MARGIN * jnp.max(jnp.abs(q_gain)) * jnp.max(jnp.abs(k_gain))
    return b.reshape(1, 1).astype(F32)


def _scores_t(ks, q):
    return lax.dot_general(ks, q, (((1,), (1,)), ((), ())), preferred_element_type=F32)


def _sink_column(sink_ref, hh, tq, m, acc):
    col = lax.broadcasted_iota(jnp.int32, m.shape, 1)
    sink = jnp.where(col < tq, sink_ref[hh, 0], sink_ref[hh, 1]) * LOG2E
    m_new = jnp.maximum(m, sink)
    den_row = lax.broadcasted_iota(jnp.int32, acc.shape, 0) == HEAD_DIM
    return acc * jnp.exp2(m - m_new) + jnp.where(den_row, jnp.exp2(sink - m_new), 0.0)


def _attn_finish(acc, tq):
    o = (acc / acc[HEAD_DIM:HEAD_DIM + 1]).T
    return jnp.concatenate([o[:tq, :HEAD_DIM], o[tq:, :HEAD_DIM]], axis=1)


EXP_CHUNK_ELEMS = 32 * SUBLANES * LANES


def _score_stage(ks, q, s_ref, bias=None):
    s = _scores_t(ks, q)
    if bias is not None:
        s = s + bias
    s_ref[0:s.shape[0]] = s
    return jnp.max(s, axis=0, keepdims=True)


def _softmax_stage(s_ref, p_ref, n, vt, mx, m, acc_ref):
    m_new = jnp.maximum(m, mx)
    rows = EXP_CHUNK_ELEMS // m.shape[1]
    for c in range(0, n, rows):
        p_ref[c:c + rows] = jnp.exp2(s_ref[c:c + rows] - m_new).astype(BF16)
    acc_ref[...] = acc_ref[...] * jnp.exp2(m - m_new) + jnp.dot(vt, p_ref[0:n], preferred_element_type=F32)
    return m_new


BOUNDED_TILES = 4
BOUND_MARGIN = 1.01
SAFE_SHIFT = 40.0


def _flash_exact(q, k_tile, vt_tile, ctx, sa_ref, sb_ref, p_ref, acc_ref, *, tk, n_main):
    pa_ref, pb_ref = p_ref.at[0:tk], p_ref.at[tk:2 * tk]
    acc_ref[...] = jnp.zeros_like(acc_ref)
    m = jnp.full((1, q.shape[0]), NEG_INF, F32)
    mx_a = _score_stage(k_tile(0), q, sa_ref)
    if n_main > 1:
        def body(i, carry):
            m, mx_a = carry
            mx_b = _score_stage(k_tile(2 * i + 1), q, sb_ref)
            m = _softmax_stage(sa_ref, pa_ref, tk, vt_tile(2 * i), mx_a, m, acc_ref)
            mx_a = _score_stage(k_tile(2 * i + 2), q, sa_ref)
            m = _softmax_stage(sb_ref, pb_ref, tk, vt_tile(2 * i + 1), mx_b, m, acc_ref)
            return m, mx_a

        m, mx_a = lax.fori_loop(0, n_main // 2 - 1, body, (m, mx_a))
        mx_b = _score_stage(k_tile(n_main - 1), q, sb_ref)
        m = _softmax_stage(sa_ref, pa_ref, tk, vt_tile(n_main - 2), mx_a, m, acc_ref)
        last = (sb_ref, pb_ref, mx_b)
        spare = (sa_ref, pa_ref)
    else:
        last = (sa_ref, pa_ref, mx_a)
        spare = (sb_ref, pb_ref)
    if ctx is not None:
        mx_c = _score_stage(ctx[0], q, spare[0])
    m = _softmax_stage(last[0], last[1], tk, vt_tile(n_main - 1), last[2], m, acc_ref)
    if ctx is not None:
        m = _softmax_stage(spare[0], spare[1], ctx[0].shape[0], ctx[1], mx_c, m, acc_ref)
    return m


def _flash_bounded(q, k_ref, vt_ref, ctx, bound, p_ref, acc_ref, *, tk, n_main):
    step = BOUNDED_TILES * tk if n_main % BOUNDED_TILES == 0 else tk

    def group(j):
        off = pl.multiple_of(j * step, step)
        for c in range(0, step, tk):
            s_t = _scores_t(k_ref[0, 0, pl.ds(off + c, tk), :], q)
            p_ref[c:c + tk] = jnp.exp2(s_t - bound).astype(BF16)
        return jnp.dot(vt_ref[0, 0, :, pl.ds(off, step)], p_ref[0:step], preferred_element_type=F32)

    first = group(0)
    if ctx is not None:
        pc_ref = p_ref.at[BOUNDED_TILES * tk:BOUNDED_TILES * tk + ctx[0].shape[0]]
        pc_ref[...] = jnp.exp2(_scores_t(ctx[0], q) - bound).astype(BF16)
        first = first + jnp.dot(ctx[1], pc_ref[...], preferred_element_type=F32)
    acc_ref[...] = first

    def body(j, carry):
        acc_ref[...] += group(j)
        return carry

    lax.fori_loop(1, n_main * tk // step, body, 0)


def _flash_kernel(*refs, tq, tk, n_main, has_ctx, has_sink):
    sa_ref, sb_ref, p_ref, acc_ref = refs[-4:]
    o_ref = refs[-5]
    refs = list(refs[:-5])
    sink_ref = refs.pop(0) if has_sink else None
    bound_ref = None if has_sink else refs.pop(0)
    q_ref, k_ref, vt_ref = refs[:3]
    m_cols = GROUP * tq
    q = q_ref[0].reshape(m_cols, HEAD_DIM)
    ctx = (refs[3][0, 0], refs[4][0, 0]) if has_ctx else None

    def k_tile(j):
        return k_ref[0, 0, pl.ds(pl.multiple_of(j * tk, tk), tk), :]

    def vt_tile(j):
        return vt_ref[0, 0, :, pl.ds(pl.multiple_of(j * tk, tk), tk)]

    exact = functools.partial(_flash_exact, q, k_tile, vt_tile, ctx, sa_ref, sb_ref, p_ref, acc_ref,
                              tk=tk, n_main=n_main)
    if has_sink:
        acc = _sink_column(sink_ref, pl.program_id(1), tq, exact(), acc_ref[...])
    else:
        bound = bound_ref[0, 0]
        safe = bound <= SAFE_SHIFT

        @pl.when(safe)
        def _():
            _flash_bounded(q, k_ref, vt_ref, ctx, bound, p_ref, acc_ref, tk=tk, n_main=n_main)

        @pl.when(jnp.logical_not(safe))
        def _():
            exact()

        acc = acc_ref[...]
    o_ref[0] = _attn_finish(acc, tq).astype(o_ref.dtype)


def _kv_specs(n):
    return [pl.BlockSpec((1, 1, n, HEAD_DIM), lambda b, h, i: (b, h, 0, 0)),
            pl.BlockSpec((1, 1, LANES, n), lambda b, h, i: (b, h, 0, 0))]


def _dense_attention(q, k, vt, kc=None, vct=None, *, bound=None, sink=None):
    bx, _, t, _ = q.shape
    s = k.shape[2]
    tq = min(512, t)
    tk = min(512, s)
    has_ctx = kc is not None
    has_sink = sink is not None
    in_specs = [pl.BlockSpec((1, GROUP, tq, HEAD_DIM), lambda b, h, i: (b, h, i, 0))] + _kv_specs(s)
    args = [q, k, vt]
    if has_ctx:
        in_specs += _kv_specs(kc.shape[2])
        args += [kc, vct]
    in_specs = [pl.BlockSpec(memory_space=pltpu.SMEM)] + in_specs
    args = [sink if has_sink else bound] + args
    n_main = s // tk
    assert n_main == 1 or n_main % 2 == 0
    n_ctx = kc.shape[2] if has_ctx else 0
    assert n_ctx <= tk
    m_cols = GROUP * tq
    return pl.pallas_call(
        functools.partial(_flash_kernel, tq=tq, tk=tk, n_main=n_main, has_ctx=has_ctx, has_sink=has_sink),
        grid=(bx, N_KV_HEADS, t // tq),
        in_specs=in_specs,
        out_specs=pl.BlockSpec((1, tq, LANES), lambda b, h, i: (b, i, h)),
        out_shape=jax.ShapeDtypeStruct((bx, t, Q_W), BF16),
        scratch_shapes=[pltpu.VMEM((tk, m_cols), F32), pltpu.VMEM((tk, m_cols), F32),
                        pltpu.VMEM((BOUNDED_TILES * tk + n_ctx, m_cols), BF16), pltpu.VMEM((LANES, m_cols), F32)],
        compiler_params=_params(("arbitrary", "arbitrary", "arbitrary")),
        name="dense_attention",
    )(*args)


BAND_SUB = 256


@functools.lru_cache(maxsize=None)
def _band_bias():
    span = BAND_SUB + 2 * WINDOW
    kr = np.arange(span)[None, :, None]
    qc = (np.arange(GROUP * BAND_SUB) % BAND_SUB)[None, None, :]
    rel = np.arange(3)[:, None, None]
    return np.where(np.abs(kr - qc - rel * WINDOW) <= WINDOW, 0.0, NEG_INF).astype(np.float32)


def _banded_kernel(sink_ref, q_ref, k_ref, vt_ref, kc_ref, vct_ref, bias_ref, o_ref,
                   sl_ref, sc_ref, pl_ref, pc_ref, acc_ref, *, n_sub, s_len):
    i = pl.program_id(2)
    sub = BAND_SUB
    m_cols = GROUP * sub
    span = sub + 2 * WINDOW
    n_ctx = kc_ref.shape[2]
    stats = []
    for u in range(n_sub):
        q = q_ref[0, :, u * sub:(u + 1) * sub, :].reshape(m_cols, HEAD_DIM)
        q0 = (i * n_sub + u) * sub
        start = pl.multiple_of(jnp.clip(q0 - WINDOW, 0, s_len - span), WINDOW)
        mx_l = _score_stage(k_ref[0, 0, pl.ds(start, span), :], q, sl_ref.at[u], bias=bias_ref[(q0 - start) // WINDOW])
        mx_c = _score_stage(kc_ref[0, 0], q, sc_ref.at[u])
        stats.append((start, mx_l, mx_c))
    for u in range(n_sub):
        start, mx_l, mx_c = stats[u]
        acc_u = acc_ref.at[u]
        acc_u[...] = jnp.zeros((LANES, m_cols), F32)
        m = jnp.full((1, m_cols), NEG_INF, F32)
        m = _softmax_stage(sl_ref.at[u], pl_ref.at[u], span, vt_ref[0, 0, :, pl.ds(start, span)], mx_l, m, acc_u)
        m = _softmax_stage(sc_ref.at[u], pc_ref.at[u], n_ctx, vct_ref[0, 0], mx_c, m, acc_u)
        acc = _sink_column(sink_ref, pl.program_id(1), sub, m, acc_u[...])
        o_ref[0, u * sub:(u + 1) * sub] = _attn_finish(acc, sub).astype(o_ref.dtype)


def _banded_attention(q, k, vt, kc, vct, sink):
    bx, _, t, _ = q.shape
    n_ctx = kc.shape[2]
    n_sub = 2
    tq = n_sub * BAND_SUB
    m_cols = GROUP * BAND_SUB
    span = BAND_SUB + 2 * WINDOW
    bias = jnp.asarray(_band_bias())
    return pl.pallas_call(
        functools.partial(_banded_kernel, n_sub=n_sub, s_len=t),
        grid=(bx, N_KV_HEADS, t // tq),
        in_specs=[pl.BlockSpec(memory_space=pltpu.SMEM),
                  pl.BlockSpec((1, GROUP, tq, HEAD_DIM), lambda b, h, i: (b, h, i, 0))]
                 + _kv_specs(t) + _kv_specs(n_ctx) + [_full(bias.shape)],
        out_specs=pl.BlockSpec((1, tq, LANES), lambda b, h, i: (b, i, h)),
        out_shape=jax.ShapeDtypeStruct((bx, t, Q_W), BF16),
        scratch_shapes=[pltpu.VMEM((n_sub, span, m_cols), F32), pltpu.VMEM((n_sub, n_ctx, m_cols), F32),
                        pltpu.VMEM((n_sub, span, m_cols), BF16), pltpu.VMEM((n_sub, n_ctx, m_cols), BF16),
                        pltpu.VMEM((n_sub, LANES, m_cols), F32)],
        compiler_params=_params(("arbitrary", "arbitrary", "arbitrary")),
        name="banded_attention",
    )(sink, q, k, vt, kc, vct, bias)


@functools.lru_cache(maxsize=None)
def _filter_features(n):
    j = np.arange(2 * n)
    d = np.where(j <= n, j, 2 * n - j)
    d = np.where(j == n, 0, d)
    bands = (FILT_EMB - 1) // 2
    t01 = np.linspace(0.0, 1.0, n)[d]
    w = 2.0 * np.pi * d.astype(np.float64) / n
    f = np.linspace(1e-4, bands - 1, bands)[None, :]
    feats = np.zeros((2 * n, LANES), np.float64)
    feats[:, 0] = t01
    feats[:, 1:1 + bands] = np.cos(f * w[:, None])
    feats[:, 1 + bands:FILT_EMB] = -np.sin(f * w[:, None])
    feats[:, 64] = t01
    feats[:, 65] = (j < n)
    feats[:, 66] = (j != n)
    return feats.astype(np.float32)


def _filter_kernel(f_ref, w1_ref, b1_ref, w2_ref, b2_ref, w3_ref, b3_ref, w4_ref, fr_ref, dl_ref, k_ref, s_ref):
    f = f_ref[...]
    fr = fr_ref[...]
    mm = lambda a, b: jnp.dot(a, b, preferred_element_type=F32, precision=HI)
    h = jnp.sin(fr * (mm(f, w1_ref[...]) + b1_ref[...]))
    h = jnp.sin(fr * (mm(h, w2_ref[...]) + b2_ref[...]))
    h = jnp.sin(fr * (mm(h, w3_ref[...]) + b3_ref[...]))
    hf = mm(h, w4_ref[...])
    win = jnp.exp(-f[:, 64:65] * dl_ref[...]) + DECAY_SHIFT
    k = jnp.where(f[:, 65:66] > 0.5, hf[:, :CONV_W], hf[:, CONV_W:]) * win * f[:, 66:67]
    k_ref[...] = k

    @pl.when(pl.program_id(0) == 0)
    def _():
        s_ref[...] = jnp.zeros_like(s_ref)

    s_ref[...] += jnp.sum(jnp.abs(k), axis=0, keepdims=True)


def _implicit_filter(n, w1, b1, w2, b2, w3, b3, w4, freq):
    feats = jnp.asarray(_filter_features(n))
    pad = LANES - FILT_WIDTH
    padc = lambda a: jnp.pad(a.reshape(1, -1), ((0, 0), (0, pad)))
    w1p = jnp.pad(w1, ((0, LANES - FILT_EMB), (0, pad)))
    w2p = jnp.pad(w2, ((0, pad), (0, pad)))
    w3p = jnp.pad(w3, ((0, pad), (0, pad)))
    w4p = jnp.pad(w4, ((0, pad), (0, 0)))
    deltas = np.abs(np.linspace(math.log(DECAY_TARGET) / SLOW_DECAY_PCT, math.log(DECAY_TARGET) / FAST_DECAY_PCT,
                                CONV_W)).astype(np.float32).reshape(1, CONV_W)
    tr = min(1024, 2 * n)
    sq = _full((LANES, LANES))
    vec = _full((1, LANES))
    return pl.pallas_call(
        _filter_kernel,
        grid=(2 * n // tr,),
        in_specs=[pl.BlockSpec((tr, LANES), lambda i: (i, 0)), sq, vec, sq, vec, sq, vec,
                  _full((LANES, 2 * CONV_W)), vec, _full((1, CONV_W))],
        out_specs=[pl.BlockSpec((tr, CONV_W), lambda i: (i, 0)), _full((1, CONV_W))],
        out_shape=[jax.ShapeDtypeStruct((2 * n, CONV_W), F32), jax.ShapeDtypeStruct((1, CONV_W), F32)],
        compiler_params=_params(("arbitrary",)),
        name="hyena_filter",
    )(feats, w1p, padc(b1), w2p, padc(b2), w3p, padc(b3), w4p, padc(freq), jnp.asarray(deltas))


def _twiddle(idx, mod):
    ang = 2.0 * np.pi * (idx % mod) / mod
    return np.cos(ang), -np.sin(ang)


def _real_form(mr, mi):
    return np.concatenate([np.concatenate([mr, -mi], -1), np.concatenate([mi, mr], -1)], -2)


@functools.lru_cache(maxsize=None)
def _dft_tables(n2):
    n1 = DFT_N1
    n = n1 * n2
    h = n2 // 2
    a2 = np.arange(n2)
    fr, fi = _twiddle(np.outer(a2, a2), n2)
    m_data = _real_form(fr[:, :h], fi[:, :h])
    m_filt = np.concatenate([fr, fi], 0)
    k2 = a2[:, None, None]
    k1 = np.arange(n1)[None, :, None]
    c1 = np.arange(n1)[None, None, :]
    g = _real_form(*_twiddle(c1 * (n2 * k1 + k2), n))
    a1 = np.arange(n1)
    f1 = _real_form(*_twiddle(np.outer(a1, a1), n1))
    t2 = a1[:, None, None]
    t1 = np.arange(h)[None, :, None]
    j1 = a2[None, None, :]
    hh = _real_form(*_twiddle(j1 * (n1 * t1 + t2), n))
    cast = lambda m: np.asarray(m, dtype=BF16)
    return cast(m_data), cast(m_filt), cast(g), cast(f1), cast(hh)


DFT_ROW_CHUNK = SUBLANES


def _dft_rows_kernel(m_ref, *refs):
    o_ref = refs[-1]
    cols = [jnp.concatenate([r[0, :, t, :] for r in refs[:-1]], axis=0) for t in range(DFT_ROW_CHUNK)]
    rhs = jnp.concatenate(cols, axis=1).astype(BF16)
    out = jnp.dot(m_ref[...], rhs, preferred_element_type=F32)
    half = out.shape[0] // 2
    o_ref[0, 0] = out[:half].astype(o_ref.dtype)
    o_ref[0, 1] = out[half:].astype(o_ref.dtype)


def _dft_stage1(mat, views, pairs, rows, n2, c):
    nin = len(views)
    lc = DFT_ROW_CHUNK * c
    in_specs = [_full(mat.shape)] + [
        pl.BlockSpec((1, rows, DFT_ROW_CHUNK, c), (lambda p, j, a=a: (nin * p + a, 0, j, 0))) for a in range(nin)]
    return pl.pallas_call(
        _dft_rows_kernel,
        grid=(pairs, DFT_N1 // DFT_ROW_CHUNK),
        in_specs=in_specs,
        out_specs=pl.BlockSpec((1, 2, n2, lc), lambda p, j: (p, 0, 0, j)),
        out_shape=jax.ShapeDtypeStruct((pairs, 2, n2, DFT_N1 * c), BF16),
        compiler_params=_params(("arbitrary", "arbitrary")),
        name="dft_stage1",
    )(mat, *views)


def _spectrum_kernel(a_ref, g_ref, sc_ref, o_ref, *, kb):
    for j in range(kb):
        rhs = jnp.concatenate([a_ref[0, 0, j], a_ref[0, 1, j]], axis=0)
        x = jnp.dot(g_ref[j], rhs, preferred_element_type=F32)
        o_ref[0, j] = x[:DFT_N1] * sc_ref[...]
        o_ref[1, j] = x[DFT_N1:] * sc_ref[...]


def _filter_spectrum(a, g, scale, n2, kb):
    c = a.shape[-1]
    return pl.pallas_call(
        functools.partial(_spectrum_kernel, kb=kb),
        grid=(n2 // kb,),
        in_specs=[pl.BlockSpec((1, 2, kb, DFT_N1, c), lambda k: (0, 0, k, 0, 0)),
                  pl.BlockSpec((kb, 2 * DFT_N1, 2 * DFT_N1), lambda k: (k, 0, 0)), _full((1, c))],
        out_specs=pl.BlockSpec((2, kb, DFT_N1, c), lambda k: (0, k, 0, 0)),
        out_shape=jax.ShapeDtypeStruct((2, n2, DFT_N1, c), F32),
        compiler_params=_params(("arbitrary",)),
        name="filter_spectrum",
    )(a, g, scale)


def _dft_mid_kernel(a_ref, g_ref, kh_ref, f_ref, o_ref, *, kb):
    for j in range(kb):
        rhs = jnp.concatenate([a_ref[0, 0, j], a_ref[0, 1, j]], axis=0)
        x = jnp.dot(g_ref[j], rhs, preferred_element_type=F32)
        xr, xi = x[:DFT_N1], x[DFT_N1:]
        kr, ki = kh_ref[0, j], kh_ref[1, j]
        yr = xr * kr - xi * ki
        yi = xr * ki + xi * kr
        v = jnp.concatenate([yr, -yi], axis=0).astype(BF16)
        b = jnp.dot(f_ref[...], v, preferred_element_type=F32)
        o_ref[0, 0, j] = b[:DFT_N1].astype(o_ref.dtype)
        o_ref[0, 1, j] = b[DFT_N1:].astype(o_ref.dtype)


def _dft_mid(a, g, khat, f1, pairs, n2, kb):
    c = a.shape[-1]
    blk = pl.BlockSpec((1, 2, kb, DFT_N1, c), lambda k, p: (p, 0, k, 0, 0))
    return pl.pallas_call(
        functools.partial(_dft_mid_kernel, kb=kb),
        grid=(n2 // kb, pairs),
        in_specs=[blk, pl.BlockSpec((kb, 2 * DFT_N1, 2 * DFT_N1), lambda k, p: (k, 0, 0)),
                  pl.BlockSpec((2, kb, DFT_N1, c), lambda k, p: (0, k, 0, 0)), _full(f1.shape)],
        out_specs=blk,
        out_shape=jax.ShapeDtypeStruct(a.shape, BF16),
        compiler_params=_params(("arbitrary", "arbitrary")),
        name="dft_mid",
    )(a, g, khat, f1)


def _dft_last_kernel(b_ref, h_ref, u_ref, x0_ref, bd_ref, o_ref, *, c):
    half = h_ref.shape[1] // 2
    bd = bd_ref[...]
    for t in range(DFT_ROW_CHUNK):
        cols = slice(t * c, (t + 1) * c)
        rhs = jnp.concatenate([b_ref[0, 0, :, cols], b_ref[0, 1, :, cols]], axis=0)
        v = jnp.dot(h_ref[t], rhs, preferred_element_type=F32)
        o_ref[0, :, t, :] = (v[:half] + u_ref[0, :, t, :] * bd) * x0_ref[0, :, t, :]
        o_ref[1, :, t, :] = (-v[half:] + u_ref[1, :, t, :] * bd) * x0_ref[1, :, t, :]


def _dft_last(bm, hh, u_view, x0_view, bias_d, pairs, n2, c):
    half = n2 // 2
    tc = DFT_ROW_CHUNK
    io = pl.BlockSpec((2, half, tc, c), lambda p, j: (p, 0, j, 0))
    return pl.pallas_call(
        functools.partial(_dft_last_kernel, c=c),
        grid=(pairs, DFT_N1 // tc),
        in_specs=[pl.BlockSpec((1, 2, n2, tc * c), lambda p, j: (p, 0, 0, j)),
                  pl.BlockSpec((tc, n2, 2 * n2), lambda p, j: (j, 0, 0)), io, io, _full((1, c))],
        out_specs=io,
        out_shape=jax.ShapeDtypeStruct((2 * pairs, half, DFT_N1, c), F32),
        compiler_params=_params(("arbitrary", "arbitrary")),
        name="dft_last",
    )(bm, hh, u_view, x0_view, bias_d)


def _long_conv_mixer(u, x0, kfilt, ksum, bias_d):
    b, n, c = u.shape
    n2 = 2 * n // DFT_N1
    half = n2 // 2
    pairs = b // 2
    m_data, m_filt, g, f1, hh = (jnp.asarray(t) for t in _dft_tables(n2))
    kb = min(8, n2)
    scale = 1.0 / (ksum * float(DFT_N1 * n2))
    ka = _dft_stage1(m_filt, [kfilt.reshape(1, n2, DFT_N1, c)], 1, n2, n2, c)
    khat = _filter_spectrum(ka.reshape(1, 2, n2, DFT_N1, c), g, scale, n2, kb)
    u_view = u.reshape(b, half, DFT_N1, c)
    a = _dft_stage1(m_data, [u_view, u_view], pairs, half, n2, c)
    bm = _dft_mid(a.reshape(pairs, 2, n2, DFT_N1, c), g, khat, f1, pairs, n2, kb)
    out = _dft_last(bm.reshape(pairs, 2, n2, DFT_N1 * c), hh, u_view, x0.reshape(b, half, DFT_N1, c),
                    bias_d.reshape(1, c), pairs, n2, c)
    return out.reshape(b, n, c)


@functools.lru_cache(maxsize=None)
def _small_dft_tables(n):
    big = 2 * n
    a = np.arange(big)
    fr, fi = _twiddle(np.outer(a, a), big)
    cast = lambda m: np.asarray(m, dtype=BF16)
    return (cast(np.concatenate([fr, fi], 0)),
            cast(_real_form(fr[:, :n], fi[:, :n])),
            cast(_real_form(fr[:n], fi[:n])))


def _small_conv_kernel(k_ref, ks_ref, u_ref, x0_ref, bd_ref, mf_ref, md_ref, mi_ref, o_ref, *, n):
    big = 2 * n
    kh = jnp.dot(mf_ref[...], k_ref[...].astype(BF16), preferred_element_type=F32) * (1.0 / (ks_ref[...] * big))
    kr, ki = kh[:big], kh[big:]
    rhs = jnp.concatenate([u_ref[0], u_ref[1]], axis=0).astype(BF16)
    x = jnp.dot(md_ref[...], rhs, preferred_element_type=F32)
    xr, xi = x[:big], x[big:]
    v = jnp.concatenate([xr * kr - xi * ki, -(xr * ki + xi * kr)], axis=0).astype(BF16)
    y = jnp.dot(mi_ref[...], v, preferred_element_type=F32)
    bd = bd_ref[...]
    o_ref[0] = ((y[:n] + u_ref[0] * bd) * x0_ref[0]).astype(o_ref.dtype)
    o_ref[1] = ((-y[n:] + u_ref[1] * bd) * x0_ref[1]).astype(o_ref.dtype)


def _small_conv_mixer(u, x0, kfilt, ksum, bias_d):
    b, n, c = u.shape
    mf, md, mi = (jnp.asarray(t) for t in _small_dft_tables(n))
    io = pl.BlockSpec((2, n, c), lambda p: (p, 0, 0))
    return pl.pallas_call(
        functools.partial(_small_conv_kernel, n=n),
        grid=(b // 2,),
        in_specs=[_full((2 * n, c)), _full((1, c)), io, io, _full((1, c)),
                  _full(mf.shape), _full(md.shape), _full(mi.shape)],
        out_specs=io,
        out_shape=jax.ShapeDtypeStruct((b, n, c), BF16),
        compiler_params=_params(("arbitrary",)),
        name="small_conv",
    )(kfilt, ksum, u, x0, bias_d.reshape(1, c), mf, md, mi)


def _outproj_kernel(x_ref, oa_ref, oc_ref, wa_ref, wc_ref, g_ref, o_ref):
    sub = min(PROJ_SUB_ROWS, x_ref.shape[1])
    for r0 in range(0, x_ref.shape[1], sub):
        rows = slice(r0, r0 + sub)
        y = (jnp.dot(oa_ref[0, rows], wa_ref[...], preferred_element_type=F32)
             + jnp.dot(oc_ref[0, rows].astype(BF16), wc_ref[...], preferred_element_type=F32))
        o_ref[0, rows] = x_ref[0, rows] + g_ref[0] * y


def _outproj(x, oa, oc, w_bf, gate):
    bx, t, d = x.shape
    tm = min(512, t)
    row = lambda w: pl.BlockSpec((1, tm, w), lambda b, i: (b, i, 0))
    return pl.pallas_call(
        _outproj_kernel,
        grid=(bx, t // tm),
        in_specs=[row(d), row(Q_W), row(CONV_W), pl.BlockSpec((Q_W, d), lambda b, i: (0, 0)),
                  pl.BlockSpec((CONV_W, d), lambda b, i: (1, 0)), pl.BlockSpec((1, 1, d), lambda b, i: (b, 0, 0))],
        out_specs=row(d),
        out_shape=jax.ShapeDtypeStruct((bx, t, d), F32),
        compiler_params=_params(("arbitrary", "arbitrary")),
        name="out_proj",
    )(x, oa, oc, w_bf, w_bf, gate)


FFN_CHUNK = D_FF // 2


def _gelu_tanh(x):
    return 0.5 * x * (1.0 + jnp.tanh(math.sqrt(2.0 / math.pi) * (x + 0.044715 * (x * x * x))))


def _ffn_kernel(x_ref, xp_ref, xn_ref, sh_ref, sc_ref, gt_ref, g_ref, wu_ref, cw_ref, cb_ref, wd_ref, o_ref):
    i = pl.program_id(1)
    nt = pl.num_programs(1)
    xm = x_ref[0]
    tm = xm.shape[0]
    ext = jnp.concatenate([xp_ref[0], xm, xn_ref[0]], axis=0)
    r = tm + 2 * SUBLANES
    h = _modulated_norm(ext, g_ref[...], sh_ref[0], sc_ref[0])
    row = lax.broadcasted_iota(jnp.int32, (r, 1), 0)
    inside = jnp.logical_and(jnp.logical_or(i > 0, row >= SUBLANES), jnp.logical_or(i < nt - 1, row < tm + SUBLANES))
    h = jnp.where(inside, h, 0.0).astype(BF16)
    hm = h[SUBLANES:tm + SUBLANES]
    acc = jnp.zeros((tm, D_MODEL), F32)
    for c0 in range(0, D_FF, FFN_CHUNK):
        a = jnp.dot(h, wu_ref[:, c0:c0 + FFN_CHUNK], preferred_element_type=F32)
        v = jnp.dot(hm, wu_ref[:, D_FF + c0:D_FF + c0 + FFN_CHUNK], preferred_element_type=F32)
        ap = pltpu.roll(a, 1, 0)[SUBLANES:tm + SUBLANES]
        an = pltpu.roll(a, r - 1, 0)[SUBLANES:tm + SUBLANES]
        cw = cw_ref[:, c0:c0 + FFN_CHUNK]
        conv = ap * cw[0:1] + a[SUBLANES:tm + SUBLANES] * cw[1:2] + an * cw[2:3] + cb_ref[:, c0:c0 + FFN_CHUNK]
        act = (_gelu_tanh(conv) * v).astype(BF16)
        acc = acc + jnp.dot(act, wd_ref[c0:c0 + FFN_CHUNK, :], preferred_element_type=F32)
    o_ref[0] = xm + gt_ref[0] * acc


def _ffn(x, sh, sc, gate, gain, wu_bf, conv_w, conv_b, wd_bf):
    bx, t, d = x.shape
    tm = min(512, t)
    vec = pl.BlockSpec((1, 1, d), lambda b, i: (b, 0, 0))
    once = pl.Buffered(1)
    return pl.pallas_call(
        _ffn_kernel,
        grid=(bx, t // tm),
        in_specs=_halo_specs(tm, t, d) + [
            vec, vec, vec, _full((1, d)),
            pl.BlockSpec((d, 2 * D_FF), lambda b, i: (0, 0), pipeline_mode=once),
            _full((3, D_FF)), _full((1, D_FF)),
            pl.BlockSpec((D_FF, d), lambda b, i: (0, 0), pipeline_mode=once)],
        out_specs=pl.BlockSpec((1, tm, d), lambda b, i: (b, i, 0)),
        out_shape=jax.ShapeDtypeStruct((bx, t, d), F32),
        compiler_params=_params(("arbitrary", "arbitrary")),
        name="conv_ffn",
    )(x, x, x, sh, sc, gate, gain, wu_bf, conv_w, conv_b.reshape(1, D_FF), wd_bf)


@functools.lru_cache(maxsize=None)
def _rope_tables(t):
    pos = np.arange(t)
    n_freq = HEAD_DIM // 4
    inv = ROPE_THETA ** (-np.arange(n_freq, dtype=np.float64) / n_freq)
    ang_r = (pos // GRID_W)[:, None] * inv
    ang_c = (pos % GRID_W)[:, None] * inv
    zero = np.zeros_like(ang_r)
    cos = np.concatenate([np.cos(ang_r)] * 2 + [np.cos(ang_c)] * 2, 1)
    sa = np.concatenate([-np.sin(ang_r), zero, -np.sin(ang_c), zero], 1)
    sb = np.concatenate([zero, np.sin(ang_r), zero, np.sin(ang_c)], 1)
    tile = lambda m: np.tile(m, (1, LANES // HEAD_DIM)).astype(np.float32)
    return tile(cos), tile(sa), tile(sb)


@functools.lru_cache(maxsize=None)
def _identity_rope_tables(t):
    return np.ones((t, LANES), np.float32), np.zeros((t, LANES), np.float32), np.zeros((t, LANES), np.float32)


@functools.lru_cache(maxsize=None)
def _head_block_diag():
    hid = np.arange(Q_W) // HEAD_DIM
    return np.asarray(hid[:, None] == hid[None, :], dtype=BF16)


def kernel(x, c, ctx, c_ctx, ada_w, ada_b, norm_mix, norm_ffn, mix_w_in, mix_w_out, attn_q_norm, attn_k_norm,
           swa_sink, hy_conv_w, hy_conv_b, hy_w1, hy_b1, hy_w2, hy_b2, hy_w3, hy_b3, hy_w4, hy_freq, hy_bias_d,
           sc_conv_w, ffn_w_up, ffn_conv_w, ffn_conv_b, ffn_w_down):
    b, s, d = x.shape
    s_ctx = ctx.shape[1]
    depth = ada_w.shape[0]
    assert d == D_MODEL and b % 2 == 0 and b + 1 <= SUBLANES and s % 1024 == 0 and s_ctx % SUBLANES == 0

    cvec = jnp.concatenate([c, c_ctx[None, :], jnp.zeros((SUBLANES - b - 1, d), F32)], axis=0)
    mods = _mods(cvec, ada_w, ada_b)
    rope = [jnp.asarray(t) for t in _rope_tables(s)]
    rope_ctx = [jnp.asarray(t) for t in _identity_rope_tables(s_ctx)]
    bd = jnp.asarray(_head_block_diag())
    xc = ctx

    for i in range(depth):
        last = i == depth - 1
        j = i // 2
        lat = [mods[i, :b, k * d:(k + 1) * d][:, None, :] for k in range(6)]
        cx = [jnp.broadcast_to(mods[i, b, k * d:(k + 1) * d][None, None, :], (b, 1, d)) for k in range(6)]
        w_in = mix_w_in[i].astype(BF16)
        w_out = mix_w_out[i].astype(BF16)
        w_up = ffn_w_up[i].astype(BF16)
        w_down = ffn_w_down[i].astype(BF16)
        g_mix = norm_mix[i].reshape(1, d)
        g_ffn = norm_ffn[i].reshape(1, d)
        qg = jnp.tile(attn_q_norm[i], N_Q_HEADS).reshape(1, Q_W)
        kg = jnp.tile(attn_k_norm[i], N_KV_HEADS).reshape(1, KV_W)

        conv = (hy_conv_w[j], hy_conv_b[j]) if i % 2 == 0 else (sc_conv_w[j],)
        q, k, vt, *mix = _inproj(x, lat[0], lat[1], g_mix, w_in, qg, kg, bd, *rope, *conv)
        qc, kc, vct, *mixc = _inproj(xc, cx[0], cx[1], g_mix, w_in, qg, kg, bd, *rope_ctx, *conv)

        if i % 2 == 0:
            bound = _score_bound(attn_q_norm[i], attn_k_norm[i])
            fargs = (hy_w1[j], hy_b1[j], hy_w2[j], hy_b2[j], hy_w3[j], hy_b3[j], hy_w4[j], hy_freq[j])
            o_attn = _dense_attention(q, k, vt, kc, vct, bound=bound)
            kf, ks = _implicit_filter(s, *fargs)
            o_conv = _long_conv_mixer(*mix, kf, ks, hy_bias_d[j])
            if not last:
                oc_attn = _dense_attention(qc, kc, vct, bound=bound)
                kfc, ksc = _implicit_filter(s_ctx, *fargs)
                oc_conv = _small_conv_mixer(*mixc, kfc, ksc, hy_bias_d[j])
        else:
            sink = swa_sink[j].reshape(N_KV_HEADS, GROUP)
            o_attn = _banded_attention(q, k, vt, kc, vct, sink)
            o_conv = mix[0]
            if not last:
                oc_attn = _dense_attention(qc, kc, vct, sink=sink)
                oc_conv = mixc[0]

        x = _outproj(x, o_attn, o_conv, w_out, lat[2])
        x = _ffn(x, lat[3], lat[4], lat[5], g_ffn, w_up, ffn_conv_w[i], ffn_conv_b[i], w_down)
        if not last:
            xc = _outproj(xc, oc_attn, oc_conv, w_out, cx[2])
            xc = _ffn(xc, cx[3], cx[4], cx[5], g_ffn, w_up, ffn_conv_w[i], ffn_conv_b[i], w_down)
    return x
```

```python
import functools
import math

import numpy as np
import jax
import jax.numpy as jnp
from jax import lax
from jax.experimental import pallas as pl
from jax.experimental.pallas import tpu as pltpu

F32 = jnp.float32
BF16 = jnp.bfloat16
HI = lax.Precision.HIGHEST

D_MODEL = 1024
GRID_W = 64
HEAD_DIM = 64
N_Q_HEADS = 8
N_KV_HEADS = 4
GROUP = N_Q_HEADS // N_KV_HEADS
Q_W = N_Q_HEADS * HEAD_DIM
KV_W = N_KV_HEADS * HEAD_DIM
QKV_W = Q_W + 2 * KV_W
CONV_W = D_MODEL // 2
MIX_IN_W = QKV_W + 3 * CONV_W
WINDOW = 128
ROPE_THETA = 10000.0
FILT_EMB = 33
FILT_WIDTH = 64
DECAY_TARGET = 1e-2
FAST_DECAY_PCT = 0.3
SLOW_DECAY_PCT = 1.5
DECAY_SHIFT = 0.05
D_FF = 2816
NEG_INF = -1e30
RMS_EPS = 1e-6
LOG2E = 1.4426950408889634
Q_SCALE = HEAD_DIM ** -0.5 * LOG2E

LANES = 128
SUBLANES = 8
DFT_N1 = 128
VMEM_LIMIT_MB = 56
PROJ_SUB_ROWS = 128


def _params(sem, vmem_mb=VMEM_LIMIT_MB):
    return pltpu.CompilerParams(dimension_semantics=sem, vmem_limit_bytes=vmem_mb * 1024 * 1024)


def _full(shape):
    nd = len(shape)
    return pl.BlockSpec(shape, lambda *_: (0,) * nd)


def _mods_kernel(c_ref, w_ref, b_ref, o_ref):
    c = c_ref[...]
    s = c / (1.0 + jnp.exp(-c))
    o_ref[0] = jnp.dot(s, w_ref[0], preferred_element_type=F32, precision=HI) + b_ref[0]


def _mods(cvec, ada_w, ada_b):
    depth, d, n6 = ada_w.shape
    tn = 1536
    return pl.pallas_call(
        _mods_kernel,
        grid=(depth, n6 // tn),
        in_specs=[_full((SUBLANES, d)),
                  pl.BlockSpec((1, d, tn), lambda l, j: (l, 0, j)),
                  pl.BlockSpec((1, 1, tn), lambda l, j: (l, 0, j))],
        out_specs=pl.BlockSpec((1, SUBLANES, tn), lambda l, j: (l, 0, j)),
        out_shape=jax.ShapeDtypeStruct((depth, SUBLANES, n6), F32),
        compiler_params=_params(("arbitrary", "arbitrary")),
        name="ada_mods",
    )(cvec, ada_w, ada_b.reshape(depth, 1, n6))


def _modulated_norm(x, gain, shift, scale):
    ms = jnp.mean(x * x, axis=-1, keepdims=True)
    return (x * lax.rsqrt(ms + RMS_EPS)) * gain * (1.0 + scale) + shift


def _halo_specs(tm, t, width, dtype=F32):
    rows = SUBLANES * 4 // jnp.dtype(dtype).itemsize
    nb = tm // rows
    last = t // rows - 1
    return [pl.BlockSpec((1, tm, width), lambda b, i: (b, i, 0)),
            pl.BlockSpec((1, rows, width), lambda b, i: (b, jnp.maximum(i * nb - 1, 0), 0)),
            pl.BlockSpec((1, rows, width), lambda b, i: (b, jnp.minimum((i + 1) * nb, last), 0))]


def _inproj_kernel(*refs, mode):
    (x_ref, xp_ref, xn_ref, sh_ref, sc_ref, g_ref, w_ref, qg_ref, kg_ref, bd_ref,
     cos_ref, sa_ref, sb_ref, cw_ref) = refs[:14]
    if mode == "hyena":
        cb_ref, q_ref, k_ref, vt_ref, u_ref, x0_ref, zc_ref = refs[14:]
    else:
        q_ref, k_ref, vt_ref, oc_ref, zc_ref = refs[14:]
    i = pl.program_id(1)
    nt = pl.num_programs(1)
    tm = x_ref.shape[1]
    sub = min(PROJ_SUB_ROWS, tm)
    gain, shift, scale = g_ref[...], sh_ref[0], sc_ref[0]

    xh = jnp.concatenate([xp_ref[0], xn_ref[0]], axis=0)
    zh = jnp.dot(_modulated_norm(xh, gain, shift, scale).astype(BF16), w_ref[:, QKV_W:], preferred_element_type=F32)
    zc_ref[0:SUBLANES] = jnp.where(i > 0, zh[:SUBLANES], 0.0)
    zc_ref[tm + SUBLANES:tm + 2 * SUBLANES] = jnp.where(i < nt - 1, zh[SUBLANES:], 0.0)

    def head_norm_rope(t, hgain, bd, cos, sa, sb):
        ssq = jnp.dot((t * t).astype(BF16), bd, preferred_element_type=F32)
        t = t * lax.rsqrt(ssq * (1.0 / HEAD_DIM) + RMS_EPS) * hgain
        outs = []
        for j in range(t.shape[1] // LANES):
            tj = t[:, j * LANES:(j + 1) * LANES]
            outs.append(tj * cos + pltpu.roll(tj, LANES - 16, 1) * sa + pltpu.roll(tj, 16, 1) * sb)
        return outs

    for r0 in range(0, tm, sub):
        rows = slice(r0, r0 + sub)
        h = _modulated_norm(x_ref[0, rows], gain, shift, scale)
        z = jnp.dot(h.astype(BF16), w_ref[...], preferred_element_type=F32)
        zc_ref[SUBLANES + r0:SUBLANES + r0 + sub] = z[:, QKV_W:]
        tabs = (cos_ref[rows], sa_ref[rows], sb_ref[rows])
        qs = head_norm_rope(z[:, :Q_W], qg_ref[...], bd_ref[...], *tabs)
        ks = head_norm_rope(z[:, Q_W:Q_W + KV_W], kg_ref[...], bd_ref[:KV_W, :KV_W], *tabs)
        for j, qj in enumerate(qs):
            qj = (qj * Q_SCALE).astype(BF16)
            q_ref[0, 2 * j, rows] = qj[:, :HEAD_DIM]
            q_ref[0, 2 * j + 1, rows] = qj[:, HEAD_DIM:]
        for j, kj in enumerate(ks):
            kj = kj.astype(BF16)
            k_ref[0, 2 * j, rows] = kj[:, :HEAD_DIM]
            k_ref[0, 2 * j + 1, rows] = kj[:, HEAD_DIM:]
        vt = z[:, Q_W + KV_W:QKV_W].T
        ones_row = (lax.broadcasted_iota(jnp.int32, (LANES - HEAD_DIM, sub), 0) == 0).astype(F32)
        for hh in range(N_KV_HEADS):
            vt_ref[0, hh, :, rows] = jnp.concatenate([vt[hh * HEAD_DIM:(hh + 1) * HEAD_DIM], ones_row],
                                                     axis=0).astype(BF16)

    cw = cw_ref[...]
    for r0 in range(0, tm, sub):
        rows = slice(r0, r0 + sub)
        ext = zc_ref[r0:r0 + sub + 2 * SUBLANES]
        n = sub + 2 * SUBLANES
        mid = slice(SUBLANES, sub + SUBLANES)
        if mode == "hyena":
            c = (pltpu.roll(ext, 1, 0)[mid] * cw[0:1] + ext[mid] * cw[1:2] + pltpu.roll(ext, n - 1, 0)[mid] * cw[2:3]
                 + cb_ref[...])
            x0_ref[0, rows] = c[:, :CONV_W]
            u_ref[0, rows] = c[:, 2 * CONV_W:] * c[:, CONV_W:2 * CONV_W]
        else:
            pr = ext[:, CONV_W:2 * CONV_W] * ext[:, 2 * CONV_W:]
            conv = pltpu.roll(pr, 1, 0)[mid] * cw[0:1] + pr[mid] * cw[1:2] + pltpu.roll(pr, n - 1, 0)[mid] * cw[2:3]
            oc_ref[0, rows] = (ext[mid, :CONV_W] * conv).astype(oc_ref.dtype)


def _inproj(x, sh, sc, gain, w_bf, qg, kg, bd, cos, sa, sb, conv_w, conv_b=None):
    bx, t, d = x.shape
    tm = min(512, t)
    mode = "hyena" if conv_b is not None else "short"
    vec = pl.BlockSpec((1, 1, d), lambda b, i: (b, 0, 0))
    tab = pl.BlockSpec((tm, LANES), lambda b, i: (i, 0))
    row = lambda w: pl.BlockSpec((1, tm, w), lambda b, i: (b, i, 0))
    in_specs = _halo_specs(tm, t, d) + [vec, vec, _full((1, d)), _full((d, MIX_IN_W)), _full((1, Q_W)),
                                        _full((1, KV_W)), _full((Q_W, Q_W)), tab, tab, tab, _full(conv_w.shape)]
    args = [x, x, x, sh, sc, gain, w_bf, qg, kg, bd, cos, sa, sb, conv_w]
    out_specs = [pl.BlockSpec((1, N_Q_HEADS, tm, HEAD_DIM), lambda b, i: (b, 0, i, 0)),
                 pl.BlockSpec((1, N_KV_HEADS, tm, HEAD_DIM), lambda b, i: (b, 0, i, 0)),
                 pl.BlockSpec((1, N_KV_HEADS, LANES, tm), lambda b, i: (b, 0, 0, i))]
    out_shape = [jax.ShapeDtypeStruct((bx, N_Q_HEADS, t, HEAD_DIM), BF16),
                 jax.ShapeDtypeStruct((bx, N_KV_HEADS, t, HEAD_DIM), BF16),
                 jax.ShapeDtypeStruct((bx, N_KV_HEADS, LANES, t), BF16)]
    if mode == "hyena":
        in_specs.append(_full((1, 3 * CONV_W)))
        args.append(conv_b.reshape(1, 3 * CONV_W))
        out_specs += [row(CONV_W), row(CONV_W)]
        out_shape += [jax.ShapeDtypeStruct((bx, t, CONV_W), F32)] * 2
    else:
        out_specs.append(row(CONV_W))
        out_shape.append(jax.ShapeDtypeStruct((bx, t, CONV_W), BF16))
    return pl.pallas_call(
        functools.partial(_inproj_kernel, mode=mode),
        grid=(bx, t // tm),
        in_specs=in_specs,
        out_specs=out_specs,
        out_shape=out_shape,
        scratch_shapes=[pltpu.VMEM((tm + 2 * SUBLANES, 3 * CONV_W), F32)],
        compiler_params=_params(("arbitrary", "arbitrary")),
        name="in_proj",
    )(*args)


def _score_bound(q_gain, k_gain):
    b = HEAD_DIM * Q_SCALE * BOUND_MARGIN * jnp.max(jnp.abs(q_gain)) * jnp.max(jnp.abs(k_gain))
    return b.reshape(1, 1).astype(F32)


def _scores_t(ks, q):
    return lax.dot_general(ks, q, (((1,), (1,)), ((), ())), preferred_element_type=F32)


def _sink_column(sink_ref, hh, tq, m, acc):
    col = lax.broadcasted_iota(jnp.int32, m.shape, 1)
    sink = jnp.where(col < tq, sink_ref[hh, 0], sink_ref[hh, 1]) * LOG2E
    m_new = jnp.maximum(m, sink)
    den_row = lax.broadcasted_iota(jnp.int32, acc.shape, 0) == HEAD_DIM
    return acc * jnp.exp2(m - m_new) + jnp.where(den_row, jnp.exp2(sink - m_new), 0.0)


def _attn_finish(acc, tq):
    o = (acc / acc[HEAD_DIM:HEAD_DIM + 1]).T
    return jnp.concatenate([o[:tq, :HEAD_DIM], o[tq:, :HEAD_DIM]], axis=1)


EXP_CHUNK_ELEMS = 32 * SUBLANES * LANES


def _score_stage(ks, q, s_ref, bias=None):
    s = _scores_t(ks, q)
    if bias is not None:
        s = s + bias
    s_ref[0:s.shape[0]] = s
    return jnp.max(s, axis=0, keepdims=True)


def _softmax_stage(s_ref, p_ref, n, vt, mx, m, acc_ref):
    m_new = jnp.maximum(m, mx)
    rows = EXP_CHUNK_ELEMS // m.shape[1]
    for c in range(0, n, rows):
        p_ref[c:c + rows] = jnp.exp2(s_ref[c:c + rows] - m_new).astype(BF16)
    acc_ref[...] = acc_ref[...] * jnp.exp2(m - m_new) + jnp.dot(vt, p_ref[0:n], preferred_element_type=F32)
    return m_new


BOUNDED_TILES = 4
BOUND_MARGIN = 1.01
SAFE_SHIFT = 40.0


def _flash_exact(q, k_tile, vt_tile, ctx, sa_ref, sb_ref, p_ref, acc_ref, *, tk, n_main):
    pa_ref, pb_ref = p_ref.at[0:tk], p_ref.at[tk:2 * tk]
    acc_ref[...] = jnp.zeros_like(acc_ref)
    m = jnp.full((1, q.shape[0]), NEG_INF, F32)
    mx_a = _score_stage(k_tile(0), q, sa_ref)
    if n_main > 1:
        def body(i, carry):
            m, mx_a = carry
            mx_b = _score_stage(k_tile(2 * i + 1), q, sb_ref)
            m = _softmax_stage(sa_ref, pa_ref, tk, vt_tile(2 * i), mx_a, m, acc_ref)
            mx_a = _score_stage(k_tile(2 * i + 2), q, sa_ref)
            m = _softmax_stage(sb_ref, pb_ref, tk, vt_tile(2 * i + 1), mx_b, m, acc_ref)
            return m, mx_a

        m, mx_a = lax.fori_loop(0, n_main // 2 - 1, body, (m, mx_a))
        mx_b = _score_stage(k_tile(n_main - 1), q, sb_ref)
        m = _softmax_stage(sa_ref, pa_ref, tk, vt_tile(n_main - 2), mx_a, m, acc_ref)
        last = (sb_ref, pb_ref, mx_b)
        spare = (sa_ref, pa_ref)
    else:
        last = (sa_ref, pa_ref, mx_a)
        spare = (sb_ref, pb_ref)
    if ctx is not None:
        mx_c = _score_stage(ctx[0], q, spare[0])
    m = _softmax_stage(last[0], last[1], tk, vt_tile(n_main - 1), last[2], m, acc_ref)
    if ctx is not None:
        m = _softmax_stage(spare[0], spare[1], ctx[0].shape[0], ctx[1], mx_c, m, acc_ref)
    return m


def _flash_bounded(q, k_ref, vt_ref, ctx, bound, p_ref, acc_ref, *, tk, n_main):
    step = BOUNDED_TILES * tk if n_main % BOUNDED_TILES == 0 else tk

    def group(j):
        off = pl.multiple_of(j * step, step)
        for c in range(0, step, tk):
            s_t = _scores_t(k_ref[0, 0, pl.ds(off + c, tk), :], q)
            p_ref[c:c + tk] = jnp.exp2(s_t - bound).astype(BF16)
        return jnp.dot(vt_ref[0, 0, :, pl.ds(off, step)], p_ref[0:step], preferred_element_type=F32)

    first = group(0)
    if ctx is not None:
        pc_ref = p_ref.at[BOUNDED_TILES * tk:BOUNDED_TILES * tk + ctx[0].shape[0]]
        pc_ref[...] = jnp.exp2(_scores_t(ctx[0], q) - bound).astype(BF16)
        first = first + jnp.dot(ctx[1], pc_ref[...], preferred_element_type=F32)
    acc_ref[...] = first

    def body(j, carry):
        acc_ref[...] += group(j)
        return carry

    lax.fori_loop(1, n_main * tk // step, body, 0)


def _flash_kernel(*refs, tq, tk, n_main, has_ctx, has_sink):
    sa_ref, sb_ref, p_ref, acc_ref = refs[-4:]
    o_ref = refs[-5]
    refs = list(refs[:-5])
    sink_ref = refs.pop(0) if has_sink else None
    bound_ref = None if has_sink else refs.pop(0)
    q_ref, k_ref, vt_ref = refs[:3]
    m_cols = GROUP * tq
    q = q_ref[0].reshape(m_cols, HEAD_DIM)
    ctx = (refs[3][0, 0], refs[4][0, 0]) if has_ctx else None

    def k_tile(j):
        return k_ref[0, 0, pl.ds(pl.multiple_of(j * tk, tk), tk), :]

    def vt_tile(j):
        return vt_ref[0, 0, :, pl.ds(pl.multiple_of(j * tk, tk), tk)]

    exact = functools.partial(_flash_exact, q, k_tile, vt_tile, ctx, sa_ref, sb_ref, p_ref, acc_ref,
                              tk=tk, n_main=n_main)
    if has_sink:
        acc = _sink_column(sink_ref, pl.program_id(1), tq, exact(), acc_ref[...])
    else:
        bound = bound_ref[0, 0]
        safe = bound <= SAFE_SHIFT

        @pl.when(safe)
        def _():
            _flash_bounded(q, k_ref, vt_ref, ctx, bound, p_ref, acc_ref, tk=tk, n_main=n_main)

        @pl.when(jnp.logical_not(safe))
        def _():
            exact()

        acc = acc_ref[...]
    o_ref[0] = _attn_finish(acc, tq).astype(o_ref.dtype)


def _kv_specs(n):
    return [pl.BlockSpec((1, 1, n, HEAD_DIM), lambda b, h, i: (b, h, 0, 0)),
            pl.BlockSpec((1, 1, LANES, n), lambda b, h, i: (b, h, 0, 0))]


def _dense_attention(q, k, vt, kc=None, vct=None, *, bound=None, sink=None):
    bx, _, t, _ = q.shape
    s = k.shape[2]
    tq = min(512, t)
    tk = min(512, s)
    has_ctx = kc is not None
    has_sink = sink is not None
    in_specs = [pl.BlockSpec((1, GROUP, tq, HEAD_DIM), lambda b, h, i: (b, h, i, 0))] + _kv_specs(s)
    args = [q, k, vt]
    if has_ctx:
        in_specs += _kv_specs(kc.shape[2])
        args += [kc, vct]
    in_specs = [pl.BlockSpec(memory_space=pltpu.SMEM)] + in_specs
    args = [sink if has_sink else bound] + args
    n_main = s // tk
    assert n_main == 1 or n_main % 2 == 0
    n_ctx = kc.shape[2] if has_ctx else 0
    assert n_ctx <= tk
    m_cols = GROUP * tq
    return pl.pallas_call(
        functools.partial(_flash_kernel, tq=tq, tk=tk, n_main=n_main, has_ctx=has_ctx, has_sink=has_sink),
        grid=(bx, N_KV_HEADS, t // tq),
        in_specs=in_specs,
        out_specs=pl.BlockSpec((1, tq, LANES), lambda b, h, i: (b, i, h)),
        out_shape=jax.ShapeDtypeStruct((bx, t, Q_W), BF16),
        scratch_shapes=[pltpu.VMEM((tk, m_cols), F32), pltpu.VMEM((tk, m_cols), F32),
                        pltpu.VMEM((BOUNDED_TILES * tk + n_ctx, m_cols), BF16), pltpu.VMEM((LANES, m_cols), F32)],
        compiler_params=_params(("arbitrary", "arbitrary", "arbitrary")),
        name="dense_attention",
    )(*args)


BAND_SUB = 256


@functools.lru_cache(maxsize=None)
def _band_bias():
    span = BAND_SUB + 2 * WINDOW
    kr = np.arange(span)[None, :, None]
    qc = (np.arange(GROUP * BAND_SUB) % BAND_SUB)[None, None, :]
    rel = np.arange(3)[:, None, None]
    return np.where(np.abs(kr - qc - rel * WINDOW) <= WINDOW, 0.0, NEG_INF).astype(np.float32)


def _banded_kernel(sink_ref, bound_ref, q_ref, k_ref, vt_ref, kc_ref, vct_ref, bias_ref, o_ref,
                   sl_ref, sc_ref, pl_ref, pc_ref, acc_ref, *, n_sub, s_len):
    i = pl.program_id(2)
    hh = pl.program_id(1)
    sub = BAND_SUB
    m_cols = GROUP * sub
    span = sub + 2 * WINDOW
    n_ctx = kc_ref.shape[2]

    def window(u):
        q = q_ref[0, :, u * sub:(u + 1) * sub, :].reshape(m_cols, HEAD_DIM)
        q0 = (i * n_sub + u) * sub
        start = pl.multiple_of(jnp.clip(q0 - WINDOW, 0, s_len - span), WINDOW)
        return q, start, bias_ref[(q0 - start) // WINDOW]

    shift = jnp.maximum(bound_ref[0, 0], jnp.maximum(sink_ref[hh, 0], sink_ref[hh, 1]) * LOG2E)
    safe = shift <= SAFE_SHIFT

    @pl.when(safe)
    def _():
        col = lax.broadcasted_iota(jnp.int32, (1, m_cols), 1)
        sink = jnp.where(col < sub, sink_ref[hh, 0], sink_ref[hh, 1]) * LOG2E
        den_row = lax.broadcasted_iota(jnp.int32, (LANES, m_cols), 0) == HEAD_DIM
        sink_den = jnp.where(den_row, jnp.exp2(sink - shift), 0.0)
        for u in range(n_sub):
            q, start, bias = window(u)
            pl_u, pc_u = pl_ref.at[u], pc_ref.at[u]
            pl_u[...] = jnp.exp2(_scores_t(k_ref[0, 0, pl.ds(start, span), :], q) + bias - shift).astype(BF16)
            pc_u[...] = jnp.exp2(_scores_t(kc_ref[0, 0], q) - shift).astype(BF16)
            acc = (jnp.dot(vt_ref[0, 0, :, pl.ds(start, span)], pl_u[...], preferred_element_type=F32)
                   + jnp.dot(vct_ref[0, 0], pc_u[...], preferred_element_type=F32) + sink_den)
            o_ref[0, u * sub:(u + 1) * sub] = _attn_finish(acc, sub).astype(o_ref.dtype)

    @pl.when(jnp.logical_not(safe))
    def _():
        stats = []
        for u in range(n_sub):
            q, start, bias = window(u)
            mx_l = _score_stage(k_ref[0, 0, pl.ds(start, span), :], q, sl_ref.at[u], bias=bias)
            mx_c = _score_stage(kc_ref[0, 0], q, sc_ref.at[u])
            stats.append((start, mx_l, mx_c))
        for u in range(n_sub):
            start, mx_l, mx_c = stats[u]
            acc_u = acc_ref.at[u]
            acc_u[...] = jnp.zeros((LANES, m_cols), F32)
            m = jnp.full((1, m_cols), NEG_INF, F32)
            m = _softmax_stage(sl_ref.at[u], pl_ref.at[u], span, vt_ref[0, 0, :, pl.ds(start, span)], mx_l, m, acc_u)
            m = _softmax_stage(sc_ref.at[u], pc_ref.at[u], n_ctx, vct_ref[0, 0], mx_c, m, acc_u)
            acc = _sink_column(sink_ref, hh, sub, m, acc_u[...])
            o_ref[0, u * sub:(u + 1) * sub] = _attn_finish(acc, sub).astype(o_ref.dtype)


def _banded_attention(q, k, vt, kc, vct, sink, bound):
    bx, _, t, _ = q.shape
    n_ctx = kc.shape[2]
    n_sub = 4 if t % (4 * BAND_SUB) == 0 else 2
    tq = n_sub * BAND_SUB
    m_cols = GROUP * BAND_SUB
    span = BAND_SUB + 2 * WINDOW
    bias = jnp.asarray(_band_bias())
    return pl.pallas_call(
        functools.partial(_banded_kernel, n_sub=n_sub, s_len=t),
        grid=(bx, N_KV_HEADS, t // tq),
        in_specs=[pl.BlockSpec(memory_space=pltpu.SMEM), pl.BlockSpec(memory_space=pltpu.SMEM),
                  pl.BlockSpec((1, GROUP, tq, HEAD_DIM), lambda b, h, i: (b, h, i, 0))]
                 + _kv_specs(t) + _kv_specs(n_ctx) + [_full(bias.shape)],
        out_specs=pl.BlockSpec((1, tq, LANES), lambda b, h, i: (b, i, h)),
        out_shape=jax.ShapeDtypeStruct((bx, t, Q_W), BF16),
        scratch_shapes=[pltpu.VMEM((n_sub, span, m_cols), F32), pltpu.VMEM((n_sub, n_ctx, m_cols), F32),
                        pltpu.VMEM((n_sub, span, m_cols), BF16), pltpu.VMEM((n_sub, n_ctx, m_cols), BF16),
                        pltpu.VMEM((n_sub, LANES, m_cols), F32)],
        compiler_params=_params(("arbitrary", "arbitrary", "arbitrary")),
        name="banded_attention",
    )(sink, bound, q, k, vt, kc, vct, bias)


@functools.lru_cache(maxsize=None)
def _filter_features(n):
    j = np.arange(2 * n)
    d = np.where(j <= n, j, 2 * n - j)
    d = np.where(j == n, 0, d)
    bands = (FILT_EMB - 1) // 2
    t01 = np.linspace(0.0, 1.0, n)[d]
    w = 2.0 * np.pi * d.astype(np.float64) / n
    f = np.linspace(1e-4, bands - 1, bands)[None, :]
    feats = np.zeros((2 * n, LANES), np.float64)
    feats[:, 0] = t01
    feats[:, 1:1 + bands] = np.cos(f * w[:, None])
    feats[:, 1 + bands:FILT_EMB] = -np.sin(f * w[:, None])
    feats[:, 64] = t01
    feats[:, 65] = (j < n)
    feats[:, 66] = (j != n)
    return feats.astype(np.float32)


def _filter_kernel(f_ref, w1_ref, b1_ref, w2_ref, b2_ref, w3_ref, b3_ref, w4_ref, fr_ref, dl_ref, k_ref, s_ref):
    f = f_ref[...]
    fr = fr_ref[...]
    mm = lambda a, b: jnp.dot(a, b, preferred_element_type=F32, precision=HI)
    h = jnp.sin(fr * (mm(f, w1_ref[...]) + b1_ref[...]))
    h = jnp.sin(fr * (mm(h, w2_ref[...]) + b2_ref[...]))
    h = jnp.sin(fr * (mm(h, w3_ref[...]) + b3_ref[...]))
    hf = mm(h, w4_ref[...])
    win = jnp.exp(-f[:, 64:65] * dl_ref[...]) + DECAY_SHIFT
    k = jnp.where(f[:, 65:66] > 0.5, hf[:, :CONV_W], hf[:, CONV_W:]) * win * f[:, 66:67]
    k_ref[...] = k

    @pl.when(pl.program_id(0) == 0)
    def _():
        s_ref[...] = jnp.zeros_like(s_ref)

    s_ref[...] += jnp.sum(jnp.abs(k), axis=0, keepdims=True)


def _implicit_filter(n, w1, b1, w2, b2, w3, b3, w4, freq):
    feats = jnp.asarray(_filter_features(n))
    pad = LANES - FILT_WIDTH
    padc = lambda a: jnp.pad(a.reshape(1, -1), ((0, 0), (0, pad)))
    w1p = jnp.pad(w1, ((0, LANES - FILT_EMB), (0, pad)))
    w2p = jnp.pad(w2, ((0, pad), (0, pad)))
    w3p = jnp.pad(w3, ((0, pad), (0, pad)))
    w4p = jnp.pad(w4, ((0, pad), (0, 0)))
    deltas = np.abs(np.linspace(math.log(DECAY_TARGET) / SLOW_DECAY_PCT, math.log(DECAY_TARGET) / FAST_DECAY_PCT,
                                CONV_W)).astype(np.float32).reshape(1, CONV_W)
    tr = min(1024, 2 * n)
    sq = _full((LANES, LANES))
    vec = _full((1, LANES))
    return pl.pallas_call(
        _filter_kernel,
        grid=(2 * n // tr,),
        in_specs=[pl.BlockSpec((tr, LANES), lambda i: (i, 0)), sq, vec, sq, vec, sq, vec,
                  _full((LANES, 2 * CONV_W)), vec, _full((1, CONV_W))],
        out_specs=[pl.BlockSpec((tr, CONV_W), lambda i: (i, 0)), _full((1, CONV_W))],
        out_shape=[jax.ShapeDtypeStruct((2 * n, CONV_W), F32), jax.ShapeDtypeStruct((1, CONV_W), F32)],
        compiler_params=_params(("arbitrary",)),
        name="hyena_filter",
    )(feats, w1p, padc(b1), w2p, padc(b2), w3p, padc(b3), w4p, padc(freq), jnp.asarray(deltas))


def _twiddle(idx, mod):
    ang = 2.0 * np.pi * (idx % mod) / mod
    return np.cos(ang), -np.sin(ang)


def _real_form(mr, mi):
    return np.concatenate([np.concatenate([mr, -mi], -1), np.concatenate([mi, mr], -1)], -2)


@functools.lru_cache(maxsize=None)
def _dft_tables(n2):
    n1 = DFT_N1
    n = n1 * n2
    h = n2 // 2
    a2 = np.arange(n2)
    fr, fi = _twiddle(np.outer(a2, a2), n2)
    m_data = _real_form(fr[:, :h], fi[:, :h])
    m_filt = np.concatenate([fr, fi], 0)
    k2 = a2[:, None, None]
    k1 = np.arange(n1)[None, :, None]
    c1 = np.arange(n1)[None, None, :]
    g = _real_form(*_twiddle(c1 * (n2 * k1 + k2), n))
    a1 = np.arange(n1)
    f1 = _real_form(*_twiddle(np.outer(a1, a1), n1))
    t2 = a1[:, None, None]
    t1 = np.arange(h)[None, :, None]
    j1 = a2[None, None, :]
    hh = _real_form(*_twiddle(j1 * (n1 * t1 + t2), n))
    cast = lambda m: np.asarray(m, dtype=BF16)
    return cast(m_data), cast(m_filt), cast(g), cast(f1), cast(hh)


DFT_ROW_CHUNK = SUBLANES


def _dft_rows_kernel(m_ref, *refs):
    o_ref = refs[-1]
    cols = [jnp.concatenate([r[0, :, t, :] for r in refs[:-1]], axis=0) for t in range(DFT_ROW_CHUNK)]
    rhs = jnp.concatenate(cols, axis=1).astype(BF16)
    out = jnp.dot(m_ref[...], rhs, preferred_element_type=F32)
    half = out.shape[0] // 2
    o_ref[0, 0] = out[:half].astype(o_ref.dtype)
    o_ref[0, 1] = out[half:].astype(o_ref.dtype)


def _dft_stage1(mat, views, pairs, rows, n2, c):
    nin = len(views)
    lc = DFT_ROW_CHUNK * c
    in_specs = [_full(mat.shape)] + [
        pl.BlockSpec((1, rows, DFT_ROW_CHUNK, c), (lambda p, j, a=a: (nin * p + a, 0, j, 0))) for a in range(nin)]
    return pl.pallas_call(
        _dft_rows_kernel,
        grid=(pairs, DFT_N1 // DFT_ROW_CHUNK),
        in_specs=in_specs,
        out_specs=pl.BlockSpec((1, 2, n2, lc), lambda p, j: (p, 0, 0, j)),
        out_shape=jax.ShapeDtypeStruct((pairs, 2, n2, DFT_N1 * c), BF16),
        compiler_params=_params(("arbitrary", "arbitrary")),
        name="dft_stage1",
    )(mat, *views)


def _spectrum_kernel(a_ref, g_ref, sc_ref, o_ref, *, kb):
    for j in range(kb):
        rhs = jnp.concatenate([a_ref[0, 0, j], a_ref[0, 1, j]], axis=0)
        x = jnp.dot(g_ref[j], rhs, preferred_element_type=F32)
        o_ref[0, j] = x[:DFT_N1] * sc_ref[...]
        o_ref[1, j] = x[DFT_N1:] * sc_ref[...]


def _filter_spectrum(a, g, scale, n2, kb):
    c = a.shape[-1]
    return pl.pallas_call(
        functools.partial(_spectrum_kernel, kb=kb),
        grid=(n2 // kb,),
        in_specs=[pl.BlockSpec((1, 2, kb, DFT_N1, c), lambda k: (0, 0, k, 0, 0)),
                  pl.BlockSpec((kb, 2 * DFT_N1, 2 * DFT_N1), lambda k: (k, 0, 0)), _full((1, c))],
        out_specs=pl.BlockSpec((2, kb, DFT_N1, c), lambda k: (0, k, 0, 0)),
        out_shape=jax.ShapeDtypeStruct((2, n2, DFT_N1, c), F32),
        compiler_params=_params(("arbitrary",)),
        name="filter_spectrum",
    )(a, g, scale)


def _dft_mid_kernel(a_ref, g_ref, kh_ref, f_ref, o_ref, *, kb):
    for j in range(kb):
        rhs = jnp.concatenate([a_ref[0, 0, j], a_ref[0, 1, j]], axis=0)
        x = jnp.dot(g_ref[j], rhs, preferred_element_type=F32)
        xr, xi = x[:DFT_N1], x[DFT_N1:]
        kr, ki = kh_ref[0, j], kh_ref[1, j]
        yr = xr * kr - xi * ki
        yi = xr * ki + xi * kr
        v = jnp.concatenate([yr, -yi], axis=0).astype(BF16)
        b = jnp.dot(f_ref[...], v, preferred_element_type=F32)
        o_ref[0, 0, j] = b[:DFT_N1].astype(o_ref.dtype)
        o_ref[0, 1, j] = b[DFT_N1:].astype(o_ref.dtype)


def _dft_mid(a, g, khat, f1, pairs, n2, kb):
    c = a.shape[-1]
    blk = pl.BlockSpec((1, 2, kb, DFT_N1, c), lambda k, p: (p, 0, k, 0, 0))
    return pl.pallas_call(
        functools.partial(_dft_mid_kernel, kb=kb),
        grid=(n2 // kb, pairs),
        in_specs=[blk, pl.BlockSpec((kb, 2 * DFT_N1, 2 * DFT_N1), lambda k, p: (k, 0, 0)),
                  pl.BlockSpec((2, kb, DFT_N1, c), lambda k, p: (0, k, 0, 0)), _full(f1.shape)],
        out_specs=blk,
        out_shape=jax.ShapeDtypeStruct(a.shape, BF16),
        compiler_params=_params(("arbitrary", "arbitrary")),
        name="dft_mid",
    )(a, g, khat, f1)


def _dft_last_kernel(b_ref, h_ref, u_ref, x0_ref, bd_ref, o_ref, *, c):
    half = h_ref.shape[1] // 2
    bd = bd_ref[...]
    for t in range(DFT_ROW_CHUNK):
        cols = slice(t * c, (t + 1) * c)
        rhs = jnp.concatenate([b_ref[0, 0, :, cols], b_ref[0, 1, :, cols]], axis=0)
        v = jnp.dot(h_ref[t], rhs, preferred_element_type=F32)
        o_ref[0, :, t, :] = (v[:half] + u_ref[0, :, t, :] * bd) * x0_ref[0, :, t, :]
        o_ref[1, :, t, :] = (-v[half:] + u_ref[1, :, t, :] * bd) * x0_ref[1, :, t, :]


def _dft_last(bm, hh, u_view, x0_view, bias_d, pairs, n2, c):
    half = n2 // 2
    tc = DFT_ROW_CHUNK
    io = pl.BlockSpec((2, half, tc, c), lambda p, j: (p, 0, j, 0))
    return pl.pallas_call(
        functools.partial(_dft_last_kernel, c=c),
        grid=(pairs, DFT_N1 // tc),
        in_specs=[pl.BlockSpec((1, 2, n2, tc * c), lambda p, j: (p, 0, 0, j)),
                  pl.BlockSpec((tc, n2, 2 * n2), lambda p, j: (j, 0, 0)), io, io, _full((1, c))],
        out_specs=io,
        out_shape=jax.ShapeDtypeStruct((2 * pairs, half, DFT_N1, c), F32),
        compiler_params=_params(("arbitrary", "arbitrary")),
        name="dft_last",
    )(bm, hh, u_view, x0_view, bias_d)


def _long_conv_mixer(u, x0, kfilt, ksum, bias_d):
    b, n, c = u.shape
    n2 = 2 * n // DFT_N1
    half = n2 // 2
    pairs = b // 2
    m_data, m_filt, g, f1, hh = (jnp.asarray(t) for t in _dft_tables(n2))
    kb = min(8, n2)
    scale = 1.0 / (ksum * float(DFT_N1 * n2))
    ka = _dft_stage1(m_filt, [kfilt.reshape(1, n2, DFT_N1, c)], 1, n2, n2, c)
    khat = _filter_spectrum(ka.reshape(1, 2, n2, DFT_N1, c), g, scale, n2, kb)
    u_view = u.reshape(b, half, DFT_N1, c)
    a = _dft_stage1(m_data, [u_view, u_view], pairs, half, n2, c)
    bm = _dft_mid(a.reshape(pairs, 2, n2, DFT_N1, c), g, khat, f1, pairs, n2, kb)
    out = _dft_last(bm.reshape(pairs, 2, n2, DFT_N1 * c), hh, u_view, x0.reshape(b, half, DFT_N1, c),
                    bias_d.reshape(1, c), pairs, n2, c)
    return out.reshape(b, n, c)


@functools.lru_cache(maxsize=None)
def _small_dft_tables(n):
    big = 2 * n
    a = np.arange(big)
    fr, fi = _twiddle(np.outer(a, a), big)
    cast = lambda m: np.asarray(m, dtype=BF16)
    return (cast(np.concatenate([fr, fi], 0)),
            cast(_real_form(fr[:, :n], fi[:, :n])),
            cast(_real_form(fr[:n], fi[:n])))


def _small_conv_kernel(k_ref, ks_ref, u_ref, x0_ref, bd_ref, mf_ref, md_ref, mi_ref, o_ref, *, n):
    big = 2 * n
    kh = jnp.dot(mf_ref[...], k_ref[...].astype(BF16), preferred_element_type=F32) * (1.0 / (ks_ref[...] * big))
    kr, ki = kh[:big], kh[big:]
    rhs = jnp.concatenate([u_ref[0], u_ref[1]], axis=0).astype(BF16)
    x = jnp.dot(md_ref[...], rhs, preferred_element_type=F32)
    xr, xi = x[:big], x[big:]
    v = jnp.concatenate([xr * kr - xi * ki, -(xr * ki + xi * kr)], axis=0).astype(BF16)
    y = jnp.dot(mi_ref[...], v, preferred_element_type=F32)
    bd = bd_ref[...]
    o_ref[0] = ((y[:n] + u_ref[0] * bd) * x0_ref[0]).astype(o_ref.dtype)
    o_ref[1] = ((-y[n:] + u_ref[1] * bd) * x0_ref[1]).astype(o_ref.dtype)


def _small_conv_mixer(u, x0, kfilt, ksum, bias_d):
    b, n, c = u.shape
    mf, md, mi = (jnp.asarray(t) for t in _small_dft_tables(n))
    io = pl.BlockSpec((2, n, c), lambda p: (p, 0, 0))
    return pl.pallas_call(
        functools.partial(_small_conv_kernel, n=n),
        grid=(b // 2,),
        in_specs=[_full((2 * n, c)), _full((1, c)), io, io, _full((1, c)),
                  _full(mf.shape), _full(md.shape), _full(mi.shape)],
        out_specs=io,
        out_shape=jax.ShapeDtypeStruct((b, n, c), BF16),
        compiler_params=_params(("arbitrary",)),
        name="small_conv",
    )(kfilt, ksum, u, x0, bias_d.reshape(1, c), mf, md, mi)


FFN_CHUNK = D_FF // 2


def _gelu_tanh(x):
    return 0.5 * x * (1.0 + jnp.tanh(math.sqrt(2.0 / math.pi) * (x + 0.044715 * (x * x * x))))


def _ffn_kernel(x_ref, xp_ref, xn_ref, oa_ref, oap_ref, oan_ref, oc_ref, ocp_ref, ocn_ref, wa_ref, wc_ref, g1_ref,
                sh_ref, sc_ref, gt_ref, g_ref, wu_ref, cw_ref, cb_ref, wd_ref, o_ref):
    i = pl.program_id(1)
    nt = pl.num_programs(1)
    tm = x_ref.shape[1]

    def halo(p_ref, n_ref):
        return jnp.concatenate([p_ref[0][-SUBLANES:], n_ref[0][:SUBLANES]], axis=0)

    def mixer_out(oa, oc):
        return (jnp.dot(oa.astype(BF16), wa_ref[...], preferred_element_type=F32)
                + jnp.dot(oc.astype(BF16), wc_ref[...], preferred_element_type=F32))

    g1 = g1_ref[0]
    xm = x_ref[0] + g1 * mixer_out(oa_ref[0], oc_ref[0])
    xh = halo(xp_ref, xn_ref) + g1 * mixer_out(halo(oap_ref, oan_ref), halo(ocp_ref, ocn_ref))
    ext = jnp.concatenate([xh[:SUBLANES], xm, xh[SUBLANES:]], axis=0)
    r = tm + 2 * SUBLANES
    h = _modulated_norm(ext, g_ref[...], sh_ref[0], sc_ref[0])
    row = lax.broadcasted_iota(jnp.int32, (r, 1), 0)
    inside = jnp.logical_and(jnp.logical_or(i > 0, row >= SUBLANES), jnp.logical_or(i < nt - 1, row < tm + SUBLANES))
    h = jnp.where(inside, h, 0.0).astype(BF16)
    hm = h[SUBLANES:tm + SUBLANES]
    acc = jnp.zeros((tm, D_MODEL), F32)
    for c0 in range(0, D_FF, FFN_CHUNK):
        a = jnp.dot(h, wu_ref[:, c0:c0 + FFN_CHUNK], preferred_element_type=F32)
        v = jnp.dot(hm, wu_ref[:, D_FF + c0:D_FF + c0 + FFN_CHUNK], preferred_element_type=F32)
        ap = pltpu.roll(a, 1, 0)[SUBLANES:tm + SUBLANES]
        an = pltpu.roll(a, r - 1, 0)[SUBLANES:tm + SUBLANES]
        cw = cw_ref[:, c0:c0 + FFN_CHUNK]
        conv = ap * cw[0:1] + a[SUBLANES:tm + SUBLANES] * cw[1:2] + an * cw[2:3] + cb_ref[:, c0:c0 + FFN_CHUNK]
        act = (_gelu_tanh(conv) * v).astype(BF16)
        acc = acc + jnp.dot(act, wd_ref[c0:c0 + FFN_CHUNK, :], preferred_element_type=F32)
    o_ref[0] = xm + gt_ref[0] * acc


def _mixer_out_ffn(x, oa, oc, wo_bf, gate1, sh, sc, gate2, gain, wu_bf, conv_w, conv_b, wd_bf):
    bx, t, d = x.shape
    tm = min(512, t)
    vec = pl.BlockSpec((1, 1, d), lambda b, i: (b, 0, 0))
    once = pl.Buffered(1)
    return pl.pallas_call(
        _ffn_kernel,
        grid=(bx, t // tm),
        in_specs=_halo_specs(tm, t, d) + _halo_specs(tm, t, Q_W, oa.dtype) + _halo_specs(tm, t, CONV_W, oc.dtype) + [
            pl.BlockSpec((Q_W, d), lambda b, i: (0, 0)), pl.BlockSpec((CONV_W, d), lambda b, i: (1, 0)),
            vec, vec, vec, vec, _full((1, d)),
            pl.BlockSpec((d, 2 * D_FF), lambda b, i: (0, 0), pipeline_mode=once),
            _full((3, D_FF)), _full((1, D_FF)),
            pl.BlockSpec((D_FF, d), lambda b, i: (0, 0), pipeline_mode=once)],
        out_specs=pl.BlockSpec((1, tm, d), lambda b, i: (b, i, 0)),
        out_shape=jax.ShapeDtypeStruct((bx, t, d), F32),
        compiler_params=_params(("arbitrary", "arbitrary")),
        name="mixer_out_ffn",
    )(x, x, x, oa, oa, oa, oc, oc, oc, wo_bf, wo_bf, gate1, sh, sc, gate2, gain, wu_bf, conv_w,
      conv_b.reshape(1, D_FF), wd_bf)


@functools.lru_cache(maxsize=None)
def _rope_tables(t):
    pos = np.arange(t)
    n_freq = HEAD_DIM // 4
    inv = ROPE_THETA ** (-np.arange(n_freq, dtype=np.float64) / n_freq)
    ang_r = (pos // GRID_W)[:, None] * inv
    ang_c = (pos % GRID_W)[:, None] * inv
    zero = np.zeros_like(ang_r)
    cos = np.concatenate([np.cos(ang_r)] * 2 + [np.cos(ang_c)] * 2, 1)
    sa = np.concatenate([-np.sin(ang_r), zero, -np.sin(ang_c), zero], 1)
    sb = np.concatenate([zero, np.sin(ang_r), zero, np.sin(ang_c)], 1)
    tile = lambda m: np.tile(m, (1, LANES // HEAD_DIM)).astype(np.float32)
    return tile(cos), tile(sa), tile(sb)


@functools.lru_cache(maxsize=None)
def _identity_rope_tables(t):
    return np.ones((t, LANES), np.float32), np.zeros((t, LANES), np.float32), np.zeros((t, LANES), np.float32)


@functools.lru_cache(maxsize=None)
def _head_block_diag():
    hid = np.arange(Q_W) // HEAD_DIM
    return np.asarray(hid[:, None] == hid[None, :], dtype=BF16)


def kernel(x, c, ctx, c_ctx, ada_w, ada_b, norm_mix, norm_ffn, mix_w_in, mix_w_out, attn_q_norm, attn_k_norm,
           swa_sink, hy_conv_w, hy_conv_b, hy_w1, hy_b1, hy_w2, hy_b2, hy_w3, hy_b3, hy_w4, hy_freq, hy_bias_d,
           sc_conv_w, ffn_w_up, ffn_conv_w, ffn_conv_b, ffn_w_down):
    b, s, d = x.shape
    s_ctx = ctx.shape[1]
    depth = ada_w.shape[0]
    assert d == D_MODEL and b % 2 == 0 and b + 1 <= SUBLANES and s % 1024 == 0 and s_ctx % SUBLANES == 0

    cvec = jnp.concatenate([c, c_ctx[None, :], jnp.zeros((SUBLANES - b - 1, d), F32)], axis=0)
    mods = _mods(cvec, ada_w, ada_b)
    rope = [jnp.asarray(t) for t in _rope_tables(s)]
    rope_ctx = [jnp.asarray(t) for t in _identity_rope_tables(s_ctx)]
    bd = jnp.asarray(_head_block_diag())
    xc = ctx

    for i in range(depth):
        last = i == depth - 1
        j = i // 2
        lat = [mods[i, :b, k * d:(k + 1) * d][:, None, :] for k in range(6)]
        cx = [jnp.broadcast_to(mods[i, b, k * d:(k + 1) * d][None, None, :], (b, 1, d)) for k in range(6)]
        w_in = mix_w_in[i].astype(BF16)
        w_out = mix_w_out[i].astype(BF16)
        w_up = ffn_w_up[i].astype(BF16)
        w_down = ffn_w_down[i].astype(BF16)
        g_mix = norm_mix[i].reshape(1, d)
        g_ffn = norm_ffn[i].reshape(1, d)
        qg = jnp.tile(attn_q_norm[i], N_Q_HEADS).reshape(1, Q_W)
        kg = jnp.tile(attn_k_norm[i], N_KV_HEADS).reshape(1, KV_W)

        conv = (hy_conv_w[j], hy_conv_b[j]) if i % 2 == 0 else (sc_conv_w[j],)
        q, k, vt, *mix = _inproj(x, lat[0], lat[1], g_mix, w_in, qg, kg, bd, *rope, *conv)
        qc, kc, vct, *mixc = _inproj(xc, cx[0], cx[1], g_mix, w_in, qg, kg, bd, *rope_ctx, *conv)

        bound = _score_bound(attn_q_norm[i], attn_k_norm[i])
        if i % 2 == 0:
            fargs = (hy_w1[j], hy_b1[j], hy_w2[j], hy_b2[j], hy_w3[j], hy_b3[j], hy_w4[j], hy_freq[j])
            o_attn = _dense_attention(q, k, vt, kc, vct, bound=bound)
            kf, ks = _implicit_filter(s, *fargs)
            o_conv = _long_conv_mixer(*mix, kf, ks, hy_bias_d[j])
            if not last:
                oc_attn = _dense_attention(qc, kc, vct, bound=bound)
                kfc, ksc = _implicit_filter(s_ctx, *fargs)
                oc_conv = _small_conv_mixer(*mixc, kfc, ksc, hy_bias_d[j])
        else:
            sink = swa_sink[j].reshape(N_KV_HEADS, GROUP)
            o_attn = _banded_attention(q, k, vt, kc, vct, sink, bound)
            o_conv = mix[0]
            if not last:
                oc_attn = _dense_attention(qc, kc, vct, sink=sink)
                oc_conv = mixc[0]

        ffn = (g_ffn, w_up, ffn_conv_w[i], ffn_conv_b[i], w_down)
        x = _mixer_out_ffn(x, o_attn, o_conv, w_out, lat[2], lat[3], lat[4], lat[5], *ffn)
        if not last:
            xc = _mixer_out_ffn(xc, oc_attn, oc_conv, w_out, cx[2], cx[3], cx[4], cx[5], *ffn)
    return x
```

```python
import functools
import math

import numpy as np
import jax
import jax.numpy as jnp
from jax import lax
from jax.experimental import pallas as pl
from jax.experimental.pallas import tpu as pltpu

F32 = jnp.float32
BF16 = jnp.bfloat16
HI = lax.Precision.HIGHEST

D_MODEL = 1024
GRID_W = 64
HEAD_DIM = 64
N_Q_HEADS = 8
N_KV_HEADS = 4
GROUP = N_Q_HEADS // N_KV_HEADS
Q_W = N_Q_HEADS * HEAD_DIM
KV_W = N_KV_HEADS * HEAD_DIM
QKV_W = Q_W + 2 * KV_W
CONV_W = D_MODEL // 2
MIX_IN_W = QKV_W + 3 * CONV_W
WINDOW = 128
ROPE_THETA = 10000.0
FILT_EMB = 33
FILT_WIDTH = 64
DECAY_TARGET = 1e-2
FAST_DECAY_PCT = 0.3
SLOW_DECAY_PCT = 1.5
DECAY_SHIFT = 0.05
D_FF = 2816
NEG_INF = -1e30
RMS_EPS = 1e-6
LOG2E = 1.4426950408889634
Q_SCALE = HEAD_DIM ** -0.5 * LOG2E

LANES = 128
SUBLANES = 8
DFT_N1 = 128
VMEM_LIMIT_MB = 56
PROJ_SUB_ROWS = 128


def _params(sem, vmem_mb=VMEM_LIMIT_MB):
    return pltpu.CompilerParams(dimension_semantics=sem, vmem_limit_bytes=vmem_mb * 1024 * 1024)


def _full(shape):
    nd = len(shape)
    return pl.BlockSpec(shape, lambda *_: (0,) * nd)


def _mods_kernel(c_ref, w_ref, b_ref, o_ref):
    c = c_ref[...]
    s = c / (1.0 + jnp.exp(-c))
    o_ref[0] = jnp.dot(s, w_ref[0], preferred_element_type=F32, precision=HI) + b_ref[0]


def _mods(cvec, ada_w, ada_b):
    depth, d, n6 = ada_w.shape
    tn = 1536
    return pl.pallas_call(
        _mods_kernel,
        grid=(depth, n6 // tn),
        in_specs=[_full((SUBLANES, d)),
                  pl.BlockSpec((1, d, tn), lambda l, j: (l, 0, j)),
                  pl.BlockSpec((1, 1, tn), lambda l, j: (l, 0, j))],
        out_specs=pl.BlockSpec((1, SUBLANES, tn), lambda l, j: (l, 0, j)),
        out_shape=jax.ShapeDtypeStruct((depth, SUBLANES, n6), F32),
        compiler_params=_params(("arbitrary", "arbitrary")),
        name="ada_mods",
    )(cvec, ada_w, ada_b.reshape(depth, 1, n6))


def _modulated_norm(x, gain, shift, scale):
    ms = jnp.mean(x * x, axis=-1, keepdims=True)
    return (x * lax.rsqrt(ms + RMS_EPS)) * gain * (1.0 + scale) + shift


def _halo_specs(tm, t, width, dtype=F32):
    rows = SUBLANES * 4 // jnp.dtype(dtype).itemsize
    nb = tm // rows
    last = t // rows - 1
    return [pl.BlockSpec((1, tm, width), lambda b, i: (b, i, 0)),
            pl.BlockSpec((1, rows, width), lambda b, i: (b, jnp.maximum(i * nb - 1, 0), 0)),
            pl.BlockSpec((1, rows, width), lambda b, i: (b, jnp.minimum((i + 1) * nb, last), 0))]


def _inproj_kernel(*refs, mode):
    (x_ref, xp_ref, xn_ref, sh_ref, sc_ref, g_ref, w_ref, qg_ref, kg_ref, bd_ref,
     cos_ref, sa_ref, sb_ref, cw_ref) = refs[:14]
    if mode == "hyena":
        cb_ref, q_ref, k_ref, vt_ref, u_ref, x0_ref = refs[14:]
    else:
        q_ref, k_ref, vt_ref, oc_ref = refs[14:]
    i = pl.program_id(1)
    nt = pl.num_programs(1)
    tm = x_ref.shape[1]
    sub = min(PROJ_SUB_ROWS, tm)
    gain, shift, scale = g_ref[...], sh_ref[0], sc_ref[0]

    xh = jnp.concatenate([xp_ref[0], xn_ref[0]], axis=0)
    zh = jnp.dot(_modulated_norm(xh, gain, shift, scale).astype(BF16), w_ref[:, QKV_W:], preferred_element_type=F32)
    before = jnp.where(i > 0, zh[:SUBLANES], 0.0)
    after = jnp.where(i < nt - 1, zh[SUBLANES:], 0.0)
    cw = cw_ref[...]

    def conv_stage(ext, rows):
        n = sub + 2 * SUBLANES
        mid = slice(SUBLANES, sub + SUBLANES)
        if mode == "hyena":
            c = (pltpu.roll(ext, 1, 0)[mid] * cw[0:1] + ext[mid] * cw[1:2] + pltpu.roll(ext, n - 1, 0)[mid] * cw[2:3]
                 + cb_ref[...])
            x0_ref[0, rows] = c[:, :CONV_W]
            u_ref[0, rows] = c[:, 2 * CONV_W:] * c[:, CONV_W:2 * CONV_W]
        else:
            pr = ext[:, CONV_W:2 * CONV_W] * ext[:, 2 * CONV_W:]
            conv = pltpu.roll(pr, 1, 0)[mid] * cw[0:1] + pr[mid] * cw[1:2] + pltpu.roll(pr, n - 1, 0)[mid] * cw[2:3]
            oc_ref[0, rows] = (ext[mid, :CONV_W] * conv).astype(oc_ref.dtype)

    def head_norm_rope(t, hgain, bd, cos, sa, sb):
        ssq = jnp.dot((t * t).astype(BF16), bd, preferred_element_type=F32)
        t = t * lax.rsqrt(ssq * (1.0 / HEAD_DIM) + RMS_EPS) * hgain
        outs = []
        for j in range(t.shape[1] // LANES):
            tj = t[:, j * LANES:(j + 1) * LANES]
            outs.append(tj * cos + pltpu.roll(tj, LANES - 16, 1) * sa + pltpu.roll(tj, 16, 1) * sb)
        return outs

    held = None
    for r0 in range(0, tm, sub):
        rows = slice(r0, r0 + sub)
        h = _modulated_norm(x_ref[0, rows], gain, shift, scale)
        hb = h.astype(BF16)
        z = jnp.dot(hb, w_ref[:, :QKV_W], preferred_element_type=F32)
        zc = jnp.dot(hb, w_ref[:, QKV_W:], preferred_element_type=F32)
        if held is not None:
            conv_stage(jnp.concatenate([before, held[0], zc[:SUBLANES]], axis=0), held[1])
            before = held[0][-SUBLANES:]
        held = (zc, rows)
        tabs = (cos_ref[rows], sa_ref[rows], sb_ref[rows])
        qs = head_norm_rope(z[:, :Q_W], qg_ref[...], bd_ref[...], *tabs)
        ks = head_norm_rope(z[:, Q_W:Q_W + KV_W], kg_ref[...], bd_ref[:KV_W, :KV_W], *tabs)
        for j, qj in enumerate(qs):
            qj = (qj * Q_SCALE).astype(BF16)
            q_ref[0, 2 * j, rows] = qj[:, :HEAD_DIM]
            q_ref[0, 2 * j + 1, rows] = qj[:, HEAD_DIM:]
        for j, kj in enumerate(ks):
            kj = kj.astype(BF16)
            k_ref[0, 2 * j, rows] = kj[:, :HEAD_DIM]
            k_ref[0, 2 * j + 1, rows] = kj[:, HEAD_DIM:]
        vt = z[:, Q_W + KV_W:QKV_W].T
        ones_row = (lax.broadcasted_iota(jnp.int32, (LANES - HEAD_DIM, sub), 0) == 0).astype(F32)
        for hh in range(N_KV_HEADS):
            vt_ref[0, hh, :, rows] = jnp.concatenate([vt[hh * HEAD_DIM:(hh + 1) * HEAD_DIM], ones_row],
                                                     axis=0).astype(BF16)

    conv_stage(jnp.concatenate([before, held[0], after], axis=0), held[1])


def _inproj(x, sh, sc, gain, w_bf, qg, kg, bd, cos, sa, sb, conv_w, conv_b=None):
    bx, t, d = x.shape
    tm = min(512, t)
    mode = "hyena" if conv_b is not None else "short"
    vec = pl.BlockSpec((1, 1, d), lambda b, i: (b, 0, 0))
    tab = pl.BlockSpec((tm, LANES), lambda b, i: (i, 0))
    row = lambda w: pl.BlockSpec((1, tm, w), lambda b, i: (b, i, 0))
    in_specs = _halo_specs(tm, t, d) + [vec, vec, _full((1, d)), _full((d, MIX_IN_W)), _full((1, Q_W)),
                                        _full((1, KV_W)), _full((Q_W, Q_W)), tab, tab, tab, _full(conv_w.shape)]
    args = [x, x, x, sh, sc, gain, w_bf, qg, kg, bd, cos, sa, sb, conv_w]
    out_specs = [pl.BlockSpec((1, N_Q_HEADS, tm, HEAD_DIM), lambda b, i: (b, 0, i, 0)),
                 pl.BlockSpec((1, N_KV_HEADS, tm, HEAD_DIM), lambda b, i: (b, 0, i, 0)),
                 pl.BlockSpec((1, N_KV_HEADS, LANES, tm), lambda b, i: (b, 0, 0, i))]
    out_shape = [jax.ShapeDtypeStruct((bx, N_Q_HEADS, t, HEAD_DIM), BF16),
                 jax.ShapeDtypeStruct((bx, N_KV_HEADS, t, HEAD_DIM), BF16),
                 jax.ShapeDtypeStruct((bx, N_KV_HEADS, LANES, t), BF16)]
    if mode == "hyena":
        in_specs.append(_full((1, 3 * CONV_W)))
        args.append(conv_b.reshape(1, 3 * CONV_W))
        out_specs += [row(CONV_W), row(CONV_W)]
        out_shape += [jax.ShapeDtypeStruct((bx, t, CONV_W), F32)] * 2
    else:
        out_specs.append(row(CONV_W))
        out_shape.append(jax.ShapeDtypeStruct((bx, t, CONV_W), BF16))
    return pl.pallas_call(
        functools.partial(_inproj_kernel, mode=mode),
        grid=(bx, t // tm),
        in_specs=in_specs,
        out_specs=out_specs,
        out_shape=out_shape,
        compiler_params=_params(("arbitrary", "arbitrary")),
        name="in_proj",
    )(*args)


def _score_bound(q_gain, k_gain):
    b = HEAD_DIM * Q_SCALE * BOUND_MARGIN * jnp.max(jnp.abs(q_gain)) * jnp.max(jnp.abs(k_gain))
    return b.reshape(1, 1).astype(F32)


def _scores_t(ks, q):
    return lax.dot_general(ks, q, (((1,), (1,)), ((), ())), preferred_element_type=F32)


def _sink_column(sink_ref, hh, tq, m, acc):
    col = lax.broadcasted_iota(jnp.int32, m.shape, 1)
    sink = jnp.where(col < tq, sink_ref[hh, 0], sink_ref[hh, 1]) * LOG2E
    m_new = jnp.maximum(m, sink)
    den_row = lax.broadcasted_iota(jnp.int32, acc.shape, 0) == HEAD_DIM
    return acc * jnp.exp2(m - m_new) + jnp.where(den_row, jnp.exp2(sink - m_new), 0.0)


def _attn_finish(acc, tq):
    o = (acc / acc[HEAD_DIM:HEAD_DIM + 1]).T
    return jnp.concatenate([o[:tq, :HEAD_DIM], o[tq:, :HEAD_DIM]], axis=1)


EXP_CHUNK_ELEMS = 32 * SUBLANES * LANES


def _score_stage(ks, q, s_ref, bias=None):
    s = _scores_t(ks, q)
    if bias is not None:
        s = s + bias
    s_ref[0:s.shape[0]] = s
    return jnp.max(s, axis=0, keepdims=True)


def _softmax_stage(s_ref, p_ref, n, vt, mx, m, acc_ref):
    m_new = jnp.maximum(m, mx)
    rows = EXP_CHUNK_ELEMS // m.shape[1]
    for c in range(0, n, rows):
        p_ref[c:c + rows] = jnp.exp2(s_ref[c:c + rows] - m_new).astype(BF16)
    acc_ref[...] = acc_ref[...] * jnp.exp2(m - m_new) + jnp.dot(vt, p_ref[0:n], preferred_element_type=F32)
    return m_new


BOUNDED_TILES = 4
BOUND_MARGIN = 1.01
SAFE_SHIFT = 40.0


def _flash_exact(q, k_tile, vt_tile, ctx, sa_ref, sb_ref, p_ref, acc_ref, *, tk, n_main):
    pa_ref, pb_ref = p_ref.at[0:tk], p_ref.at[tk:2 * tk]
    acc_ref[...] = jnp.zeros_like(acc_ref)
    m = jnp.full((1, q.shape[0]), NEG_INF, F32)
    mx_a = _score_stage(k_tile(0), q, sa_ref)
    if n_main > 1:
        def body(i, carry):
            m, mx_a = carry
            mx_b = _score_stage(k_tile(2 * i + 1), q, sb_ref)
            m = _softmax_stage(sa_ref, pa_ref, tk, vt_tile(2 * i), mx_a, m, acc_ref)
            mx_a = _score_stage(k_tile(2 * i + 2), q, sa_ref)
            m = _softmax_stage(sb_ref, pb_ref, tk, vt_tile(2 * i + 1), mx_b, m, acc_ref)
            return m, mx_a

        m, mx_a = lax.fori_loop(0, n_main // 2 - 1, body, (m, mx_a))
        mx_b = _score_stage(k_tile(n_main - 1), q, sb_ref)
        m = _softmax_stage(sa_ref, pa_ref, tk, vt_tile(n_main - 2), mx_a, m, acc_ref)
        last = (sb_ref, pb_ref, mx_b)
        spare = (sa_ref, pa_ref)
    else:
        last = (sa_ref, pa_ref, mx_a)
        spare = (sb_ref, pb_ref)
    if ctx is not None:
        mx_c = _score_stage(ctx[0], q, spare[0])
    m = _softmax_stage(last[0], last[1], tk, vt_tile(n_main - 1), last[2], m, acc_ref)
    if ctx is not None:
        m = _softmax_stage(spare[0], spare[1], ctx[0].shape[0], ctx[1], mx_c, m, acc_ref)
    return m


def _flash_bounded(q, k_ref, vt_ref, ctx, bound, p_ref, acc_ref, *, tk, n_main):
    step = BOUNDED_TILES * tk if n_main % BOUNDED_TILES == 0 else tk

    def group(j):
        off = pl.multiple_of(j * step, step)
        for c in range(0, step, tk):
            s_t = _scores_t(k_ref[0, 0, pl.ds(off + c, tk), :], q)
            p_ref[c:c + tk] = jnp.exp2(s_t - bound).astype(BF16)
        return jnp.dot(vt_ref[0, 0, :, pl.ds(off, step)], p_ref[0:step], preferred_element_type=F32)

    first = group(0)
    if ctx is not None:
        pc_ref = p_ref.at[BOUNDED_TILES * tk:BOUNDED_TILES * tk + ctx[0].shape[0]]
        pc_ref[...] = jnp.exp2(_scores_t(ctx[0], q) - bound).astype(BF16)
        first = first + jnp.dot(ctx[1], pc_ref[...], preferred_element_type=F32)
    acc_ref[...] = first

    def body(j, carry):
        acc_ref[...] += group(j)
        return carry

    lax.fori_loop(1, n_main * tk // step, body, 0)


def _flash_kernel(*refs, tq, tk, n_main, has_ctx, has_sink):
    sa_ref, sb_ref, p_ref, acc_ref = refs[-4:]
    o_ref = refs[-5]
    refs = list(refs[:-5])
    sink_ref = refs.pop(0) if has_sink else None
    bound_ref = None if has_sink else refs.pop(0)
    q_ref, k_ref, vt_ref = refs[:3]
    m_cols = GROUP * tq
    q = q_ref[0].reshape(m_cols, HEAD_DIM)
    ctx = (refs[3][0, 0], refs[4][0, 0]) if has_ctx else None

    def k_tile(j):
        return k_ref[0, 0, pl.ds(pl.multiple_of(j * tk, tk), tk), :]

    def vt_tile(j):
        return vt_ref[0, 0, :, pl.ds(pl.multiple_of(j * tk, tk), tk)]

    exact = functools.partial(_flash_exact, q, k_tile, vt_tile, ctx, sa_ref, sb_ref, p_ref, acc_ref,
                              tk=tk, n_main=n_main)
    if has_sink:
        acc = _sink_column(sink_ref, pl.program_id(1), tq, exact(), acc_ref[...])
    else:
        bound = bound_ref[0, 0]
        safe = bound <= SAFE_SHIFT

        @pl.when(safe)
        def _():
            _flash_bounded(q, k_ref, vt_ref, ctx, bound, p_ref, acc_ref, tk=tk, n_main=n_main)

        @pl.when(jnp.logical_not(safe))
        def _():
            exact()

        acc = acc_ref[...]
    o_ref[0] = _attn_finish(acc, tq).astype(o_ref.dtype)


def _kv_specs(n):
    return [pl.BlockSpec((1, 1, n, HEAD_DIM), lambda b, h, i: (b, h, 0, 0)),
            pl.BlockSpec((1, 1, LANES, n), lambda b, h, i: (b, h, 0, 0))]


def _dense_attention(q, k, vt, kc=None, vct=None, *, bound=None, sink=None):
    bx, _, t, _ = q.shape
    s = k.shape[2]
    tq = min(512, t)
    tk = min(512, s)
    has_ctx = kc is not None
    has_sink = sink is not None
    in_specs = [pl.BlockSpec((1, GROUP, tq, HEAD_DIM), lambda b, h, i: (b, h, i, 0))] + _kv_specs(s)
    args = [q, k, vt]
    if has_ctx:
        in_specs += _kv_specs(kc.shape[2])
        args += [kc, vct]
    in_specs = [pl.BlockSpec(memory_space=pltpu.SMEM)] + in_specs
    args = [sink if has_sink else bound] + args
    n_main = s // tk
    assert n_main == 1 or n_main % 2 == 0
    n_ctx = kc.shape[2] if has_ctx else 0
    assert n_ctx <= tk
    m_cols = GROUP * tq
    return pl.pallas_call(
        functools.partial(_flash_kernel, tq=tq, tk=tk, n_main=n_main, has_ctx=has_ctx, has_sink=has_sink),
        grid=(bx, N_KV_HEADS, t // tq),
        in_specs=in_specs,
        out_specs=pl.BlockSpec((1, tq, LANES), lambda b, h, i: (b, i, h)),
        out_shape=jax.ShapeDtypeStruct((bx, t, Q_W), BF16),
        scratch_shapes=[pltpu.VMEM((tk, m_cols), F32), pltpu.VMEM((tk, m_cols), F32),
                        pltpu.VMEM((BOUNDED_TILES * tk + n_ctx, m_cols), BF16), pltpu.VMEM((LANES, m_cols), F32)],
        compiler_params=_params(("arbitrary", "arbitrary", "arbitrary")),
        name="dense_attention",
    )(*args)


BAND_SUB = 256


@functools.lru_cache(maxsize=None)
def _band_bias():
    span = BAND_SUB + 2 * WINDOW
    kr = np.arange(span)[None, :, None]
    qc = (np.arange(GROUP * BAND_SUB) % BAND_SUB)[None, None, :]
    rel = np.arange(3)[:, None, None]
    return np.where(np.abs(kr - qc - rel * WINDOW) <= WINDOW, 0.0, NEG_INF).astype(np.float32)


def _banded_kernel(sink_ref, bound_ref, q_ref, k_ref, vt_ref, kc_ref, vct_ref, bias_ref, o_ref,
                   sl_ref, sc_ref, pl_ref, pc_ref, acc_ref, *, n_sub, s_len):
    i = pl.program_id(2)
    hh = pl.program_id(1)
    sub = BAND_SUB
    m_cols = GROUP * sub
    span = sub + 2 * WINDOW
    n_ctx = kc_ref.shape[2]

    def window(u):
        q = q_ref[0, :, u * sub:(u + 1) * sub, :].reshape(m_cols, HEAD_DIM)
        q0 = (i * n_sub + u) * sub
        start = pl.multiple_of(jnp.clip(q0 - WINDOW, 0, s_len - span), WINDOW)
        return q, start, bias_ref[(q0 - start) // WINDOW]

    shift = jnp.maximum(bound_ref[0, 0], jnp.maximum(sink_ref[hh, 0], sink_ref[hh, 1]) * LOG2E)
    safe = shift <= SAFE_SHIFT

    @pl.when(safe)
    def _():
        col = lax.broadcasted_iota(jnp.int32, (1, m_cols), 1)
        sink = jnp.where(col < sub, sink_ref[hh, 0], sink_ref[hh, 1]) * LOG2E
        den_row = lax.broadcasted_iota(jnp.int32, (LANES, m_cols), 0) == HEAD_DIM
        sink_den = jnp.where(den_row, jnp.exp2(sink - shift), 0.0)
        for u in range(n_sub):
            q, start, bias = window(u)
            pl_u, pc_u = pl_ref.at[u], pc_ref.at[u]
            pl_u[...] = jnp.exp2(_scores_t(k_ref[0, 0, pl.ds(start, span), :], q) + bias - shift).astype(BF16)
            pc_u[...] = jnp.exp2(_scores_t(kc_ref[0, 0], q) - shift).astype(BF16)
            acc = (jnp.dot(vt_ref[0, 0, :, pl.ds(start, span)], pl_u[...], preferred_element_type=F32)
                   + jnp.dot(vct_ref[0, 0], pc_u[...], preferred_element_type=F32) + sink_den)
            o_ref[0, u * sub:(u + 1) * sub] = _attn_finish(acc, sub).astype(o_ref.dtype)

    @pl.when(jnp.logical_not(safe))
    def _():
        stats = []
        for u in range(n_sub):
            q, start, bias = window(u)
            mx_l = _score_stage(k_ref[0, 0, pl.ds(start, span), :], q, sl_ref.at[u], bias=bias)
            mx_c = _score_stage(kc_ref[0, 0], q, sc_ref.at[u])
            stats.append((start, mx_l, mx_c))
        for u in range(n_sub):
            start, mx_l, mx_c = stats[u]
            acc_u = acc_ref.at[u]
            acc_u[...] = jnp.zeros((LANES, m_cols), F32)
            m = jnp.full((1, m_cols), NEG_INF, F32)
            m = _softmax_stage(sl_ref.at[u], pl_ref.at[u], span, vt_ref[0, 0, :, pl.ds(start, span)], mx_l, m, acc_u)
            m = _softmax_stage(sc_ref.at[u], pc_ref.at[u], n_ctx, vct_ref[0, 0], mx_c, m, acc_u)
            acc = _sink_column(sink_ref, hh, sub, m, acc_u[...])
            o_ref[0, u * sub:(u + 1) * sub] = _attn_finish(acc, sub).astype(o_ref.dtype)


def _banded_attention(q, k, vt, kc, vct, sink, bound):
    bx, _, t, _ = q.shape
    n_ctx = kc.shape[2]
    n_sub = 4 if t % (4 * BAND_SUB) == 0 else 2
    tq = n_sub * BAND_SUB
    m_cols = GROUP * BAND_SUB
    span = BAND_SUB + 2 * WINDOW
    bias = jnp.asarray(_band_bias())
    return pl.pallas_call(
        functools.partial(_banded_kernel, n_sub=n_sub, s_len=t),
        grid=(bx, N_KV_HEADS, t // tq),
        in_specs=[pl.BlockSpec(memory_space=pltpu.SMEM), pl.BlockSpec(memory_space=pltpu.SMEM),
                  pl.BlockSpec((1, GROUP, tq, HEAD_DIM), lambda b, h, i: (b, h, i, 0))]
                 + _kv_specs(t) + _kv_specs(n_ctx) + [_full(bias.shape)],
        out_specs=pl.BlockSpec((1, tq, LANES), lambda b, h, i: (b, i, h)),
        out_shape=jax.ShapeDtypeStruct((bx, t, Q_W), BF16),
        scratch_shapes=[pltpu.VMEM((n_sub, span, m_cols), F32), pltpu.VMEM((n_sub, n_ctx, m_cols), F32),
                        pltpu.VMEM((n_sub, span, m_cols), BF16), pltpu.VMEM((n_sub, n_ctx, m_cols), BF16),
                        pltpu.VMEM((n_sub, LANES, m_cols), F32)],
        compiler_params=_params(("arbitrary", "arbitrary", "arbitrary")),
        name="banded_attention",
    )(sink, bound, q, k, vt, kc, vct, bias)


@functools.lru_cache(maxsize=None)
def _filter_features(n):
    j = np.arange(2 * n)
    d = np.where(j <= n, j, 2 * n - j)
    d = np.where(j == n, 0, d)
    bands = (FILT_EMB - 1) // 2
    t01 = np.linspace(0.0, 1.0, n)[d]
    w = 2.0 * np.pi * d.astype(np.float64) / n
    f = np.linspace(1e-4, bands - 1, bands)[None, :]
    feats = np.zeros((2 * n, LANES), np.float64)
    feats[:, 0] = t01
    feats[:, 1:1 + bands] = np.cos(f * w[:, None])
    feats[:, 1 + bands:FILT_EMB] = -np.sin(f * w[:, None])
    feats[:, 64] = t01
    feats[:, 65] = (j < n)
    feats[:, 66] = (j != n)
    return feats.astype(np.float32)


def _filter_kernel(f_ref, w1_ref, b1_ref, w2_ref, b2_ref, w3_ref, b3_ref, w4_ref, fr_ref, dl_ref, k_ref, s_ref):
    f = f_ref[...]
    fr = fr_ref[...]
    mm = lambda a, b: jnp.dot(a, b, preferred_element_type=F32, precision=HI)
    h = jnp.sin(fr * (mm(f, w1_ref[...]) + b1_ref[...]))
    h = jnp.sin(fr * (mm(h, w2_ref[...]) + b2_ref[...]))
    h = jnp.sin(fr * (mm(h, w3_ref[...]) + b3_ref[...]))
    hf = mm(h, w4_ref[...])
    win = jnp.exp(-f[:, 64:65] * dl_ref[...]) + DECAY_SHIFT
    k = jnp.where(f[:, 65:66] > 0.5, hf[:, :CONV_W], hf[:, CONV_W:]) * win * f[:, 66:67]
    k_ref[...] = k

    @pl.when(pl.program_id(0) == 0)
    def _():
        s_ref[...] = jnp.zeros_like(s_ref)

    s_ref[...] += jnp.sum(jnp.abs(k), axis=0, keepdims=True)


def _implicit_filter(n, w1, b1, w2, b2, w3, b3, w4, freq):
    feats = jnp.asarray(_filter_features(n))
    pad = LANES - FILT_WIDTH
    padc = lambda a: jnp.pad(a.reshape(1, -1), ((0, 0), (0, pad)))
    w1p = jnp.pad(w1, ((0, LANES - FILT_EMB), (0, pad)))
    w2p = jnp.pad(w2, ((0, pad), (0, pad)))
    w3p = jnp.pad(w3, ((0, pad), (0, pad)))
    w4p = jnp.pad(w4, ((0, pad), (0, 0)))
    deltas = np.abs(np.linspace(math.log(DECAY_TARGET) / SLOW_DECAY_PCT, math.log(DECAY_TARGET) / FAST_DECAY_PCT,
                                CONV_W)).astype(np.float32).reshape(1, CONV_W)
    tr = min(1024, 2 * n)
    sq = _full((LANES, LANES))
    vec = _full((1, LANES))
    return pl.pallas_call(
        _filter_kernel,
        grid=(2 * n // tr,),
        in_specs=[pl.BlockSpec((tr, LANES), lambda i: (i, 0)), sq, vec, sq, vec, sq, vec,
                  _full((LANES, 2 * CONV_W)), vec, _full((1, CONV_W))],
        out_specs=[pl.BlockSpec((tr, CONV_W), lambda i: (i, 0)), _full((1, CONV_W))],
        out_shape=[jax.ShapeDtypeStruct((2 * n, CONV_W), F32), jax.ShapeDtypeStruct((1, CONV_W), F32)],
        compiler_params=_params(("arbitrary",)),
        name="hyena_filter",
    )(feats, w1p, padc(b1), w2p, padc(b2), w3p, padc(b3), w4p, padc(freq), jnp.asarray(deltas))


def _twiddle(idx, mod):
    ang = 2.0 * np.pi * (idx % mod) / mod
    return np.cos(ang), -np.sin(ang)


def _real_form(mr, mi):
    return np.concatenate([np.concatenate([mr, -mi], -1), np.concatenate([mi, mr], -1)], -2)


@functools.lru_cache(maxsize=None)
def _dft_tables(n2):
    n1 = DFT_N1
    n = n1 * n2
    h = n2 // 2
    a2 = np.arange(n2)
    fr, fi = _twiddle(np.outer(a2, a2), n2)
    m_data = _real_form(fr[:, :h], fi[:, :h])
    m_filt = np.concatenate([fr, fi], 0)
    k2 = a2[:, None, None]
    k1 = np.arange(n1)[None, :, None]
    c1 = np.arange(n1)[None, None, :]
    g = _real_form(*_twiddle(c1 * (n2 * k1 + k2), n))
    a1 = np.arange(n1)
    f1 = _real_form(*_twiddle(np.outer(a1, a1), n1))
    t2 = a1[:, None, None]
    t1 = np.arange(h)[None, :, None]
    j1 = a2[None, None, :]
    hh = _real_form(*_twiddle(j1 * (n1 * t1 + t2), n))
    cast = lambda m: np.asarray(m, dtype=BF16)
    return cast(m_data), cast(m_filt), cast(g), cast(f1), cast(hh)


DFT_ROW_CHUNK = SUBLANES


def _dft_rows_kernel(m_ref, *refs):
    o_ref = refs[-1]
    cols = [jnp.concatenate([r[0, :, t, :] for r in refs[:-1]], axis=0) for t in range(DFT_ROW_CHUNK)]
    rhs = jnp.concatenate(cols, axis=1).astype(BF16)
    out = jnp.dot(m_ref[...], rhs, preferred_element_type=F32)
    half = out.shape[0] // 2
    o_ref[0, 0] = out[:half].astype(o_ref.dtype)
    o_ref[0, 1] = out[half:].astype(o_ref.dtype)


def _dft_stage1(mat, views, pairs, rows, n2, c):
    nin = len(views)
    lc = DFT_ROW_CHUNK * c
    in_specs = [_full(mat.shape)] + [
        pl.BlockSpec((1, rows, DFT_ROW_CHUNK, c), (lambda p, j, a=a: (nin * p + a, 0, j, 0))) for a in range(nin)]
    return pl.pallas_call(
        _dft_rows_kernel,
        grid=(pairs, DFT_N1 // DFT_ROW_CHUNK),
        in_specs=in_specs,
        out_specs=pl.BlockSpec((1, 2, n2, lc), lambda p, j: (p, 0, 0, j)),
        out_shape=jax.ShapeDtypeStruct((pairs, 2, n2, DFT_N1 * c), BF16),
        compiler_params=_params(("arbitrary", "arbitrary")),
        name="dft_stage1",
    )(mat, *views)


def _spectrum_kernel(a_ref, g_ref, sc_ref, o_ref, *, kb):
    for j in range(kb):
        rhs = jnp.concatenate([a_ref[0, 0, j], a_ref[0, 1, j]], axis=0)
        x = jnp.dot(g_ref[j], rhs, preferred_element_type=F32)
        o_ref[0, j] = x[:DFT_N1] * sc_ref[...]
        o_ref[1, j] = x[DFT_N1:] * sc_ref[...]


def _filter_spectrum(a, g, scale, n2, kb):
    c = a.shape[-1]
    return pl.pallas_call(
        functools.partial(_spectrum_kernel, kb=kb),
        grid=(n2 // kb,),
        in_specs=[pl.BlockSpec((1, 2, kb, DFT_N1, c), lambda k: (0, 0, k, 0, 0)),
                  pl.BlockSpec((kb, 2 * DFT_N1, 2 * DFT_N1), lambda k: (k, 0, 0)), _full((1, c))],
        out_specs=pl.BlockSpec((2, kb, DFT_N1, c), lambda k: (0, k, 0, 0)),
        out_shape=jax.ShapeDtypeStruct((2, n2, DFT_N1, c), F32),
        compiler_params=_params(("arbitrary",)),
        name="filter_spectrum",
    )(a, g, scale)


def _dft_mid_kernel(a_ref, g_ref, kh_ref, f_ref, o_ref, *, kb):
    for j in range(kb):
        rhs = jnp.concatenate([a_ref[0, 0, j], a_ref[0, 1, j]], axis=0)
        x = jnp.dot(g_ref[j], rhs, preferred_element_type=F32)
        xr, xi = x[:DFT_N1], x[DFT_N1:]
        kr, ki = kh_ref[0, j], kh_ref[1, j]
        yr = xr * kr - xi * ki
        yi = xr * ki + xi * kr
        v = jnp.concatenate([yr, -yi], axis=0).astype(BF16)
        b = jnp.dot(f_ref[...], v, preferred_element_type=F32)
        o_ref[0, 0, j] = b[:DFT_N1].astype(o_ref.dtype)
        o_ref[0, 1, j] = b[DFT_N1:].astype(o_ref.dtype)


def _dft_mid(a, g, khat, f1, pairs, n2, kb):
    c = a.shape[-1]
    blk = pl.BlockSpec((1, 2, kb, DFT_N1, c), lambda k, p: (p, 0, k, 0, 0))
    return pl.pallas_call(
        functools.partial(_dft_mid_kernel, kb=kb),
        grid=(n2 // kb, pairs),
        in_specs=[blk, pl.BlockSpec((kb, 2 * DFT_N1, 2 * DFT_N1), lambda k, p: (k, 0, 0)),
                  pl.BlockSpec((2, kb, DFT_N1, c), lambda k, p: (0, k, 0, 0)), _full(f1.shape)],
        out_specs=blk,
        out_shape=jax.ShapeDtypeStruct(a.shape, BF16),
        compiler_params=_params(("arbitrary", "arbitrary")),
        name="dft_mid",
    )(a, g, khat, f1)


def _dft_last_kernel(b_ref, h_ref, u_ref, x0_ref, bd_ref, o_ref, *, c):
    half = h_ref.shape[1] // 2
    bd = bd_ref[...]
    for t in range(DFT_ROW_CHUNK):
        cols = slice(t * c, (t + 1) * c)
        rhs = jnp.concatenate([b_ref[0, 0, :, cols], b_ref[0, 1, :, cols]], axis=0)
        v = jnp.dot(h_ref[t], rhs, preferred_element_type=F32)
        o_ref[0, :, t, :] = (v[:half] + u_ref[0, :, t, :] * bd) * x0_ref[0, :, t, :]
        o_ref[1, :, t, :] = (-v[half:] + u_ref[1, :, t, :] * bd) * x0_ref[1, :, t, :]


def _dft_last(bm, hh, u_view, x0_view, bias_d, pairs, n2, c):
    half = n2 // 2
    tc = DFT_ROW_CHUNK
    io = pl.BlockSpec((2, half, tc, c), lambda p, j: (p, 0, j, 0))
    return pl.pallas_call(
        functools.partial(_dft_last_kernel, c=c),
        grid=(pairs, DFT_N1 // tc),
        in_specs=[pl.BlockSpec((1, 2, n2, tc * c), lambda p, j: (p, 0, 0, j)),
                  pl.BlockSpec((tc, n2, 2 * n2), lambda p, j: (j, 0, 0)), io, io, _full((1, c))],
        out_specs=io,
        out_shape=jax.ShapeDtypeStruct((2 * pairs, half, DFT_N1, c), F32),
        compiler_params=_params(("arbitrary", "arbitrary")),
        name="dft_last",
    )(bm, hh, u_view, x0_view, bias_d)


def _long_conv_mixer(u, x0, kfilt, ksum, bias_d):
    b, n, c = u.shape
    n2 = 2 * n // DFT_N1
    half = n2 // 2
    pairs = b // 2
    m_data, m_filt, g, f1, hh = (jnp.asarray(t) for t in _dft_tables(n2))
    kb = min(8, n2)
    scale = 1.0 / (ksum * float(DFT_N1 * n2))
    ka = _dft_stage1(m_filt, [kfilt.reshape(1, n2, DFT_N1, c)], 1, n2, n2, c)
    khat = _filter_spectrum(ka.reshape(1, 2, n2, DFT_N1, c), g, scale, n2, kb)
    u_view = u.reshape(b, half, DFT_N1, c)
    a = _dft_stage1(m_data, [u_view, u_view], pairs, half, n2, c)
    bm = _dft_mid(a.reshape(pairs, 2, n2, DFT_N1, c), g, khat, f1, pairs, n2, kb)
    out = _dft_last(bm.reshape(pairs, 2, n2, DFT_N1 * c), hh, u_view, x0.reshape(b, half, DFT_N1, c),
                    bias_d.reshape(1, c), pairs, n2, c)
    return out.reshape(b, n, c)


@functools.lru_cache(maxsize=None)
def _small_dft_tables(n):
    big = 2 * n
    a = np.arange(big)
    fr, fi = _twiddle(np.outer(a, a), big)
    cast = lambda m: np.asarray(m, dtype=BF16)
    return (cast(np.concatenate([fr, fi], 0)),
            cast(_real_form(fr[:, :n], fi[:, :n])),
            cast(_real_form(fr[:n], fi[:n])))


def _small_conv_kernel(k_ref, ks_ref, u_ref, x0_ref, bd_ref, mf_ref, md_ref, mi_ref, o_ref, *, n):
    big = 2 * n
    kh = jnp.dot(mf_ref[...], k_ref[...].astype(BF16), preferred_element_type=F32) * (1.0 / (ks_ref[...] * big))
    kr, ki = kh[:big], kh[big:]
    rhs = jnp.concatenate([u_ref[0], u_ref[1]], axis=0).astype(BF16)
    x = jnp.dot(md_ref[...], rhs, preferred_element_type=F32)
    xr, xi = x[:big], x[big:]
    v = jnp.concatenate([xr * kr - xi * ki, -(xr * ki + xi * kr)], axis=0).astype(BF16)
    y = jnp.dot(mi_ref[...], v, preferred_element_type=F32)
    bd = bd_ref[...]
    o_ref[0] = ((y[:n] + u_ref[0] * bd) * x0_ref[0]).astype(o_ref.dtype)
    o_ref[1] = ((-y[n:] + u_ref[1] * bd) * x0_ref[1]).astype(o_ref.dtype)


def _small_conv_mixer(u, x0, kfilt, ksum, bias_d):
    b, n, c = u.shape
    mf, md, mi = (jnp.asarray(t) for t in _small_dft_tables(n))
    io = pl.BlockSpec((2, n, c), lambda p: (p, 0, 0))
    return pl.pallas_call(
        functools.partial(_small_conv_kernel, n=n),
        grid=(b // 2,),
        in_specs=[_full((2 * n, c)), _full((1, c)), io, io, _full((1, c)),
                  _full(mf.shape), _full(md.shape), _full(mi.shape)],
        out_specs=io,
        out_shape=jax.ShapeDtypeStruct((b, n, c), BF16),
        compiler_params=_params(("arbitrary",)),
        name="small_conv",
    )(kfilt, ksum, u, x0, bias_d.reshape(1, c), mf, md, mi)


MXU_DIM = 256
FFN_CHUNK_EDGES = (0, 6 * MXU_DIM, D_FF)


def _gelu_tanh(x):
    return 0.5 * x * (1.0 + jnp.tanh(math.sqrt(2.0 / math.pi) * (x + 0.044715 * (x * x * x))))


def _ffn_kernel(x_ref, xp_ref, xn_ref, oa_ref, oap_ref, oan_ref, oc_ref, ocp_ref, ocn_ref, wa_ref, wc_ref, g1_ref,
                sh_ref, sc_ref, gt_ref, g_ref, wu_ref, cw_ref, cb_ref, wd_ref, o_ref):
    i = pl.program_id(1)
    nt = pl.num_programs(1)
    tm = x_ref.shape[1]

    def halo(p_ref, n_ref):
        return jnp.concatenate([p_ref[0][-SUBLANES:], n_ref[0][:SUBLANES]], axis=0)

    def mixer_out(oa, oc):
        return (jnp.dot(oa.astype(BF16), wa_ref[...], preferred_element_type=F32)
                + jnp.dot(oc.astype(BF16), wc_ref[...], preferred_element_type=F32))

    g1 = g1_ref[0]
    xm = x_ref[0] + g1 * mixer_out(oa_ref[0], oc_ref[0])
    xh = halo(xp_ref, xn_ref) + g1 * mixer_out(halo(oap_ref, oan_ref), halo(ocp_ref, ocn_ref))
    ext = jnp.concatenate([xh[:SUBLANES], xm, xh[SUBLANES:]], axis=0)
    r = tm + 2 * SUBLANES
    h = _modulated_norm(ext, g_ref[...], sh_ref[0], sc_ref[0])
    row = lax.broadcasted_iota(jnp.int32, (r, 1), 0)
    inside = jnp.logical_and(jnp.logical_or(i > 0, row >= SUBLANES), jnp.logical_or(i < nt - 1, row < tm + SUBLANES))
    h = jnp.where(inside, h, 0.0).astype(BF16)
    hm = h[SUBLANES:tm + SUBLANES]
    acc = jnp.zeros((tm, D_MODEL), F32)
    for c0, c1 in zip(FFN_CHUNK_EDGES[:-1], FFN_CHUNK_EDGES[1:]):
        a = jnp.dot(h, wu_ref[:, c0:c1], preferred_element_type=F32)
        v = jnp.dot(hm, wu_ref[:, D_FF + c0:D_FF + c1], preferred_element_type=F32)
        ap = pltpu.roll(a, 1, 0)[SUBLANES:tm + SUBLANES]
        an = pltpu.roll(a, r - 1, 0)[SUBLANES:tm + SUBLANES]
        cw = cw_ref[:, c0:c1]
        conv = ap * cw[0:1] + a[SUBLANES:tm + SUBLANES] * cw[1:2] + an * cw[2:3] + cb_ref[:, c0:c1]
        act = (_gelu_tanh(conv) * v).astype(BF16)
        acc = acc + jnp.dot(act, wd_ref[c0:c1, :], preferred_element_type=F32)
    o_ref[0] = xm + gt_ref[0] * acc


def _mixer_out_ffn(x, oa, oc, wo_bf, gate1, sh, sc, gate2, gain, wu_bf, conv_w, conv_b, wd_bf):
    bx, t, d = x.shape
    tm = min(512, t)
    vec = pl.BlockSpec((1, 1, d), lambda b, i: (b, 0, 0))
    once = pl.Buffered(1)
    return pl.pallas_call(
        _ffn_kernel,
        grid=(bx, t // tm),
        in_specs=_halo_specs(tm, t, d) + _halo_specs(tm, t, Q_W, oa.dtype) + _halo_specs(tm, t, CONV_W, oc.dtype) + [
            pl.BlockSpec((Q_W, d), lambda b, i: (0, 0)), pl.BlockSpec((CONV_W, d), lambda b, i: (1, 0)),
            vec, vec, vec, vec, _full((1, d)),
            pl.BlockSpec((d, 2 * D_FF), lambda b, i: (0, 0), pipeline_mode=once),
            _full((3, D_FF)), _full((1, D_FF)),
            pl.BlockSpec((D_FF, d), lambda b, i: (0, 0), pipeline_mode=once)],
        out_specs=pl.BlockSpec((1, tm, d), lambda b, i: (b, i, 0)),
        out_shape=jax.ShapeDtypeStruct((bx, t, d), F32),
        compiler_params=_params(("arbitrary", "arbitrary")),
        name="mixer_out_ffn",
    )(x, x, x, oa, oa, oa, oc, oc, oc, wo_bf, wo_bf, gate1, sh, sc, gate2, gain, wu_bf, conv_w,
      conv_b.reshape(1, D_FF), wd_bf)


@functools.lru_cache(maxsize=None)
def _rope_tables(t):
    pos = np.arange(t)
    n_freq = HEAD_DIM // 4
    inv = ROPE_THETA ** (-np.arange(n_freq, dtype=np.float64) / n_freq)
    ang_r = (pos // GRID_W)[:, None] * inv
    ang_c = (pos % GRID_W)[:, None] * inv
    zero = np.zeros_like(ang_r)
    cos = np.concatenate([np.cos(ang_r)] * 2 + [np.cos(ang_c)] * 2, 1)
    sa = np.concatenate([-np.sin(ang_r), zero, -np.sin(ang_c), zero], 1)
    sb = np.concatenate([zero, np.sin(ang_r), zero, np.sin(ang_c)], 1)
    tile = lambda m: np.tile(m, (1, LANES // HEAD_DIM)).astype(np.float32)
    return tile(cos), tile(sa), tile(sb)


@functools.lru_cache(maxsize=None)
def _identity_rope_tables(t):
    return np.ones((t, LANES), np.float32), np.zeros((t, LANES), np.float32), np.zeros((t, LANES), np.float32)


@functools.lru_cache(maxsize=None)
def _head_block_diag():
    hid = np.arange(Q_W) // HEAD_DIM
    return np.asarray(hid[:, None] == hid[None, :], dtype=BF16)


def kernel(x, c, ctx, c_ctx, ada_w, ada_b, norm_mix, norm_ffn, mix_w_in, mix_w_out, attn_q_norm, attn_k_norm,
           swa_sink, hy_conv_w, hy_conv_b, hy_w1, hy_b1, hy_w2, hy_b2, hy_w3, hy_b3, hy_w4, hy_freq, hy_bias_d,
           sc_conv_w, ffn_w_up, ffn_conv_w, ffn_conv_b, ffn_w_down):
    b, s, d = x.shape
    s_ctx = ctx.shape[1]
    depth = ada_w.shape[0]
    assert d == D_MODEL and b % 2 == 0 and b + 1 <= SUBLANES and s % 1024 == 0 and s_ctx % SUBLANES == 0

    cvec = jnp.concatenate([c, c_ctx[None, :], jnp.zeros((SUBLANES - b - 1, d), F32)], axis=0)
    mods = _mods(cvec, ada_w, ada_b)
    rope = [jnp.asarray(t) for t in _rope_tables(s)]
    rope_ctx = [jnp.asarray(t) for t in _identity_rope_tables(s_ctx)]
    bd = jnp.asarray(_head_block_diag())
    xc = ctx

    for i in range(depth):
        last = i == depth - 1
        j = i // 2
        lat = [mods[i, :b, k * d:(k + 1) * d][:, None, :] for k in range(6)]
        cx = [jnp.broadcast_to(mods[i, b, k * d:(k + 1) * d][None, None, :], (b, 1, d)) for k in range(6)]
        w_in = mix_w_in[i].astype(BF16)
        w_out = mix_w_out[i].astype(BF16)
        w_up = ffn_w_up[i].astype(BF16)
        w_down = ffn_w_down[i].astype(BF16)
        g_mix = norm_mix[i].reshape(1, d)
        g_ffn = norm_ffn[i].reshape(1, d)
        qg = jnp.tile(attn_q_norm[i], N_Q_HEADS).reshape(1, Q_W)
        kg = jnp.tile(attn_k_norm[i], N_KV_HEADS).reshape(1, KV_W)

        conv = (hy_conv_w[j], hy_conv_b[j]) if i % 2 == 0 else (sc_conv_w[j],)
        q, k, vt, *mix = _inproj(x, lat[0], lat[1], g_mix, w_in, qg, kg, bd, *rope, *conv)
        qc, kc, vct, *mixc = _inproj(xc, cx[0], cx[1], g_mix, w_in, qg, kg, bd, *rope_ctx, *conv)

        bound = _score_bound(attn_q_norm[i], attn_k_norm[i])
        if i % 2 == 0:
            fargs = (hy_w1[j], hy_b1[j], hy_w2[j], hy_b2[j], hy_w3[j], hy_b3[j], hy_w4[j], hy_freq[j])
            o_attn = _dense_attention(q, k, vt, kc, vct, bound=bound)
            kf, ks = _implicit_filter(s, *fargs)
            o_conv = _long_conv_mixer(*mix, kf, ks, hy_bias_d[j])
            if not last:
                oc_attn = _dense_attention(qc, kc, vct, bound=bound)
                kfc, ksc = _implicit_filter(s_ctx, *fargs)
                oc_conv = _small_conv_mixer(*mixc, kfc, ksc, hy_bias_d[j])
        else:
            sink = swa_sink[j].reshape(N_KV_HEADS, GROUP)
            o_attn = _banded_attention(q, k, vt, kc, vct, sink, bound)
            o_conv = mix[0]
            if not last:
                oc_attn = _dense_attention(qc, kc, vct, sink=sink)
                oc_conv = mixc[0]

        ffn = (g_ffn, w_up, ffn_conv_w[i], ffn_conv_b[i], w_down)
        x = _mixer_out_ffn(x, o_attn, o_conv, w_out, lat[2], lat[3], lat[4], lat[5], *ffn)
        if not last:
            xc = _mixer_out_ffn(xc, oc_attn, oc_conv, w_out, cx[2], cx[3], cx[4], cx[5], *ffn)
    return x
```

```python
import functools
import math

import numpy as np
import jax
import jax.numpy as jnp
from jax import lax
from jax.experimental import pallas as pl
from jax.experimental.pallas import tpu as pltpu

F32 = jnp.float32
BF16 = jnp.bfloat16
HI = lax.Precision.HIGHEST

D_MODEL = 1024
GRID_W = 64
HEAD_DIM = 64
N_Q_HEADS = 8
N_KV_HEADS = 4
GROUP = N_Q_HEADS // N_KV_HEADS
Q_W = N_Q_HEADS * HEAD_DIM
KV_W = N_KV_HEADS * HEAD_DIM
QKV_W = Q_W + 2 * KV_W
CONV_W = D_MODEL // 2
MIX_IN_W = QKV_W + 3 * CONV_W
WINDOW = 128
ROPE_THETA = 10000.0
FILT_EMB = 33
FILT_WIDTH = 64
DECAY_TARGET = 1e-2
FAST_DECAY_PCT = 0.3
SLOW_DECAY_PCT = 1.5
DECAY_SHIFT = 0.05
D_FF = 2816
NEG_INF = -1e30
RMS_EPS = 1e-6
LOG2E = 1.4426950408889634
Q_SCALE = HEAD_DIM ** -0.5 * LOG2E

LANES = 128
SUBLANES = 8
DFT_N1 = 128
VMEM_LIMIT_MB = 56
PROJ_SUB_ROWS = 128


def _params(sem, vmem_mb=VMEM_LIMIT_MB):
    return pltpu.CompilerParams(dimension_semantics=sem, vmem_limit_bytes=vmem_mb * 1024 * 1024)


def _full(shape):
    nd = len(shape)
    return pl.BlockSpec(shape, lambda *_: (0,) * nd)


def _mods_kernel(c_ref, w_ref, b_ref, o_ref):
    c = c_ref[...]
    s = c / (1.0 + jnp.exp(-c))
    o_ref[0] = jnp.dot(s, w_ref[0], preferred_element_type=F32, precision=HI) + b_ref[0]


def _mods(cvec, ada_w, ada_b):
    depth, d, n6 = ada_w.shape
    tn = 1536
    return pl.pallas_call(
        _mods_kernel,
        grid=(depth, n6 // tn),
        in_specs=[_full((SUBLANES, d)),
                  pl.BlockSpec((1, d, tn), lambda l, j: (l, 0, j)),
                  pl.BlockSpec((1, 1, tn), lambda l, j: (l, 0, j))],
        out_specs=pl.BlockSpec((1, SUBLANES, tn), lambda l, j: (l, 0, j)),
        out_shape=jax.ShapeDtypeStruct((depth, SUBLANES, n6), F32),
        compiler_params=_params(("arbitrary", "arbitrary")),
        name="ada_mods",
    )(cvec, ada_w, ada_b.reshape(depth, 1, n6))


def _modulated_norm(x, gain, shift, scale):
    ms = jnp.mean(x * x, axis=-1, keepdims=True)
    return (x * lax.rsqrt(ms + RMS_EPS)) * gain * (1.0 + scale) + shift


def _cast_kernel(w_ref, o_ref):
    o_ref[...] = w_ref[0].astype(o_ref.dtype)


def _layer_weight_bf16(w, layer):
    _, rows, cols = w.shape
    tr = 256
    return pl.pallas_call(
        _cast_kernel,
        grid=(rows // tr,),
        in_specs=[pl.BlockSpec((1, tr, cols), lambda r: (layer, r, 0))],
        out_specs=pl.BlockSpec((tr, cols), lambda r: (r, 0)),
        out_shape=jax.ShapeDtypeStruct((rows, cols), BF16),
        compiler_params=_params(("arbitrary",)),
        name="weight_bf16",
    )(w)


def _halo_specs(tm, t, width, dtype=F32):
    rows = SUBLANES * 4 // jnp.dtype(dtype).itemsize
    nb = tm // rows
    last = t // rows - 1
    return [pl.BlockSpec((1, tm, width), lambda b, i: (b, i, 0)),
            pl.BlockSpec((1, rows, width), lambda b, i: (b, jnp.maximum(i * nb - 1, 0), 0)),
            pl.BlockSpec((1, rows, width), lambda b, i: (b, jnp.minimum((i + 1) * nb, last), 0))]


def _inproj_kernel(*refs, mode):
    (x_ref, xp_ref, xn_ref, sh_ref, sc_ref, g_ref, w_ref, qg_ref, kg_ref, bd_ref,
     cos_ref, sa_ref, sb_ref, cw_ref) = refs[:14]
    if mode == "hyena":
        cb_ref, q_ref, k_ref, vt_ref, u_ref, x0_ref = refs[14:]
    else:
        q_ref, k_ref, vt_ref, oc_ref = refs[14:]
    i = pl.program_id(1)
    nt = pl.num_programs(1)
    tm = x_ref.shape[1]
    sub = min(PROJ_SUB_ROWS, tm)
    gain, shift, scale = g_ref[...], sh_ref[0], sc_ref[0]

    xh = jnp.concatenate([xp_ref[0], xn_ref[0]], axis=0)
    zh = jnp.dot(_modulated_norm(xh, gain, shift, scale).astype(BF16), w_ref[:, QKV_W:], preferred_element_type=F32)
    before = jnp.where(i > 0, zh[:SUBLANES], 0.0)
    after = jnp.where(i < nt - 1, zh[SUBLANES:], 0.0)
    cw = cw_ref[...]

    def conv_stage(ext, rows):
        n = sub + 2 * SUBLANES
        mid = slice(SUBLANES, sub + SUBLANES)
        if mode == "hyena":
            c = (pltpu.roll(ext, 1, 0)[mid] * cw[0:1] + ext[mid] * cw[1:2] + pltpu.roll(ext, n - 1, 0)[mid] * cw[2:3]
                 + cb_ref[...])
            x0_ref[0, rows] = c[:, :CONV_W]
            u_ref[0, rows] = c[:, 2 * CONV_W:] * c[:, CONV_W:2 * CONV_W]
        else:
            pr = ext[:, CONV_W:2 * CONV_W] * ext[:, 2 * CONV_W:]
            conv = pltpu.roll(pr, 1, 0)[mid] * cw[0:1] + pr[mid] * cw[1:2] + pltpu.roll(pr, n - 1, 0)[mid] * cw[2:3]
            oc_ref[0, rows] = (ext[mid, :CONV_W] * conv).astype(oc_ref.dtype)

    def head_norm_rope(t, hgain, bd, cos, sa, sb):
        ssq = jnp.dot((t * t).astype(BF16), bd, preferred_element_type=F32)
        t = t * lax.rsqrt(ssq * (1.0 / HEAD_DIM) + RMS_EPS) * hgain
        outs = []
        for j in range(t.shape[1] // LANES):
            tj = t[:, j * LANES:(j + 1) * LANES]
            outs.append(tj * cos + pltpu.roll(tj, LANES - 16, 1) * sa + pltpu.roll(tj, 16, 1) * sb)
        return outs

    held = None
    for r0 in range(0, tm, sub):
        rows = slice(r0, r0 + sub)
        h = _modulated_norm(x_ref[0, rows], gain, shift, scale)
        hb = h.astype(BF16)
        z = jnp.dot(hb, w_ref[:, :QKV_W], preferred_element_type=F32)
        zc = jnp.dot(hb, w_ref[:, QKV_W:], preferred_element_type=F32)
        if held is not None:
            conv_stage(jnp.concatenate([before, held[0], zc[:SUBLANES]], axis=0), held[1])
            before = held[0][-SUBLANES:]
        held = (zc, rows)
        tabs = (cos_ref[rows], sa_ref[rows], sb_ref[rows])
        qs = head_norm_rope(z[:, :Q_W], qg_ref[...], bd_ref[...], *tabs)
        ks = head_norm_rope(z[:, Q_W:Q_W + KV_W], kg_ref[...], bd_ref[:KV_W, :KV_W], *tabs)
        for j, qj in enumerate(qs):
            qj = (qj * Q_SCALE).astype(BF16)
            q_ref[0, 2 * j, rows] = qj[:, :HEAD_DIM]
            q_ref[0, 2 * j + 1, rows] = qj[:, HEAD_DIM:]
        for j, kj in enumerate(ks):
            kj = kj.astype(BF16)
            k_ref[0, 2 * j, rows] = kj[:, :HEAD_DIM]
            k_ref[0, 2 * j + 1, rows] = kj[:, HEAD_DIM:]
        vt = z[:, Q_W + KV_W:QKV_W].T
        ones_row = (lax.broadcasted_iota(jnp.int32, (LANES - HEAD_DIM, sub), 0) == 0).astype(F32)
        for hh in range(N_KV_HEADS):
            vt_ref[0, hh, :, rows] = jnp.concatenate([vt[hh * HEAD_DIM:(hh + 1) * HEAD_DIM], ones_row],
                                                     axis=0).astype(BF16)

    conv_stage(jnp.concatenate([before, held[0], after], axis=0), held[1])


def _inproj(x, sh, sc, gain, w_bf, qg, kg, bd, cos, sa, sb, conv_w, conv_b=None):
    bx, t, d = x.shape
    tm = min(512, t)
    mode = "hyena" if conv_b is not None else "short"
    vec = pl.BlockSpec((1, 1, d), lambda b, i: (b, 0, 0))
    tab = pl.BlockSpec((tm, LANES), lambda b, i: (i, 0))
    row = lambda w: pl.BlockSpec((1, tm, w), lambda b, i: (b, i, 0))
    in_specs = _halo_specs(tm, t, d) + [vec, vec, _full((1, d)), _full((d, MIX_IN_W)), _full((1, Q_W)),
                                        _full((1, KV_W)), _full((Q_W, Q_W)), tab, tab, tab, _full(conv_w.shape)]
    args = [x, x, x, sh, sc, gain, w_bf, qg, kg, bd, cos, sa, sb, conv_w]
    out_specs = [pl.BlockSpec((1, N_Q_HEADS, tm, HEAD_DIM), lambda b, i: (b, 0, i, 0)),
                 pl.BlockSpec((1, N_KV_HEADS, tm, HEAD_DIM), lambda b, i: (b, 0, i, 0)),
                 pl.BlockSpec((1, N_KV_HEADS, LANES, tm), lambda b, i: (b, 0, 0, i))]
    out_shape = [jax.ShapeDtypeStruct((bx, N_Q_HEADS, t, HEAD_DIM), BF16),
                 jax.ShapeDtypeStruct((bx, N_KV_HEADS, t, HEAD_DIM), BF16),
                 jax.ShapeDtypeStruct((bx, N_KV_HEADS, LANES, t), BF16)]
    if mode == "hyena":
        in_specs.append(_full((1, 3 * CONV_W)))
        args.append(conv_b.reshape(1, 3 * CONV_W))
        out_specs += [row(CONV_W), row(CONV_W)]
        out_shape += [jax.ShapeDtypeStruct((bx, t, CONV_W), F32)] * 2
    else:
        out_specs.append(row(CONV_W))
        out_shape.append(jax.ShapeDtypeStruct((bx, t, CONV_W), BF16))
    return pl.pallas_call(
        functools.partial(_inproj_kernel, mode=mode),
        grid=(bx, t // tm),
        in_specs=in_specs,
        out_specs=out_specs,
        out_shape=out_shape,
        compiler_params=_params(("arbitrary", "arbitrary")),
        name="in_proj",
    )(*args)


def _score_bound(q_gain, k_gain):
    b = HEAD_DIM * Q_SCALE * BOUND_MARGIN * jnp.max(jnp.abs(q_gain)) * jnp.max(jnp.abs(k_gain))
    return b.reshape(1, 1).astype(F32)


def _scores_t(ks, q):
    return lax.dot_general(ks, q, (((1,), (1,)), ((), ())), preferred_element_type=F32)


def _sink_column(sink_ref, hh, tq, m, acc):
    col = lax.broadcasted_iota(jnp.int32, m.shape, 1)
    sink = jnp.where(col < tq, sink_ref[hh, 0], sink_ref[hh, 1]) * LOG2E
    m_new = jnp.maximum(m, sink)
    den_row = lax.broadcasted_iota(jnp.int32, acc.shape, 0) == HEAD_DIM
    return acc * jnp.exp2(m - m_new) + jnp.where(den_row, jnp.exp2(sink - m_new), 0.0)


def _attn_finish(acc, tq):
    o = (acc / acc[HEAD_DIM:HEAD_DIM + 1]).T
    return jnp.concatenate([o[:tq, :HEAD_DIM], o[tq:, :HEAD_DIM]], axis=1)


EXP_CHUNK_ELEMS = 32 * SUBLANES * LANES


def _score_stage(ks, q, s_ref, bias=None):
    s = _scores_t(ks, q)
    if bias is not None:
        s = s + bias
    s_ref[0:s.shape[0]] = s
    return jnp.max(s, axis=0, keepdims=True)


def _softmax_stage(s_ref, p_ref, n, vt, mx, m, acc_ref):
    m_new = jnp.maximum(m, mx)
    rows = EXP_CHUNK_ELEMS // m.shape[1]
    for c in range(0, n, rows):
        p_ref[c:c + rows] = jnp.exp2(s_ref[c:c + rows] - m_new).astype(BF16)
    acc_ref[...] = acc_ref[...] * jnp.exp2(m - m_new) + jnp.dot(vt, p_ref[0:n], preferred_element_type=F32)
    return m_new


BOUNDED_TILES = 4
BOUND_MARGIN = 1.01
SAFE_SHIFT = 40.0


def _flash_exact(q, k_tile, vt_tile, ctx, sa_ref, sb_ref, p_ref, acc_ref, *, tk, n_main):
    pa_ref, pb_ref = p_ref.at[0:tk], p_ref.at[tk:2 * tk]
    acc_ref[...] = jnp.zeros_like(acc_ref)
    m = jnp.full((1, q.shape[0]), NEG_INF, F32)
    mx_a = _score_stage(k_tile(0), q, sa_ref)
    if n_main > 1:
        def body(i, carry):
            m, mx_a = carry
            mx_b = _score_stage(k_tile(2 * i + 1), q, sb_ref)
            m = _softmax_stage(sa_ref, pa_ref, tk, vt_tile(2 * i), mx_a, m, acc_ref)
            mx_a = _score_stage(k_tile(2 * i + 2), q, sa_ref)
            m = _softmax_stage(sb_ref, pb_ref, tk, vt_tile(2 * i + 1), mx_b, m, acc_ref)
            return m, mx_a

        m, mx_a = lax.fori_loop(0, n_main // 2 - 1, body, (m, mx_a))
        mx_b = _score_stage(k_tile(n_main - 1), q, sb_ref)
        m = _softmax_stage(sa_ref, pa_ref, tk, vt_tile(n_main - 2), mx_a, m, acc_ref)
        last = (sb_ref, pb_ref, mx_b)
        spare = (sa_ref, pa_ref)
    else:
        last = (sa_ref, pa_ref, mx_a)
        spare = (sb_ref, pb_ref)
    if ctx is not None:
        mx_c = _score_stage(ctx[0], q, spare[0])
    m = _softmax_stage(last[0], last[1], tk, vt_tile(n_main - 1), last[2], m, acc_ref)
    if ctx is not None:
        m = _softmax_stage(spare[0], spare[1], ctx[0].shape[0], ctx[1], mx_c, m, acc_ref)
    return m


def _flash_bounded(q, k_ref, vt_ref, ctx, bound, p_ref, acc_ref, *, tk, n_main):
    step = BOUNDED_TILES * tk if n_main % BOUNDED_TILES == 0 else tk

    def group(j):
        off = pl.multiple_of(j * step, step)
        for c in range(0, step, tk):
            s_t = _scores_t(k_ref[0, 0, pl.ds(off + c, tk), :], q)
            p_ref[c:c + tk] = jnp.exp2(s_t - bound).astype(BF16)
        return jnp.dot(vt_ref[0, 0, :, pl.ds(off, step)], p_ref[0:step], preferred_element_type=F32)

    first = group(0)
    if ctx is not None:
        pc_ref = p_ref.at[BOUNDED_TILES * tk:BOUNDED_TILES * tk + ctx[0].shape[0]]
        pc_ref[...] = jnp.exp2(_scores_t(ctx[0], q) - bound).astype(BF16)
        first = first + jnp.dot(ctx[1], pc_ref[...], preferred_element_type=F32)
    acc_ref[...] = first

    def body(j, carry):
        acc_ref[...] += group(j)
        return carry

    lax.fori_loop(1, n_main * tk // step, body, 0)


def _flash_kernel(*refs, tq, tk, n_main, has_ctx, has_sink):
    sa_ref, sb_ref, p_ref, acc_ref = refs[-4:]
    o_ref = refs[-5]
    refs = list(refs[:-5])
    sink_ref = refs.pop(0) if has_sink else None
    bound_ref = None if has_sink else refs.pop(0)
    q_ref, k_ref, vt_ref = refs[:3]
    m_cols = GROUP * tq
    q = q_ref[0].reshape(m_cols, HEAD_DIM)
    ctx = (refs[3][0, 0], refs[4][0, 0]) if has_ctx else None

    def k_tile(j):
        return k_ref[0, 0, pl.ds(pl.multiple_of(j * tk, tk), tk), :]

    def vt_tile(j):
        return vt_ref[0, 0, :, pl.ds(pl.multiple_of(j * tk, tk), tk)]

    exact = functools.partial(_flash_exact, q, k_tile, vt_tile, ctx, sa_ref, sb_ref, p_ref, acc_ref,
                              tk=tk, n_main=n_main)
    if has_sink:
        acc = _sink_column(sink_ref, pl.program_id(1), tq, exact(), acc_ref[...])
    else:
        bound = bound_ref[0, 0]
        safe = bound <= SAFE_SHIFT

        @pl.when(safe)
        def _():
            _flash_bounded(q, k_ref, vt_ref, ctx, bound, p_ref, acc_ref, tk=tk, n_main=n_main)

        @pl.when(jnp.logical_not(safe))
        def _():
            exact()

        acc = acc_ref[...]
    o_ref[0] = _attn_finish(acc, tq).astype(o_ref.dtype)


def _kv_specs(n):
    return [pl.BlockSpec((1, 1, n, HEAD_DIM), lambda b, h, i: (b, h, 0, 0)),
            pl.BlockSpec((1, 1, LANES, n), lambda b, h, i: (b, h, 0, 0))]


def _dense_attention(q, k, vt, kc=None, vct=None, *, bound=None, sink=None):
    bx, _, t, _ = q.shape
    s = k.shape[2]
    tq = min(512, t)
    tk = min(512, s)
    has_ctx = kc is not None
    has_sink = sink is not None
    in_specs = [pl.BlockSpec((1, GROUP, tq, HEAD_DIM), lambda b, h, i: (b, h, i, 0))] + _kv_specs(s)
    args = [q, k, vt]
    if has_ctx:
        in_specs += _kv_specs(kc.shape[2])
        args += [kc, vct]
    in_specs = [pl.BlockSpec(memory_space=pltpu.SMEM)] + in_specs
    args = [sink if has_sink else bound] + args
    n_main = s // tk
    assert n_main == 1 or n_main % 2 == 0
    n_ctx = kc.shape[2] if has_ctx else 0
    assert n_ctx <= tk
    m_cols = GROUP * tq
    return pl.pallas_call(
        functools.partial(_flash_kernel, tq=tq, tk=tk, n_main=n_main, has_ctx=has_ctx, has_sink=has_sink),
        grid=(bx, N_KV_HEADS, t // tq),
        in_specs=in_specs,
        out_specs=pl.BlockSpec((1, tq, LANES), lambda b, h, i: (b, i, h)),
        out_shape=jax.ShapeDtypeStruct((bx, t, Q_W), BF16),
        scratch_shapes=[pltpu.VMEM((tk, m_cols), F32), pltpu.VMEM((tk, m_cols), F32),
                        pltpu.VMEM((BOUNDED_TILES * tk + n_ctx, m_cols), BF16), pltpu.VMEM((LANES, m_cols), F32)],
        compiler_params=_params(("arbitrary", "arbitrary", "arbitrary")),
        name="dense_attention",
    )(*args)


BAND_SUB = 256


@functools.lru_cache(maxsize=None)
def _band_bias():
    span = BAND_SUB + 2 * WINDOW
    kr = np.arange(span)[None, :, None]
    qc = (np.arange(GROUP * BAND_SUB) % BAND_SUB)[None, None, :]
    rel = np.arange(3)[:, None, None]
    return np.where(np.abs(kr - qc - rel * WINDOW) <= WINDOW, 0.0, NEG_INF).astype(np.float32)


def _banded_kernel(sink_ref, bound_ref, q_ref, k_ref, vt_ref, kc_ref, vct_ref, bias_ref, o_ref,
                   sl_ref, sc_ref, pl_ref, pc_ref, acc_ref, *, n_sub, s_len):
    i = pl.program_id(2)
    hh = pl.program_id(1)
    sub = BAND_SUB
    m_cols = GROUP * sub
    span = sub + 2 * WINDOW
    n_ctx = kc_ref.shape[2]

    def window(u):
        q = q_ref[0, :, u * sub:(u + 1) * sub, :].reshape(m_cols, HEAD_DIM)
        q0 = (i * n_sub + u) * sub
        start = pl.multiple_of(jnp.clip(q0 - WINDOW, 0, s_len - span), WINDOW)
        return q, start, bias_ref[(q0 - start) // WINDOW]

    shift = jnp.maximum(bound_ref[0, 0], jnp.maximum(sink_ref[hh, 0], sink_ref[hh, 1]) * LOG2E)
    safe = shift <= SAFE_SHIFT

    @pl.when(safe)
    def _():
        col = lax.broadcasted_iota(jnp.int32, (1, m_cols), 1)
        sink = jnp.where(col < sub, sink_ref[hh, 0], sink_ref[hh, 1]) * LOG2E
        den_row = lax.broadcasted_iota(jnp.int32, (LANES, m_cols), 0) == HEAD_DIM
        sink_den = jnp.where(den_row, jnp.exp2(sink - shift), 0.0)
        for u in range(n_sub):
            q, start, bias = window(u)
            pl_u, pc_u = pl_ref.at[u], pc_ref.at[u]
            pl_u[...] = jnp.exp2(_scores_t(k_ref[0, 0, pl.ds(start, span), :], q) + bias - shift).astype(BF16)
            pc_u[...] = jnp.exp2(_scores_t(kc_ref[0, 0], q) - shift).astype(BF16)
            acc = (jnp.dot(vt_ref[0, 0, :, pl.ds(start, span)], pl_u[...], preferred_element_type=F32)
                   + jnp.dot(vct_ref[0, 0], pc_u[...], preferred_element_type=F32) + sink_den)
            o_ref[0, u * sub:(u + 1) * sub] = _attn_finish(acc, sub).astype(o_ref.dtype)

    @pl.when(jnp.logical_not(safe))
    def _():
        stats = []
        for u in range(n_sub):
            q, start, bias = window(u)
            mx_l = _score_stage(k_ref[0, 0, pl.ds(start, span), :], q, sl_ref.at[u], bias=bias)
            mx_c = _score_stage(kc_ref[0, 0], q, sc_ref.at[u])
            stats.append((start, mx_l, mx_c))
        for u in range(n_sub):
            start, mx_l, mx_c = stats[u]
            acc_u = acc_ref.at[u]
            acc_u[...] = jnp.zeros((LANES, m_cols), F32)
            m = jnp.full((1, m_cols), NEG_INF, F32)
            m = _softmax_stage(sl_ref.at[u], pl_ref.at[u], span, vt_ref[0, 0, :, pl.ds(start, span)], mx_l, m, acc_u)
            m = _softmax_stage(sc_ref.at[u], pc_ref.at[u], n_ctx, vct_ref[0, 0], mx_c, m, acc_u)
            acc = _sink_column(sink_ref, hh, sub, m, acc_u[...])
            o_ref[0, u * sub:(u + 1) * sub] = _attn_finish(acc, sub).astype(o_ref.dtype)


def _banded_attention(q, k, vt, kc, vct, sink, bound):
    bx, _, t, _ = q.shape
    n_ctx = kc.shape[2]
    n_sub = 4 if t % (4 * BAND_SUB) == 0 else 2
    tq = n_sub * BAND_SUB
    m_cols = GROUP * BAND_SUB
    span = BAND_SUB + 2 * WINDOW
    bias = jnp.asarray(_band_bias())
    return pl.pallas_call(
        functools.partial(_banded_kernel, n_sub=n_sub, s_len=t),
        grid=(bx, N_KV_HEADS, t // tq),
        in_specs=[pl.BlockSpec(memory_space=pltpu.SMEM), pl.BlockSpec(memory_space=pltpu.SMEM),
                  pl.BlockSpec((1, GROUP, tq, HEAD_DIM), lambda b, h, i: (b, h, i, 0))]
                 + _kv_specs(t) + _kv_specs(n_ctx) + [_full(bias.shape)],
        out_specs=pl.BlockSpec((1, tq, LANES), lambda b, h, i: (b, i, h)),
        out_shape=jax.ShapeDtypeStruct((bx, t, Q_W), BF16),
        scratch_shapes=[pltpu.VMEM((n_sub, span, m_cols), F32), pltpu.VMEM((n_sub, n_ctx, m_cols), F32),
                        pltpu.VMEM((n_sub, span, m_cols), BF16), pltpu.VMEM((n_sub, n_ctx, m_cols), BF16),
                        pltpu.VMEM((n_sub, LANES, m_cols), F32)],
        compiler_params=_params(("arbitrary", "arbitrary", "arbitrary")),
        name="banded_attention",
    )(sink, bound, q, k, vt, kc, vct, bias)


@functools.lru_cache(maxsize=None)
def _filter_features(n):
    j = np.arange(2 * n)
    d = np.where(j <= n, j, 2 * n - j)
    d = np.where(j == n, 0, d)
    bands = (FILT_EMB - 1) // 2
    t01 = np.linspace(0.0, 1.0, n)[d]
    w = 2.0 * np.pi * d.astype(np.float64) / n
    f = np.linspace(1e-4, bands - 1, bands)[None, :]
    feats = np.zeros((2 * n, LANES), np.float64)
    feats[:, 0] = t01
    feats[:, 1:1 + bands] = np.cos(f * w[:, None])
    feats[:, 1 + bands:FILT_EMB] = -np.sin(f * w[:, None])
    feats[:, 64] = t01
    feats[:, 65] = (j < n)
    feats[:, 66] = (j != n)
    return feats.astype(np.float32)


def _filter_kernel(f_ref, w1_ref, b1_ref, w2_ref, b2_ref, w3_ref, b3_ref, w4_ref, fr_ref, dl_ref, k_ref, s_ref):
    f = f_ref[...]
    fr = fr_ref[...]
    mm = lambda a, b: jnp.dot(a, b, preferred_element_type=F32, precision=HI)
    h = jnp.sin(fr * (mm(w1_ref[...], f) + b1_ref[...]))
    h = jnp.sin(fr * (mm(w2_ref[...], h) + b2_ref[...]))
    h = jnp.sin(fr * (mm(w3_ref[...], h) + b3_ref[...]))
    hf = mm(w4_ref[...], h)
    win = jnp.exp(-dl_ref[...] * f[64:65]) + DECAY_SHIFT
    k = (jnp.where(f[65:66] > 0.5, hf[:CONV_W], hf[CONV_W:]) * win * f[66:67]).T
    k_ref[...] = k

    @pl.when(pl.program_id(0) == 0)
    def _():
        s_ref[...] = jnp.zeros_like(s_ref)

    s_ref[...] += jnp.sum(jnp.abs(k), axis=0, keepdims=True)


def _implicit_filter(n, w1, b1, w2, b2, w3, b3, w4, freq):
    feats_t = jnp.asarray(np.ascontiguousarray(_filter_features(n).T))
    col = lambda a: a.reshape(-1, 1)
    w1t = jnp.pad(w1.T, ((0, 0), (0, LANES - FILT_EMB)))
    deltas = np.abs(np.linspace(math.log(DECAY_TARGET) / SLOW_DECAY_PCT, math.log(DECAY_TARGET) / FAST_DECAY_PCT,
                                CONV_W)).astype(np.float32).reshape(CONV_W, 1)
    tr = min(1024, 2 * n)
    sq = _full((FILT_WIDTH, FILT_WIDTH))
    vec = _full((FILT_WIDTH, 1))
    return pl.pallas_call(
        _filter_kernel,
        grid=(2 * n // tr,),
        in_specs=[pl.BlockSpec((LANES, tr), lambda i: (0, i)), _full((FILT_WIDTH, LANES)), vec, sq, vec, sq, vec,
                  _full((2 * CONV_W, FILT_WIDTH)), vec, _full((CONV_W, 1))],
        out_specs=[pl.BlockSpec((tr, CONV_W), lambda i: (i, 0)), _full((1, CONV_W))],
        out_shape=[jax.ShapeDtypeStruct((2 * n, CONV_W), F32), jax.ShapeDtypeStruct((1, CONV_W), F32)],
        compiler_params=_params(("arbitrary",)),
        name="hyena_filter",
    )(feats_t, w1t, col(b1), w2.T, col(b2), w3.T, col(b3), w4.T, col(freq), jnp.asarray(deltas))


def _twiddle(idx, mod):
    ang = 2.0 * np.pi * (idx % mod) / mod
    return np.cos(ang), -np.sin(ang)


def _real_form(mr, mi):
    return np.concatenate([np.concatenate([mr, -mi], -1), np.concatenate([mi, mr], -1)], -2)


@functools.lru_cache(maxsize=None)
def _dft_tables(n2):
    n1 = DFT_N1
    n = n1 * n2
    h = n2 // 2
    a2 = np.arange(n2)
    fr, fi = _twiddle(np.outer(a2, a2), n2)
    m_data = _real_form(fr[:, :h], fi[:, :h])
    m_filt = np.concatenate([fr, fi], 0)
    k2 = a2[:, None, None]
    k1 = np.arange(n1)[None, :, None]
    c1 = np.arange(n1)[None, None, :]
    g = _real_form(*_twiddle(c1 * (n2 * k1 + k2), n))
    a1 = np.arange(n1)
    f1 = _real_form(*_twiddle(np.outer(a1, a1), n1))
    t2 = a1[:, None, None]
    t1 = np.arange(h)[None, :, None]
    j1 = a2[None, None, :]
    hh = _real_form(*_twiddle(j1 * (n1 * t1 + t2), n))
    cast = lambda m: np.asarray(m, dtype=BF16)
    return cast(m_data), cast(m_filt), cast(g), cast(f1), cast(hh)


DFT_ROW_CHUNK = SUBLANES


def _dft_rows_kernel(m_ref, *refs):
    o_ref = refs[-1]
    cols = [jnp.concatenate([r[0, :, t, :] for r in refs[:-1]], axis=0) for t in range(DFT_ROW_CHUNK)]
    rhs = jnp.concatenate(cols, axis=1).astype(BF16)
    out = jnp.dot(m_ref[...], rhs, preferred_element_type=F32)
    half = out.shape[0] // 2
    o_ref[0, 0] = out[:half].astype(o_ref.dtype)
    o_ref[0, 1] = out[half:].astype(o_ref.dtype)


def _dft_stage1(mat, views, pairs, rows, n2, c):
    nin = len(views)
    lc = DFT_ROW_CHUNK * c
    in_specs = [_full(mat.shape)] + [
        pl.BlockSpec((1, rows, DFT_ROW_CHUNK, c), (lambda p, j, a=a: (nin * p + a, 0, j, 0))) for a in range(nin)]
    return pl.pallas_call(
        _dft_rows_kernel,
        grid=(pairs, DFT_N1 // DFT_ROW_CHUNK),
        in_specs=in_specs,
        out_specs=pl.BlockSpec((1, 2, n2, lc), lambda p, j: (p, 0, 0, j)),
        out_shape=jax.ShapeDtypeStruct((pairs, 2, n2, DFT_N1 * c), BF16),
        compiler_params=_params(("arbitrary", "arbitrary")),
        name="dft_stage1",
    )(mat, *views)


def _spectrum_kernel(a_ref, g_ref, sc_ref, o_ref, *, kb):
    for j in range(kb):
        rhs = jnp.concatenate([a_ref[0, 0, j], a_ref[0, 1, j]], axis=0)
        x = jnp.dot(g_ref[j], rhs, preferred_element_type=F32)
        o_ref[0, j] = x[:DFT_N1] * sc_ref[...]
        o_ref[1, j] = x[DFT_N1:] * sc_ref[...]


def _filter_spectrum(a, g, scale, n2, kb):
    c = a.shape[-1]
    return pl.pallas_call(
        functools.partial(_spectrum_kernel, kb=kb),
        grid=(n2 // kb,),
        in_specs=[pl.BlockSpec((1, 2, kb, DFT_N1, c), lambda k: (0, 0, k, 0, 0)),
                  pl.BlockSpec((kb, 2 * DFT_N1, 2 * DFT_N1), lambda k: (k, 0, 0)), _full((1, c))],
        out_specs=pl.BlockSpec((2, kb, DFT_N1, c), lambda k: (0, k, 0, 0)),
        out_shape=jax.ShapeDtypeStruct((2, n2, DFT_N1, c), F32),
        compiler_params=_params(("arbitrary",)),
        name="filter_spectrum",
    )(a, g, scale)


def _dft_mid_kernel(a_ref, g_ref, kh_ref, f_ref, o_ref, *, kb):
    for j in range(kb):
        rhs = jnp.concatenate([a_ref[0, 0, j], a_ref[0, 1, j]], axis=0)
        x = jnp.dot(g_ref[j], rhs, preferred_element_type=F32)
        xr, xi = x[:DFT_N1], x[DFT_N1:]
        kr, ki = kh_ref[0, j], kh_ref[1, j]
        yr = xr * kr - xi * ki
        yi = xr * ki + xi * kr
        v = jnp.concatenate([yr, -yi], axis=0).astype(BF16)
        b = jnp.dot(f_ref[...], v, preferred_element_type=F32)
        o_ref[0, 0, j] = b[:DFT_N1].astype(o_ref.dtype)
        o_ref[0, 1, j] = b[DFT_N1:].astype(o_ref.dtype)


def _dft_mid(a, g, khat, f1, pairs, n2, kb):
    c = a.shape[-1]
    blk = pl.BlockSpec((1, 2, kb, DFT_N1, c), lambda k, p: (p, 0, k, 0, 0))
    return pl.pallas_call(
        functools.partial(_dft_mid_kernel, kb=kb),
        grid=(n2 // kb, pairs),
        in_specs=[blk, pl.BlockSpec((kb, 2 * DFT_N1, 2 * DFT_N1), lambda k, p: (k, 0, 0)),
                  pl.BlockSpec((2, kb, DFT_N1, c), lambda k, p: (0, k, 0, 0)), _full(f1.shape)],
        out_specs=blk,
        out_shape=jax.ShapeDtypeStruct(a.shape, BF16),
        compiler_params=_params(("arbitrary", "arbitrary")),
        name="dft_mid",
    )(a, g, khat, f1)


def _dft_last_kernel(b_ref, h_ref, u_ref, x0_ref, bd_ref, o_ref, *, c):
    half = h_ref.shape[1] // 2
    bd = bd_ref[...]
    for t in range(DFT_ROW_CHUNK):
        cols = slice(t * c, (t + 1) * c)
        rhs = jnp.concatenate([b_ref[0, 0, :, cols], b_ref[0, 1, :, cols]], axis=0)
        v = jnp.dot(h_ref[t], rhs, preferred_element_type=F32)
        o_ref[0, :, t, :] = (v[:half] + u_ref[0, :, t, :] * bd) * x0_ref[0, :, t, :]
        o_ref[1, :, t, :] = (-v[half:] + u_ref[1, :, t, :] * bd) * x0_ref[1, :, t, :]


def _dft_last(bm, hh, u_view, x0_view, bias_d, pairs, n2, c):
    half = n2 // 2
    tc = DFT_ROW_CHUNK
    io = pl.BlockSpec((2, half, tc, c), lambda p, j: (p, 0, j, 0))
    return pl.pallas_call(
        functools.partial(_dft_last_kernel, c=c),
        grid=(pairs, DFT_N1 // tc),
        in_specs=[pl.BlockSpec((1, 2, n2, tc * c), lambda p, j: (p, 0, 0, j)),
                  pl.BlockSpec((tc, n2, 2 * n2), lambda p, j: (j, 0, 0)), io, io, _full((1, c))],
        out_specs=io,
        out_shape=jax.ShapeDtypeStruct((2 * pairs, half, DFT_N1, c), F32),
        compiler_params=_params(("arbitrary", "arbitrary")),
        name="dft_last",
    )(bm, hh, u_view, x0_view, bias_d)


def _long_conv_mixer(u, x0, kfilt, ksum, bias_d):
    b, n, c = u.shape
    n2 = 2 * n // DFT_N1
    half = n2 // 2
    pairs = b // 2
    m_data, m_filt, g, f1, hh = (jnp.asarray(t) for t in _dft_tables(n2))
    kb = min(8, n2)
    scale = 1.0 / (ksum * float(DFT_N1 * n2))
    ka = _dft_stage1(m_filt, [kfilt.reshape(1, n2, DFT_N1, c)], 1, n2, n2, c)
    khat = _filter_spectrum(ka.reshape(1, 2, n2, DFT_N1, c), g, scale, n2, kb)
    u_view = u.reshape(b, half, DFT_N1, c)
    a = _dft_stage1(m_data, [u_view, u_view], pairs, half, n2, c)
    bm = _dft_mid(a.reshape(pairs, 2, n2, DFT_N1, c), g, khat, f1, pairs, n2, kb)
    out = _dft_last(bm.reshape(pairs, 2, n2, DFT_N1 * c), hh, u_view, x0.reshape(b, half, DFT_N1, c),
                    bias_d.reshape(1, c), pairs, n2, c)
    return out.reshape(b, n, c)


@functools.lru_cache(maxsize=None)
def _small_dft_tables(n):
    big = 2 * n
    a = np.arange(big)
    fr, fi = _twiddle(np.outer(a, a), big)
    cast = lambda m: np.asarray(m, dtype=BF16)
    return (cast(np.concatenate([fr, fi], 0)),
            cast(_real_form(fr[:, :n], fi[:, :n])),
            cast(_real_form(fr[:n], fi[:n])))


def _small_conv_kernel(k_ref, ks_ref, u_ref, x0_ref, bd_ref, mf_ref, md_ref, mi_ref, o_ref, *, n):
    big = 2 * n
    kh = jnp.dot(mf_ref[...], k_ref[...].astype(BF16), preferred_element_type=F32) * (1.0 / (ks_ref[...] * big))
    kr, ki = kh[:big], kh[big:]
    rhs = jnp.concatenate([u_ref[0], u_ref[1]], axis=0).astype(BF16)
    x = jnp.dot(md_ref[...], rhs, preferred_element_type=F32)
    xr, xi = x[:big], x[big:]
    v = jnp.concatenate([xr * kr - xi * ki, -(xr * ki + xi * kr)], axis=0).astype(BF16)
    y = jnp.dot(mi_ref[...], v, preferred_element_type=F32)
    bd = bd_ref[...]
    o_ref[0] = ((y[:n] + u_ref[0] * bd) * x0_ref[0]).astype(o_ref.dtype)
    o_ref[1] = ((-y[n:] + u_ref[1] * bd) * x0_ref[1]).astype(o_ref.dtype)


def _small_conv_mixer(u, x0, kfilt, ksum, bias_d):
    b, n, c = u.shape
    mf, md, mi = (jnp.asarray(t) for t in _small_dft_tables(n))
    io = pl.BlockSpec((2, n, c), lambda p: (p, 0, 0))
    return pl.pallas_call(
        functools.partial(_small_conv_kernel, n=n),
        grid=(b // 2,),
        in_specs=[_full((2 * n, c)), _full((1, c)), io, io, _full((1, c)),
                  _full(mf.shape), _full(md.shape), _full(mi.shape)],
        out_specs=io,
        out_shape=jax.ShapeDtypeStruct((b, n, c), BF16),
        compiler_params=_params(("arbitrary",)),
        name="small_conv",
    )(kfilt, ksum, u, x0, bias_d.reshape(1, c), mf, md, mi)


MXU_DIM = 256
FFN_CHUNK_EDGES = (0, 6 * MXU_DIM, D_FF)


def _gelu_tanh(x):
    return 0.5 * x * (1.0 + jnp.tanh(math.sqrt(2.0 / math.pi) * (x + 0.044715 * (x * x * x))))


def _ffn_kernel(x_ref, xp_ref, xn_ref, oa_ref, oap_ref, oan_ref, oc_ref, ocp_ref, ocn_ref, wa_ref, wc_ref, g1_ref,
                sh_ref, sc_ref, gt_ref, g_ref, wu_ref, cw_ref, cb_ref, wd_ref, o_ref):
    i = pl.program_id(1)
    nt = pl.num_programs(1)
    tm = x_ref.shape[1]

    def halo(p_ref, n_ref):
        return jnp.concatenate([p_ref[0][-SUBLANES:], n_ref[0][:SUBLANES]], axis=0)

    def mixer_out(oa, oc):
        return (jnp.dot(oa.astype(BF16), wa_ref[...], preferred_element_type=F32)
                + jnp.dot(oc.astype(BF16), wc_ref[...], preferred_element_type=F32))

    g1 = g1_ref[0]
    xm = x_ref[0] + g1 * mixer_out(oa_ref[0], oc_ref[0])
    xh = halo(xp_ref, xn_ref) + g1 * mixer_out(halo(oap_ref, oan_ref), halo(ocp_ref, ocn_ref))
    ext = jnp.concatenate([xh[:SUBLANES], xm, xh[SUBLANES:]], axis=0)
    r = tm + 2 * SUBLANES
    h = _modulated_norm(ext, g_ref[...], sh_ref[0], sc_ref[0])
    row = lax.broadcasted_iota(jnp.int32, (r, 1), 0)
    inside = jnp.logical_and(jnp.logical_or(i > 0, row >= SUBLANES), jnp.logical_or(i < nt - 1, row < tm + SUBLANES))
    h = jnp.where(inside, h, 0.0).astype(BF16)
    hm = h[SUBLANES:tm + SUBLANES]
    acc = jnp.zeros((tm, D_MODEL), F32)
    for c0, c1 in zip(FFN_CHUNK_EDGES[:-1], FFN_CHUNK_EDGES[1:]):
        a = jnp.dot(h, wu_ref[:, c0:c1], preferred_element_type=F32)
        v = jnp.dot(hm, wu_ref[:, D_FF + c0:D_FF + c1], preferred_element_type=F32)
        ap = pltpu.roll(a, 1, 0)[SUBLANES:tm + SUBLANES]
        an = pltpu.roll(a, r - 1, 0)[SUBLANES:tm + SUBLANES]
        cw = cw_ref[:, c0:c1]
        conv = ap * cw[0:1] + a[SUBLANES:tm + SUBLANES] * cw[1:2] + an * cw[2:3] + cb_ref[:, c0:c1]
        act = (_gelu_tanh(conv) * v).astype(BF16)
        acc = acc + jnp.dot(act, wd_ref[c0:c1, :], preferred_element_type=F32)
    o_ref[0] = xm + gt_ref[0] * acc


def _mixer_out_ffn(x, oa, oc, wo_bf, gate1, sh, sc, gate2, gain, wu_bf, conv_w, conv_b, wd_bf):
    bx, t, d = x.shape
    tm = min(512, t)
    vec = pl.BlockSpec((1, 1, d), lambda b, i: (b, 0, 0))
    once = pl.Buffered(1)
    return pl.pallas_call(
        _ffn_kernel,
        grid=(bx, t // tm),
        in_specs=_halo_specs(tm, t, d) + _halo_specs(tm, t, Q_W, oa.dtype) + _halo_specs(tm, t, CONV_W, oc.dtype) + [
            pl.BlockSpec((Q_W, d), lambda b, i: (0, 0)), pl.BlockSpec((CONV_W, d), lambda b, i: (1, 0)),
            vec, vec, vec, vec, _full((1, d)),
            pl.BlockSpec((d, 2 * D_FF), lambda b, i: (0, 0), pipeline_mode=once),
            _full((3, D_FF)), _full((1, D_FF)),
            pl.BlockSpec((D_FF, d), lambda b, i: (0, 0), pipeline_mode=once)],
        out_specs=pl.BlockSpec((1, tm, d), lambda b, i: (b, i, 0)),
        out_shape=jax.ShapeDtypeStruct((bx, t, d), F32),
        compiler_params=_params(("arbitrary", "arbitrary")),
        name="mixer_out_ffn",
    )(x, x, x, oa, oa, oa, oc, oc, oc, wo_bf, wo_bf, gate1, sh, sc, gate2, gain, wu_bf, conv_w,
      conv_b.reshape(1, D_FF), wd_bf)


@functools.lru_cache(maxsize=None)
def _rope_tables(t):
    pos = np.arange(t)
    n_freq = HEAD_DIM // 4
    inv = ROPE_THETA ** (-np.arange(n_freq, dtype=np.float64) / n_freq)
    ang_r = (pos // GRID_W)[:, None] * inv
    ang_c = (pos % GRID_W)[:, None] * inv
    zero = np.zeros_like(ang_r)
    cos = np.concatenate([np.cos(ang_r)] * 2 + [np.cos(ang_c)] * 2, 1)
    sa = np.concatenate([-np.sin(ang_r), zero, -np.sin(ang_c), zero], 1)
    sb = np.concatenate([zero, np.sin(ang_r), zero, np.sin(ang_c)], 1)
    tile = lambda m: np.tile(m, (1, LANES // HEAD_DIM)).astype(np.float32)
    return tile(cos), tile(sa), tile(sb)


@functools.lru_cache(maxsize=None)
def _identity_rope_tables(t):
    return np.ones((t, LANES), np.float32), np.zeros((t, LANES), np.float32), np.zeros((t, LANES), np.float32)


@functools.lru_cache(maxsize=None)
def _head_block_diag():
    hid = np.arange(Q_W) // HEAD_DIM
    return np.asarray(hid[:, None] == hid[None, :], dtype=BF16)


def kernel(x, c, ctx, c_ctx, ada_w, ada_b, norm_mix, norm_ffn, mix_w_in, mix_w_out, attn_q_norm, attn_k_norm,
           swa_sink, hy_conv_w, hy_conv_b, hy_w1, hy_b1, hy_w2, hy_b2, hy_w3, hy_b3, hy_w4, hy_freq, hy_bias_d,
           sc_conv_w, ffn_w_up, ffn_conv_w, ffn_conv_b, ffn_w_down):
    b, s, d = x.shape
    s_ctx = ctx.shape[1]
    depth = ada_w.shape[0]
    assert d == D_MODEL and b % 2 == 0 and b + 1 <= SUBLANES and s % 1024 == 0 and s_ctx % SUBLANES == 0

    cvec = jnp.concatenate([c, c_ctx[None, :], jnp.zeros((SUBLANES - b - 1, d), F32)], axis=0)
    mods = _mods(cvec, ada_w, ada_b)
    rope = [jnp.asarray(t) for t in _rope_tables(s)]
    rope_ctx = [jnp.asarray(t) for t in _identity_rope_tables(s_ctx)]
    bd = jnp.asarray(_head_block_diag())
    xc = ctx

    for i in range(depth):
        last = i == depth - 1
        j = i // 2
        lat = [mods[i, :b, k * d:(k + 1) * d][:, None, :] for k in range(6)]
        cx = [jnp.broadcast_to(mods[i, b, k * d:(k + 1) * d][None, None, :], (b, 1, d)) for k in range(6)]
        w_in = _layer_weight_bf16(mix_w_in, i)
        w_out = _layer_weight_bf16(mix_w_out, i)
        w_up = _layer_weight_bf16(ffn_w_up, i)
        w_down = _layer_weight_bf16(ffn_w_down, i)
        g_mix = norm_mix[i].reshape(1, d)
        g_ffn = norm_ffn[i].reshape(1, d)
        qg = jnp.tile(attn_q_norm[i], N_Q_HEADS).reshape(1, Q_W)
        kg = jnp.tile(attn_k_norm[i], N_KV_HEADS).reshape(1, KV_W)

        conv = (hy_conv_w[j], hy_conv_b[j]) if i % 2 == 0 else (sc_conv_w[j],)
        q, k, vt, *mix = _inproj(x, lat[0], lat[1], g_mix, w_in, qg, kg, bd, *rope, *conv)
        qc, kc, vct, *mixc = _inproj(xc, cx[0], cx[1], g_mix, w_in, qg, kg, bd, *rope_ctx, *conv)

        bound = _score_bound(attn_q_norm[i], attn_k_norm[i])
        if i % 2 == 0:
            fargs = (hy_w1[j], hy_b1[j], hy_w2[j], hy_b2[j], hy_w3[j], hy_b3[j], hy_w4[j], hy_freq[j])
            o_attn = _dense_attention(q, k, vt, kc, vct, bound=bound)
            kf, ks = _implicit_filter(s, *fargs)
            o_conv = _long_conv_mixer(*mix, kf, ks, hy_bias_d[j])
            if not last:
                oc_attn = _dense_attention(qc, kc, vct, bound=bound)
                kfc, ksc = _implicit_filter(s_ctx, *fargs)
                oc_conv = _small_conv_mixer(*mixc, kfc, ksc, hy_bias_d[j])
        else:
            sink = swa_sink[j].reshape(N_KV_HEADS, GROUP)
            o_attn = _banded_attention(q, k, vt, kc, vct, sink, bound)
            o_conv = mix[0]
            if not last:
                oc_attn = _dense_attention(qc, kc, vct, sink=sink)
                oc_conv = mixc[0]

        ffn = (g_ffn, w_up, ffn_conv_w[i], ffn_conv_b[i], w_down)
        x = _mixer_out_ffn(x, o_attn, o_conv, w_out, lat[2], lat[3], lat[4], lat[5], *ffn)
        if not last:
            xc = _mixer_out_ffn(xc, oc_attn, oc_conv, w_out, cx[2], cx[3], cx[4], cx[5], *ffn)
    return x
```

```python
import functools
import math

import numpy as np
import jax
import jax.numpy as jnp
from jax import lax
from jax.experimental import pallas as pl
from jax.experimental.pallas import tpu as pltpu

F32 = jnp.float32
BF16 = jnp.bfloat16
HI = lax.Precision.HIGHEST

D_MODEL = 1024
GRID_W = 64
HEAD_DIM = 64
N_Q_HEADS = 8
N_KV_HEADS = 4
GROUP = N_Q_HEADS // N_KV_HEADS
Q_W = N_Q_HEADS * HEAD_DIM
KV_W = N_KV_HEADS * HEAD_DIM
QKV_W = Q_W + 2 * KV_W
CONV_W = D_MODEL // 2
MIX_IN_W = QKV_W + 3 * CONV_W
WINDOW = 128
ROPE_THETA = 10000.0
FILT_EMB = 33
FILT_WIDTH = 64
DECAY_TARGET = 1e-2
FAST_DECAY_PCT = 0.3
SLOW_DECAY_PCT = 1.5
DECAY_SHIFT = 0.05
D_FF = 2816
NEG_INF = -1e30
RMS_EPS = 1e-6
LOG2E = 1.4426950408889634
Q_SCALE = HEAD_DIM ** -0.5 * LOG2E

LANES = 128
SUBLANES = 8
DFT_N1 = 128
VMEM_LIMIT_MB = 56
PROJ_SUB_ROWS = 128


def _params(sem, vmem_mb=VMEM_LIMIT_MB):
    return pltpu.CompilerParams(dimension_semantics=sem, vmem_limit_bytes=vmem_mb * 1024 * 1024)


def _full(shape):
    nd = len(shape)
    return pl.BlockSpec(shape, lambda *_: (0,) * nd)


def _mods_kernel(c_ref, w_ref, b_ref, o_ref):
    c = c_ref[...]
    s = c / (1.0 + jnp.exp(-c))
    o_ref[0] = jnp.dot(s, w_ref[0], preferred_element_type=F32, precision=HI) + b_ref[0]


def _mods(cvec, ada_w, ada_b):
    depth, d, n6 = ada_w.shape
    tn = 512
    return pl.pallas_call(
        _mods_kernel,
        grid=(depth, n6 // tn),
        in_specs=[_full((SUBLANES, d)),
                  pl.BlockSpec((1, d, tn), lambda l, j: (l, 0, j)),
                  pl.BlockSpec((1, 1, tn), lambda l, j: (l, 0, j))],
        out_specs=pl.BlockSpec((1, SUBLANES, tn), lambda l, j: (l, 0, j)),
        out_shape=jax.ShapeDtypeStruct((depth, SUBLANES, n6), F32),
        compiler_params=_params(("arbitrary", "arbitrary")),
        name="ada_mods",
    )(cvec, ada_w, ada_b.reshape(depth, 1, n6))


def _modulated_norm(x, gain, shift, scale):
    ms = jnp.mean(x * x, axis=-1, keepdims=True)
    return (x * lax.rsqrt(ms + RMS_EPS)) * gain * (1.0 + scale) + shift


def _cast_kernel(w_ref, o_ref):
    o_ref[...] = w_ref[0].astype(o_ref.dtype)


def _layer_weight_bf16(w, layer):
    _, rows, cols = w.shape
    tr = 128
    return pl.pallas_call(
        _cast_kernel,
        grid=(rows // tr,),
        in_specs=[pl.BlockSpec((1, tr, cols), lambda r: (layer, r, 0))],
        out_specs=pl.BlockSpec((tr, cols), lambda r: (r, 0)),
        out_shape=jax.ShapeDtypeStruct((rows, cols), BF16),
        compiler_params=_params(("arbitrary",)),
        name="weight_bf16",
    )(w)


def _halo_specs(tm, t, width, dtype=F32):
    rows = SUBLANES * 4 // jnp.dtype(dtype).itemsize
    nb = tm // rows
    last = t // rows - 1
    return [pl.BlockSpec((1, tm, width), lambda b, i: (b, i, 0)),
            pl.BlockSpec((1, rows, width), lambda b, i: (b, jnp.maximum(i * nb - 1, 0), 0)),
            pl.BlockSpec((1, rows, width), lambda b, i: (b, jnp.minimum((i + 1) * nb, last), 0))]


def _inproj_kernel(*refs, mode):
    (x_ref, xp_ref, xn_ref, sh_ref, sc_ref, g_ref, w_ref, qg_ref, kg_ref, bd_ref,
     cos_ref, sa_ref, sb_ref, cw_ref) = refs[:14]
    if mode == "hyena":
        cb_ref, q_ref, k_ref, vt_ref, u_ref, x0_ref = refs[14:]
    else:
        q_ref, k_ref, vt_ref, oc_ref = refs[14:]
    i = pl.program_id(1)
    nt = pl.num_programs(1)
    tm = x_ref.shape[1]
    sub = min(PROJ_SUB_ROWS, tm)
    gain, shift, scale = g_ref[...], sh_ref[0], sc_ref[0]

    xh = jnp.concatenate([xp_ref[0], xn_ref[0]], axis=0)
    zh = jnp.dot(_modulated_norm(xh, gain, shift, scale).astype(BF16), w_ref[:, QKV_W:], preferred_element_type=F32)
    before = jnp.where(i > 0, zh[:SUBLANES], 0.0)
    after = jnp.where(i < nt - 1, zh[SUBLANES:], 0.0)
    cw = cw_ref[...]

    def conv_stage(ext, rows):
        n = sub + 2 * SUBLANES
        mid = slice(SUBLANES, sub + SUBLANES)
        if mode == "hyena":
            c = (pltpu.roll(ext, 1, 0)[mid] * cw[0:1] + ext[mid] * cw[1:2] + pltpu.roll(ext, n - 1, 0)[mid] * cw[2:3]
                 + cb_ref[...])
            x0_ref[0, rows] = c[:, :CONV_W]
            u_ref[0, rows] = c[:, 2 * CONV_W:] * c[:, CONV_W:2 * CONV_W]
        else:
            pr = ext[:, CONV_W:2 * CONV_W] * ext[:, 2 * CONV_W:]
            conv = pltpu.roll(pr, 1, 0)[mid] * cw[0:1] + pr[mid] * cw[1:2] + pltpu.roll(pr, n - 1, 0)[mid] * cw[2:3]
            oc_ref[0, rows] = (ext[mid, :CONV_W] * conv).astype(oc_ref.dtype)

    def head_norm_rope(t, hgain, bd, cos, sa, sb):
        ssq = jnp.dot((t * t).astype(BF16), bd, preferred_element_type=F32)
        t = t * lax.rsqrt(ssq * (1.0 / HEAD_DIM) + RMS_EPS) * hgain
        outs = []
        for j in range(t.shape[1] // LANES):
            tj = t[:, j * LANES:(j + 1) * LANES]
            outs.append(tj * cos + pltpu.roll(tj, LANES - 16, 1) * sa + pltpu.roll(tj, 16, 1) * sb)
        return outs

    held = None
    for r0 in range(0, tm, sub):
        rows = slice(r0, r0 + sub)
        h = _modulated_norm(x_ref[0, rows], gain, shift, scale)
        hb = h.astype(BF16)
        z = jnp.dot(hb, w_ref[:, :QKV_W], preferred_element_type=F32)
        zc = jnp.dot(hb, w_ref[:, QKV_W:], preferred_element_type=F32)
        if held is not None:
            conv_stage(jnp.concatenate([before, held[0], zc[:SUBLANES]], axis=0), held[1])
            before = held[0][-SUBLANES:]
        held = (zc, rows)
        tabs = (cos_ref[rows], sa_ref[rows], sb_ref[rows])
        qs = head_norm_rope(z[:, :Q_W], qg_ref[...], bd_ref[...], *tabs)
        ks = head_norm_rope(z[:, Q_W:Q_W + KV_W], kg_ref[...], bd_ref[:KV_W, :KV_W], *tabs)
        for j, qj in enumerate(qs):
            qj = (qj * Q_SCALE).astype(BF16)
            q_ref[0, 2 * j, rows] = qj[:, :HEAD_DIM]
            q_ref[0, 2 * j + 1, rows] = qj[:, HEAD_DIM:]
        for j, kj in enumerate(ks):
            kj = kj.astype(BF16)
            k_ref[0, 2 * j, rows] = kj[:, :HEAD_DIM]
            k_ref[0, 2 * j + 1, rows] = kj[:, HEAD_DIM:]
        vt = z[:, Q_W + KV_W:QKV_W].T
        ones_row = (lax.broadcasted_iota(jnp.int32, (LANES - HEAD_DIM, sub), 0) == 0).astype(F32)
        for hh in range(N_KV_HEADS):
            vt_ref[0, hh, :, rows] = jnp.concatenate([vt[hh * HEAD_DIM:(hh + 1) * HEAD_DIM], ones_row],
                                                     axis=0).astype(BF16)

    conv_stage(jnp.concatenate([before, held[0], after], axis=0), held[1])


def _inproj(x, sh, sc, gain, w_bf, qg, kg, bd, cos, sa, sb, conv_w, conv_b=None):
    bx, t, d = x.shape
    tm = min(512, t)
    mode = "hyena" if conv_b is not None else "short"
    vec = pl.BlockSpec((1, 1, d), lambda b, i: (b, 0, 0))
    tab = pl.BlockSpec((tm, LANES), lambda b, i: (i, 0))
    row = lambda w: pl.BlockSpec((1, tm, w), lambda b, i: (b, i, 0))
    in_specs = _halo_specs(tm, t, d) + [vec, vec, _full((1, d)), _full((d, MIX_IN_W)), _full((1, Q_W)),
                                        _full((1, KV_W)), _full((Q_W, Q_W)), tab, tab, tab, _full(conv_w.shape)]
    args = [x, x, x, sh, sc, gain, w_bf, qg, kg, bd, cos, sa, sb, conv_w]
    out_specs = [pl.BlockSpec((1, N_Q_HEADS, tm, HEAD_DIM), lambda b, i: (b, 0, i, 0)),
                 pl.BlockSpec((1, N_KV_HEADS, tm, HEAD_DIM), lambda b, i: (b, 0, i, 0)),
                 pl.BlockSpec((1, N_KV_HEADS, LANES, tm), lambda b, i: (b, 0, 0, i))]
    out_shape = [jax.ShapeDtypeStruct((bx, N_Q_HEADS, t, HEAD_DIM), BF16),
                 jax.ShapeDtypeStruct((bx, N_KV_HEADS, t, HEAD_DIM), BF16),
                 jax.ShapeDtypeStruct((bx, N_KV_HEADS, LANES, t), BF16)]
    if mode == "hyena":
        in_specs.append(_full((1, 3 * CONV_W)))
        args.append(conv_b.reshape(1, 3 * CONV_W))
        out_specs += [row(CONV_W), row(CONV_W)]
        out_shape += [jax.ShapeDtypeStruct((bx, t, CONV_W), F32)] * 2
    else:
        out_specs.append(row(CONV_W))
        out_shape.append(jax.ShapeDtypeStruct((bx, t, CONV_W), BF16))
    return pl.pallas_call(
        functools.partial(_inproj_kernel, mode=mode),
        grid=(bx, t // tm),
        in_specs=in_specs,
        out_specs=out_specs,
        out_shape=out_shape,
        compiler_params=_params(("arbitrary", "arbitrary")),
        name="in_proj",
    )(*args)


def _score_bound(q_gain, k_gain):
    b = HEAD_DIM * Q_SCALE * BOUND_MARGIN * jnp.max(jnp.abs(q_gain)) * jnp.max(jnp.abs(k_gain))
    return b.reshape(1, 1).astype(F32)


def _scores_t(ks, q):
    return lax.dot_general(ks, q, (((1,), (1,)), ((), ())), preferred_element_type=F32)


def _sink_column(sink_ref, hh, tq, m, acc):
    col = lax.broadcasted_iota(jnp.int32, m.shape, 1)
    sink = jnp.where(col < tq, sink_ref[hh, 0], sink_ref[hh, 1]) * LOG2E
    m_new = jnp.maximum(m, sink)
    den_row = lax.broadcasted_iota(jnp.int32, acc.shape, 0) == HEAD_DIM
    return acc * jnp.exp2(m - m_new) + jnp.where(den_row, jnp.exp2(sink - m_new), 0.0)


def _attn_finish(acc, tq):
    o = (acc / acc[HEAD_DIM:HEAD_DIM + 1]).T
    return jnp.concatenate([o[:tq, :HEAD_DIM], o[tq:, :HEAD_DIM]], axis=1)


EXP_CHUNK_ELEMS = 32 * SUBLANES * LANES


def _score_stage(ks, q, s_ref, bias=None):
    s = _scores_t(ks, q)
    if bias is not None:
        s = s + bias
    s_ref[0:s.shape[0]] = s
    return jnp.max(s, axis=0, keepdims=True)


def _softmax_stage(s_ref, p_ref, n, vt, mx, m, acc_ref):
    m_new = jnp.maximum(m, mx)
    rows = EXP_CHUNK_ELEMS // m.shape[1]
    for c in range(0, n, rows):
        p_ref[c:c + rows] = jnp.exp2(s_ref[c:c + rows] - m_new).astype(BF16)
    acc_ref[...] = acc_ref[...] * jnp.exp2(m - m_new) + jnp.dot(vt, p_ref[0:n], preferred_element_type=F32)
    return m_new


BOUNDED_TILES = 4
BOUND_MARGIN = 1.01
SAFE_SHIFT = 40.0


def _flash_exact(q, k_tile, vt_tile, ctx, sa_ref, sb_ref, p_ref, acc_ref, *, tk, n_main):
    pa_ref, pb_ref = p_ref.at[0:tk], p_ref.at[tk:2 * tk]
    acc_ref[...] = jnp.zeros_like(acc_ref)
    m = jnp.full((1, q.shape[0]), NEG_INF, F32)
    mx_a = _score_stage(k_tile(0), q, sa_ref)
    if n_main > 1:
        def body(i, carry):
            m, mx_a = carry
            mx_b = _score_stage(k_tile(2 * i + 1), q, sb_ref)
            m = _softmax_stage(sa_ref, pa_ref, tk, vt_tile(2 * i), mx_a, m, acc_ref)
            mx_a = _score_stage(k_tile(2 * i + 2), q, sa_ref)
            m = _softmax_stage(sb_ref, pb_ref, tk, vt_tile(2 * i + 1), mx_b, m, acc_ref)
            return m, mx_a

        m, mx_a = lax.fori_loop(0, n_main // 2 - 1, body, (m, mx_a))
        mx_b = _score_stage(k_tile(n_main - 1), q, sb_ref)
        m = _softmax_stage(sa_ref, pa_ref, tk, vt_tile(n_main - 2), mx_a, m, acc_ref)
        last = (sb_ref, pb_ref, mx_b)
        spare = (sa_ref, pa_ref)
    else:
        last = (sa_ref, pa_ref, mx_a)
        spare = (sb_ref, pb_ref)
    if ctx is not None:
        mx_c = _score_stage(ctx[0], q, spare[0])
    m = _softmax_stage(last[0], last[1], tk, vt_tile(n_main - 1), last[2], m, acc_ref)
    if ctx is not None:
        m = _softmax_stage(spare[0], spare[1], ctx[0].shape[0], ctx[1], mx_c, m, acc_ref)
    return m


def _flash_bounded(q, k_ref, vt_ref, ctx, bound, p_ref, acc_ref, *, tk, n_main):
    step = BOUNDED_TILES * tk if n_main % BOUNDED_TILES == 0 else tk

    def group(j):
        off = pl.multiple_of(j * step, step)
        for c in range(0, step, tk):
            s_t = _scores_t(k_ref[0, 0, pl.ds(off + c, tk), :], q)
            p_ref[c:c + tk] = jnp.exp2(s_t - bound).astype(BF16)
        return jnp.dot(vt_ref[0, 0, :, pl.ds(off, step)], p_ref[0:step], preferred_element_type=F32)

    first = group(0)
    if ctx is not None:
        pc_ref = p_ref.at[BOUNDED_TILES * tk:BOUNDED_TILES * tk + ctx[0].shape[0]]
        pc_ref[...] = jnp.exp2(_scores_t(ctx[0], q) - bound).astype(BF16)
        first = first + jnp.dot(ctx[1], pc_ref[...], preferred_element_type=F32)
    acc_ref[...] = first

    def body(j, carry):
        acc_ref[...] += group(j)
        return carry

    lax.fori_loop(1, n_main * tk // step, body, 0)


def _flash_kernel(*refs, tq, tk, n_main, has_ctx, has_sink):
    sa_ref, sb_ref, p_ref, acc_ref = refs[-4:]
    o_ref = refs[-5]
    refs = list(refs[:-5])
    sink_ref = refs.pop(0) if has_sink else None
    bound_ref = None if has_sink else refs.pop(0)
    q_ref, k_ref, vt_ref = refs[:3]
    m_cols = GROUP * tq
    q = q_ref[0].reshape(m_cols, HEAD_DIM)
    ctx = (refs[3][0, 0], refs[4][0, 0]) if has_ctx else None

    def k_tile(j):
        return k_ref[0, 0, pl.ds(pl.multiple_of(j * tk, tk), tk), :]

    def vt_tile(j):
        return vt_ref[0, 0, :, pl.ds(pl.multiple_of(j * tk, tk), tk)]

    exact = functools.partial(_flash_exact, q, k_tile, vt_tile, ctx, sa_ref, sb_ref, p_ref, acc_ref,
                              tk=tk, n_main=n_main)
    if has_sink:
        acc = _sink_column(sink_ref, pl.program_id(1), tq, exact(), acc_ref[...])
    else:
        bound = bound_ref[0, 0]
        safe = bound <= SAFE_SHIFT

        @pl.when(safe)
        def _():
            _flash_bounded(q, k_ref, vt_ref, ctx, bound, p_ref, acc_ref, tk=tk, n_main=n_main)

        @pl.when(jnp.logical_not(safe))
        def _():
            exact()

        acc = acc_ref[...]
    o_ref[0] = _attn_finish(acc, tq).astype(o_ref.dtype)


def _kv_specs(n):
    return [pl.BlockSpec((1, 1, n, HEAD_DIM), lambda b, h, i: (b, h, 0, 0)),
            pl.BlockSpec((1, 1, LANES, n), lambda b, h, i: (b, h, 0, 0))]


def _dense_attention(q, k, vt, kc=None, vct=None, *, bound=None, sink=None):
    bx, _, t, _ = q.shape
    s = k.shape[2]
    tq = min(512, t)
    tk = min(512, s)
    has_ctx = kc is not None
    has_sink = sink is not None
    in_specs = [pl.BlockSpec((1, GROUP, tq, HEAD_DIM), lambda b, h, i: (b, h, i, 0))] + _kv_specs(s)
    args = [q, k, vt]
    if has_ctx:
        in_specs += _kv_specs(kc.shape[2])
        args += [kc, vct]
    in_specs = [pl.BlockSpec(memory_space=pltpu.SMEM)] + in_specs
    args = [sink if has_sink else bound] + args
    n_main = s // tk
    assert n_main == 1 or n_main % 2 == 0
    n_ctx = kc.shape[2] if has_ctx else 0
    assert n_ctx <= tk
    m_cols = GROUP * tq
    return pl.pallas_call(
        functools.partial(_flash_kernel, tq=tq, tk=tk, n_main=n_main, has_ctx=has_ctx, has_sink=has_sink),
        grid=(bx, N_KV_HEADS, t // tq),
        in_specs=in_specs,
        out_specs=pl.BlockSpec((1, tq, LANES), lambda b, h, i: (b, i, h)),
        out_shape=jax.ShapeDtypeStruct((bx, t, Q_W), BF16),
        scratch_shapes=[pltpu.VMEM((tk, m_cols), F32), pltpu.VMEM((tk, m_cols), F32),
                        pltpu.VMEM((BOUNDED_TILES * tk + n_ctx, m_cols), BF16), pltpu.VMEM((LANES, m_cols), F32)],
        compiler_params=_params(("arbitrary", "arbitrary", "arbitrary")),
        name="dense_attention",
    )(*args)


BAND_SUB = 256


@functools.lru_cache(maxsize=None)
def _band_bias():
    span = BAND_SUB + 2 * WINDOW
    kr = np.arange(span)[None, :, None]
    qc = (np.arange(GROUP * BAND_SUB) % BAND_SUB)[None, None, :]
    rel = np.arange(3)[:, None, None]
    return np.where(np.abs(kr - qc - rel * WINDOW) <= WINDOW, 0.0, NEG_INF).astype(np.float32)


def _banded_kernel(sink_ref, bound_ref, q_ref, k_ref, vt_ref, kc_ref, vct_ref, bias_ref, o_ref,
                   sl_ref, sc_ref, pl_ref, pc_ref, acc_ref, *, n_sub, s_len):
    i = pl.program_id(2)
    hh = pl.program_id(1)
    sub = BAND_SUB
    m_cols = GROUP * sub
    span = sub + 2 * WINDOW
    n_ctx = kc_ref.shape[2]

    def window(u):
        q = q_ref[0, :, u * sub:(u + 1) * sub, :].reshape(m_cols, HEAD_DIM)
        q0 = (i * n_sub + u) * sub
        start = pl.multiple_of(jnp.clip(q0 - WINDOW, 0, s_len - span), WINDOW)
        return q, start, bias_ref[(q0 - start) // WINDOW]

    shift = jnp.maximum(bound_ref[0, 0], jnp.maximum(sink_ref[hh, 0], sink_ref[hh, 1]) * LOG2E)
    safe = shift <= SAFE_SHIFT

    @pl.when(safe)
    def _():
        col = lax.broadcasted_iota(jnp.int32, (1, m_cols), 1)
        sink = jnp.where(col < sub, sink_ref[hh, 0], sink_ref[hh, 1]) * LOG2E
        den_row = lax.broadcasted_iota(jnp.int32, (LANES, m_cols), 0) == HEAD_DIM
        sink_den = jnp.where(den_row, jnp.exp2(sink - shift), 0.0)
        for u in range(n_sub):
            q, start, bias = window(u)
            pl_u, pc_u = pl_ref.at[u], pc_ref.at[u]
            pl_u[...] = jnp.exp2(_scores_t(k_ref[0, 0, pl.ds(start, span), :], q) + bias - shift).astype(BF16)
            pc_u[...] = jnp.exp2(_scores_t(kc_ref[0, 0], q) - shift).astype(BF16)
            acc = (jnp.dot(vt_ref[0, 0, :, pl.ds(start, span)], pl_u[...], preferred_element_type=F32)
                   + jnp.dot(vct_ref[0, 0], pc_u[...], preferred_element_type=F32) + sink_den)
            o_ref[0, u * sub:(u + 1) * sub] = _attn_finish(acc, sub).astype(o_ref.dtype)

    @pl.when(jnp.logical_not(safe))
    def _():
        stats = []
        for u in range(n_sub):
            q, start, bias = window(u)
            mx_l = _score_stage(k_ref[0, 0, pl.ds(start, span), :], q, sl_ref.at[u], bias=bias)
            mx_c = _score_stage(kc_ref[0, 0], q, sc_ref.at[u])
            stats.append((start, mx_l, mx_c))
        for u in range(n_sub):
            start, mx_l, mx_c = stats[u]
            acc_u = acc_ref.at[u]
            acc_u[...] = jnp.zeros((LANES, m_cols), F32)
            m = jnp.full((1, m_cols), NEG_INF, F32)
            m = _softmax_stage(sl_ref.at[u], pl_ref.at[u], span, vt_ref[0, 0, :, pl.ds(start, span)], mx_l, m, acc_u)
            m = _softmax_stage(sc_ref.at[u], pc_ref.at[u], n_ctx, vct_ref[0, 0], mx_c, m, acc_u)
            acc = _sink_column(sink_ref, hh, sub, m, acc_u[...])
            o_ref[0, u * sub:(u + 1) * sub] = _attn_finish(acc, sub).astype(o_ref.dtype)


def _banded_attention(q, k, vt, kc, vct, sink, bound):
    bx, _, t, _ = q.shape
    n_ctx = kc.shape[2]
    n_sub = 4 if t % (4 * BAND_SUB) == 0 else 2
    tq = n_sub * BAND_SUB
    m_cols = GROUP * BAND_SUB
    span = BAND_SUB + 2 * WINDOW
    bias = jnp.asarray(_band_bias())
    return pl.pallas_call(
        functools.partial(_banded_kernel, n_sub=n_sub, s_len=t),
        grid=(bx, N_KV_HEADS, t // tq),
        in_specs=[pl.BlockSpec(memory_space=pltpu.SMEM), pl.BlockSpec(memory_space=pltpu.SMEM),
                  pl.BlockSpec((1, GROUP, tq, HEAD_DIM), lambda b, h, i: (b, h, i, 0))]
                 + _kv_specs(t) + _kv_specs(n_ctx) + [_full(bias.shape)],
        out_specs=pl.BlockSpec((1, tq, LANES), lambda b, h, i: (b, i, h)),
        out_shape=jax.ShapeDtypeStruct((bx, t, Q_W), BF16),
        scratch_shapes=[pltpu.VMEM((n_sub, span, m_cols), F32), pltpu.VMEM((n_sub, n_ctx, m_cols), F32),
                        pltpu.VMEM((n_sub, span, m_cols), BF16), pltpu.VMEM((n_sub, n_ctx, m_cols), BF16),
                        pltpu.VMEM((n_sub, LANES, m_cols), F32)],
        compiler_params=_params(("arbitrary", "arbitrary", "arbitrary")),
        name="banded_attention",
    )(sink, bound, q, k, vt, kc, vct, bias)


@functools.lru_cache(maxsize=None)
def _filter_features(n):
    j = np.arange(2 * n)
    d = np.where(j <= n, j, 2 * n - j)
    d = np.where(j == n, 0, d)
    bands = (FILT_EMB - 1) // 2
    t01 = np.linspace(0.0, 1.0, n)[d]
    w = 2.0 * np.pi * d.astype(np.float64) / n
    f = np.linspace(1e-4, bands - 1, bands)[None, :]
    feats = np.zeros((2 * n, LANES), np.float64)
    feats[:, 0] = t01
    feats[:, 1:1 + bands] = np.cos(f * w[:, None])
    feats[:, 1 + bands:FILT_EMB] = -np.sin(f * w[:, None])
    feats[:, 64] = t01
    feats[:, 65] = (j < n)
    feats[:, 66] = (j != n)
    return feats.astype(np.float32)


def _filter_kernel(f_ref, w1_ref, b1_ref, w2_ref, b2_ref, w3_ref, b3_ref, w4_ref, fr_ref, dl_ref, k_ref, s_ref):
    f = f_ref[...]
    fr = fr_ref[...]
    mm = lambda a, b: jnp.dot(a, b, preferred_element_type=F32, precision=HI)
    h = jnp.sin(fr * (mm(w1_ref[...], f) + b1_ref[...]))
    h = jnp.sin(fr * (mm(w2_ref[...], h) + b2_ref[...]))
    h = jnp.sin(fr * (mm(w3_ref[...], h) + b3_ref[...]))
    hf = mm(w4_ref[...], h)
    win = jnp.exp(-dl_ref[...] * f[64:65]) + DECAY_SHIFT
    k = (jnp.where(f[65:66] > 0.5, hf[:CONV_W], hf[CONV_W:]) * win * f[66:67]).T
    k_ref[...] = k

    @pl.when(pl.program_id(0) == 0)
    def _():
        s_ref[...] = jnp.zeros_like(s_ref)

    s_ref[...] += jnp.sum(jnp.abs(k), axis=0, keepdims=True)


def _implicit_filter(n, w1, b1, w2, b2, w3, b3, w4, freq):
    feats_t = jnp.asarray(np.ascontiguousarray(_filter_features(n).T))
    col = lambda a: a.reshape(-1, 1)
    w1t = jnp.pad(w1.T, ((0, 0), (0, LANES - FILT_EMB)))
    deltas = np.abs(np.linspace(math.log(DECAY_TARGET) / SLOW_DECAY_PCT, math.log(DECAY_TARGET) / FAST_DECAY_PCT,
                                CONV_W)).astype(np.float32).reshape(CONV_W, 1)
    tr = min(1024, 2 * n)
    sq = _full((FILT_WIDTH, FILT_WIDTH))
    vec = _full((FILT_WIDTH, 1))
    return pl.pallas_call(
        _filter_kernel,
        grid=(2 * n // tr,),
        in_specs=[pl.BlockSpec((LANES, tr), lambda i: (0, i)), _full((FILT_WIDTH, LANES)), vec, sq, vec, sq, vec,
                  _full((2 * CONV_W, FILT_WIDTH)), vec, _full((CONV_W, 1))],
        out_specs=[pl.BlockSpec((tr, CONV_W), lambda i: (i, 0)), _full((1, CONV_W))],
        out_shape=[jax.ShapeDtypeStruct((2 * n, CONV_W), F32), jax.ShapeDtypeStruct((1, CONV_W), F32)],
        compiler_params=_params(("arbitrary",)),
        name="hyena_filter",
    )(feats_t, w1t, col(b1), w2.T, col(b2), w3.T, col(b3), w4.T, col(freq), jnp.asarray(deltas))


def _twiddle(idx, mod):
    ang = 2.0 * np.pi * (idx % mod) / mod
    return np.cos(ang), -np.sin(ang)


def _real_form(mr, mi):
    return np.concatenate([np.concatenate([mr, -mi], -1), np.concatenate([mi, mr], -1)], -2)


@functools.lru_cache(maxsize=None)
def _dft_tables(n2):
    n1 = DFT_N1
    n = n1 * n2
    h = n2 // 2
    a2 = np.arange(n2)
    fr, fi = _twiddle(np.outer(a2, a2), n2)
    m_data = _real_form(fr[:, :h], fi[:, :h])
    m_filt = np.concatenate([fr, fi], 0)
    k2 = a2[:, None, None]
    k1 = np.arange(n1)[None, :, None]
    c1 = np.arange(n1)[None, None, :]
    g = _real_form(*_twiddle(c1 * (n2 * k1 + k2), n))
    a1 = np.arange(n1)
    f1 = _real_form(*_twiddle(np.outer(a1, a1), n1))
    t2 = a1[:, None, None]
    t1 = np.arange(h)[None, :, None]
    j1 = a2[None, None, :]
    hh = _real_form(*_twiddle(j1 * (n1 * t1 + t2), n))
    cast = lambda m: np.asarray(m, dtype=BF16)
    return cast(m_data), cast(m_filt), cast(g), cast(f1), cast(hh)


DFT_ROW_CHUNK = SUBLANES


def _dft_rows_kernel(m_ref, *refs):
    o_ref = refs[-1]
    cols = [jnp.concatenate([r[0, :, t, :] for r in refs[:-1]], axis=0) for t in range(DFT_ROW_CHUNK)]
    rhs = jnp.concatenate(cols, axis=1).astype(BF16)
    out = jnp.dot(m_ref[...], rhs, preferred_element_type=F32)
    half = out.shape[0] // 2
    o_ref[0, 0] = out[:half].astype(o_ref.dtype)
    o_ref[0, 1] = out[half:].astype(o_ref.dtype)


def _dft_stage1(mat, views, pairs, rows, n2, c):
    nin = len(views)
    lc = DFT_ROW_CHUNK * c
    in_specs = [_full(mat.shape)] + [
        pl.BlockSpec((1, rows, DFT_ROW_CHUNK, c), (lambda p, j, a=a: (nin * p + a, 0, j, 0))) for a in range(nin)]
    return pl.pallas_call(
        _dft_rows_kernel,
        grid=(pairs, DFT_N1 // DFT_ROW_CHUNK),
        in_specs=in_specs,
        out_specs=pl.BlockSpec((1, 2, n2, lc), lambda p, j: (p, 0, 0, j)),
        out_shape=jax.ShapeDtypeStruct((pairs, 2, n2, DFT_N1 * c), BF16),
        compiler_params=_params(("arbitrary", "arbitrary")),
        name="dft_stage1",
    )(mat, *views)


def _spectrum_kernel(a_ref, g_ref, sc_ref, o_ref, *, kb):
    for j in range(kb):
        rhs = jnp.concatenate([a_ref[0, 0, j], a_ref[0, 1, j]], axis=0)
        x = jnp.dot(g_ref[j], rhs, preferred_element_type=F32)
        o_ref[0, j] = (x[:DFT_N1] * sc_ref[...]).astype(o_ref.dtype)
        o_ref[1, j] = (x[DFT_N1:] * sc_ref[...]).astype(o_ref.dtype)


def _filter_spectrum(a, g, scale, n2, kb):
    c = a.shape[-1]
    return pl.pallas_call(
        functools.partial(_spectrum_kernel, kb=kb),
        grid=(n2 // kb,),
        in_specs=[pl.BlockSpec((1, 2, kb, DFT_N1, c), lambda k: (0, 0, k, 0, 0)),
                  pl.BlockSpec((kb, 2 * DFT_N1, 2 * DFT_N1), lambda k: (k, 0, 0)), _full((1, c))],
        out_specs=pl.BlockSpec((2, kb, DFT_N1, c), lambda k: (0, k, 0, 0)),
        out_shape=jax.ShapeDtypeStruct((2, n2, DFT_N1, c), BF16),
        compiler_params=_params(("arbitrary",)),
        name="filter_spectrum",
    )(a, g, scale)


def _dft_mid_kernel(a_ref, g_ref, kh_ref, f_ref, o_ref, *, kb):
    for j in range(kb):
        rhs = jnp.concatenate([a_ref[0, 0, j], a_ref[0, 1, j]], axis=0)
        x = jnp.dot(g_ref[j], rhs, preferred_element_type=F32)
        xr, xi = x[:DFT_N1], x[DFT_N1:]
        kr, ki = kh_ref[0, j].astype(F32), kh_ref[1, j].astype(F32)
        yr = xr * kr - xi * ki
        yi = xr * ki + xi * kr
        v = jnp.concatenate([yr, -yi], axis=0).astype(BF16)
        b = jnp.dot(f_ref[...], v, preferred_element_type=F32)
        o_ref[0, 0, :, j, :] = b[:DFT_N1]
        o_ref[0, 1, :, j, :] = b[DFT_N1:]


def _dft_mid(a, g, khat, f1, pairs, n2, kb):
    c = a.shape[-1]
    blk = pl.BlockSpec((1, 2, kb, DFT_N1, c), lambda k, p: (p, 0, k, 0, 0))
    return pl.pallas_call(
        functools.partial(_dft_mid_kernel, kb=kb),
        grid=(n2 // kb, pairs),
        in_specs=[blk, pl.BlockSpec((kb, 2 * DFT_N1, 2 * DFT_N1), lambda k, p: (k, 0, 0)),
                  pl.BlockSpec((2, kb, DFT_N1, c), lambda k, p: (0, k, 0, 0)), _full(f1.shape)],
        out_specs=pl.BlockSpec((1, 2, DFT_N1, kb, c), lambda k, p: (p, 0, 0, k, 0)),
        out_shape=jax.ShapeDtypeStruct((pairs, 2, DFT_N1, n2, c), F32),
        compiler_params=_params(("arbitrary", "arbitrary")),
        name="dft_mid",
    )(a, g, khat, f1)


def _dft_last_kernel(b_ref, h_ref, u_ref, x0_ref, bd_ref, o_ref, *, c):
    half = h_ref.shape[1] // 2
    bd = bd_ref[...]
    for t in range(DFT_ROW_CHUNK):
        rhs = jnp.concatenate([b_ref[0, 0, t], b_ref[0, 1, t]], axis=0).astype(BF16)
        v = jnp.dot(h_ref[t], rhs, preferred_element_type=F32)
        o_ref[0, :, t, :] = (v[:half] + u_ref[0, :, t, :] * bd) * x0_ref[0, :, t, :]
        o_ref[1, :, t, :] = (-v[half:] + u_ref[1, :, t, :] * bd) * x0_ref[1, :, t, :]


def _dft_last(bm, hh, u_view, x0_view, bias_d, pairs, n2, c):
    half = n2 // 2
    tc = DFT_ROW_CHUNK
    io = pl.BlockSpec((2, half, tc, c), lambda p, j: (p, 0, j, 0))
    return pl.pallas_call(
        functools.partial(_dft_last_kernel, c=c),
        grid=(pairs, DFT_N1 // tc),
        in_specs=[pl.BlockSpec((1, 2, tc, n2, c), lambda p, j: (p, 0, j, 0, 0)),
                  pl.BlockSpec((tc, n2, 2 * n2), lambda p, j: (j, 0, 0)), io, io, _full((1, c))],
        out_specs=io,
        out_shape=jax.ShapeDtypeStruct((2 * pairs, half, DFT_N1, c), F32),
        compiler_params=_params(("arbitrary", "arbitrary")),
        name="dft_last",
    )(bm, hh, u_view, x0_view, bias_d)


def _long_conv_mixer(u, x0, kfilt, ksum, bias_d):
    b, n, c = u.shape
    n2 = 2 * n // DFT_N1
    half = n2 // 2
    pairs = b // 2
    m_data, m_filt, g, f1, hh = (jnp.asarray(t) for t in _dft_tables(n2))
    kb = min(8, n2)
    scale = 1.0 / (ksum * float(DFT_N1 * n2))
    ka = _dft_stage1(m_filt, [kfilt.reshape(1, n2, DFT_N1, c)], 1, n2, n2, c)
    khat = _filter_spectrum(ka.reshape(1, 2, n2, DFT_N1, c), g, scale, n2, kb)
    u_view = u.reshape(b, half, DFT_N1, c)
    a = _dft_stage1(m_data, [u_view, u_view], pairs, half, n2, c)
    bm = _dft_mid(a.reshape(pairs, 2, n2, DFT_N1, c), g, khat, f1, pairs, n2, kb)
    out = _dft_last(bm, hh, u_view, x0.reshape(b, half, DFT_N1, c),
                    bias_d.reshape(1, c), pairs, n2, c)
    return out.reshape(b, n, c)


@functools.lru_cache(maxsize=None)
def _small_dft_tables(n):
    big = 2 * n
    a = np.arange(big)
    fr, fi = _twiddle(np.outer(a, a), big)
    cast = lambda m: np.asarray(m, dtype=BF16)
    return (cast(np.concatenate([fr, fi], 0)),
            cast(_real_form(fr[:, :n], fi[:, :n])),
            cast(_real_form(fr[:n], fi[:n])))


def _small_conv_kernel(k_ref, ks_ref, u_ref, x0_ref, bd_ref, mf_ref, md_ref, mi_ref, o_ref, *, n):
    big = 2 * n
    kh = jnp.dot(mf_ref[...], k_ref[...].astype(BF16), preferred_element_type=F32) * (1.0 / (ks_ref[...] * big))
    kr, ki = kh[:big], kh[big:]
    rhs = jnp.concatenate([u_ref[0], u_ref[1]], axis=0).astype(BF16)
    x = jnp.dot(md_ref[...], rhs, preferred_element_type=F32)
    xr, xi = x[:big], x[big:]
    v = jnp.concatenate([xr * kr - xi * ki, -(xr * ki + xi * kr)], axis=0).astype(BF16)
    y = jnp.dot(mi_ref[...], v, preferred_element_type=F32)
    bd = bd_ref[...]
    o_ref[0] = ((y[:n] + u_ref[0] * bd) * x0_ref[0]).astype(o_ref.dtype)
    o_ref[1] = ((-y[n:] + u_ref[1] * bd) * x0_ref[1]).astype(o_ref.dtype)


def _small_conv_mixer(u, x0, kfilt, ksum, bias_d):
    b, n, c = u.shape
    mf, md, mi = (jnp.asarray(t) for t in _small_dft_tables(n))
    io = pl.BlockSpec((2, n, c), lambda p: (p, 0, 0))
    return pl.pallas_call(
        functools.partial(_small_conv_kernel, n=n),
        grid=(b // 2,),
        in_specs=[_full((2 * n, c)), _full((1, c)), io, io, _full((1, c)),
                  _full(mf.shape), _full(md.shape), _full(mi.shape)],
        out_specs=io,
        out_shape=jax.ShapeDtypeStruct((b, n, c), BF16),
        compiler_params=_params(("arbitrary",)),
        name="small_conv",
    )(kfilt, ksum, u, x0, bias_d.reshape(1, c), mf, md, mi)


MXU_DIM = 256
FFN_CHUNK_EDGES = (0, 6 * MXU_DIM, D_FF)


def _gelu_tanh(x):
    return 0.5 * x * (1.0 + jnp.tanh(math.sqrt(2.0 / math.pi) * (x + 0.044715 * (x * x * x))))


def _ffn_kernel(x_ref, xp_ref, xn_ref, oa_ref, oap_ref, oan_ref, oc_ref, ocp_ref, ocn_ref, wa_ref, wc_ref, g1_ref,
                sh_ref, sc_ref, gt_ref, g_ref, wu_ref, cw_ref, cb_ref, wd_ref, o_ref):
    i = pl.program_id(1)
    nt = pl.num_programs(1)
    tm = x_ref.shape[1]

    def halo(p_ref, n_ref):
        return jnp.concatenate([p_ref[0][-SUBLANES:], n_ref[0][:SUBLANES]], axis=0)

    def mixer_out(oa, oc):
        return (jnp.dot(oa.astype(BF16), wa_ref[...], preferred_element_type=F32)
                + jnp.dot(oc.astype(BF16), wc_ref[...], preferred_element_type=F32))

    g1 = g1_ref[0]
    xm = x_ref[0] + g1 * mixer_out(oa_ref[0], oc_ref[0])
    xh = halo(xp_ref, xn_ref) + g1 * mixer_out(halo(oap_ref, oan_ref), halo(ocp_ref, ocn_ref))
    ext = jnp.concatenate([xh[:SUBLANES], xm, xh[SUBLANES:]], axis=0)
    r = tm + 2 * SUBLANES
    h = _modulated_norm(ext, g_ref[...], sh_ref[0], sc_ref[0])
    row = lax.broadcasted_iota(jnp.int32, (r, 1), 0)
    inside = jnp.logical_and(jnp.logical_or(i > 0, row >= SUBLANES), jnp.logical_or(i < nt - 1, row < tm + SUBLANES))
    h = jnp.where(inside, h, 0.0).astype(BF16)
    hm = h[SUBLANES:tm + SUBLANES]
    acc = jnp.zeros((tm, D_MODEL), F32)
    for c0, c1 in zip(FFN_CHUNK_EDGES[:-1], FFN_CHUNK_EDGES[1:]):
        a = jnp.dot(h, wu_ref[:, c0:c1], preferred_element_type=F32)
        v = jnp.dot(hm, wu_ref[:, D_FF + c0:D_FF + c1], preferred_element_type=F32)
        ap = pltpu.roll(a, 1, 0)[SUBLANES:tm + SUBLANES]
        an = pltpu.roll(a, r - 1, 0)[SUBLANES:tm + SUBLANES]
        cw = cw_ref[:, c0:c1]
        conv = ap * cw[0:1] + a[SUBLANES:tm + SUBLANES] * cw[1:2] + an * cw[2:3] + cb_ref[:, c0:c1]
        act = (_gelu_tanh(conv) * v).astype(BF16)
        acc = acc + jnp.dot(act, wd_ref[c0:c1, :], preferred_element_type=F32)
    o_ref[0] = xm + gt_ref[0] * acc


def _mixer_out_ffn(x, oa, oc, wo_bf, gate1, sh, sc, gate2, gain, wu_bf, conv_w, conv_b, wd_bf):
    bx, t, d = x.shape
    tm = min(512, t)
    vec = pl.BlockSpec((1, 1, d), lambda b, i: (b, 0, 0))
    once = pl.Buffered(1)
    return pl.pallas_call(
        _ffn_kernel,
        grid=(bx, t // tm),
        in_specs=_halo_specs(tm, t, d) + _halo_specs(tm, t, Q_W, oa.dtype) + _halo_specs(tm, t, CONV_W, oc.dtype) + [
            pl.BlockSpec((Q_W, d), lambda b, i: (0, 0)), pl.BlockSpec((CONV_W, d), lambda b, i: (1, 0)),
            vec, vec, vec, vec, _full((1, d)),
            pl.BlockSpec((d, 2 * D_FF), lambda b, i: (0, 0), pipeline_mode=once),
            _full((3, D_FF)), _full((1, D_FF)),
            pl.BlockSpec((D_FF, d), lambda b, i: (0, 0), pipeline_mode=once)],
        out_specs=pl.BlockSpec((1, tm, d), lambda b, i: (b, i, 0)),
        out_shape=jax.ShapeDtypeStruct((bx, t, d), F32),
        compiler_params=_params(("arbitrary", "arbitrary")),
        name="mixer_out_ffn",
    )(x, x, x, oa, oa, oa, oc, oc, oc, wo_bf, wo_bf, gate1, sh, sc, gate2, gain, wu_bf, conv_w,
      conv_b.reshape(1, D_FF), wd_bf)


@functools.lru_cache(maxsize=None)
def _rope_tables(t):
    pos = np.arange(t)
    n_freq = HEAD_DIM // 4
    inv = ROPE_THETA ** (-np.arange(n_freq, dtype=np.float64) / n_freq)
    ang_r = (pos // GRID_W)[:, None] * inv
    ang_c = (pos % GRID_W)[:, None] * inv
    zero = np.zeros_like(ang_r)
    cos = np.concatenate([np.cos(ang_r)] * 2 + [np.cos(ang_c)] * 2, 1)
    sa = np.concatenate([-np.sin(ang_r), zero, -np.sin(ang_c), zero], 1)
    sb = np.concatenate([zero, np.sin(ang_r), zero, np.sin(ang_c)], 1)
    tile = lambda m: np.tile(m, (1, LANES // HEAD_DIM)).astype(np.float32)
    return tile(cos), tile(sa), tile(sb)


@functools.lru_cache(maxsize=None)
def _identity_rope_tables(t):
    return np.ones((t, LANES), np.float32), np.zeros((t, LANES), np.float32), np.zeros((t, LANES), np.float32)


@functools.lru_cache(maxsize=None)
def _head_block_diag():
    hid = np.arange(Q_W) // HEAD_DIM
    return np.asarray(hid[:, None] == hid[None, :], dtype=BF16)


def kernel(x, c, ctx, c_ctx, ada_w, ada_b, norm_mix, norm_ffn, mix_w_in, mix_w_out, attn_q_norm, attn_k_norm,
           swa_sink, hy_conv_w, hy_conv_b, hy_w1, hy_b1, hy_w2, hy_b2, hy_w3, hy_b3, hy_w4, hy_freq, hy_bias_d,
           sc_conv_w, ffn_w_up, ffn_conv_w, ffn_conv_b, ffn_w_down):
    b, s, d = x.shape
    s_ctx = ctx.shape[1]
    depth = ada_w.shape[0]
    assert d == D_MODEL and b % 2 == 0 and b + 1 <= SUBLANES and s % 1024 == 0 and s_ctx % SUBLANES == 0

    cvec = jnp.concatenate([c, c_ctx[None, :], jnp.zeros((SUBLANES - b - 1, d), F32)], axis=0)
    mods = _mods(cvec, ada_w, ada_b)
    rope = [jnp.asarray(t) for t in _rope_tables(s)]
    rope_ctx = [jnp.asarray(t) for t in _identity_rope_tables(s_ctx)]
    bd = jnp.asarray(_head_block_diag())
    xc = ctx

    for i in range(depth):
        last = i == depth - 1
        j = i // 2
        lat = [mods[i, :b, k * d:(k + 1) * d][:, None, :] for k in range(6)]
        cx = [jnp.broadcast_to(mods[i, b, k * d:(k + 1) * d][None, None, :], (b, 1, d)) for k in range(6)]
        w_in = _layer_weight_bf16(mix_w_in, i)
        w_out = _layer_weight_bf16(mix_w_out, i)
        w_up = _layer_weight_bf16(ffn_w_up, i)
        w_down = _layer_weight_bf16(ffn_w_down, i)
        g_mix = norm_mix[i].reshape(1, d)
        g_ffn = norm_ffn[i].reshape(1, d)
        qg = jnp.tile(attn_q_norm[i], N_Q_HEADS).reshape(1, Q_W)
        kg = jnp.tile(attn_k_norm[i], N_KV_HEADS).reshape(1, KV_W)

        conv = (hy_conv_w[j], hy_conv_b[j]) if i % 2 == 0 else (sc_conv_w[j],)
        q, k, vt, *mix = _inproj(x, lat[0], lat[1], g_mix, w_in, qg, kg, bd, *rope, *conv)
        qc, kc, vct, *mixc = _inproj(xc, cx[0], cx[1], g_mix, w_in, qg, kg, bd, *rope_ctx, *conv)

        bound = _score_bound(attn_q_norm[i], attn_k_norm[i])
        if i % 2 == 0:
            fargs = (hy_w1[j], hy_b1[j], hy_w2[j], hy_b2[j], hy_w3[j], hy_b3[j], hy_w4[j], hy_freq[j])
            o_attn = _dense_attention(q, k, vt, kc, vct, bound=bound)
            kf, ks = _implicit_filter(s, *fargs)
            o_conv = _long_conv_mixer(*mix, kf, ks, hy_bias_d[j])
            if not last:
                oc_attn = _dense_attention(qc, kc, vct, bound=bound)
                kfc, ksc = _implicit_filter(s_ctx, *fargs)
                oc_conv = _small_conv_mixer(*mixc, kfc, ksc, hy_bias_d[j])
        else:
            sink = swa_sink[j].reshape(N_KV_HEADS, GROUP)
            o_attn = _banded_attention(q, k, vt, kc, vct, sink, bound)
            o_conv = mix[0]
            if not last:
                oc_attn = _dense_attention(qc, kc, vct, sink=sink)
                oc_conv = mixc[0]

        ffn = (g_ffn, w_up, ffn_conv_w[i], ffn_conv_b[i], w_down)
        x = _mixer_out_ffn(x, o_attn, o_conv, w_out, lat[2], lat[3], lat[4], lat[5], *ffn)
        if not last:
            xc = _mixer_out_ffn(xc, oc_attn, oc_conv, w_out, cx[2], cx[3], cx[4], cx[5], *ffn)
    return x
```

```python
import functools
import math

import numpy as np
import jax
import jax.numpy as jnp
from jax import lax
from jax.experimental import pallas as pl
from jax.experimental.pallas import tpu as pltpu

F32 = jnp.float32
BF16 = jnp.bfloat16
HI = lax.Precision.HIGHEST

D_MODEL = 1024
GRID_W = 64
HEAD_DIM = 64
N_Q_HEADS = 8
N_KV_HEADS = 4
GROUP = N_Q_HEADS // N_KV_HEADS
Q_W = N_Q_HEADS * HEAD_DIM
KV_W = N_KV_HEADS * HEAD_DIM
QKV_W = Q_W + 2 * KV_W
CONV_W = D_MODEL // 2
MIX_IN_W = QKV_W + 3 * CONV_W
WINDOW = 128
ROPE_THETA = 10000.0
FILT_EMB = 33
FILT_WIDTH = 64
DECAY_TARGET = 1e-2
FAST_DECAY_PCT = 0.3
SLOW_DECAY_PCT = 1.5
DECAY_SHIFT = 0.05
D_FF = 2816
NEG_INF = -1e30
RMS_EPS = 1e-6
LOG2E = 1.4426950408889634
Q_SCALE = HEAD_DIM ** -0.5 * LOG2E

LANES = 128
SUBLANES = 8
DFT_N1 = 128
VMEM_LIMIT_MB = 56
PROJ_SUB_ROWS = 128


def _params(sem, vmem_mb=VMEM_LIMIT_MB):
    return pltpu.CompilerParams(dimension_semantics=sem, vmem_limit_bytes=vmem_mb * 1024 * 1024)


def _full(shape):
    nd = len(shape)
    return pl.BlockSpec(shape, lambda *_: (0,) * nd)


def _mods_kernel(c_ref, w_ref, b_ref, o_ref):
    c = c_ref[...]
    s = c / (1.0 + jnp.exp(-c))
    o_ref[0] = jnp.dot(s, w_ref[0], preferred_element_type=F32, precision=HI) + b_ref[0]


def _mods(cvec, ada_w, ada_b):
    depth, d, n6 = ada_w.shape
    tn = 512
    return pl.pallas_call(
        _mods_kernel,
        grid=(depth, n6 // tn),
        in_specs=[_full((SUBLANES, d)),
                  pl.BlockSpec((1, d, tn), lambda l, j: (l, 0, j)),
                  pl.BlockSpec((1, 1, tn), lambda l, j: (l, 0, j))],
        out_specs=pl.BlockSpec((1, SUBLANES, tn), lambda l, j: (l, 0, j)),
        out_shape=jax.ShapeDtypeStruct((depth, SUBLANES, n6), F32),
        compiler_params=_params(("arbitrary", "arbitrary")),
        name="ada_mods",
    )(cvec, ada_w, ada_b.reshape(depth, 1, n6))


def _modulated_norm(x, gain, shift, scale):
    ms = jnp.mean(x * x, axis=-1, keepdims=True)
    return (x * lax.rsqrt(ms + RMS_EPS)) * gain * (1.0 + scale) + shift


WEIGHT_CAST_STEPS = 8


def _cast_kernel(*refs):
    n = len(refs) // 2
    for w_ref, o_ref in zip(refs[:n], refs[n:]):
        o_ref[...] = w_ref[0].astype(o_ref.dtype)


def _weights_bf16(stacked):
    depth = stacked[0].shape[0]
    in_specs, out_specs, out_shape, args = [], [], [], []
    for w in stacked:
        _, rows, cols = w.shape
        tr = rows // WEIGHT_CAST_STEPS
        for layer in range(depth):
            in_specs.append(pl.BlockSpec((1, tr, cols), lambda r, layer=layer: (layer, r, 0)))
            out_specs.append(pl.BlockSpec((tr, cols), lambda r: (r, 0)))
            out_shape.append(jax.ShapeDtypeStruct((rows, cols), BF16))
            args.append(w)
    outs = pl.pallas_call(
        _cast_kernel,
        grid=(WEIGHT_CAST_STEPS,),
        in_specs=in_specs,
        out_specs=out_specs,
        out_shape=out_shape,
        compiler_params=_params(("arbitrary",)),
        name="weights_bf16",
    )(*args)
    return [outs[k * depth:(k + 1) * depth] for k in range(len(stacked))]


def _halo_specs(tm, t, width, dtype=F32):
    rows = SUBLANES * 4 // jnp.dtype(dtype).itemsize
    nb = tm // rows
    last = t // rows - 1
    return [pl.BlockSpec((1, tm, width), lambda b, i: (b, i, 0)),
            pl.BlockSpec((1, rows, width), lambda b, i: (b, jnp.maximum(i * nb - 1, 0), 0)),
            pl.BlockSpec((1, rows, width), lambda b, i: (b, jnp.minimum((i + 1) * nb, last), 0))]


def _inproj_kernel(*refs, mode):
    (x_ref, xp_ref, xn_ref, sh_ref, sc_ref, g_ref, w_ref, qg_ref, kg_ref, bd_ref,
     cos_ref, sa_ref, sb_ref, cw_ref) = refs[:14]
    if mode == "hyena":
        cb_ref, q_ref, k_ref, vt_ref, u_ref, x0_ref = refs[14:]
    else:
        q_ref, k_ref, vt_ref, oc_ref = refs[14:]
    i = pl.program_id(1)
    nt = pl.num_programs(1)
    tm = x_ref.shape[1]
    sub = min(PROJ_SUB_ROWS, tm)
    gain, shift, scale = g_ref[...], sh_ref[0], sc_ref[0]

    xh = jnp.concatenate([xp_ref[0], xn_ref[0]], axis=0)
    zh = jnp.dot(_modulated_norm(xh, gain, shift, scale).astype(BF16), w_ref[:, QKV_W:], preferred_element_type=F32)
    before = jnp.where(i > 0, zh[:SUBLANES], 0.0)
    after = jnp.where(i < nt - 1, zh[SUBLANES:], 0.0)
    cw = cw_ref[...]

    def conv_stage(ext, rows):
        n = sub + 2 * SUBLANES
        mid = slice(SUBLANES, sub + SUBLANES)
        if mode == "hyena":
            c = (pltpu.roll(ext, 1, 0)[mid] * cw[0:1] + ext[mid] * cw[1:2] + pltpu.roll(ext, n - 1, 0)[mid] * cw[2:3]
                 + cb_ref[...])
            x0_ref[0, rows] = c[:, :CONV_W]
            u_ref[0, rows] = c[:, 2 * CONV_W:] * c[:, CONV_W:2 * CONV_W]
        else:
            pr = ext[:, CONV_W:2 * CONV_W] * ext[:, 2 * CONV_W:]
            conv = pltpu.roll(pr, 1, 0)[mid] * cw[0:1] + pr[mid] * cw[1:2] + pltpu.roll(pr, n - 1, 0)[mid] * cw[2:3]
            oc_ref[0, rows] = (ext[mid, :CONV_W] * conv).astype(oc_ref.dtype)

    def head_norm_rope(t, hgain, bd, cos, sa, sb):
        ssq = jnp.dot((t * t).astype(BF16), bd, preferred_element_type=F32)
        t = t * lax.rsqrt(ssq * (1.0 / HEAD_DIM) + RMS_EPS) * hgain
        outs = []
        for j in range(t.shape[1] // LANES):
            tj = t[:, j * LANES:(j + 1) * LANES]
            outs.append(tj * cos + pltpu.roll(tj, LANES - 16, 1) * sa + pltpu.roll(tj, 16, 1) * sb)
        return outs

    held = None
    for r0 in range(0, tm, sub):
        rows = slice(r0, r0 + sub)
        h = _modulated_norm(x_ref[0, rows], gain, shift, scale)
        hb = h.astype(BF16)
        z = jnp.dot(hb, w_ref[:, :QKV_W], preferred_element_type=F32)
        zc = jnp.dot(hb, w_ref[:, QKV_W:], preferred_element_type=F32)
        if held is not None:
            conv_stage(jnp.concatenate([before, held[0], zc[:SUBLANES]], axis=0), held[1])
            before = held[0][-SUBLANES:]
        held = (zc, rows)
        tabs = (cos_ref[rows], sa_ref[rows], sb_ref[rows])
        qs = head_norm_rope(z[:, :Q_W], qg_ref[...], bd_ref[...], *tabs)
        ks = head_norm_rope(z[:, Q_W:Q_W + KV_W], kg_ref[...], bd_ref[:KV_W, :KV_W], *tabs)
        for j, qj in enumerate(qs):
            qj = (qj * Q_SCALE).astype(BF16)
            q_ref[0, 2 * j, rows] = qj[:, :HEAD_DIM]
            q_ref[0, 2 * j + 1, rows] = qj[:, HEAD_DIM:]
        for j, kj in enumerate(ks):
            kj = kj.astype(BF16)
            k_ref[0, 2 * j, rows] = kj[:, :HEAD_DIM]
            k_ref[0, 2 * j + 1, rows] = kj[:, HEAD_DIM:]
        vt = z[:, Q_W + KV_W:QKV_W].T
        ones_row = (lax.broadcasted_iota(jnp.int32, (LANES - HEAD_DIM, sub), 0) == 0).astype(F32)
        for hh in range(N_KV_HEADS):
            vt_ref[0, hh, :, rows] = jnp.concatenate([vt[hh * HEAD_DIM:(hh + 1) * HEAD_DIM], ones_row],
                                                     axis=0).astype(BF16)

    conv_stage(jnp.concatenate([before, held[0], after], axis=0), held[1])


def _inproj(x, sh, sc, gain, w_bf, qg, kg, bd, cos, sa, sb, conv_w, conv_b=None):
    bx, t, d = x.shape
    tm = min(512, t)
    mode = "hyena" if conv_b is not None else "short"
    vec = pl.BlockSpec((1, 1, d), lambda b, i: (b, 0, 0))
    tab = pl.BlockSpec((tm, LANES), lambda b, i: (i, 0))
    row = lambda w: pl.BlockSpec((1, tm, w), lambda b, i: (b, i, 0))
    in_specs = _halo_specs(tm, t, d) + [vec, vec, _full((1, d)), _full((d, MIX_IN_W)), _full((1, Q_W)),
                                        _full((1, KV_W)), _full((Q_W, Q_W)), tab, tab, tab, _full(conv_w.shape)]
    args = [x, x, x, sh, sc, gain, w_bf, qg, kg, bd, cos, sa, sb, conv_w]
    out_specs = [pl.BlockSpec((1, N_Q_HEADS, tm, HEAD_DIM), lambda b, i: (b, 0, i, 0)),
                 pl.BlockSpec((1, N_KV_HEADS, tm, HEAD_DIM), lambda b, i: (b, 0, i, 0)),
                 pl.BlockSpec((1, N_KV_HEADS, LANES, tm), lambda b, i: (b, 0, 0, i))]
    out_shape = [jax.ShapeDtypeStruct((bx, N_Q_HEADS, t, HEAD_DIM), BF16),
                 jax.ShapeDtypeStruct((bx, N_KV_HEADS, t, HEAD_DIM), BF16),
                 jax.ShapeDtypeStruct((bx, N_KV_HEADS, LANES, t), BF16)]
    if mode == "hyena":
        in_specs.append(_full((1, 3 * CONV_W)))
        args.append(conv_b.reshape(1, 3 * CONV_W))
        out_specs += [row(CONV_W), row(CONV_W)]
        out_shape += [jax.ShapeDtypeStruct((bx, t, CONV_W), F32)] * 2
    else:
        out_specs.append(row(CONV_W))
        out_shape.append(jax.ShapeDtypeStruct((bx, t, CONV_W), BF16))
    return pl.pallas_call(
        functools.partial(_inproj_kernel, mode=mode),
        grid=(bx, t // tm),
        in_specs=in_specs,
        out_specs=out_specs,
        out_shape=out_shape,
        compiler_params=_params(("arbitrary", "arbitrary")),
        name="in_proj",
    )(*args)


def _score_bound(q_gain, k_gain):
    b = HEAD_DIM * Q_SCALE * BOUND_MARGIN * jnp.max(jnp.abs(q_gain)) * jnp.max(jnp.abs(k_gain))
    return b.reshape(1, 1).astype(F32)


def _scores_t(ks, q):
    return lax.dot_general(ks, q, (((1,), (1,)), ((), ())), preferred_element_type=F32)


def _sink_column(sink_ref, hh, tq, m, acc):
    col = lax.broadcasted_iota(jnp.int32, m.shape, 1)
    sink = jnp.where(col < tq, sink_ref[hh, 0], sink_ref[hh, 1]) * LOG2E
    m_new = jnp.maximum(m, sink)
    den_row = lax.broadcasted_iota(jnp.int32, acc.shape, 0) == HEAD_DIM
    return acc * jnp.exp2(m - m_new) + jnp.where(den_row, jnp.exp2(sink - m_new), 0.0)


def _attn_finish(acc, tq):
    o = (acc / acc[HEAD_DIM:HEAD_DIM + 1]).T
    return jnp.concatenate([o[:tq, :HEAD_DIM], o[tq:, :HEAD_DIM]], axis=1)


EXP_CHUNK_ELEMS = 32 * SUBLANES * LANES


def _score_stage(ks, q, s_ref, bias=None):
    s = _scores_t(ks, q)
    if bias is not None:
        s = s + bias
    s_ref[0:s.shape[0]] = s
    return jnp.max(s, axis=0, keepdims=True)


def _softmax_stage(s_ref, p_ref, n, vt, mx, m, acc_ref):
    m_new = jnp.maximum(m, mx)
    rows = EXP_CHUNK_ELEMS // m.shape[1]
    for c in range(0, n, rows):
        p_ref[c:c + rows] = jnp.exp2(s_ref[c:c + rows] - m_new).astype(BF16)
    acc_ref[...] = acc_ref[...] * jnp.exp2(m - m_new) + jnp.dot(vt, p_ref[0:n], preferred_element_type=F32)
    return m_new


BOUNDED_TILES = 4
BOUND_MARGIN = 1.01
SAFE_SHIFT = 40.0


def _flash_exact(q, k_tile, vt_tile, ctx, sa_ref, sb_ref, p_ref, acc_ref, *, tk, n_main):
    pa_ref, pb_ref = p_ref.at[0:tk], p_ref.at[tk:2 * tk]
    acc_ref[...] = jnp.zeros_like(acc_ref)
    m = jnp.full((1, q.shape[0]), NEG_INF, F32)
    mx_a = _score_stage(k_tile(0), q, sa_ref)
    if n_main > 1:
        def body(i, carry):
            m, mx_a = carry
            mx_b = _score_stage(k_tile(2 * i + 1), q, sb_ref)
            m = _softmax_stage(sa_ref, pa_ref, tk, vt_tile(2 * i), mx_a, m, acc_ref)
            mx_a = _score_stage(k_tile(2 * i + 2), q, sa_ref)
            m = _softmax_stage(sb_ref, pb_ref, tk, vt_tile(2 * i + 1), mx_b, m, acc_ref)
            return m, mx_a

        m, mx_a = lax.fori_loop(0, n_main // 2 - 1, body, (m, mx_a))
        mx_b = _score_stage(k_tile(n_main - 1), q, sb_ref)
        m = _softmax_stage(sa_ref, pa_ref, tk, vt_tile(n_main - 2), mx_a, m, acc_ref)
        last = (sb_ref, pb_ref, mx_b)
        spare = (sa_ref, pa_ref)
    else:
        last = (sa_ref, pa_ref, mx_a)
        spare = (sb_ref, pb_ref)
    if ctx is not None:
        mx_c = _score_stage(ctx[0], q, spare[0])
    m = _softmax_stage(last[0], last[1], tk, vt_tile(n_main - 1), last[2], m, acc_ref)
    if ctx is not None:
        m = _softmax_stage(spare[0], spare[1], ctx[0].shape[0], ctx[1], mx_c, m, acc_ref)
    return m


def _flash_bounded(q, k_ref, vt_ref, ctx, bound, p_ref, acc_ref, *, tk, n_main):
    step = BOUNDED_TILES * tk if n_main % BOUNDED_TILES == 0 else tk

    def group(j):
        off = pl.multiple_of(j * step, step)
        for c in range(0, step, tk):
            s_t = _scores_t(k_ref[0, 0, pl.ds(off + c, tk), :], q)
            p_ref[c:c + tk] = jnp.exp2(s_t - bound).astype(BF16)
        return jnp.dot(vt_ref[0, 0, :, pl.ds(off, step)], p_ref[0:step], preferred_element_type=F32)

    first = group(0)
    if ctx is not None:
        pc_ref = p_ref.at[BOUNDED_TILES * tk:BOUNDED_TILES * tk + ctx[0].shape[0]]
        pc_ref[...] = jnp.exp2(_scores_t(ctx[0], q) - bound).astype(BF16)
        first = first + jnp.dot(ctx[1], pc_ref[...], preferred_element_type=F32)
    acc_ref[...] = first

    def body(j, carry):
        acc_ref[...] += group(j)
        return carry

    lax.fori_loop(1, n_main * tk // step, body, 0)


def _flash_kernel(*refs, tq, tk, n_main, has_ctx, has_sink):
    sa_ref, sb_ref, p_ref, acc_ref = refs[-4:]
    o_ref = refs[-5]
    refs = list(refs[:-5])
    sink_ref = refs.pop(0) if has_sink else None
    bound_ref = None if has_sink else refs.pop(0)
    q_ref, k_ref, vt_ref = refs[:3]
    m_cols = GROUP * tq
    q = q_ref[0].reshape(m_cols, HEAD_DIM)
    ctx = (refs[3][0, 0], refs[4][0, 0]) if has_ctx else None

    def k_tile(j):
        return k_ref[0, 0, pl.ds(pl.multiple_of(j * tk, tk), tk), :]

    def vt_tile(j):
        return vt_ref[0, 0, :, pl.ds(pl.multiple_of(j * tk, tk), tk)]

    exact = functools.partial(_flash_exact, q, k_tile, vt_tile, ctx, sa_ref, sb_ref, p_ref, acc_ref,
                              tk=tk, n_main=n_main)
    if has_sink:
        acc = _sink_column(sink_ref, pl.program_id(1), tq, exact(), acc_ref[...])
    else:
        bound = bound_ref[0, 0]
        safe = bound <= SAFE_SHIFT

        @pl.when(safe)
        def _():
            _flash_bounded(q, k_ref, vt_ref, ctx, bound, p_ref, acc_ref, tk=tk, n_main=n_main)

        @pl.when(jnp.logical_not(safe))
        def _():
            exact()

        acc = acc_ref[...]
    o_ref[0] = _attn_finish(acc, tq).astype(o_ref.dtype)


def _kv_specs(n):
    return [pl.BlockSpec((1, 1, n, HEAD_DIM), lambda b, h, i: (b, h, 0, 0)),
            pl.BlockSpec((1, 1, LANES, n), lambda b, h, i: (b, h, 0, 0))]


def _dense_attention(q, k, vt, kc=None, vct=None, *, bound=None, sink=None):
    bx, _, t, _ = q.shape
    s = k.shape[2]
    tq = min(512, t)
    tk = min(512, s)
    has_ctx = kc is not None
    has_sink = sink is not None
    in_specs = [pl.BlockSpec((1, GROUP, tq, HEAD_DIM), lambda b, h, i: (b, h, i, 0))] + _kv_specs(s)
    args = [q, k, vt]
    if has_ctx:
        in_specs += _kv_specs(kc.shape[2])
        args += [kc, vct]
    in_specs = [pl.BlockSpec(memory_space=pltpu.SMEM)] + in_specs
    args = [sink if has_sink else bound] + args
    n_main = s // tk
    assert n_main == 1 or n_main % 2 == 0
    n_ctx = kc.shape[2] if has_ctx else 0
    assert n_ctx <= tk
    m_cols = GROUP * tq
    return pl.pallas_call(
        functools.partial(_flash_kernel, tq=tq, tk=tk, n_main=n_main, has_ctx=has_ctx, has_sink=has_sink),
        grid=(bx, N_KV_HEADS, t // tq),
        in_specs=in_specs,
        out_specs=pl.BlockSpec((1, tq, LANES), lambda b, h, i: (b, i, h)),
        out_shape=jax.ShapeDtypeStruct((bx, t, Q_W), BF16),
        scratch_shapes=[pltpu.VMEM((tk, m_cols), F32), pltpu.VMEM((tk, m_cols), F32),
                        pltpu.VMEM((BOUNDED_TILES * tk + n_ctx, m_cols), BF16), pltpu.VMEM((LANES, m_cols), F32)],
        compiler_params=_params(("arbitrary", "arbitrary", "arbitrary")),
        name="dense_attention",
    )(*args)


BAND_SUB = 256


@functools.lru_cache(maxsize=None)
def _band_bias():
    span = BAND_SUB + 2 * WINDOW
    kr = np.arange(span)[None, :, None]
    qc = (np.arange(GROUP * BAND_SUB) % BAND_SUB)[None, None, :]
    rel = np.arange(3)[:, None, None]
    return np.where(np.abs(kr - qc - rel * WINDOW) <= WINDOW, 0.0, NEG_INF).astype(np.float32)


def _banded_kernel(sink_ref, bound_ref, q_ref, k_ref, vt_ref, kc_ref, vct_ref, bias_ref, o_ref,
                   sl_ref, sc_ref, pl_ref, pc_ref, acc_ref, *, n_sub, s_len):
    i = pl.program_id(2)
    hh = pl.program_id(1)
    sub = BAND_SUB
    m_cols = GROUP * sub
    span = sub + 2 * WINDOW
    n_ctx = kc_ref.shape[2]

    def window(u):
        q = q_ref[0, :, u * sub:(u + 1) * sub, :].reshape(m_cols, HEAD_DIM)
        q0 = (i * n_sub + u) * sub
        start = pl.multiple_of(jnp.clip(q0 - WINDOW, 0, s_len - span), WINDOW)
        return q, start, bias_ref[(q0 - start) // WINDOW]

    shift = jnp.maximum(bound_ref[0, 0], jnp.maximum(sink_ref[hh, 0], sink_ref[hh, 1]) * LOG2E)
    safe = shift <= SAFE_SHIFT

    @pl.when(safe)
    def _():
        col = lax.broadcasted_iota(jnp.int32, (1, m_cols), 1)
        sink = jnp.where(col < sub, sink_ref[hh, 0], sink_ref[hh, 1]) * LOG2E
        den_row = lax.broadcasted_iota(jnp.int32, (LANES, m_cols), 0) == HEAD_DIM
        sink_den = jnp.where(den_row, jnp.exp2(sink - shift), 0.0)
        for u in range(n_sub):
            q, start, bias = window(u)
            pl_u, pc_u = pl_ref.at[u], pc_ref.at[u]
            pl_u[...] = jnp.exp2(_scores_t(k_ref[0, 0, pl.ds(start, span), :], q) + bias - shift).astype(BF16)
            pc_u[...] = jnp.exp2(_scores_t(kc_ref[0, 0], q) - shift).astype(BF16)
            acc = (jnp.dot(vt_ref[0, 0, :, pl.ds(start, span)], pl_u[...], preferred_element_type=F32)
                   + jnp.dot(vct_ref[0, 0], pc_u[...], preferred_element_type=F32) + sink_den)
            o_ref[0, u * sub:(u + 1) * sub] = _attn_finish(acc, sub).astype(o_ref.dtype)

    @pl.when(jnp.logical_not(safe))
    def _():
        stats = []
        for u in range(n_sub):
            q, start, bias = window(u)
            mx_l = _score_stage(k_ref[0, 0, pl.ds(start, span), :], q, sl_ref.at[u], bias=bias)
            mx_c = _score_stage(kc_ref[0, 0], q, sc_ref.at[u])
            stats.append((start, mx_l, mx_c))
        for u in range(n_sub):
            start, mx_l, mx_c = stats[u]
            acc_u = acc_ref.at[u]
            acc_u[...] = jnp.zeros((LANES, m_cols), F32)
            m = jnp.full((1, m_cols), NEG_INF, F32)
            m = _softmax_stage(sl_ref.at[u], pl_ref.at[u], span, vt_ref[0, 0, :, pl.ds(start, span)], mx_l, m, acc_u)
            m = _softmax_stage(sc_ref.at[u], pc_ref.at[u], n_ctx, vct_ref[0, 0], mx_c, m, acc_u)
            acc = _sink_column(sink_ref, hh, sub, m, acc_u[...])
            o_ref[0, u * sub:(u + 1) * sub] = _attn_finish(acc, sub).astype(o_ref.dtype)


def _banded_attention(q, k, vt, kc, vct, sink, bound):
    bx, _, t, _ = q.shape
    n_ctx = kc.shape[2]
    n_sub = next(k for k in (8, 4, 2, 1) if t % (k * BAND_SUB) == 0)
    tq = n_sub * BAND_SUB
    m_cols = GROUP * BAND_SUB
    span = BAND_SUB + 2 * WINDOW
    bias = jnp.asarray(_band_bias())
    return pl.pallas_call(
        functools.partial(_banded_kernel, n_sub=n_sub, s_len=t),
        grid=(bx, N_KV_HEADS, t // tq),
        in_specs=[pl.BlockSpec(memory_space=pltpu.SMEM), pl.BlockSpec(memory_space=pltpu.SMEM),
                  pl.BlockSpec((1, GROUP, tq, HEAD_DIM), lambda b, h, i: (b, h, i, 0))]
                 + _kv_specs(t) + _kv_specs(n_ctx) + [_full(bias.shape)],
        out_specs=pl.BlockSpec((1, tq, LANES), lambda b, h, i: (b, i, h)),
        out_shape=jax.ShapeDtypeStruct((bx, t, Q_W), BF16),
        scratch_shapes=[pltpu.VMEM((n_sub, span, m_cols), F32), pltpu.VMEM((n_sub, n_ctx, m_cols), F32),
                        pltpu.VMEM((n_sub, span, m_cols), BF16), pltpu.VMEM((n_sub, n_ctx, m_cols), BF16),
                        pltpu.VMEM((n_sub, LANES, m_cols), F32)],
        compiler_params=_params(("arbitrary", "arbitrary", "arbitrary")),
        name="banded_attention",
    )(sink, bound, q, k, vt, kc, vct, bias)


@functools.lru_cache(maxsize=None)
def _filter_features(n):
    j = np.arange(2 * n)
    d = np.where(j <= n, j, 2 * n - j)
    d = np.where(j == n, 0, d)
    bands = (FILT_EMB - 1) // 2
    t01 = np.linspace(0.0, 1.0, n)[d]
    w = 2.0 * np.pi * d.astype(np.float64) / n
    f = np.linspace(1e-4, bands - 1, bands)[None, :]
    feats = np.zeros((2 * n, LANES), np.float64)
    feats[:, 0] = t01
    feats[:, 1:1 + bands] = np.cos(f * w[:, None])
    feats[:, 1 + bands:FILT_EMB] = -np.sin(f * w[:, None])
    feats[:, 64] = t01
    feats[:, 65] = (j < n)
    feats[:, 66] = (j != n)
    return feats.astype(np.float32)


def _filter_kernel(f_ref, w1_ref, b1_ref, w2_ref, b2_ref, w3_ref, b3_ref, w4_ref, fr_ref, dl_ref, k_ref, s_ref):
    f = f_ref[...]
    fr = fr_ref[...]
    mm = lambda a, b: jnp.dot(a, b, preferred_element_type=F32, precision=HI)
    h = jnp.sin(fr * (mm(w1_ref[...], f) + b1_ref[...]))
    h = jnp.sin(fr * (mm(w2_ref[...], h) + b2_ref[...]))
    h = jnp.sin(fr * (mm(w3_ref[...], h) + b3_ref[...]))
    hf = mm(w4_ref[...], h)
    win = jnp.exp(-dl_ref[...] * f[64:65]) + DECAY_SHIFT
    k = (jnp.where(f[65:66] > 0.5, hf[:CONV_W], hf[CONV_W:]) * win * f[66:67]).T
    k_ref[...] = k

    @pl.when(pl.program_id(0) == 0)
    def _():
        s_ref[...] = jnp.zeros_like(s_ref)

    s_ref[...] += jnp.sum(jnp.abs(k), axis=0, keepdims=True)


def _implicit_filter(n, w1, b1, w2, b2, w3, b3, w4, freq):
    feats_t = jnp.asarray(np.ascontiguousarray(_filter_features(n).T))
    col = lambda a: a.reshape(-1, 1)
    w1t = jnp.pad(w1.T, ((0, 0), (0, LANES - FILT_EMB)))
    deltas = np.abs(np.linspace(math.log(DECAY_TARGET) / SLOW_DECAY_PCT, math.log(DECAY_TARGET) / FAST_DECAY_PCT,
                                CONV_W)).astype(np.float32).reshape(CONV_W, 1)
    tr = min(1024, 2 * n)
    sq = _full((FILT_WIDTH, FILT_WIDTH))
    vec = _full((FILT_WIDTH, 1))
    return pl.pallas_call(
        _filter_kernel,
        grid=(2 * n // tr,),
        in_specs=[pl.BlockSpec((LANES, tr), lambda i: (0, i)), _full((FILT_WIDTH, LANES)), vec, sq, vec, sq, vec,
                  _full((2 * CONV_W, FILT_WIDTH)), vec, _full((CONV_W, 1))],
        out_specs=[pl.BlockSpec((tr, CONV_W), lambda i: (i, 0)), _full((1, CONV_W))],
        out_shape=[jax.ShapeDtypeStruct((2 * n, CONV_W), F32), jax.ShapeDtypeStruct((1, CONV_W), F32)],
        compiler_params=_params(("arbitrary",)),
        name="hyena_filter",
    )(feats_t, w1t, col(b1), w2.T, col(b2), w3.T, col(b3), w4.T, col(freq), jnp.asarray(deltas))


def _twiddle(idx, mod):
    ang = 2.0 * np.pi * (idx % mod) / mod
    return np.cos(ang), -np.sin(ang)


def _real_form(mr, mi):
    return np.concatenate([np.concatenate([mr, -mi], -1), np.concatenate([mi, mr], -1)], -2)


@functools.lru_cache(maxsize=None)
def _dft_tables(n2):
    n1 = DFT_N1
    n = n1 * n2
    h = n2 // 2
    a2 = np.arange(n2)
    fr, fi = _twiddle(np.outer(a2, a2), n2)
    m_data = _real_form(fr[:, :h], fi[:, :h])
    m_filt = np.concatenate([fr, fi], 0)
    k2 = a2[:, None, None]
    k1 = np.arange(n1)[None, :, None]
    c1 = np.arange(n1)[None, None, :]
    g = _real_form(*_twiddle(c1 * (n2 * k1 + k2), n))
    a1 = np.arange(n1)
    f1 = _real_form(*_twiddle(np.outer(a1, a1), n1))
    t2 = a1[:, None, None]
    t1 = np.arange(h)[None, :, None]
    j1 = a2[None, None, :]
    hh = _real_form(*_twiddle(j1 * (n1 * t1 + t2), n))
    cast = lambda m: np.asarray(m, dtype=BF16)
    return cast(m_data), cast(m_filt), cast(g), cast(f1), cast(hh)


DFT_ROW_CHUNK = SUBLANES


def _dft_rows_kernel(m_ref, *refs):
    o_ref = refs[-1]
    cols = [jnp.concatenate([r[0, :, t, :] for r in refs[:-1]], axis=0) for t in range(DFT_ROW_CHUNK)]
    rhs = jnp.concatenate(cols, axis=1).astype(BF16)
    out = jnp.dot(m_ref[...], rhs, preferred_element_type=F32)
    half = out.shape[0] // 2
    o_ref[0, 0] = out[:half].astype(o_ref.dtype)
    o_ref[0, 1] = out[half:].astype(o_ref.dtype)


def _dft_stage1(mat, views, pairs, rows, n2, c):
    nin = len(views)
    lc = DFT_ROW_CHUNK * c
    in_specs = [_full(mat.shape)] + [
        pl.BlockSpec((1, rows, DFT_ROW_CHUNK, c), (lambda p, j, a=a: (nin * p + a, 0, j, 0))) for a in range(nin)]
    return pl.pallas_call(
        _dft_rows_kernel,
        grid=(pairs, DFT_N1 // DFT_ROW_CHUNK),
        in_specs=in_specs,
        out_specs=pl.BlockSpec((1, 2, n2, lc), lambda p, j: (p, 0, 0, j)),
        out_shape=jax.ShapeDtypeStruct((pairs, 2, n2, DFT_N1 * c), BF16),
        compiler_params=_params(("arbitrary", "arbitrary")),
        name="dft_stage1",
    )(mat, *views)


def _spectrum_kernel(a_ref, g_ref, sc_ref, o_ref, *, kb):
    for j in range(kb):
        rhs = jnp.concatenate([a_ref[0, 0, j], a_ref[0, 1, j]], axis=0)
        x = jnp.dot(g_ref[j], rhs, preferred_element_type=F32)
        o_ref[0, j] = (x[:DFT_N1] * sc_ref[...]).astype(o_ref.dtype)
        o_ref[1, j] = (x[DFT_N1:] * sc_ref[...]).astype(o_ref.dtype)


def _filter_spectrum(a, g, scale, n2, kb):
    c = a.shape[-1]
    return pl.pallas_call(
        functools.partial(_spectrum_kernel, kb=kb),
        grid=(n2 // kb,),
        in_specs=[pl.BlockSpec((1, 2, kb, DFT_N1, c), lambda k: (0, 0, k, 0, 0)),
                  pl.BlockSpec((kb, 2 * DFT_N1, 2 * DFT_N1), lambda k: (k, 0, 0)), _full((1, c))],
        out_specs=pl.BlockSpec((2, kb, DFT_N1, c), lambda k: (0, k, 0, 0)),
        out_shape=jax.ShapeDtypeStruct((2, n2, DFT_N1, c), BF16),
        compiler_params=_params(("arbitrary",)),
        name="filter_spectrum",
    )(a, g, scale)


def _dft_mid_kernel(a_ref, g_ref, kh_ref, f_ref, o_ref, *, kb):
    for j in range(kb):
        rhs = jnp.concatenate([a_ref[0, 0, j], a_ref[0, 1, j]], axis=0)
        x = jnp.dot(g_ref[j], rhs, preferred_element_type=F32)
        xr, xi = x[:DFT_N1], x[DFT_N1:]
        kr, ki = kh_ref[0, j].astype(F32), kh_ref[1, j].astype(F32)
        yr = xr * kr - xi * ki
        yi = xr * ki + xi * kr
        v = jnp.concatenate([yr, -yi], axis=0).astype(BF16)
        b = jnp.dot(f_ref[...], v, preferred_element_type=F32)
        o_ref[0, 0, :, j, :] = b[:DFT_N1]
        o_ref[0, 1, :, j, :] = b[DFT_N1:]


def _dft_mid(a, g, khat, f1, pairs, n2, kb):
    c = a.shape[-1]
    blk = pl.BlockSpec((1, 2, kb, DFT_N1, c), lambda k, p: (p, 0, k, 0, 0))
    return pl.pallas_call(
        functools.partial(_dft_mid_kernel, kb=kb),
        grid=(n2 // kb, pairs),
        in_specs=[blk, pl.BlockSpec((kb, 2 * DFT_N1, 2 * DFT_N1), lambda k, p: (k, 0, 0)),
                  pl.BlockSpec((2, kb, DFT_N1, c), lambda k, p: (0, k, 0, 0)), _full(f1.shape)],
        out_specs=pl.BlockSpec((1, 2, DFT_N1, kb, c), lambda k, p: (p, 0, 0, k, 0)),
        out_shape=jax.ShapeDtypeStruct((pairs, 2, DFT_N1, n2, c), F32),
        compiler_params=_params(("arbitrary", "arbitrary")),
        name="dft_mid",
    )(a, g, khat, f1)


def _dft_last_kernel(b_ref, h_ref, u_ref, x0_ref, bd_ref, o_ref, *, c):
    half = h_ref.shape[1] // 2
    bd = bd_ref[...]
    for t in range(DFT_ROW_CHUNK):
        rhs = jnp.concatenate([b_ref[0, 0, t], b_ref[0, 1, t]], axis=0).astype(BF16)
        v = jnp.dot(h_ref[t], rhs, preferred_element_type=F32)
        o_ref[0, :, t, :] = (v[:half] + u_ref[0, :, t, :] * bd) * x0_ref[0, :, t, :]
        o_ref[1, :, t, :] = (-v[half:] + u_ref[1, :, t, :] * bd) * x0_ref[1, :, t, :]


def _dft_last(bm, hh, u_view, x0_view, bias_d, pairs, n2, c):
    half = n2 // 2
    tc = DFT_ROW_CHUNK
    io = pl.BlockSpec((2, half, tc, c), lambda p, j: (p, 0, j, 0))
    return pl.pallas_call(
        functools.partial(_dft_last_kernel, c=c),
        grid=(pairs, DFT_N1 // tc),
        in_specs=[pl.BlockSpec((1, 2, tc, n2, c), lambda p, j: (p, 0, j, 0, 0)),
                  pl.BlockSpec((tc, n2, 2 * n2), lambda p, j: (j, 0, 0)), io, io, _full((1, c))],
        out_specs=io,
        out_shape=jax.ShapeDtypeStruct((2 * pairs, half, DFT_N1, c), F32),
        compiler_params=_params(("arbitrary", "arbitrary")),
        name="dft_last",
    )(bm, hh, u_view, x0_view, bias_d)


def _long_conv_mixer(u, x0, kfilt, ksum, bias_d):
    b, n, c = u.shape
    n2 = 2 * n // DFT_N1
    half = n2 // 2
    pairs = b // 2
    m_data, m_filt, g, f1, hh = (jnp.asarray(t) for t in _dft_tables(n2))
    kb = min(8, n2)
    scale = 1.0 / (ksum * float(DFT_N1 * n2))
    ka = _dft_stage1(m_filt, [kfilt.reshape(1, n2, DFT_N1, c)], 1, n2, n2, c)
    khat = _filter_spectrum(ka.reshape(1, 2, n2, DFT_N1, c), g, scale, n2, kb)
    u_view = u.reshape(b, half, DFT_N1, c)
    a = _dft_stage1(m_data, [u_view, u_view], pairs, half, n2, c)
    bm = _dft_mid(a.reshape(pairs, 2, n2, DFT_N1, c), g, khat, f1, pairs, n2, kb)
    out = _dft_last(bm, hh, u_view, x0.reshape(b, half, DFT_N1, c),
                    bias_d.reshape(1, c), pairs, n2, c)
    return out.reshape(b, n, c)


@functools.lru_cache(maxsize=None)
def _small_dft_tables(n):
    big = 2 * n
    a = np.arange(big)
    fr, fi = _twiddle(np.outer(a, a), big)
    cast = lambda m: np.asarray(m, dtype=BF16)
    return (cast(np.concatenate([fr, fi], 0)),
            cast(_real_form(fr[:, :n], fi[:, :n])),
            cast(_real_form(fr[:n], fi[:n])))


def _small_conv_kernel(k_ref, ks_ref, u_ref, x0_ref, bd_ref, mf_ref, md_ref, mi_ref, o_ref, *, n):
    big = 2 * n
    kh = jnp.dot(mf_ref[...], k_ref[...].astype(BF16), preferred_element_type=F32) * (1.0 / (ks_ref[...] * big))
    kr, ki = kh[:big], kh[big:]
    rhs = jnp.concatenate([u_ref[0], u_ref[1]], axis=0).astype(BF16)
    x = jnp.dot(md_ref[...], rhs, preferred_element_type=F32)
    xr, xi = x[:big], x[big:]
    v = jnp.concatenate([xr * kr - xi * ki, -(xr * ki + xi * kr)], axis=0).astype(BF16)
    y = jnp.dot(mi_ref[...], v, preferred_element_type=F32)
    bd = bd_ref[...]
    o_ref[0] = ((y[:n] + u_ref[0] * bd) * x0_ref[0]).astype(o_ref.dtype)
    o_ref[1] = ((-y[n:] + u_ref[1] * bd) * x0_ref[1]).astype(o_ref.dtype)


def _small_conv_mixer(u, x0, kfilt, ksum, bias_d):
    b, n, c = u.shape
    mf, md, mi = (jnp.asarray(t) for t in _small_dft_tables(n))
    io = pl.BlockSpec((2, n, c), lambda p: (p, 0, 0))
    return pl.pallas_call(
        functools.partial(_small_conv_kernel, n=n),
        grid=(b // 2,),
        in_specs=[_full((2 * n, c)), _full((1, c)), io, io, _full((1, c)),
                  _full(mf.shape), _full(md.shape), _full(mi.shape)],
        out_specs=io,
        out_shape=jax.ShapeDtypeStruct((b, n, c), BF16),
        compiler_params=_params(("arbitrary",)),
        name="small_conv",
    )(kfilt, ksum, u, x0, bias_d.reshape(1, c), mf, md, mi)


MXU_DIM = 256
FFN_CHUNK_EDGES = (0, 6 * MXU_DIM, D_FF)


def _gelu_tanh(x):
    return 0.5 * x * (1.0 + jnp.tanh(math.sqrt(2.0 / math.pi) * (x + 0.044715 * (x * x * x))))


def _ffn_kernel(x_ref, xp_ref, xn_ref, oa_ref, oap_ref, oan_ref, oc_ref, ocp_ref, ocn_ref, wa_ref, wc_ref, g1_ref,
                sh_ref, sc_ref, gt_ref, g_ref, wu_ref, cw_ref, cb_ref, wd_ref, o_ref):
    i = pl.program_id(1)
    nt = pl.num_programs(1)
    tm = x_ref.shape[1]

    def halo(p_ref, n_ref):
        return jnp.concatenate([p_ref[0][-SUBLANES:], n_ref[0][:SUBLANES]], axis=0)

    def mixer_out(oa, oc):
        return (jnp.dot(oa.astype(BF16), wa_ref[...], preferred_element_type=F32)
                + jnp.dot(oc.astype(BF16), wc_ref[...], preferred_element_type=F32))

    g1 = g1_ref[0]
    xm = x_ref[0] + g1 * mixer_out(oa_ref[0], oc_ref[0])
    xh = halo(xp_ref, xn_ref) + g1 * mixer_out(halo(oap_ref, oan_ref), halo(ocp_ref, ocn_ref))
    ext = jnp.concatenate([xh[:SUBLANES], xm, xh[SUBLANES:]], axis=0)
    r = tm + 2 * SUBLANES
    h = _modulated_norm(ext, g_ref[...], sh_ref[0], sc_ref[0])
    row = lax.broadcasted_iota(jnp.int32, (r, 1), 0)
    inside = jnp.logical_and(jnp.logical_or(i > 0, row >= SUBLANES), jnp.logical_or(i < nt - 1, row < tm + SUBLANES))
    h = jnp.where(inside, h, 0.0).astype(BF16)
    hm = h[SUBLANES:tm + SUBLANES]
    acc = jnp.zeros((tm, D_MODEL), F32)
    for c0, c1 in zip(FFN_CHUNK_EDGES[:-1], FFN_CHUNK_EDGES[1:]):
        a = jnp.dot(h, wu_ref[:, c0:c1], preferred_element_type=F32)
        v = jnp.dot(hm, wu_ref[:, D_FF + c0:D_FF + c1], preferred_element_type=F32)
        ap = pltpu.roll(a, 1, 0)[SUBLANES:tm + SUBLANES]
        an = pltpu.roll(a, r - 1, 0)[SUBLANES:tm + SUBLANES]
        cw = cw_ref[:, c0:c1]
        conv = ap * cw[0:1] + a[SUBLANES:tm + SUBLANES] * cw[1:2] + an * cw[2:3] + cb_ref[:, c0:c1]
        act = (_gelu_tanh(conv) * v).astype(BF16)
        acc = acc + jnp.dot(act, wd_ref[c0:c1, :], preferred_element_type=F32)
    o_ref[0] = xm + gt_ref[0] * acc


def _mixer_out_ffn(x, oa, oc, wo_bf, gate1, sh, sc, gate2, gain, wu_bf, conv_w, conv_b, wd_bf):
    bx, t, d = x.shape
    tm = min(512, t)
    vec = pl.BlockSpec((1, 1, d), lambda b, i: (b, 0, 0))
    once = pl.Buffered(1)
    return pl.pallas_call(
        _ffn_kernel,
        grid=(bx, t // tm),
        in_specs=_halo_specs(tm, t, d) + _halo_specs(tm, t, Q_W, oa.dtype) + _halo_specs(tm, t, CONV_W, oc.dtype) + [
            pl.BlockSpec((Q_W, d), lambda b, i: (0, 0)), pl.BlockSpec((CONV_W, d), lambda b, i: (1, 0)),
            vec, vec, vec, vec, _full((1, d)),
            pl.BlockSpec((d, 2 * D_FF), lambda b, i: (0, 0), pipeline_mode=once),
            _full((3, D_FF)), _full((1, D_FF)),
            pl.BlockSpec((D_FF, d), lambda b, i: (0, 0), pipeline_mode=once)],
        out_specs=pl.BlockSpec((1, tm, d), lambda b, i: (b, i, 0)),
        out_shape=jax.ShapeDtypeStruct((bx, t, d), F32),
        compiler_params=_params(("arbitrary", "arbitrary")),
        name="mixer_out_ffn",
    )(x, x, x, oa, oa, oa, oc, oc, oc, wo_bf, wo_bf, gate1, sh, sc, gate2, gain, wu_bf, conv_w,
      conv_b.reshape(1, D_FF), wd_bf)


@functools.lru_cache(maxsize=None)
def _rope_tables(t):
    pos = np.arange(t)
    n_freq = HEAD_DIM // 4
    inv = ROPE_THETA ** (-np.arange(n_freq, dtype=np.float64) / n_freq)
    ang_r = (pos // GRID_W)[:, None] * inv
    ang_c = (pos % GRID_W)[:, None] * inv
    zero = np.zeros_like(ang_r)
    cos = np.concatenate([np.cos(ang_r)] * 2 + [np.cos(ang_c)] * 2, 1)
    sa = np.concatenate([-np.sin(ang_r), zero, -np.sin(ang_c), zero], 1)
    sb = np.concatenate([zero, np.sin(ang_r), zero, np.sin(ang_c)], 1)
    tile = lambda m: np.tile(m, (1, LANES // HEAD_DIM)).astype(np.float32)
    return tile(cos), tile(sa), tile(sb)


@functools.lru_cache(maxsize=None)
def _identity_rope_tables(t):
    return np.ones((t, LANES), np.float32), np.zeros((t, LANES), np.float32), np.zeros((t, LANES), np.float32)


@functools.lru_cache(maxsize=None)
def _head_block_diag():
    hid = np.arange(Q_W) // HEAD_DIM
    return np.asarray(hid[:, None] == hid[None, :], dtype=BF16)


def kernel(x, c, ctx, c_ctx, ada_w, ada_b, norm_mix, norm_ffn, mix_w_in, mix_w_out, attn_q_norm, attn_k_norm,
           swa_sink, hy_conv_w, hy_conv_b, hy_w1, hy_b1, hy_w2, hy_b2, hy_w3, hy_b3, hy_w4, hy_freq, hy_bias_d,
           sc_conv_w, ffn_w_up, ffn_conv_w, ffn_conv_b, ffn_w_down):
    b, s, d = x.shape
    s_ctx = ctx.shape[1]
    depth = ada_w.shape[0]
    assert d == D_MODEL and b % 2 == 0 and b + 1 <= SUBLANES and s % 1024 == 0 and s_ctx % SUBLANES == 0

    cvec = jnp.concatenate([c, c_ctx[None, :], jnp.zeros((SUBLANES - b - 1, d), F32)], axis=0)
    mods = _mods(cvec, ada_w, ada_b)
    rope = [jnp.asarray(t) for t in _rope_tables(s)]
    rope_ctx = [jnp.asarray(t) for t in _identity_rope_tables(s_ctx)]
    bd = jnp.asarray(_head_block_diag())
    xc = ctx
    weights = _weights_bf16([mix_w_in, mix_w_out, ffn_w_up, ffn_w_down])

    for i in range(depth):
        last = i == depth - 1
        j = i // 2
        lat = [mods[i, :b, k * d:(k + 1) * d][:, None, :] for k in range(6)]
        cx = [jnp.broadcast_to(mods[i, b, k * d:(k + 1) * d][None, None, :], (b, 1, d)) for k in range(6)]
        w_in, w_out, w_up, w_down = (w[i] for w in weights)
        g_mix = norm_mix[i].reshape(1, d)
        g_ffn = norm_ffn[i].reshape(1, d)
        qg = jnp.tile(attn_q_norm[i], N_Q_HEADS).reshape(1, Q_W)
        kg = jnp.tile(attn_k_norm[i], N_KV_HEADS).reshape(1, KV_W)

        conv = (hy_conv_w[j], hy_conv_b[j]) if i % 2 == 0 else (sc_conv_w[j],)
        q, k, vt, *mix = _inproj(x, lat[0], lat[1], g_mix, w_in, qg, kg, bd, *rope, *conv)
        qc, kc, vct, *mixc = _inproj(xc, cx[0], cx[1], g_mix, w_in, qg, kg, bd, *rope_ctx, *conv)

        bound = _score_bound(attn_q_norm[i], attn_k_norm[i])
        if i % 2 == 0:
            fargs = (hy_w1[j], hy_b1[j], hy_w2[j], hy_b2[j], hy_w3[j], hy_b3[j], hy_w4[j], hy_freq[j])
            o_attn = _dense_attention(q, k, vt, kc, vct, bound=bound)
            kf, ks = _implicit_filter(s, *fargs)
            o_conv = _long_conv_mixer(*mix, kf, ks, hy_bias_d[j])
            if not last:
                oc_attn = _dense_attention(qc, kc, vct, bound=bound)
                kfc, ksc = _implicit_filter(s_ctx, *fargs)
                oc_conv = _small_conv_mixer(*mixc, kfc, ksc, hy_bias_d[j])
        else:
            sink = swa_sink[j].reshape(N_KV_HEADS, GROUP)
            o_attn = _banded_attention(q, k, vt, kc, vct, sink, bound)
            o_conv = mix[0]
            if not last:
                oc_attn = _dense_attention(qc, kc, vct, sink=sink)
                oc_conv = mixc[0]

        ffn = (g_ffn, w_up, ffn_conv_w[i], ffn_conv_b[i], w_down)
        x = _mixer_out_ffn(x, o_attn, o_conv, w_out, lat[2], lat[3], lat[4], lat[5], *ffn)
        if not last:
            xc = _mixer_out_ffn(xc, oc_attn, oc_conv, w_out, cx[2], cx[3], cx[4], cx[5], *ffn)
    return x
```

```python
import functools
import math

import numpy as np
import jax
import jax.numpy as jnp
from jax import lax
from jax.experimental import pallas as pl
from jax.experimental.pallas import tpu as pltpu

F32 = jnp.float32
BF16 = jnp.bfloat16
HI = lax.Precision.HIGHEST

D_MODEL = 1024
GRID_W = 64
HEAD_DIM = 64
N_Q_HEADS = 8
N_KV_HEADS = 4
GROUP = N_Q_HEADS // N_KV_HEADS
Q_W = N_Q_HEADS * HEAD_DIM
KV_W = N_KV_HEADS * HEAD_DIM
QKV_W = Q_W + 2 * KV_W
CONV_W = D_MODEL // 2
MIX_IN_W = QKV_W + 3 * CONV_W
WINDOW = 128
ROPE_THETA = 10000.0
FILT_EMB = 33
FILT_WIDTH = 64
DECAY_TARGET = 1e-2
FAST_DECAY_PCT = 0.3
SLOW_DECAY_PCT = 1.5
DECAY_SHIFT = 0.05
D_FF = 2816
NEG_INF = -1e30
RMS_EPS = 1e-6
LOG2E = 1.4426950408889634
Q_SCALE = HEAD_DIM ** -0.5 * LOG2E

LANES = 128
SUBLANES = 8
DFT_N1 = 128
VMEM_LIMIT_MB = 56
MOD_ROWS = SUBLANES
PROJ_SUB_ROWS = 128


def _params(sem, vmem_mb=VMEM_LIMIT_MB):
    return pltpu.CompilerParams(dimension_semantics=sem, vmem_limit_bytes=vmem_mb * 1024 * 1024)


def _full(shape):
    nd = len(shape)
    return pl.BlockSpec(shape, lambda *_: (0,) * nd)


def _mods_kernel(c_ref, w_ref, b_ref, o_ref):
    c = c_ref[...]
    s = c / (1.0 + jnp.exp(-c))
    o_ref[0] = jnp.dot(s, w_ref[0], preferred_element_type=F32, precision=HI) + b_ref[0]


def _mods(cvec, ada_w, ada_b):
    depth, d, n6 = ada_w.shape
    tn = 1536
    return pl.pallas_call(
        _mods_kernel,
        grid=(depth, n6 // tn),
        in_specs=[_full((MOD_ROWS, d)),
                  pl.BlockSpec((1, d, tn), lambda l, j: (l, 0, j)),
                  pl.BlockSpec((1, 1, tn), lambda l, j: (l, 0, j))],
        out_specs=pl.BlockSpec((1, MOD_ROWS, tn), lambda l, j: (l, 0, j)),
        out_shape=jax.ShapeDtypeStruct((depth, MOD_ROWS, n6), F32),
        compiler_params=_params(("arbitrary", "arbitrary")),
        name="ada_mods",
    )(cvec, ada_w, ada_b.reshape(depth, 1, n6))


def _modulated_norm(x, gain, shift, scale):
    ms = jnp.mean(x * x, axis=-1, keepdims=True)
    return (x * lax.rsqrt(ms + RMS_EPS)) * gain * (1.0 + scale) + shift


WEIGHT_CAST_STEPS = 8


def _cast_kernel(*refs):
    n = len(refs) // 2
    for w_ref, o_ref in zip(refs[:n], refs[n:]):
        o_ref[...] = w_ref[0].astype(o_ref.dtype)


def _weights_bf16(stacked):
    depth = stacked[0].shape[0]
    in_specs, out_specs, out_shape, args = [], [], [], []
    for w in stacked:
        _, rows, cols = w.shape
        tr = rows // WEIGHT_CAST_STEPS
        for layer in range(depth):
            in_specs.append(pl.BlockSpec((1, tr, cols), lambda r, layer=layer: (layer, r, 0)))
            out_specs.append(pl.BlockSpec((tr, cols), lambda r: (r, 0)))
            out_shape.append(jax.ShapeDtypeStruct((rows, cols), BF16))
            args.append(w)
    outs = pl.pallas_call(
        _cast_kernel,
        grid=(WEIGHT_CAST_STEPS,),
        in_specs=in_specs,
        out_specs=out_specs,
        out_shape=out_shape,
        compiler_params=_params(("arbitrary",)),
        name="weights_bf16",
    )(*args)
    return [outs[k * depth:(k + 1) * depth] for k in range(len(stacked))]


def _halo_specs(tm, t, width, dtype=F32):
    rows = SUBLANES * 4 // jnp.dtype(dtype).itemsize
    nb = tm // rows
    last = t // rows - 1
    return [pl.BlockSpec((1, tm, width), lambda b, i: (b, i, 0)),
            pl.BlockSpec((1, rows, width), lambda b, i: (b, jnp.maximum(i * nb - 1, 0), 0)),
            pl.BlockSpec((1, rows, width), lambda b, i: (b, jnp.minimum((i + 1) * nb, last), 0))]


def _inproj_kernel(*refs, mode):
    (x_ref, xp_ref, xn_ref, sh_ref, sc_ref, g_ref, w_ref, qg_ref, kg_ref, bd_ref,
     cos_ref, sa_ref, sb_ref, cw_ref) = refs[:14]
    if mode == "hyena":
        cb_ref, q_ref, k_ref, vt_ref, u_ref, x0_ref = refs[14:]
    else:
        q_ref, k_ref, vt_ref, oc_ref = refs[14:]
    i = pl.program_id(1)
    nt = pl.num_programs(1)
    tm = x_ref.shape[1]
    sub = min(PROJ_SUB_ROWS, tm)
    gain, shift, scale = g_ref[...], sh_ref[0], sc_ref[0]

    xh = jnp.concatenate([xp_ref[0], xn_ref[0]], axis=0)
    zh = jnp.dot(_modulated_norm(xh, gain, shift, scale).astype(BF16), w_ref[:, QKV_W:], preferred_element_type=F32)
    before = jnp.where(i > 0, zh[:SUBLANES], 0.0)
    after = jnp.where(i < nt - 1, zh[SUBLANES:], 0.0)
    cw = cw_ref[...]

    def conv_stage(ext, rows):
        n = sub + 2 * SUBLANES
        mid = slice(SUBLANES, sub + SUBLANES)
        if mode == "hyena":
            c = (pltpu.roll(ext, 1, 0)[mid] * cw[0:1] + ext[mid] * cw[1:2] + pltpu.roll(ext, n - 1, 0)[mid] * cw[2:3]
                 + cb_ref[...])
            x0_ref[0, rows] = c[:, :CONV_W]
            u_ref[0, rows] = c[:, 2 * CONV_W:] * c[:, CONV_W:2 * CONV_W]
        else:
            pr = ext[:, CONV_W:2 * CONV_W] * ext[:, 2 * CONV_W:]
            conv = pltpu.roll(pr, 1, 0)[mid] * cw[0:1] + pr[mid] * cw[1:2] + pltpu.roll(pr, n - 1, 0)[mid] * cw[2:3]
            oc_ref[0, rows] = (ext[mid, :CONV_W] * conv).astype(oc_ref.dtype)

    def head_norm_rope(t, hgain, bd, cos, sa, sb):
        ssq = jnp.dot((t * t).astype(BF16), bd, preferred_element_type=F32)
        t = t * lax.rsqrt(ssq * (1.0 / HEAD_DIM) + RMS_EPS) * hgain
        outs = []
        for j in range(t.shape[1] // LANES):
            tj = t[:, j * LANES:(j + 1) * LANES]
            outs.append(tj * cos + pltpu.roll(tj, LANES - 16, 1) * sa + pltpu.roll(tj, 16, 1) * sb)
        return outs

    held = None
    for r0 in range(0, tm, sub):
        rows = slice(r0, r0 + sub)
        h = _modulated_norm(x_ref[0, rows], gain, shift, scale)
        hb = h.astype(BF16)
        z = jnp.dot(hb, w_ref[:, :QKV_W], preferred_element_type=F32)
        zc = jnp.dot(hb, w_ref[:, QKV_W:], preferred_element_type=F32)
        if held is not None:
            conv_stage(jnp.concatenate([before, held[0], zc[:SUBLANES]], axis=0), held[1])
            before = held[0][-SUBLANES:]
        held = (zc, rows)
        tabs = (cos_ref[rows], sa_ref[rows], sb_ref[rows])
        qs = head_norm_rope(z[:, :Q_W], qg_ref[...], bd_ref[...], *tabs)
        ks = head_norm_rope(z[:, Q_W:Q_W + KV_W], kg_ref[...], bd_ref[:KV_W, :KV_W], *tabs)
        for j, qj in enumerate(qs):
            qj = (qj * Q_SCALE).astype(BF16)
            q_ref[0, 2 * j, rows] = qj[:, :HEAD_DIM]
            q_ref[0, 2 * j + 1, rows] = qj[:, HEAD_DIM:]
        for j, kj in enumerate(ks):
            kj = kj.astype(BF16)
            k_ref[0, 2 * j, rows] = kj[:, :HEAD_DIM]
            k_ref[0, 2 * j + 1, rows] = kj[:, HEAD_DIM:]
        vt = z[:, Q_W + KV_W:QKV_W].T
        ones_row = (lax.broadcasted_iota(jnp.int32, (LANES - HEAD_DIM, sub), 0) == 0).astype(F32)
        for hh in range(N_KV_HEADS):
            vt_ref[0, hh, :, rows] = jnp.concatenate([vt[hh * HEAD_DIM:(hh + 1) * HEAD_DIM], ones_row],
                                                     axis=0).astype(BF16)

    conv_stage(jnp.concatenate([before, held[0], after], axis=0), held[1])


def _inproj(x, sh, sc, gain, w_bf, qg, kg, bd, cos, sa, sb, conv_w, conv_b=None):
    bx, t, d = x.shape
    tm = min(512, t)
    mode = "hyena" if conv_b is not None else "short"
    vec = pl.BlockSpec((1, 1, d), lambda b, i: (b, 0, 0))
    tab = pl.BlockSpec((tm, LANES), lambda b, i: (i, 0))
    row = lambda w: pl.BlockSpec((1, tm, w), lambda b, i: (b, i, 0))
    in_specs = _halo_specs(tm, t, d) + [vec, vec, _full((1, d)), _full((d, MIX_IN_W)), _full((1, Q_W)),
                                        _full((1, KV_W)), _full((Q_W, Q_W)), tab, tab, tab, _full(conv_w.shape)]
    args = [x, x, x, sh, sc, gain, w_bf, qg, kg, bd, cos, sa, sb, conv_w]
    out_specs = [pl.BlockSpec((1, N_Q_HEADS, tm, HEAD_DIM), lambda b, i: (b, 0, i, 0)),
                 pl.BlockSpec((1, N_KV_HEADS, tm, HEAD_DIM), lambda b, i: (b, 0, i, 0)),
                 pl.BlockSpec((1, N_KV_HEADS, LANES, tm), lambda b, i: (b, 0, 0, i))]
    out_shape = [jax.ShapeDtypeStruct((bx, N_Q_HEADS, t, HEAD_DIM), BF16),
                 jax.ShapeDtypeStruct((bx, N_KV_HEADS, t, HEAD_DIM), BF16),
                 jax.ShapeDtypeStruct((bx, N_KV_HEADS, LANES, t), BF16)]
    if mode == "hyena":
        in_specs.append(_full((1, 3 * CONV_W)))
        args.append(conv_b.reshape(1, 3 * CONV_W))
        out_specs += [row(CONV_W), row(CONV_W)]
        out_shape += [jax.ShapeDtypeStruct((bx, t, CONV_W), F32)] * 2
    else:
        out_specs.append(row(CONV_W))
        out_shape.append(jax.ShapeDtypeStruct((bx, t, CONV_W), BF16))
    return pl.pallas_call(
        functools.partial(_inproj_kernel, mode=mode),
        grid=(bx, t // tm),
        in_specs=in_specs,
        out_specs=out_specs,
        out_shape=out_shape,
        compiler_params=_params(("arbitrary", "arbitrary")),
        name="in_proj",
    )(*args)


def _score_bound(q_gain, k_gain):
    b = HEAD_DIM * Q_SCALE * BOUND_MARGIN * jnp.max(jnp.abs(q_gain)) * jnp.max(jnp.abs(k_gain))
    return b.reshape(1, 1).astype(F32)


def _scores_t(ks, q):
    return lax.dot_general(ks, q, (((1,), (1,)), ((), ())), preferred_element_type=F32)


def _sink_column(sink_ref, hh, tq, m, acc):
    col = lax.broadcasted_iota(jnp.int32, m.shape, 1)
    sink = jnp.where(col < tq, sink_ref[hh, 0], sink_ref[hh, 1]) * LOG2E
    m_new = jnp.maximum(m, sink)
    den_row = lax.broadcasted_iota(jnp.int32, acc.shape, 0) == HEAD_DIM
    return acc * jnp.exp2(m - m_new) + jnp.where(den_row, jnp.exp2(sink - m_new), 0.0)


def _attn_finish(acc, tq):
    o = (acc / acc[HEAD_DIM:HEAD_DIM + 1]).T
    return jnp.concatenate([o[:tq, :HEAD_DIM], o[tq:, :HEAD_DIM]], axis=1)


EXP_CHUNK_ELEMS = 32 * SUBLANES * LANES


def _score_stage(ks, q, s_ref, bias=None):
    s = _scores_t(ks, q)
    if bias is not None:
        s = s + bias
    s_ref[0:s.shape[0]] = s
    return jnp.max(s, axis=0, keepdims=True)


def _softmax_stage(s_ref, p_ref, n, vt, mx, m, acc_ref):
    m_new = jnp.maximum(m, mx)
    rows = EXP_CHUNK_ELEMS // m.shape[1]
    for c in range(0, n, rows):
        p_ref[c:c + rows] = jnp.exp2(s_ref[c:c + rows] - m_new).astype(BF16)
    acc_ref[...] = acc_ref[...] * jnp.exp2(m - m_new) + jnp.dot(vt, p_ref[0:n], preferred_element_type=F32)
    return m_new


BOUNDED_TILES = 8
BOUND_MARGIN = 1.01
SAFE_SHIFT = 40.0


def _flash_exact(q, k_tile, vt_tile, ctx, sa_ref, sb_ref, p_ref, acc_ref, *, tk, n_main):
    pa_ref, pb_ref = p_ref.at[0:tk], p_ref.at[tk:2 * tk]
    acc_ref[...] = jnp.zeros_like(acc_ref)
    m = jnp.full((1, q.shape[0]), NEG_INF, F32)
    mx_a = _score_stage(k_tile(0), q, sa_ref)
    if n_main > 1:
        def body(i, carry):
            m, mx_a = carry
            mx_b = _score_stage(k_tile(2 * i + 1), q, sb_ref)
            m = _softmax_stage(sa_ref, pa_ref, tk, vt_tile(2 * i), mx_a, m, acc_ref)
            mx_a = _score_stage(k_tile(2 * i + 2), q, sa_ref)
            m = _softmax_stage(sb_ref, pb_ref, tk, vt_tile(2 * i + 1), mx_b, m, acc_ref)
            return m, mx_a

        m, mx_a = lax.fori_loop(0, n_main // 2 - 1, body, (m, mx_a))
        mx_b = _score_stage(k_tile(n_main - 1), q, sb_ref)
        m = _softmax_stage(sa_ref, pa_ref, tk, vt_tile(n_main - 2), mx_a, m, acc_ref)
        last = (sb_ref, pb_ref, mx_b)
        spare = (sa_ref, pa_ref)
    else:
        last = (sa_ref, pa_ref, mx_a)
        spare = (sb_ref, pb_ref)
    if ctx is not None:
        mx_c = _score_stage(ctx[0], q, spare[0])
    m = _softmax_stage(last[0], last[1], tk, vt_tile(n_main - 1), last[2], m, acc_ref)
    if ctx is not None:
        m = _softmax_stage(spare[0], spare[1], ctx[0].shape[0], ctx[1], mx_c, m, acc_ref)
    return m


def _flash_bounded(q, k_ref, vt_ref, ctx, bound, p_ref, acc_ref, *, tk, n_main):
    step = BOUNDED_TILES * tk if n_main % BOUNDED_TILES == 0 else tk

    def group(j):
        off = pl.multiple_of(j * step, step)
        for c in range(0, step, tk):
            s_t = _scores_t(k_ref[0, 0, pl.ds(off + c, tk), :], q)
            p_ref[c:c + tk] = jnp.exp2(s_t - bound).astype(BF16)
        return jnp.dot(vt_ref[0, 0, :, pl.ds(off, step)], p_ref[0:step], preferred_element_type=F32)

    first = group(0)
    if ctx is not None:
        pc_ref = p_ref.at[BOUNDED_TILES * tk:BOUNDED_TILES * tk + ctx[0].shape[0]]
        pc_ref[...] = jnp.exp2(_scores_t(ctx[0], q) - bound).astype(BF16)
        first = first + jnp.dot(ctx[1], pc_ref[...], preferred_element_type=F32)
    acc_ref[...] = first

    def body(j, carry):
        acc_ref[...] += group(j)
        return carry

    lax.fori_loop(1, n_main * tk // step, body, 0)


def _flash_kernel(*refs, tq, tk, n_main, has_ctx, has_sink):
    sa_ref, sb_ref, p_ref, acc_ref = refs[-4:]
    o_ref = refs[-5]
    refs = list(refs[:-5])
    sink_ref = refs.pop(0) if has_sink else None
    bound_ref = None if has_sink else refs.pop(0)
    q_ref, k_ref, vt_ref = refs[:3]
    m_cols = GROUP * tq
    q = q_ref[0].reshape(m_cols, HEAD_DIM)
    ctx = (refs[3][0, 0], refs[4][0, 0]) if has_ctx else None

    def k_tile(j):
        return k_ref[0, 0, pl.ds(pl.multiple_of(j * tk, tk), tk), :]

    def vt_tile(j):
        return vt_ref[0, 0, :, pl.ds(pl.multiple_of(j * tk, tk), tk)]

    exact = functools.partial(_flash_exact, q, k_tile, vt_tile, ctx, sa_ref, sb_ref, p_ref, acc_ref,
                              tk=tk, n_main=n_main)
    if has_sink:
        acc = _sink_column(sink_ref, pl.program_id(1), tq, exact(), acc_ref[...])
    else:
        bound = bound_ref[0, 0]
        safe = bound <= SAFE_SHIFT

        @pl.when(safe)
        def _():
            _flash_bounded(q, k_ref, vt_ref, ctx, bound, p_ref, acc_ref, tk=tk, n_main=n_main)

        @pl.when(jnp.logical_not(safe))
        def _():
            exact()

        acc = acc_ref[...]
    o_ref[0] = _attn_finish(acc, tq).astype(o_ref.dtype)


def _kv_specs(n):
    return [pl.BlockSpec((1, 1, n, HEAD_DIM), lambda b, h, i: (b, h, 0, 0)),
            pl.BlockSpec((1, 1, LANES, n), lambda b, h, i: (b, h, 0, 0))]


def _dense_attention(q, k, vt, kc=None, vct=None, *, bound=None, sink=None):
    bx, _, t, _ = q.shape
    s = k.shape[2]
    tq = min(512, t)
    tk = min(512, s)
    has_ctx = kc is not None
    has_sink = sink is not None
    in_specs = [pl.BlockSpec((1, GROUP, tq, HEAD_DIM), lambda b, h, i: (b, h, i, 0))] + _kv_specs(s)
    args = [q, k, vt]
    if has_ctx:
        in_specs += _kv_specs(kc.shape[2])
        args += [kc, vct]
    in_specs = [pl.BlockSpec(memory_space=pltpu.SMEM)] + in_specs
    args = [sink if has_sink else bound] + args
    n_main = s // tk
    assert n_main == 1 or n_main % 2 == 0
    n_ctx = kc.shape[2] if has_ctx else 0
    assert n_ctx <= tk
    m_cols = GROUP * tq
    return pl.pallas_call(
        functools.partial(_flash_kernel, tq=tq, tk=tk, n_main=n_main, has_ctx=has_ctx, has_sink=has_sink),
        grid=(bx, N_KV_HEADS, t // tq),
        in_specs=in_specs,
        out_specs=pl.BlockSpec((1, tq, LANES), lambda b, h, i: (b, i, h)),
        out_shape=jax.ShapeDtypeStruct((bx, t, Q_W), BF16),
        scratch_shapes=[pltpu.VMEM((tk, m_cols), F32), pltpu.VMEM((tk, m_cols), F32),
                        pltpu.VMEM((BOUNDED_TILES * tk + n_ctx, m_cols), BF16), pltpu.VMEM((LANES, m_cols), F32)],
        compiler_params=_params(("arbitrary", "arbitrary", "arbitrary")),
        name="dense_attention",
    )(*args)


BAND_SUB = 256


@functools.lru_cache(maxsize=None)
def _band_bias():
    span = BAND_SUB + 2 * WINDOW
    kr = np.arange(span)[None, :, None]
    qc = (np.arange(GROUP * BAND_SUB) % BAND_SUB)[None, None, :]
    rel = np.arange(3)[:, None, None]
    return np.where(np.abs(kr - qc - rel * WINDOW) <= WINDOW, 0.0, NEG_INF).astype(np.float32)


def _banded_kernel(sink_ref, bound_ref, q_ref, k_ref, vt_ref, kc_ref, vct_ref, bias_ref, o_ref,
                   sl_ref, sc_ref, pl_ref, pc_ref, acc_ref, *, n_sub, s_len):
    i = pl.program_id(2)
    hh = pl.program_id(1)
    sub = BAND_SUB
    m_cols = GROUP * sub
    span = sub + 2 * WINDOW
    n_ctx = kc_ref.shape[2]

    def window(u):
        q = q_ref[0, :, u * sub:(u + 1) * sub, :].reshape(m_cols, HEAD_DIM)
        q0 = (i * n_sub + u) * sub
        start = pl.multiple_of(jnp.clip(q0 - WINDOW, 0, s_len - span), WINDOW)
        return q, start, bias_ref[(q0 - start) // WINDOW]

    shift = jnp.maximum(bound_ref[0, 0], jnp.maximum(sink_ref[hh, 0], sink_ref[hh, 1]) * LOG2E)
    safe = shift <= SAFE_SHIFT

    @pl.when(safe)
    def _():
        col = lax.broadcasted_iota(jnp.int32, (1, m_cols), 1)
        sink = jnp.where(col < sub, sink_ref[hh, 0], sink_ref[hh, 1]) * LOG2E
        den_row = lax.broadcasted_iota(jnp.int32, (LANES, m_cols), 0) == HEAD_DIM
        sink_den = jnp.where(den_row, jnp.exp2(sink - shift), 0.0)
        for u in range(n_sub):
            q, start, bias = window(u)
            pl_u, pc_u = pl_ref.at[u], pc_ref.at[u]
            pl_u[...] = jnp.exp2(_scores_t(k_ref[0, 0, pl.ds(start, span), :], q) + bias - shift).astype(BF16)
            pc_u[...] = jnp.exp2(_scores_t(kc_ref[0, 0], q) - shift).astype(BF16)
            acc = (jnp.dot(vt_ref[0, 0, :, pl.ds(start, span)], pl_u[...], preferred_element_type=F32)
                   + jnp.dot(vct_ref[0, 0], pc_u[...], preferred_element_type=F32) + sink_den)
            o_ref[0, u * sub:(u + 1) * sub] = _attn_finish(acc, sub).astype(o_ref.dtype)

    @pl.when(jnp.logical_not(safe))
    def _():
        stats = []
        for u in range(n_sub):
            q, start, bias = window(u)
            mx_l = _score_stage(k_ref[0, 0, pl.ds(start, span), :], q, sl_ref.at[u], bias=bias)
            mx_c = _score_stage(kc_ref[0, 0], q, sc_ref.at[u])
            stats.append((start, mx_l, mx_c))
        for u in range(n_sub):
            start, mx_l, mx_c = stats[u]
            acc_u = acc_ref.at[u]
            acc_u[...] = jnp.zeros((LANES, m_cols), F32)
            m = jnp.full((1, m_cols), NEG_INF, F32)
            m = _softmax_stage(sl_ref.at[u], pl_ref.at[u], span, vt_ref[0, 0, :, pl.ds(start, span)], mx_l, m, acc_u)
            m = _softmax_stage(sc_ref.at[u], pc_ref.at[u], n_ctx, vct_ref[0, 0], mx_c, m, acc_u)
            acc = _sink_column(sink_ref, hh, sub, m, acc_u[...])
            o_ref[0, u * sub:(u + 1) * sub] = _attn_finish(acc, sub).astype(o_ref.dtype)


def _banded_attention(q, k, vt, kc, vct, sink, bound):
    bx, _, t, _ = q.shape
    n_ctx = kc.shape[2]
    n_sub = next(k for k in (8, 4, 2, 1) if t % (k * BAND_SUB) == 0)
    tq = n_sub * BAND_SUB
    m_cols = GROUP * BAND_SUB
    span = BAND_SUB + 2 * WINDOW
    bias = jnp.asarray(_band_bias())
    return pl.pallas_call(
        functools.partial(_banded_kernel, n_sub=n_sub, s_len=t),
        grid=(bx, N_KV_HEADS, t // tq),
        in_specs=[pl.BlockSpec(memory_space=pltpu.SMEM), pl.BlockSpec(memory_space=pltpu.SMEM),
                  pl.BlockSpec((1, GROUP, tq, HEAD_DIM), lambda b, h, i: (b, h, i, 0))]
                 + _kv_specs(t) + _kv_specs(n_ctx) + [_full(bias.shape)],
        out_specs=pl.BlockSpec((1, tq, LANES), lambda b, h, i: (b, i, h)),
        out_shape=jax.ShapeDtypeStruct((bx, t, Q_W), BF16),
        scratch_shapes=[pltpu.VMEM((n_sub, span, m_cols), F32), pltpu.VMEM((n_sub, n_ctx, m_cols), F32),
                        pltpu.VMEM((n_sub, span, m_cols), BF16), pltpu.VMEM((n_sub, n_ctx, m_cols), BF16),
                        pltpu.VMEM((n_sub, LANES, m_cols), F32)],
        compiler_params=_params(("arbitrary", "arbitrary", "arbitrary")),
        name="banded_attention",
    )(sink, bound, q, k, vt, kc, vct, bias)


@functools.lru_cache(maxsize=None)
def _filter_features(n):
    j = np.arange(2 * n)
    d = np.where(j <= n, j, 2 * n - j)
    d = np.where(j == n, 0, d)
    bands = (FILT_EMB - 1) // 2
    t01 = np.linspace(0.0, 1.0, n)[d]
    w = 2.0 * np.pi * d.astype(np.float64) / n
    f = np.linspace(1e-4, bands - 1, bands)[None, :]
    feats = np.zeros((2 * n, LANES), np.float64)
    feats[:, 0] = t01
    feats[:, 1:1 + bands] = np.cos(f * w[:, None])
    feats[:, 1 + bands:FILT_EMB] = -np.sin(f * w[:, None])
    feats[:, 64] = t01
    feats[:, 65] = (j < n)
    feats[:, 66] = (j != n)
    return feats.astype(np.float32)


def _filter_kernel(f_ref, w1_ref, b1_ref, w2_ref, b2_ref, w3_ref, b3_ref, w4_ref, fr_ref, dl_ref, k_ref, s_ref):
    f = f_ref[...]
    fr = fr_ref[...]
    mm = lambda a, b: jnp.dot(a, b, preferred_element_type=F32, precision=HI)
    h = jnp.sin(fr * (mm(w1_ref[...], f) + b1_ref[...]))
    h = jnp.sin(fr * (mm(w2_ref[...], h) + b2_ref[...]))
    h = jnp.sin(fr * (mm(w3_ref[...], h) + b3_ref[...]))
    hf = mm(w4_ref[...], h)
    win = jnp.exp(-dl_ref[...] * f[64:65]) + DECAY_SHIFT
    k = (jnp.where(f[65:66] > 0.5, hf[:CONV_W], hf[CONV_W:]) * win * f[66:67]).T
    k_ref[...] = k

    @pl.when(pl.program_id(0) == 0)
    def _():
        s_ref[...] = jnp.zeros_like(s_ref)

    s_ref[...] += jnp.sum(jnp.abs(k), axis=0, keepdims=True)


def _implicit_filter(n, w1, b1, w2, b2, w3, b3, w4, freq):
    feats_t = jnp.asarray(np.ascontiguousarray(_filter_features(n).T))
    col = lambda a: a.reshape(-1, 1)
    w1t = jnp.pad(w1.T, ((0, 0), (0, LANES - FILT_EMB)))
    deltas = np.abs(np.linspace(math.log(DECAY_TARGET) / SLOW_DECAY_PCT, math.log(DECAY_TARGET) / FAST_DECAY_PCT,
                                CONV_W)).astype(np.float32).reshape(CONV_W, 1)
    tr = min(1024, 2 * n)
    sq = _full((FILT_WIDTH, FILT_WIDTH))
    vec = _full((FILT_WIDTH, 1))
    return pl.pallas_call(
        _filter_kernel,
        grid=(2 * n // tr,),
        in_specs=[pl.BlockSpec((LANES, tr), lambda i: (0, i)), _full((FILT_WIDTH, LANES)), vec, sq, vec, sq, vec,
                  _full((2 * CONV_W, FILT_WIDTH)), vec, _full((CONV_W, 1))],
        out_specs=[pl.BlockSpec((tr, CONV_W), lambda i: (i, 0)), _full((1, CONV_W))],
        out_shape=[jax.ShapeDtypeStruct((2 * n, CONV_W), F32), jax.ShapeDtypeStruct((1, CONV_W), F32)],
        compiler_params=_params(("arbitrary",)),
        name="hyena_filter",
    )(feats_t, w1t, col(b1), w2.T, col(b2), w3.T, col(b3), w4.T, col(freq), jnp.asarray(deltas))


def _twiddle(idx, mod):
    ang = 2.0 * np.pi * (idx % mod) / mod
    return np.cos(ang), -np.sin(ang)


def _real_form(mr, mi):
    return np.concatenate([np.concatenate([mr, -mi], -1), np.concatenate([mi, mr], -1)], -2)


@functools.lru_cache(maxsize=None)
def _dft_tables(n2):
    n1 = DFT_N1
    n = n1 * n2
    h = n2 // 2
    a2 = np.arange(n2)
    fr, fi = _twiddle(np.outer(a2, a2), n2)
    m_data = _real_form(fr[:, :h], fi[:, :h])
    m_filt = np.concatenate([fr, fi], 0)
    k2 = a2[:, None, None]
    k1 = np.arange(n1)[None, :, None]
    c1 = np.arange(n1)[None, None, :]
    g = _real_form(*_twiddle(c1 * (n2 * k1 + k2), n))
    a1 = np.arange(n1)
    f1 = _real_form(*_twiddle(np.outer(a1, a1), n1))
    t2 = a1[:, None, None]
    t1 = np.arange(h)[None, :, None]
    j1 = a2[None, None, :]
    hh = _real_form(*_twiddle(j1 * (n1 * t1 + t2), n))
    cast = lambda m: np.asarray(m, dtype=BF16)
    return cast(m_data), cast(m_filt), cast(g), cast(f1), cast(hh)


DFT_ROW_CHUNK = SUBLANES


def _dft_rows_kernel(m_ref, *refs):
    o_ref = refs[-1]
    cols = [jnp.concatenate([r[0, :, t, :] for r in refs[:-1]], axis=0) for t in range(DFT_ROW_CHUNK)]
    rhs = jnp.concatenate(cols, axis=1).astype(BF16)
    out = jnp.dot(m_ref[...], rhs, preferred_element_type=F32)
    half = out.shape[0] // 2
    o_ref[0, 0] = out[:half].astype(o_ref.dtype)
    o_ref[0, 1] = out[half:].astype(o_ref.dtype)


def _dft_stage1(mat, views, pairs, rows, n2, c):
    nin = len(views)
    lc = DFT_ROW_CHUNK * c
    in_specs = [_full(mat.shape)] + [
        pl.BlockSpec((1, rows, DFT_ROW_CHUNK, c), (lambda p, j, a=a: (nin * p + a, 0, j, 0))) for a in range(nin)]
    return pl.pallas_call(
        _dft_rows_kernel,
        grid=(pairs, DFT_N1 // DFT_ROW_CHUNK),
        in_specs=in_specs,
        out_specs=pl.BlockSpec((1, 2, n2, lc), lambda p, j: (p, 0, 0, j)),
        out_shape=jax.ShapeDtypeStruct((pairs, 2, n2, DFT_N1 * c), BF16),
        compiler_params=_params(("arbitrary", "arbitrary")),
        name="dft_stage1",
    )(mat, *views)


def _spectrum_kernel(a_ref, g_ref, sc_ref, o_ref, *, kb):
    for j in range(kb):
        rhs = jnp.concatenate([a_ref[0, 0, j], a_ref[0, 1, j]], axis=0)
        x = jnp.dot(g_ref[j], rhs, preferred_element_type=F32)
        o_ref[0, j] = (x[:DFT_N1] * sc_ref[...]).astype(o_ref.dtype)
        o_ref[1, j] = (x[DFT_N1:] * sc_ref[...]).astype(o_ref.dtype)


def _filter_spectrum(a, g, scale, n2, kb):
    c = a.shape[-1]
    return pl.pallas_call(
        functools.partial(_spectrum_kernel, kb=kb),
        grid=(n2 // kb,),
        in_specs=[pl.BlockSpec((1, 2, kb, DFT_N1, c), lambda k: (0, 0, k, 0, 0)),
                  pl.BlockSpec((kb, 2 * DFT_N1, 2 * DFT_N1), lambda k: (k, 0, 0)), _full((1, c))],
        out_specs=pl.BlockSpec((2, kb, DFT_N1, c), lambda k: (0, k, 0, 0)),
        out_shape=jax.ShapeDtypeStruct((2, n2, DFT_N1, c), BF16),
        compiler_params=_params(("arbitrary",)),
        name="filter_spectrum",
    )(a, g, scale)


def _dft_mid_kernel(a_ref, g_ref, kh_ref, f_ref, o_ref, *, kb):
    for j in range(kb):
        rhs = jnp.concatenate([a_ref[0, 0, j], a_ref[0, 1, j]], axis=0)
        x = jnp.dot(g_ref[j], rhs, preferred_element_type=F32)
        xr, xi = x[:DFT_N1], x[DFT_N1:]
        kr, ki = kh_ref[0, j].astype(F32), kh_ref[1, j].astype(F32)
        yr = xr * kr - xi * ki
        yi = xr * ki + xi * kr
        v = jnp.concatenate([yr, -yi], axis=0).astype(BF16)
        b = jnp.dot(f_ref[...], v, preferred_element_type=F32)
        o_ref[0, 0, :, j, :] = b[:DFT_N1]
        o_ref[0, 1, :, j, :] = b[DFT_N1:]


def _dft_mid(a, g, khat, f1, pairs, n2, kb):
    c = a.shape[-1]
    blk = pl.BlockSpec((1, 2, kb, DFT_N1, c), lambda k, p: (p, 0, k, 0, 0))
    return pl.pallas_call(
        functools.partial(_dft_mid_kernel, kb=kb),
        grid=(n2 // kb, pairs),
        in_specs=[blk, pl.BlockSpec((kb, 2 * DFT_N1, 2 * DFT_N1), lambda k, p: (k, 0, 0)),
                  pl.BlockSpec((2, kb, DFT_N1, c), lambda k, p: (0, k, 0, 0)), _full(f1.shape)],
        out_specs=pl.BlockSpec((1, 2, DFT_N1, kb, c), lambda k, p: (p, 0, 0, k, 0)),
        out_shape=jax.ShapeDtypeStruct((pairs, 2, DFT_N1, n2, c), F32),
        compiler_params=_params(("arbitrary", "arbitrary")),
        name="dft_mid",
    )(a, g, khat, f1)


def _dft_last_kernel(b_ref, h_ref, u_ref, x0_ref, bd_ref, o_ref, *, c):
    half = h_ref.shape[1] // 2
    bd = bd_ref[...]
    for t in range(DFT_ROW_CHUNK):
        rhs = jnp.concatenate([b_ref[0, 0, t], b_ref[0, 1, t]], axis=0).astype(BF16)
        v = jnp.dot(h_ref[t], rhs, preferred_element_type=F32)
        o_ref[0, :, t, :] = (v[:half] + u_ref[0, :, t, :] * bd) * x0_ref[0, :, t, :]
        o_ref[1, :, t, :] = (-v[half:] + u_ref[1, :, t, :] * bd) * x0_ref[1, :, t, :]


def _dft_last(bm, hh, u_view, x0_view, bias_d, pairs, n2, c):
    half = n2 // 2
    tc = DFT_ROW_CHUNK
    io = pl.BlockSpec((2, half, tc, c), lambda p, j: (p, 0, j, 0))
    return pl.pallas_call(
        functools.partial(_dft_last_kernel, c=c),
        grid=(pairs, DFT_N1 // tc),
        in_specs=[pl.BlockSpec((1, 2, tc, n2, c), lambda p, j: (p, 0, j, 0, 0)),
                  pl.BlockSpec((tc, n2, 2 * n2), lambda p, j: (j, 0, 0)), io, io, _full((1, c))],
        out_specs=io,
        out_shape=jax.ShapeDtypeStruct((2 * pairs, half, DFT_N1, c), F32),
        compiler_params=_params(("arbitrary", "arbitrary")),
        name="dft_last",
    )(bm, hh, u_view, x0_view, bias_d)


def _long_conv_mixer(u, x0, kfilt, ksum, bias_d):
    b, n, c = u.shape
    n2 = 2 * n // DFT_N1
    half = n2 // 2
    pairs = b // 2
    m_data, m_filt, g, f1, hh = (jnp.asarray(t) for t in _dft_tables(n2))
    kb = min(8, n2)
    scale = 1.0 / (ksum * float(DFT_N1 * n2))
    ka = _dft_stage1(m_filt, [kfilt.reshape(1, n2, DFT_N1, c)], 1, n2, n2, c)
    khat = _filter_spectrum(ka.reshape(1, 2, n2, DFT_N1, c), g, scale, n2, kb)
    u_view = u.reshape(b, half, DFT_N1, c)
    a = _dft_stage1(m_data, [u_view, u_view], pairs, half, n2, c)
    bm = _dft_mid(a.reshape(pairs, 2, n2, DFT_N1, c), g, khat, f1, pairs, n2, kb)
    out = _dft_last(bm, hh, u_view, x0.reshape(b, half, DFT_N1, c),
                    bias_d.reshape(1, c), pairs, n2, c)
    return out.reshape(b, n, c)


@functools.lru_cache(maxsize=None)
def _small_dft_tables(n):
    big = 2 * n
    a = np.arange(big)
    fr, fi = _twiddle(np.outer(a, a), big)
    cast = lambda m: np.asarray(m, dtype=BF16)
    return (cast(np.concatenate([fr, fi], 0)),
            cast(_real_form(fr[:, :n], fi[:, :n])),
            cast(_real_form(fr[:n], fi[:n])))


def _small_conv_kernel(k_ref, ks_ref, u_ref, x0_ref, bd_ref, mf_ref, md_ref, mi_ref, o_ref, *, n):
    big = 2 * n
    kh = jnp.dot(mf_ref[...], k_ref[...].astype(BF16), preferred_element_type=F32) * (1.0 / (ks_ref[...] * big))
    kr, ki = kh[:big], kh[big:]
    rhs = jnp.concatenate([u_ref[0], u_ref[1]], axis=0).astype(BF16)
    x = jnp.dot(md_ref[...], rhs, preferred_element_type=F32)
    xr, xi = x[:big], x[big:]
    v = jnp.concatenate([xr * kr - xi * ki, -(xr * ki + xi * kr)], axis=0).astype(BF16)
    y = jnp.dot(mi_ref[...], v, preferred_element_type=F32)
    bd = bd_ref[...]
    o_ref[0] = ((y[:n] + u_ref[0] * bd) * x0_ref[0]).astype(o_ref.dtype)
    o_ref[1] = ((-y[n:] + u_ref[1] * bd) * x0_ref[1]).astype(o_ref.dtype)


def _small_conv_mixer(u, x0, kfilt, ksum, bias_d):
    b, n, c = u.shape
    mf, md, mi = (jnp.asarray(t) for t in _small_dft_tables(n))
    io = pl.BlockSpec((2, n, c), lambda p: (p, 0, 0))
    return pl.pallas_call(
        functools.partial(_small_conv_kernel, n=n),
        grid=(b // 2,),
        in_specs=[_full((2 * n, c)), _full((1, c)), io, io, _full((1, c)),
                  _full(mf.shape), _full(md.shape), _full(mi.shape)],
        out_specs=io,
        out_shape=jax.ShapeDtypeStruct((b, n, c), BF16),
        compiler_params=_params(("arbitrary",)),
        name="small_conv",
    )(kfilt, ksum, u, x0, bias_d.reshape(1, c), mf, md, mi)


MXU_DIM = 256
FFN_CHUNK_EDGES = (0, 6 * MXU_DIM, D_FF)


def _gelu_tanh(x):
    return 0.5 * x * (1.0 + jnp.tanh(math.sqrt(2.0 / math.pi) * (x + 0.044715 * (x * x * x))))


def _ffn_kernel(x_ref, xp_ref, xn_ref, oa_ref, oap_ref, oan_ref, oc_ref, ocp_ref, ocn_ref, wa_ref, wc_ref, g1_ref,
                sh_ref, sc_ref, gt_ref, g_ref, wu_ref, cw_ref, cb_ref, wd_ref, o_ref):
    i = pl.program_id(1)
    nt = pl.num_programs(1)
    tm = x_ref.shape[1]

    def halo(p_ref, n_ref):
        return jnp.concatenate([p_ref[0][-SUBLANES:], n_ref[0][:SUBLANES]], axis=0)

    def mixer_out(oa, oc):
        return (jnp.dot(oa.astype(BF16), wa_ref[...], preferred_element_type=F32)
                + jnp.dot(oc.astype(BF16), wc_ref[...], preferred_element_type=F32))

    g1 = g1_ref[0]
    xm = x_ref[0] + g1 * mixer_out(oa_ref[0], oc_ref[0])
    xh = halo(xp_ref, xn_ref) + g1 * mixer_out(halo(oap_ref, oan_ref), halo(ocp_ref, ocn_ref))
    ext = jnp.concatenate([xh[:SUBLANES], xm, xh[SUBLANES:]], axis=0)
    r = tm + 2 * SUBLANES
    h = _modulated_norm(ext, g_ref[...], sh_ref[0], sc_ref[0])
    row = lax.broadcasted_iota(jnp.int32, (r, 1), 0)
    inside = jnp.logical_and(jnp.logical_or(i > 0, row >= SUBLANES), jnp.logical_or(i < nt - 1, row < tm + SUBLANES))
    h = jnp.where(inside, h, 0.0).astype(BF16)
    hm = h[SUBLANES:tm + SUBLANES]
    acc = jnp.zeros((tm, D_MODEL), F32)
    for c0, c1 in zip(FFN_CHUNK_EDGES[:-1], FFN_CHUNK_EDGES[1:]):
        a = jnp.dot(h, wu_ref[:, c0:c1], preferred_element_type=F32)
        v = jnp.dot(hm, wu_ref[:, D_FF + c0:D_FF + c1], preferred_element_type=F32)
        ap = pltpu.roll(a, 1, 0)[SUBLANES:tm + SUBLANES]
        an = pltpu.roll(a, r - 1, 0)[SUBLANES:tm + SUBLANES]
        cw = cw_ref[:, c0:c1]
        conv = ap * cw[0:1] + a[SUBLANES:tm + SUBLANES] * cw[1:2] + an * cw[2:3] + cb_ref[:, c0:c1]
        act = (_gelu_tanh(conv) * v).astype(BF16)
        acc = acc + jnp.dot(act, wd_ref[c0:c1, :], preferred_element_type=F32)
    o_ref[0] = xm + gt_ref[0] * acc


def _mixer_out_ffn(x, oa, oc, wo_bf, gate1, sh, sc, gate2, gain, wu_bf, conv_w, conv_b, wd_bf):
    bx, t, d = x.shape
    tm = min(512, t)
    vec = pl.BlockSpec((1, 1, d), lambda b, i: (b, 0, 0))
    once = pl.Buffered(1)
    return pl.pallas_call(
        _ffn_kernel,
        grid=(bx, t // tm),
        in_specs=_halo_specs(tm, t, d) + _halo_specs(tm, t, Q_W, oa.dtype) + _halo_specs(tm, t, CONV_W, oc.dtype) + [
            pl.BlockSpec((Q_W, d), lambda b, i: (0, 0)), pl.BlockSpec((CONV_W, d), lambda b, i: (1, 0)),
            vec, vec, vec, vec, _full((1, d)),
            pl.BlockSpec((d, 2 * D_FF), lambda b, i: (0, 0), pipeline_mode=once),
            _full((3, D_FF)), _full((1, D_FF)),
            pl.BlockSpec((D_FF, d), lambda b, i: (0, 0), pipeline_mode=once)],
        out_specs=pl.BlockSpec((1, tm, d), lambda b, i: (b, i, 0)),
        out_shape=jax.ShapeDtypeStruct((bx, t, d), F32),
        compiler_params=_params(("arbitrary", "arbitrary")),
        name="mixer_out_ffn",
    )(x, x, x, oa, oa, oa, oc, oc, oc, wo_bf, wo_bf, gate1, sh, sc, gate2, gain, wu_bf, conv_w,
      conv_b.reshape(1, D_FF), wd_bf)


@functools.lru_cache(maxsize=None)
def _rope_tables(t):
    pos = np.arange(t)
    n_freq = HEAD_DIM // 4
    inv = ROPE_THETA ** (-np.arange(n_freq, dtype=np.float64) / n_freq)
    ang_r = (pos // GRID_W)[:, None] * inv
    ang_c = (pos % GRID_W)[:, None] * inv
    zero = np.zeros_like(ang_r)
    cos = np.concatenate([np.cos(ang_r)] * 2 + [np.cos(ang_c)] * 2, 1)
    sa = np.concatenate([-np.sin(ang_r), zero, -np.sin(ang_c), zero], 1)
    sb = np.concatenate([zero, np.sin(ang_r), zero, np.sin(ang_c)], 1)
    tile = lambda m: np.tile(m, (1, LANES // HEAD_DIM)).astype(np.float32)
    return tile(cos), tile(sa), tile(sb)


@functools.lru_cache(maxsize=None)
def _identity_rope_tables(t):
    return np.ones((t, LANES), np.float32), np.zeros((t, LANES), np.float32), np.zeros((t, LANES), np.float32)


@functools.lru_cache(maxsize=None)
def _head_block_diag():
    hid = np.arange(Q_W) // HEAD_DIM
    return np.asarray(hid[:, None] == hid[None, :], dtype=BF16)


def kernel(x, c, ctx, c_ctx, ada_w, ada_b, norm_mix, norm_ffn, mix_w_in, mix_w_out, attn_q_norm, attn_k_norm,
           swa_sink, hy_conv_w, hy_conv_b, hy_w1, hy_b1, hy_w2, hy_b2, hy_w3, hy_b3, hy_w4, hy_freq, hy_bias_d,
           sc_conv_w, ffn_w_up, ffn_conv_w, ffn_conv_b, ffn_w_down):
    b, s, d = x.shape
    s_ctx = ctx.shape[1]
    depth = ada_w.shape[0]
    assert d == D_MODEL and b % 2 == 0 and b + 1 <= MOD_ROWS and s % 1024 == 0 and s_ctx % SUBLANES == 0

    cvec = jnp.concatenate([c, c_ctx[None, :], jnp.zeros((MOD_ROWS - b - 1, d), F32)], axis=0)
    mods = _mods(cvec, ada_w, ada_b)
    rope = [jnp.asarray(t) for t in _rope_tables(s)]
    rope_ctx = [jnp.asarray(t) for t in _identity_rope_tables(s_ctx)]
    bd = jnp.asarray(_head_block_diag())
    xc = ctx
    weights = _weights_bf16([mix_w_in, mix_w_out, ffn_w_up, ffn_w_down])

    for i in range(depth):
        last = i == depth - 1
        j = i // 2
        lat = [mods[i, :b, k * d:(k + 1) * d][:, None, :] for k in range(6)]
        cx = [jnp.broadcast_to(mods[i, b, k * d:(k + 1) * d][None, None, :], (b, 1, d)) for k in range(6)]
        w_in, w_out, w_up, w_down = (w[i] for w in weights)
        g_mix = norm_mix[i].reshape(1, d)
        g_ffn = norm_ffn[i].reshape(1, d)
        qg = jnp.tile(attn_q_norm[i], N_Q_HEADS).reshape(1, Q_W)
        kg = jnp.tile(attn_k_norm[i], N_KV_HEADS).reshape(1, KV_W)

        conv = (hy_conv_w[j], hy_conv_b[j]) if i % 2 == 0 else (sc_conv_w[j],)
        q, k, vt, *mix = _inproj(x, lat[0], lat[1], g_mix, w_in, qg, kg, bd, *rope, *conv)
        qc, kc, vct, *mixc = _inproj(xc, cx[0], cx[1], g_mix, w_in, qg, kg, bd, *rope_ctx, *conv)

        bound = _score_bound(attn_q_norm[i], attn_k_norm[i])
        if i % 2 == 0:
            fargs = (hy_w1[j], hy_b1[j], hy_w2[j], hy_b2[j], hy_w3[j], hy_b3[j], hy_w4[j], hy_freq[j])
            o_attn = _dense_attention(q, k, vt, kc, vct, bound=bound)
            kf, ks = _implicit_filter(s, *fargs)
            o_conv = _long_conv_mixer(*mix, kf, ks, hy_bias_d[j])
            if not last:
                oc_attn = _dense_attention(qc, kc, vct, bound=bound)
                kfc, ksc = _implicit_filter(s_ctx, *fargs)
                oc_conv = _small_conv_mixer(*mixc, kfc, ksc, hy_bias_d[j])
        else:
            sink = swa_sink[j].reshape(N_KV_HEADS, GROUP)
            o_attn = _banded_attention(q, k, vt, kc, vct, sink, bound)
            o_conv = mix[0]
            if not last:
                oc_attn = _dense_attention(qc, kc, vct, sink=sink)
                oc_conv = mixc[0]

        ffn = (g_ffn, w_up, ffn_conv_w[i], ffn_conv_b[i], w_down)
        x = _mixer_out_ffn(x, o_attn, o_conv, w_out, lat[2], lat[3], lat[4], lat[5], *ffn)
        if not last:
            xc = _mixer_out_ffn(xc, oc_attn, oc_conv, w_out, cx[2], cx[3], cx[4], cx[5], *ffn)
    return x
```

```python
import functools
import math

import numpy as np
import jax
import jax.numpy as jnp
from jax import lax
from jax.experimental import pallas as pl
from jax.experimental.pallas import tpu as pltpu

F32 = jnp.float32
BF16 = jnp.bfloat16
HI = lax.Precision.HIGHEST

D_MODEL = 1024
GRID_W = 64
HEAD_DIM = 64
N_Q_HEADS = 8
N_KV_HEADS = 4
GROUP = N_Q_HEADS // N_KV_HEADS
Q_W = N_Q_HEADS * HEAD_DIM
KV_W = N_KV_HEADS * HEAD_DIM
QKV_W = Q_W + 2 * KV_W
CONV_W = D_MODEL // 2
MIX_IN_W = QKV_W + 3 * CONV_W
WINDOW = 128
ROPE_THETA = 10000.0
FILT_EMB = 33
FILT_WIDTH = 64
DECAY_TARGET = 1e-2
FAST_DECAY_PCT = 0.3
SLOW_DECAY_PCT = 1.5
DECAY_SHIFT = 0.05
D_FF = 2816
NEG_INF = -1e30
RMS_EPS = 1e-6
LOG2E = 1.4426950408889634
Q_SCALE = HEAD_DIM ** -0.5 * LOG2E

LANES = 128
SUBLANES = 8
DFT_N1 = 128
VMEM_LIMIT_MB = 56
VT_ROWS = 64 + 2 * SUBLANES
MOD_ROWS = SUBLANES
PROJ_SUB_ROWS = 128


def _params(sem, vmem_mb=VMEM_LIMIT_MB):
    return pltpu.CompilerParams(dimension_semantics=sem, vmem_limit_bytes=vmem_mb * 1024 * 1024)


def _full(shape):
    nd = len(shape)
    return pl.BlockSpec(shape, lambda *_: (0,) * nd)


def _mods_kernel(c_ref, w_ref, b_ref, o_ref):
    c = c_ref[...]
    s = c / (1.0 + jnp.exp(-c))
    o_ref[0] = jnp.dot(s, w_ref[0], preferred_element_type=F32, precision=HI) + b_ref[0]


def _mods(cvec, ada_w, ada_b):
    depth, d, n6 = ada_w.shape
    tn = 1536
    return pl.pallas_call(
        _mods_kernel,
        grid=(depth, n6 // tn),
        in_specs=[_full((MOD_ROWS, d)),
                  pl.BlockSpec((1, d, tn), lambda l, j: (l, 0, j)),
                  pl.BlockSpec((1, 1, tn), lambda l, j: (l, 0, j))],
        out_specs=pl.BlockSpec((1, MOD_ROWS, tn), lambda l, j: (l, 0, j)),
        out_shape=jax.ShapeDtypeStruct((depth, MOD_ROWS, n6), F32),
        compiler_params=_params(("arbitrary", "arbitrary")),
        name="ada_mods",
    )(cvec, ada_w, ada_b.reshape(depth, 1, n6))


def _modulated_norm(x, gain, shift, scale):
    ms = jnp.mean(x * x, axis=-1, keepdims=True)
    return (x * lax.rsqrt(ms + RMS_EPS)) * gain * (1.0 + scale) + shift


WEIGHT_CAST_STEPS = 8


def _cast_kernel(*refs):
    n = len(refs) // 2
    for w_ref, o_ref in zip(refs[:n], refs[n:]):
        o_ref[...] = w_ref[0].astype(o_ref.dtype)


def _weights_bf16(stacked):
    depth = stacked[0].shape[0]
    in_specs, out_specs, out_shape, args = [], [], [], []
    for w in stacked:
        _, rows, cols = w.shape
        tr = rows // WEIGHT_CAST_STEPS
        for layer in range(depth):
            in_specs.append(pl.BlockSpec((1, tr, cols), lambda r, layer=layer: (layer, r, 0)))
            out_specs.append(pl.BlockSpec((tr, cols), lambda r: (r, 0)))
            out_shape.append(jax.ShapeDtypeStruct((rows, cols), BF16))
            args.append(w)
    outs = pl.pallas_call(
        _cast_kernel,
        grid=(WEIGHT_CAST_STEPS,),
        in_specs=in_specs,
        out_specs=out_specs,
        out_shape=out_shape,
        compiler_params=_params(("arbitrary",)),
        name="weights_bf16",
    )(*args)
    return [outs[k * depth:(k + 1) * depth] for k in range(len(stacked))]


def _halo_specs(tm, t, width, dtype=F32):
    rows = SUBLANES * 4 // jnp.dtype(dtype).itemsize
    nb = tm // rows
    last = t // rows - 1
    return [pl.BlockSpec((1, tm, width), lambda b, i: (b, i, 0)),
            pl.BlockSpec((1, rows, width), lambda b, i: (b, jnp.maximum(i * nb - 1, 0), 0)),
            pl.BlockSpec((1, rows, width), lambda b, i: (b, jnp.minimum((i + 1) * nb, last), 0))]


def _inproj_kernel(*refs, mode):
    (x_ref, xp_ref, xn_ref, sh_ref, sc_ref, g_ref, w_ref, qg_ref, kg_ref, bd_ref,
     cos_ref, sa_ref, sb_ref, cw_ref) = refs[:14]
    if mode == "hyena":
        cb_ref, q_ref, k_ref, vt_ref, u_ref, x0_ref = refs[14:]
    else:
        q_ref, k_ref, vt_ref, oc_ref = refs[14:]
    i = pl.program_id(1)
    nt = pl.num_programs(1)
    tm = x_ref.shape[1]
    sub = min(PROJ_SUB_ROWS, tm)
    gain, shift, scale = g_ref[...], sh_ref[0], sc_ref[0]

    xh = jnp.concatenate([xp_ref[0], xn_ref[0]], axis=0)
    zh = jnp.dot(_modulated_norm(xh, gain, shift, scale).astype(BF16), w_ref[:, QKV_W:], preferred_element_type=F32)
    before = jnp.where(i > 0, zh[:SUBLANES], 0.0)
    after = jnp.where(i < nt - 1, zh[SUBLANES:], 0.0)
    cw = cw_ref[...]

    def conv_stage(ext, rows):
        n = sub + 2 * SUBLANES
        mid = slice(SUBLANES, sub + SUBLANES)
        if mode == "hyena":
            c = (pltpu.roll(ext, 1, 0)[mid] * cw[0:1] + ext[mid] * cw[1:2] + pltpu.roll(ext, n - 1, 0)[mid] * cw[2:3]
                 + cb_ref[...])
            x0_ref[0, rows] = c[:, :CONV_W]
            u_ref[0, rows] = c[:, 2 * CONV_W:] * c[:, CONV_W:2 * CONV_W]
        else:
            pr = ext[:, CONV_W:2 * CONV_W] * ext[:, 2 * CONV_W:]
            conv = pltpu.roll(pr, 1, 0)[mid] * cw[0:1] + pr[mid] * cw[1:2] + pltpu.roll(pr, n - 1, 0)[mid] * cw[2:3]
            oc_ref[0, rows] = (ext[mid, :CONV_W] * conv).astype(oc_ref.dtype)

    def head_norm_rope(t, hgain, bd, cos, sa, sb):
        ssq = jnp.dot((t * t).astype(BF16), bd, preferred_element_type=F32)
        t = t * lax.rsqrt(ssq * (1.0 / HEAD_DIM) + RMS_EPS) * hgain
        outs = []
        for j in range(t.shape[1] // LANES):
            tj = t[:, j * LANES:(j + 1) * LANES]
            outs.append(tj * cos + pltpu.roll(tj, LANES - 16, 1) * sa + pltpu.roll(tj, 16, 1) * sb)
        return outs

    held = None
    for r0 in range(0, tm, sub):
        rows = slice(r0, r0 + sub)
        h = _modulated_norm(x_ref[0, rows], gain, shift, scale)
        hb = h.astype(BF16)
        z = jnp.dot(hb, w_ref[:, :QKV_W], preferred_element_type=F32)
        zc = jnp.dot(hb, w_ref[:, QKV_W:], preferred_element_type=F32)
        if held is not None:
            conv_stage(jnp.concatenate([before, held[0], zc[:SUBLANES]], axis=0), held[1])
            before = held[0][-SUBLANES:]
        held = (zc, rows)
        tabs = (cos_ref[rows], sa_ref[rows], sb_ref[rows])
        qs = head_norm_rope(z[:, :Q_W], qg_ref[...], bd_ref[...], *tabs)
        ks = head_norm_rope(z[:, Q_W:Q_W + KV_W], kg_ref[...], bd_ref[:KV_W, :KV_W], *tabs)
        for j, qj in enumerate(qs):
            qj = (qj * Q_SCALE).astype(BF16)
            q_ref[0, 2 * j, rows] = qj[:, :HEAD_DIM]
            q_ref[0, 2 * j + 1, rows] = qj[:, HEAD_DIM:]
        for j, kj in enumerate(ks):
            kj = kj.astype(BF16)
            k_ref[0, 2 * j, rows] = kj[:, :HEAD_DIM]
            k_ref[0, 2 * j + 1, rows] = kj[:, HEAD_DIM:]
        vt = z[:, Q_W + KV_W:QKV_W].T
        ones_row = (lax.broadcasted_iota(jnp.int32, (VT_ROWS - HEAD_DIM, sub), 0) == 0).astype(F32)
        for hh in range(N_KV_HEADS):
            vt_ref[0, hh, :, rows] = jnp.concatenate([vt[hh * HEAD_DIM:(hh + 1) * HEAD_DIM], ones_row],
                                                     axis=0).astype(BF16)

    conv_stage(jnp.concatenate([before, held[0], after], axis=0), held[1])


def _inproj(x, sh, sc, gain, w_bf, qg, kg, bd, cos, sa, sb, conv_w, conv_b=None):
    bx, t, d = x.shape
    tm = min(512, t)
    mode = "hyena" if conv_b is not None else "short"
    vec = pl.BlockSpec((1, 1, d), lambda b, i: (b, 0, 0))
    tab = pl.BlockSpec((tm, LANES), lambda b, i: (i, 0))
    row = lambda w: pl.BlockSpec((1, tm, w), lambda b, i: (b, i, 0))
    in_specs = _halo_specs(tm, t, d) + [vec, vec, _full((1, d)), _full((d, MIX_IN_W)), _full((1, Q_W)),
                                        _full((1, KV_W)), _full((Q_W, Q_W)), tab, tab, tab, _full(conv_w.shape)]
    args = [x, x, x, sh, sc, gain, w_bf, qg, kg, bd, cos, sa, sb, conv_w]
    out_specs = [pl.BlockSpec((1, N_Q_HEADS, tm, HEAD_DIM), lambda b, i: (b, 0, i, 0)),
                 pl.BlockSpec((1, N_KV_HEADS, tm, HEAD_DIM), lambda b, i: (b, 0, i, 0)),
                 pl.BlockSpec((1, N_KV_HEADS, VT_ROWS, tm), lambda b, i: (b, 0, 0, i))]
    out_shape = [jax.ShapeDtypeStruct((bx, N_Q_HEADS, t, HEAD_DIM), BF16),
                 jax.ShapeDtypeStruct((bx, N_KV_HEADS, t, HEAD_DIM), BF16),
                 jax.ShapeDtypeStruct((bx, N_KV_HEADS, VT_ROWS, t), BF16)]
    if mode == "hyena":
        in_specs.append(_full((1, 3 * CONV_W)))
        args.append(conv_b.reshape(1, 3 * CONV_W))
        out_specs += [row(CONV_W), row(CONV_W)]
        out_shape += [jax.ShapeDtypeStruct((bx, t, CONV_W), F32)] * 2
    else:
        out_specs.append(row(CONV_W))
        out_shape.append(jax.ShapeDtypeStruct((bx, t, CONV_W), BF16))
    return pl.pallas_call(
        functools.partial(_inproj_kernel, mode=mode),
        grid=(bx, t // tm),
        in_specs=in_specs,
        out_specs=out_specs,
        out_shape=out_shape,
        compiler_params=_params(("arbitrary", "arbitrary")),
        name="in_proj",
    )(*args)


def _score_bound(q_gain, k_gain):
    b = HEAD_DIM * Q_SCALE * BOUND_MARGIN * jnp.max(jnp.abs(q_gain)) * jnp.max(jnp.abs(k_gain))
    return b.reshape(1, 1).astype(F32)


def _scores_t(ks, q):
    return lax.dot_general(ks, q, (((1,), (1,)), ((), ())), preferred_element_type=F32)


def _sink_column(sink_ref, hh, tq, m, acc):
    col = lax.broadcasted_iota(jnp.int32, m.shape, 1)
    sink = jnp.where(col < tq, sink_ref[hh, 0], sink_ref[hh, 1]) * LOG2E
    m_new = jnp.maximum(m, sink)
    den_row = lax.broadcasted_iota(jnp.int32, acc.shape, 0) == HEAD_DIM
    return acc * jnp.exp2(m - m_new) + jnp.where(den_row, jnp.exp2(sink - m_new), 0.0)


def _attn_finish(acc, tq):
    o = acc / acc[HEAD_DIM:HEAD_DIM + 1]
    pad = jnp.zeros((LANES - VT_ROWS, acc.shape[1]), F32)
    o = jnp.concatenate([o, pad], axis=0).T
    return jnp.concatenate([o[:tq, :HEAD_DIM], o[tq:, :HEAD_DIM]], axis=1)


EXP_CHUNK_ELEMS = 32 * SUBLANES * LANES


def _score_stage(ks, q, s_ref, bias=None):
    s = _scores_t(ks, q)
    if bias is not None:
        s = s + bias
    s_ref[0:s.shape[0]] = s
    return jnp.max(s, axis=0, keepdims=True)


def _softmax_stage(s_ref, p_ref, n, vt, mx, m, acc_ref):
    m_new = jnp.maximum(m, mx)
    rows = EXP_CHUNK_ELEMS // m.shape[1]
    for c in range(0, n, rows):
        p_ref[c:c + rows] = jnp.exp2(s_ref[c:c + rows] - m_new).astype(BF16)
    acc_ref[...] = acc_ref[...] * jnp.exp2(m - m_new) + jnp.dot(vt, p_ref[0:n], preferred_element_type=F32)
    return m_new


BOUNDED_TILES = 8
BOUND_MARGIN = 1.01
SAFE_SHIFT = 40.0


def _flash_exact(q, k_tile, vt_tile, ctx, sa_ref, sb_ref, p_ref, acc_ref, *, tk, n_main):
    pa_ref, pb_ref = p_ref.at[0:tk], p_ref.at[tk:2 * tk]
    acc_ref[...] = jnp.zeros_like(acc_ref)
    m = jnp.full((1, q.shape[0]), NEG_INF, F32)
    mx_a = _score_stage(k_tile(0), q, sa_ref)
    if n_main > 1:
        def body(i, carry):
            m, mx_a = carry
            mx_b = _score_stage(k_tile(2 * i + 1), q, sb_ref)
            m = _softmax_stage(sa_ref, pa_ref, tk, vt_tile(2 * i), mx_a, m, acc_ref)
            mx_a = _score_stage(k_tile(2 * i + 2), q, sa_ref)
            m = _softmax_stage(sb_ref, pb_ref, tk, vt_tile(2 * i + 1), mx_b, m, acc_ref)
            return m, mx_a

        m, mx_a = lax.fori_loop(0, n_main // 2 - 1, body, (m, mx_a))
        mx_b = _score_stage(k_tile(n_main - 1), q, sb_ref)
        m = _softmax_stage(sa_ref, pa_ref, tk, vt_tile(n_main - 2), mx_a, m, acc_ref)
        last = (sb_ref, pb_ref, mx_b)
        spare = (sa_ref, pa_ref)
    else:
        last = (sa_ref, pa_ref, mx_a)
        spare = (sb_ref, pb_ref)
    if ctx is not None:
        mx_c = _score_stage(ctx[0], q, spare[0])
    m = _softmax_stage(last[0], last[1], tk, vt_tile(n_main - 1), last[2], m, acc_ref)
    if ctx is not None:
        m = _softmax_stage(spare[0], spare[1], ctx[0].shape[0], ctx[1], mx_c, m, acc_ref)
    return m


def _exp2_bf16(s):
    return jnp.exp2(s.astype(BF16))


def _flash_bounded(q, k_ref, vt_ref, ctx, p_ref, acc_ref, *, tk, n_main):
    step = BOUNDED_TILES * tk if n_main % BOUNDED_TILES == 0 else tk

    def group(j):
        off = pl.multiple_of(j * step, step)
        for c in range(0, step, tk):
            p_ref[c:c + tk] = _exp2_bf16(_scores_t(k_ref[0, 0, pl.ds(off + c, tk), :], q))
        return jnp.dot(vt_ref[0, 0, :, pl.ds(off, step)], p_ref[0:step], preferred_element_type=F32)

    first = group(0)
    if ctx is not None:
        pc_ref = p_ref.at[BOUNDED_TILES * tk:BOUNDED_TILES * tk + ctx[0].shape[0]]
        pc_ref[...] = _exp2_bf16(_scores_t(ctx[0], q))
        first = first + jnp.dot(ctx[1], pc_ref[...], preferred_element_type=F32)
    acc_ref[...] = first

    def body(j, carry):
        acc_ref[...] += group(j)
        return carry

    lax.fori_loop(1, n_main * tk // step, body, 0)


def _flash_kernel(*refs, tq, tk, n_main, has_ctx, has_sink):
    sa_ref, sb_ref, p_ref, acc_ref = refs[-4:]
    o_ref = refs[-5]
    refs = list(refs[:-5])
    sink_ref = refs.pop(0) if has_sink else None
    bound_ref = None if has_sink else refs.pop(0)
    q_ref, k_ref, vt_ref = refs[:3]
    m_cols = GROUP * tq
    q = q_ref[0].reshape(m_cols, HEAD_DIM)
    ctx = (refs[3][0, 0], refs[4][0, 0]) if has_ctx else None

    def k_tile(j):
        return k_ref[0, 0, pl.ds(pl.multiple_of(j * tk, tk), tk), :]

    def vt_tile(j):
        return vt_ref[0, 0, :, pl.ds(pl.multiple_of(j * tk, tk), tk)]

    exact = functools.partial(_flash_exact, q, k_tile, vt_tile, ctx, sa_ref, sb_ref, p_ref, acc_ref,
                              tk=tk, n_main=n_main)
    if has_sink:
        acc = _sink_column(sink_ref, pl.program_id(1), tq, exact(), acc_ref[...])
    else:
        bound = bound_ref[0, 0]
        safe = bound <= SAFE_SHIFT

        @pl.when(safe)
        def _():
            _flash_bounded(q, k_ref, vt_ref, ctx, p_ref, acc_ref, tk=tk, n_main=n_main)

        @pl.when(jnp.logical_not(safe))
        def _():
            exact()

        acc = acc_ref[...]
    o_ref[0] = _attn_finish(acc, tq).astype(o_ref.dtype)


def _kv_specs(n):
    return [pl.BlockSpec((1, 1, n, HEAD_DIM), lambda b, h, i: (b, h, 0, 0)),
            pl.BlockSpec((1, 1, VT_ROWS, n), lambda b, h, i: (b, h, 0, 0))]


def _dense_attention(q, k, vt, kc=None, vct=None, *, bound=None, sink=None):
    bx, _, t, _ = q.shape
    s = k.shape[2]
    tq = min(512, t)
    tk = min(512, s)
    has_ctx = kc is not None
    has_sink = sink is not None
    in_specs = [pl.BlockSpec((1, GROUP, tq, HEAD_DIM), lambda b, h, i: (b, h, i, 0))] + _kv_specs(s)
    args = [q, k, vt]
    if has_ctx:
        in_specs += _kv_specs(kc.shape[2])
        args += [kc, vct]
    in_specs = [pl.BlockSpec(memory_space=pltpu.SMEM)] + in_specs
    args = [sink if has_sink else bound] + args
    n_main = s // tk
    assert n_main == 1 or n_main % 2 == 0
    n_ctx = kc.shape[2] if has_ctx else 0
    assert n_ctx <= tk
    m_cols = GROUP * tq
    return pl.pallas_call(
        functools.partial(_flash_kernel, tq=tq, tk=tk, n_main=n_main, has_ctx=has_ctx, has_sink=has_sink),
        grid=(bx, N_KV_HEADS, t // tq),
        in_specs=in_specs,
        out_specs=pl.BlockSpec((1, tq, LANES), lambda b, h, i: (b, i, h)),
        out_shape=jax.ShapeDtypeStruct((bx, t, Q_W), BF16),
        scratch_shapes=[pltpu.VMEM((tk, m_cols), F32), pltpu.VMEM((tk, m_cols), F32),
                        pltpu.VMEM((BOUNDED_TILES * tk + n_ctx, m_cols), BF16), pltpu.VMEM((VT_ROWS, m_cols), F32)],
        compiler_params=_params(("arbitrary", "arbitrary", "arbitrary")),
        name="dense_attention",
    )(*args)


BAND_SUB = 256


@functools.lru_cache(maxsize=None)
def _band_bias():
    span = BAND_SUB + 2 * WINDOW
    kr = np.arange(span)[None, :, None]
    qc = (np.arange(GROUP * BAND_SUB) % BAND_SUB)[None, None, :]
    rel = np.arange(3)[:, None, None]
    return np.where(np.abs(kr - qc - rel * WINDOW) <= WINDOW, 0.0, NEG_INF).astype(np.float32)


def _banded_kernel(sink_ref, bound_ref, q_ref, k_ref, vt_ref, kc_ref, vct_ref, bias_ref, o_ref,
                   sl_ref, sc_ref, pl_ref, pc_ref, acc_ref, *, n_sub, s_len):
    i = pl.program_id(2)
    hh = pl.program_id(1)
    sub = BAND_SUB
    m_cols = GROUP * sub
    span = sub + 2 * WINDOW
    n_ctx = kc_ref.shape[2]

    def window(u):
        q = q_ref[0, :, u * sub:(u + 1) * sub, :].reshape(m_cols, HEAD_DIM)
        q0 = (i * n_sub + u) * sub
        start = pl.multiple_of(jnp.clip(q0 - WINDOW, 0, s_len - span), WINDOW)
        return q, start, bias_ref[(q0 - start) // WINDOW]

    shift = jnp.maximum(bound_ref[0, 0], jnp.maximum(sink_ref[hh, 0], sink_ref[hh, 1]) * LOG2E)
    safe = shift <= SAFE_SHIFT

    @pl.when(safe)
    def _():
        col = lax.broadcasted_iota(jnp.int32, (1, m_cols), 1)
        sink = jnp.where(col < sub, sink_ref[hh, 0], sink_ref[hh, 1]) * LOG2E
        den_row = lax.broadcasted_iota(jnp.int32, (VT_ROWS, m_cols), 0) == HEAD_DIM
        sink_den = jnp.where(den_row, jnp.exp2(sink), 0.0)
        for u in range(n_sub):
            q, start, bias = window(u)
            pl_u, pc_u = pl_ref.at[u], pc_ref.at[u]
            pl_u[...] = _exp2_bf16(_scores_t(k_ref[0, 0, pl.ds(start, span), :], q) + bias)
            pc_u[...] = _exp2_bf16(_scores_t(kc_ref[0, 0], q))
            acc = (jnp.dot(vt_ref[0, 0, :, pl.ds(start, span)], pl_u[...], preferred_element_type=F32)
                   + jnp.dot(vct_ref[0, 0], pc_u[...], preferred_element_type=F32) + sink_den)
            o_ref[0, u * sub:(u + 1) * sub] = _attn_finish(acc, sub).astype(o_ref.dtype)

    @pl.when(jnp.logical_not(safe))
    def _():
        stats = []
        for u in range(n_sub):
            q, start, bias = window(u)
            mx_l = _score_stage(k_ref[0, 0, pl.ds(start, span), :], q, sl_ref.at[u], bias=bias)
            mx_c = _score_stage(kc_ref[0, 0], q, sc_ref.at[u])
            stats.append((start, mx_l, mx_c))
        for u in range(n_sub):
            start, mx_l, mx_c = stats[u]
            acc_u = acc_ref.at[u]
            acc_u[...] = jnp.zeros((VT_ROWS, m_cols), F32)
            m = jnp.full((1, m_cols), NEG_INF, F32)
            m = _softmax_stage(sl_ref.at[u], pl_ref.at[u], span, vt_ref[0, 0, :, pl.ds(start, span)], mx_l, m, acc_u)
            m = _softmax_stage(sc_ref.at[u], pc_ref.at[u], n_ctx, vct_ref[0, 0], mx_c, m, acc_u)
            acc = _sink_column(sink_ref, hh, sub, m, acc_u[...])
            o_ref[0, u * sub:(u + 1) * sub] = _attn_finish(acc, sub).astype(o_ref.dtype)


def _banded_attention(q, k, vt, kc, vct, sink, bound):
    bx, _, t, _ = q.shape
    n_ctx = kc.shape[2]
    n_sub = next(k for k in (8, 4, 2, 1) if t % (k * BAND_SUB) == 0)
    tq = n_sub * BAND_SUB
    m_cols = GROUP * BAND_SUB
    span = BAND_SUB + 2 * WINDOW
    bias = jnp.asarray(_band_bias())
    return pl.pallas_call(
        functools.partial(_banded_kernel, n_sub=n_sub, s_len=t),
        grid=(bx, N_KV_HEADS, t // tq),
        in_specs=[pl.BlockSpec(memory_space=pltpu.SMEM), pl.BlockSpec(memory_space=pltpu.SMEM),
                  pl.BlockSpec((1, GROUP, tq, HEAD_DIM), lambda b, h, i: (b, h, i, 0))]
                 + _kv_specs(t) + _kv_specs(n_ctx) + [_full(bias.shape)],
        out_specs=pl.BlockSpec((1, tq, LANES), lambda b, h, i: (b, i, h)),
        out_shape=jax.ShapeDtypeStruct((bx, t, Q_W), BF16),
        scratch_shapes=[pltpu.VMEM((n_sub, span, m_cols), F32), pltpu.VMEM((n_sub, n_ctx, m_cols), F32),
                        pltpu.VMEM((n_sub, span, m_cols), BF16), pltpu.VMEM((n_sub, n_ctx, m_cols), BF16),
                        pltpu.VMEM((n_sub, VT_ROWS, m_cols), F32)],
        compiler_params=_params(("arbitrary", "arbitrary", "arbitrary")),
        name="banded_attention",
    )(sink, bound, q, k, vt, kc, vct, bias)


@functools.lru_cache(maxsize=None)
def _filter_features(n):
    j = np.arange(2 * n)
    d = np.where(j <= n, j, 2 * n - j)
    d = np.where(j == n, 0, d)
    bands = (FILT_EMB - 1) // 2
    t01 = np.linspace(0.0, 1.0, n)[d]
    w = 2.0 * np.pi * d.astype(np.float64) / n
    f = np.linspace(1e-4, bands - 1, bands)[None, :]
    feats = np.zeros((2 * n, LANES), np.float64)
    feats[:, 0] = t01
    feats[:, 1:1 + bands] = np.cos(f * w[:, None])
    feats[:, 1 + bands:FILT_EMB] = -np.sin(f * w[:, None])
    feats[:, 64] = t01
    feats[:, 65] = (j < n)
    feats[:, 66] = (j != n)
    return feats.astype(np.float32)


def _filter_kernel(f_ref, w1_ref, b1_ref, w2_ref, b2_ref, w3_ref, b3_ref, w4_ref, fr_ref, dl_ref, k_ref, s_ref):
    f = f_ref[...]
    fr = fr_ref[...]
    mm = lambda a, b: jnp.dot(a, b, preferred_element_type=F32, precision=HI)
    h = jnp.sin(fr * (mm(w1_ref[...], f) + b1_ref[...]))
    h = jnp.sin(fr * (mm(w2_ref[...], h) + b2_ref[...]))
    h = jnp.sin(fr * (mm(w3_ref[...], h) + b3_ref[...]))
    hf = mm(w4_ref[...], h)
    win = jnp.exp(-dl_ref[...] * f[64:65]) + DECAY_SHIFT
    k = (jnp.where(f[65:66] > 0.5, hf[:CONV_W], hf[CONV_W:]) * win * f[66:67]).T
    k_ref[...] = k

    @pl.when(pl.program_id(0) == 0)
    def _():
        s_ref[...] = jnp.zeros_like(s_ref)

    s_ref[...] += jnp.sum(jnp.abs(k), axis=0, keepdims=True)


def _implicit_filter(n, w1, b1, w2, b2, w3, b3, w4, freq):
    feats_t = jnp.asarray(np.ascontiguousarray(_filter_features(n).T))
    col = lambda a: a.reshape(-1, 1)
    w1t = jnp.pad(w1.T, ((0, 0), (0, LANES - FILT_EMB)))
    deltas = np.abs(np.linspace(math.log(DECAY_TARGET) / SLOW_DECAY_PCT, math.log(DECAY_TARGET) / FAST_DECAY_PCT,
                                CONV_W)).astype(np.float32).reshape(CONV_W, 1)
    tr = min(1024, 2 * n)
    sq = _full((FILT_WIDTH, FILT_WIDTH))
    vec = _full((FILT_WIDTH, 1))
    return pl.pallas_call(
        _filter_kernel,
        grid=(2 * n // tr,),
        in_specs=[pl.BlockSpec((LANES, tr), lambda i: (0, i)), _full((FILT_WIDTH, LANES)), vec, sq, vec, sq, vec,
                  _full((2 * CONV_W, FILT_WIDTH)), vec, _full((CONV_W, 1))],
        out_specs=[pl.BlockSpec((tr, CONV_W), lambda i: (i, 0)), _full((1, CONV_W))],
        out_shape=[jax.ShapeDtypeStruct((2 * n, CONV_W), F32), jax.ShapeDtypeStruct((1, CONV_W), F32)],
        compiler_params=_params(("arbitrary",)),
        name="hyena_filter",
    )(feats_t, w1t, col(b1), w2.T, col(b2), w3.T, col(b3), w4.T, col(freq), jnp.asarray(deltas))


def _twiddle(idx, mod):
    ang = 2.0 * np.pi * (idx % mod) / mod
    return np.cos(ang), -np.sin(ang)


def _real_form(mr, mi):
    return np.concatenate([np.concatenate([mr, -mi], -1), np.concatenate([mi, mr], -1)], -2)


@functools.lru_cache(maxsize=None)
def _dft_tables(n2):
    n1 = DFT_N1
    n = n1 * n2
    h = n2 // 2
    a2 = np.arange(n2)
    fr, fi = _twiddle(np.outer(a2, a2), n2)
    m_data = _real_form(fr[:, :h], fi[:, :h])
    m_filt = np.concatenate([fr, fi], 0)
    k2 = a2[:, None, None]
    k1 = np.arange(n1)[None, :, None]
    c1 = np.arange(n1)[None, None, :]
    g = _real_form(*_twiddle(c1 * (n2 * k1 + k2), n))
    a1 = np.arange(n1)
    f1 = _real_form(*_twiddle(np.outer(a1, a1), n1))
    t2 = a1[:, None, None]
    t1 = np.arange(h)[None, :, None]
    j1 = a2[None, None, :]
    hh = _real_form(*_twiddle(j1 * (n1 * t1 + t2), n))
    cast = lambda m: np.asarray(m, dtype=BF16)
    return cast(m_data), cast(m_filt), cast(g), cast(f1), cast(hh)


DFT_ROW_CHUNK = SUBLANES


def _dft_rows_kernel(m_ref, *refs):
    o_ref = refs[-1]
    cols = [jnp.concatenate([r[0, :, t, :] for r in refs[:-1]], axis=0) for t in range(DFT_ROW_CHUNK)]
    rhs = jnp.concatenate(cols, axis=1).astype(BF16)
    out = jnp.dot(m_ref[...], rhs, preferred_element_type=F32)
    half = out.shape[0] // 2
    o_ref[0, 0] = out[:half].astype(o_ref.dtype)
    o_ref[0, 1] = out[half:].astype(o_ref.dtype)


def _dft_stage1(mat, views, pairs, rows, n2, c):
    nin = len(views)
    lc = DFT_ROW_CHUNK * c
    in_specs = [_full(mat.shape)] + [
        pl.BlockSpec((1, rows, DFT_ROW_CHUNK, c), (lambda p, j, a=a: (nin * p + a, 0, j, 0))) for a in range(nin)]
    return pl.pallas_call(
        _dft_rows_kernel,
        grid=(pairs, DFT_N1 // DFT_ROW_CHUNK),
        in_specs=in_specs,
        out_specs=pl.BlockSpec((1, 2, n2, lc), lambda p, j: (p, 0, 0, j)),
        out_shape=jax.ShapeDtypeStruct((pairs, 2, n2, DFT_N1 * c), BF16),
        compiler_params=_params(("arbitrary", "arbitrary")),
        name="dft_stage1",
    )(mat, *views)


def _spectrum_kernel(a_ref, g_ref, sc_ref, o_ref, *, kb):
    for j in range(kb):
        rhs = jnp.concatenate([a_ref[0, 0, j], a_ref[0, 1, j]], axis=0)
        x = jnp.dot(g_ref[j], rhs, preferred_element_type=F32)
        o_ref[0, j] = (x[:DFT_N1] * sc_ref[...]).astype(o_ref.dtype)
        o_ref[1, j] = (x[DFT_N1:] * sc_ref[...]).astype(o_ref.dtype)


def _filter_spectrum(a, g, scale, n2, kb):
    c = a.shape[-1]
    return pl.pallas_call(
        functools.partial(_spectrum_kernel, kb=kb),
        grid=(n2 // kb,),
        in_specs=[pl.BlockSpec((1, 2, kb, DFT_N1, c), lambda k: (0, 0, k, 0, 0)),
                  pl.BlockSpec((kb, 2 * DFT_N1, 2 * DFT_N1), lambda k: (k, 0, 0)), _full((1, c))],
        out_specs=pl.BlockSpec((2, kb, DFT_N1, c), lambda k: (0, k, 0, 0)),
        out_shape=jax.ShapeDtypeStruct((2, n2, DFT_N1, c), BF16),
        compiler_params=_params(("arbitrary",)),
        name="filter_spectrum",
    )(a, g, scale)


def _dft_mid_kernel(a_ref, g_ref, kh_ref, f_ref, o_ref, *, kb):
    for j in range(kb):
        rhs = jnp.concatenate([a_ref[0, 0, j], a_ref[0, 1, j]], axis=0)
        x = jnp.dot(g_ref[j], rhs, preferred_element_type=F32)
        xr, xi = x[:DFT_N1], x[DFT_N1:]
        kr, ki = kh_ref[0, j].astype(F32), kh_ref[1, j].astype(F32)
        yr = xr * kr - xi * ki
        yi = xr * ki + xi * kr
        v = jnp.concatenate([yr, -yi], axis=0).astype(BF16)
        b = jnp.dot(f_ref[...], v, preferred_element_type=F32)
        o_ref[0, 0, :, j, :] = b[:DFT_N1]
        o_ref[0, 1, :, j, :] = b[DFT_N1:]


def _dft_mid(a, g, khat, f1, pairs, n2, kb):
    c = a.shape[-1]
    blk = pl.BlockSpec((1, 2, kb, DFT_N1, c), lambda k, p: (p, 0, k, 0, 0))
    return pl.pallas_call(
        functools.partial(_dft_mid_kernel, kb=kb),
        grid=(n2 // kb, pairs),
        in_specs=[blk, pl.BlockSpec((kb, 2 * DFT_N1, 2 * DFT_N1), lambda k, p: (k, 0, 0)),
                  pl.BlockSpec((2, kb, DFT_N1, c), lambda k, p: (0, k, 0, 0)), _full(f1.shape)],
        out_specs=pl.BlockSpec((1, 2, DFT_N1, kb, c), lambda k, p: (p, 0, 0, k, 0)),
        out_shape=jax.ShapeDtypeStruct((pairs, 2, DFT_N1, n2, c), F32),
        compiler_params=_params(("arbitrary", "arbitrary")),
        name="dft_mid",
    )(a, g, khat, f1)


def _dft_last_kernel(b_ref, h_ref, u_ref, x0_ref, bd_ref, o_ref, *, c):
    half = h_ref.shape[1] // 2
    bd = bd_ref[...]
    for t in range(DFT_ROW_CHUNK):
        rhs = jnp.concatenate([b_ref[0, 0, t], b_ref[0, 1, t]], axis=0).astype(BF16)
        v = jnp.dot(h_ref[t], rhs, preferred_element_type=F32)
        o_ref[0, :, t, :] = (v[:half] + u_ref[0, :, t, :] * bd) * x0_ref[0, :, t, :]
        o_ref[1, :, t, :] = (-v[half:] + u_ref[1, :, t, :] * bd) * x0_ref[1, :, t, :]


def _dft_last(bm, hh, u_view, x0_view, bias_d, pairs, n2, c):
    half = n2 // 2
    tc = DFT_ROW_CHUNK
    io = pl.BlockSpec((2, half, tc, c), lambda p, j: (p, 0, j, 0))
    return pl.pallas_call(
        functools.partial(_dft_last_kernel, c=c),
        grid=(pairs, DFT_N1 // tc),
        in_specs=[pl.BlockSpec((1, 2, tc, n2, c), lambda p, j: (p, 0, j, 0, 0)),
                  pl.BlockSpec((tc, n2, 2 * n2), lambda p, j: (j, 0, 0)), io, io, _full((1, c))],
        out_specs=io,
        out_shape=jax.ShapeDtypeStruct((2 * pairs, half, DFT_N1, c), F32),
        compiler_params=_params(("arbitrary", "arbitrary")),
        name="dft_last",
    )(bm, hh, u_view, x0_view, bias_d)


def _long_conv_mixer(u, x0, kfilt, ksum, bias_d):
    b, n, c = u.shape
    n2 = 2 * n // DFT_N1
    half = n2 // 2
    pairs = b // 2
    m_data, m_filt, g, f1, hh = (jnp.asarray(t) for t in _dft_tables(n2))
    kb = min(8, n2)
    scale = 1.0 / (ksum * float(DFT_N1 * n2))
    ka = _dft_stage1(m_filt, [kfilt.reshape(1, n2, DFT_N1, c)], 1, n2, n2, c)
    khat = _filter_spectrum(ka.reshape(1, 2, n2, DFT_N1, c), g, scale, n2, kb)
    u_view = u.reshape(b, half, DFT_N1, c)
    a = _dft_stage1(m_data, [u_view, u_view], pairs, half, n2, c)
    bm = _dft_mid(a.reshape(pairs, 2, n2, DFT_N1, c), g, khat, f1, pairs, n2, kb)
    out = _dft_last(bm, hh, u_view, x0.reshape(b, half, DFT_N1, c),
                    bias_d.reshape(1, c), pairs, n2, c)
    return out.reshape(b, n, c)


@functools.lru_cache(maxsize=None)
def _small_dft_tables(n):
    big = 2 * n
    a = np.arange(big)
    fr, fi = _twiddle(np.outer(a, a), big)
    cast = lambda m: np.asarray(m, dtype=BF16)
    return (cast(np.concatenate([fr, fi], 0)),
            cast(_real_form(fr[:, :n], fi[:, :n])),
            cast(_real_form(fr[:n], fi[:n])))


def _small_conv_kernel(k_ref, ks_ref, u_ref, x0_ref, bd_ref, mf_ref, md_ref, mi_ref, o_ref, *, n):
    big = 2 * n
    kh = jnp.dot(mf_ref[...], k_ref[...].astype(BF16), preferred_element_type=F32) * (1.0 / (ks_ref[...] * big))
    kr, ki = kh[:big], kh[big:]
    rhs = jnp.concatenate([u_ref[0], u_ref[1]], axis=0).astype(BF16)
    x = jnp.dot(md_ref[...], rhs, preferred_element_type=F32)
    xr, xi = x[:big], x[big:]
    v = jnp.concatenate([xr * kr - xi * ki, -(xr * ki + xi * kr)], axis=0).astype(BF16)
    y = jnp.dot(mi_ref[...], v, preferred_element_type=F32)
    bd = bd_ref[...]
    o_ref[0] = ((y[:n] + u_ref[0] * bd) * x0_ref[0]).astype(o_ref.dtype)
    o_ref[1] = ((-y[n:] + u_ref[1] * bd) * x0_ref[1]).astype(o_ref.dtype)


def _small_conv_mixer(u, x0, kfilt, ksum, bias_d):
    b, n, c = u.shape
    mf, md, mi = (jnp.asarray(t) for t in _small_dft_tables(n))
    io = pl.BlockSpec((2, n, c), lambda p: (p, 0, 0))
    return pl.pallas_call(
        functools.partial(_small_conv_kernel, n=n),
        grid=(b // 2,),
        in_specs=[_full((2 * n, c)), _full((1, c)), io, io, _full((1, c)),
                  _full(mf.shape), _full(md.shape), _full(mi.shape)],
        out_specs=io,
        out_shape=jax.ShapeDtypeStruct((b, n, c), BF16),
        compiler_params=_params(("arbitrary",)),
        name="small_conv",
    )(kfilt, ksum, u, x0, bias_d.reshape(1, c), mf, md, mi)


MXU_DIM = 256
FFN_CHUNK_EDGES = (0, 6 * MXU_DIM, D_FF)


def _gelu_tanh(x):
    return 0.5 * x * (1.0 + jnp.tanh(math.sqrt(2.0 / math.pi) * (x + 0.044715 * (x * x * x))))


def _ffn_kernel(x_ref, xp_ref, xn_ref, oa_ref, oap_ref, oan_ref, oc_ref, ocp_ref, ocn_ref, wa_ref, wc_ref, g1_ref,
                sh_ref, sc_ref, gt_ref, g_ref, wu_ref, cw_ref, cb_ref, wd_ref, o_ref):
    i = pl.program_id(1)
    nt = pl.num_programs(1)
    tm = x_ref.shape[1]

    def halo(p_ref, n_ref):
        return jnp.concatenate([p_ref[0][-SUBLANES:], n_ref[0][:SUBLANES]], axis=0)

    def mixer_out(oa, oc):
        return (jnp.dot(oa.astype(BF16), wa_ref[...], preferred_element_type=F32)
                + jnp.dot(oc.astype(BF16), wc_ref[...], preferred_element_type=F32))

    g1 = g1_ref[0]
    xm = x_ref[0] + g1 * mixer_out(oa_ref[0], oc_ref[0])
    xh = halo(xp_ref, xn_ref) + g1 * mixer_out(halo(oap_ref, oan_ref), halo(ocp_ref, ocn_ref))
    ext = jnp.concatenate([xh[:SUBLANES], xm, xh[SUBLANES:]], axis=0)
    r = tm + 2 * SUBLANES
    h = _modulated_norm(ext, g_ref[...], sh_ref[0], sc_ref[0])
    row = lax.broadcasted_iota(jnp.int32, (r, 1), 0)
    inside = jnp.logical_and(jnp.logical_or(i > 0, row >= SUBLANES), jnp.logical_or(i < nt - 1, row < tm + SUBLANES))
    h = jnp.where(inside, h, 0.0).astype(BF16)
    hm = h[SUBLANES:tm + SUBLANES]
    acc = jnp.zeros((tm, D_MODEL), F32)
    for c0, c1 in zip(FFN_CHUNK_EDGES[:-1], FFN_CHUNK_EDGES[1:]):
        a = jnp.dot(h, wu_ref[:, c0:c1], preferred_element_type=F32)
        v = jnp.dot(hm, wu_ref[:, D_FF + c0:D_FF + c1], preferred_element_type=F32)
        ap = pltpu.roll(a, 1, 0)[SUBLANES:tm + SUBLANES]
        an = pltpu.roll(a, r - 1, 0)[SUBLANES:tm + SUBLANES]
        cw = cw_ref[:, c0:c1]
        conv = ap * cw[0:1] + a[SUBLANES:tm + SUBLANES] * cw[1:2] + an * cw[2:3] + cb_ref[:, c0:c1]
        act = (_gelu_tanh(conv) * v).astype(BF16)
        acc = acc + jnp.dot(act, wd_ref[c0:c1, :], preferred_element_type=F32)
    o_ref[0] = xm + gt_ref[0] * acc


def _mixer_out_ffn(x, oa, oc, wo_bf, gate1, sh, sc, gate2, gain, wu_bf, conv_w, conv_b, wd_bf):
    bx, t, d = x.shape
    tm = min(512, t)
    vec = pl.BlockSpec((1, 1, d), lambda b, i: (b, 0, 0))
    once = pl.Buffered(1)
    return pl.pallas_call(
        _ffn_kernel,
        grid=(bx, t // tm),
        in_specs=_halo_specs(tm, t, d) + _halo_specs(tm, t, Q_W, oa.dtype) + _halo_specs(tm, t, CONV_W, oc.dtype) + [
            pl.BlockSpec((Q_W, d), lambda b, i: (0, 0)), pl.BlockSpec((CONV_W, d), lambda b, i: (1, 0)),
            vec, vec, vec, vec, _full((1, d)),
            pl.BlockSpec((d, 2 * D_FF), lambda b, i: (0, 0), pipeline_mode=once),
            _full((3, D_FF)), _full((1, D_FF)),
            pl.BlockSpec((D_FF, d), lambda b, i: (0, 0), pipeline_mode=once)],
        out_specs=pl.BlockSpec((1, tm, d), lambda b, i: (b, i, 0)),
        out_shape=jax.ShapeDtypeStruct((bx, t, d), F32),
        compiler_params=_params(("arbitrary", "arbitrary")),
        name="mixer_out_ffn",
    )(x, x, x, oa, oa, oa, oc, oc, oc, wo_bf, wo_bf, gate1, sh, sc, gate2, gain, wu_bf, conv_w,
      conv_b.reshape(1, D_FF), wd_bf)


@functools.lru_cache(maxsize=None)
def _rope_tables(t):
    pos = np.arange(t)
    n_freq = HEAD_DIM // 4
    inv = ROPE_THETA ** (-np.arange(n_freq, dtype=np.float64) / n_freq)
    ang_r = (pos // GRID_W)[:, None] * inv
    ang_c = (pos % GRID_W)[:, None] * inv
    zero = np.zeros_like(ang_r)
    cos = np.concatenate([np.cos(ang_r)] * 2 + [np.cos(ang_c)] * 2, 1)
    sa = np.concatenate([-np.sin(ang_r), zero, -np.sin(ang_c), zero], 1)
    sb = np.concatenate([zero, np.sin(ang_r), zero, np.sin(ang_c)], 1)
    tile = lambda m: np.tile(m, (1, LANES // HEAD_DIM)).astype(np.float32)
    return tile(cos), tile(sa), tile(sb)


@functools.lru_cache(maxsize=None)
def _identity_rope_tables(t):
    return np.ones((t, LANES), np.float32), np.zeros((t, LANES), np.float32), np.zeros((t, LANES), np.float32)


@functools.lru_cache(maxsize=None)
def _head_block_diag():
    hid = np.arange(Q_W) // HEAD_DIM
    return np.asarray(hid[:, None] == hid[None, :], dtype=BF16)


def kernel(x, c, ctx, c_ctx, ada_w, ada_b, norm_mix, norm_ffn, mix_w_in, mix_w_out, attn_q_norm, attn_k_norm,
           swa_sink, hy_conv_w, hy_conv_b, hy_w1, hy_b1, hy_w2, hy_b2, hy_w3, hy_b3, hy_w4, hy_freq, hy_bias_d,
           sc_conv_w, ffn_w_up, ffn_conv_w, ffn_conv_b, ffn_w_down):
    b, s, d = x.shape
    s_ctx = ctx.shape[1]
    depth = ada_w.shape[0]
    assert d == D_MODEL and b % 2 == 0 and b + 1 <= MOD_ROWS and s % 1024 == 0 and s_ctx % SUBLANES == 0

    cvec = jnp.concatenate([c, c_ctx[None, :], jnp.zeros((MOD_ROWS - b - 1, d), F32)], axis=0)
    mods = _mods(cvec, ada_w, ada_b)
    rope = [jnp.asarray(t) for t in _rope_tables(s)]
    rope_ctx = [jnp.asarray(t) for t in _identity_rope_tables(s_ctx)]
    bd = jnp.asarray(_head_block_diag())
    xc = ctx
    weights = _weights_bf16([mix_w_in, mix_w_out, ffn_w_up, ffn_w_down])

    for i in range(depth):
        last = i == depth - 1
        j = i // 2
        lat = [mods[i, :b, k * d:(k + 1) * d][:, None, :] for k in range(6)]
        cx = [jnp.broadcast_to(mods[i, b, k * d:(k + 1) * d][None, None, :], (b, 1, d)) for k in range(6)]
        w_in, w_out, w_up, w_down = (w[i] for w in weights)
        g_mix = norm_mix[i].reshape(1, d)
        g_ffn = norm_ffn[i].reshape(1, d)
        qg = jnp.tile(attn_q_norm[i], N_Q_HEADS).reshape(1, Q_W)
        kg = jnp.tile(attn_k_norm[i], N_KV_HEADS).reshape(1, KV_W)

        conv = (hy_conv_w[j], hy_conv_b[j]) if i % 2 == 0 else (sc_conv_w[j],)
        q, k, vt, *mix = _inproj(x, lat[0], lat[1], g_mix, w_in, qg, kg, bd, *rope, *conv)
        qc, kc, vct, *mixc = _inproj(xc, cx[0], cx[1], g_mix, w_in, qg, kg, bd, *rope_ctx, *conv)

        bound = _score_bound(attn_q_norm[i], attn_k_norm[i])
        if i % 2 == 0:
            fargs = (hy_w1[j], hy_b1[j], hy_w2[j], hy_b2[j], hy_w3[j], hy_b3[j], hy_w4[j], hy_freq[j])
            o_attn = _dense_attention(q, k, vt, kc, vct, bound=bound)
            kf, ks = _implicit_filter(s, *fargs)
            o_conv = _long_conv_mixer(*mix, kf, ks, hy_bias_d[j])
            if not last:
                oc_attn = _dense_attention(qc, kc, vct, bound=bound)
                kfc, ksc = _implicit_filter(s_ctx, *fargs)
                oc_conv = _small_conv_mixer(*mixc, kfc, ksc, hy_bias_d[j])
        else:
            sink = swa_sink[j].reshape(N_KV_HEADS, GROUP)
            o_attn = _banded_attention(q, k, vt, kc, vct, sink, bound)
            o_conv = mix[0]
            if not last:
                oc_attn = _dense_attention(qc, kc, vct, sink=sink)
                oc_conv = mixc[0]

        ffn = (g_ffn, w_up, ffn_conv_w[i], ffn_conv_b[i], w_down)
        x = _mixer_out_ffn(x, o_attn, o_conv, w_out, lat[2], lat[3], lat[4], lat[5], *ffn)
        if not last:
            xc = _mixer_out_ffn(xc, oc_attn, oc_conv, w_out, cx[2], cx[3], cx[4], cx[5], *ffn)
    return x
```

```python
import functools
import math

import numpy as np
import jax
import jax.numpy as jnp
from jax import lax
from jax.experimental import pallas as pl
from jax.experimental.pallas import tpu as pltpu

F32 = jnp.float32
BF16 = jnp.bfloat16
HI = lax.Precision.HIGHEST

D_MODEL = 1024
GRID_W = 64
HEAD_DIM = 64
N_Q_HEADS = 8
N_KV_HEADS = 4
GROUP = N_Q_HEADS // N_KV_HEADS
Q_W = N_Q_HEADS * HEAD_DIM
KV_W = N_KV_HEADS * HEAD_DIM
QKV_W = Q_W + 2 * KV_W
CONV_W = D_MODEL // 2
MIX_IN_W = QKV_W + 3 * CONV_W
WINDOW = 128
ROPE_THETA = 10000.0
FILT_EMB = 33
FILT_WIDTH = 64
DECAY_TARGET = 1e-2
FAST_DECAY_PCT = 0.3
SLOW_DECAY_PCT = 1.5
DECAY_SHIFT = 0.05
D_FF = 2816
NEG_INF = -1e30
RMS_EPS = 1e-6
LOG2E = 1.4426950408889634
Q_SCALE = HEAD_DIM ** -0.5 * LOG2E

LANES = 128
SUBLANES = 8
DFT_N1 = 128
VMEM_LIMIT_MB = 56
VT_ROWS = 64 + 2 * SUBLANES
MOD_ROWS = SUBLANES
PROJ_SUB_ROWS = 128


def _params(sem, vmem_mb=VMEM_LIMIT_MB):
    return pltpu.CompilerParams(dimension_semantics=sem, vmem_limit_bytes=vmem_mb * 1024 * 1024)


def _full(shape):
    nd = len(shape)
    return pl.BlockSpec(shape, lambda *_: (0,) * nd)


def _mods_kernel(c_ref, w_ref, b_ref, o_ref):
    c = c_ref[...]
    s = c / (1.0 + jnp.exp(-c))
    o_ref[0] = jnp.dot(s, w_ref[0], preferred_element_type=F32, precision=HI) + b_ref[0]


def _mods(cvec, ada_w, ada_b):
    depth, d, n6 = ada_w.shape
    tn = 1536
    return pl.pallas_call(
        _mods_kernel,
        grid=(depth, n6 // tn),
        in_specs=[_full((MOD_ROWS, d)),
                  pl.BlockSpec((1, d, tn), lambda l, j: (l, 0, j)),
                  pl.BlockSpec((1, 1, tn), lambda l, j: (l, 0, j))],
        out_specs=pl.BlockSpec((1, MOD_ROWS, tn), lambda l, j: (l, 0, j)),
        out_shape=jax.ShapeDtypeStruct((depth, MOD_ROWS, n6), F32),
        compiler_params=_params(("arbitrary", "arbitrary")),
        name="ada_mods",
    )(cvec, ada_w, ada_b.reshape(depth, 1, n6))


def _modulated_norm(x, gain, shift, scale):
    ms = jnp.mean(x * x, axis=-1, keepdims=True)
    return (x * lax.rsqrt(ms + RMS_EPS)) * gain * (1.0 + scale) + shift


WEIGHT_CAST_STEPS = 8


def _cast_kernel(*refs):
    n = len(refs) // 2
    for w_ref, o_ref in zip(refs[:n], refs[n:]):
        o_ref[...] = w_ref[0].astype(o_ref.dtype)


def _weights_bf16(stacked):
    depth = stacked[0].shape[0]
    in_specs, out_specs, out_shape, args = [], [], [], []
    for w in stacked:
        _, rows, cols = w.shape
        tr = rows // WEIGHT_CAST_STEPS
        for layer in range(depth):
            in_specs.append(pl.BlockSpec((1, tr, cols), lambda r, layer=layer: (layer, r, 0)))
            out_specs.append(pl.BlockSpec((tr, cols), lambda r: (r, 0)))
            out_shape.append(jax.ShapeDtypeStruct((rows, cols), BF16))
            args.append(w)
    outs = pl.pallas_call(
        _cast_kernel,
        grid=(WEIGHT_CAST_STEPS,),
        in_specs=in_specs,
        out_specs=out_specs,
        out_shape=out_shape,
        compiler_params=_params(("arbitrary",)),
        name="weights_bf16",
    )(*args)
    return [outs[k * depth:(k + 1) * depth] for k in range(len(stacked))]


def _halo_specs(tm, t, width, dtype=F32):
    rows = SUBLANES * 4 // jnp.dtype(dtype).itemsize
    nb = tm // rows
    last = t // rows - 1
    return [pl.BlockSpec((1, tm, width), lambda b, i: (b, i, 0)),
            pl.BlockSpec((1, rows, width), lambda b, i: (b, jnp.maximum(i * nb - 1, 0), 0)),
            pl.BlockSpec((1, rows, width), lambda b, i: (b, jnp.minimum((i + 1) * nb, last), 0))]


def _inproj_kernel(*refs, mode):
    (x_ref, xp_ref, xn_ref, sh_ref, sc_ref, g_ref, w_ref, qg_ref, kg_ref, bd_ref,
     cos_ref, sa_ref, sb_ref, cw_ref) = refs[:14]
    if mode == "hyena":
        cb_ref, q_ref, k_ref, vt_ref, u_ref, x0_ref = refs[14:]
    else:
        q_ref, k_ref, vt_ref, oc_ref = refs[14:]
    i = pl.program_id(1)
    nt = pl.num_programs(1)
    tm = x_ref.shape[1]
    sub = min(PROJ_SUB_ROWS, tm)
    gain, shift, scale = g_ref[...], sh_ref[0], sc_ref[0]

    xh = jnp.concatenate([xp_ref[0], xn_ref[0]], axis=0)
    zh = jnp.dot(_modulated_norm(xh, gain, shift, scale).astype(BF16), w_ref[:, QKV_W:], preferred_element_type=F32)
    before = jnp.where(i > 0, zh[:SUBLANES], 0.0)
    after = jnp.where(i < nt - 1, zh[SUBLANES:], 0.0)
    cw = cw_ref[...]

    def conv_stage(ext, rows):
        n = sub + 2 * SUBLANES
        mid = slice(SUBLANES, sub + SUBLANES)
        if mode == "hyena":
            c = (pltpu.roll(ext, 1, 0)[mid] * cw[0:1] + ext[mid] * cw[1:2] + pltpu.roll(ext, n - 1, 0)[mid] * cw[2:3]
                 + cb_ref[...])
            x0_ref[0, rows] = c[:, :CONV_W]
            u_ref[0, rows] = c[:, 2 * CONV_W:] * c[:, CONV_W:2 * CONV_W]
        else:
            pr = ext[:, CONV_W:2 * CONV_W] * ext[:, 2 * CONV_W:]
            conv = pltpu.roll(pr, 1, 0)[mid] * cw[0:1] + pr[mid] * cw[1:2] + pltpu.roll(pr, n - 1, 0)[mid] * cw[2:3]
            oc_ref[0, rows] = (ext[mid, :CONV_W] * conv).astype(oc_ref.dtype)

    def head_norm_rope(t, hgain, bd, cos, sa, sb):
        ssq = jnp.dot((t * t).astype(BF16), bd, preferred_element_type=F32)
        t = t * lax.rsqrt(ssq * (1.0 / HEAD_DIM) + RMS_EPS) * hgain
        outs = []
        for j in range(t.shape[1] // LANES):
            tj = t[:, j * LANES:(j + 1) * LANES]
            outs.append(tj * cos + pltpu.roll(tj, LANES - 16, 1) * sa + pltpu.roll(tj, 16, 1) * sb)
        return outs

    held = None
    for r0 in range(0, tm, sub):
        rows = slice(r0, r0 + sub)
        h = _modulated_norm(x_ref[0, rows], gain, shift, scale)
        hb = h.astype(BF16)
        z = jnp.dot(hb, w_ref[:, :QKV_W], preferred_element_type=F32)
        zc = jnp.dot(hb, w_ref[:, QKV_W:], preferred_element_type=F32)
        if held is not None:
            conv_stage(jnp.concatenate([before, held[0], zc[:SUBLANES]], axis=0), held[1])
            before = held[0][-SUBLANES:]
        held = (zc, rows)
        tabs = (cos_ref[rows], sa_ref[rows], sb_ref[rows])
        qs = head_norm_rope(z[:, :Q_W], qg_ref[...], bd_ref[...], *tabs)
        ks = head_norm_rope(z[:, Q_W:Q_W + KV_W], kg_ref[...], bd_ref[:KV_W, :KV_W], *tabs)
        for j, qj in enumerate(qs):
            qj = (qj * Q_SCALE).astype(BF16)
            q_ref[0, 2 * j, rows] = qj[:, :HEAD_DIM]
            q_ref[0, 2 * j + 1, rows] = qj[:, HEAD_DIM:]
        for j, kj in enumerate(ks):
            kj = kj.astype(BF16)
            k_ref[0, 2 * j, rows] = kj[:, :HEAD_DIM]
            k_ref[0, 2 * j + 1, rows] = kj[:, HEAD_DIM:]
        vt = z[:, Q_W + KV_W:QKV_W].T
        ones_row = (lax.broadcasted_iota(jnp.int32, (VT_ROWS - HEAD_DIM, sub), 0) == 0).astype(F32)
        for hh in range(N_KV_HEADS):
            vt_ref[0, hh, :, rows] = jnp.concatenate([vt[hh * HEAD_DIM:(hh + 1) * HEAD_DIM], ones_row],
                                                     axis=0).astype(BF16)

    conv_stage(jnp.concatenate([before, held[0], after], axis=0), held[1])


def _inproj(x, sh, sc, gain, w_bf, qg, kg, bd, cos, sa, sb, conv_w, conv_b=None):
    bx, t, d = x.shape
    tm = min(512, t)
    mode = "hyena" if conv_b is not None else "short"
    vec = pl.BlockSpec((1, 1, d), lambda b, i: (b, 0, 0))
    tab = pl.BlockSpec((tm, LANES), lambda b, i: (i, 0))
    row = lambda w: pl.BlockSpec((1, tm, w), lambda b, i: (b, i, 0))
    in_specs = _halo_specs(tm, t, d) + [vec, vec, _full((1, d)), _full((d, MIX_IN_W)), _full((1, Q_W)),
                                        _full((1, KV_W)), _full((Q_W, Q_W)), tab, tab, tab, _full(conv_w.shape)]
    args = [x, x, x, sh, sc, gain, w_bf, qg, kg, bd, cos, sa, sb, conv_w]
    out_specs = [pl.BlockSpec((1, N_Q_HEADS, tm, HEAD_DIM), lambda b, i: (b, 0, i, 0)),
                 pl.BlockSpec((1, N_KV_HEADS, tm, HEAD_DIM), lambda b, i: (b, 0, i, 0)),
                 pl.BlockSpec((1, N_KV_HEADS, VT_ROWS, tm), lambda b, i: (b, 0, 0, i))]
    out_shape = [jax.ShapeDtypeStruct((bx, N_Q_HEADS, t, HEAD_DIM), BF16),
                 jax.ShapeDtypeStruct((bx, N_KV_HEADS, t, HEAD_DIM), BF16),
                 jax.ShapeDtypeStruct((bx, N_KV_HEADS, VT_ROWS, t), BF16)]
    if mode == "hyena":
        in_specs.append(_full((1, 3 * CONV_W)))
        args.append(conv_b.reshape(1, 3 * CONV_W))
        out_specs += [row(CONV_W), row(CONV_W)]
        out_shape += [jax.ShapeDtypeStruct((bx, t, CONV_W), F32)] * 2
    else:
        out_specs.append(row(CONV_W))
        out_shape.append(jax.ShapeDtypeStruct((bx, t, CONV_W), BF16))
    return pl.pallas_call(
        functools.partial(_inproj_kernel, mode=mode),
        grid=(bx, t // tm),
        in_specs=in_specs,
        out_specs=out_specs,
        out_shape=out_shape,
        compiler_params=_params(("arbitrary", "arbitrary")),
        name="in_proj",
    )(*args)


def _score_bound(q_gain, k_gain):
    b = HEAD_DIM * Q_SCALE * BOUND_MARGIN * jnp.max(jnp.abs(q_gain)) * jnp.max(jnp.abs(k_gain))
    return b.reshape(1, 1).astype(F32)


def _scores_t(ks, q):
    return lax.dot_general(ks, q, (((1,), (1,)), ((), ())), preferred_element_type=F32)


def _sink_column(sink_ref, hh, tq, m, acc):
    col = lax.broadcasted_iota(jnp.int32, m.shape, 1)
    sink = jnp.where(col < tq, sink_ref[hh, 0], sink_ref[hh, 1]) * LOG2E
    m_new = jnp.maximum(m, sink)
    den_row = lax.broadcasted_iota(jnp.int32, acc.shape, 0) == HEAD_DIM
    return acc * jnp.exp2(m - m_new) + jnp.where(den_row, jnp.exp2(sink - m_new), 0.0)


def _attn_finish(acc, tq):
    o = acc / acc[HEAD_DIM:HEAD_DIM + 1]
    pad = jnp.zeros((LANES - VT_ROWS, acc.shape[1]), F32)
    o = jnp.concatenate([o, pad], axis=0).T
    return jnp.concatenate([o[:tq, :HEAD_DIM], o[tq:, :HEAD_DIM]], axis=1)


EXP_CHUNK_ELEMS = 32 * SUBLANES * LANES


def _score_stage(ks, q, s_ref, bias=None):
    s = _scores_t(ks, q)
    if bias is not None:
        s = s + bias
    s_ref[0:s.shape[0]] = s
    return jnp.max(s, axis=0, keepdims=True)


def _softmax_stage(s_ref, p_ref, n, vt, mx, m, acc_ref):
    m_new = jnp.maximum(m, mx)
    rows = EXP_CHUNK_ELEMS // m.shape[1]
    for c in range(0, n, rows):
        p_ref[c:c + rows] = jnp.exp2(s_ref[c:c + rows] - m_new).astype(BF16)
    acc_ref[...] = acc_ref[...] * jnp.exp2(m - m_new) + jnp.dot(vt, p_ref[0:n], preferred_element_type=F32)
    return m_new


BOUNDED_TILES = 8
BOUND_MARGIN = 1.01
SAFE_SHIFT = 40.0


def _flash_exact(q, k_tile, vt_tile, ctx, sa_ref, sb_ref, p_ref, acc_ref, *, tk, n_main):
    pa_ref, pb_ref = p_ref.at[0:tk], p_ref.at[tk:2 * tk]
    acc_ref[...] = jnp.zeros_like(acc_ref)
    m = jnp.full((1, q.shape[0]), NEG_INF, F32)
    mx_a = _score_stage(k_tile(0), q, sa_ref)
    if n_main > 1:
        def body(i, carry):
            m, mx_a = carry
            mx_b = _score_stage(k_tile(2 * i + 1), q, sb_ref)
            m = _softmax_stage(sa_ref, pa_ref, tk, vt_tile(2 * i), mx_a, m, acc_ref)
            mx_a = _score_stage(k_tile(2 * i + 2), q, sa_ref)
            m = _softmax_stage(sb_ref, pb_ref, tk, vt_tile(2 * i + 1), mx_b, m, acc_ref)
            return m, mx_a

        m, mx_a = lax.fori_loop(0, n_main // 2 - 1, body, (m, mx_a))
        mx_b = _score_stage(k_tile(n_main - 1), q, sb_ref)
        m = _softmax_stage(sa_ref, pa_ref, tk, vt_tile(n_main - 2), mx_a, m, acc_ref)
        last = (sb_ref, pb_ref, mx_b)
        spare = (sa_ref, pa_ref)
    else:
        last = (sa_ref, pa_ref, mx_a)
        spare = (sb_ref, pb_ref)
    if ctx is not None:
        mx_c = _score_stage(ctx[0], q, spare[0])
    m = _softmax_stage(last[0], last[1], tk, vt_tile(n_main - 1), last[2], m, acc_ref)
    if ctx is not None:
        m = _softmax_stage(spare[0], spare[1], ctx[0].shape[0], ctx[1], mx_c, m, acc_ref)
    return m


def _exp2_bf16(s):
    return jnp.exp2(s).astype(BF16)


def _flash_bounded(q, k_ref, vt_ref, ctx, p_ref, acc_ref, *, tk, n_main):
    step = BOUNDED_TILES * tk if n_main % BOUNDED_TILES == 0 else tk

    def group(j):
        off = pl.multiple_of(j * step, step)
        for c in range(0, step, tk):
            p_ref[c:c + tk] = _exp2_bf16(_scores_t(k_ref[0, 0, pl.ds(off + c, tk), :], q))
        return jnp.dot(vt_ref[0, 0, :, pl.ds(off, step)], p_ref[0:step], preferred_element_type=F32)

    first = group(0)
    if ctx is not None:
        pc_ref = p_ref.at[BOUNDED_TILES * tk:BOUNDED_TILES * tk + ctx[0].shape[0]]
        pc_ref[...] = _exp2_bf16(_scores_t(ctx[0], q))
        first = first + jnp.dot(ctx[1], pc_ref[...], preferred_element_type=F32)
    acc_ref[...] = first

    def body(j, carry):
        acc_ref[...] += group(j)
        return carry

    lax.fori_loop(1, n_main * tk // step, body, 0)


def _flash_kernel(*refs, tq, tk, n_main, has_ctx, has_sink):
    sa_ref, sb_ref, p_ref, acc_ref = refs[-4:]
    o_ref = refs[-5]
    refs = list(refs[:-5])
    sink_ref = refs.pop(0) if has_sink else None
    bound_ref = None if has_sink else refs.pop(0)
    q_ref, k_ref, vt_ref = refs[:3]
    m_cols = GROUP * tq
    q = q_ref[0].reshape(m_cols, HEAD_DIM)
    ctx = (refs[3][0, 0], refs[4][0, 0]) if has_ctx else None

    def k_tile(j):
        return k_ref[0, 0, pl.ds(pl.multiple_of(j * tk, tk), tk), :]

    def vt_tile(j):
        return vt_ref[0, 0, :, pl.ds(pl.multiple_of(j * tk, tk), tk)]

    exact = functools.partial(_flash_exact, q, k_tile, vt_tile, ctx, sa_ref, sb_ref, p_ref, acc_ref,
                              tk=tk, n_main=n_main)
    if has_sink:
        acc = _sink_column(sink_ref, pl.program_id(1), tq, exact(), acc_ref[...])
    else:
        bound = bound_ref[0, 0]
        safe = bound <= SAFE_SHIFT

        @pl.when(safe)
        def _():
            _flash_bounded(q, k_ref, vt_ref, ctx, p_ref, acc_ref, tk=tk, n_main=n_main)

        @pl.when(jnp.logical_not(safe))
        def _():
            exact()

        acc = acc_ref[...]
    o_ref[0] = _attn_finish(acc, tq).astype(o_ref.dtype)


def _kv_specs(n):
    return [pl.BlockSpec((1, 1, n, HEAD_DIM), lambda b, h, i: (b, h, 0, 0)),
            pl.BlockSpec((1, 1, VT_ROWS, n), lambda b, h, i: (b, h, 0, 0))]


def _dense_attention(q, k, vt, kc=None, vct=None, *, bound=None, sink=None):
    bx, _, t, _ = q.shape
    s = k.shape[2]
    tq = min(512, t)
    tk = min(512, s)
    has_ctx = kc is not None
    has_sink = sink is not None
    in_specs = [pl.BlockSpec((1, GROUP, tq, HEAD_DIM), lambda b, h, i: (b, h, i, 0))] + _kv_specs(s)
    args = [q, k, vt]
    if has_ctx:
        in_specs += _kv_specs(kc.shape[2])
        args += [kc, vct]
    in_specs = [pl.BlockSpec(memory_space=pltpu.SMEM)] + in_specs
    args = [sink if has_sink else bound] + args
    n_main = s // tk
    assert n_main == 1 or n_main % 2 == 0
    n_ctx = kc.shape[2] if has_ctx else 0
    assert n_ctx <= tk
    m_cols = GROUP * tq
    return pl.pallas_call(
        functools.partial(_flash_kernel, tq=tq, tk=tk, n_main=n_main, has_ctx=has_ctx, has_sink=has_sink),
        grid=(bx, N_KV_HEADS, t // tq),
        in_specs=in_specs,
        out_specs=pl.BlockSpec((1, tq, LANES), lambda b, h, i: (b, i, h)),
        out_shape=jax.ShapeDtypeStruct((bx, t, Q_W), BF16),
        scratch_shapes=[pltpu.VMEM((tk, m_cols), F32), pltpu.VMEM((tk, m_cols), F32),
                        pltpu.VMEM((BOUNDED_TILES * tk + n_ctx, m_cols), BF16), pltpu.VMEM((VT_ROWS, m_cols), F32)],
        compiler_params=_params(("arbitrary", "arbitrary", "arbitrary")),
        name="dense_attention",
    )(*args)


BAND_SUB = 256


@functools.lru_cache(maxsize=None)
def _band_bias():
    span = BAND_SUB + 2 * WINDOW
    kr = np.arange(span)[None, :, None]
    qc = (np.arange(GROUP * BAND_SUB) % BAND_SUB)[None, None, :]
    rel = np.arange(3)[:, None, None]
    return np.where(np.abs(kr - qc - rel * WINDOW) <= WINDOW, 0.0, NEG_INF).astype(np.float32)


def _banded_kernel(sink_ref, bound_ref, q_ref, k_ref, vt_ref, kc_ref, vct_ref, bias_ref, o_ref,
                   sl_ref, sc_ref, pl_ref, pc_ref, acc_ref, *, n_sub, s_len):
    i = pl.program_id(2)
    hh = pl.program_id(1)
    sub = BAND_SUB
    m_cols = GROUP * sub
    span = sub + 2 * WINDOW
    n_ctx = kc_ref.shape[2]

    def window(u):
        q = q_ref[0, :, u * sub:(u + 1) * sub, :].reshape(m_cols, HEAD_DIM)
        q0 = (i * n_sub + u) * sub
        start = pl.multiple_of(jnp.clip(q0 - WINDOW, 0, s_len - span), WINDOW)
        return q, start, bias_ref[(q0 - start) // WINDOW]

    shift = jnp.maximum(bound_ref[0, 0], jnp.maximum(sink_ref[hh, 0], sink_ref[hh, 1]) * LOG2E)
    safe = shift <= SAFE_SHIFT

    @pl.when(safe)
    def _():
        col = lax.broadcasted_iota(jnp.int32, (1, m_cols), 1)
        sink = jnp.where(col < sub, sink_ref[hh, 0], sink_ref[hh, 1]) * LOG2E
        den_row = lax.broadcasted_iota(jnp.int32, (VT_ROWS, m_cols), 0) == HEAD_DIM
        sink_den = jnp.where(den_row, jnp.exp2(sink), 0.0)
        for u in range(n_sub):
            q, start, bias = window(u)
            pl_u, pc_u = pl_ref.at[u], pc_ref.at[u]
            pl_u[...] = _exp2_bf16(_scores_t(k_ref[0, 0, pl.ds(start, span), :], q) + bias)
            pc_u[...] = _exp2_bf16(_scores_t(kc_ref[0, 0], q))
            acc = (jnp.dot(vt_ref[0, 0, :, pl.ds(start, span)], pl_u[...], preferred_element_type=F32)
                   + jnp.dot(vct_ref[0, 0], pc_u[...], preferred_element_type=F32) + sink_den)
            o_ref[0, u * sub:(u + 1) * sub] = _attn_finish(acc, sub).astype(o_ref.dtype)

    @pl.when(jnp.logical_not(safe))
    def _():
        stats = []
        for u in range(n_sub):
            q, start, bias = window(u)
            mx_l = _score_stage(k_ref[0, 0, pl.ds(start, span), :], q, sl_ref.at[u], bias=bias)
            mx_c = _score_stage(kc_ref[0, 0], q, sc_ref.at[u])
            stats.append((start, mx_l, mx_c))
        for u in range(n_sub):
            start, mx_l, mx_c = stats[u]
            acc_u = acc_ref.at[u]
            acc_u[...] = jnp.zeros((VT_ROWS, m_cols), F32)
            m = jnp.full((1, m_cols), NEG_INF, F32)
            m = _softmax_stage(sl_ref.at[u], pl_ref.at[u], span, vt_ref[0, 0, :, pl.ds(start, span)], mx_l, m, acc_u)
            m = _softmax_stage(sc_ref.at[u], pc_ref.at[u], n_ctx, vct_ref[0, 0], mx_c, m, acc_u)
            acc = _sink_column(sink_ref, hh, sub, m, acc_u[...])
            o_ref[0, u * sub:(u + 1) * sub] = _attn_finish(acc, sub).astype(o_ref.dtype)


def _banded_attention(q, k, vt, kc, vct, sink, bound):
    bx, _, t, _ = q.shape
    n_ctx = kc.shape[2]
    n_sub = next(k for k in (8, 4, 2, 1) if t % (k * BAND_SUB) == 0)
    tq = n_sub * BAND_SUB
    m_cols = GROUP * BAND_SUB
    span = BAND_SUB + 2 * WINDOW
    bias = jnp.asarray(_band_bias())
    return pl.pallas_call(
        functools.partial(_banded_kernel, n_sub=n_sub, s_len=t),
        grid=(bx, N_KV_HEADS, t // tq),
        in_specs=[pl.BlockSpec(memory_space=pltpu.SMEM), pl.BlockSpec(memory_space=pltpu.SMEM),
                  pl.BlockSpec((1, GROUP, tq, HEAD_DIM), lambda b, h, i: (b, h, i, 0))]
                 + _kv_specs(t) + _kv_specs(n_ctx) + [_full(bias.shape)],
        out_specs=pl.BlockSpec((1, tq, LANES), lambda b, h, i: (b, i, h)),
        out_shape=jax.ShapeDtypeStruct((bx, t, Q_W), BF16),
        scratch_shapes=[pltpu.VMEM((n_sub, span, m_cols), F32), pltpu.VMEM((n_sub, n_ctx, m_cols), F32),
                        pltpu.VMEM((n_sub, span, m_cols), BF16), pltpu.VMEM((n_sub, n_ctx, m_cols), BF16),
                        pltpu.VMEM((n_sub, VT_ROWS, m_cols), F32)],
        compiler_params=_params(("arbitrary", "arbitrary", "arbitrary")),
        name="banded_attention",
    )(sink, bound, q, k, vt, kc, vct, bias)


@functools.lru_cache(maxsize=None)
def _filter_features(n):
    j = np.arange(2 * n)
    d = np.where(j <= n, j, 2 * n - j)
    d = np.where(j == n, 0, d)
    bands = (FILT_EMB - 1) // 2
    t01 = np.linspace(0.0, 1.0, n)[d]
    w = 2.0 * np.pi * d.astype(np.float64) / n
    f = np.linspace(1e-4, bands - 1, bands)[None, :]
    feats = np.zeros((2 * n, LANES), np.float64)
    feats[:, 0] = t01
    feats[:, 1:1 + bands] = np.cos(f * w[:, None])
    feats[:, 1 + bands:FILT_EMB] = -np.sin(f * w[:, None])
    feats[:, 64] = t01
    feats[:, 65] = (j < n)
    feats[:, 66] = (j != n)
    return feats.astype(np.float32)


def _filter_kernel(f_ref, w1_ref, b1_ref, w2_ref, b2_ref, w3_ref, b3_ref, w4_ref, fr_ref, dl_ref, k_ref, s_ref):
    f = f_ref[...]
    fr = fr_ref[...]
    mm = lambda a, b: jnp.dot(a, b, preferred_element_type=F32, precision=HI)
    h = jnp.sin(fr * (mm(w1_ref[...], f) + b1_ref[...]))
    h = jnp.sin(fr * (mm(w2_ref[...], h) + b2_ref[...]))
    h = jnp.sin(fr * (mm(w3_ref[...], h) + b3_ref[...]))
    hf = mm(w4_ref[...], h)
    win = jnp.exp(-dl_ref[...] * f[64:65]) + DECAY_SHIFT
    k = (jnp.where(f[65:66] > 0.5, hf[:CONV_W], hf[CONV_W:]) * win * f[66:67]).T
    k_ref[...] = k

    @pl.when(pl.program_id(0) == 0)
    def _():
        s_ref[...] = jnp.zeros_like(s_ref)

    s_ref[...] += jnp.sum(jnp.abs(k), axis=0, keepdims=True)


def _implicit_filter(n, w1, b1, w2, b2, w3, b3, w4, freq):
    feats_t = jnp.asarray(np.ascontiguousarray(_filter_features(n).T))
    col = lambda a: a.reshape(-1, 1)
    w1t = jnp.pad(w1.T, ((0, 0), (0, LANES - FILT_EMB)))
    deltas = np.abs(np.linspace(math.log(DECAY_TARGET) / SLOW_DECAY_PCT, math.log(DECAY_TARGET) / FAST_DECAY_PCT,
                                CONV_W)).astype(np.float32).reshape(CONV_W, 1)
    tr = min(1024, 2 * n)
    sq = _full((FILT_WIDTH, FILT_WIDTH))
    vec = _full((FILT_WIDTH, 1))
    return pl.pallas_call(
        _filter_kernel,
        grid=(2 * n // tr,),
        in_specs=[pl.BlockSpec((LANES, tr), lambda i: (0, i)), _full((FILT_WIDTH, LANES)), vec, sq, vec, sq, vec,
                  _full((2 * CONV_W, FILT_WIDTH)), vec, _full((CONV_W, 1))],
        out_specs=[pl.BlockSpec((tr, CONV_W), lambda i: (i, 0)), _full((1, CONV_W))],
        out_shape=[jax.ShapeDtypeStruct((2 * n, CONV_W), F32), jax.ShapeDtypeStruct((1, CONV_W), F32)],
        compiler_params=_params(("arbitrary",)),
        name="hyena_filter",
    )(feats_t, w1t, col(b1), w2.T, col(b2), w3.T, col(b3), w4.T, col(freq), jnp.asarray(deltas))


def _twiddle(idx, mod):
    ang = 2.0 * np.pi * (idx % mod) / mod
    return np.cos(ang), -np.sin(ang)


def _real_form(mr, mi):
    return np.concatenate([np.concatenate([mr, -mi], -1), np.concatenate([mi, mr], -1)], -2)


@functools.lru_cache(maxsize=None)
def _dft_tables(n2):
    n1 = DFT_N1
    n = n1 * n2
    h = n2 // 2
    a2 = np.arange(n2)
    fr, fi = _twiddle(np.outer(a2, a2), n2)
    m_data = _real_form(fr[:, :h], fi[:, :h])
    m_filt = np.concatenate([fr, fi], 0)
    k2 = a2[:, None, None]
    k1 = np.arange(n1)[None, :, None]
    c1 = np.arange(n1)[None, None, :]
    g = _real_form(*_twiddle(c1 * (n2 * k1 + k2), n))
    a1 = np.arange(n1)
    f1 = _real_form(*_twiddle(np.outer(a1, a1), n1))
    t2 = a1[:, None, None]
    t1 = np.arange(h)[None, :, None]
    j1 = a2[None, None, :]
    hh = _real_form(*_twiddle(j1 * (n1 * t1 + t2), n))
    cast = lambda m: np.asarray(m, dtype=BF16)
    return cast(m_data), cast(m_filt), cast(g), cast(f1), cast(hh)


DFT_ROW_CHUNK = SUBLANES


def _dft_rows_kernel(m_ref, *refs):
    o_ref = refs[-1]
    cols = [jnp.concatenate([r[0, :, t, :] for r in refs[:-1]], axis=0) for t in range(DFT_ROW_CHUNK)]
    rhs = jnp.concatenate(cols, axis=1).astype(BF16)
    out = jnp.dot(m_ref[...], rhs, preferred_element_type=F32)
    half = out.shape[0] // 2
    o_ref[0, 0] = out[:half].astype(o_ref.dtype)
    o_ref[0, 1] = out[half:].astype(o_ref.dtype)


def _dft_stage1(mat, views, pairs, rows, n2, c):
    nin = len(views)
    lc = DFT_ROW_CHUNK * c
    in_specs = [_full(mat.shape)] + [
        pl.BlockSpec((1, rows, DFT_ROW_CHUNK, c), (lambda p, j, a=a: (nin * p + a, 0, j, 0))) for a in range(nin)]
    return pl.pallas_call(
        _dft_rows_kernel,
        grid=(pairs, DFT_N1 // DFT_ROW_CHUNK),
        in_specs=in_specs,
        out_specs=pl.BlockSpec((1, 2, n2, lc), lambda p, j: (p, 0, 0, j)),
        out_shape=jax.ShapeDtypeStruct((pairs, 2, n2, DFT_N1 * c), BF16),
        compiler_params=_params(("arbitrary", "arbitrary")),
        name="dft_stage1",
    )(mat, *views)


def _spectrum_kernel(a_ref, g_ref, sc_ref, o_ref, *, kb):
    for j in range(kb):
        rhs = jnp.concatenate([a_ref[0, 0, j], a_ref[0, 1, j]], axis=0)
        x = jnp.dot(g_ref[j], rhs, preferred_element_type=F32)
        o_ref[0, j] = (x[:DFT_N1] * sc_ref[...]).astype(o_ref.dtype)
        o_ref[1, j] = (x[DFT_N1:] * sc_ref[...]).astype(o_ref.dtype)


def _filter_spectrum(a, g, scale, n2, kb):
    c = a.shape[-1]
    return pl.pallas_call(
        functools.partial(_spectrum_kernel, kb=kb),
        grid=(n2 // kb,),
        in_specs=[pl.BlockSpec((1, 2, kb, DFT_N1, c), lambda k: (0, 0, k, 0, 0)),
                  pl.BlockSpec((kb, 2 * DFT_N1, 2 * DFT_N1), lambda k: (k, 0, 0)), _full((1, c))],
        out_specs=pl.BlockSpec((2, kb, DFT_N1, c), lambda k: (0, k, 0, 0)),
        out_shape=jax.ShapeDtypeStruct((2, n2, DFT_N1, c), BF16),
        compiler_params=_params(("arbitrary",)),
        name="filter_spectrum",
    )(a, g, scale)


def _dft_mid_kernel(a_ref, g_ref, kh_ref, f_ref, o_ref, *, kb):
    for j in range(kb):
        rhs = jnp.concatenate([a_ref[0, 0, j], a_ref[0, 1, j]], axis=0)
        x = jnp.dot(g_ref[j], rhs, preferred_element_type=F32)
        xr, xi = x[:DFT_N1], x[DFT_N1:]
        kr, ki = kh_ref[0, j].astype(F32), kh_ref[1, j].astype(F32)
        yr = xr * kr - xi * ki
        yi = xr * ki + xi * kr
        v = jnp.concatenate([yr, -yi], axis=0).astype(BF16)
        b = jnp.dot(f_ref[...], v, preferred_element_type=F32)
        o_ref[0, 0, :, j, :] = b[:DFT_N1]
        o_ref[0, 1, :, j, :] = b[DFT_N1:]


def _dft_mid(a, g, khat, f1, pairs, n2, kb):
    c = a.shape[-1]
    blk = pl.BlockSpec((1, 2, kb, DFT_N1, c), lambda k, p: (p, 0, k, 0, 0))
    return pl.pallas_call(
        functools.partial(_dft_mid_kernel, kb=kb),
        grid=(n2 // kb, pairs),
        in_specs=[blk, pl.BlockSpec((kb, 2 * DFT_N1, 2 * DFT_N1), lambda k, p: (k, 0, 0)),
                  pl.BlockSpec((2, kb, DFT_N1, c), lambda k, p: (0, k, 0, 0)), _full(f1.shape)],
        out_specs=pl.BlockSpec((1, 2, DFT_N1, kb, c), lambda k, p: (p, 0, 0, k, 0)),
        out_shape=jax.ShapeDtypeStruct((pairs, 2, DFT_N1, n2, c), F32),
        compiler_params=_params(("arbitrary", "arbitrary")),
        name="dft_mid",
    )(a, g, khat, f1)


def _dft_last_kernel(b_ref, h_ref, u_ref, x0_ref, bd_ref, o_ref, *, c):
    half = h_ref.shape[1] // 2
    bd = bd_ref[...]
    for t in range(DFT_ROW_CHUNK):
        rhs = jnp.concatenate([b_ref[0, 0, t], b_ref[0, 1, t]], axis=0).astype(BF16)
        v = jnp.dot(h_ref[t], rhs, preferred_element_type=F32)
        o_ref[0, :, t, :] = (v[:half] + u_ref[0, :, t, :] * bd) * x0_ref[0, :, t, :]
        o_ref[1, :, t, :] = (-v[half:] + u_ref[1, :, t, :] * bd) * x0_ref[1, :, t, :]


def _dft_last(bm, hh, u_view, x0_view, bias_d, pairs, n2, c):
    half = n2 // 2
    tc = DFT_ROW_CHUNK
    io = pl.BlockSpec((2, half, tc, c), lambda p, j: (p, 0, j, 0))
    return pl.pallas_call(
        functools.partial(_dft_last_kernel, c=c),
        grid=(pairs, DFT_N1 // tc),
        in_specs=[pl.BlockSpec((1, 2, tc, n2, c), lambda p, j: (p, 0, j, 0, 0)),
                  pl.BlockSpec((tc, n2, 2 * n2), lambda p, j: (j, 0, 0)), io, io, _full((1, c))],
        out_specs=io,
        out_shape=jax.ShapeDtypeStruct((2 * pairs, half, DFT_N1, c), F32),
        compiler_params=_params(("arbitrary", "arbitrary")),
        name="dft_last",
    )(bm, hh, u_view, x0_view, bias_d)


def _long_conv_mixer(u, x0, kfilt, ksum, bias_d):
    b, n, c = u.shape
    n2 = 2 * n // DFT_N1
    half = n2 // 2
    pairs = b // 2
    m_data, m_filt, g, f1, hh = (jnp.asarray(t) for t in _dft_tables(n2))
    kb = min(8, n2)
    scale = 1.0 / (ksum * float(DFT_N1 * n2))
    ka = _dft_stage1(m_filt, [kfilt.reshape(1, n2, DFT_N1, c)], 1, n2, n2, c)
    khat = _filter_spectrum(ka.reshape(1, 2, n2, DFT_N1, c), g, scale, n2, kb)
    u_view = u.reshape(b, half, DFT_N1, c)
    a = _dft_stage1(m_data, [u_view, u_view], pairs, half, n2, c)
    bm = _dft_mid(a.reshape(pairs, 2, n2, DFT_N1, c), g, khat, f1, pairs, n2, kb)
    out = _dft_last(bm, hh, u_view, x0.reshape(b, half, DFT_N1, c),
                    bias_d.reshape(1, c), pairs, n2, c)
    return out.reshape(b, n, c)


@functools.lru_cache(maxsize=None)
def _small_dft_tables(n):
    big = 2 * n
    a = np.arange(big)
    fr, fi = _twiddle(np.outer(a, a), big)
    cast = lambda m: np.asarray(m, dtype=BF16)
    return (cast(np.concatenate([fr, fi], 0)),
            cast(_real_form(fr[:, :n], fi[:, :n])),
            cast(_real_form(fr[:n], fi[:n])))


def _small_conv_kernel(k_ref, ks_ref, u_ref, x0_ref, bd_ref, mf_ref, md_ref, mi_ref, o_ref, *, n):
    big = 2 * n
    kh = jnp.dot(mf_ref[...], k_ref[...].astype(BF16), preferred_element_type=F32) * (1.0 / (ks_ref[...] * big))
    kr, ki = kh[:big], kh[big:]
    rhs = jnp.concatenate([u_ref[0], u_ref[1]], axis=0).astype(BF16)
    x = jnp.dot(md_ref[...], rhs, preferred_element_type=F32)
    xr, xi = x[:big], x[big:]
    v = jnp.concatenate([xr * kr - xi * ki, -(xr * ki + xi * kr)], axis=0).astype(BF16)
    y = jnp.dot(mi_ref[...], v, preferred_element_type=F32)
    bd = bd_ref[...]
    o_ref[0] = ((y[:n] + u_ref[0] * bd) * x0_ref[0]).astype(o_ref.dtype)
    o_ref[1] = ((-y[n:] + u_ref[1] * bd) * x0_ref[1]).astype(o_ref.dtype)


def _small_conv_mixer(u, x0, kfilt, ksum, bias_d):
    b, n, c = u.shape
    mf, md, mi = (jnp.asarray(t) for t in _small_dft_tables(n))
    io = pl.BlockSpec((2, n, c), lambda p: (p, 0, 0))
    return pl.pallas_call(
        functools.partial(_small_conv_kernel, n=n),
        grid=(b // 2,),
        in_specs=[_full((2 * n, c)), _full((1, c)), io, io, _full((1, c)),
                  _full(mf.shape), _full(md.shape), _full(mi.shape)],
        out_specs=io,
        out_shape=jax.ShapeDtypeStruct((b, n, c), BF16),
        compiler_params=_params(("arbitrary",)),
        name="small_conv",
    )(kfilt, ksum, u, x0, bias_d.reshape(1, c), mf, md, mi)


MXU_DIM = 256
FFN_CHUNK_EDGES = (0, 6 * MXU_DIM, D_FF)


def _gelu_tanh(x):
    return 0.5 * x * (1.0 + jnp.tanh(math.sqrt(2.0 / math.pi) * (x + 0.044715 * (x * x * x))))


def _ffn_kernel(x_ref, xp_ref, xn_ref, oa_ref, oap_ref, oan_ref, oc_ref, ocp_ref, ocn_ref, wa_ref, wc_ref, g1_ref,
                sh_ref, sc_ref, gt_ref, g_ref, wu_ref, cw_ref, cb_ref, wd_ref, o_ref):
    i = pl.program_id(1)
    nt = pl.num_programs(1)
    tm = x_ref.shape[1]

    def halo(p_ref, n_ref):
        return jnp.concatenate([p_ref[0][-SUBLANES:], n_ref[0][:SUBLANES]], axis=0)

    def mixer_out(oa, oc):
        return (jnp.dot(oa.astype(BF16), wa_ref[...], preferred_element_type=F32)
                + jnp.dot(oc.astype(BF16), wc_ref[...], preferred_element_type=F32))

    g1 = g1_ref[0]
    xm = x_ref[0] + g1 * mixer_out(oa_ref[0], oc_ref[0])
    xh = halo(xp_ref, xn_ref) + g1 * mixer_out(halo(oap_ref, oan_ref), halo(ocp_ref, ocn_ref))
    ext = jnp.concatenate([xh[:SUBLANES], xm, xh[SUBLANES:]], axis=0)
    r = tm + 2 * SUBLANES
    h = _modulated_norm(ext, g_ref[...], sh_ref[0], sc_ref[0])
    row = lax.broadcasted_iota(jnp.int32, (r, 1), 0)
    inside = jnp.logical_and(jnp.logical_or(i > 0, row >= SUBLANES), jnp.logical_or(i < nt - 1, row < tm + SUBLANES))
    h = jnp.where(inside, h, 0.0).astype(BF16)
    hm = h[SUBLANES:tm + SUBLANES]
    acc = jnp.zeros((tm, D_MODEL), F32)
    for c0, c1 in zip(FFN_CHUNK_EDGES[:-1], FFN_CHUNK_EDGES[1:]):
        a = jnp.dot(h, wu_ref[:, c0:c1], preferred_element_type=F32)
        v = jnp.dot(hm, wu_ref[:, D_FF + c0:D_FF + c1], preferred_element_type=F32)
        ap = pltpu.roll(a, 1, 0)[SUBLANES:tm + SUBLANES]
        an = pltpu.roll(a, r - 1, 0)[SUBLANES:tm + SUBLANES]
        cw = cw_ref[:, c0:c1]
        conv = ap * cw[0:1] + a[SUBLANES:tm + SUBLANES] * cw[1:2] + an * cw[2:3] + cb_ref[:, c0:c1]
        act = (_gelu_tanh(conv) * v).astype(BF16)
        acc = acc + jnp.dot(act, wd_ref[c0:c1, :], preferred_element_type=F32)
    o_ref[0] = xm + gt_ref[0] * acc


def _mixer_out_ffn(x, oa, oc, wo_bf, gate1, sh, sc, gate2, gain, wu_bf, conv_w, conv_b, wd_bf):
    bx, t, d = x.shape
    tm = min(512, t)
    vec = pl.BlockSpec((1, 1, d), lambda b, i: (b, 0, 0))
    once = pl.Buffered(1)
    return pl.pallas_call(
        _ffn_kernel,
        grid=(bx, t // tm),
        in_specs=_halo_specs(tm, t, d) + _halo_specs(tm, t, Q_W, oa.dtype) + _halo_specs(tm, t, CONV_W, oc.dtype) + [
            pl.BlockSpec((Q_W, d), lambda b, i: (0, 0)), pl.BlockSpec((CONV_W, d), lambda b, i: (1, 0)),
            vec, vec, vec, vec, _full((1, d)),
            pl.BlockSpec((d, 2 * D_FF), lambda b, i: (0, 0), pipeline_mode=once),
            _full((3, D_FF)), _full((1, D_FF)),
            pl.BlockSpec((D_FF, d), lambda b, i: (0, 0), pipeline_mode=once)],
        out_specs=pl.BlockSpec((1, tm, d), lambda b, i: (b, i, 0)),
        out_shape=jax.ShapeDtypeStruct((bx, t, d), F32),
        compiler_params=_params(("arbitrary", "arbitrary")),
        name="mixer_out_ffn",
    )(x, x, x, oa, oa, oa, oc, oc, oc, wo_bf, wo_bf, gate1, sh, sc, gate2, gain, wu_bf, conv_w,
      conv_b.reshape(1, D_FF), wd_bf)


@functools.lru_cache(maxsize=None)
def _rope_tables(t):
    pos = np.arange(t)
    n_freq = HEAD_DIM // 4
    inv = ROPE_THETA ** (-np.arange(n_freq, dtype=np.float64) / n_freq)
    ang_r = (pos // GRID_W)[:, None] * inv
    ang_c = (pos % GRID_W)[:, None] * inv
    zero = np.zeros_like(ang_r)
    cos = np.concatenate([np.cos(ang_r)] * 2 + [np.cos(ang_c)] * 2, 1)
    sa = np.concatenate([-np.sin(ang_r), zero, -np.sin(ang_c), zero], 1)
    sb = np.concatenate([zero, np.sin(ang_r), zero, np.sin(ang_c)], 1)
    tile = lambda m: np.tile(m, (1, LANES // HEAD_DIM)).astype(np.float32)
    return tile(cos), tile(sa), tile(sb)


@functools.lru_cache(maxsize=None)
def _identity_rope_tables(t):
    return np.ones((t, LANES), np.float32), np.zeros((t, LANES), np.float32), np.zeros((t, LANES), np.float32)


@functools.lru_cache(maxsize=None)
def _head_block_diag():
    hid = np.arange(Q_W) // HEAD_DIM
    return np.asarray(hid[:, None] == hid[None, :], dtype=BF16)


def kernel(x, c, ctx, c_ctx, ada_w, ada_b, norm_mix, norm_ffn, mix_w_in, mix_w_out, attn_q_norm, attn_k_norm,
           swa_sink, hy_conv_w, hy_conv_b, hy_w1, hy_b1, hy_w2, hy_b2, hy_w3, hy_b3, hy_w4, hy_freq, hy_bias_d,
           sc_conv_w, ffn_w_up, ffn_conv_w, ffn_conv_b, ffn_w_down):
    b, s, d = x.shape
    s_ctx = ctx.shape[1]
    depth = ada_w.shape[0]
    assert d == D_MODEL and b % 2 == 0 and b + 1 <= MOD_ROWS and s % 1024 == 0 and s_ctx % SUBLANES == 0

    cvec = jnp.concatenate([c, c_ctx[None, :], jnp.zeros((MOD_ROWS - b - 1, d), F32)], axis=0)
    mods = _mods(cvec, ada_w, ada_b)
    rope = [jnp.asarray(t) for t in _rope_tables(s)]
    rope_ctx = [jnp.asarray(t) for t in _identity_rope_tables(s_ctx)]
    bd = jnp.asarray(_head_block_diag())
    xc = ctx
    weights = _weights_bf16([mix_w_in, mix_w_out, ffn_w_up, ffn_w_down])

    for i in range(depth):
        last = i == depth - 1
        j = i // 2
        lat = [mods[i, :b, k * d:(k + 1) * d][:, None, :] for k in range(6)]
        cx = [jnp.broadcast_to(mods[i, b, k * d:(k + 1) * d][None, None, :], (b, 1, d)) for k in range(6)]
        w_in, w_out, w_up, w_down = (w[i] for w in weights)
        g_mix = norm_mix[i].reshape(1, d)
        g_ffn = norm_ffn[i].reshape(1, d)
        qg = jnp.tile(attn_q_norm[i], N_Q_HEADS).reshape(1, Q_W)
        kg = jnp.tile(attn_k_norm[i], N_KV_HEADS).reshape(1, KV_W)

        conv = (hy_conv_w[j], hy_conv_b[j]) if i % 2 == 0 else (sc_conv_w[j],)
        q, k, vt, *mix = _inproj(x, lat[0], lat[1], g_mix, w_in, qg, kg, bd, *rope, *conv)
        qc, kc, vct, *mixc = _inproj(xc, cx[0], cx[1], g_mix, w_in, qg, kg, bd, *rope_ctx, *conv)

        bound = _score_bound(attn_q_norm[i], attn_k_norm[i])
        if i % 2 == 0:
            fargs = (hy_w1[j], hy_b1[j], hy_w2[j], hy_b2[j], hy_w3[j], hy_b3[j], hy_w4[j], hy_freq[j])
            o_attn = _dense_attention(q, k, vt, kc, vct, bound=bound)
            kf, ks = _implicit_filter(s, *fargs)
            o_conv = _long_conv_mixer(*mix, kf, ks, hy_bias_d[j])
            if not last:
                oc_attn = _dense_attention(qc, kc, vct, bound=bound)
                kfc, ksc = _implicit_filter(s_ctx, *fargs)
                oc_conv = _small_conv_mixer(*mixc, kfc, ksc, hy_bias_d[j])
        else:
            sink = swa_sink[j].reshape(N_KV_HEADS, GROUP)
            o_attn = _banded_attention(q, k, vt, kc, vct, sink, bound)
            o_conv = mix[0]
            if not last:
                oc_attn = _dense_attention(qc, kc, vct, sink=sink)
                oc_conv = mixc[0]

        ffn = (g_ffn, w_up, ffn_conv_w[i], ffn_conv_b[i], w_down)
        x = _mixer_out_ffn(x, o_attn, o_conv, w_out, lat[2], lat[3], lat[4], lat[5], *ffn)
        if not last:
            xc = _mixer_out_ffn(xc, oc_attn, oc_conv, w_out, cx[2], cx[3], cx[4], cx[5], *ffn)
    return x
```

```python
import functools
import math

import numpy as np
import jax
import jax.numpy as jnp
from jax import lax
from jax.experimental import pallas as pl
from jax.experimental.pallas import tpu as pltpu

F32 = jnp.float32
BF16 = jnp.bfloat16
HI = lax.Precision.HIGHEST

D_MODEL = 1024
GRID_W = 64
HEAD_DIM = 64
N_Q_HEADS = 8
N_KV_HEADS = 4
GROUP = N_Q_HEADS // N_KV_HEADS
Q_W = N_Q_HEADS * HEAD_DIM
KV_W = N_KV_HEADS * HEAD_DIM
QKV_W = Q_W + 2 * KV_W
CONV_W = D_MODEL // 2
MIX_IN_W = QKV_W + 3 * CONV_W
WINDOW = 128
ROPE_THETA = 10000.0
FILT_EMB = 33
FILT_WIDTH = 64
DECAY_TARGET = 1e-2
FAST_DECAY_PCT = 0.3
SLOW_DECAY_PCT = 1.5
DECAY_SHIFT = 0.05
D_FF = 2816
NEG_INF = -1e30
RMS_EPS = 1e-6
LOG2E = 1.4426950408889634
Q_SCALE = HEAD_DIM ** -0.5 * LOG2E

LANES = 128
SUBLANES = 8
DFT_N1 = 128
VMEM_LIMIT_MB = 56
VT_ROWS = LANES
MOD_ROWS = SUBLANES
PROJ_SUB_ROWS = 128


def _params(sem, vmem_mb=VMEM_LIMIT_MB):
    return pltpu.CompilerParams(dimension_semantics=sem, vmem_limit_bytes=vmem_mb * 1024 * 1024)


def _full(shape):
    nd = len(shape)
    return pl.BlockSpec(shape, lambda *_: (0,) * nd)


def _mods_kernel(c_ref, w_ref, b_ref, o_ref):
    c = c_ref[...]
    s = c / (1.0 + jnp.exp(-c))
    o_ref[0] = jnp.dot(s, w_ref[0], preferred_element_type=F32, precision=HI) + b_ref[0]


def _mods(cvec, ada_w, ada_b):
    depth, d, n6 = ada_w.shape
    tn = 1536
    return pl.pallas_call(
        _mods_kernel,
        grid=(depth, n6 // tn),
        in_specs=[_full((MOD_ROWS, d)),
                  pl.BlockSpec((1, d, tn), lambda l, j: (l, 0, j)),
                  pl.BlockSpec((1, 1, tn), lambda l, j: (l, 0, j))],
        out_specs=pl.BlockSpec((1, MOD_ROWS, tn), lambda l, j: (l, 0, j)),
        out_shape=jax.ShapeDtypeStruct((depth, MOD_ROWS, n6), F32),
        compiler_params=_params(("arbitrary", "arbitrary")),
        name="ada_mods",
    )(cvec, ada_w, ada_b.reshape(depth, 1, n6))


def _modulated_norm(x, gain, shift, scale):
    ms = jnp.mean(x * x, axis=-1, keepdims=True)
    return (x * lax.rsqrt(ms + RMS_EPS)) * gain * (1.0 + scale) + shift


WEIGHT_CAST_STEPS = 8


def _cast_kernel(*refs):
    n = len(refs) // 2
    for w_ref, o_ref in zip(refs[:n], refs[n:]):
        o_ref[...] = w_ref[0].astype(o_ref.dtype)


def _weights_bf16(stacked):
    depth = stacked[0].shape[0]
    in_specs, out_specs, out_shape, args = [], [], [], []
    for w in stacked:
        _, rows, cols = w.shape
        tr = rows // WEIGHT_CAST_STEPS
        for layer in range(depth):
            in_specs.append(pl.BlockSpec((1, tr, cols), lambda r, layer=layer: (layer, r, 0)))
            out_specs.append(pl.BlockSpec((tr, cols), lambda r: (r, 0)))
            out_shape.append(jax.ShapeDtypeStruct((rows, cols), BF16))
            args.append(w)
    outs = pl.pallas_call(
        _cast_kernel,
        grid=(WEIGHT_CAST_STEPS,),
        in_specs=in_specs,
        out_specs=out_specs,
        out_shape=out_shape,
        compiler_params=_params(("arbitrary",)),
        name="weights_bf16",
    )(*args)
    return [outs[k * depth:(k + 1) * depth] for k in range(len(stacked))]


def _halo_specs(tm, t, width, dtype=F32):
    rows = SUBLANES * 4 // jnp.dtype(dtype).itemsize
    nb = tm // rows
    last = t // rows - 1
    return [pl.BlockSpec((1, tm, width), lambda b, i: (b, i, 0)),
            pl.BlockSpec((1, rows, width), lambda b, i: (b, jnp.maximum(i * nb - 1, 0), 0)),
            pl.BlockSpec((1, rows, width), lambda b, i: (b, jnp.minimum((i + 1) * nb, last), 0))]


def _inproj_kernel(*refs, mode):
    (x_ref, xp_ref, xn_ref, sh_ref, sc_ref, g_ref, w_ref, qg_ref, kg_ref, bd_ref,
     cos_ref, sa_ref, sb_ref, cw_ref) = refs[:14]
    if mode == "hyena":
        cb_ref, q_ref, k_ref, vt_ref, u_ref, x0_ref = refs[14:]
    else:
        q_ref, k_ref, vt_ref, oc_ref = refs[14:]
    i = pl.program_id(1)
    nt = pl.num_programs(1)
    tm = x_ref.shape[1]
    sub = min(PROJ_SUB_ROWS, tm)
    gain, shift, scale = g_ref[...], sh_ref[0], sc_ref[0]

    xh = jnp.concatenate([xp_ref[0], xn_ref[0]], axis=0)
    zh = jnp.dot(_modulated_norm(xh, gain, shift, scale).astype(BF16), w_ref[:, QKV_W:], preferred_element_type=F32)
    before = jnp.where(i > 0, zh[:SUBLANES], 0.0)
    after = jnp.where(i < nt - 1, zh[SUBLANES:], 0.0)
    cw = cw_ref[...]

    def conv_stage(ext, rows):
        n = sub + 2 * SUBLANES
        mid = slice(SUBLANES, sub + SUBLANES)
        if mode == "hyena":
            c = (pltpu.roll(ext, 1, 0)[mid] * cw[0:1] + ext[mid] * cw[1:2] + pltpu.roll(ext, n - 1, 0)[mid] * cw[2:3]
                 + cb_ref[...])
            x0_ref[0, rows] = c[:, :CONV_W]
            u_ref[0, rows] = c[:, 2 * CONV_W:] * c[:, CONV_W:2 * CONV_W]
        else:
            pr = ext[:, CONV_W:2 * CONV_W] * ext[:, 2 * CONV_W:]
            conv = pltpu.roll(pr, 1, 0)[mid] * cw[0:1] + pr[mid] * cw[1:2] + pltpu.roll(pr, n - 1, 0)[mid] * cw[2:3]
            oc_ref[0, rows] = (ext[mid, :CONV_W] * conv).astype(oc_ref.dtype)

    def head_norm_rope(t, hgain, bd, cos, sa, sb):
        ssq = jnp.dot((t * t).astype(BF16), bd, preferred_element_type=F32)
        t = t * lax.rsqrt(ssq * (1.0 / HEAD_DIM) + RMS_EPS) * hgain
        outs = []
        for j in range(t.shape[1] // LANES):
            tj = t[:, j * LANES:(j + 1) * LANES]
            outs.append(tj * cos + pltpu.roll(tj, LANES - 16, 1) * sa + pltpu.roll(tj, 16, 1) * sb)
        return outs

    held = None
    for r0 in range(0, tm, sub):
        rows = slice(r0, r0 + sub)
        h = _modulated_norm(x_ref[0, rows], gain, shift, scale)
        hb = h.astype(BF16)
        z = jnp.dot(hb, w_ref[:, :QKV_W], preferred_element_type=F32)
        zc = jnp.dot(hb, w_ref[:, QKV_W:], preferred_element_type=F32)
        if held is not None:
            conv_stage(jnp.concatenate([before, held[0], zc[:SUBLANES]], axis=0), held[1])
            before = held[0][-SUBLANES:]
        held = (zc, rows)
        tabs = (cos_ref[rows], sa_ref[rows], sb_ref[rows])
        qs = head_norm_rope(z[:, :Q_W], qg_ref[...], bd_ref[...], *tabs)
        ks = head_norm_rope(z[:, Q_W:Q_W + KV_W], kg_ref[...], bd_ref[:KV_W, :KV_W], *tabs)
        for j, qj in enumerate(qs):
            qj = (qj * Q_SCALE).astype(BF16)
            q_ref[0, 2 * j, rows] = qj[:, :HEAD_DIM]
            q_ref[0, 2 * j + 1, rows] = qj[:, HEAD_DIM:]
        for j, kj in enumerate(ks):
            kj = kj.astype(BF16)
            k_ref[0, 2 * j, rows] = kj[:, :HEAD_DIM]
            k_ref[0, 2 * j + 1, rows] = kj[:, HEAD_DIM:]
        vt = z[:, Q_W + KV_W:QKV_W].T
        ones_row = (lax.broadcasted_iota(jnp.int32, (VT_ROWS - HEAD_DIM, sub), 0) == 0).astype(F32)
        for hh in range(N_KV_HEADS):
            vt_ref[0, hh, :, rows] = jnp.concatenate([vt[hh * HEAD_DIM:(hh + 1) * HEAD_DIM], ones_row],
                                                     axis=0).astype(BF16)

    conv_stage(jnp.concatenate([before, held[0], after], axis=0), held[1])


def _inproj(x, sh, sc, gain, w_bf, qg, kg, bd, cos, sa, sb, conv_w, conv_b=None):
    bx, t, d = x.shape
    tm = min(512, t)
    mode = "hyena" if conv_b is not None else "short"
    vec = pl.BlockSpec((1, 1, d), lambda b, i: (b, 0, 0))
    tab = pl.BlockSpec((tm, LANES), lambda b, i: (i, 0))
    row = lambda w: pl.BlockSpec((1, tm, w), lambda b, i: (b, i, 0))
    in_specs = _halo_specs(tm, t, d) + [vec, vec, _full((1, d)), _full((d, MIX_IN_W)), _full((1, Q_W)),
                                        _full((1, KV_W)), _full((Q_W, Q_W)), tab, tab, tab, _full(conv_w.shape)]
    args = [x, x, x, sh, sc, gain, w_bf, qg, kg, bd, cos, sa, sb, conv_w]
    out_specs = [pl.BlockSpec((1, N_Q_HEADS, tm, HEAD_DIM), lambda b, i: (b, 0, i, 0)),
                 pl.BlockSpec((1, N_KV_HEADS, tm, HEAD_DIM), lambda b, i: (b, 0, i, 0)),
                 pl.BlockSpec((1, N_KV_HEADS, VT_ROWS, tm), lambda b, i: (b, 0, 0, i))]
    out_shape = [jax.ShapeDtypeStruct((bx, N_Q_HEADS, t, HEAD_DIM), BF16),
                 jax.ShapeDtypeStruct((bx, N_KV_HEADS, t, HEAD_DIM), BF16),
                 jax.ShapeDtypeStruct((bx, N_KV_HEADS, VT_ROWS, t), BF16)]
    if mode == "hyena":
        in_specs.append(_full((1, 3 * CONV_W)))
        args.append(conv_b.reshape(1, 3 * CONV_W))
        out_specs += [row(CONV_W), row(CONV_W)]
        out_shape += [jax.ShapeDtypeStruct((bx, t, CONV_W), F32)] * 2
    else:
        out_specs.append(row(CONV_W))
        out_shape.append(jax.ShapeDtypeStruct((bx, t, CONV_W), BF16))
    return pl.pallas_call(
        functools.partial(_inproj_kernel, mode=mode),
        grid=(bx, t // tm),
        in_specs=in_specs,
        out_specs=out_specs,
        out_shape=out_shape,
        compiler_params=_params(("arbitrary", "arbitrary")),
        name="in_proj",
    )(*args)


def _score_bound(q_gain, k_gain):
    b = HEAD_DIM * Q_SCALE * BOUND_MARGIN * jnp.max(jnp.abs(q_gain)) * jnp.max(jnp.abs(k_gain))
    return b.reshape(1, 1).astype(F32)


def _scores_t(ks, q):
    return lax.dot_general(ks, q, (((1,), (1,)), ((), ())), preferred_element_type=F32)


def _sink_column(sink_ref, hh, tq, m, acc):
    col = lax.broadcasted_iota(jnp.int32, m.shape, 1)
    sink = jnp.where(col < tq, sink_ref[hh, 0], sink_ref[hh, 1]) * LOG2E
    m_new = jnp.maximum(m, sink)
    den_row = lax.broadcasted_iota(jnp.int32, acc.shape, 0) == HEAD_DIM
    return acc * jnp.exp2(m - m_new) + jnp.where(den_row, jnp.exp2(sink - m_new), 0.0)


def _attn_finish(acc, tq):
    o = (acc / acc[HEAD_DIM:HEAD_DIM + 1]).T
    return jnp.concatenate([o[:tq, :HEAD_DIM], o[tq:, :HEAD_DIM]], axis=1)


EXP_CHUNK_ELEMS = 32 * SUBLANES * LANES


def _score_stage(ks, q, s_ref, bias=None):
    s = _scores_t(ks, q)
    if bias is not None:
        s = s + bias
    s_ref[0:s.shape[0]] = s
    return jnp.max(s, axis=0, keepdims=True)


def _softmax_stage(s_ref, p_ref, n, vt, mx, m, acc_ref):
    m_new = jnp.maximum(m, mx)
    rows = EXP_CHUNK_ELEMS // m.shape[1]
    for c in range(0, n, rows):
        p_ref[c:c + rows] = jnp.exp2(s_ref[c:c + rows] - m_new).astype(BF16)
    acc_ref[...] = acc_ref[...] * jnp.exp2(m - m_new) + jnp.dot(vt, p_ref[0:n], preferred_element_type=F32)
    return m_new


BOUNDED_TILES = 8
BOUND_MARGIN = 1.01
SAFE_SHIFT = 40.0


def _flash_exact(q, k_tile, vt_tile, ctx, sa_ref, sb_ref, p_ref, acc_ref, *, tk, n_main):
    pa_ref, pb_ref = p_ref.at[0:tk], p_ref.at[tk:2 * tk]
    acc_ref[...] = jnp.zeros_like(acc_ref)
    m = jnp.full((1, q.shape[0]), NEG_INF, F32)
    mx_a = _score_stage(k_tile(0), q, sa_ref)
    if n_main > 1:
        def body(i, carry):
            m, mx_a = carry
            mx_b = _score_stage(k_tile(2 * i + 1), q, sb_ref)
            m = _softmax_stage(sa_ref, pa_ref, tk, vt_tile(2 * i), mx_a, m, acc_ref)
            mx_a = _score_stage(k_tile(2 * i + 2), q, sa_ref)
            m = _softmax_stage(sb_ref, pb_ref, tk, vt_tile(2 * i + 1), mx_b, m, acc_ref)
            return m, mx_a

        m, mx_a = lax.fori_loop(0, n_main // 2 - 1, body, (m, mx_a))
        mx_b = _score_stage(k_tile(n_main - 1), q, sb_ref)
        m = _softmax_stage(sa_ref, pa_ref, tk, vt_tile(n_main - 2), mx_a, m, acc_ref)
        last = (sb_ref, pb_ref, mx_b)
        spare = (sa_ref, pa_ref)
    else:
        last = (sa_ref, pa_ref, mx_a)
        spare = (sb_ref, pb_ref)
    if ctx is not None:
        mx_c = _score_stage(ctx[0], q, spare[0])
    m = _softmax_stage(last[0], last[1], tk, vt_tile(n_main - 1), last[2], m, acc_ref)
    if ctx is not None:
        m = _softmax_stage(spare[0], spare[1], ctx[0].shape[0], ctx[1], mx_c, m, acc_ref)
    return m


def _exp2_bf16(s):
    return jnp.exp2(s).astype(BF16)


def _flash_bounded(q, k_ref, vt_ref, ctx, p_ref, acc_ref, *, tk, n_main):
    step = BOUNDED_TILES * tk if n_main % BOUNDED_TILES == 0 else tk

    def group(j):
        off = pl.multiple_of(j * step, step)
        for c in range(0, step, tk):
            p_ref[c:c + tk] = _exp2_bf16(_scores_t(k_ref[0, 0, pl.ds(off + c, tk), :], q))
        return jnp.dot(vt_ref[0, 0, :, pl.ds(off, step)], p_ref[0:step], preferred_element_type=F32)

    first = group(0)
    if ctx is not None:
        pc_ref = p_ref.at[BOUNDED_TILES * tk:BOUNDED_TILES * tk + ctx[0].shape[0]]
        pc_ref[...] = _exp2_bf16(_scores_t(ctx[0], q))
        first = first + jnp.dot(ctx[1], pc_ref[...], preferred_element_type=F32)
    acc_ref[...] = first

    def body(j, carry):
        acc_ref[...] += group(j)
        return carry

    lax.fori_loop(1, n_main * tk // step, body, 0)


def _flash_kernel(*refs, tq, tk, n_main, has_ctx, has_sink):
    sa_ref, sb_ref, p_ref, acc_ref = refs[-4:]
    o_ref = refs[-5]
    refs = list(refs[:-5])
    sink_ref = refs.pop(0) if has_sink else None
    bound_ref = None if has_sink else refs.pop(0)
    q_ref, k_ref, vt_ref = refs[:3]
    m_cols = GROUP * tq
    q = q_ref[0].reshape(m_cols, HEAD_DIM)
    ctx = (refs[3][0, 0], refs[4][0, 0]) if has_ctx else None

    def k_tile(j):
        return k_ref[0, 0, pl.ds(pl.multiple_of(j * tk, tk), tk), :]

    def vt_tile(j):
        return vt_ref[0, 0, :, pl.ds(pl.multiple_of(j * tk, tk), tk)]

    exact = functools.partial(_flash_exact, q, k_tile, vt_tile, ctx, sa_ref, sb_ref, p_ref, acc_ref,
                              tk=tk, n_main=n_main)
    if has_sink:
        acc = _sink_column(sink_ref, pl.program_id(1), tq, exact(), acc_ref[...])
    else:
        bound = bound_ref[0, 0]
        safe = bound <= SAFE_SHIFT

        @pl.when(safe)
        def _():
            _flash_bounded(q, k_ref, vt_ref, ctx, p_ref, acc_ref, tk=tk, n_main=n_main)

        @pl.when(jnp.logical_not(safe))
        def _():
            exact()

        acc = acc_ref[...]
    o_ref[0] = _attn_finish(acc, tq).astype(o_ref.dtype)


def _kv_specs(n):
    return [pl.BlockSpec((1, 1, n, HEAD_DIM), lambda b, h, i: (b, h, 0, 0)),
            pl.BlockSpec((1, 1, VT_ROWS, n), lambda b, h, i: (b, h, 0, 0))]


def _dense_attention(q, k, vt, kc=None, vct=None, *, bound=None, sink=None):
    bx, _, t, _ = q.shape
    s = k.shape[2]
    tq = min(512, t)
    tk = min(512, s)
    has_ctx = kc is not None
    has_sink = sink is not None
    in_specs = [pl.BlockSpec((1, GROUP, tq, HEAD_DIM), lambda b, h, i: (b, h, i, 0))] + _kv_specs(s)
    args = [q, k, vt]
    if has_ctx:
        in_specs += _kv_specs(kc.shape[2])
        args += [kc, vct]
    in_specs = [pl.BlockSpec(memory_space=pltpu.SMEM)] + in_specs
    args = [sink if has_sink else bound] + args
    n_main = s // tk
    assert n_main == 1 or n_main % 2 == 0
    n_ctx = kc.shape[2] if has_ctx else 0
    assert n_ctx <= tk
    m_cols = GROUP * tq
    return pl.pallas_call(
        functools.partial(_flash_kernel, tq=tq, tk=tk, n_main=n_main, has_ctx=has_ctx, has_sink=has_sink),
        grid=(bx, N_KV_HEADS, t // tq),
        in_specs=in_specs,
        out_specs=pl.BlockSpec((1, tq, LANES), lambda b, h, i: (b, i, h)),
        out_shape=jax.ShapeDtypeStruct((bx, t, Q_W), BF16),
        scratch_shapes=[pltpu.VMEM((tk, m_cols), F32), pltpu.VMEM((tk, m_cols), F32),
                        pltpu.VMEM((BOUNDED_TILES * tk + n_ctx, m_cols), BF16), pltpu.VMEM((VT_ROWS, m_cols), F32)],
        compiler_params=_params(("arbitrary", "arbitrary", "arbitrary")),
        name="dense_attention",
    )(*args)


BAND_SUB = 256


@functools.lru_cache(maxsize=None)
def _band_bias():
    span = BAND_SUB + 2 * WINDOW
    kr = np.arange(span)[None, :, None]
    qc = (np.arange(GROUP * BAND_SUB) % BAND_SUB)[None, None, :]
    rel = np.arange(3)[:, None, None]
    return np.where(np.abs(kr - qc - rel * WINDOW) <= WINDOW, 0.0, NEG_INF).astype(np.float32)


def _banded_kernel(sink_ref, bound_ref, q_ref, k_ref, vt_ref, kc_ref, vct_ref, bias_ref, o_ref,
                   sl_ref, sc_ref, pl_ref, pc_ref, acc_ref, *, n_sub, s_len):
    i = pl.program_id(2)
    hh = pl.program_id(1)
    sub = BAND_SUB
    m_cols = GROUP * sub
    span = sub + 2 * WINDOW
    n_ctx = kc_ref.shape[2]

    def window(u):
        q = q_ref[0, :, u * sub:(u + 1) * sub, :].reshape(m_cols, HEAD_DIM)
        q0 = (i * n_sub + u) * sub
        start = pl.multiple_of(jnp.clip(q0 - WINDOW, 0, s_len - span), WINDOW)
        return q, start, bias_ref[(q0 - start) // WINDOW]

    shift = jnp.maximum(bound_ref[0, 0], jnp.maximum(sink_ref[hh, 0], sink_ref[hh, 1]) * LOG2E)
    safe = shift <= SAFE_SHIFT

    @pl.when(safe)
    def _():
        col = lax.broadcasted_iota(jnp.int32, (1, m_cols), 1)
        sink = jnp.where(col < sub, sink_ref[hh, 0], sink_ref[hh, 1]) * LOG2E
        den_row = lax.broadcasted_iota(jnp.int32, (VT_ROWS, m_cols), 0) == HEAD_DIM
        sink_den = jnp.where(den_row, jnp.exp2(sink), 0.0)
        for u in range(n_sub):
            q, start, bias = window(u)
            pl_u, pc_u = pl_ref.at[u], pc_ref.at[u]
            pl_u[...] = _exp2_bf16(_scores_t(k_ref[0, 0, pl.ds(start, span), :], q) + bias)
            pc_u[...] = _exp2_bf16(_scores_t(kc_ref[0, 0], q))
            acc = (jnp.dot(vt_ref[0, 0, :, pl.ds(start, span)], pl_u[...], preferred_element_type=F32)
                   + jnp.dot(vct_ref[0, 0], pc_u[...], preferred_element_type=F32) + sink_den)
            o_ref[0, u * sub:(u + 1) * sub] = _attn_finish(acc, sub).astype(o_ref.dtype)

    @pl.when(jnp.logical_not(safe))
    def _():
        stats = []
        for u in range(n_sub):
            q, start, bias = window(u)
            mx_l = _score_stage(k_ref[0, 0, pl.ds(start, span), :], q, sl_ref.at[u], bias=bias)
            mx_c = _score_stage(kc_ref[0, 0], q, sc_ref.at[u])
            stats.append((start, mx_l, mx_c))
        for u in range(n_sub):
            start, mx_l, mx_c = stats[u]
            acc_u = acc_ref.at[u]
            acc_u[...] = jnp.zeros((VT_ROWS, m_cols), F32)
            m = jnp.full((1, m_cols), NEG_INF, F32)
            m = _softmax_stage(sl_ref.at[u], pl_ref.at[u], span, vt_ref[0, 0, :, pl.ds(start, span)], mx_l, m, acc_u)
            m = _softmax_stage(sc_ref.at[u], pc_ref.at[u], n_ctx, vct_ref[0, 0], mx_c, m, acc_u)
            acc = _sink_column(sink_ref, hh, sub, m, acc_u[...])
            o_ref[0, u * sub:(u + 1) * sub] = _attn_finish(acc, sub).astype(o_ref.dtype)


def _banded_attention(q, k, vt, kc, vct, sink, bound):
    bx, _, t, _ = q.shape
    n_ctx = kc.shape[2]
    n_sub = next(k for k in (8, 4, 2, 1) if t % (k * BAND_SUB) == 0)
    tq = n_sub * BAND_SUB
    m_cols = GROUP * BAND_SUB
    span = BAND_SUB + 2 * WINDOW
    bias = jnp.asarray(_band_bias())
    return pl.pallas_call(
        functools.partial(_banded_kernel, n_sub=n_sub, s_len=t),
        grid=(bx, N_KV_HEADS, t // tq),
        in_specs=[pl.BlockSpec(memory_space=pltpu.SMEM), pl.BlockSpec(memory_space=pltpu.SMEM),
                  pl.BlockSpec((1, GROUP, tq, HEAD_DIM), lambda b, h, i: (b, h, i, 0))]
                 + _kv_specs(t) + _kv_specs(n_ctx) + [_full(bias.shape)],
        out_specs=pl.BlockSpec((1, tq, LANES), lambda b, h, i: (b, i, h)),
        out_shape=jax.ShapeDtypeStruct((bx, t, Q_W), BF16),
        scratch_shapes=[pltpu.VMEM((n_sub, span, m_cols), F32), pltpu.VMEM((n_sub, n_ctx, m_cols), F32),
                        pltpu.VMEM((n_sub, span, m_cols), BF16), pltpu.VMEM((n_sub, n_ctx, m_cols), BF16),
                        pltpu.VMEM((n_sub, VT_ROWS, m_cols), F32)],
        compiler_params=_params(("arbitrary", "arbitrary", "arbitrary")),
        name="banded_attention",
    )(sink, bound, q, k, vt, kc, vct, bias)


@functools.lru_cache(maxsize=None)
def _filter_features(n):
    j = np.arange(2 * n)
    d = np.where(j <= n, j, 2 * n - j)
    d = np.where(j == n, 0, d)
    bands = (FILT_EMB - 1) // 2
    t01 = np.linspace(0.0, 1.0, n)[d]
    w = 2.0 * np.pi * d.astype(np.float64) / n
    f = np.linspace(1e-4, bands - 1, bands)[None, :]
    feats = np.zeros((2 * n, LANES), np.float64)
    feats[:, 0] = t01
    feats[:, 1:1 + bands] = np.cos(f * w[:, None])
    feats[:, 1 + bands:FILT_EMB] = -np.sin(f * w[:, None])
    feats[:, 64] = t01
    feats[:, 65] = (j < n)
    feats[:, 66] = (j != n)
    return feats.astype(np.float32)


def _filter_kernel(f_ref, w1_ref, b1_ref, w2_ref, b2_ref, w3_ref, b3_ref, w4_ref, fr_ref, dl_ref, k_ref, s_ref):
    f = f_ref[...]
    fr = fr_ref[...]
    mm = lambda a, b: jnp.dot(a, b, preferred_element_type=F32, precision=HI)
    h = jnp.sin(fr * (mm(w1_ref[...], f) + b1_ref[...]))
    h = jnp.sin(fr * (mm(w2_ref[...], h) + b2_ref[...]))
    h = jnp.sin(fr * (mm(w3_ref[...], h) + b3_ref[...]))
    hf = mm(w4_ref[...], h)
    win = jnp.exp(-dl_ref[...] * f[64:65]) + DECAY_SHIFT
    k = (jnp.where(f[65:66] > 0.5, hf[:CONV_W], hf[CONV_W:]) * win * f[66:67]).T
    k_ref[...] = k

    @pl.when(pl.program_id(0) == 0)
    def _():
        s_ref[...] = jnp.zeros_like(s_ref)

    s_ref[...] += jnp.sum(jnp.abs(k), axis=0, keepdims=True)


def _implicit_filter(n, w1, b1, w2, b2, w3, b3, w4, freq):
    feats_t = jnp.asarray(np.ascontiguousarray(_filter_features(n).T))
    col = lambda a: a.reshape(-1, 1)
    w1t = jnp.pad(w1.T, ((0, 0), (0, LANES - FILT_EMB)))
    deltas = np.abs(np.linspace(math.log(DECAY_TARGET) / SLOW_DECAY_PCT, math.log(DECAY_TARGET) / FAST_DECAY_PCT,
                                CONV_W)).astype(np.float32).reshape(CONV_W, 1)
    tr = min(1024, 2 * n)
    sq = _full((FILT_WIDTH, FILT_WIDTH))
    vec = _full((FILT_WIDTH, 1))
    return pl.pallas_call(
        _filter_kernel,
        grid=(2 * n // tr,),
        in_specs=[pl.BlockSpec((LANES, tr), lambda i: (0, i)), _full((FILT_WIDTH, LANES)), vec, sq, vec, sq, vec,
                  _full((2 * CONV_W, FILT_WIDTH)), vec, _full((CONV_W, 1))],
        out_specs=[pl.BlockSpec((tr, CONV_W), lambda i: (i, 0)), _full((1, CONV_W))],
        out_shape=[jax.ShapeDtypeStruct((2 * n, CONV_W), F32), jax.ShapeDtypeStruct((1, CONV_W), F32)],
        compiler_params=_params(("arbitrary",)),
        name="hyena_filter",
    )(feats_t, w1t, col(b1), w2.T, col(b2), w3.T, col(b3), w4.T, col(freq), jnp.asarray(deltas))


def _twiddle(idx, mod):
    ang = 2.0 * np.pi * (idx % mod) / mod
    return np.cos(ang), -np.sin(ang)


def _real_form(mr, mi):
    return np.concatenate([np.concatenate([mr, -mi], -1), np.concatenate([mi, mr], -1)], -2)


@functools.lru_cache(maxsize=None)
def _dft_tables(n2):
    n1 = DFT_N1
    n = n1 * n2
    h = n2 // 2
    a2 = np.arange(n2)
    fr, fi = _twiddle(np.outer(a2, a2), n2)
    m_data = _real_form(fr[:, :h], fi[:, :h])
    m_filt = np.concatenate([fr, fi], 0)
    k2 = a2[:, None, None]
    k1 = np.arange(n1)[None, :, None]
    c1 = np.arange(n1)[None, None, :]
    g = _real_form(*_twiddle(c1 * (n2 * k1 + k2), n))
    a1 = np.arange(n1)
    f1 = _real_form(*_twiddle(np.outer(a1, a1), n1))
    t2 = a1[:, None, None]
    t1 = np.arange(h)[None, :, None]
    j1 = a2[None, None, :]
    hh = _real_form(*_twiddle(j1 * (n1 * t1 + t2), n))
    cast = lambda m: np.asarray(m, dtype=BF16)
    return cast(m_data), cast(m_filt), cast(g), cast(f1), cast(hh)


DFT_ROW_CHUNK = SUBLANES


def _dft_rows_kernel(m_ref, *refs):
    o_ref = refs[-1]
    cols = [jnp.concatenate([r[0, :, t, :] for r in refs[:-1]], axis=0) for t in range(DFT_ROW_CHUNK)]
    rhs = jnp.concatenate(cols, axis=1).astype(BF16)
    out = jnp.dot(m_ref[...], rhs, preferred_element_type=F32)
    half = out.shape[0] // 2
    o_ref[0, 0] = out[:half].astype(o_ref.dtype)
    o_ref[0, 1] = out[half:].astype(o_ref.dtype)


def _dft_stage1(mat, views, pairs, rows, n2, c):
    nin = len(views)
    lc = DFT_ROW_CHUNK * c
    in_specs = [_full(mat.shape)] + [
        pl.BlockSpec((1, rows, DFT_ROW_CHUNK, c), (lambda p, j, a=a: (nin * p + a, 0, j, 0))) for a in range(nin)]
    return pl.pallas_call(
        _dft_rows_kernel,
        grid=(pairs, DFT_N1 // DFT_ROW_CHUNK),
        in_specs=in_specs,
        out_specs=pl.BlockSpec((1, 2, n2, lc), lambda p, j: (p, 0, 0, j)),
        out_shape=jax.ShapeDtypeStruct((pairs, 2, n2, DFT_N1 * c), BF16),
        compiler_params=_params(("arbitrary", "arbitrary")),
        name="dft_stage1",
    )(mat, *views)


def _spectrum_kernel(a_ref, g_ref, sc_ref, o_ref, *, kb):
    for j in range(kb):
        rhs = jnp.concatenate([a_ref[0, 0, j], a_ref[0, 1, j]], axis=0)
        x = jnp.dot(g_ref[j], rhs, preferred_element_type=F32)
        o_ref[0, j] = (x[:DFT_N1] * sc_ref[...]).astype(o_ref.dtype)
        o_ref[1, j] = (x[DFT_N1:] * sc_ref[...]).astype(o_ref.dtype)


def _filter_spectrum(a, g, scale, n2, kb):
    c = a.shape[-1]
    return pl.pallas_call(
        functools.partial(_spectrum_kernel, kb=kb),
        grid=(n2 // kb,),
        in_specs=[pl.BlockSpec((1, 2, kb, DFT_N1, c), lambda k: (0, 0, k, 0, 0)),
                  pl.BlockSpec((kb, 2 * DFT_N1, 2 * DFT_N1), lambda k: (k, 0, 0)), _full((1, c))],
        out_specs=pl.BlockSpec((2, kb, DFT_N1, c), lambda k: (0, k, 0, 0)),
        out_shape=jax.ShapeDtypeStruct((2, n2, DFT_N1, c), BF16),
        compiler_params=_params(("arbitrary",)),
        name="filter_spectrum",
    )(a, g, scale)


def _dft_mid_kernel(a_ref, g_ref, kh_ref, f_ref, o_ref, *, kb):
    for j in range(kb):
        rhs = jnp.concatenate([a_ref[0, 0, j], a_ref[0, 1, j]], axis=0)
        x = jnp.dot(g_ref[j], rhs, preferred_element_type=F32)
        xr, xi = x[:DFT_N1], x[DFT_N1:]
        kr, ki = kh_ref[0, j].astype(F32), kh_ref[1, j].astype(F32)
        yr = xr * kr - xi * ki
        yi = xr * ki + xi * kr
        v = jnp.concatenate([yr, -yi], axis=0).astype(BF16)
        b = jnp.dot(f_ref[...], v, preferred_element_type=F32)
        o_ref[0, 0, :, j, :] = b[:DFT_N1]
        o_ref[0, 1, :, j, :] = b[DFT_N1:]


def _dft_mid(a, g, khat, f1, pairs, n2, kb):
    c = a.shape[-1]
    blk = pl.BlockSpec((1, 2, kb, DFT_N1, c), lambda k, p: (p, 0, k, 0, 0))
    return pl.pallas_call(
        functools.partial(_dft_mid_kernel, kb=kb),
        grid=(n2 // kb, pairs),
        in_specs=[blk, pl.BlockSpec((kb, 2 * DFT_N1, 2 * DFT_N1), lambda k, p: (k, 0, 0)),
                  pl.BlockSpec((2, kb, DFT_N1, c), lambda k, p: (0, k, 0, 0)), _full(f1.shape)],
        out_specs=pl.BlockSpec((1, 2, DFT_N1, kb, c), lambda k, p: (p, 0, 0, k, 0)),
        out_shape=jax.ShapeDtypeStruct((pairs, 2, DFT_N1, n2, c), F32),
        compiler_params=_params(("arbitrary", "arbitrary")),
        name="dft_mid",
    )(a, g, khat, f1)


def _dft_last_kernel(b_ref, h_ref, u_ref, x0_ref, bd_ref, o_ref, *, c):
    half = h_ref.shape[1] // 2
    bd = bd_ref[...]
    for t in range(DFT_ROW_CHUNK):
        rhs = jnp.concatenate([b_ref[0, 0, t], b_ref[0, 1, t]], axis=0).astype(BF16)
        v = jnp.dot(h_ref[t], rhs, preferred_element_type=F32)
        o_ref[0, :, t, :] = (v[:half] + u_ref[0, :, t, :] * bd) * x0_ref[0, :, t, :]
        o_ref[1, :, t, :] = (-v[half:] + u_ref[1, :, t, :] * bd) * x0_ref[1, :, t, :]


def _dft_last(bm, hh, u_view, x0_view, bias_d, pairs, n2, c):
    half = n2 // 2
    tc = DFT_ROW_CHUNK
    io = pl.BlockSpec((2, half, tc, c), lambda p, j: (p, 0, j, 0))
    return pl.pallas_call(
        functools.partial(_dft_last_kernel, c=c),
        grid=(pairs, DFT_N1 // tc),
        in_specs=[pl.BlockSpec((1, 2, tc, n2, c), lambda p, j: (p, 0, j, 0, 0)),
                  pl.BlockSpec((tc, n2, 2 * n2), lambda p, j: (j, 0, 0)), io, io, _full((1, c))],
        out_specs=io,
        out_shape=jax.ShapeDtypeStruct((2 * pairs, half, DFT_N1, c), F32),
        compiler_params=_params(("arbitrary", "arbitrary")),
        name="dft_last",
    )(bm, hh, u_view, x0_view, bias_d)


def _long_conv_mixer(u, x0, kfilt, ksum, bias_d):
    b, n, c = u.shape
    n2 = 2 * n // DFT_N1
    half = n2 // 2
    pairs = b // 2
    m_data, m_filt, g, f1, hh = (jnp.asarray(t) for t in _dft_tables(n2))
    kb = min(8, n2)
    scale = 1.0 / (ksum * float(DFT_N1 * n2))
    ka = _dft_stage1(m_filt, [kfilt.reshape(1, n2, DFT_N1, c)], 1, n2, n2, c)
    khat = _filter_spectrum(ka.reshape(1, 2, n2, DFT_N1, c), g, scale, n2, kb)
    u_view = u.reshape(b, half, DFT_N1, c)
    a = _dft_stage1(m_data, [u_view, u_view], pairs, half, n2, c)
    bm = _dft_mid(a.reshape(pairs, 2, n2, DFT_N1, c), g, khat, f1, pairs, n2, kb)
    out = _dft_last(bm, hh, u_view, x0.reshape(b, half, DFT_N1, c),
                    bias_d.reshape(1, c), pairs, n2, c)
    return out.reshape(b, n, c)


@functools.lru_cache(maxsize=None)
def _small_dft_tables(n):
    big = 2 * n
    a = np.arange(big)
    fr, fi = _twiddle(np.outer(a, a), big)
    cast = lambda m: np.asarray(m, dtype=BF16)
    return (cast(np.concatenate([fr, fi], 0)),
            cast(_real_form(fr[:, :n], fi[:, :n])),
            cast(_real_form(fr[:n], fi[:n])))


def _small_conv_kernel(k_ref, ks_ref, u_ref, x0_ref, bd_ref, mf_ref, md_ref, mi_ref, o_ref, *, n):
    big = 2 * n
    kh = jnp.dot(mf_ref[...], k_ref[...].astype(BF16), preferred_element_type=F32) * (1.0 / (ks_ref[...] * big))
    kr, ki = kh[:big], kh[big:]
    rhs = jnp.concatenate([u_ref[0], u_ref[1]], axis=0).astype(BF16)
    x = jnp.dot(md_ref[...], rhs, preferred_element_type=F32)
    xr, xi = x[:big], x[big:]
    v = jnp.concatenate([xr * kr - xi * ki, -(xr * ki + xi * kr)], axis=0).astype(BF16)
    y = jnp.dot(mi_ref[...], v, preferred_element_type=F32)
    bd = bd_ref[...]
    o_ref[0] = ((y[:n] + u_ref[0] * bd) * x0_ref[0]).astype(o_ref.dtype)
    o_ref[1] = ((-y[n:] + u_ref[1] * bd) * x0_ref[1]).astype(o_ref.dtype)


def _small_conv_mixer(u, x0, kfilt, ksum, bias_d):
    b, n, c = u.shape
    mf, md, mi = (jnp.asarray(t) for t in _small_dft_tables(n))
    io = pl.BlockSpec((2, n, c), lambda p: (p, 0, 0))
    return pl.pallas_call(
        functools.partial(_small_conv_kernel, n=n),
        grid=(b // 2,),
        in_specs=[_full((2 * n, c)), _full((1, c)), io, io, _full((1, c)),
                  _full(mf.shape), _full(md.shape), _full(mi.shape)],
        out_specs=io,
        out_shape=jax.ShapeDtypeStruct((b, n, c), BF16),
        compiler_params=_params(("arbitrary",)),
        name="small_conv",
    )(kfilt, ksum, u, x0, bias_d.reshape(1, c), mf, md, mi)


MXU_DIM = 256
FFN_CHUNK_EDGES = (0, 6 * MXU_DIM, D_FF)


def _gelu_tanh(x):
    return 0.5 * x * (1.0 + jnp.tanh(math.sqrt(2.0 / math.pi) * (x + 0.044715 * (x * x * x))))


def _ffn_kernel(x_ref, xp_ref, xn_ref, oa_ref, oap_ref, oan_ref, oc_ref, ocp_ref, ocn_ref, wa_ref, wc_ref, g1_ref,
                sh_ref, sc_ref, gt_ref, g_ref, wu_ref, cw_ref, cb_ref, wd_ref, o_ref):
    i = pl.program_id(1)
    nt = pl.num_programs(1)
    tm = x_ref.shape[1]

    def halo(p_ref, n_ref):
        return jnp.concatenate([p_ref[0][-SUBLANES:], n_ref[0][:SUBLANES]], axis=0)

    def mixer_out(oa, oc):
        return (jnp.dot(oa.astype(BF16), wa_ref[...], preferred_element_type=F32)
                + jnp.dot(oc.astype(BF16), wc_ref[...], preferred_element_type=F32))

    g1 = g1_ref[0]
    xm = x_ref[0] + g1 * mixer_out(oa_ref[0], oc_ref[0])
    xh = halo(xp_ref, xn_ref) + g1 * mixer_out(halo(oap_ref, oan_ref), halo(ocp_ref, ocn_ref))
    ext = jnp.concatenate([xh[:SUBLANES], xm, xh[SUBLANES:]], axis=0)
    r = tm + 2 * SUBLANES
    h = _modulated_norm(ext, g_ref[...], sh_ref[0], sc_ref[0])
    row = lax.broadcasted_iota(jnp.int32, (r, 1), 0)
    inside = jnp.logical_and(jnp.logical_or(i > 0, row >= SUBLANES), jnp.logical_or(i < nt - 1, row < tm + SUBLANES))
    h = jnp.where(inside, h, 0.0).astype(BF16)
    hm = h[SUBLANES:tm + SUBLANES]
    acc = jnp.zeros((tm, D_MODEL), F32)
    for c0, c1 in zip(FFN_CHUNK_EDGES[:-1], FFN_CHUNK_EDGES[1:]):
        a = jnp.dot(h, wu_ref[:, c0:c1], preferred_element_type=F32)
        v = jnp.dot(hm, wu_ref[:, D_FF + c0:D_FF + c1], preferred_element_type=F32)
        ap = pltpu.roll(a, 1, 0)[SUBLANES:tm + SUBLANES]
        an = pltpu.roll(a, r - 1, 0)[SUBLANES:tm + SUBLANES]
        cw = cw_ref[:, c0:c1]
        conv = ap * cw[0:1] + a[SUBLANES:tm + SUBLANES] * cw[1:2] + an * cw[2:3] + cb_ref[:, c0:c1]
        act = (_gelu_tanh(conv) * v).astype(BF16)
        acc = acc + jnp.dot(act, wd_ref[c0:c1, :], preferred_element_type=F32)
    o_ref[0] = xm + gt_ref[0] * acc


def _mixer_out_ffn(x, oa, oc, wo_bf, gate1, sh, sc, gate2, gain, wu_bf, conv_w, conv_b, wd_bf):
    bx, t, d = x.shape
    tm = min(512, t)
    vec = pl.BlockSpec((1, 1, d), lambda b, i: (b, 0, 0))
    once = pl.Buffered(1)
    return pl.pallas_call(
        _ffn_kernel,
        grid=(bx, t // tm),
        in_specs=_halo_specs(tm, t, d) + _halo_specs(tm, t, Q_W, oa.dtype) + _halo_specs(tm, t, CONV_W, oc.dtype) + [
            pl.BlockSpec((Q_W, d), lambda b, i: (0, 0)), pl.BlockSpec((CONV_W, d), lambda b, i: (1, 0)),
            vec, vec, vec, vec, _full((1, d)),
            pl.BlockSpec((d, 2 * D_FF), lambda b, i: (0, 0), pipeline_mode=once),
            _full((3, D_FF)), _full((1, D_FF)),
            pl.BlockSpec((D_FF, d), lambda b, i: (0, 0), pipeline_mode=once)],
        out_specs=pl.BlockSpec((1, tm, d), lambda b, i: (b, i, 0)),
        out_shape=jax.ShapeDtypeStruct((bx, t, d), F32),
        compiler_params=_params(("arbitrary", "arbitrary")),
        name="mixer_out_ffn",
    )(x, x, x, oa, oa, oa, oc, oc, oc, wo_bf, wo_bf, gate1, sh, sc, gate2, gain, wu_bf, conv_w,
      conv_b.reshape(1, D_FF), wd_bf)


@functools.lru_cache(maxsize=None)
def _rope_tables(t):
    pos = np.arange(t)
    n_freq = HEAD_DIM // 4
    inv = ROPE_THETA ** (-np.arange(n_freq, dtype=np.float64) / n_freq)
    ang_r = (pos // GRID_W)[:, None] * inv
    ang_c = (pos % GRID_W)[:, None] * inv
    zero = np.zeros_like(ang_r)
    cos = np.concatenate([np.cos(ang_r)] * 2 + [np.cos(ang_c)] * 2, 1)
    sa = np.concatenate([-np.sin(ang_r), zero, -np.sin(ang_c), zero], 1)
    sb = np.concatenate([zero, np.sin(ang_r), zero, np.sin(ang_c)], 1)
    tile = lambda m: np.tile(m, (1, LANES // HEAD_DIM)).astype(np.float32)
    return tile(cos), tile(sa), tile(sb)


@functools.lru_cache(maxsize=None)
def _identity_rope_tables(t):
    return np.ones((t, LANES), np.float32), np.zeros((t, LANES), np.float32), np.zeros((t, LANES), np.float32)


@functools.lru_cache(maxsize=None)
def _head_block_diag():
    hid = np.arange(Q_W) // HEAD_DIM
    return np.asarray(hid[:, None] == hid[None, :], dtype=BF16)


def kernel(x, c, ctx, c_ctx, ada_w, ada_b, norm_mix, norm_ffn, mix_w_in, mix_w_out, attn_q_norm, attn_k_norm,
           swa_sink, hy_conv_w, hy_conv_b, hy_w1, hy_b1, hy_w2, hy_b2, hy_w3, hy_b3, hy_w4, hy_freq, hy_bias_d,
           sc_conv_w, ffn_w_up, ffn_conv_w, ffn_conv_b, ffn_w_down):
    b, s, d = x.shape
    s_ctx = ctx.shape[1]
    depth = ada_w.shape[0]
    assert d == D_MODEL and b % 2 == 0 and b + 1 <= MOD_ROWS and s % 1024 == 0 and s_ctx % SUBLANES == 0

    cvec = jnp.concatenate([c, c_ctx[None, :], jnp.zeros((MOD_ROWS - b - 1, d), F32)], axis=0)
    mods = _mods(cvec, ada_w, ada_b)
    rope = [jnp.asarray(t) for t in _rope_tables(s)]
    rope_ctx = [jnp.asarray(t) for t in _identity_rope_tables(s_ctx)]
    bd = jnp.asarray(_head_block_diag())
    xc = ctx
    weights = _weights_bf16([mix_w_in, mix_w_out, ffn_w_up, ffn_w_down])

    for i in range(depth):
        last = i == depth - 1
        j = i // 2
        lat = [mods[i, :b, k * d:(k + 1) * d][:, None, :] for k in range(6)]
        cx = [jnp.broadcast_to(mods[i, b, k * d:(k + 1) * d][None, None, :], (b, 1, d)) for k in range(6)]
        w_in, w_out, w_up, w_down = (w[i] for w in weights)
        g_mix = norm_mix[i].reshape(1, d)
        g_ffn = norm_ffn[i].reshape(1, d)
        qg = jnp.tile(attn_q_norm[i], N_Q_HEADS).reshape(1, Q_W)
        kg = jnp.tile(attn_k_norm[i], N_KV_HEADS).reshape(1, KV_W)

        conv = (hy_conv_w[j], hy_conv_b[j]) if i % 2 == 0 else (sc_conv_w[j],)
        q, k, vt, *mix = _inproj(x, lat[0], lat[1], g_mix, w_in, qg, kg, bd, *rope, *conv)
        qc, kc, vct, *mixc = _inproj(xc, cx[0], cx[1], g_mix, w_in, qg, kg, bd, *rope_ctx, *conv)

        bound = _score_bound(attn_q_norm[i], attn_k_norm[i])
        if i % 2 == 0:
            fargs = (hy_w1[j], hy_b1[j], hy_w2[j], hy_b2[j], hy_w3[j], hy_b3[j], hy_w4[j], hy_freq[j])
            o_attn = _dense_attention(q, k, vt, kc, vct, bound=bound)
            kf, ks = _implicit_filter(s, *fargs)
            o_conv = _long_conv_mixer(*mix, kf, ks, hy_bias_d[j])
            if not last:
                oc_attn = _dense_attention(qc, kc, vct, bound=bound)
                kfc, ksc = _implicit_filter(s_ctx, *fargs)
                oc_conv = _small_conv_mixer(*mixc, kfc, ksc, hy_bias_d[j])
        else:
            sink = swa_sink[j].reshape(N_KV_HEADS, GROUP)
            o_attn = _banded_attention(q, k, vt, kc, vct, sink, bound)
            o_conv = mix[0]
            if not last:
                oc_attn = _dense_attention(qc, kc, vct, sink=sink)
                oc_conv = mixc[0]

        ffn = (g_ffn, w_up, ffn_conv_w[i], ffn_conv_b[i], w_down)
        x = _mixer_out_ffn(x, o_attn, o_conv, w_out, lat[2], lat[3], lat[4], lat[5], *ffn)
        if not last:
            xc = _mixer_out_ffn(xc, oc_attn, oc_conv, w_out, cx[2], cx[3], cx[4], cx[5], *ffn)
    return x
```

```python
import functools
import math

import numpy as np
import jax
import jax.numpy as jnp
from jax import lax
from jax.experimental import pallas as pl
from jax.experimental.pallas import tpu as pltpu

F32 = jnp.float32
BF16 = jnp.bfloat16
HI = lax.Precision.HIGHEST

D_MODEL = 1024
GRID_W = 64
HEAD_DIM = 64
N_Q_HEADS = 8
N_KV_HEADS = 4
GROUP = N_Q_HEADS // N_KV_HEADS
Q_W = N_Q_HEADS * HEAD_DIM
KV_W = N_KV_HEADS * HEAD_DIM
QKV_W = Q_W + 2 * KV_W
CONV_W = D_MODEL // 2
MIX_IN_W = QKV_W + 3 * CONV_W
WINDOW = 128
ROPE_THETA = 10000.0
FILT_EMB = 33
FILT_WIDTH = 64
DECAY_TARGET = 1e-2
FAST_DECAY_PCT = 0.3
SLOW_DECAY_PCT = 1.5
DECAY_SHIFT = 0.05
D_FF = 2816
NEG_INF = -1e30
RMS_EPS = 1e-6
LOG2E = 1.4426950408889634
Q_SCALE = HEAD_DIM ** -0.5 * LOG2E

LANES = 128
SUBLANES = 8
DFT_N1 = 128
VMEM_LIMIT_MB = 56
VT_ROWS = LANES
MOD_ROWS = SUBLANES
PROJ_SUB_ROWS = 128


def _params(sem, vmem_mb=VMEM_LIMIT_MB):
    return pltpu.CompilerParams(dimension_semantics=sem, vmem_limit_bytes=vmem_mb * 1024 * 1024)


def _full(shape):
    nd = len(shape)
    return pl.BlockSpec(shape, lambda *_: (0,) * nd)


def _mods_kernel(c_ref, w_ref, b_ref, o_ref):
    c = c_ref[...]
    s = c / (1.0 + jnp.exp(-c))
    o_ref[0] = jnp.dot(s, w_ref[0], preferred_element_type=F32, precision=HI) + b_ref[0]


def _mods(cvec, ada_w, ada_b):
    depth, d, n6 = ada_w.shape
    tn = 1536
    return pl.pallas_call(
        _mods_kernel,
        grid=(depth, n6 // tn),
        in_specs=[_full((MOD_ROWS, d)),
                  pl.BlockSpec((1, d, tn), lambda l, j: (l, 0, j)),
                  pl.BlockSpec((1, 1, tn), lambda l, j: (l, 0, j))],
        out_specs=pl.BlockSpec((1, MOD_ROWS, tn), lambda l, j: (l, 0, j)),
        out_shape=jax.ShapeDtypeStruct((depth, MOD_ROWS, n6), F32),
        compiler_params=_params(("arbitrary", "arbitrary")),
        name="ada_mods",
    )(cvec, ada_w, ada_b.reshape(depth, 1, n6))


def _modulated_norm(x, gain, shift, scale):
    ms = jnp.mean(x * x, axis=-1, keepdims=True)
    return (x * lax.rsqrt(ms + RMS_EPS)) * gain * (1.0 + scale) + shift


WEIGHT_CAST_STEPS = 8


def _cast_kernel(*refs):
    n = len(refs) // 2
    for w_ref, o_ref in zip(refs[:n], refs[n:]):
        o_ref[...] = w_ref[0].astype(o_ref.dtype)


def _weights_bf16(stacked):
    depth = stacked[0].shape[0]
    in_specs, out_specs, out_shape, args = [], [], [], []
    for w in stacked:
        _, rows, cols = w.shape
        tr = rows // WEIGHT_CAST_STEPS
        for layer in range(depth):
            in_specs.append(pl.BlockSpec((1, tr, cols), lambda r, layer=layer: (layer, r, 0)))
            out_specs.append(pl.BlockSpec((tr, cols), lambda r: (r, 0)))
            out_shape.append(jax.ShapeDtypeStruct((rows, cols), BF16))
            args.append(w)
    outs = pl.pallas_call(
        _cast_kernel,
        grid=(WEIGHT_CAST_STEPS,),
        in_specs=in_specs,
        out_specs=out_specs,
        out_shape=out_shape,
        compiler_params=_params(("arbitrary",)),
        name="weights_bf16",
    )(*args)
    return [outs[k * depth:(k + 1) * depth] for k in range(len(stacked))]


def _halo_specs(tm, t, width, dtype=F32):
    rows = SUBLANES * 4 // jnp.dtype(dtype).itemsize
    nb = tm // rows
    last = t // rows - 1
    return [pl.BlockSpec((1, tm, width), lambda b, i: (b, i, 0)),
            pl.BlockSpec((1, rows, width), lambda b, i: (b, jnp.maximum(i * nb - 1, 0), 0)),
            pl.BlockSpec((1, rows, width), lambda b, i: (b, jnp.minimum((i + 1) * nb, last), 0))]


def _inproj_kernel(*refs, mode):
    (x_ref, xp_ref, xn_ref, sh_ref, sc_ref, g_ref, w_ref, qg_ref, kg_ref, bd_ref,
     cos_ref, sa_ref, sb_ref, cw_ref) = refs[:14]
    if mode == "hyena":
        cb_ref, q_ref, k_ref, vt_ref, u_ref, x0_ref = refs[14:]
    else:
        q_ref, k_ref, vt_ref, oc_ref = refs[14:]
    i = pl.program_id(1)
    nt = pl.num_programs(1)
    tm = x_ref.shape[1]
    sub = min(PROJ_SUB_ROWS, tm)
    gain, shift, scale = g_ref[...], sh_ref[0], sc_ref[0]

    xh = jnp.concatenate([xp_ref[0], xn_ref[0]], axis=0)
    zh = jnp.dot(_modulated_norm(xh, gain, shift, scale).astype(BF16), w_ref[:, QKV_W:], preferred_element_type=F32)
    before = jnp.where(i > 0, zh[:SUBLANES], 0.0)
    after = jnp.where(i < nt - 1, zh[SUBLANES:], 0.0)
    cw = cw_ref[...]

    def conv_stage(ext, rows):
        n = sub + 2 * SUBLANES
        mid = slice(SUBLANES, sub + SUBLANES)
        if mode == "hyena":
            c = (pltpu.roll(ext, 1, 0)[mid] * cw[0:1] + ext[mid] * cw[1:2] + pltpu.roll(ext, n - 1, 0)[mid] * cw[2:3]
                 + cb_ref[...])
            x0_ref[0, rows] = c[:, :CONV_W]
            u_ref[0, rows] = c[:, 2 * CONV_W:] * c[:, CONV_W:2 * CONV_W]
        else:
            pr = ext[:, CONV_W:2 * CONV_W] * ext[:, 2 * CONV_W:]
            conv = pltpu.roll(pr, 1, 0)[mid] * cw[0:1] + pr[mid] * cw[1:2] + pltpu.roll(pr, n - 1, 0)[mid] * cw[2:3]
            oc_ref[0, rows] = (ext[mid, :CONV_W] * conv).astype(oc_ref.dtype)

    def head_norm_rope(t, hgain, bd, cos, sa, sb):
        ssq = jnp.dot((t * t).astype(BF16), bd, preferred_element_type=F32)
        t = t * lax.rsqrt(ssq * (1.0 / HEAD_DIM) + RMS_EPS) * hgain
        outs = []
        for j in range(t.shape[1] // LANES):
            tj = t[:, j * LANES:(j + 1) * LANES]
            outs.append(tj * cos + pltpu.roll(tj, LANES - 16, 1) * sa + pltpu.roll(tj, 16, 1) * sb)
        return outs

    held = None
    for r0 in range(0, tm, sub):
        rows = slice(r0, r0 + sub)
        h = _modulated_norm(x_ref[0, rows], gain, shift, scale)
        hb = h.astype(BF16)
        z = jnp.dot(hb, w_ref[:, :QKV_W], preferred_element_type=F32)
        zc = jnp.dot(hb, w_ref[:, QKV_W:], preferred_element_type=F32)
        if held is not None:
            conv_stage(jnp.concatenate([before, held[0], zc[:SUBLANES]], axis=0), held[1])
            before = held[0][-SUBLANES:]
        held = (zc, rows)
        tabs = (cos_ref[rows], sa_ref[rows], sb_ref[rows])
        qs = head_norm_rope(z[:, :Q_W], qg_ref[...], bd_ref[...], *tabs)
        ks = head_norm_rope(z[:, Q_W:Q_W + KV_W], kg_ref[...], bd_ref[:KV_W, :KV_W], *tabs)
        for j, qj in enumerate(qs):
            qj = (qj * Q_SCALE).astype(BF16)
            q_ref[0, 2 * j, rows] = qj[:, :HEAD_DIM]
            q_ref[0, 2 * j + 1, rows] = qj[:, HEAD_DIM:]
        for j, kj in enumerate(ks):
            kj = kj.astype(BF16)
            k_ref[0, 2 * j, rows] = kj[:, :HEAD_DIM]
            k_ref[0, 2 * j + 1, rows] = kj[:, HEAD_DIM:]
        vt = z[:, Q_W + KV_W:QKV_W].T
        ones_row = (lax.broadcasted_iota(jnp.int32, (VT_ROWS - HEAD_DIM, sub), 0) == 0).astype(F32)
        for hh in range(N_KV_HEADS):
            vt_ref[0, hh, :, rows] = jnp.concatenate([vt[hh * HEAD_DIM:(hh + 1) * HEAD_DIM], ones_row],
                                                     axis=0).astype(BF16)

    conv_stage(jnp.concatenate([before, held[0], after], axis=0), held[1])


def _inproj(x, sh, sc, gain, w_bf, qg, kg, bd, cos, sa, sb, conv_w, conv_b=None):
    bx, t, d = x.shape
    tm = min(1024, t)
    mode = "hyena" if conv_b is not None else "short"
    vec = pl.BlockSpec((1, 1, d), lambda b, i: (b, 0, 0))
    tab = pl.BlockSpec((tm, LANES), lambda b, i: (i, 0))
    row = lambda w: pl.BlockSpec((1, tm, w), lambda b, i: (b, i, 0))
    in_specs = _halo_specs(tm, t, d) + [vec, vec, _full((1, d)), _full((d, MIX_IN_W)), _full((1, Q_W)),
                                        _full((1, KV_W)), _full((Q_W, Q_W)), tab, tab, tab, _full(conv_w.shape)]
    args = [x, x, x, sh, sc, gain, w_bf, qg, kg, bd, cos, sa, sb, conv_w]
    out_specs = [pl.BlockSpec((1, N_Q_HEADS, tm, HEAD_DIM), lambda b, i: (b, 0, i, 0)),
                 pl.BlockSpec((1, N_KV_HEADS, tm, HEAD_DIM), lambda b, i: (b, 0, i, 0)),
                 pl.BlockSpec((1, N_KV_HEADS, VT_ROWS, tm), lambda b, i: (b, 0, 0, i))]
    out_shape = [jax.ShapeDtypeStruct((bx, N_Q_HEADS, t, HEAD_DIM), BF16),
                 jax.ShapeDtypeStruct((bx, N_KV_HEADS, t, HEAD_DIM), BF16),
                 jax.ShapeDtypeStruct((bx, N_KV_HEADS, VT_ROWS, t), BF16)]
    if mode == "hyena":
        in_specs.append(_full((1, 3 * CONV_W)))
        args.append(conv_b.reshape(1, 3 * CONV_W))
        out_specs += [row(CONV_W), row(CONV_W)]
        out_shape += [jax.ShapeDtypeStruct((bx, t, CONV_W), F32)] * 2
    else:
        out_specs.append(row(CONV_W))
        out_shape.append(jax.ShapeDtypeStruct((bx, t, CONV_W), BF16))
    return pl.pallas_call(
        functools.partial(_inproj_kernel, mode=mode),
        grid=(bx, t // tm),
        in_specs=in_specs,
        out_specs=out_specs,
        out_shape=out_shape,
        compiler_params=_params(("arbitrary", "arbitrary")),
        name="in_proj",
    )(*args)


def _score_bound(q_gain, k_gain):
    b = HEAD_DIM * Q_SCALE * BOUND_MARGIN * jnp.max(jnp.abs(q_gain)) * jnp.max(jnp.abs(k_gain))
    return b.reshape(1, 1).astype(F32)


def _scores_t(ks, q):
    return lax.dot_general(ks, q, (((1,), (1,)), ((), ())), preferred_element_type=F32)


def _sink_column(sink_ref, hh, tq, m, acc):
    col = lax.broadcasted_iota(jnp.int32, m.shape, 1)
    sink = jnp.where(col < tq, sink_ref[hh, 0], sink_ref[hh, 1]) * LOG2E
    m_new = jnp.maximum(m, sink)
    den_row = lax.broadcasted_iota(jnp.int32, acc.shape, 0) == HEAD_DIM
    return acc * jnp.exp2(m - m_new) + jnp.where(den_row, jnp.exp2(sink - m_new), 0.0)


def _attn_finish(acc, tq):
    o = (acc / acc[HEAD_DIM:HEAD_DIM + 1]).T
    return jnp.concatenate([o[:tq, :HEAD_DIM], o[tq:, :HEAD_DIM]], axis=1)


EXP_CHUNK_ELEMS = 32 * SUBLANES * LANES


def _score_stage(ks, q, s_ref, bias=None):
    s = _scores_t(ks, q)
    if bias is not None:
        s = s + bias
    s_ref[0:s.shape[0]] = s
    return jnp.max(s, axis=0, keepdims=True)


def _softmax_stage(s_ref, p_ref, n, vt, mx, m, acc_ref):
    m_new = jnp.maximum(m, mx)
    rows = EXP_CHUNK_ELEMS // m.shape[1]
    for c in range(0, n, rows):
        p_ref[c:c + rows] = jnp.exp2(s_ref[c:c + rows] - m_new).astype(BF16)
    acc_ref[...] = acc_ref[...] * jnp.exp2(m - m_new) + jnp.dot(vt, p_ref[0:n], preferred_element_type=F32)
    return m_new


BOUNDED_TILES = 8
BOUND_MARGIN = 1.01
SAFE_SHIFT = 40.0


def _flash_exact(q, k_tile, vt_tile, ctx, sa_ref, sb_ref, p_ref, acc_ref, *, tk, n_main):
    pa_ref, pb_ref = p_ref.at[0:tk], p_ref.at[tk:2 * tk]
    acc_ref[...] = jnp.zeros_like(acc_ref)
    m = jnp.full((1, q.shape[0]), NEG_INF, F32)
    mx_a = _score_stage(k_tile(0), q, sa_ref)
    if n_main > 1:
        def body(i, carry):
            m, mx_a = carry
            mx_b = _score_stage(k_tile(2 * i + 1), q, sb_ref)
            m = _softmax_stage(sa_ref, pa_ref, tk, vt_tile(2 * i), mx_a, m, acc_ref)
            mx_a = _score_stage(k_tile(2 * i + 2), q, sa_ref)
            m = _softmax_stage(sb_ref, pb_ref, tk, vt_tile(2 * i + 1), mx_b, m, acc_ref)
            return m, mx_a

        m, mx_a = lax.fori_loop(0, n_main // 2 - 1, body, (m, mx_a))
        mx_b = _score_stage(k_tile(n_main - 1), q, sb_ref)
        m = _softmax_stage(sa_ref, pa_ref, tk, vt_tile(n_main - 2), mx_a, m, acc_ref)
        last = (sb_ref, pb_ref, mx_b)
        spare = (sa_ref, pa_ref)
    else:
        last = (sa_ref, pa_ref, mx_a)
        spare = (sb_ref, pb_ref)
    if ctx is not None:
        mx_c = _score_stage(ctx[0], q, spare[0])
    m = _softmax_stage(last[0], last[1], tk, vt_tile(n_main - 1), last[2], m, acc_ref)
    if ctx is not None:
        m = _softmax_stage(spare[0], spare[1], ctx[0].shape[0], ctx[1], mx_c, m, acc_ref)
    return m


def _exp2_bf16(s):
    return jnp.exp2(s).astype(BF16)


def _flash_bounded(q, k_ref, vt_ref, ctx, p_ref, acc_ref, *, tk, n_main):
    step = BOUNDED_TILES * tk if n_main % BOUNDED_TILES == 0 else tk

    def group(j):
        off = pl.multiple_of(j * step, step)
        for c in range(0, step, tk):
            p_ref[c:c + tk] = _exp2_bf16(_scores_t(k_ref[0, 0, pl.ds(off + c, tk), :], q))
        return jnp.dot(vt_ref[0, 0, :, pl.ds(off, step)], p_ref[0:step], preferred_element_type=F32)

    first = group(0)
    if ctx is not None:
        pc_ref = p_ref.at[BOUNDED_TILES * tk:BOUNDED_TILES * tk + ctx[0].shape[0]]
        pc_ref[...] = _exp2_bf16(_scores_t(ctx[0], q))
        first = first + jnp.dot(ctx[1], pc_ref[...], preferred_element_type=F32)
    acc_ref[...] = first

    def body(j, carry):
        acc_ref[...] += group(j)
        return carry

    lax.fori_loop(1, n_main * tk // step, body, 0)


def _flash_kernel(*refs, tq, tk, n_main, has_ctx, has_sink):
    sa_ref, sb_ref, p_ref, acc_ref = refs[-4:]
    o_ref = refs[-5]
    refs = list(refs[:-5])
    sink_ref = refs.pop(0) if has_sink else None
    bound_ref = None if has_sink else refs.pop(0)
    q_ref, k_ref, vt_ref = refs[:3]
    m_cols = GROUP * tq
    q = q_ref[0].reshape(m_cols, HEAD_DIM)
    ctx = (refs[3][0, 0], refs[4][0, 0]) if has_ctx else None

    def k_tile(j):
        return k_ref[0, 0, pl.ds(pl.multiple_of(j * tk, tk), tk), :]

    def vt_tile(j):
        return vt_ref[0, 0, :, pl.ds(pl.multiple_of(j * tk, tk), tk)]

    exact = functools.partial(_flash_exact, q, k_tile, vt_tile, ctx, sa_ref, sb_ref, p_ref, acc_ref,
                              tk=tk, n_main=n_main)
    if has_sink:
        acc = _sink_column(sink_ref, pl.program_id(1), tq, exact(), acc_ref[...])
    else:
        bound = bound_ref[0, 0]
        safe = bound <= SAFE_SHIFT

        @pl.when(safe)
        def _():
            _flash_bounded(q, k_ref, vt_ref, ctx, p_ref, acc_ref, tk=tk, n_main=n_main)

        @pl.when(jnp.logical_not(safe))
        def _():
            exact()

        acc = acc_ref[...]
    o_ref[0] = _attn_finish(acc, tq).astype(o_ref.dtype)


def _kv_specs(n):
    return [pl.BlockSpec((1, 1, n, HEAD_DIM), lambda b, h, i: (b, h, 0, 0)),
            pl.BlockSpec((1, 1, VT_ROWS, n), lambda b, h, i: (b, h, 0, 0))]


def _dense_attention(q, k, vt, kc=None, vct=None, *, bound=None, sink=None):
    bx, _, t, _ = q.shape
    s = k.shape[2]
    tq = min(1024, t)
    tk = min(512, s)
    has_ctx = kc is not None
    has_sink = sink is not None
    in_specs = [pl.BlockSpec((1, GROUP, tq, HEAD_DIM), lambda b, h, i: (b, h, i, 0))] + _kv_specs(s)
    args = [q, k, vt]
    if has_ctx:
        in_specs += _kv_specs(kc.shape[2])
        args += [kc, vct]
    in_specs = [pl.BlockSpec(memory_space=pltpu.SMEM)] + in_specs
    args = [sink if has_sink else bound] + args
    n_main = s // tk
    assert n_main == 1 or n_main % 2 == 0
    n_ctx = kc.shape[2] if has_ctx else 0
    assert n_ctx <= tk
    m_cols = GROUP * tq
    return pl.pallas_call(
        functools.partial(_flash_kernel, tq=tq, tk=tk, n_main=n_main, has_ctx=has_ctx, has_sink=has_sink),
        grid=(bx, N_KV_HEADS, t // tq),
        in_specs=in_specs,
        out_specs=pl.BlockSpec((1, tq, LANES), lambda b, h, i: (b, i, h)),
        out_shape=jax.ShapeDtypeStruct((bx, t, Q_W), BF16),
        scratch_shapes=[pltpu.VMEM((tk, m_cols), F32), pltpu.VMEM((tk, m_cols), F32),
                        pltpu.VMEM((BOUNDED_TILES * tk + n_ctx, m_cols), BF16), pltpu.VMEM((VT_ROWS, m_cols), F32)],
        compiler_params=_params(("arbitrary", "arbitrary", "arbitrary")),
        name="dense_attention",
    )(*args)


BAND_SUB = 256


@functools.lru_cache(maxsize=None)
def _band_bias():
    span = BAND_SUB + 2 * WINDOW
    kr = np.arange(span)[None, :, None]
    qc = (np.arange(GROUP * BAND_SUB) % BAND_SUB)[None, None, :]
    rel = np.arange(3)[:, None, None]
    return np.where(np.abs(kr - qc - rel * WINDOW) <= WINDOW, 0.0, NEG_INF).astype(np.float32)


def _banded_kernel(sink_ref, bound_ref, q_ref, k_ref, vt_ref, kc_ref, vct_ref, bias_ref, o_ref,
                   sl_ref, sc_ref, pl_ref, pc_ref, acc_ref, *, n_sub, s_len):
    i = pl.program_id(2)
    hh = pl.program_id(1)
    sub = BAND_SUB
    m_cols = GROUP * sub
    span = sub + 2 * WINDOW
    n_ctx = kc_ref.shape[2]

    def window(u):
        q = q_ref[0, :, u * sub:(u + 1) * sub, :].reshape(m_cols, HEAD_DIM)
        q0 = (i * n_sub + u) * sub
        start = pl.multiple_of(jnp.clip(q0 - WINDOW, 0, s_len - span), WINDOW)
        return q, start, bias_ref[(q0 - start) // WINDOW]

    shift = jnp.maximum(bound_ref[0, 0], jnp.maximum(sink_ref[hh, 0], sink_ref[hh, 1]) * LOG2E)
    safe = shift <= SAFE_SHIFT

    @pl.when(safe)
    def _():
        col = lax.broadcasted_iota(jnp.int32, (1, m_cols), 1)
        sink = jnp.where(col < sub, sink_ref[hh, 0], sink_ref[hh, 1]) * LOG2E
        den_row = lax.broadcasted_iota(jnp.int32, (VT_ROWS, m_cols), 0) == HEAD_DIM
        sink_den = jnp.where(den_row, jnp.exp2(sink), 0.0)
        for u in range(n_sub):
            q, start, bias = window(u)
            pl_u, pc_u = pl_ref.at[u], pc_ref.at[u]
            pl_u[...] = _exp2_bf16(_scores_t(k_ref[0, 0, pl.ds(start, span), :], q) + bias)
            pc_u[...] = _exp2_bf16(_scores_t(kc_ref[0, 0], q))
            acc = (jnp.dot(vt_ref[0, 0, :, pl.ds(start, span)], pl_u[...], preferred_element_type=F32)
                   + jnp.dot(vct_ref[0, 0], pc_u[...], preferred_element_type=F32) + sink_den)
            o_ref[0, u * sub:(u + 1) * sub] = _attn_finish(acc, sub).astype(o_ref.dtype)

    @pl.when(jnp.logical_not(safe))
    def _():
        stats = []
        for u in range(n_sub):
            q, start, bias = window(u)
            mx_l = _score_stage(k_ref[0, 0, pl.ds(start, span), :], q, sl_ref.at[u], bias=bias)
            mx_c = _score_stage(kc_ref[0, 0], q, sc_ref.at[u])
            stats.append((start, mx_l, mx_c))
        for u in range(n_sub):
            start, mx_l, mx_c = stats[u]
            acc_u = acc_ref.at[u]
            acc_u[...] = jnp.zeros((VT_ROWS, m_cols), F32)
            m = jnp.full((1, m_cols), NEG_INF, F32)
            m = _softmax_stage(sl_ref.at[u], pl_ref.at[u], span, vt_ref[0, 0, :, pl.ds(start, span)], mx_l, m, acc_u)
            m = _softmax_stage(sc_ref.at[u], pc_ref.at[u], n_ctx, vct_ref[0, 0], mx_c, m, acc_u)
            acc = _sink_column(sink_ref, hh, sub, m, acc_u[...])
            o_ref[0, u * sub:(u + 1) * sub] = _attn_finish(acc, sub).astype(o_ref.dtype)


def _banded_attention(q, k, vt, kc, vct, sink, bound):
    bx, _, t, _ = q.shape
    n_ctx = kc.shape[2]
    n_sub = next(k for k in (8, 4, 2, 1) if t % (k * BAND_SUB) == 0)
    tq = n_sub * BAND_SUB
    m_cols = GROUP * BAND_SUB
    span = BAND_SUB + 2 * WINDOW
    bias = jnp.asarray(_band_bias())
    return pl.pallas_call(
        functools.partial(_banded_kernel, n_sub=n_sub, s_len=t),
        grid=(bx, N_KV_HEADS, t // tq),
        in_specs=[pl.BlockSpec(memory_space=pltpu.SMEM), pl.BlockSpec(memory_space=pltpu.SMEM),
                  pl.BlockSpec((1, GROUP, tq, HEAD_DIM), lambda b, h, i: (b, h, i, 0))]
                 + _kv_specs(t) + _kv_specs(n_ctx) + [_full(bias.shape)],
        out_specs=pl.BlockSpec((1, tq, LANES), lambda b, h, i: (b, i, h)),
        out_shape=jax.ShapeDtypeStruct((bx, t, Q_W), BF16),
        scratch_shapes=[pltpu.VMEM((n_sub, span, m_cols), F32), pltpu.VMEM((n_sub, n_ctx, m_cols), F32),
                        pltpu.VMEM((n_sub, span, m_cols), BF16), pltpu.VMEM((n_sub, n_ctx, m_cols), BF16),
                        pltpu.VMEM((n_sub, VT_ROWS, m_cols), F32)],
        compiler_params=_params(("arbitrary", "arbitrary", "arbitrary")),
        name="banded_attention",
    )(sink, bound, q, k, vt, kc, vct, bias)


@functools.lru_cache(maxsize=None)
def _filter_features(n):
    j = np.arange(2 * n)
    d = np.where(j <= n, j, 2 * n - j)
    d = np.where(j == n, 0, d)
    bands = (FILT_EMB - 1) // 2
    t01 = np.linspace(0.0, 1.0, n)[d]
    w = 2.0 * np.pi * d.astype(np.float64) / n
    f = np.linspace(1e-4, bands - 1, bands)[None, :]
    feats = np.zeros((2 * n, LANES), np.float64)
    feats[:, 0] = t01
    feats[:, 1:1 + bands] = np.cos(f * w[:, None])
    feats[:, 1 + bands:FILT_EMB] = -np.sin(f * w[:, None])
    feats[:, 64] = t01
    feats[:, 65] = (j < n)
    feats[:, 66] = (j != n)
    return feats.astype(np.float32)


def _filter_kernel(f_ref, w1_ref, b1_ref, w2_ref, b2_ref, w3_ref, b3_ref, w4_ref, fr_ref, dl_ref, k_ref, s_ref):
    f = f_ref[...]
    fr = fr_ref[...]
    mm = lambda a, b: jnp.dot(a, b, preferred_element_type=F32, precision=HI)
    h = jnp.sin(fr * (mm(w1_ref[...], f) + b1_ref[...]))
    h = jnp.sin(fr * (mm(w2_ref[...], h) + b2_ref[...]))
    h = jnp.sin(fr * (mm(w3_ref[...], h) + b3_ref[...]))
    hf = mm(w4_ref[...], h)
    win = jnp.exp(-dl_ref[...] * f[64:65]) + DECAY_SHIFT
    k = (jnp.where(f[65:66] > 0.5, hf[:CONV_W], hf[CONV_W:]) * win * f[66:67]).T
    k_ref[...] = k

    @pl.when(pl.program_id(0) == 0)
    def _():
        s_ref[...] = jnp.zeros_like(s_ref)

    s_ref[...] += jnp.sum(jnp.abs(k), axis=0, keepdims=True)


def _implicit_filter(n, w1, b1, w2, b2, w3, b3, w4, freq):
    feats_t = jnp.asarray(np.ascontiguousarray(_filter_features(n).T))
    col = lambda a: a.reshape(-1, 1)
    w1t = jnp.pad(w1.T, ((0, 0), (0, LANES - FILT_EMB)))
    deltas = np.abs(np.linspace(math.log(DECAY_TARGET) / SLOW_DECAY_PCT, math.log(DECAY_TARGET) / FAST_DECAY_PCT,
                                CONV_W)).astype(np.float32).reshape(CONV_W, 1)
    tr = min(1024, 2 * n)
    sq = _full((FILT_WIDTH, FILT_WIDTH))
    vec = _full((FILT_WIDTH, 1))
    return pl.pallas_call(
        _filter_kernel,
        grid=(2 * n // tr,),
        in_specs=[pl.BlockSpec((LANES, tr), lambda i: (0, i)), _full((FILT_WIDTH, LANES)), vec, sq, vec, sq, vec,
                  _full((2 * CONV_W, FILT_WIDTH)), vec, _full((CONV_W, 1))],
        out_specs=[pl.BlockSpec((tr, CONV_W), lambda i: (i, 0)), _full((1, CONV_W))],
        out_shape=[jax.ShapeDtypeStruct((2 * n, CONV_W), F32), jax.ShapeDtypeStruct((1, CONV_W), F32)],
        compiler_params=_params(("arbitrary",)),
        name="hyena_filter",
    )(feats_t, w1t, col(b1), w2.T, col(b2), w3.T, col(b3), w4.T, col(freq), jnp.asarray(deltas))


def _twiddle(idx, mod):
    ang = 2.0 * np.pi * (idx % mod) / mod
    return np.cos(ang), -np.sin(ang)


def _real_form(mr, mi):
    return np.concatenate([np.concatenate([mr, -mi], -1), np.concatenate([mi, mr], -1)], -2)


@functools.lru_cache(maxsize=None)
def _dft_tables(n2):
    n1 = DFT_N1
    n = n1 * n2
    h = n2 // 2
    a2 = np.arange(n2)
    fr, fi = _twiddle(np.outer(a2, a2), n2)
    m_data = _real_form(fr[:, :h], fi[:, :h])
    m_filt = np.concatenate([fr, fi], 0)
    k2 = a2[:, None, None]
    k1 = np.arange(n1)[None, :, None]
    c1 = np.arange(n1)[None, None, :]
    g = _real_form(*_twiddle(c1 * (n2 * k1 + k2), n))
    a1 = np.arange(n1)
    f1 = _real_form(*_twiddle(np.outer(a1, a1), n1))
    t2 = a1[:, None, None]
    t1 = np.arange(h)[None, :, None]
    j1 = a2[None, None, :]
    hh = _real_form(*_twiddle(j1 * (n1 * t1 + t2), n))
    cast = lambda m: np.asarray(m, dtype=BF16)
    return cast(m_data), cast(m_filt), cast(g), cast(f1), cast(hh)


DFT_ROW_CHUNK = SUBLANES


def _dft_rows_kernel(m_ref, *refs):
    o_ref = refs[-1]
    cols = [jnp.concatenate([r[0, :, t, :] for r in refs[:-1]], axis=0) for t in range(DFT_ROW_CHUNK)]
    rhs = jnp.concatenate(cols, axis=1).astype(BF16)
    out = jnp.dot(m_ref[...], rhs, preferred_element_type=F32)
    half = out.shape[0] // 2
    o_ref[0, 0] = out[:half].astype(o_ref.dtype)
    o_ref[0, 1] = out[half:].astype(o_ref.dtype)


def _dft_stage1(mat, views, pairs, rows, n2, c):
    nin = len(views)
    lc = DFT_ROW_CHUNK * c
    in_specs = [_full(mat.shape)] + [
        pl.BlockSpec((1, rows, DFT_ROW_CHUNK, c), (lambda p, j, a=a: (nin * p + a, 0, j, 0))) for a in range(nin)]
    return pl.pallas_call(
        _dft_rows_kernel,
        grid=(pairs, DFT_N1 // DFT_ROW_CHUNK),
        in_specs=in_specs,
        out_specs=pl.BlockSpec((1, 2, n2, lc), lambda p, j: (p, 0, 0, j)),
        out_shape=jax.ShapeDtypeStruct((pairs, 2, n2, DFT_N1 * c), BF16),
        compiler_params=_params(("arbitrary", "arbitrary")),
        name="dft_stage1",
    )(mat, *views)


def _spectrum_kernel(a_ref, g_ref, sc_ref, o_ref, *, kb):
    for j in range(kb):
        rhs = jnp.concatenate([a_ref[0, 0, j], a_ref[0, 1, j]], axis=0)
        x = jnp.dot(g_ref[j], rhs, preferred_element_type=F32)
        o_ref[0, j] = (x[:DFT_N1] * sc_ref[...]).astype(o_ref.dtype)
        o_ref[1, j] = (x[DFT_N1:] * sc_ref[...]).astype(o_ref.dtype)


def _filter_spectrum(a, g, scale, n2, kb):
    c = a.shape[-1]
    return pl.pallas_call(
        functools.partial(_spectrum_kernel, kb=kb),
        grid=(n2 // kb,),
        in_specs=[pl.BlockSpec((1, 2, kb, DFT_N1, c), lambda k: (0, 0, k, 0, 0)),
                  pl.BlockSpec((kb, 2 * DFT_N1, 2 * DFT_N1), lambda k: (k, 0, 0)), _full((1, c))],
        out_specs=pl.BlockSpec((2, kb, DFT_N1, c), lambda k: (0, k, 0, 0)),
        out_shape=jax.ShapeDtypeStruct((2, n2, DFT_N1, c), BF16),
        compiler_params=_params(("arbitrary",)),
        name="filter_spectrum",
    )(a, g, scale)


def _dft_mid_kernel(a_ref, g_ref, kh_ref, f_ref, o_ref, *, kb):
    for j in range(kb):
        rhs = jnp.concatenate([a_ref[0, 0, j], a_ref[0, 1, j]], axis=0)
        x = jnp.dot(g_ref[j], rhs, preferred_element_type=F32)
        xr, xi = x[:DFT_N1], x[DFT_N1:]
        kr, ki = kh_ref[0, j].astype(F32), kh_ref[1, j].astype(F32)
        yr = xr * kr - xi * ki
        yi = xr * ki + xi * kr
        v = jnp.concatenate([yr, -yi], axis=0).astype(BF16)
        b = jnp.dot(f_ref[...], v, preferred_element_type=F32)
        o_ref[0, 0, :, j, :] = b[:DFT_N1]
        o_ref[0, 1, :, j, :] = b[DFT_N1:]


def _dft_mid(a, g, khat, f1, pairs, n2, kb):
    c = a.shape[-1]
    blk = pl.BlockSpec((1, 2, kb, DFT_N1, c), lambda k, p: (p, 0, k, 0, 0))
    return pl.pallas_call(
        functools.partial(_dft_mid_kernel, kb=kb),
        grid=(n2 // kb, pairs),
        in_specs=[blk, pl.BlockSpec((kb, 2 * DFT_N1, 2 * DFT_N1), lambda k, p: (k, 0, 0)),
                  pl.BlockSpec((2, kb, DFT_N1, c), lambda k, p: (0, k, 0, 0)), _full(f1.shape)],
        out_specs=pl.BlockSpec((1, 2, DFT_N1, kb, c), lambda k, p: (p, 0, 0, k, 0)),
        out_shape=jax.ShapeDtypeStruct((pairs, 2, DFT_N1, n2, c), F32),
        compiler_params=_params(("arbitrary", "arbitrary")),
        name="dft_mid",
    )(a, g, khat, f1)


def _dft_last_kernel(b_ref, h_ref, u_ref, x0_ref, bd_ref, o_ref, *, c):
    half = h_ref.shape[1] // 2
    bd = bd_ref[...]
    for t in range(DFT_ROW_CHUNK):
        rhs = jnp.concatenate([b_ref[0, 0, t], b_ref[0, 1, t]], axis=0).astype(BF16)
        v = jnp.dot(h_ref[t], rhs, preferred_element_type=F32)
        o_ref[0, :, t, :] = (v[:half] + u_ref[0, :, t, :] * bd) * x0_ref[0, :, t, :]
        o_ref[1, :, t, :] = (-v[half:] + u_ref[1, :, t, :] * bd) * x0_ref[1, :, t, :]


def _dft_last(bm, hh, u_view, x0_view, bias_d, pairs, n2, c):
    half = n2 // 2
    tc = DFT_ROW_CHUNK
    io = pl.BlockSpec((2, half, tc, c), lambda p, j: (p, 0, j, 0))
    return pl.pallas_call(
        functools.partial(_dft_last_kernel, c=c),
        grid=(pairs, DFT_N1 // tc),
        in_specs=[pl.BlockSpec((1, 2, tc, n2, c), lambda p, j: (p, 0, j, 0, 0)),
                  pl.BlockSpec((tc, n2, 2 * n2), lambda p, j: (j, 0, 0)), io, io, _full((1, c))],
        out_specs=io,
        out_shape=jax.ShapeDtypeStruct((2 * pairs, half, DFT_N1, c), F32),
        compiler_params=_params(("arbitrary", "arbitrary")),
        name="dft_last",
    )(bm, hh, u_view, x0_view, bias_d)


def _long_conv_mixer(u, x0, kfilt, ksum, bias_d):
    b, n, c = u.shape
    n2 = 2 * n // DFT_N1
    half = n2 // 2
    pairs = b // 2
    m_data, m_filt, g, f1, hh = (jnp.asarray(t) for t in _dft_tables(n2))
    kb = min(8, n2)
    scale = 1.0 / (ksum * float(DFT_N1 * n2))
    ka = _dft_stage1(m_filt, [kfilt.reshape(1, n2, DFT_N1, c)], 1, n2, n2, c)
    khat = _filter_spectrum(ka.reshape(1, 2, n2, DFT_N1, c), g, scale, n2, kb)
    u_view = u.reshape(b, half, DFT_N1, c)
    a = _dft_stage1(m_data, [u_view, u_view], pairs, half, n2, c)
    bm = _dft_mid(a.reshape(pairs, 2, n2, DFT_N1, c), g, khat, f1, pairs, n2, kb)
    out = _dft_last(bm, hh, u_view, x0.reshape(b, half, DFT_N1, c),
                    bias_d.reshape(1, c), pairs, n2, c)
    return out.reshape(b, n, c)


@functools.lru_cache(maxsize=None)
def _small_dft_tables(n):
    big = 2 * n
    a = np.arange(big)
    fr, fi = _twiddle(np.outer(a, a), big)
    cast = lambda m: np.asarray(m, dtype=BF16)
    return (cast(np.concatenate([fr, fi], 0)),
            cast(_real_form(fr[:, :n], fi[:, :n])),
            cast(_real_form(fr[:n], fi[:n])))


def _small_conv_kernel(k_ref, ks_ref, u_ref, x0_ref, bd_ref, mf_ref, md_ref, mi_ref, o_ref, *, n):
    big = 2 * n
    kh = jnp.dot(mf_ref[...], k_ref[...].astype(BF16), preferred_element_type=F32) * (1.0 / (ks_ref[...] * big))
    kr, ki = kh[:big], kh[big:]
    rhs = jnp.concatenate([u_ref[0], u_ref[1]], axis=0).astype(BF16)
    x = jnp.dot(md_ref[...], rhs, preferred_element_type=F32)
    xr, xi = x[:big], x[big:]
    v = jnp.concatenate([xr * kr - xi * ki, -(xr * ki + xi * kr)], axis=0).astype(BF16)
    y = jnp.dot(mi_ref[...], v, preferred_element_type=F32)
    bd = bd_ref[...]
    o_ref[0] = ((y[:n] + u_ref[0] * bd) * x0_ref[0]).astype(o_ref.dtype)
    o_ref[1] = ((-y[n:] + u_ref[1] * bd) * x0_ref[1]).astype(o_ref.dtype)


def _small_conv_mixer(u, x0, kfilt, ksum, bias_d):
    b, n, c = u.shape
    mf, md, mi = (jnp.asarray(t) for t in _small_dft_tables(n))
    io = pl.BlockSpec((2, n, c), lambda p: (p, 0, 0))
    return pl.pallas_call(
        functools.partial(_small_conv_kernel, n=n),
        grid=(b // 2,),
        in_specs=[_full((2 * n, c)), _full((1, c)), io, io, _full((1, c)),
                  _full(mf.shape), _full(md.shape), _full(mi.shape)],
        out_specs=io,
        out_shape=jax.ShapeDtypeStruct((b, n, c), BF16),
        compiler_params=_params(("arbitrary",)),
        name="small_conv",
    )(kfilt, ksum, u, x0, bias_d.reshape(1, c), mf, md, mi)


MXU_DIM = 256
FFN_CHUNK_EDGES = (0, 6 * MXU_DIM, D_FF)


def _gelu_tanh(x):
    return 0.5 * x * (1.0 + jnp.tanh(math.sqrt(2.0 / math.pi) * (x + 0.044715 * (x * x * x))))


def _ffn_kernel(x_ref, xp_ref, xn_ref, oa_ref, oap_ref, oan_ref, oc_ref, ocp_ref, ocn_ref, wa_ref, wc_ref, g1_ref,
                sh_ref, sc_ref, gt_ref, g_ref, wu_ref, cw_ref, cb_ref, wd_ref, o_ref):
    i = pl.program_id(1)
    nt = pl.num_programs(1)
    tm = x_ref.shape[1]

    def halo(p_ref, n_ref):
        return jnp.concatenate([p_ref[0][-SUBLANES:], n_ref[0][:SUBLANES]], axis=0)

    def mixer_out(oa, oc):
        return (jnp.dot(oa.astype(BF16), wa_ref[...], preferred_element_type=F32)
                + jnp.dot(oc.astype(BF16), wc_ref[...], preferred_element_type=F32))

    g1 = g1_ref[0]
    xm = x_ref[0] + g1 * mixer_out(oa_ref[0], oc_ref[0])
    xh = halo(xp_ref, xn_ref) + g1 * mixer_out(halo(oap_ref, oan_ref), halo(ocp_ref, ocn_ref))
    ext = jnp.concatenate([xh[:SUBLANES], xm, xh[SUBLANES:]], axis=0)
    r = tm + 2 * SUBLANES
    h = _modulated_norm(ext, g_ref[...], sh_ref[0], sc_ref[0])
    row = lax.broadcasted_iota(jnp.int32, (r, 1), 0)
    inside = jnp.logical_and(jnp.logical_or(i > 0, row >= SUBLANES), jnp.logical_or(i < nt - 1, row < tm + SUBLANES))
    h = jnp.where(inside, h, 0.0).astype(BF16)
    hm = h[SUBLANES:tm + SUBLANES]
    acc = jnp.zeros((tm, D_MODEL), F32)
    for c0, c1 in zip(FFN_CHUNK_EDGES[:-1], FFN_CHUNK_EDGES[1:]):
        a = jnp.dot(h, wu_ref[:, c0:c1], preferred_element_type=F32)
        v = jnp.dot(hm, wu_ref[:, D_FF + c0:D_FF + c1], preferred_element_type=F32)
        ap = pltpu.roll(a, 1, 0)[SUBLANES:tm + SUBLANES]
        an = pltpu.roll(a, r - 1, 0)[SUBLANES:tm + SUBLANES]
        cw = cw_ref[:, c0:c1]
        conv = ap * cw[0:1] + a[SUBLANES:tm + SUBLANES] * cw[1:2] + an * cw[2:3] + cb_ref[:, c0:c1]
        act = (_gelu_tanh(conv) * v).astype(BF16)
        acc = acc + jnp.dot(act, wd_ref[c0:c1, :], preferred_element_type=F32)
    o_ref[0] = xm + gt_ref[0] * acc


def _mixer_out_ffn(x, oa, oc, wo_bf, gate1, sh, sc, gate2, gain, wu_bf, conv_w, conv_b, wd_bf):
    bx, t, d = x.shape
    tm = min(512, t)
    vec = pl.BlockSpec((1, 1, d), lambda b, i: (b, 0, 0))
    once = pl.Buffered(1)
    return pl.pallas_call(
        _ffn_kernel,
        grid=(bx, t // tm),
        in_specs=_halo_specs(tm, t, d) + _halo_specs(tm, t, Q_W, oa.dtype) + _halo_specs(tm, t, CONV_W, oc.dtype) + [
            pl.BlockSpec((Q_W, d), lambda b, i: (0, 0)), pl.BlockSpec((CONV_W, d), lambda b, i: (1, 0)),
            vec, vec, vec, vec, _full((1, d)),
            pl.BlockSpec((d, 2 * D_FF), lambda b, i: (0, 0), pipeline_mode=once),
            _full((3, D_FF)), _full((1, D_FF)),
            pl.BlockSpec((D_FF, d), lambda b, i: (0, 0), pipeline_mode=once)],
        out_specs=pl.BlockSpec((1, tm, d), lambda b, i: (b, i, 0)),
        out_shape=jax.ShapeDtypeStruct((bx, t, d), F32),
        compiler_params=_params(("arbitrary", "arbitrary")),
        name="mixer_out_ffn",
    )(x, x, x, oa, oa, oa, oc, oc, oc, wo_bf, wo_bf, gate1, sh, sc, gate2, gain, wu_bf, conv_w,
      conv_b.reshape(1, D_FF), wd_bf)


@functools.lru_cache(maxsize=None)
def _rope_tables(t):
    pos = np.arange(t)
    n_freq = HEAD_DIM // 4
    inv = ROPE_THETA ** (-np.arange(n_freq, dtype=np.float64) / n_freq)
    ang_r = (pos // GRID_W)[:, None] * inv
    ang_c = (pos % GRID_W)[:, None] * inv
    zero = np.zeros_like(ang_r)
    cos = np.concatenate([np.cos(ang_r)] * 2 + [np.cos(ang_c)] * 2, 1)
    sa = np.concatenate([-np.sin(ang_r), zero, -np.sin(ang_c), zero], 1)
    sb = np.concatenate([zero, np.sin(ang_r), zero, np.sin(ang_c)], 1)
    tile = lambda m: np.tile(m, (1, LANES // HEAD_DIM)).astype(np.float32)
    return tile(cos), tile(sa), tile(sb)


@functools.lru_cache(maxsize=None)
def _identity_rope_tables(t):
    return np.ones((t, LANES), np.float32), np.zeros((t, LANES), np.float32), np.zeros((t, LANES), np.float32)


@functools.lru_cache(maxsize=None)
def _head_block_diag():
    hid = np.arange(Q_W) // HEAD_DIM
    return np.asarray(hid[:, None] == hid[None, :], dtype=BF16)


def kernel(x, c, ctx, c_ctx, ada_w, ada_b, norm_mix, norm_ffn, mix_w_in, mix_w_out, attn_q_norm, attn_k_norm,
           swa_sink, hy_conv_w, hy_conv_b, hy_w1, hy_b1, hy_w2, hy_b2, hy_w3, hy_b3, hy_w4, hy_freq, hy_bias_d,
           sc_conv_w, ffn_w_up, ffn_conv_w, ffn_conv_b, ffn_w_down):
    b, s, d = x.shape
    s_ctx = ctx.shape[1]
    depth = ada_w.shape[0]
    assert d == D_MODEL and b % 2 == 0 and b + 1 <= MOD_ROWS and s % 1024 == 0 and s_ctx % SUBLANES == 0

    cvec = jnp.concatenate([c, c_ctx[None, :], jnp.zeros((MOD_ROWS - b - 1, d), F32)], axis=0)
    mods = _mods(cvec, ada_w, ada_b)
    rope = [jnp.asarray(t) for t in _rope_tables(s)]
    rope_ctx = [jnp.asarray(t) for t in _identity_rope_tables(s_ctx)]
    bd = jnp.asarray(_head_block_diag())
    xc = ctx
    weights = _weights_bf16([mix_w_in, mix_w_out, ffn_w_up, ffn_w_down])

    for i in range(depth):
        last = i == depth - 1
        j = i // 2
        lat = [mods[i, :b, k * d:(k + 1) * d][:, None, :] for k in range(6)]
        cx = [jnp.broadcast_to(mods[i, b, k * d:(k + 1) * d][None, None, :], (b, 1, d)) for k in range(6)]
        w_in, w_out, w_up, w_down = (w[i] for w in weights)
        g_mix = norm_mix[i].reshape(1, d)
        g_ffn = norm_ffn[i].reshape(1, d)
        qg = jnp.tile(attn_q_norm[i], N_Q_HEADS).reshape(1, Q_W)
        kg = jnp.tile(attn_k_norm[i], N_KV_HEADS).reshape(1, KV_W)

        conv = (hy_conv_w[j], hy_conv_b[j]) if i % 2 == 0 else (sc_conv_w[j],)
        q, k, vt, *mix = _inproj(x, lat[0], lat[1], g_mix, w_in, qg, kg, bd, *rope, *conv)
        qc, kc, vct, *mixc = _inproj(xc, cx[0], cx[1], g_mix, w_in, qg, kg, bd, *rope_ctx, *conv)

        bound = _score_bound(attn_q_norm[i], attn_k_norm[i])
        if i % 2 == 0:
            fargs = (hy_w1[j], hy_b1[j], hy_w2[j], hy_b2[j], hy_w3[j], hy_b3[j], hy_w4[j], hy_freq[j])
            o_attn = _dense_attention(q, k, vt, kc, vct, bound=bound)
            kf, ks = _implicit_filter(s, *fargs)
            o_conv = _long_conv_mixer(*mix, kf, ks, hy_bias_d[j])
            if not last:
                oc_attn = _dense_attention(qc, kc, vct, bound=bound)
                kfc, ksc = _implicit_filter(s_ctx, *fargs)
                oc_conv = _small_conv_mixer(*mixc, kfc, ksc, hy_bias_d[j])
        else:
            sink = swa_sink[j].reshape(N_KV_HEADS, GROUP)
            o_attn = _banded_attention(q, k, vt, kc, vct, sink, bound)
            o_conv = mix[0]
            if not last:
                oc_attn = _dense_attention(qc, kc, vct, sink=sink)
                oc_conv = mixc[0]

        ffn = (g_ffn, w_up, ffn_conv_w[i], ffn_conv_b[i], w_down)
        x = _mixer_out_ffn(x, o_attn, o_conv, w_out, lat[2], lat[3], lat[4], lat[5], *ffn)
        if not last:
            xc = _mixer_out_ffn(xc, oc_attn, oc_conv, w_out, cx[2], cx[3], cx[4], cx[5], *ffn)
    return x
```

```python
import functools
import math

import numpy as np
import jax
import jax.numpy as jnp
from jax import lax
from jax.experimental import pallas as pl
from jax.experimental.pallas import tpu as pltpu

F32 = jnp.float32
BF16 = jnp.bfloat16
HI = lax.Precision.HIGHEST

D_MODEL = 1024
GRID_W = 64
HEAD_DIM = 64
N_Q_HEADS = 8
N_KV_HEADS = 4
GROUP = N_Q_HEADS // N_KV_HEADS
Q_W = N_Q_HEADS * HEAD_DIM
KV_W = N_KV_HEADS * HEAD_DIM
QKV_W = Q_W + 2 * KV_W
CONV_W = D_MODEL // 2
MIX_IN_W = QKV_W + 3 * CONV_W
WINDOW = 128
ROPE_THETA = 10000.0
FILT_EMB = 33
FILT_WIDTH = 64
DECAY_TARGET = 1e-2
FAST_DECAY_PCT = 0.3
SLOW_DECAY_PCT = 1.5
DECAY_SHIFT = 0.05
D_FF = 2816
NEG_INF = -1e30
RMS_EPS = 1e-6
LOG2E = 1.4426950408889634
Q_SCALE = HEAD_DIM ** -0.5 * LOG2E

LANES = 128
SUBLANES = 8
DFT_N1 = 128
VMEM_LIMIT_MB = 56
VT_ROWS = LANES
MOD_ROWS = SUBLANES
PROJ_SUB_ROWS = 128


def _params(sem, vmem_mb=VMEM_LIMIT_MB):
    return pltpu.CompilerParams(dimension_semantics=sem, vmem_limit_bytes=vmem_mb * 1024 * 1024)


def _full(shape):
    nd = len(shape)
    return pl.BlockSpec(shape, lambda *_: (0,) * nd)


def _mods_kernel(c_ref, w_ref, b_ref, o_ref):
    c = c_ref[...]
    s = c / (1.0 + jnp.exp(-c))
    o_ref[0] = jnp.dot(s, w_ref[0], preferred_element_type=F32, precision=HI) + b_ref[0]


def _mods(cvec, ada_w, ada_b):
    depth, d, n6 = ada_w.shape
    tn = 1536
    return pl.pallas_call(
        _mods_kernel,
        grid=(depth, n6 // tn),
        in_specs=[_full((MOD_ROWS, d)),
                  pl.BlockSpec((1, d, tn), lambda l, j: (l, 0, j)),
                  pl.BlockSpec((1, 1, tn), lambda l, j: (l, 0, j))],
        out_specs=pl.BlockSpec((1, MOD_ROWS, tn), lambda l, j: (l, 0, j)),
        out_shape=jax.ShapeDtypeStruct((depth, MOD_ROWS, n6), F32),
        compiler_params=_params(("arbitrary", "arbitrary")),
        name="ada_mods",
    )(cvec, ada_w, ada_b.reshape(depth, 1, n6))


def _modulated_norm(x, gain, shift, scale):
    ms = jnp.mean(x * x, axis=-1, keepdims=True)
    return (x * lax.rsqrt(ms + RMS_EPS)) * gain * (1.0 + scale) + shift


WEIGHT_CAST_STEPS = 8


def _cast_kernel(*refs):
    n = len(refs) // 2
    for w_ref, o_ref in zip(refs[:n], refs[n:]):
        o_ref[...] = w_ref[0].astype(o_ref.dtype)


def _weights_bf16(stacked):
    depth = stacked[0].shape[0]
    in_specs, out_specs, out_shape, args = [], [], [], []
    for w in stacked:
        _, rows, cols = w.shape
        tr = rows // WEIGHT_CAST_STEPS
        for layer in range(depth):
            in_specs.append(pl.BlockSpec((1, tr, cols), lambda r, layer=layer: (layer, r, 0)))
            out_specs.append(pl.BlockSpec((tr, cols), lambda r: (r, 0)))
            out_shape.append(jax.ShapeDtypeStruct((rows, cols), BF16))
            args.append(w)
    outs = pl.pallas_call(
        _cast_kernel,
        grid=(WEIGHT_CAST_STEPS,),
        in_specs=in_specs,
        out_specs=out_specs,
        out_shape=out_shape,
        compiler_params=_params(("arbitrary",)),
        name="weights_bf16",
    )(*args)
    return [outs[k * depth:(k + 1) * depth] for k in range(len(stacked))]


def _halo_specs(tm, t, width, dtype=F32):
    rows = SUBLANES * 4 // jnp.dtype(dtype).itemsize
    nb = tm // rows
    last = t // rows - 1
    return [pl.BlockSpec((1, tm, width), lambda b, i: (b, i, 0)),
            pl.BlockSpec((1, rows, width), lambda b, i: (b, jnp.maximum(i * nb - 1, 0), 0)),
            pl.BlockSpec((1, rows, width), lambda b, i: (b, jnp.minimum((i + 1) * nb, last), 0))]


def _inproj_kernel(*refs, mode):
    (x_ref, xp_ref, xn_ref, sh_ref, sc_ref, g_ref, w_ref, qg_ref, kg_ref, bd_ref,
     cos_ref, sa_ref, sb_ref, cw_ref) = refs[:14]
    if mode == "hyena":
        cb_ref, q_ref, k_ref, vt_ref, u_ref, x0_ref = refs[14:]
    else:
        q_ref, k_ref, vt_ref, oc_ref = refs[14:]
    i = pl.program_id(1)
    nt = pl.num_programs(1)
    tm = x_ref.shape[1]
    sub = min(PROJ_SUB_ROWS, tm)
    gain, shift, scale = g_ref[...], sh_ref[0], sc_ref[0]

    xh = jnp.concatenate([xp_ref[0], xn_ref[0]], axis=0)
    zh = jnp.dot(_modulated_norm(xh, gain, shift, scale).astype(BF16), w_ref[:, QKV_W:], preferred_element_type=F32)
    before = jnp.where(i > 0, zh[:SUBLANES], 0.0)
    after = jnp.where(i < nt - 1, zh[SUBLANES:], 0.0)
    cw = cw_ref[...]

    def conv_stage(ext, rows):
        n = sub + 2 * SUBLANES
        mid = slice(SUBLANES, sub + SUBLANES)
        if mode == "hyena":
            c = (pltpu.roll(ext, 1, 0)[mid] * cw[0:1] + ext[mid] * cw[1:2] + pltpu.roll(ext, n - 1, 0)[mid] * cw[2:3]
                 + cb_ref[...])
            x0_ref[0, rows] = c[:, :CONV_W]
            u_ref[0, rows] = c[:, 2 * CONV_W:] * c[:, CONV_W:2 * CONV_W]
        else:
            pr = ext[:, CONV_W:2 * CONV_W] * ext[:, 2 * CONV_W:]
            conv = pltpu.roll(pr, 1, 0)[mid] * cw[0:1] + pr[mid] * cw[1:2] + pltpu.roll(pr, n - 1, 0)[mid] * cw[2:3]
            oc_ref[0, rows] = (ext[mid, :CONV_W] * conv).astype(oc_ref.dtype)

    def head_norm_rope(t, hgain, bd, cos, sa, sb):
        ssq = jnp.dot((t * t).astype(BF16), bd, preferred_element_type=F32)
        t = t * lax.rsqrt(ssq * (1.0 / HEAD_DIM) + RMS_EPS) * hgain
        outs = []
        for j in range(t.shape[1] // LANES):
            tj = t[:, j * LANES:(j + 1) * LANES]
            outs.append(tj * cos + pltpu.roll(tj, LANES - 16, 1) * sa + pltpu.roll(tj, 16, 1) * sb)
        return outs

    held = None
    for r0 in range(0, tm, sub):
        rows = slice(r0, r0 + sub)
        h = _modulated_norm(x_ref[0, rows], gain, shift, scale)
        hb = h.astype(BF16)
        z = jnp.dot(hb, w_ref[:, :QKV_W], preferred_element_type=F32)
        zc = jnp.dot(hb, w_ref[:, QKV_W:], preferred_element_type=F32)
        if held is not None:
            conv_stage(jnp.concatenate([before, held[0], zc[:SUBLANES]], axis=0), held[1])
            before = held[0][-SUBLANES:]
        held = (zc, rows)
        tabs = (cos_ref[rows], sa_ref[rows], sb_ref[rows])
        qs = head_norm_rope(z[:, :Q_W], qg_ref[...], bd_ref[...], *tabs)
        ks = head_norm_rope(z[:, Q_W:Q_W + KV_W], kg_ref[...], bd_ref[:KV_W, :KV_W], *tabs)
        for j, qj in enumerate(qs):
            qj = (qj * Q_SCALE).astype(BF16)
            q_ref[0, 2 * j, rows] = qj[:, :HEAD_DIM]
            q_ref[0, 2 * j + 1, rows] = qj[:, HEAD_DIM:]
        for j, kj in enumerate(ks):
            kj = kj.astype(BF16)
            k_ref[0, 2 * j, rows] = kj[:, :HEAD_DIM]
            k_ref[0, 2 * j + 1, rows] = kj[:, HEAD_DIM:]
        vt = z[:, Q_W + KV_W:QKV_W].T
        ones_row = (lax.broadcasted_iota(jnp.int32, (VT_ROWS - HEAD_DIM, sub), 0) == 0).astype(F32)
        for hh in range(N_KV_HEADS):
            vt_ref[0, hh, :, rows] = jnp.concatenate([vt[hh * HEAD_DIM:(hh + 1) * HEAD_DIM], ones_row],
                                                     axis=0).astype(BF16)

    conv_stage(jnp.concatenate([before, held[0], after], axis=0), held[1])


def _inproj(x, sh, sc, gain, w_bf, qg, kg, bd, cos, sa, sb, conv_w, conv_b=None):
    bx, t, d = x.shape
    tm = min(1024, t)
    mode = "hyena" if conv_b is not None else "short"
    vec = pl.BlockSpec((1, 1, d), lambda b, i: (b, 0, 0))
    tab = pl.BlockSpec((tm, LANES), lambda b, i: (i, 0))
    row = lambda w: pl.BlockSpec((1, tm, w), lambda b, i: (b, i, 0))
    in_specs = _halo_specs(tm, t, d) + [vec, vec, _full((1, d)), _full((d, MIX_IN_W)), _full((1, Q_W)),
                                        _full((1, KV_W)), _full((Q_W, Q_W)), tab, tab, tab, _full(conv_w.shape)]
    args = [x, x, x, sh, sc, gain, w_bf, qg, kg, bd, cos, sa, sb, conv_w]
    out_specs = [pl.BlockSpec((1, N_Q_HEADS, tm, HEAD_DIM), lambda b, i: (b, 0, i, 0)),
                 pl.BlockSpec((1, N_KV_HEADS, tm, HEAD_DIM), lambda b, i: (b, 0, i, 0)),
                 pl.BlockSpec((1, N_KV_HEADS, VT_ROWS, tm), lambda b, i: (b, 0, 0, i))]
    out_shape = [jax.ShapeDtypeStruct((bx, N_Q_HEADS, t, HEAD_DIM), BF16),
                 jax.ShapeDtypeStruct((bx, N_KV_HEADS, t, HEAD_DIM), BF16),
                 jax.ShapeDtypeStruct((bx, N_KV_HEADS, VT_ROWS, t), BF16)]
    if mode == "hyena":
        in_specs.append(_full((1, 3 * CONV_W)))
        args.append(conv_b.reshape(1, 3 * CONV_W))
        out_specs += [row(CONV_W), row(CONV_W)]
        out_shape += [jax.ShapeDtypeStruct((bx, t, CONV_W), F32)] * 2
    else:
        out_specs.append(row(CONV_W))
        out_shape.append(jax.ShapeDtypeStruct((bx, t, CONV_W), BF16))
    return pl.pallas_call(
        functools.partial(_inproj_kernel, mode=mode),
        grid=(bx, t // tm),
        in_specs=in_specs,
        out_specs=out_specs,
        out_shape=out_shape,
        compiler_params=_params(("arbitrary", "arbitrary")),
        name="in_proj",
    )(*args)


def _score_bound(q_gain, k_gain):
    b = HEAD_DIM * Q_SCALE * BOUND_MARGIN * jnp.max(jnp.abs(q_gain)) * jnp.max(jnp.abs(k_gain))
    return b.reshape(1, 1).astype(F32)


def _scores_t(ks, q):
    return lax.dot_general(ks, q, (((1,), (1,)), ((), ())), preferred_element_type=F32)


def _sink_column(sink_ref, hh, tq, m, acc):
    col = lax.broadcasted_iota(jnp.int32, m.shape, 1)
    sink = jnp.where(col < tq, sink_ref[hh, 0], sink_ref[hh, 1]) * LOG2E
    m_new = jnp.maximum(m, sink)
    den_row = lax.broadcasted_iota(jnp.int32, acc.shape, 0) == HEAD_DIM
    return acc * jnp.exp2(m - m_new) + jnp.where(den_row, jnp.exp2(sink - m_new), 0.0)


def _attn_finish(acc, tq):
    o = (acc / acc[HEAD_DIM:HEAD_DIM + 1]).T
    return jnp.concatenate([o[:tq, :HEAD_DIM], o[tq:, :HEAD_DIM]], axis=1)


EXP_CHUNK_ELEMS = 32 * SUBLANES * LANES


def _score_stage(ks, q, s_ref, bias=None):
    s = _scores_t(ks, q)
    if bias is not None:
        s = s + bias
    s_ref[0:s.shape[0]] = s
    return jnp.max(s, axis=0, keepdims=True)


def _softmax_stage(s_ref, p_ref, n, vt, mx, m, acc_ref):
    m_new = jnp.maximum(m, mx)
    rows = EXP_CHUNK_ELEMS // m.shape[1]
    for c in range(0, n, rows):
        p_ref[c:c + rows] = jnp.exp2(s_ref[c:c + rows] - m_new).astype(BF16)
    acc_ref[...] = acc_ref[...] * jnp.exp2(m - m_new) + jnp.dot(vt, p_ref[0:n], preferred_element_type=F32)
    return m_new


BOUNDED_TILES = 8
BOUND_MARGIN = 1.01
SAFE_SHIFT = 40.0


def _flash_exact(q, k_tile, vt_tile, ctx, sa_ref, sb_ref, p_ref, acc_ref, *, tk, n_main):
    pa_ref, pb_ref = p_ref.at[0:tk], p_ref.at[tk:2 * tk]
    acc_ref[...] = jnp.zeros_like(acc_ref)
    m = jnp.full((1, q.shape[0]), NEG_INF, F32)
    mx_a = _score_stage(k_tile(0), q, sa_ref)
    if n_main > 1:
        def body(i, carry):
            m, mx_a = carry
            mx_b = _score_stage(k_tile(2 * i + 1), q, sb_ref)
            m = _softmax_stage(sa_ref, pa_ref, tk, vt_tile(2 * i), mx_a, m, acc_ref)
            mx_a = _score_stage(k_tile(2 * i + 2), q, sa_ref)
            m = _softmax_stage(sb_ref, pb_ref, tk, vt_tile(2 * i + 1), mx_b, m, acc_ref)
            return m, mx_a

        m, mx_a = lax.fori_loop(0, n_main // 2 - 1, body, (m, mx_a))
        mx_b = _score_stage(k_tile(n_main - 1), q, sb_ref)
        m = _softmax_stage(sa_ref, pa_ref, tk, vt_tile(n_main - 2), mx_a, m, acc_ref)
        last = (sb_ref, pb_ref, mx_b)
        spare = (sa_ref, pa_ref)
    else:
        last = (sa_ref, pa_ref, mx_a)
        spare = (sb_ref, pb_ref)
    if ctx is not None:
        mx_c = _score_stage(ctx[0], q, spare[0])
    m = _softmax_stage(last[0], last[1], tk, vt_tile(n_main - 1), last[2], m, acc_ref)
    if ctx is not None:
        m = _softmax_stage(spare[0], spare[1], ctx[0].shape[0], ctx[1], mx_c, m, acc_ref)
    return m


def _exp2_bf16(s):
    return jnp.exp2(s).astype(BF16)


def _flash_bounded(q, k_ref, vt_ref, ctx, p_ref, acc_ref, *, tk, n_main):
    step = BOUNDED_TILES * tk if n_main % BOUNDED_TILES == 0 else tk

    def group(j):
        off = pl.multiple_of(j * step, step)
        for c in range(0, step, tk):
            p_ref[c:c + tk] = _exp2_bf16(_scores_t(k_ref[0, 0, pl.ds(off + c, tk), :], q))
        return jnp.dot(vt_ref[0, 0, :, pl.ds(off, step)], p_ref[0:step], preferred_element_type=F32)

    first = group(0)
    if ctx is not None:
        pc_ref = p_ref.at[BOUNDED_TILES * tk:BOUNDED_TILES * tk + ctx[0].shape[0]]
        pc_ref[...] = _exp2_bf16(_scores_t(ctx[0], q))
        first = first + jnp.dot(ctx[1], pc_ref[...], preferred_element_type=F32)
    acc_ref[...] = first

    def body(j, carry):
        acc_ref[...] += group(j)
        return carry

    lax.fori_loop(1, n_main * tk // step, body, 0)


def _flash_kernel(*refs, tq, tk, n_main, has_ctx, has_sink):
    sa_ref, sb_ref, p_ref, acc_ref = refs[-4:]
    o_ref = refs[-5]
    refs = list(refs[:-5])
    sink_ref = refs.pop(0) if has_sink else None
    bound_ref = None if has_sink else refs.pop(0)
    q_ref, k_ref, vt_ref = refs[:3]
    m_cols = GROUP * tq
    q = q_ref[0].reshape(m_cols, HEAD_DIM)
    ctx = (refs[3][0, 0], refs[4][0, 0]) if has_ctx else None

    def k_tile(j):
        return k_ref[0, 0, pl.ds(pl.multiple_of(j * tk, tk), tk), :]

    def vt_tile(j):
        return vt_ref[0, 0, :, pl.ds(pl.multiple_of(j * tk, tk), tk)]

    exact = functools.partial(_flash_exact, q, k_tile, vt_tile, ctx, sa_ref, sb_ref, p_ref, acc_ref,
                              tk=tk, n_main=n_main)
    if has_sink:
        acc = _sink_column(sink_ref, pl.program_id(1), tq, exact(), acc_ref[...])
    else:
        bound = bound_ref[0, 0]
        safe = bound <= SAFE_SHIFT

        @pl.when(safe)
        def _():
            _flash_bounded(q, k_ref, vt_ref, ctx, p_ref, acc_ref, tk=tk, n_main=n_main)

        @pl.when(jnp.logical_not(safe))
        def _():
            exact()

        acc = acc_ref[...]
    o_ref[0] = _attn_finish(acc, tq).astype(o_ref.dtype)


def _kv_specs(n):
    return [pl.BlockSpec((1, 1, n, HEAD_DIM), lambda b, h, i: (b, h, 0, 0)),
            pl.BlockSpec((1, 1, VT_ROWS, n), lambda b, h, i: (b, h, 0, 0))]


def _dense_attention(q, k, vt, kc=None, vct=None, *, bound=None, sink=None):
    bx, _, t, _ = q.shape
    s = k.shape[2]
    tq = min(1024, t)
    tk = min(512, s)
    has_ctx = kc is not None
    has_sink = sink is not None
    in_specs = [pl.BlockSpec((1, GROUP, tq, HEAD_DIM), lambda b, h, i: (b, h, i, 0))] + _kv_specs(s)
    args = [q, k, vt]
    if has_ctx:
        in_specs += _kv_specs(kc.shape[2])
        args += [kc, vct]
    in_specs = [pl.BlockSpec(memory_space=pltpu.SMEM)] + in_specs
    args = [sink if has_sink else bound] + args
    n_main = s // tk
    assert n_main == 1 or n_main % 2 == 0
    n_ctx = kc.shape[2] if has_ctx else 0
    assert n_ctx <= tk
    m_cols = GROUP * tq
    return pl.pallas_call(
        functools.partial(_flash_kernel, tq=tq, tk=tk, n_main=n_main, has_ctx=has_ctx, has_sink=has_sink),
        grid=(bx, N_KV_HEADS, t // tq),
        in_specs=in_specs,
        out_specs=pl.BlockSpec((1, tq, LANES), lambda b, h, i: (b, i, h)),
        out_shape=jax.ShapeDtypeStruct((bx, t, Q_W), BF16),
        scratch_shapes=[pltpu.VMEM((tk, m_cols), F32), pltpu.VMEM((tk, m_cols), F32),
                        pltpu.VMEM((BOUNDED_TILES * tk + n_ctx, m_cols), BF16), pltpu.VMEM((VT_ROWS, m_cols), F32)],
        compiler_params=_params(("arbitrary", "arbitrary", "arbitrary")),
        name="dense_attention",
    )(*args)


BAND_SUB = 256


@functools.lru_cache(maxsize=None)
def _band_bias():
    span = BAND_SUB + 2 * WINDOW
    kr = np.arange(span)[None, :, None]
    qc = (np.arange(GROUP * BAND_SUB) % BAND_SUB)[None, None, :]
    rel = np.arange(3)[:, None, None]
    return np.where(np.abs(kr - qc - rel * WINDOW) <= WINDOW, 0.0, NEG_INF).astype(np.float32)


def _banded_kernel(sink_ref, bound_ref, q_ref, k_ref, vt_ref, kc_ref, vct_ref, bias_ref, o_ref,
                   sl_ref, sc_ref, pl_ref, pc_ref, acc_ref, *, n_sub, s_len):
    i = pl.program_id(2)
    hh = pl.program_id(1)
    sub = BAND_SUB
    m_cols = GROUP * sub
    span = sub + 2 * WINDOW
    n_ctx = kc_ref.shape[2]

    def window(u):
        q = q_ref[0, :, u * sub:(u + 1) * sub, :].reshape(m_cols, HEAD_DIM)
        q0 = (i * n_sub + u) * sub
        start = pl.multiple_of(jnp.clip(q0 - WINDOW, 0, s_len - span), WINDOW)
        return q, start, bias_ref[(q0 - start) // WINDOW]

    shift = jnp.maximum(bound_ref[0, 0], jnp.maximum(sink_ref[hh, 0], sink_ref[hh, 1]) * LOG2E)
    safe = shift <= SAFE_SHIFT

    @pl.when(safe)
    def _():
        col = lax.broadcasted_iota(jnp.int32, (1, m_cols), 1)
        sink = jnp.where(col < sub, sink_ref[hh, 0], sink_ref[hh, 1]) * LOG2E
        den_row = lax.broadcasted_iota(jnp.int32, (VT_ROWS, m_cols), 0) == HEAD_DIM
        sink_den = jnp.where(den_row, jnp.exp2(sink), 0.0)
        for u in range(n_sub):
            q, start, bias = window(u)
            pl_u, pc_u = pl_ref.at[u], pc_ref.at[u]
            pl_u[...] = _exp2_bf16(_scores_t(k_ref[0, 0, pl.ds(start, span), :], q) + bias)
            pc_u[...] = _exp2_bf16(_scores_t(kc_ref[0, 0], q))
            acc = (jnp.dot(vt_ref[0, 0, :, pl.ds(start, span)], pl_u[...], preferred_element_type=F32)
                   + jnp.dot(vct_ref[0, 0], pc_u[...], preferred_element_type=F32) + sink_den)
            o_ref[0, u * sub:(u + 1) * sub] = _attn_finish(acc, sub).astype(o_ref.dtype)

    @pl.when(jnp.logical_not(safe))
    def _():
        stats = []
        for u in range(n_sub):
            q, start, bias = window(u)
            mx_l = _score_stage(k_ref[0, 0, pl.ds(start, span), :], q, sl_ref.at[u], bias=bias)
            mx_c = _score_stage(kc_ref[0, 0], q, sc_ref.at[u])
            stats.append((start, mx_l, mx_c))
        for u in range(n_sub):
            start, mx_l, mx_c = stats[u]
            acc_u = acc_ref.at[u]
            acc_u[...] = jnp.zeros((VT_ROWS, m_cols), F32)
            m = jnp.full((1, m_cols), NEG_INF, F32)
            m = _softmax_stage(sl_ref.at[u], pl_ref.at[u], span, vt_ref[0, 0, :, pl.ds(start, span)], mx_l, m, acc_u)
            m = _softmax_stage(sc_ref.at[u], pc_ref.at[u], n_ctx, vct_ref[0, 0], mx_c, m, acc_u)
            acc = _sink_column(sink_ref, hh, sub, m, acc_u[...])
            o_ref[0, u * sub:(u + 1) * sub] = _attn_finish(acc, sub).astype(o_ref.dtype)


def _banded_attention(q, k, vt, kc, vct, sink, bound):
    bx, _, t, _ = q.shape
    n_ctx = kc.shape[2]
    n_sub = next(k for k in (8, 4, 2, 1) if t % (k * BAND_SUB) == 0)
    tq = n_sub * BAND_SUB
    m_cols = GROUP * BAND_SUB
    span = BAND_SUB + 2 * WINDOW
    bias = jnp.asarray(_band_bias())
    return pl.pallas_call(
        functools.partial(_banded_kernel, n_sub=n_sub, s_len=t),
        grid=(bx, N_KV_HEADS, t // tq),
        in_specs=[pl.BlockSpec(memory_space=pltpu.SMEM), pl.BlockSpec(memory_space=pltpu.SMEM),
                  pl.BlockSpec((1, GROUP, tq, HEAD_DIM), lambda b, h, i: (b, h, i, 0))]
                 + _kv_specs(t) + _kv_specs(n_ctx) + [_full(bias.shape)],
        out_specs=pl.BlockSpec((1, tq, LANES), lambda b, h, i: (b, i, h)),
        out_shape=jax.ShapeDtypeStruct((bx, t, Q_W), BF16),
        scratch_shapes=[pltpu.VMEM((n_sub, span, m_cols), F32), pltpu.VMEM((n_sub, n_ctx, m_cols), F32),
                        pltpu.VMEM((n_sub, span, m_cols), BF16), pltpu.VMEM((n_sub, n_ctx, m_cols), BF16),
                        pltpu.VMEM((n_sub, VT_ROWS, m_cols), F32)],
        compiler_params=_params(("arbitrary", "arbitrary", "arbitrary")),
        name="banded_attention",
    )(sink, bound, q, k, vt, kc, vct, bias)


@functools.lru_cache(maxsize=None)
def _filter_features(n):
    j = np.arange(2 * n)
    d = np.where(j <= n, j, 2 * n - j)
    d = np.where(j == n, 0, d)
    bands = (FILT_EMB - 1) // 2
    t01 = np.linspace(0.0, 1.0, n)[d]
    w = 2.0 * np.pi * d.astype(np.float64) / n
    f = np.linspace(1e-4, bands - 1, bands)[None, :]
    feats = np.zeros((2 * n, LANES), np.float64)
    feats[:, 0] = t01
    feats[:, 1:1 + bands] = np.cos(f * w[:, None])
    feats[:, 1 + bands:FILT_EMB] = -np.sin(f * w[:, None])
    feats[:, 64] = t01
    feats[:, 65] = (j < n)
    feats[:, 66] = (j != n)
    return feats.astype(np.float32)


def _dot_bf16x3(a, b):
    a_hi = a.astype(BF16)
    a_lo = (a - a_hi.astype(F32)).astype(BF16)
    b_hi = b.astype(BF16)
    b_lo = (b - b_hi.astype(F32)).astype(BF16)
    dot = lambda x, y: jnp.dot(x, y, preferred_element_type=F32)
    return dot(a_hi, b_hi) + dot(a_hi, b_lo) + dot(a_lo, b_hi)


def _filter_kernel(f_ref, w1_ref, b1_ref, w2_ref, b2_ref, w3_ref, b3_ref, w4_ref, fr_ref, dl_ref, k_ref, s_ref):
    f = f_ref[...]
    fr = fr_ref[...]
    mm = lambda a, b: jnp.dot(a, b, preferred_element_type=F32, precision=HI)
    h = jnp.sin(fr * (mm(w1_ref[...], f) + b1_ref[...]))
    h = jnp.sin(fr * (mm(w2_ref[...], h) + b2_ref[...]))
    h = jnp.sin(fr * (mm(w3_ref[...], h) + b3_ref[...]))
    hf = _dot_bf16x3(w4_ref[...], h)
    win = jnp.exp(-dl_ref[...] * f[64:65]) + DECAY_SHIFT
    k = (jnp.where(f[65:66] > 0.5, hf[:CONV_W], hf[CONV_W:]) * win * f[66:67]).T
    k_ref[...] = k

    @pl.when(pl.program_id(0) == 0)
    def _():
        s_ref[...] = jnp.zeros_like(s_ref)

    s_ref[...] += jnp.sum(jnp.abs(k), axis=0, keepdims=True)


def _implicit_filter(n, w1, b1, w2, b2, w3, b3, w4, freq):
    feats_t = jnp.asarray(np.ascontiguousarray(_filter_features(n).T))
    col = lambda a: a.reshape(-1, 1)
    w1t = jnp.pad(w1.T, ((0, 0), (0, LANES - FILT_EMB)))
    deltas = np.abs(np.linspace(math.log(DECAY_TARGET) / SLOW_DECAY_PCT, math.log(DECAY_TARGET) / FAST_DECAY_PCT,
                                CONV_W)).astype(np.float32).reshape(CONV_W, 1)
    tr = min(1024, 2 * n)
    sq = _full((FILT_WIDTH, FILT_WIDTH))
    vec = _full((FILT_WIDTH, 1))
    return pl.pallas_call(
        _filter_kernel,
        grid=(2 * n // tr,),
        in_specs=[pl.BlockSpec((LANES, tr), lambda i: (0, i)), _full((FILT_WIDTH, LANES)), vec, sq, vec, sq, vec,
                  _full((2 * CONV_W, FILT_WIDTH)), vec, _full((CONV_W, 1))],
        out_specs=[pl.BlockSpec((tr, CONV_W), lambda i: (i, 0)), _full((1, CONV_W))],
        out_shape=[jax.ShapeDtypeStruct((2 * n, CONV_W), F32), jax.ShapeDtypeStruct((1, CONV_W), F32)],
        compiler_params=_params(("arbitrary",)),
        name="hyena_filter",
    )(feats_t, w1t, col(b1), w2.T, col(b2), w3.T, col(b3), w4.T, col(freq), jnp.asarray(deltas))


def _twiddle(idx, mod):
    ang = 2.0 * np.pi * (idx % mod) / mod
    return np.cos(ang), -np.sin(ang)


def _real_form(mr, mi):
    return np.concatenate([np.concatenate([mr, -mi], -1), np.concatenate([mi, mr], -1)], -2)


@functools.lru_cache(maxsize=None)
def _dft_tables(n2):
    n1 = DFT_N1
    n = n1 * n2
    h = n2 // 2
    a2 = np.arange(n2)
    fr, fi = _twiddle(np.outer(a2, a2), n2)
    m_data = _real_form(fr[:, :h], fi[:, :h])
    m_filt = np.concatenate([fr, fi], 0)
    k2 = a2[:, None, None]
    k1 = np.arange(n1)[None, :, None]
    c1 = np.arange(n1)[None, None, :]
    g = _real_form(*_twiddle(c1 * (n2 * k1 + k2), n))
    a1 = np.arange(n1)
    f1 = _real_form(*_twiddle(np.outer(a1, a1), n1))
    t2 = a1[:, None, None]
    t1 = np.arange(h)[None, :, None]
    j1 = a2[None, None, :]
    hh = _real_form(*_twiddle(j1 * (n1 * t1 + t2), n))
    cast = lambda m: np.asarray(m, dtype=BF16)
    return cast(m_data), cast(m_filt), cast(g), cast(f1), cast(hh)


DFT_ROW_CHUNK = 2 * SUBLANES


def _dft_rows_kernel(m_ref, *refs):
    o_ref = refs[-1]
    cols = [jnp.concatenate([r[0, :, t, :] for r in refs[:-1]], axis=0) for t in range(DFT_ROW_CHUNK)]
    rhs = jnp.concatenate(cols, axis=1).astype(BF16)
    out = jnp.dot(m_ref[...], rhs, preferred_element_type=F32)
    half = out.shape[0] // 2
    o_ref[0, 0] = out[:half].astype(o_ref.dtype)
    o_ref[0, 1] = out[half:].astype(o_ref.dtype)


def _dft_stage1(mat, views, pairs, rows, n2, c):
    nin = len(views)
    lc = DFT_ROW_CHUNK * c
    in_specs = [_full(mat.shape)] + [
        pl.BlockSpec((1, rows, DFT_ROW_CHUNK, c), (lambda p, j, a=a: (nin * p + a, 0, j, 0))) for a in range(nin)]
    return pl.pallas_call(
        _dft_rows_kernel,
        grid=(pairs, DFT_N1 // DFT_ROW_CHUNK),
        in_specs=in_specs,
        out_specs=pl.BlockSpec((1, 2, n2, lc), lambda p, j: (p, 0, 0, j)),
        out_shape=jax.ShapeDtypeStruct((pairs, 2, n2, DFT_N1 * c), BF16),
        compiler_params=_params(("arbitrary", "arbitrary")),
        name="dft_stage1",
    )(mat, *views)


def _spectrum_kernel(a_ref, g_ref, sc_ref, o_ref, *, kb):
    for j in range(kb):
        rhs = jnp.concatenate([a_ref[0, 0, j], a_ref[0, 1, j]], axis=0)
        x = jnp.dot(g_ref[j], rhs, preferred_element_type=F32)
        o_ref[0, j] = (x[:DFT_N1] * sc_ref[...]).astype(o_ref.dtype)
        o_ref[1, j] = (x[DFT_N1:] * sc_ref[...]).astype(o_ref.dtype)


def _filter_spectrum(a, g, scale, n2, kb):
    c = a.shape[-1]
    return pl.pallas_call(
        functools.partial(_spectrum_kernel, kb=kb),
        grid=(n2 // kb,),
        in_specs=[pl.BlockSpec((1, 2, kb, DFT_N1, c), lambda k: (0, 0, k, 0, 0)),
                  pl.BlockSpec((kb, 2 * DFT_N1, 2 * DFT_N1), lambda k: (k, 0, 0)), _full((1, c))],
        out_specs=pl.BlockSpec((2, kb, DFT_N1, c), lambda k: (0, k, 0, 0)),
        out_shape=jax.ShapeDtypeStruct((2, n2, DFT_N1, c), BF16),
        compiler_params=_params(("arbitrary",)),
        name="filter_spectrum",
    )(a, g, scale)


def _dft_mid_kernel(a_ref, g_ref, kh_ref, f_ref, o_ref, *, kb):
    for j in range(kb):
        rhs = jnp.concatenate([a_ref[0, 0, j], a_ref[0, 1, j]], axis=0)
        x = jnp.dot(g_ref[j], rhs, preferred_element_type=F32)
        xr, xi = x[:DFT_N1], x[DFT_N1:]
        kr, ki = kh_ref[0, j].astype(F32), kh_ref[1, j].astype(F32)
        yr = xr * kr - xi * ki
        yi = xr * ki + xi * kr
        v = jnp.concatenate([yr, -yi], axis=0).astype(BF16)
        b = jnp.dot(f_ref[...], v, preferred_element_type=F32)
        o_ref[0, 0, :, j, :] = b[:DFT_N1]
        o_ref[0, 1, :, j, :] = b[DFT_N1:]


def _dft_mid(a, g, khat, f1, pairs, n2, kb):
    c = a.shape[-1]
    blk = pl.BlockSpec((1, 2, kb, DFT_N1, c), lambda k, p: (p, 0, k, 0, 0))
    return pl.pallas_call(
        functools.partial(_dft_mid_kernel, kb=kb),
        grid=(n2 // kb, pairs),
        in_specs=[blk, pl.BlockSpec((kb, 2 * DFT_N1, 2 * DFT_N1), lambda k, p: (k, 0, 0)),
                  pl.BlockSpec((2, kb, DFT_N1, c), lambda k, p: (0, k, 0, 0)), _full(f1.shape)],
        out_specs=pl.BlockSpec((1, 2, DFT_N1, kb, c), lambda k, p: (p, 0, 0, k, 0)),
        out_shape=jax.ShapeDtypeStruct((pairs, 2, DFT_N1, n2, c), F32),
        compiler_params=_params(("arbitrary", "arbitrary")),
        name="dft_mid",
    )(a, g, khat, f1)


def _dft_last_kernel(b_ref, h_ref, u_ref, x0_ref, bd_ref, o_ref, *, c):
    half = h_ref.shape[1] // 2
    bd = bd_ref[...]
    for t in range(DFT_ROW_CHUNK):
        rhs = jnp.concatenate([b_ref[0, 0, t], b_ref[0, 1, t]], axis=0).astype(BF16)
        v = jnp.dot(h_ref[t], rhs, preferred_element_type=F32)
        o_ref[0, :, t, :] = (v[:half] + u_ref[0, :, t, :] * bd) * x0_ref[0, :, t, :]
        o_ref[1, :, t, :] = (-v[half:] + u_ref[1, :, t, :] * bd) * x0_ref[1, :, t, :]


def _dft_last(bm, hh, u_view, x0_view, bias_d, pairs, n2, c):
    half = n2 // 2
    tc = DFT_ROW_CHUNK
    io = pl.BlockSpec((2, half, tc, c), lambda p, j: (p, 0, j, 0))
    return pl.pallas_call(
        functools.partial(_dft_last_kernel, c=c),
        grid=(pairs, DFT_N1 // tc),
        in_specs=[pl.BlockSpec((1, 2, tc, n2, c), lambda p, j: (p, 0, j, 0, 0)),
                  pl.BlockSpec((tc, n2, 2 * n2), lambda p, j: (j, 0, 0)), io, io, _full((1, c))],
        out_specs=io,
        out_shape=jax.ShapeDtypeStruct((2 * pairs, half, DFT_N1, c), F32),
        compiler_params=_params(("arbitrary", "arbitrary")),
        name="dft_last",
    )(bm, hh, u_view, x0_view, bias_d)


def _long_conv_mixer(u, x0, kfilt, ksum, bias_d):
    b, n, c = u.shape
    n2 = 2 * n // DFT_N1
    half = n2 // 2
    pairs = b // 2
    m_data, m_filt, g, f1, hh = (jnp.asarray(t) for t in _dft_tables(n2))
    kb = min(16, n2)
    scale = 1.0 / (ksum * float(DFT_N1 * n2))
    ka = _dft_stage1(m_filt, [kfilt.reshape(1, n2, DFT_N1, c)], 1, n2, n2, c)
    khat = _filter_spectrum(ka.reshape(1, 2, n2, DFT_N1, c), g, scale, n2, kb)
    u_view = u.reshape(b, half, DFT_N1, c)
    a = _dft_stage1(m_data, [u_view, u_view], pairs, half, n2, c)
    bm = _dft_mid(a.reshape(pairs, 2, n2, DFT_N1, c), g, khat, f1, pairs, n2, kb)
    out = _dft_last(bm, hh, u_view, x0.reshape(b, half, DFT_N1, c),
                    bias_d.reshape(1, c), pairs, n2, c)
    return out.reshape(b, n, c)


@functools.lru_cache(maxsize=None)
def _small_dft_tables(n):
    big = 2 * n
    a = np.arange(big)
    fr, fi = _twiddle(np.outer(a, a), big)
    cast = lambda m: np.asarray(m, dtype=BF16)
    return (cast(np.concatenate([fr, fi], 0)),
            cast(_real_form(fr[:, :n], fi[:, :n])),
            cast(_real_form(fr[:n], fi[:n])))


def _small_conv_kernel(k_ref, ks_ref, u_ref, x0_ref, bd_ref, mf_ref, md_ref, mi_ref, o_ref, *, n):
    big = 2 * n
    kh = jnp.dot(mf_ref[...], k_ref[...].astype(BF16), preferred_element_type=F32) * (1.0 / (ks_ref[...] * big))
    kr, ki = kh[:big], kh[big:]
    rhs = jnp.concatenate([u_ref[0], u_ref[1]], axis=0).astype(BF16)
    x = jnp.dot(md_ref[...], rhs, preferred_element_type=F32)
    xr, xi = x[:big], x[big:]
    v = jnp.concatenate([xr * kr - xi * ki, -(xr * ki + xi * kr)], axis=0).astype(BF16)
    y = jnp.dot(mi_ref[...], v, preferred_element_type=F32)
    bd = bd_ref[...]
    o_ref[0] = ((y[:n] + u_ref[0] * bd) * x0_ref[0]).astype(o_ref.dtype)
    o_ref[1] = ((-y[n:] + u_ref[1] * bd) * x0_ref[1]).astype(o_ref.dtype)


def _small_conv_mixer(u, x0, kfilt, ksum, bias_d):
    b, n, c = u.shape
    mf, md, mi = (jnp.asarray(t) for t in _small_dft_tables(n))
    io = pl.BlockSpec((2, n, c), lambda p: (p, 0, 0))
    return pl.pallas_call(
        functools.partial(_small_conv_kernel, n=n),
        grid=(b // 2,),
        in_specs=[_full((2 * n, c)), _full((1, c)), io, io, _full((1, c)),
                  _full(mf.shape), _full(md.shape), _full(mi.shape)],
        out_specs=io,
        out_shape=jax.ShapeDtypeStruct((b, n, c), BF16),
        compiler_params=_params(("arbitrary",)),
        name="small_conv",
    )(kfilt, ksum, u, x0, bias_d.reshape(1, c), mf, md, mi)


MXU_DIM = 256
FFN_CHUNK_EDGES = (0, 6 * MXU_DIM, D_FF)


def _gelu_tanh(x):
    return 0.5 * x * (1.0 + jnp.tanh(math.sqrt(2.0 / math.pi) * (x + 0.044715 * (x * x * x))))


def _ffn_kernel(x_ref, xp_ref, xn_ref, oa_ref, oap_ref, oan_ref, oc_ref, ocp_ref, ocn_ref, wa_ref, wc_ref, g1_ref,
                sh_ref, sc_ref, gt_ref, g_ref, wu_ref, cw_ref, cb_ref, wd_ref, o_ref):
    i = pl.program_id(1)
    nt = pl.num_programs(1)
    tm = x_ref.shape[1]

    def halo(p_ref, n_ref):
        return jnp.concatenate([p_ref[0][-SUBLANES:], n_ref[0][:SUBLANES]], axis=0)

    def mixer_out(oa, oc):
        return (jnp.dot(oa.astype(BF16), wa_ref[...], preferred_element_type=F32)
                + jnp.dot(oc.astype(BF16), wc_ref[...], preferred_element_type=F32))

    g1 = g1_ref[0]
    xm = x_ref[0] + g1 * mixer_out(oa_ref[0], oc_ref[0])
    xh = halo(xp_ref, xn_ref) + g1 * mixer_out(halo(oap_ref, oan_ref), halo(ocp_ref, ocn_ref))
    ext = jnp.concatenate([xh[:SUBLANES], xm, xh[SUBLANES:]], axis=0)
    r = tm + 2 * SUBLANES
    h = _modulated_norm(ext, g_ref[...], sh_ref[0], sc_ref[0])
    row = lax.broadcasted_iota(jnp.int32, (r, 1), 0)
    inside = jnp.logical_and(jnp.logical_or(i > 0, row >= SUBLANES), jnp.logical_or(i < nt - 1, row < tm + SUBLANES))
    h = jnp.where(inside, h, 0.0).astype(BF16)
    hm = h[SUBLANES:tm + SUBLANES]
    acc = jnp.zeros((tm, D_MODEL), F32)
    for c0, c1 in zip(FFN_CHUNK_EDGES[:-1], FFN_CHUNK_EDGES[1:]):
        a = jnp.dot(h, wu_ref[:, c0:c1], preferred_element_type=F32)
        v = jnp.dot(hm, wu_ref[:, D_FF + c0:D_FF + c1], preferred_element_type=F32)
        ap = pltpu.roll(a, 1, 0)[SUBLANES:tm + SUBLANES]
        an = pltpu.roll(a, r - 1, 0)[SUBLANES:tm + SUBLANES]
        cw = cw_ref[:, c0:c1]
        conv = ap * cw[0:1] + a[SUBLANES:tm + SUBLANES] * cw[1:2] + an * cw[2:3] + cb_ref[:, c0:c1]
        act = (_gelu_tanh(conv) * v).astype(BF16)
        acc = acc + jnp.dot(act, wd_ref[c0:c1, :], preferred_element_type=F32)
    o_ref[0] = xm + gt_ref[0] * acc


def _mixer_out_ffn(x, oa, oc, wo_bf, gate1, sh, sc, gate2, gain, wu_bf, conv_w, conv_b, wd_bf):
    bx, t, d = x.shape
    tm = min(512, t)
    vec = pl.BlockSpec((1, 1, d), lambda b, i: (b, 0, 0))
    once = pl.Buffered(1)
    return pl.pallas_call(
        _ffn_kernel,
        grid=(bx, t // tm),
        in_specs=_halo_specs(tm, t, d) + _halo_specs(tm, t, Q_W, oa.dtype) + _halo_specs(tm, t, CONV_W, oc.dtype) + [
            pl.BlockSpec((Q_W, d), lambda b, i: (0, 0)), pl.BlockSpec((CONV_W, d), lambda b, i: (1, 0)),
            vec, vec, vec, vec, _full((1, d)),
            pl.BlockSpec((d, 2 * D_FF), lambda b, i: (0, 0), pipeline_mode=once),
            _full((3, D_FF)), _full((1, D_FF)),
            pl.BlockSpec((D_FF, d), lambda b, i: (0, 0), pipeline_mode=once)],
        out_specs=pl.BlockSpec((1, tm, d), lambda b, i: (b, i, 0)),
        out_shape=jax.ShapeDtypeStruct((bx, t, d), F32),
        compiler_params=_params(("arbitrary", "arbitrary")),
        name="mixer_out_ffn",
    )(x, x, x, oa, oa, oa, oc, oc, oc, wo_bf, wo_bf, gate1, sh, sc, gate2, gain, wu_bf, conv_w,
      conv_b.reshape(1, D_FF), wd_bf)


@functools.lru_cache(maxsize=None)
def _rope_tables(t):
    pos = np.arange(t)
    n_freq = HEAD_DIM // 4
    inv = ROPE_THETA ** (-np.arange(n_freq, dtype=np.float64) / n_freq)
    ang_r = (pos // GRID_W)[:, None] * inv
    ang_c = (pos % GRID_W)[:, None] * inv
    zero = np.zeros_like(ang_r)
    cos = np.concatenate([np.cos(ang_r)] * 2 + [np.cos(ang_c)] * 2, 1)
    sa = np.concatenate([-np.sin(ang_r), zero, -np.sin(ang_c), zero], 1)
    sb = np.concatenate([zero, np.sin(ang_r), zero, np.sin(ang_c)], 1)
    tile = lambda m: np.tile(m, (1, LANES // HEAD_DIM)).astype(np.float32)
    return tile(cos), tile(sa), tile(sb)


@functools.lru_cache(maxsize=None)
def _identity_rope_tables(t):
    return np.ones((t, LANES), np.float32), np.zeros((t, LANES), np.float32), np.zeros((t, LANES), np.float32)


@functools.lru_cache(maxsize=None)
def _head_block_diag():
    hid = np.arange(Q_W) // HEAD_DIM
    return np.asarray(hid[:, None] == hid[None, :], dtype=BF16)


def kernel(x, c, ctx, c_ctx, ada_w, ada_b, norm_mix, norm_ffn, mix_w_in, mix_w_out, attn_q_norm, attn_k_norm,
           swa_sink, hy_conv_w, hy_conv_b, hy_w1, hy_b1, hy_w2, hy_b2, hy_w3, hy_b3, hy_w4, hy_freq, hy_bias_d,
           sc_conv_w, ffn_w_up, ffn_conv_w, ffn_conv_b, ffn_w_down):
    b, s, d = x.shape
    s_ctx = ctx.shape[1]
    depth = ada_w.shape[0]
    assert d == D_MODEL and b % 2 == 0 and b + 1 <= MOD_ROWS and s % 1024 == 0 and s_ctx % SUBLANES == 0

    cvec = jnp.concatenate([c, c_ctx[None, :], jnp.zeros((MOD_ROWS - b - 1, d), F32)], axis=0)
    mods = _mods(cvec, ada_w, ada_b)
    rope = [jnp.asarray(t) for t in _rope_tables(s)]
    rope_ctx = [jnp.asarray(t) for t in _identity_rope_tables(s_ctx)]
    bd = jnp.asarray(_head_block_diag())
    xc = ctx
    weights = _weights_bf16([mix_w_in, mix_w_out, ffn_w_up, ffn_w_down])

    for i in range(depth):
        last = i == depth - 1
        j = i // 2
        lat = [mods[i, :b, k * d:(k + 1) * d][:, None, :] for k in range(6)]
        cx = [jnp.broadcast_to(mods[i, b, k * d:(k + 1) * d][None, None, :], (b, 1, d)) for k in range(6)]
        w_in, w_out, w_up, w_down = (w[i] for w in weights)
        g_mix = norm_mix[i].reshape(1, d)
        g_ffn = norm_ffn[i].reshape(1, d)
        qg = jnp.tile(attn_q_norm[i], N_Q_HEADS).reshape(1, Q_W)
        kg = jnp.tile(attn_k_norm[i], N_KV_HEADS).reshape(1, KV_W)

        conv = (hy_conv_w[j], hy_conv_b[j]) if i % 2 == 0 else (sc_conv_w[j],)
        q, k, vt, *mix = _inproj(x, lat[0], lat[1], g_mix, w_in, qg, kg, bd, *rope, *conv)
        qc, kc, vct, *mixc = _inproj(xc, cx[0], cx[1], g_mix, w_in, qg, kg, bd, *rope_ctx, *conv)

        bound = _score_bound(attn_q_norm[i], attn_k_norm[i])
        if i % 2 == 0:
            fargs = (hy_w1[j], hy_b1[j], hy_w2[j], hy_b2[j], hy_w3[j], hy_b3[j], hy_w4[j], hy_freq[j])
            o_attn = _dense_attention(q, k, vt, kc, vct, bound=bound)
            kf, ks = _implicit_filter(s, *fargs)
            o_conv = _long_conv_mixer(*mix, kf, ks, hy_bias_d[j])
            if not last:
                oc_attn = _dense_attention(qc, kc, vct, bound=bound)
                kfc, ksc = _implicit_filter(s_ctx, *fargs)
                oc_conv = _small_conv_mixer(*mixc, kfc, ksc, hy_bias_d[j])
        else:
            sink = swa_sink[j].reshape(N_KV_HEADS, GROUP)
            o_attn = _banded_attention(q, k, vt, kc, vct, sink, bound)
            o_conv = mix[0]
            if not last:
                oc_attn = _dense_attention(qc, kc, vct, sink=sink)
                oc_conv = mixc[0]

        ffn = (g_ffn, w_up, ffn_conv_w[i], ffn_conv_b[i], w_down)
        x = _mixer_out_ffn(x, o_attn, o_conv, w_out, lat[2], lat[3], lat[4], lat[5], *ffn)
        if not last:
            xc = _mixer_out_ffn(xc, oc_attn, oc_conv, w_out, cx[2], cx[3], cx[4], cx[5], *ffn)
    return x
```

```python
import functools
import math

import numpy as np
import jax
import jax.numpy as jnp
from jax import lax
from jax.experimental import pallas as pl
from jax.experimental.pallas import tpu as pltpu

F32 = jnp.float32
BF16 = jnp.bfloat16
HI = lax.Precision.HIGHEST

D_MODEL = 1024
GRID_W = 64
HEAD_DIM = 64
N_Q_HEADS = 8
N_KV_HEADS = 4
GROUP = N_Q_HEADS // N_KV_HEADS
Q_W = N_Q_HEADS * HEAD_DIM
KV_W = N_KV_HEADS * HEAD_DIM
QKV_W = Q_W + 2 * KV_W
CONV_W = D_MODEL // 2
MIX_IN_W = QKV_W + 3 * CONV_W
WINDOW = 128
ROPE_THETA = 10000.0
FILT_EMB = 33
FILT_WIDTH = 64
DECAY_TARGET = 1e-2
FAST_DECAY_PCT = 0.3
SLOW_DECAY_PCT = 1.5
DECAY_SHIFT = 0.05
D_FF = 2816
NEG_INF = -1e30
RMS_EPS = 1e-6
LOG2E = 1.4426950408889634
Q_SCALE = HEAD_DIM ** -0.5 * LOG2E

LANES = 128
SUBLANES = 8
DFT_N1 = 128
VMEM_LIMIT_MB = 56
VT_ROWS = LANES
MOD_ROWS = SUBLANES
PROJ_SUB_ROWS = 128


def _params(sem, vmem_mb=VMEM_LIMIT_MB):
    return pltpu.CompilerParams(dimension_semantics=sem, vmem_limit_bytes=vmem_mb * 1024 * 1024)


def _full(shape):
    nd = len(shape)
    return pl.BlockSpec(shape, lambda *_: (0,) * nd)


def _mods_kernel(c_ref, w_ref, b_ref, o_ref):
    c = c_ref[...]
    s = c / (1.0 + jnp.exp(-c))
    o_ref[0] = _dot_bf16x3(s, w_ref[0]) + b_ref[0]


def _mods(cvec, ada_w, ada_b):
    depth, d, n6 = ada_w.shape
    tn = 1536
    return pl.pallas_call(
        _mods_kernel,
        grid=(depth, n6 // tn),
        in_specs=[_full((MOD_ROWS, d)),
                  pl.BlockSpec((1, d, tn), lambda l, j: (l, 0, j)),
                  pl.BlockSpec((1, 1, tn), lambda l, j: (l, 0, j))],
        out_specs=pl.BlockSpec((1, MOD_ROWS, tn), lambda l, j: (l, 0, j)),
        out_shape=jax.ShapeDtypeStruct((depth, MOD_ROWS, n6), F32),
        compiler_params=_params(("arbitrary", "arbitrary")),
        name="ada_mods",
    )(cvec, ada_w, ada_b.reshape(depth, 1, n6))


def _modulated_norm(x, gain, shift, scale):
    ms = jnp.mean(x * x, axis=-1, keepdims=True)
    return (x * lax.rsqrt(ms + RMS_EPS)) * gain * (1.0 + scale) + shift


WEIGHT_CAST_STEPS = 8


def _cast_kernel(*refs):
    n = len(refs) // 2
    for w_ref, o_ref in zip(refs[:n], refs[n:]):
        o_ref[...] = w_ref[0].astype(o_ref.dtype)


def _weights_bf16(stacked):
    depth = stacked[0].shape[0]
    in_specs, out_specs, out_shape, args = [], [], [], []
    for w in stacked:
        _, rows, cols = w.shape
        tr = rows // WEIGHT_CAST_STEPS
        for layer in range(depth):
            in_specs.append(pl.BlockSpec((1, tr, cols), lambda r, layer=layer: (layer, r, 0)))
            out_specs.append(pl.BlockSpec((tr, cols), lambda r: (r, 0)))
            out_shape.append(jax.ShapeDtypeStruct((rows, cols), BF16))
            args.append(w)
    outs = pl.pallas_call(
        _cast_kernel,
        grid=(WEIGHT_CAST_STEPS,),
        in_specs=in_specs,
        out_specs=out_specs,
        out_shape=out_shape,
        compiler_params=_params(("arbitrary",)),
        name="weights_bf16",
    )(*args)
    return [outs[k * depth:(k + 1) * depth] for k in range(len(stacked))]


def _halo_specs(tm, t, width, dtype=F32):
    rows = SUBLANES * 4 // jnp.dtype(dtype).itemsize
    nb = tm // rows
    last = t // rows - 1
    return [pl.BlockSpec((1, tm, width), lambda b, i: (b, i, 0)),
            pl.BlockSpec((1, rows, width), lambda b, i: (b, jnp.maximum(i * nb - 1, 0), 0)),
            pl.BlockSpec((1, rows, width), lambda b, i: (b, jnp.minimum((i + 1) * nb, last), 0))]


def _inproj_kernel(*refs, mode):
    (x_ref, xp_ref, xn_ref, sh_ref, sc_ref, g_ref, w_ref, qg_ref, kg_ref, bd_ref,
     cos_ref, sa_ref, sb_ref, cw_ref) = refs[:14]
    if mode == "hyena":
        cb_ref, q_ref, k_ref, vt_ref, u_ref, x0_ref = refs[14:]
    else:
        q_ref, k_ref, vt_ref, oc_ref = refs[14:]
    i = pl.program_id(1)
    nt = pl.num_programs(1)
    tm = x_ref.shape[1]
    sub = min(PROJ_SUB_ROWS, tm)
    gain, shift, scale = g_ref[...], sh_ref[0], sc_ref[0]

    xh = jnp.concatenate([xp_ref[0], xn_ref[0]], axis=0)
    zh = jnp.dot(_modulated_norm(xh, gain, shift, scale).astype(BF16), w_ref[:, QKV_W:], preferred_element_type=F32)
    before = jnp.where(i > 0, zh[:SUBLANES], 0.0)
    after = jnp.where(i < nt - 1, zh[SUBLANES:], 0.0)
    cw = cw_ref[...]

    def conv_stage(ext, rows):
        n = sub + 2 * SUBLANES
        mid = slice(SUBLANES, sub + SUBLANES)
        if mode == "hyena":
            c = (pltpu.roll(ext, 1, 0)[mid] * cw[0:1] + ext[mid] * cw[1:2] + pltpu.roll(ext, n - 1, 0)[mid] * cw[2:3]
                 + cb_ref[...])
            x0_ref[0, rows] = c[:, :CONV_W]
            u_ref[0, rows] = c[:, 2 * CONV_W:] * c[:, CONV_W:2 * CONV_W]
        else:
            pr = ext[:, CONV_W:2 * CONV_W] * ext[:, 2 * CONV_W:]
            conv = pltpu.roll(pr, 1, 0)[mid] * cw[0:1] + pr[mid] * cw[1:2] + pltpu.roll(pr, n - 1, 0)[mid] * cw[2:3]
            oc_ref[0, rows] = (ext[mid, :CONV_W] * conv).astype(oc_ref.dtype)

    def head_norm_rope(t, hgain, bd, cos, sa, sb):
        ssq = jnp.dot((t * t).astype(BF16), bd, preferred_element_type=F32)
        t = t * lax.rsqrt(ssq * (1.0 / HEAD_DIM) + RMS_EPS) * hgain
        outs = []
        for j in range(t.shape[1] // LANES):
            tj = t[:, j * LANES:(j + 1) * LANES]
            outs.append(tj * cos + pltpu.roll(tj, LANES - 16, 1) * sa + pltpu.roll(tj, 16, 1) * sb)
        return outs

    held = None
    for r0 in range(0, tm, sub):
        rows = slice(r0, r0 + sub)
        h = _modulated_norm(x_ref[0, rows], gain, shift, scale)
        hb = h.astype(BF16)
        z = jnp.dot(hb, w_ref[:, :QKV_W], preferred_element_type=F32)
        zc = jnp.dot(hb, w_ref[:, QKV_W:], preferred_element_type=F32)
        if held is not None:
            conv_stage(jnp.concatenate([before, held[0], zc[:SUBLANES]], axis=0), held[1])
            before = held[0][-SUBLANES:]
        held = (zc, rows)
        tabs = (cos_ref[rows], sa_ref[rows], sb_ref[rows])
        qs = head_norm_rope(z[:, :Q_W], qg_ref[...], bd_ref[...], *tabs)
        ks = head_norm_rope(z[:, Q_W:Q_W + KV_W], kg_ref[...], bd_ref[:KV_W, :KV_W], *tabs)
        for j, qj in enumerate(qs):
            qj = (qj * Q_SCALE).astype(BF16)
            q_ref[0, 2 * j, rows] = qj[:, :HEAD_DIM]
            q_ref[0, 2 * j + 1, rows] = qj[:, HEAD_DIM:]
        for j, kj in enumerate(ks):
            kj = kj.astype(BF16)
            k_ref[0, 2 * j, rows] = kj[:, :HEAD_DIM]
            k_ref[0, 2 * j + 1, rows] = kj[:, HEAD_DIM:]
        vt = z[:, Q_W + KV_W:QKV_W].T
        ones_row = (lax.broadcasted_iota(jnp.int32, (VT_ROWS - HEAD_DIM, sub), 0) == 0).astype(F32)
        for hh in range(N_KV_HEADS):
            vt_ref[0, hh, :, rows] = jnp.concatenate([vt[hh * HEAD_DIM:(hh + 1) * HEAD_DIM], ones_row],
                                                     axis=0).astype(BF16)

    conv_stage(jnp.concatenate([before, held[0], after], axis=0), held[1])


def _inproj(x, sh, sc, gain, w_bf, qg, kg, bd, cos, sa, sb, conv_w, conv_b=None):
    bx, t, d = x.shape
    tm = min(1024, t)
    mode = "hyena" if conv_b is not None else "short"
    vec = pl.BlockSpec((1, 1, d), lambda b, i: (b, 0, 0))
    tab = pl.BlockSpec((tm, LANES), lambda b, i: (i, 0))
    row = lambda w: pl.BlockSpec((1, tm, w), lambda b, i: (b, i, 0))
    in_specs = _halo_specs(tm, t, d) + [vec, vec, _full((1, d)), _full((d, MIX_IN_W)), _full((1, Q_W)),
                                        _full((1, KV_W)), _full((Q_W, Q_W)), tab, tab, tab, _full(conv_w.shape)]
    args = [x, x, x, sh, sc, gain, w_bf, qg, kg, bd, cos, sa, sb, conv_w]
    out_specs = [pl.BlockSpec((1, N_Q_HEADS, tm, HEAD_DIM), lambda b, i: (b, 0, i, 0)),
                 pl.BlockSpec((1, N_KV_HEADS, tm, HEAD_DIM), lambda b, i: (b, 0, i, 0)),
                 pl.BlockSpec((1, N_KV_HEADS, VT_ROWS, tm), lambda b, i: (b, 0, 0, i))]
    out_shape = [jax.ShapeDtypeStruct((bx, N_Q_HEADS, t, HEAD_DIM), BF16),
                 jax.ShapeDtypeStruct((bx, N_KV_HEADS, t, HEAD_DIM), BF16),
                 jax.ShapeDtypeStruct((bx, N_KV_HEADS, VT_ROWS, t), BF16)]
    if mode == "hyena":
        in_specs.append(_full((1, 3 * CONV_W)))
        args.append(conv_b.reshape(1, 3 * CONV_W))
        out_specs += [row(CONV_W), row(CONV_W)]
        out_shape += [jax.ShapeDtypeStruct((bx, t, CONV_W), F32)] * 2
    else:
        out_specs.append(row(CONV_W))
        out_shape.append(jax.ShapeDtypeStruct((bx, t, CONV_W), BF16))
    return pl.pallas_call(
        functools.partial(_inproj_kernel, mode=mode),
        grid=(bx, t // tm),
        in_specs=in_specs,
        out_specs=out_specs,
        out_shape=out_shape,
        compiler_params=_params(("arbitrary", "arbitrary")),
        name="in_proj",
    )(*args)


def _score_bound(q_gain, k_gain):
    b = HEAD_DIM * Q_SCALE * BOUND_MARGIN * jnp.max(jnp.abs(q_gain)) * jnp.max(jnp.abs(k_gain))
    return b.reshape(1, 1).astype(F32)


def _scores_t(ks, q):
    return lax.dot_general(ks, q, (((1,), (1,)), ((), ())), preferred_element_type=F32)


def _sink_column(sink_ref, hh, tq, m, acc):
    col = lax.broadcasted_iota(jnp.int32, m.shape, 1)
    sink = jnp.where(col < tq, sink_ref[hh, 0], sink_ref[hh, 1]) * LOG2E
    m_new = jnp.maximum(m, sink)
    den_row = lax.broadcasted_iota(jnp.int32, acc.shape, 0) == HEAD_DIM
    return acc * jnp.exp2(m - m_new) + jnp.where(den_row, jnp.exp2(sink - m_new), 0.0)


def _attn_finish(acc, tq):
    o = (acc / acc[HEAD_DIM:HEAD_DIM + 1]).T
    return jnp.concatenate([o[:tq, :HEAD_DIM], o[tq:, :HEAD_DIM]], axis=1)


EXP_CHUNK_ELEMS = 32 * SUBLANES * LANES


def _score_stage(ks, q, s_ref, bias=None):
    s = _scores_t(ks, q)
    if bias is not None:
        s = s + bias
    s_ref[0:s.shape[0]] = s
    return jnp.max(s, axis=0, keepdims=True)


def _softmax_stage(s_ref, p_ref, n, vt, mx, m, acc_ref):
    m_new = jnp.maximum(m, mx)
    rows = EXP_CHUNK_ELEMS // m.shape[1]
    for c in range(0, n, rows):
        p_ref[c:c + rows] = jnp.exp2(s_ref[c:c + rows] - m_new).astype(BF16)
    acc_ref[...] = acc_ref[...] * jnp.exp2(m - m_new) + jnp.dot(vt, p_ref[0:n], preferred_element_type=F32)
    return m_new


BOUNDED_TILES = 8
BOUND_MARGIN = 1.01
SAFE_SHIFT = 40.0


def _flash_exact(q, k_tile, vt_tile, ctx, sa_ref, sb_ref, p_ref, acc_ref, *, tk, n_main):
    pa_ref, pb_ref = p_ref.at[0:tk], p_ref.at[tk:2 * tk]
    acc_ref[...] = jnp.zeros_like(acc_ref)
    m = jnp.full((1, q.shape[0]), NEG_INF, F32)
    mx_a = _score_stage(k_tile(0), q, sa_ref)
    if n_main > 1:
        def body(i, carry):
            m, mx_a = carry
            mx_b = _score_stage(k_tile(2 * i + 1), q, sb_ref)
            m = _softmax_stage(sa_ref, pa_ref, tk, vt_tile(2 * i), mx_a, m, acc_ref)
            mx_a = _score_stage(k_tile(2 * i + 2), q, sa_ref)
            m = _softmax_stage(sb_ref, pb_ref, tk, vt_tile(2 * i + 1), mx_b, m, acc_ref)
            return m, mx_a

        m, mx_a = lax.fori_loop(0, n_main // 2 - 1, body, (m, mx_a))
        mx_b = _score_stage(k_tile(n_main - 1), q, sb_ref)
        m = _softmax_stage(sa_ref, pa_ref, tk, vt_tile(n_main - 2), mx_a, m, acc_ref)
        last = (sb_ref, pb_ref, mx_b)
        spare = (sa_ref, pa_ref)
    else:
        last = (sa_ref, pa_ref, mx_a)
        spare = (sb_ref, pb_ref)
    if ctx is not None:
        mx_c = _score_stage(ctx[0], q, spare[0])
    m = _softmax_stage(last[0], last[1], tk, vt_tile(n_main - 1), last[2], m, acc_ref)
    if ctx is not None:
        m = _softmax_stage(spare[0], spare[1], ctx[0].shape[0], ctx[1], mx_c, m, acc_ref)
    return m


def _exp2_bf16(s):
    return jnp.exp2(s).astype(BF16)


def _flash_bounded(q, k_ref, vt_ref, ctx, p_ref, acc_ref, *, tk, n_main):
    step = BOUNDED_TILES * tk if n_main % BOUNDED_TILES == 0 else tk

    def group(j):
        off = pl.multiple_of(j * step, step)
        for c in range(0, step, tk):
            p_ref[c:c + tk] = _exp2_bf16(_scores_t(k_ref[0, 0, pl.ds(off + c, tk), :], q))
        return jnp.dot(vt_ref[0, 0, :, pl.ds(off, step)], p_ref[0:step], preferred_element_type=F32)

    first = group(0)
    if ctx is not None:
        pc_ref = p_ref.at[BOUNDED_TILES * tk:BOUNDED_TILES * tk + ctx[0].shape[0]]
        pc_ref[...] = _exp2_bf16(_scores_t(ctx[0], q))
        first = first + jnp.dot(ctx[1], pc_ref[...], preferred_element_type=F32)
    acc_ref[...] = first

    def body(j, carry):
        acc_ref[...] += group(j)
        return carry

    lax.fori_loop(1, n_main * tk // step, body, 0)


def _flash_kernel(*refs, tq, tk, n_main, has_ctx, has_sink):
    sa_ref, sb_ref, p_ref, acc_ref = refs[-4:]
    o_ref = refs[-5]
    refs = list(refs[:-5])
    sink_ref = refs.pop(0) if has_sink else None
    bound_ref = None if has_sink else refs.pop(0)
    q_ref, k_ref, vt_ref = refs[:3]
    m_cols = GROUP * tq
    q = q_ref[0].reshape(m_cols, HEAD_DIM)
    ctx = (refs[3][0, 0], refs[4][0, 0]) if has_ctx else None

    def k_tile(j):
        return k_ref[0, 0, pl.ds(pl.multiple_of(j * tk, tk), tk), :]

    def vt_tile(j):
        return vt_ref[0, 0, :, pl.ds(pl.multiple_of(j * tk, tk), tk)]

    exact = functools.partial(_flash_exact, q, k_tile, vt_tile, ctx, sa_ref, sb_ref, p_ref, acc_ref,
                              tk=tk, n_main=n_main)
    if has_sink:
        acc = _sink_column(sink_ref, pl.program_id(1), tq, exact(), acc_ref[...])
    else:
        bound = bound_ref[0, 0]
        safe = bound <= SAFE_SHIFT

        @pl.when(safe)
        def _():
            _flash_bounded(q, k_ref, vt_ref, ctx, p_ref, acc_ref, tk=tk, n_main=n_main)

        @pl.when(jnp.logical_not(safe))
        def _():
            exact()

        acc = acc_ref[...]
    o_ref[0] = _attn_finish(acc, tq).astype(o_ref.dtype)


def _kv_specs(n):
    return [pl.BlockSpec((1, 1, n, HEAD_DIM), lambda b, h, i: (b, h, 0, 0)),
            pl.BlockSpec((1, 1, VT_ROWS, n), lambda b, h, i: (b, h, 0, 0))]


def _dense_attention(q, k, vt, kc=None, vct=None, *, bound=None, sink=None):
    bx, _, t, _ = q.shape
    s = k.shape[2]
    tq = min(1024, t)
    tk = min(512, s)
    has_ctx = kc is not None
    has_sink = sink is not None
    in_specs = [pl.BlockSpec((1, GROUP, tq, HEAD_DIM), lambda b, h, i: (b, h, i, 0))] + _kv_specs(s)
    args = [q, k, vt]
    if has_ctx:
        in_specs += _kv_specs(kc.shape[2])
        args += [kc, vct]
    in_specs = [pl.BlockSpec(memory_space=pltpu.SMEM)] + in_specs
    args = [sink if has_sink else bound] + args
    n_main = s // tk
    assert n_main == 1 or n_main % 2 == 0
    n_ctx = kc.shape[2] if has_ctx else 0
    assert n_ctx <= tk
    m_cols = GROUP * tq
    return pl.pallas_call(
        functools.partial(_flash_kernel, tq=tq, tk=tk, n_main=n_main, has_ctx=has_ctx, has_sink=has_sink),
        grid=(bx, N_KV_HEADS, t // tq),
        in_specs=in_specs,
        out_specs=pl.BlockSpec((1, tq, LANES), lambda b, h, i: (b, i, h)),
        out_shape=jax.ShapeDtypeStruct((bx, t, Q_W), BF16),
        scratch_shapes=[pltpu.VMEM((tk, m_cols), F32), pltpu.VMEM((tk, m_cols), F32),
                        pltpu.VMEM((BOUNDED_TILES * tk + n_ctx, m_cols), BF16), pltpu.VMEM((VT_ROWS, m_cols), F32)],
        compiler_params=_params(("arbitrary", "arbitrary", "arbitrary")),
        name="dense_attention",
    )(*args)


BAND_SUB = 256


@functools.lru_cache(maxsize=None)
def _band_bias():
    span = BAND_SUB + 2 * WINDOW
    kr = np.arange(span)[None, :, None]
    qc = (np.arange(GROUP * BAND_SUB) % BAND_SUB)[None, None, :]
    rel = np.arange(3)[:, None, None]
    return np.where(np.abs(kr - qc - rel * WINDOW) <= WINDOW, 0.0, NEG_INF).astype(np.float32)


def _banded_kernel(sink_ref, bound_ref, q_ref, k_ref, vt_ref, kc_ref, vct_ref, bias_ref, o_ref,
                   sl_ref, sc_ref, pl_ref, pc_ref, acc_ref, *, n_sub, s_len):
    i = pl.program_id(2)
    hh = pl.program_id(1)
    sub = BAND_SUB
    m_cols = GROUP * sub
    span = sub + 2 * WINDOW
    n_ctx = kc_ref.shape[2]

    def window(u):
        q = q_ref[0, :, u * sub:(u + 1) * sub, :].reshape(m_cols, HEAD_DIM)
        q0 = (i * n_sub + u) * sub
        start = pl.multiple_of(jnp.clip(q0 - WINDOW, 0, s_len - span), WINDOW)
        return q, start, bias_ref[(q0 - start) // WINDOW]

    shift = jnp.maximum(bound_ref[0, 0], jnp.maximum(sink_ref[hh, 0], sink_ref[hh, 1]) * LOG2E)
    safe = shift <= SAFE_SHIFT

    @pl.when(safe)
    def _():
        col = lax.broadcasted_iota(jnp.int32, (1, m_cols), 1)
        sink = jnp.where(col < sub, sink_ref[hh, 0], sink_ref[hh, 1]) * LOG2E
        den_row = lax.broadcasted_iota(jnp.int32, (VT_ROWS, m_cols), 0) == HEAD_DIM
        sink_den = jnp.where(den_row, jnp.exp2(sink), 0.0)
        for u in range(n_sub):
            q, start, bias = window(u)
            pl_u, pc_u = pl_ref.at[u], pc_ref.at[u]
            pl_u[...] = _exp2_bf16(_scores_t(k_ref[0, 0, pl.ds(start, span), :], q) + bias)
            pc_u[...] = _exp2_bf16(_scores_t(kc_ref[0, 0], q))
            acc = (jnp.dot(vt_ref[0, 0, :, pl.ds(start, span)], pl_u[...], preferred_element_type=F32)
                   + jnp.dot(vct_ref[0, 0], pc_u[...], preferred_element_type=F32) + sink_den)
            o_ref[0, u * sub:(u + 1) * sub] = _attn_finish(acc, sub).astype(o_ref.dtype)

    @pl.when(jnp.logical_not(safe))
    def _():
        stats = []
        for u in range(n_sub):
            q, start, bias = window(u)
            mx_l = _score_stage(k_ref[0, 0, pl.ds(start, span), :], q, sl_ref.at[u], bias=bias)
            mx_c = _score_stage(kc_ref[0, 0], q, sc_ref.at[u])
            stats.append((start, mx_l, mx_c))
        for u in range(n_sub):
            start, mx_l, mx_c = stats[u]
            acc_u = acc_ref.at[u]
            acc_u[...] = jnp.zeros((VT_ROWS, m_cols), F32)
            m = jnp.full((1, m_cols), NEG_INF, F32)
            m = _softmax_stage(sl_ref.at[u], pl_ref.at[u], span, vt_ref[0, 0, :, pl.ds(start, span)], mx_l, m, acc_u)
            m = _softmax_stage(sc_ref.at[u], pc_ref.at[u], n_ctx, vct_ref[0, 0], mx_c, m, acc_u)
            acc = _sink_column(sink_ref, hh, sub, m, acc_u[...])
            o_ref[0, u * sub:(u + 1) * sub] = _attn_finish(acc, sub).astype(o_ref.dtype)


def _banded_attention(q, k, vt, kc, vct, sink, bound):
    bx, _, t, _ = q.shape
    n_ctx = kc.shape[2]
    n_sub = next(k for k in (8, 4, 2, 1) if t % (k * BAND_SUB) == 0)
    tq = n_sub * BAND_SUB
    m_cols = GROUP * BAND_SUB
    span = BAND_SUB + 2 * WINDOW
    bias = jnp.asarray(_band_bias())
    return pl.pallas_call(
        functools.partial(_banded_kernel, n_sub=n_sub, s_len=t),
        grid=(bx, N_KV_HEADS, t // tq),
        in_specs=[pl.BlockSpec(memory_space=pltpu.SMEM), pl.BlockSpec(memory_space=pltpu.SMEM),
                  pl.BlockSpec((1, GROUP, tq, HEAD_DIM), lambda b, h, i: (b, h, i, 0))]
                 + _kv_specs(t) + _kv_specs(n_ctx) + [_full(bias.shape)],
        out_specs=pl.BlockSpec((1, tq, LANES), lambda b, h, i: (b, i, h)),
        out_shape=jax.ShapeDtypeStruct((bx, t, Q_W), BF16),
        scratch_shapes=[pltpu.VMEM((n_sub, span, m_cols), F32), pltpu.VMEM((n_sub, n_ctx, m_cols), F32),
                        pltpu.VMEM((n_sub, span, m_cols), BF16), pltpu.VMEM((n_sub, n_ctx, m_cols), BF16),
                        pltpu.VMEM((n_sub, VT_ROWS, m_cols), F32)],
        compiler_params=_params(("arbitrary", "arbitrary", "arbitrary")),
        name="banded_attention",
    )(sink, bound, q, k, vt, kc, vct, bias)


@functools.lru_cache(maxsize=None)
def _filter_features(n):
    j = np.arange(2 * n)
    d = np.where(j <= n, j, 2 * n - j)
    d = np.where(j == n, 0, d)
    bands = (FILT_EMB - 1) // 2
    t01 = np.linspace(0.0, 1.0, n)[d]
    w = 2.0 * np.pi * d.astype(np.float64) / n
    f = np.linspace(1e-4, bands - 1, bands)[None, :]
    feats = np.zeros((2 * n, LANES), np.float64)
    feats[:, 0] = t01
    feats[:, 1:1 + bands] = np.cos(f * w[:, None])
    feats[:, 1 + bands:FILT_EMB] = -np.sin(f * w[:, None])
    feats[:, 64] = t01
    feats[:, 65] = (j < n)
    feats[:, 66] = (j != n)
    return feats.astype(np.float32)


def _dot_bf16x3(a, b):
    a_hi = a.astype(BF16)
    a_lo = (a - a_hi.astype(F32)).astype(BF16)
    b_hi = b.astype(BF16)
    b_lo = (b - b_hi.astype(F32)).astype(BF16)
    dot = lambda x, y: jnp.dot(x, y, preferred_element_type=F32)
    return dot(a_hi, b_hi) + dot(a_hi, b_lo) + dot(a_lo, b_hi)


def _filter_kernel(f_ref, w1_ref, b1_ref, w2_ref, b2_ref, w3_ref, b3_ref, w4_ref, fr_ref, dl_ref, k_ref, s_ref):
    f = f_ref[...]
    fr = fr_ref[...]
    mm = lambda a, b: jnp.dot(a, b, preferred_element_type=F32, precision=HI)
    h = jnp.sin(fr * (mm(w1_ref[...], f) + b1_ref[...]))
    h = jnp.sin(fr * (mm(w2_ref[...], h) + b2_ref[...]))
    h = jnp.sin(fr * (mm(w3_ref[...], h) + b3_ref[...]))
    hf = _dot_bf16x3(w4_ref[...], h)
    win = jnp.exp(-dl_ref[...] * f[64:65]) + DECAY_SHIFT
    k = (jnp.where(f[65:66] > 0.5, hf[:CONV_W], hf[CONV_W:]) * win * f[66:67]).T
    k_ref[...] = k

    @pl.when(pl.program_id(0) == 0)
    def _():
        s_ref[...] = jnp.zeros_like(s_ref)

    s_ref[...] += jnp.sum(jnp.abs(k), axis=0, keepdims=True)


def _implicit_filter(n, w1, b1, w2, b2, w3, b3, w4, freq):
    feats_t = jnp.asarray(np.ascontiguousarray(_filter_features(n).T))
    col = lambda a: a.reshape(-1, 1)
    w1t = jnp.pad(w1.T, ((0, 0), (0, LANES - FILT_EMB)))
    deltas = np.abs(np.linspace(math.log(DECAY_TARGET) / SLOW_DECAY_PCT, math.log(DECAY_TARGET) / FAST_DECAY_PCT,
                                CONV_W)).astype(np.float32).reshape(CONV_W, 1)
    tr = min(1024, 2 * n)
    sq = _full((FILT_WIDTH, FILT_WIDTH))
    vec = _full((FILT_WIDTH, 1))
    return pl.pallas_call(
        _filter_kernel,
        grid=(2 * n // tr,),
        in_specs=[pl.BlockSpec((LANES, tr), lambda i: (0, i)), _full((FILT_WIDTH, LANES)), vec, sq, vec, sq, vec,
                  _full((2 * CONV_W, FILT_WIDTH)), vec, _full((CONV_W, 1))],
        out_specs=[pl.BlockSpec((tr, CONV_W), lambda i: (i, 0)), _full((1, CONV_W))],
        out_shape=[jax.ShapeDtypeStruct((2 * n, CONV_W), F32), jax.ShapeDtypeStruct((1, CONV_W), F32)],
        compiler_params=_params(("arbitrary",)),
        name="hyena_filter",
    )(feats_t, w1t, col(b1), w2.T, col(b2), w3.T, col(b3), w4.T, col(freq), jnp.asarray(deltas))


def _twiddle(idx, mod):
    ang = 2.0 * np.pi * (idx % mod) / mod
    return np.cos(ang), -np.sin(ang)


def _real_form(mr, mi):
    return np.concatenate([np.concatenate([mr, -mi], -1), np.concatenate([mi, mr], -1)], -2)


@functools.lru_cache(maxsize=None)
def _dft_tables(n2):
    n1 = DFT_N1
    n = n1 * n2
    h = n2 // 2
    a2 = np.arange(n2)
    fr, fi = _twiddle(np.outer(a2, a2), n2)
    m_data = _real_form(fr[:, :h], fi[:, :h])
    m_filt = np.concatenate([fr, fi], 0)
    k2 = a2[:, None, None]
    k1 = np.arange(n1)[None, :, None]
    c1 = np.arange(n1)[None, None, :]
    g = _real_form(*_twiddle(c1 * (n2 * k1 + k2), n))
    a1 = np.arange(n1)
    f1 = _real_form(*_twiddle(np.outer(a1, a1), n1))
    t2 = a1[:, None, None]
    t1 = np.arange(h)[None, :, None]
    j1 = a2[None, None, :]
    hh = _real_form(*_twiddle(j1 * (n1 * t1 + t2), n))
    cast = lambda m: np.asarray(m, dtype=BF16)
    return cast(m_data), cast(m_filt), cast(g), cast(f1), cast(hh)


DFT_ROW_CHUNK = 2 * SUBLANES


def _dft_rows_kernel(m_ref, *refs):
    o_ref = refs[-1]
    cols = [jnp.concatenate([r[0, :, t, :] for r in refs[:-1]], axis=0) for t in range(DFT_ROW_CHUNK)]
    rhs = jnp.concatenate(cols, axis=1).astype(BF16)
    out = jnp.dot(m_ref[...], rhs, preferred_element_type=F32)
    half = out.shape[0] // 2
    o_ref[0, 0] = out[:half].astype(o_ref.dtype)
    o_ref[0, 1] = out[half:].astype(o_ref.dtype)


def _dft_stage1(mat, views, pairs, rows, n2, c):
    nin = len(views)
    lc = DFT_ROW_CHUNK * c
    in_specs = [_full(mat.shape)] + [
        pl.BlockSpec((1, rows, DFT_ROW_CHUNK, c), (lambda p, j, a=a: (nin * p + a, 0, j, 0))) for a in range(nin)]
    return pl.pallas_call(
        _dft_rows_kernel,
        grid=(pairs, DFT_N1 // DFT_ROW_CHUNK),
        in_specs=in_specs,
        out_specs=pl.BlockSpec((1, 2, n2, lc), lambda p, j: (p, 0, 0, j)),
        out_shape=jax.ShapeDtypeStruct((pairs, 2, n2, DFT_N1 * c), BF16),
        compiler_params=_params(("arbitrary", "arbitrary")),
        name="dft_stage1",
    )(mat, *views)


def _spectrum_kernel(a_ref, g_ref, sc_ref, o_ref, *, kb):
    for j in range(kb):
        rhs = jnp.concatenate([a_ref[0, 0, j], a_ref[0, 1, j]], axis=0)
        x = jnp.dot(g_ref[j], rhs, preferred_element_type=F32)
        o_ref[0, j] = (x[:DFT_N1] * sc_ref[...]).astype(o_ref.dtype)
        o_ref[1, j] = (x[DFT_N1:] * sc_ref[...]).astype(o_ref.dtype)


def _filter_spectrum(a, g, scale, n2, kb):
    c = a.shape[-1]
    return pl.pallas_call(
        functools.partial(_spectrum_kernel, kb=kb),
        grid=(n2 // kb,),
        in_specs=[pl.BlockSpec((1, 2, kb, DFT_N1, c), lambda k: (0, 0, k, 0, 0)),
                  pl.BlockSpec((kb, 2 * DFT_N1, 2 * DFT_N1), lambda k: (k, 0, 0)), _full((1, c))],
        out_specs=pl.BlockSpec((2, kb, DFT_N1, c), lambda k: (0, k, 0, 0)),
        out_shape=jax.ShapeDtypeStruct((2, n2, DFT_N1, c), BF16),
        compiler_params=_params(("arbitrary",)),
        name="filter_spectrum",
    )(a, g, scale)


def _dft_mid_kernel(a_ref, g_ref, kh_ref, f_ref, o_ref, *, kb):
    for j in range(kb):
        rhs = jnp.concatenate([a_ref[0, 0, j], a_ref[0, 1, j]], axis=0)
        x = jnp.dot(g_ref[j], rhs, preferred_element_type=F32)
        xr, xi = x[:DFT_N1], x[DFT_N1:]
        kr, ki = kh_ref[0, j].astype(F32), kh_ref[1, j].astype(F32)
        yr = xr * kr - xi * ki
        yi = xr * ki + xi * kr
        v = jnp.concatenate([yr, -yi], axis=0).astype(BF16)
        b = jnp.dot(f_ref[...], v, preferred_element_type=F32)
        o_ref[0, 0, :, j, :] = b[:DFT_N1]
        o_ref[0, 1, :, j, :] = b[DFT_N1:]


def _dft_mid(a, g, khat, f1, pairs, n2, kb):
    c = a.shape[-1]
    blk = pl.BlockSpec((1, 2, kb, DFT_N1, c), lambda k, p: (p, 0, k, 0, 0))
    return pl.pallas_call(
        functools.partial(_dft_mid_kernel, kb=kb),
        grid=(n2 // kb, pairs),
        in_specs=[blk, pl.BlockSpec((kb, 2 * DFT_N1, 2 * DFT_N1), lambda k, p: (k, 0, 0)),
                  pl.BlockSpec((2, kb, DFT_N1, c), lambda k, p: (0, k, 0, 0)), _full(f1.shape)],
        out_specs=pl.BlockSpec((1, 2, DFT_N1, kb, c), lambda k, p: (p, 0, 0, k, 0)),
        out_shape=jax.ShapeDtypeStruct((pairs, 2, DFT_N1, n2, c), F32),
        compiler_params=_params(("arbitrary", "arbitrary")),
        name="dft_mid",
    )(a, g, khat, f1)


def _dft_last_kernel(b_ref, h_ref, u_ref, x0_ref, bd_ref, o_ref, *, c):
    half = h_ref.shape[1] // 2
    bd = bd_ref[...]
    for t in range(DFT_ROW_CHUNK):
        rhs = jnp.concatenate([b_ref[0, 0, t], b_ref[0, 1, t]], axis=0).astype(BF16)
        v = jnp.dot(h_ref[t], rhs, preferred_element_type=F32)
        o_ref[0, :, t, :] = (v[:half] + u_ref[0, :, t, :] * bd) * x0_ref[0, :, t, :]
        o_ref[1, :, t, :] = (-v[half:] + u_ref[1, :, t, :] * bd) * x0_ref[1, :, t, :]


def _dft_last(bm, hh, u_view, x0_view, bias_d, pairs, n2, c):
    half = n2 // 2
    tc = DFT_ROW_CHUNK
    io = pl.BlockSpec((2, half, tc, c), lambda p, j: (p, 0, j, 0))
    return pl.pallas_call(
        functools.partial(_dft_last_kernel, c=c),
        grid=(pairs, DFT_N1 // tc),
        in_specs=[pl.BlockSpec((1, 2, tc, n2, c), lambda p, j: (p, 0, j, 0, 0)),
                  pl.BlockSpec((tc, n2, 2 * n2), lambda p, j: (j, 0, 0)), io, io, _full((1, c))],
        out_specs=io,
        out_shape=jax.ShapeDtypeStruct((2 * pairs, half, DFT_N1, c), F32),
        compiler_params=_params(("arbitrary", "arbitrary")),
        name="dft_last",
    )(bm, hh, u_view, x0_view, bias_d)


def _long_conv_mixer(u, x0, kfilt, ksum, bias_d):
    b, n, c = u.shape
    n2 = 2 * n // DFT_N1
    half = n2 // 2
    pairs = b // 2
    m_data, m_filt, g, f1, hh = (jnp.asarray(t) for t in _dft_tables(n2))
    kb = min(16, n2)
    scale = 1.0 / (ksum * float(DFT_N1 * n2))
    ka = _dft_stage1(m_filt, [kfilt.reshape(1, n2, DFT_N1, c)], 1, n2, n2, c)
    khat = _filter_spectrum(ka.reshape(1, 2, n2, DFT_N1, c), g, scale, n2, kb)
    u_view = u.reshape(b, half, DFT_N1, c)
    a = _dft_stage1(m_data, [u_view, u_view], pairs, half, n2, c)
    bm = _dft_mid(a.reshape(pairs, 2, n2, DFT_N1, c), g, khat, f1, pairs, n2, kb)
    out = _dft_last(bm, hh, u_view, x0.reshape(b, half, DFT_N1, c),
                    bias_d.reshape(1, c), pairs, n2, c)
    return out.reshape(b, n, c)


@functools.lru_cache(maxsize=None)
def _small_dft_tables(n):
    big = 2 * n
    a = np.arange(big)
    fr, fi = _twiddle(np.outer(a, a), big)
    cast = lambda m: np.asarray(m, dtype=BF16)
    return (cast(np.concatenate([fr, fi], 0)),
            cast(_real_form(fr[:, :n], fi[:, :n])),
            cast(_real_form(fr[:n], fi[:n])))


def _small_conv_kernel(k_ref, ks_ref, u_ref, x0_ref, bd_ref, mf_ref, md_ref, mi_ref, o_ref, *, n):
    big = 2 * n
    kh = jnp.dot(mf_ref[...], k_ref[...].astype(BF16), preferred_element_type=F32) * (1.0 / (ks_ref[...] * big))
    kr, ki = kh[:big], kh[big:]
    rhs = jnp.concatenate([u_ref[0], u_ref[1]], axis=0).astype(BF16)
    x = jnp.dot(md_ref[...], rhs, preferred_element_type=F32)
    xr, xi = x[:big], x[big:]
    v = jnp.concatenate([xr * kr - xi * ki, -(xr * ki + xi * kr)], axis=0).astype(BF16)
    y = jnp.dot(mi_ref[...], v, preferred_element_type=F32)
    bd = bd_ref[...]
    o_ref[0] = ((y[:n] + u_ref[0] * bd) * x0_ref[0]).astype(o_ref.dtype)
    o_ref[1] = ((-y[n:] + u_ref[1] * bd) * x0_ref[1]).astype(o_ref.dtype)


def _small_conv_mixer(u, x0, kfilt, ksum, bias_d):
    b, n, c = u.shape
    mf, md, mi = (jnp.asarray(t) for t in _small_dft_tables(n))
    io = pl.BlockSpec((2, n, c), lambda p: (p, 0, 0))
    return pl.pallas_call(
        functools.partial(_small_conv_kernel, n=n),
        grid=(b // 2,),
        in_specs=[_full((2 * n, c)), _full((1, c)), io, io, _full((1, c)),
                  _full(mf.shape), _full(md.shape), _full(mi.shape)],
        out_specs=io,
        out_shape=jax.ShapeDtypeStruct((b, n, c), BF16),
        compiler_params=_params(("arbitrary",)),
        name="small_conv",
    )(kfilt, ksum, u, x0, bias_d.reshape(1, c), mf, md, mi)


MXU_DIM = 256
FFN_CHUNK_EDGES = (0, 5 * MXU_DIM, D_FF)


def _gelu_tanh(x):
    return 0.5 * x * (1.0 + jnp.tanh(math.sqrt(2.0 / math.pi) * (x + 0.044715 * (x * x * x))))


def _ffn_kernel(x_ref, xp_ref, xn_ref, oa_ref, oap_ref, oan_ref, oc_ref, ocp_ref, ocn_ref, wa_ref, wc_ref, g1_ref,
                sh_ref, sc_ref, gt_ref, g_ref, wu_ref, cw_ref, cb_ref, wd_ref, o_ref):
    i = pl.program_id(1)
    nt = pl.num_programs(1)
    tm = x_ref.shape[1]

    def halo(p_ref, n_ref):
        return jnp.concatenate([p_ref[0][-SUBLANES:], n_ref[0][:SUBLANES]], axis=0)

    def mixer_out(oa, oc):
        return (jnp.dot(oa.astype(BF16), wa_ref[...], preferred_element_type=F32)
                + jnp.dot(oc.astype(BF16), wc_ref[...], preferred_element_type=F32))

    g1 = g1_ref[0]
    xm = x_ref[0] + g1 * mixer_out(oa_ref[0], oc_ref[0])
    xh = halo(xp_ref, xn_ref) + g1 * mixer_out(halo(oap_ref, oan_ref), halo(ocp_ref, ocn_ref))
    ext = jnp.concatenate([xh[:SUBLANES], xm, xh[SUBLANES:]], axis=0)
    r = tm + 2 * SUBLANES
    h = _modulated_norm(ext, g_ref[...], sh_ref[0], sc_ref[0])
    row = lax.broadcasted_iota(jnp.int32, (r, 1), 0)
    inside = jnp.logical_and(jnp.logical_or(i > 0, row >= SUBLANES), jnp.logical_or(i < nt - 1, row < tm + SUBLANES))
    h = jnp.where(inside, h, 0.0).astype(BF16)
    hm = h[SUBLANES:tm + SUBLANES]
    acc = jnp.zeros((tm, D_MODEL), F32)
    for c0, c1 in zip(FFN_CHUNK_EDGES[:-1], FFN_CHUNK_EDGES[1:]):
        a = jnp.dot(h, wu_ref[:, c0:c1], preferred_element_type=F32)
        v = jnp.dot(hm, wu_ref[:, D_FF + c0:D_FF + c1], preferred_element_type=F32)
        ap = pltpu.roll(a, 1, 0)[SUBLANES:tm + SUBLANES]
        an = pltpu.roll(a, r - 1, 0)[SUBLANES:tm + SUBLANES]
        cw = cw_ref[:, c0:c1]
        conv = ap * cw[0:1] + a[SUBLANES:tm + SUBLANES] * cw[1:2] + an * cw[2:3] + cb_ref[:, c0:c1]
        act = (_gelu_tanh(conv) * v).astype(BF16)
        acc = acc + jnp.dot(act, wd_ref[c0:c1, :], preferred_element_type=F32)
    o_ref[0] = xm + gt_ref[0] * acc


def _mixer_out_ffn(x, oa, oc, wo_bf, gate1, sh, sc, gate2, gain, wu_bf, conv_w, conv_b, wd_bf):
    bx, t, d = x.shape
    tm = min(512, t)
    vec = pl.BlockSpec((1, 1, d), lambda b, i: (b, 0, 0))
    once = pl.Buffered(1)
    return pl.pallas_call(
        _ffn_kernel,
        grid=(bx, t // tm),
        in_specs=_halo_specs(tm, t, d) + _halo_specs(tm, t, Q_W, oa.dtype) + _halo_specs(tm, t, CONV_W, oc.dtype) + [
            pl.BlockSpec((Q_W, d), lambda b, i: (0, 0)), pl.BlockSpec((CONV_W, d), lambda b, i: (1, 0)),
            vec, vec, vec, vec, _full((1, d)),
            pl.BlockSpec((d, 2 * D_FF), lambda b, i: (0, 0), pipeline_mode=once),
            _full((3, D_FF)), _full((1, D_FF)),
            pl.BlockSpec((D_FF, d), lambda b, i: (0, 0), pipeline_mode=once)],
        out_specs=pl.BlockSpec((1, tm, d), lambda b, i: (b, i, 0)),
        out_shape=jax.ShapeDtypeStruct((bx, t, d), F32),
        compiler_params=_params(("arbitrary", "arbitrary")),
        name="mixer_out_ffn",
    )(x, x, x, oa, oa, oa, oc, oc, oc, wo_bf, wo_bf, gate1, sh, sc, gate2, gain, wu_bf, conv_w,
      conv_b.reshape(1, D_FF), wd_bf)


@functools.lru_cache(maxsize=None)
def _rope_tables(t):
    pos = np.arange(t)
    n_freq = HEAD_DIM // 4
    inv = ROPE_THETA ** (-np.arange(n_freq, dtype=np.float64) / n_freq)
    ang_r = (pos // GRID_W)[:, None] * inv
    ang_c = (pos % GRID_W)[:, None] * inv
    zero = np.zeros_like(ang_r)
    cos = np.concatenate([np.cos(ang_r)] * 2 + [np.cos(ang_c)] * 2, 1)
    sa = np.concatenate([-np.sin(ang_r), zero, -np.sin(ang_c), zero], 1)
    sb = np.concatenate([zero, np.sin(ang_r), zero, np.sin(ang_c)], 1)
    tile = lambda m: np.tile(m, (1, LANES // HEAD_DIM)).astype(np.float32)
    return tile(cos), tile(sa), tile(sb)


@functools.lru_cache(maxsize=None)
def _identity_rope_tables(t):
    return np.ones((t, LANES), np.float32), np.zeros((t, LANES), np.float32), np.zeros((t, LANES), np.float32)


@functools.lru_cache(maxsize=None)
def _head_block_diag():
    hid = np.arange(Q_W) // HEAD_DIM
    return np.asarray(hid[:, None] == hid[None, :], dtype=BF16)


def kernel(x, c, ctx, c_ctx, ada_w, ada_b, norm_mix, norm_ffn, mix_w_in, mix_w_out, attn_q_norm, attn_k_norm,
           swa_sink, hy_conv_w, hy_conv_b, hy_w1, hy_b1, hy_w2, hy_b2, hy_w3, hy_b3, hy_w4, hy_freq, hy_bias_d,
           sc_conv_w, ffn_w_up, ffn_conv_w, ffn_conv_b, ffn_w_down):
    b, s, d = x.shape
    s_ctx = ctx.shape[1]
    depth = ada_w.shape[0]
    assert d == D_MODEL and b % 2 == 0 and b + 1 <= MOD_ROWS and s % 1024 == 0 and s_ctx % SUBLANES == 0

    cvec = jnp.concatenate([c, c_ctx[None, :], jnp.zeros((MOD_ROWS - b - 1, d), F32)], axis=0)
    mods = _mods(cvec, ada_w, ada_b)
    rope = [jnp.asarray(t) for t in _rope_tables(s)]
    rope_ctx = [jnp.asarray(t) for t in _identity_rope_tables(s_ctx)]
    bd = jnp.asarray(_head_block_diag())
    xc = ctx
    weights = _weights_bf16([mix_w_in, mix_w_out, ffn_w_up, ffn_w_down])

    for i in range(depth):
        last = i == depth - 1
        j = i // 2
        lat = [mods[i, :b, k * d:(k + 1) * d][:, None, :] for k in range(6)]
        cx = [jnp.broadcast_to(mods[i, b, k * d:(k + 1) * d][None, None, :], (b, 1, d)) for k in range(6)]
        w_in, w_out, w_up, w_down = (w[i] for w in weights)
        g_mix = norm_mix[i].reshape(1, d)
        g_ffn = norm_ffn[i].reshape(1, d)
        qg = jnp.tile(attn_q_norm[i], N_Q_HEADS).reshape(1, Q_W)
        kg = jnp.tile(attn_k_norm[i], N_KV_HEADS).reshape(1, KV_W)

        conv = (hy_conv_w[j], hy_conv_b[j]) if i % 2 == 0 else (sc_conv_w[j],)
        q, k, vt, *mix = _inproj(x, lat[0], lat[1], g_mix, w_in, qg, kg, bd, *rope, *conv)
        qc, kc, vct, *mixc = _inproj(xc, cx[0], cx[1], g_mix, w_in, qg, kg, bd, *rope_ctx, *conv)

        bound = _score_bound(attn_q_norm[i], attn_k_norm[i])
        if i % 2 == 0:
            fargs = (hy_w1[j], hy_b1[j], hy_w2[j], hy_b2[j], hy_w3[j], hy_b3[j], hy_w4[j], hy_freq[j])
            o_attn = _dense_attention(q, k, vt, kc, vct, bound=bound)
            kf, ks = _implicit_filter(s, *fargs)
            o_conv = _long_conv_mixer(*mix, kf, ks, hy_bias_d[j])
            if not last:
                oc_attn = _dense_attention(qc, kc, vct, bound=bound)
                kfc, ksc = _implicit_filter(s_ctx, *fargs)
                oc_conv = _small_conv_mixer(*mixc, kfc, ksc, hy_bias_d[j])
        else:
            sink = swa_sink[j].reshape(N_KV_HEADS, GROUP)
            o_attn = _banded_attention(q, k, vt, kc, vct, sink, bound)
            o_conv = mix[0]
            if not last:
                oc_attn = _dense_attention(qc, kc, vct, sink=sink)
                oc_conv = mixc[0]

        ffn = (g_ffn, w_up, ffn_conv_w[i], ffn_conv_b[i], w_down)
        x = _mixer_out_ffn(x, o_attn, o_conv, w_out, lat[2], lat[3], lat[4], lat[5], *ffn)
        if not last:
            xc = _mixer_out_ffn(xc, oc_attn, oc_conv, w_out, cx[2], cx[3], cx[4], cx[5], *ffn)
    return x
```

```python
import functools
import math

import numpy as np
import jax
import jax.numpy as jnp
from jax import lax
from jax.experimental import pallas as pl
from jax.experimental.pallas import tpu as pltpu

F32 = jnp.float32
BF16 = jnp.bfloat16
HI = lax.Precision.HIGHEST

D_MODEL = 1024
GRID_W = 64
HEAD_DIM = 64
N_Q_HEADS = 8
N_KV_HEADS = 4
GROUP = N_Q_HEADS // N_KV_HEADS
Q_W = N_Q_HEADS * HEAD_DIM
KV_W = N_KV_HEADS * HEAD_DIM
QKV_W = Q_W + 2 * KV_W
CONV_W = D_MODEL // 2
MIX_IN_W = QKV_W + 3 * CONV_W
WINDOW = 128
ROPE_THETA = 10000.0
FILT_EMB = 33
FILT_WIDTH = 64
DECAY_TARGET = 1e-2
FAST_DECAY_PCT = 0.3
SLOW_DECAY_PCT = 1.5
DECAY_SHIFT = 0.05
D_FF = 2816
NEG_INF = -1e30
RMS_EPS = 1e-6
LOG2E = 1.4426950408889634
Q_SCALE = HEAD_DIM ** -0.5 * LOG2E

LANES = 128
SUBLANES = 8
DFT_N1 = 128
VMEM_LIMIT_MB = 56
VT_ROWS = LANES
MOD_ROWS = SUBLANES
PROJ_SUB_ROWS = 128


def _params(sem, vmem_mb=VMEM_LIMIT_MB):
    return pltpu.CompilerParams(dimension_semantics=sem, vmem_limit_bytes=vmem_mb * 1024 * 1024)


def _full(shape):
    nd = len(shape)
    return pl.BlockSpec(shape, lambda *_: (0,) * nd)


def _mods_kernel(c_ref, w_ref, b_ref, o_ref):
    c = c_ref[...]
    s = c / (1.0 + jnp.exp(-c))
    o_ref[0] = _dot_bf16x3(s, w_ref[0]) + b_ref[0]


def _mods(cvec, ada_w, ada_b):
    depth, d, n6 = ada_w.shape
    tn = 1536
    return pl.pallas_call(
        _mods_kernel,
        grid=(depth, n6 // tn),
        in_specs=[_full((MOD_ROWS, d)),
                  pl.BlockSpec((1, d, tn), lambda l, j: (l, 0, j)),
                  pl.BlockSpec((1, 1, tn), lambda l, j: (l, 0, j))],
        out_specs=pl.BlockSpec((1, MOD_ROWS, tn), lambda l, j: (l, 0, j)),
        out_shape=jax.ShapeDtypeStruct((depth, MOD_ROWS, n6), F32),
        compiler_params=_params(("arbitrary", "arbitrary")),
        name="ada_mods",
    )(cvec, ada_w, ada_b.reshape(depth, 1, n6))


def _modulated_norm(x, gain, shift, scale):
    ms = jnp.mean(x * x, axis=-1, keepdims=True)
    return (x * lax.rsqrt(ms + RMS_EPS)) * gain * (1.0 + scale) + shift


WEIGHT_CAST_STEPS = 8


def _cast_kernel(*refs):
    n = len(refs) // 2
    for w_ref, o_ref in zip(refs[:n], refs[n:]):
        o_ref[...] = w_ref[0].astype(o_ref.dtype)


def _weights_bf16(stacked):
    depth = stacked[0].shape[0]
    in_specs, out_specs, out_shape, args = [], [], [], []
    for w in stacked:
        _, rows, cols = w.shape
        tr = rows // WEIGHT_CAST_STEPS
        for layer in range(depth):
            in_specs.append(pl.BlockSpec((1, tr, cols), lambda r, layer=layer: (layer, r, 0)))
            out_specs.append(pl.BlockSpec((tr, cols), lambda r: (r, 0)))
            out_shape.append(jax.ShapeDtypeStruct((rows, cols), BF16))
            args.append(w)
    outs = pl.pallas_call(
        _cast_kernel,
        grid=(WEIGHT_CAST_STEPS,),
        in_specs=in_specs,
        out_specs=out_specs,
        out_shape=out_shape,
        compiler_params=_params(("arbitrary",)),
        name="weights_bf16",
    )(*args)
    return [outs[k * depth:(k + 1) * depth] for k in range(len(stacked))]


def _halo_specs(tm, t, width, dtype=F32):
    rows = SUBLANES * 4 // jnp.dtype(dtype).itemsize
    nb = tm // rows
    last = t // rows - 1
    return [pl.BlockSpec((1, tm, width), lambda b, i: (b, i, 0)),
            pl.BlockSpec((1, rows, width), lambda b, i: (b, jnp.maximum(i * nb - 1, 0), 0)),
            pl.BlockSpec((1, rows, width), lambda b, i: (b, jnp.minimum((i + 1) * nb, last), 0))]


def _inproj_kernel(*refs, mode):
    (x_ref, xp_ref, xn_ref, sh_ref, sc_ref, g_ref, w_ref, qg_ref, kg_ref, bd_ref,
     cos_ref, sa_ref, sb_ref, cw_ref) = refs[:14]
    if mode == "hyena":
        cb_ref, q_ref, k_ref, vt_ref, u_ref, x0_ref = refs[14:]
    else:
        q_ref, k_ref, vt_ref, oc_ref = refs[14:]
    i = pl.program_id(1)
    nt = pl.num_programs(1)
    tm = x_ref.shape[1]
    sub = min(PROJ_SUB_ROWS, tm)
    gain, shift, scale = g_ref[...], sh_ref[0], sc_ref[0]

    xh = jnp.concatenate([xp_ref[0], xn_ref[0]], axis=0)
    zh = jnp.dot(_modulated_norm(xh, gain, shift, scale).astype(BF16), w_ref[:, QKV_W:], preferred_element_type=F32)
    before = jnp.where(i > 0, zh[:SUBLANES], 0.0)
    after = jnp.where(i < nt - 1, zh[SUBLANES:], 0.0)
    cw = cw_ref[...]

    def conv_stage(ext, rows):
        n = sub + 2 * SUBLANES
        mid = slice(SUBLANES, sub + SUBLANES)
        if mode == "hyena":
            c = (pltpu.roll(ext, 1, 0)[mid] * cw[0:1] + ext[mid] * cw[1:2] + pltpu.roll(ext, n - 1, 0)[mid] * cw[2:3]
                 + cb_ref[...])
            x0_ref[0, rows] = c[:, :CONV_W]
            u_ref[0, rows] = c[:, 2 * CONV_W:] * c[:, CONV_W:2 * CONV_W]
        else:
            pr = ext[:, CONV_W:2 * CONV_W] * ext[:, 2 * CONV_W:]
            conv = pltpu.roll(pr, 1, 0)[mid] * cw[0:1] + pr[mid] * cw[1:2] + pltpu.roll(pr, n - 1, 0)[mid] * cw[2:3]
            oc_ref[0, rows] = (ext[mid, :CONV_W] * conv).astype(oc_ref.dtype)

    def head_norm_rope(t, hgain, bd, cos, sa, sb):
        sq = (t * t).astype(BF16)
        ssq = jnp.concatenate([jnp.dot(sq[:, c:c + MXU_DIM], bd[:MXU_DIM, :MXU_DIM], preferred_element_type=F32)
                               for c in range(0, t.shape[1], MXU_DIM)], axis=1)
        t = t * lax.rsqrt(ssq * (1.0 / HEAD_DIM) + RMS_EPS) * hgain
        outs = []
        for j in range(t.shape[1] // LANES):
            tj = t[:, j * LANES:(j + 1) * LANES]
            outs.append(tj * cos + pltpu.roll(tj, LANES - 16, 1) * sa + pltpu.roll(tj, 16, 1) * sb)
        return outs

    held = None
    for r0 in range(0, tm, sub):
        rows = slice(r0, r0 + sub)
        h = _modulated_norm(x_ref[0, rows], gain, shift, scale)
        hb = h.astype(BF16)
        z = jnp.dot(hb, w_ref[:, :QKV_W], preferred_element_type=F32)
        zc = jnp.dot(hb, w_ref[:, QKV_W:], preferred_element_type=F32)
        if held is not None:
            conv_stage(jnp.concatenate([before, held[0], zc[:SUBLANES]], axis=0), held[1])
            before = held[0][-SUBLANES:]
        held = (zc, rows)
        tabs = (cos_ref[rows], sa_ref[rows], sb_ref[rows])
        qs = head_norm_rope(z[:, :Q_W], qg_ref[...], bd_ref[...], *tabs)
        ks = head_norm_rope(z[:, Q_W:Q_W + KV_W], kg_ref[...], bd_ref[:KV_W, :KV_W], *tabs)
        for j, qj in enumerate(qs):
            qj = (qj * Q_SCALE).astype(BF16)
            q_ref[0, 2 * j, rows] = qj[:, :HEAD_DIM]
            q_ref[0, 2 * j + 1, rows] = qj[:, HEAD_DIM:]
        for j, kj in enumerate(ks):
            kj = kj.astype(BF16)
            k_ref[0, 2 * j, rows] = kj[:, :HEAD_DIM]
            k_ref[0, 2 * j + 1, rows] = kj[:, HEAD_DIM:]
        vt = z[:, Q_W + KV_W:QKV_W].T
        ones_row = (lax.broadcasted_iota(jnp.int32, (VT_ROWS - HEAD_DIM, sub), 0) == 0).astype(F32)
        for hh in range(N_KV_HEADS):
            vt_ref[0, hh, :, rows] = jnp.concatenate([vt[hh * HEAD_DIM:(hh + 1) * HEAD_DIM], ones_row],
                                                     axis=0).astype(BF16)

    conv_stage(jnp.concatenate([before, held[0], after], axis=0), held[1])


def _inproj(x, sh, sc, gain, w_bf, qg, kg, bd, cos, sa, sb, conv_w, conv_b=None):
    bx, t, d = x.shape
    tm = min(1024, t)
    mode = "hyena" if conv_b is not None else "short"
    vec = pl.BlockSpec((1, 1, d), lambda b, i: (b, 0, 0))
    tab = pl.BlockSpec((tm, LANES), lambda b, i: (i, 0))
    row = lambda w: pl.BlockSpec((1, tm, w), lambda b, i: (b, i, 0))
    in_specs = _halo_specs(tm, t, d) + [vec, vec, _full((1, d)), _full((d, MIX_IN_W)), _full((1, Q_W)),
                                        _full((1, KV_W)), _full((Q_W, Q_W)), tab, tab, tab, _full(conv_w.shape)]
    args = [x, x, x, sh, sc, gain, w_bf, qg, kg, bd, cos, sa, sb, conv_w]
    out_specs = [pl.BlockSpec((1, N_Q_HEADS, tm, HEAD_DIM), lambda b, i: (b, 0, i, 0)),
                 pl.BlockSpec((1, N_KV_HEADS, tm, HEAD_DIM), lambda b, i: (b, 0, i, 0)),
                 pl.BlockSpec((1, N_KV_HEADS, VT_ROWS, tm), lambda b, i: (b, 0, 0, i))]
    out_shape = [jax.ShapeDtypeStruct((bx, N_Q_HEADS, t, HEAD_DIM), BF16),
                 jax.ShapeDtypeStruct((bx, N_KV_HEADS, t, HEAD_DIM), BF16),
                 jax.ShapeDtypeStruct((bx, N_KV_HEADS, VT_ROWS, t), BF16)]
    if mode == "hyena":
        in_specs.append(_full((1, 3 * CONV_W)))
        args.append(conv_b.reshape(1, 3 * CONV_W))
        out_specs += [row(CONV_W), row(CONV_W)]
        out_shape += [jax.ShapeDtypeStruct((bx, t, CONV_W), F32)] * 2
    else:
        out_specs.append(row(CONV_W))
        out_shape.append(jax.ShapeDtypeStruct((bx, t, CONV_W), BF16))
    return pl.pallas_call(
        functools.partial(_inproj_kernel, mode=mode),
        grid=(bx, t // tm),
        in_specs=in_specs,
        out_specs=out_specs,
        out_shape=out_shape,
        compiler_params=_params(("arbitrary", "arbitrary")),
        name="in_proj",
    )(*args)


def _score_bound(q_gain, k_gain):
    b = HEAD_DIM * Q_SCALE * BOUND_MARGIN * jnp.max(jnp.abs(q_gain)) * jnp.max(jnp.abs(k_gain))
    return b.reshape(1, 1).astype(F32)


def _scores_t(ks, q):
    return lax.dot_general(ks, q, (((1,), (1,)), ((), ())), preferred_element_type=F32)


def _sink_column(sink_ref, hh, tq, m, acc):
    col = lax.broadcasted_iota(jnp.int32, m.shape, 1)
    sink = jnp.where(col < tq, sink_ref[hh, 0], sink_ref[hh, 1]) * LOG2E
    m_new = jnp.maximum(m, sink)
    den_row = lax.broadcasted_iota(jnp.int32, acc.shape, 0) == HEAD_DIM
    return acc * jnp.exp2(m - m_new) + jnp.where(den_row, jnp.exp2(sink - m_new), 0.0)


def _attn_finish(acc, tq):
    o = (acc / acc[HEAD_DIM:HEAD_DIM + 1]).T
    return jnp.concatenate([o[:tq, :HEAD_DIM], o[tq:, :HEAD_DIM]], axis=1)


EXP_CHUNK_ELEMS = 32 * SUBLANES * LANES


def _score_stage(ks, q, s_ref, bias=None):
    s = _scores_t(ks, q)
    if bias is not None:
        s = s + bias
    s_ref[0:s.shape[0]] = s
    return jnp.max(s, axis=0, keepdims=True)


def _softmax_stage(s_ref, p_ref, n, vt, mx, m, acc_ref):
    m_new = jnp.maximum(m, mx)
    rows = EXP_CHUNK_ELEMS // m.shape[1]
    for c in range(0, n, rows):
        p_ref[c:c + rows] = jnp.exp2(s_ref[c:c + rows] - m_new).astype(BF16)
    acc_ref[...] = acc_ref[...] * jnp.exp2(m - m_new) + jnp.dot(vt, p_ref[0:n], preferred_element_type=F32)
    return m_new


BOUNDED_TILES = 8
BOUND_MARGIN = 1.01
SAFE_SHIFT = 40.0


def _flash_exact(q, k_tile, vt_tile, ctx, sa_ref, sb_ref, p_ref, acc_ref, *, tk, n_main):
    pa_ref, pb_ref = p_ref.at[0:tk], p_ref.at[tk:2 * tk]
    acc_ref[...] = jnp.zeros_like(acc_ref)
    m = jnp.full((1, q.shape[0]), NEG_INF, F32)
    mx_a = _score_stage(k_tile(0), q, sa_ref)
    if n_main > 1:
        def body(i, carry):
            m, mx_a = carry
            mx_b = _score_stage(k_tile(2 * i + 1), q, sb_ref)
            m = _softmax_stage(sa_ref, pa_ref, tk, vt_tile(2 * i), mx_a, m, acc_ref)
            mx_a = _score_stage(k_tile(2 * i + 2), q, sa_ref)
            m = _softmax_stage(sb_ref, pb_ref, tk, vt_tile(2 * i + 1), mx_b, m, acc_ref)
            return m, mx_a

        m, mx_a = lax.fori_loop(0, n_main // 2 - 1, body, (m, mx_a))
        mx_b = _score_stage(k_tile(n_main - 1), q, sb_ref)
        m = _softmax_stage(sa_ref, pa_ref, tk, vt_tile(n_main - 2), mx_a, m, acc_ref)
        last = (sb_ref, pb_ref, mx_b)
        spare = (sa_ref, pa_ref)
    else:
        last = (sa_ref, pa_ref, mx_a)
        spare = (sb_ref, pb_ref)
    if ctx is not None:
        mx_c = _score_stage(ctx[0], q, spare[0])
    m = _softmax_stage(last[0], last[1], tk, vt_tile(n_main - 1), last[2], m, acc_ref)
    if ctx is not None:
        m = _softmax_stage(spare[0], spare[1], ctx[0].shape[0], ctx[1], mx_c, m, acc_ref)
    return m


def _exp2_as_bf16(s):
    return jnp.exp2(s).astype(BF16)


def _flash_bounded(q, k_ref, vt_ref, ctx, p_ref, acc_ref, *, tk, n_main):
    step = BOUNDED_TILES * tk if n_main % BOUNDED_TILES == 0 else tk

    def group(j):
        off = pl.multiple_of(j * step, step)
        for c in range(0, step, tk):
            p_ref[c:c + tk] = _exp2_as_bf16(_scores_t(k_ref[0, 0, pl.ds(off + c, tk), :], q))
        return jnp.dot(vt_ref[0, 0, :, pl.ds(off, step)], p_ref[0:step], preferred_element_type=F32)

    first = group(0)
    if ctx is not None:
        pc_ref = p_ref.at[BOUNDED_TILES * tk:BOUNDED_TILES * tk + ctx[0].shape[0]]
        pc_ref[...] = _exp2_as_bf16(_scores_t(ctx[0], q))
        first = first + jnp.dot(ctx[1], pc_ref[...], preferred_element_type=F32)
    acc_ref[...] = first

    def body(j, carry):
        acc_ref[...] += group(j)
        return carry

    lax.fori_loop(1, n_main * tk // step, body, 0)


def _flash_kernel(*refs, tq, tk, n_main, has_ctx, has_sink):
    sa_ref, sb_ref, p_ref, acc_ref = refs[-4:]
    o_ref = refs[-5]
    refs = list(refs[:-5])
    sink_ref = refs.pop(0) if has_sink else None
    bound_ref = None if has_sink else refs.pop(0)
    q_ref, k_ref, vt_ref = refs[:3]
    m_cols = GROUP * tq
    q = q_ref[0].reshape(m_cols, HEAD_DIM)
    ctx = (refs[3][0, 0], refs[4][0, 0]) if has_ctx else None

    def k_tile(j):
        return k_ref[0, 0, pl.ds(pl.multiple_of(j * tk, tk), tk), :]

    def vt_tile(j):
        return vt_ref[0, 0, :, pl.ds(pl.multiple_of(j * tk, tk), tk)]

    exact = functools.partial(_flash_exact, q, k_tile, vt_tile, ctx, sa_ref, sb_ref, p_ref, acc_ref,
                              tk=tk, n_main=n_main)
    if has_sink:
        acc = _sink_column(sink_ref, pl.program_id(1), tq, exact(), acc_ref[...])
    else:
        bound = bound_ref[0, 0]
        safe = bound <= SAFE_SHIFT

        @pl.when(safe)
        def _():
            _flash_bounded(q, k_ref, vt_ref, ctx, p_ref, acc_ref, tk=tk, n_main=n_main)

        @pl.when(jnp.logical_not(safe))
        def _():
            exact()

        acc = acc_ref[...]
    o_ref[0] = _attn_finish(acc, tq).astype(o_ref.dtype)


def _kv_specs(n):
    return [pl.BlockSpec((1, 1, n, HEAD_DIM), lambda b, h, i: (b, h, 0, 0)),
            pl.BlockSpec((1, 1, VT_ROWS, n), lambda b, h, i: (b, h, 0, 0))]


def _dense_attention(q, k, vt, kc=None, vct=None, *, bound=None, sink=None):
    bx, _, t, _ = q.shape
    s = k.shape[2]
    tq = min(1024, t)
    tk = min(512, s)
    has_ctx = kc is not None
    has_sink = sink is not None
    in_specs = [pl.BlockSpec((1, GROUP, tq, HEAD_DIM), lambda b, h, i: (b, h, i, 0))] + _kv_specs(s)
    args = [q, k, vt]
    if has_ctx:
        in_specs += _kv_specs(kc.shape[2])
        args += [kc, vct]
    in_specs = [pl.BlockSpec(memory_space=pltpu.SMEM)] + in_specs
    args = [sink if has_sink else bound] + args
    n_main = s // tk
    assert n_main == 1 or n_main % 2 == 0
    n_ctx = kc.shape[2] if has_ctx else 0
    assert n_ctx <= tk
    m_cols = GROUP * tq
    return pl.pallas_call(
        functools.partial(_flash_kernel, tq=tq, tk=tk, n_main=n_main, has_ctx=has_ctx, has_sink=has_sink),
        grid=(bx, N_KV_HEADS, t // tq),
        in_specs=in_specs,
        out_specs=pl.BlockSpec((1, tq, LANES), lambda b, h, i: (b, i, h)),
        out_shape=jax.ShapeDtypeStruct((bx, t, Q_W), BF16),
        scratch_shapes=[pltpu.VMEM((tk, m_cols), F32), pltpu.VMEM((tk, m_cols), F32),
                        pltpu.VMEM((BOUNDED_TILES * tk + n_ctx, m_cols), BF16), pltpu.VMEM((VT_ROWS, m_cols), F32)],
        compiler_params=_params(("arbitrary", "arbitrary", "arbitrary")),
        name="dense_attention",
    )(*args)


BAND_SUB = 256


@functools.lru_cache(maxsize=None)
def _band_bias():
    span = BAND_SUB + 2 * WINDOW
    kr = np.arange(span)[None, :, None]
    qc = (np.arange(GROUP * BAND_SUB) % BAND_SUB)[None, None, :]
    rel = np.arange(3)[:, None, None]
    return np.where(np.abs(kr - qc - rel * WINDOW) <= WINDOW, 0.0, NEG_INF).astype(np.float32)


def _banded_kernel(sink_ref, bound_ref, q_ref, k_ref, vt_ref, kc_ref, vct_ref, bias_ref, o_ref,
                   sl_ref, sc_ref, pl_ref, pc_ref, acc_ref, *, n_sub, s_len):
    i = pl.program_id(2)
    hh = pl.program_id(1)
    sub = BAND_SUB
    m_cols = GROUP * sub
    span = sub + 2 * WINDOW
    n_ctx = kc_ref.shape[2]

    def window(u):
        q = q_ref[0, :, u * sub:(u + 1) * sub, :].reshape(m_cols, HEAD_DIM)
        q0 = (i * n_sub + u) * sub
        start = pl.multiple_of(jnp.clip(q0 - WINDOW, 0, s_len - span), WINDOW)
        return q, start, bias_ref[(q0 - start) // WINDOW]

    shift = jnp.maximum(bound_ref[0, 0], jnp.maximum(sink_ref[hh, 0], sink_ref[hh, 1]) * LOG2E)
    safe = shift <= SAFE_SHIFT

    @pl.when(safe)
    def _():
        col = lax.broadcasted_iota(jnp.int32, (1, m_cols), 1)
        sink = jnp.where(col < sub, sink_ref[hh, 0], sink_ref[hh, 1]) * LOG2E
        den_row = lax.broadcasted_iota(jnp.int32, (VT_ROWS, m_cols), 0) == HEAD_DIM
        sink_den = jnp.where(den_row, jnp.exp2(sink), 0.0)
        for u in range(n_sub):
            q, start, bias = window(u)
            pl_u, pc_u = pl_ref.at[u], pc_ref.at[u]
            pl_u[...] = _exp2_as_bf16(_scores_t(k_ref[0, 0, pl.ds(start, span), :], q) + bias)
            pc_u[...] = _exp2_as_bf16(_scores_t(kc_ref[0, 0], q))
            acc = (jnp.dot(vt_ref[0, 0, :, pl.ds(start, span)], pl_u[...], preferred_element_type=F32)
                   + jnp.dot(vct_ref[0, 0], pc_u[...], preferred_element_type=F32) + sink_den)
            o_ref[0, u * sub:(u + 1) * sub] = _attn_finish(acc, sub).astype(o_ref.dtype)

    @pl.when(jnp.logical_not(safe))
    def _():
        stats = []
        for u in range(n_sub):
            q, start, bias = window(u)
            mx_l = _score_stage(k_ref[0, 0, pl.ds(start, span), :], q, sl_ref.at[u], bias=bias)
            mx_c = _score_stage(kc_ref[0, 0], q, sc_ref.at[u])
            stats.append((start, mx_l, mx_c))
        for u in range(n_sub):
            start, mx_l, mx_c = stats[u]
            acc_u = acc_ref.at[u]
            acc_u[...] = jnp.zeros((VT_ROWS, m_cols), F32)
            m = jnp.full((1, m_cols), NEG_INF, F32)
            m = _softmax_stage(sl_ref.at[u], pl_ref.at[u], span, vt_ref[0, 0, :, pl.ds(start, span)], mx_l, m, acc_u)
            m = _softmax_stage(sc_ref.at[u], pc_ref.at[u], n_ctx, vct_ref[0, 0], mx_c, m, acc_u)
            acc = _sink_column(sink_ref, hh, sub, m, acc_u[...])
            o_ref[0, u * sub:(u + 1) * sub] = _attn_finish(acc, sub).astype(o_ref.dtype)


def _banded_attention(q, k, vt, kc, vct, sink, bound):
    bx, _, t, _ = q.shape
    n_ctx = kc.shape[2]
    n_sub = next(k for k in (8, 4, 2, 1) if t % (k * BAND_SUB) == 0)
    tq = n_sub * BAND_SUB
    m_cols = GROUP * BAND_SUB
    span = BAND_SUB + 2 * WINDOW
    bias = jnp.asarray(_band_bias())
    return pl.pallas_call(
        functools.partial(_banded_kernel, n_sub=n_sub, s_len=t),
        grid=(bx, N_KV_HEADS, t // tq),
        in_specs=[pl.BlockSpec(memory_space=pltpu.SMEM), pl.BlockSpec(memory_space=pltpu.SMEM),
                  pl.BlockSpec((1, GROUP, tq, HEAD_DIM), lambda b, h, i: (b, h, i, 0))]
                 + _kv_specs(t) + _kv_specs(n_ctx) + [_full(bias.shape)],
        out_specs=pl.BlockSpec((1, tq, LANES), lambda b, h, i: (b, i, h)),
        out_shape=jax.ShapeDtypeStruct((bx, t, Q_W), BF16),
        scratch_shapes=[pltpu.VMEM((n_sub, span, m_cols), F32), pltpu.VMEM((n_sub, n_ctx, m_cols), F32),
                        pltpu.VMEM((n_sub, span, m_cols), BF16), pltpu.VMEM((n_sub, n_ctx, m_cols), BF16),
                        pltpu.VMEM((n_sub, VT_ROWS, m_cols), F32)],
        compiler_params=_params(("arbitrary", "arbitrary", "arbitrary")),
        name="banded_attention",
    )(sink, bound, q, k, vt, kc, vct, bias)


@functools.lru_cache(maxsize=None)
def _filter_features(n):
    j = np.arange(2 * n)
    d = np.where(j <= n, j, 2 * n - j)
    d = np.where(j == n, 0, d)
    bands = (FILT_EMB - 1) // 2
    t01 = np.linspace(0.0, 1.0, n)[d]
    w = 2.0 * np.pi * d.astype(np.float64) / n
    f = np.linspace(1e-4, bands - 1, bands)[None, :]
    feats = np.zeros((2 * n, LANES), np.float64)
    feats[:, 0] = t01
    feats[:, 1:1 + bands] = np.cos(f * w[:, None])
    feats[:, 1 + bands:FILT_EMB] = -np.sin(f * w[:, None])
    feats[:, 64] = t01
    feats[:, 65] = (j < n)
    feats[:, 66] = (j != n)
    return feats.astype(np.float32)


def _dot_bf16x3(a, b):
    a_hi = a.astype(BF16)
    a_lo = (a - a_hi.astype(F32)).astype(BF16)
    b_hi = b.astype(BF16)
    b_lo = (b - b_hi.astype(F32)).astype(BF16)
    dot = lambda x, y: jnp.dot(x, y, preferred_element_type=F32)
    return dot(a_hi, b_hi) + dot(a_hi, b_lo) + dot(a_lo, b_hi)


def _filter_kernel(f_ref, w1_ref, b1_ref, w2_ref, b2_ref, w3_ref, b3_ref, w4_ref, fr_ref, dl_ref, k_ref, s_ref):
    f = f_ref[...]
    fr = fr_ref[...]
    mm = lambda a, b: jnp.dot(a, b, preferred_element_type=F32, precision=HI)
    h = jnp.sin(fr * (mm(w1_ref[...], f) + b1_ref[...]))
    h = jnp.sin(fr * (mm(w2_ref[...], h) + b2_ref[...]))
    h = jnp.sin(fr * (mm(w3_ref[...], h) + b3_ref[...]))
    hf = _dot_bf16x3(w4_ref[...], h)
    win = jnp.exp(-dl_ref[...] * f[64:65]) + DECAY_SHIFT
    k = (jnp.where(f[65:66] > 0.5, hf[:CONV_W], hf[CONV_W:]) * win * f[66:67]).T
    k_ref[...] = k

    @pl.when(pl.program_id(0) == 0)
    def _():
        s_ref[...] = jnp.zeros_like(s_ref)

    s_ref[...] += jnp.sum(jnp.abs(k), axis=0, keepdims=True)


def _implicit_filter(n, w1, b1, w2, b2, w3, b3, w4, freq):
    feats_t = jnp.asarray(np.ascontiguousarray(_filter_features(n).T))
    col = lambda a: a.reshape(-1, 1)
    w1t = jnp.pad(w1.T, ((0, 0), (0, LANES - FILT_EMB)))
    deltas = np.abs(np.linspace(math.log(DECAY_TARGET) / SLOW_DECAY_PCT, math.log(DECAY_TARGET) / FAST_DECAY_PCT,
                                CONV_W)).astype(np.float32).reshape(CONV_W, 1)
    tr = min(1024, 2 * n)
    sq = _full((FILT_WIDTH, FILT_WIDTH))
    vec = _full((FILT_WIDTH, 1))
    return pl.pallas_call(
        _filter_kernel,
        grid=(2 * n // tr,),
        in_specs=[pl.BlockSpec((LANES, tr), lambda i: (0, i)), _full((FILT_WIDTH, LANES)), vec, sq, vec, sq, vec,
                  _full((2 * CONV_W, FILT_WIDTH)), vec, _full((CONV_W, 1))],
        out_specs=[pl.BlockSpec((tr, CONV_W), lambda i: (i, 0)), _full((1, CONV_W))],
        out_shape=[jax.ShapeDtypeStruct((2 * n, CONV_W), F32), jax.ShapeDtypeStruct((1, CONV_W), F32)],
        compiler_params=_params(("arbitrary",)),
        name="hyena_filter",
    )(feats_t, w1t, col(b1), w2.T, col(b2), w3.T, col(b3), w4.T, col(freq), jnp.asarray(deltas))


def _twiddle(idx, mod):
    ang = 2.0 * np.pi * (idx % mod) / mod
    return np.cos(ang), -np.sin(ang)


def _real_form(mr, mi):
    return np.concatenate([np.concatenate([mr, -mi], -1), np.concatenate([mi, mr], -1)], -2)


@functools.lru_cache(maxsize=None)
def _dft_tables(n2):
    n1 = DFT_N1
    n = n1 * n2
    h = n2 // 2
    a2 = np.arange(n2)
    fr, fi = _twiddle(np.outer(a2, a2), n2)
    m_data = _real_form(fr[:, :h], fi[:, :h])
    m_filt = np.concatenate([fr, fi], 0)
    k2 = a2[:, None, None]
    k1 = np.arange(n1)[None, :, None]
    c1 = np.arange(n1)[None, None, :]
    g = _real_form(*_twiddle(c1 * (n2 * k1 + k2), n))
    a1 = np.arange(n1)
    f1 = _real_form(*_twiddle(np.outer(a1, a1), n1))
    t2 = a1[:, None, None]
    t1 = np.arange(h)[None, :, None]
    j1 = a2[None, None, :]
    hh = _real_form(*_twiddle(j1 * (n1 * t1 + t2), n))
    cast = lambda m: np.asarray(m, dtype=BF16)
    return cast(m_data), cast(m_filt), cast(g), cast(f1), cast(hh)


DFT_ROW_CHUNK = 2 * SUBLANES


def _dft_rows_kernel(m_ref, *refs):
    o_ref = refs[-1]
    cols = [jnp.concatenate([r[0, :, t, :] for r in refs[:-1]], axis=0) for t in range(DFT_ROW_CHUNK)]
    rhs = jnp.concatenate(cols, axis=1).astype(BF16)
    out = jnp.dot(m_ref[...], rhs, preferred_element_type=F32)
    half = out.shape[0] // 2
    o_ref[0, 0] = out[:half].astype(o_ref.dtype)
    o_ref[0, 1] = out[half:].astype(o_ref.dtype)


def _dft_stage1(mat, views, pairs, rows, n2, c):
    nin = len(views)
    lc = DFT_ROW_CHUNK * c
    in_specs = [_full(mat.shape)] + [
        pl.BlockSpec((1, rows, DFT_ROW_CHUNK, c), (lambda p, j, a=a: (nin * p + a, 0, j, 0))) for a in range(nin)]
    return pl.pallas_call(
        _dft_rows_kernel,
        grid=(pairs, DFT_N1 // DFT_ROW_CHUNK),
        in_specs=in_specs,
        out_specs=pl.BlockSpec((1, 2, n2, lc), lambda p, j: (p, 0, 0, j)),
        out_shape=jax.ShapeDtypeStruct((pairs, 2, n2, DFT_N1 * c), BF16),
        compiler_params=_params(("arbitrary", "arbitrary")),
        name="dft_stage1",
    )(mat, *views)


def _spectrum_kernel(a_ref, g_ref, sc_ref, o_ref, *, kb):
    for j in range(kb):
        rhs = jnp.concatenate([a_ref[0, 0, j], a_ref[0, 1, j]], axis=0)
        x = jnp.dot(g_ref[j], rhs, preferred_element_type=F32)
        o_ref[0, j] = (x[:DFT_N1] * sc_ref[...]).astype(o_ref.dtype)
        o_ref[1, j] = (x[DFT_N1:] * sc_ref[...]).astype(o_ref.dtype)


def _filter_spectrum(a, g, scale, n2, kb):
    c = a.shape[-1]
    return pl.pallas_call(
        functools.partial(_spectrum_kernel, kb=kb),
        grid=(n2 // kb,),
        in_specs=[pl.BlockSpec((1, 2, kb, DFT_N1, c), lambda k: (0, 0, k, 0, 0)),
                  pl.BlockSpec((kb, 2 * DFT_N1, 2 * DFT_N1), lambda k: (k, 0, 0)), _full((1, c))],
        out_specs=pl.BlockSpec((2, kb, DFT_N1, c), lambda k: (0, k, 0, 0)),
        out_shape=jax.ShapeDtypeStruct((2, n2, DFT_N1, c), BF16),
        compiler_params=_params(("arbitrary",)),
        name="filter_spectrum",
    )(a, g, scale)


def _dft_mid_kernel(a_ref, g_ref, kh_ref, f_ref, o_ref, *, kb):
    for j in range(kb):
        rhs = jnp.concatenate([a_ref[0, 0, j], a_ref[0, 1, j]], axis=0)
        x = jnp.dot(g_ref[j], rhs, preferred_element_type=F32)
        xr, xi = x[:DFT_N1], x[DFT_N1:]
        kr, ki = kh_ref[0, j].astype(F32), kh_ref[1, j].astype(F32)
        yr = xr * kr - xi * ki
        yi = xr * ki + xi * kr
        v = jnp.concatenate([yr, -yi], axis=0).astype(BF16)
        b = jnp.dot(f_ref[...], v, preferred_element_type=F32)
        o_ref[0, 0, :, j, :] = b[:DFT_N1]
        o_ref[0, 1, :, j, :] = b[DFT_N1:]


def _dft_mid(a, g, khat, f1, pairs, n2, kb):
    c = a.shape[-1]
    blk = pl.BlockSpec((1, 2, kb, DFT_N1, c), lambda k, p: (p, 0, k, 0, 0))
    return pl.pallas_call(
        functools.partial(_dft_mid_kernel, kb=kb),
        grid=(n2 // kb, pairs),
        in_specs=[blk, pl.BlockSpec((kb, 2 * DFT_N1, 2 * DFT_N1), lambda k, p: (k, 0, 0)),
                  pl.BlockSpec((2, kb, DFT_N1, c), lambda k, p: (0, k, 0, 0)), _full(f1.shape)],
        out_specs=pl.BlockSpec((1, 2, DFT_N1, kb, c), lambda k, p: (p, 0, 0, k, 0)),
        out_shape=jax.ShapeDtypeStruct((pairs, 2, DFT_N1, n2, c), F32),
        compiler_params=_params(("arbitrary", "arbitrary")),
        name="dft_mid",
    )(a, g, khat, f1)


def _dft_last_kernel(b_ref, h_ref, u_ref, x0_ref, bd_ref, o_ref, *, c):
    half = h_ref.shape[1] // 2
    bd = bd_ref[...]
    for t in range(DFT_ROW_CHUNK):
        rhs = jnp.concatenate([b_ref[0, 0, t], b_ref[0, 1, t]], axis=0).astype(BF16)
        v = jnp.dot(h_ref[t], rhs, preferred_element_type=F32)
        o_ref[0, :, t, :] = (v[:half] + u_ref[0, :, t, :] * bd) * x0_ref[0, :, t, :]
        o_ref[1, :, t, :] = (-v[half:] + u_ref[1, :, t, :] * bd) * x0_ref[1, :, t, :]


def _dft_last(bm, hh, u_view, x0_view, bias_d, pairs, n2, c):
    half = n2 // 2
    tc = DFT_ROW_CHUNK
    io = pl.BlockSpec((2, half, tc, c), lambda p, j: (p, 0, j, 0))
    return pl.pallas_call(
        functools.partial(_dft_last_kernel, c=c),
        grid=(pairs, DFT_N1 // tc),
        in_specs=[pl.BlockSpec((1, 2, tc, n2, c), lambda p, j: (p, 0, j, 0, 0)),
                  pl.BlockSpec((tc, n2, 2 * n2), lambda p, j: (j, 0, 0)), io, io, _full((1, c))],
        out_specs=io,
        out_shape=jax.ShapeDtypeStruct((2 * pairs, half, DFT_N1, c), F32),
        compiler_params=_params(("arbitrary", "arbitrary")),
        name="dft_last",
    )(bm, hh, u_view, x0_view, bias_d)


def _long_conv_mixer(u, x0, kfilt, ksum, bias_d):
    b, n, c = u.shape
    n2 = 2 * n // DFT_N1
    half = n2 // 2
    pairs = b // 2
    m_data, m_filt, g, f1, hh = (jnp.asarray(t) for t in _dft_tables(n2))
    kb = min(16, n2)
    scale = 1.0 / (ksum * float(DFT_N1 * n2))
    ka = _dft_stage1(m_filt, [kfilt.reshape(1, n2, DFT_N1, c)], 1, n2, n2, c)
    khat = _filter_spectrum(ka.reshape(1, 2, n2, DFT_N1, c), g, scale, n2, kb)
    u_view = u.reshape(b, half, DFT_N1, c)
    a = _dft_stage1(m_data, [u_view, u_view], pairs, half, n2, c)
    bm = _dft_mid(a.reshape(pairs, 2, n2, DFT_N1, c), g, khat, f1, pairs, n2, kb)
    out = _dft_last(bm, hh, u_view, x0.reshape(b, half, DFT_N1, c),
                    bias_d.reshape(1, c), pairs, n2, c)
    return out.reshape(b, n, c)


@functools.lru_cache(maxsize=None)
def _small_dft_tables(n):
    big = 2 * n
    a = np.arange(big)
    fr, fi = _twiddle(np.outer(a, a), big)
    cast = lambda m: np.asarray(m, dtype=BF16)
    return (cast(np.concatenate([fr, fi], 0)),
            cast(_real_form(fr[:, :n], fi[:, :n])),
            cast(_real_form(fr[:n], fi[:n])))


def _small_conv_kernel(k_ref, ks_ref, u_ref, x0_ref, bd_ref, mf_ref, md_ref, mi_ref, o_ref, *, n):
    big = 2 * n
    kh = jnp.dot(mf_ref[...], k_ref[...].astype(BF16), preferred_element_type=F32) * (1.0 / (ks_ref[...] * big))
    kr, ki = kh[:big], kh[big:]
    rhs = jnp.concatenate([u_ref[0], u_ref[1]], axis=0).astype(BF16)
    x = jnp.dot(md_ref[...], rhs, preferred_element_type=F32)
    xr, xi = x[:big], x[big:]
    v = jnp.concatenate([xr * kr - xi * ki, -(xr * ki + xi * kr)], axis=0).astype(BF16)
    y = jnp.dot(mi_ref[...], v, preferred_element_type=F32)
    bd = bd_ref[...]
    o_ref[0] = ((y[:n] + u_ref[0] * bd) * x0_ref[0]).astype(o_ref.dtype)
    o_ref[1] = ((-y[n:] + u_ref[1] * bd) * x0_ref[1]).astype(o_ref.dtype)


def _small_conv_mixer(u, x0, kfilt, ksum, bias_d):
    b, n, c = u.shape
    mf, md, mi = (jnp.asarray(t) for t in _small_dft_tables(n))
    io = pl.BlockSpec((2, n, c), lambda p: (p, 0, 0))
    return pl.pallas_call(
        functools.partial(_small_conv_kernel, n=n),
        grid=(b // 2,),
        in_specs=[_full((2 * n, c)), _full((1, c)), io, io, _full((1, c)),
                  _full(mf.shape), _full(md.shape), _full(mi.shape)],
        out_specs=io,
        out_shape=jax.ShapeDtypeStruct((b, n, c), BF16),
        compiler_params=_params(("arbitrary",)),
        name="small_conv",
    )(kfilt, ksum, u, x0, bias_d.reshape(1, c), mf, md, mi)


MXU_DIM = 256
FFN_CHUNK_EDGES = (0, 5 * MXU_DIM, D_FF)


def _gelu_tanh(x):
    return 0.5 * x * (1.0 + jnp.tanh(math.sqrt(2.0 / math.pi) * (x + 0.044715 * (x * x * x))))


def _ffn_kernel(x_ref, xp_ref, xn_ref, oa_ref, oap_ref, oan_ref, oc_ref, ocp_ref, ocn_ref, wa_ref, wc_ref, g1_ref,
                sh_ref, sc_ref, gt_ref, g_ref, wu_ref, cw_ref, cb_ref, wd_ref, o_ref):
    i = pl.program_id(1)
    nt = pl.num_programs(1)
    tm = x_ref.shape[1]

    def halo(p_ref, n_ref):
        return jnp.concatenate([p_ref[0][-SUBLANES:], n_ref[0][:SUBLANES]], axis=0)

    def mixer_out(oa, oc):
        return (jnp.dot(oa.astype(BF16), wa_ref[...], preferred_element_type=F32)
                + jnp.dot(oc.astype(BF16), wc_ref[...], preferred_element_type=F32))

    g1 = g1_ref[0]
    xm = x_ref[0] + g1 * mixer_out(oa_ref[0], oc_ref[0])
    xh = halo(xp_ref, xn_ref) + g1 * mixer_out(halo(oap_ref, oan_ref), halo(ocp_ref, ocn_ref))
    ext = jnp.concatenate([xh[:SUBLANES], xm, xh[SUBLANES:]], axis=0)
    r = tm + 2 * SUBLANES
    h = _modulated_norm(ext, g_ref[...], sh_ref[0], sc_ref[0])
    row = lax.broadcasted_iota(jnp.int32, (r, 1), 0)
    inside = jnp.logical_and(jnp.logical_or(i > 0, row >= SUBLANES), jnp.logical_or(i < nt - 1, row < tm + SUBLANES))
    h = jnp.where(inside, h, 0.0).astype(BF16)
    hm = h[SUBLANES:tm + SUBLANES]
    acc = jnp.zeros((tm, D_MODEL), F32)
    for c0, c1 in zip(FFN_CHUNK_EDGES[:-1], FFN_CHUNK_EDGES[1:]):
        a = jnp.dot(h, wu_ref[:, c0:c1], preferred_element_type=F32)
        v = jnp.dot(hm, wu_ref[:, D_FF + c0:D_FF + c1], preferred_element_type=F32)
        ap = pltpu.roll(a, 1, 0)[SUBLANES:tm + SUBLANES]
        an = pltpu.roll(a, r - 1, 0)[SUBLANES:tm + SUBLANES]
        cw = cw_ref[:, c0:c1]
        conv = ap * cw[0:1] + a[SUBLANES:tm + SUBLANES] * cw[1:2] + an * cw[2:3] + cb_ref[:, c0:c1]
        act = (_gelu_tanh(conv) * v).astype(BF16)
        acc = acc + jnp.dot(act, wd_ref[c0:c1, :], preferred_element_type=F32)
    o_ref[0] = xm + gt_ref[0] * acc


def _mixer_out_ffn(x, oa, oc, wo_bf, gate1, sh, sc, gate2, gain, wu_bf, conv_w, conv_b, wd_bf):
    bx, t, d = x.shape
    tm = min(512, t)
    vec = pl.BlockSpec((1, 1, d), lambda b, i: (b, 0, 0))
    once = pl.Buffered(1)
    return pl.pallas_call(
        _ffn_kernel,
        grid=(bx, t // tm),
        in_specs=_halo_specs(tm, t, d) + _halo_specs(tm, t, Q_W, oa.dtype) + _halo_specs(tm, t, CONV_W, oc.dtype) + [
            pl.BlockSpec((Q_W, d), lambda b, i: (0, 0)), pl.BlockSpec((CONV_W, d), lambda b, i: (1, 0)),
            vec, vec, vec, vec, _full((1, d)),
            pl.BlockSpec((d, 2 * D_FF), lambda b, i: (0, 0), pipeline_mode=once),
            _full((3, D_FF)), _full((1, D_FF)),
            pl.BlockSpec((D_FF, d), lambda b, i: (0, 0), pipeline_mode=once)],
        out_specs=pl.BlockSpec((1, tm, d), lambda b, i: (b, i, 0)),
        out_shape=jax.ShapeDtypeStruct((bx, t, d), F32),
        compiler_params=_params(("arbitrary", "arbitrary")),
        name="mixer_out_ffn",
    )(x, x, x, oa, oa, oa, oc, oc, oc, wo_bf, wo_bf, gate1, sh, sc, gate2, gain, wu_bf, conv_w,
      conv_b.reshape(1, D_FF), wd_bf)


@functools.lru_cache(maxsize=None)
def _rope_tables(t):
    pos = np.arange(t)
    n_freq = HEAD_DIM // 4
    inv = ROPE_THETA ** (-np.arange(n_freq, dtype=np.float64) / n_freq)
    ang_r = (pos // GRID_W)[:, None] * inv
    ang_c = (pos % GRID_W)[:, None] * inv
    zero = np.zeros_like(ang_r)
    cos = np.concatenate([np.cos(ang_r)] * 2 + [np.cos(ang_c)] * 2, 1)
    sa = np.concatenate([-np.sin(ang_r), zero, -np.sin(ang_c), zero], 1)
    sb = np.concatenate([zero, np.sin(ang_r), zero, np.sin(ang_c)], 1)
    tile = lambda m: np.tile(m, (1, LANES // HEAD_DIM)).astype(np.float32)
    return tile(cos), tile(sa), tile(sb)


@functools.lru_cache(maxsize=None)
def _identity_rope_tables(t):
    return np.ones((t, LANES), np.float32), np.zeros((t, LANES), np.float32), np.zeros((t, LANES), np.float32)


@functools.lru_cache(maxsize=None)
def _head_block_diag():
    hid = np.arange(Q_W) // HEAD_DIM
    return np.asarray(hid[:, None] == hid[None, :], dtype=BF16)


def kernel(x, c, ctx, c_ctx, ada_w, ada_b, norm_mix, norm_ffn, mix_w_in, mix_w_out, attn_q_norm, attn_k_norm,
           swa_sink, hy_conv_w, hy_conv_b, hy_w1, hy_b1, hy_w2, hy_b2, hy_w3, hy_b3, hy_w4, hy_freq, hy_bias_d,
           sc_conv_w, ffn_w_up, ffn_conv_w, ffn_conv_b, ffn_w_down):
    b, s, d = x.shape
    s_ctx = ctx.shape[1]
    depth = ada_w.shape[0]
    assert d == D_MODEL and b % 2 == 0 and b + 1 <= MOD_ROWS and s % 1024 == 0 and s_ctx % SUBLANES == 0

    cvec = jnp.concatenate([c, c_ctx[None, :], jnp.zeros((MOD_ROWS - b - 1, d), F32)], axis=0)
    mods = _mods(cvec, ada_w, ada_b)
    rope = [jnp.asarray(t) for t in _rope_tables(s)]
    rope_ctx = [jnp.asarray(t) for t in _identity_rope_tables(s_ctx)]
    bd = jnp.asarray(_head_block_diag())
    xc = ctx
    weights = _weights_bf16([mix_w_in, mix_w_out, ffn_w_up, ffn_w_down])

    for i in range(depth):
        last = i == depth - 1
        j = i // 2
        lat = [mods[i, :b, k * d:(k + 1) * d][:, None, :] for k in range(6)]
        cx = [jnp.broadcast_to(mods[i, b, k * d:(k + 1) * d][None, None, :], (b, 1, d)) for k in range(6)]
        w_in, w_out, w_up, w_down = (w[i] for w in weights)
        g_mix = norm_mix[i].reshape(1, d)
        g_ffn = norm_ffn[i].reshape(1, d)
        qg = jnp.tile(attn_q_norm[i], N_Q_HEADS).reshape(1, Q_W)
        kg = jnp.tile(attn_k_norm[i], N_KV_HEADS).reshape(1, KV_W)

        conv = (hy_conv_w[j], hy_conv_b[j]) if i % 2 == 0 else (sc_conv_w[j],)
        q, k, vt, *mix = _inproj(x, lat[0], lat[1], g_mix, w_in, qg, kg, bd, *rope, *conv)
        qc, kc, vct, *mixc = _inproj(xc, cx[0], cx[1], g_mix, w_in, qg, kg, bd, *rope_ctx, *conv)

        bound = _score_bound(attn_q_norm[i], attn_k_norm[i])
        if i % 2 == 0:
            fargs = (hy_w1[j], hy_b1[j], hy_w2[j], hy_b2[j], hy_w3[j], hy_b3[j], hy_w4[j], hy_freq[j])
            o_attn = _dense_attention(q, k, vt, kc, vct, bound=bound)
            kf, ks = _implicit_filter(s, *fargs)
            o_conv = _long_conv_mixer(*mix, kf, ks, hy_bias_d[j])
            if not last:
                oc_attn = _dense_attention(qc, kc, vct, bound=bound)
                kfc, ksc = _implicit_filter(s_ctx, *fargs)
                oc_conv = _small_conv_mixer(*mixc, kfc, ksc, hy_bias_d[j])
        else:
            sink = swa_sink[j].reshape(N_KV_HEADS, GROUP)
            o_attn = _banded_attention(q, k, vt, kc, vct, sink, bound)
            o_conv = mix[0]
            if not last:
                oc_attn = _dense_attention(qc, kc, vct, sink=sink)
                oc_conv = mixc[0]

        ffn = (g_ffn, w_up, ffn_conv_w[i], ffn_conv_b[i], w_down)
        x = _mixer_out_ffn(x, o_attn, o_conv, w_out, lat[2], lat[3], lat[4], lat[5], *ffn)
        if not last:
            xc = _mixer_out_ffn(xc, oc_attn, oc_conv, w_out, cx[2], cx[3], cx[4], cx[5], *ffn)
    return x
```

```python
import functools
import math

import numpy as np
import jax
import jax.numpy as jnp
from jax import lax
from jax.experimental import pallas as pl
from jax.experimental.pallas import tpu as pltpu

F32 = jnp.float32
BF16 = jnp.bfloat16
HI = lax.Precision.HIGHEST

D_MODEL = 1024
GRID_W = 64
HEAD_DIM = 64
N_Q_HEADS = 8
N_KV_HEADS = 4
GROUP = N_Q_HEADS // N_KV_HEADS
Q_W = N_Q_HEADS * HEAD_DIM
KV_W = N_KV_HEADS * HEAD_DIM
QKV_W = Q_W + 2 * KV_W
CONV_W = D_MODEL // 2
MIX_IN_W = QKV_W + 3 * CONV_W
WINDOW = 128
ROPE_THETA = 10000.0
FILT_EMB = 33
FILT_WIDTH = 64
DECAY_TARGET = 1e-2
FAST_DECAY_PCT = 0.3
SLOW_DECAY_PCT = 1.5
DECAY_SHIFT = 0.05
D_FF = 2816
NEG_INF = -1e30
RMS_EPS = 1e-6
LOG2E = 1.4426950408889634
Q_SCALE = HEAD_DIM ** -0.5 * LOG2E

LANES = 128
SUBLANES = 8
DFT_N1 = 128
VMEM_LIMIT_MB = 56
VT_ROWS = LANES
MOD_ROWS = SUBLANES
PROJ_SUB_ROWS = 128


def _params(sem, vmem_mb=VMEM_LIMIT_MB):
    return pltpu.CompilerParams(dimension_semantics=sem, vmem_limit_bytes=vmem_mb * 1024 * 1024)


def _full(shape):
    nd = len(shape)
    return pl.BlockSpec(shape, lambda *_: (0,) * nd)


def _mods_kernel(c_ref, w_ref, b_ref, o_ref):
    c = c_ref[...]
    s = c / (1.0 + jnp.exp(-c))
    o_ref[0] = _dot_bf16x3(s, w_ref[0]) + b_ref[0]


def _mods(cvec, ada_w, ada_b):
    depth, d, n6 = ada_w.shape
    tn = 1536
    return pl.pallas_call(
        _mods_kernel,
        grid=(depth, n6 // tn),
        in_specs=[_full((MOD_ROWS, d)),
                  pl.BlockSpec((1, d, tn), lambda l, j: (l, 0, j)),
                  pl.BlockSpec((1, 1, tn), lambda l, j: (l, 0, j))],
        out_specs=pl.BlockSpec((1, MOD_ROWS, tn), lambda l, j: (l, 0, j)),
        out_shape=jax.ShapeDtypeStruct((depth, MOD_ROWS, n6), F32),
        compiler_params=_params(("arbitrary", "arbitrary")),
        name="ada_mods",
    )(cvec, ada_w, ada_b.reshape(depth, 1, n6))


def _modulated_norm(x, gain, shift, scale):
    ms = jnp.mean(x * x, axis=-1, keepdims=True)
    return (x * lax.rsqrt(ms + RMS_EPS)) * gain * (1.0 + scale) + shift


WEIGHT_CAST_STEPS = 8


def _cast_kernel(*refs):
    n = len(refs) // 2
    for w_ref, o_ref in zip(refs[:n], refs[n:]):
        o_ref[...] = w_ref[0].astype(o_ref.dtype)


def _weights_bf16(stacked):
    depth = stacked[0].shape[0]
    in_specs, out_specs, out_shape, args = [], [], [], []
    for w in stacked:
        _, rows, cols = w.shape
        tr = rows // WEIGHT_CAST_STEPS
        for layer in range(depth):
            in_specs.append(pl.BlockSpec((1, tr, cols), lambda r, layer=layer: (layer, r, 0)))
            out_specs.append(pl.BlockSpec((tr, cols), lambda r: (r, 0)))
            out_shape.append(jax.ShapeDtypeStruct((rows, cols), BF16))
            args.append(w)
    outs = pl.pallas_call(
        _cast_kernel,
        grid=(WEIGHT_CAST_STEPS,),
        in_specs=in_specs,
        out_specs=out_specs,
        out_shape=out_shape,
        compiler_params=_params(("arbitrary",)),
        name="weights_bf16",
    )(*args)
    return [outs[k * depth:(k + 1) * depth] for k in range(len(stacked))]


def _halo_specs(tm, t, width, rows=SUBLANES):
    nb = tm // rows
    last = t // rows - 1
    return [pl.BlockSpec((1, tm, width), lambda b, i: (b, i, 0)),
            pl.BlockSpec((1, rows, width), lambda b, i: (b, jnp.maximum(i * nb - 1, 0), 0)),
            pl.BlockSpec((1, rows, width), lambda b, i: (b, jnp.minimum((i + 1) * nb, last), 0))]


def _inproj_kernel(*refs, mode):
    (x_ref, xp_ref, xn_ref, sh_ref, sc_ref, g_ref, w_ref, qg_ref, kg_ref, bd_ref,
     cos_ref, sa_ref, sb_ref, cw_ref) = refs[:14]
    if mode == "hyena":
        cb_ref, q_ref, k_ref, vt_ref, u_ref, x0_ref = refs[14:]
    else:
        q_ref, k_ref, vt_ref, oc_ref = refs[14:]
    i = pl.program_id(1)
    nt = pl.num_programs(1)
    tm = x_ref.shape[1]
    sub = min(PROJ_SUB_ROWS, tm)
    gain, shift, scale = g_ref[...], sh_ref[0], sc_ref[0]

    cw = cw_ref[...]

    def conv_stage(ext, rows):
        n = sub + 2 * SUBLANES
        mid = slice(SUBLANES, sub + SUBLANES)
        if mode == "hyena":
            c = (pltpu.roll(ext, 1, 0)[mid] * cw[0:1] + ext[mid] * cw[1:2] + pltpu.roll(ext, n - 1, 0)[mid] * cw[2:3]
                 + cb_ref[...])
            x0_ref[0, rows] = c[:, :CONV_W]
            u_ref[0, rows] = c[:, 2 * CONV_W:] * c[:, CONV_W:2 * CONV_W]
        else:
            pr = ext[:, CONV_W:2 * CONV_W] * ext[:, 2 * CONV_W:]
            conv = pltpu.roll(pr, 1, 0)[mid] * cw[0:1] + pr[mid] * cw[1:2] + pltpu.roll(pr, n - 1, 0)[mid] * cw[2:3]
            oc_ref[0, rows] = (ext[mid, :CONV_W] * conv).astype(oc_ref.dtype)

    def head_norm_rope(t, hgain, bd, cos, sa, sb):
        sq = (t * t).astype(BF16)
        ssq = jnp.concatenate([jnp.dot(sq[:, c:c + MXU_DIM], bd[:MXU_DIM, :MXU_DIM], preferred_element_type=F32)
                               for c in range(0, t.shape[1], MXU_DIM)], axis=1)
        t = t * lax.rsqrt(ssq * (1.0 / HEAD_DIM) + RMS_EPS) * hgain
        outs = []
        for j in range(t.shape[1] // LANES):
            tj = t[:, j * LANES:(j + 1) * LANES]
            outs.append(tj * cos + pltpu.roll(tj, LANES - 16, 1) * sa + pltpu.roll(tj, 16, 1) * sb)
        return outs

    held = None
    for r0 in range(0, tm, sub):
        rows = slice(r0, r0 + sub)
        h = _modulated_norm(x_ref[0, rows], gain, shift, scale)
        z = jnp.dot(h.astype(BF16), w_ref[:, :QKV_W], preferred_element_type=F32)
        first, last = r0 == 0, r0 + sub == tm
        hc = jnp.concatenate(([_modulated_norm(xp_ref[0], gain, shift, scale)] if first else []) + [h]
                             + ([_modulated_norm(xn_ref[0], gain, shift, scale)] if last else []), axis=0)
        zc = jnp.dot(hc.astype(BF16), w_ref[:, QKV_W:], preferred_element_type=F32)
        if first:
            before = jnp.where(i > 0, zc[:SUBLANES], 0.0)
            zc = zc[SUBLANES:]
        if last:
            after = jnp.where(i < nt - 1, zc[sub:], 0.0)
            zc = zc[:sub]
        if held is not None:
            conv_stage(jnp.concatenate([before, held[0], zc[:SUBLANES]], axis=0), held[1])
            before = held[0][-SUBLANES:]
        held = (zc, rows)
        tabs = (cos_ref[rows], sa_ref[rows], sb_ref[rows])
        qs = head_norm_rope(z[:, :Q_W], qg_ref[...], bd_ref[...], *tabs)
        ks = head_norm_rope(z[:, Q_W:Q_W + KV_W], kg_ref[...], bd_ref[:KV_W, :KV_W], *tabs)
        for j, qj in enumerate(qs):
            qj = (qj * Q_SCALE).astype(BF16)
            q_ref[0, 2 * j, rows] = qj[:, :HEAD_DIM]
            q_ref[0, 2 * j + 1, rows] = qj[:, HEAD_DIM:]
        for j, kj in enumerate(ks):
            kj = kj.astype(BF16)
            k_ref[0, 2 * j, rows] = kj[:, :HEAD_DIM]
            k_ref[0, 2 * j + 1, rows] = kj[:, HEAD_DIM:]
        vt = z[:, Q_W + KV_W:QKV_W].T
        ones_row = (lax.broadcasted_iota(jnp.int32, (VT_ROWS - HEAD_DIM, sub), 0) == 0).astype(F32)
        for hh in range(N_KV_HEADS):
            vt_ref[0, hh, :, rows] = jnp.concatenate([vt[hh * HEAD_DIM:(hh + 1) * HEAD_DIM], ones_row],
                                                     axis=0).astype(BF16)

    conv_stage(jnp.concatenate([before, held[0], after], axis=0), held[1])


def _inproj(x, sh, sc, gain, w_bf, qg, kg, bd, cos, sa, sb, conv_w, conv_b=None):
    bx, t, d = x.shape
    tm = min(1024, t)
    mode = "hyena" if conv_b is not None else "short"
    vec = pl.BlockSpec((1, 1, d), lambda b, i: (b, 0, 0))
    tab = pl.BlockSpec((tm, LANES), lambda b, i: (i, 0))
    row = lambda w: pl.BlockSpec((1, tm, w), lambda b, i: (b, i, 0))
    in_specs = _halo_specs(tm, t, d) + [vec, vec, _full((1, d)), _full((d, MIX_IN_W)), _full((1, Q_W)),
                                        _full((1, KV_W)), _full((Q_W, Q_W)), tab, tab, tab, _full(conv_w.shape)]
    args = [x, x, x, sh, sc, gain, w_bf, qg, kg, bd, cos, sa, sb, conv_w]
    out_specs = [pl.BlockSpec((1, N_Q_HEADS, tm, HEAD_DIM), lambda b, i: (b, 0, i, 0)),
                 pl.BlockSpec((1, N_KV_HEADS, tm, HEAD_DIM), lambda b, i: (b, 0, i, 0)),
                 pl.BlockSpec((1, N_KV_HEADS, VT_ROWS, tm), lambda b, i: (b, 0, 0, i))]
    out_shape = [jax.ShapeDtypeStruct((bx, N_Q_HEADS, t, HEAD_DIM), BF16),
                 jax.ShapeDtypeStruct((bx, N_KV_HEADS, t, HEAD_DIM), BF16),
                 jax.ShapeDtypeStruct((bx, N_KV_HEADS, VT_ROWS, t), BF16)]
    if mode == "hyena":
        in_specs.append(_full((1, 3 * CONV_W)))
        args.append(conv_b.reshape(1, 3 * CONV_W))
        out_specs += [row(CONV_W), row(CONV_W)]
        out_shape += [jax.ShapeDtypeStruct((bx, t, CONV_W), F32)] * 2
    else:
        out_specs.append(row(CONV_W))
        out_shape.append(jax.ShapeDtypeStruct((bx, t, CONV_W), BF16))
    return pl.pallas_call(
        functools.partial(_inproj_kernel, mode=mode),
        grid=(bx, t // tm),
        in_specs=in_specs,
        out_specs=out_specs,
        out_shape=out_shape,
        compiler_params=_params(("arbitrary", "arbitrary")),
        name="in_proj",
    )(*args)


def _score_bound(q_gain, k_gain):
    b = HEAD_DIM * Q_SCALE * BOUND_MARGIN * jnp.max(jnp.abs(q_gain)) * jnp.max(jnp.abs(k_gain))
    return b.reshape(1, 1).astype(F32)


def _scores_t(ks, q):
    return lax.dot_general(ks, q, (((1,), (1,)), ((), ())), preferred_element_type=F32)


def _sink_column(sink_ref, hh, tq, m, acc):
    col = lax.broadcasted_iota(jnp.int32, m.shape, 1)
    sink = jnp.where(col < tq, sink_ref[hh, 0], sink_ref[hh, 1]) * LOG2E
    m_new = jnp.maximum(m, sink)
    den_row = lax.broadcasted_iota(jnp.int32, acc.shape, 0) == HEAD_DIM
    return acc * jnp.exp2(m - m_new) + jnp.where(den_row, jnp.exp2(sink - m_new), 0.0)


def _attn_finish(acc, tq):
    o = (acc / acc[HEAD_DIM:HEAD_DIM + 1]).T
    return jnp.concatenate([o[:tq, :HEAD_DIM], o[tq:, :HEAD_DIM]], axis=1)


EXP_CHUNK_ELEMS = 32 * SUBLANES * LANES


def _score_stage(ks, q, s_ref, bias=None):
    s = _scores_t(ks, q)
    if bias is not None:
        s = s + bias
    s_ref[0:s.shape[0]] = s
    return jnp.max(s, axis=0, keepdims=True)


def _softmax_stage(s_ref, p_ref, n, vt, mx, m, acc_ref):
    m_new = jnp.maximum(m, mx)
    rows = EXP_CHUNK_ELEMS // m.shape[1]
    for c in range(0, n, rows):
        p_ref[c:c + rows] = jnp.exp2(s_ref[c:c + rows] - m_new).astype(BF16)
    acc_ref[...] = acc_ref[...] * jnp.exp2(m - m_new) + jnp.dot(vt, p_ref[0:n], preferred_element_type=F32)
    return m_new


BOUNDED_TILES = 8
BOUND_MARGIN = 1.01
SAFE_SHIFT = 40.0


def _flash_exact(q, k_tile, vt_tile, ctx, sa_ref, sb_ref, p_ref, acc_ref, *, tk, n_main):
    pa_ref, pb_ref = p_ref.at[0:tk], p_ref.at[tk:2 * tk]
    acc_ref[...] = jnp.zeros_like(acc_ref)
    m = jnp.full((1, q.shape[0]), NEG_INF, F32)
    mx_a = _score_stage(k_tile(0), q, sa_ref)
    if n_main > 1:
        def body(i, carry):
            m, mx_a = carry
            mx_b = _score_stage(k_tile(2 * i + 1), q, sb_ref)
            m = _softmax_stage(sa_ref, pa_ref, tk, vt_tile(2 * i), mx_a, m, acc_ref)
            mx_a = _score_stage(k_tile(2 * i + 2), q, sa_ref)
            m = _softmax_stage(sb_ref, pb_ref, tk, vt_tile(2 * i + 1), mx_b, m, acc_ref)
            return m, mx_a

        m, mx_a = lax.fori_loop(0, n_main // 2 - 1, body, (m, mx_a))
        mx_b = _score_stage(k_tile(n_main - 1), q, sb_ref)
        m = _softmax_stage(sa_ref, pa_ref, tk, vt_tile(n_main - 2), mx_a, m, acc_ref)
        last = (sb_ref, pb_ref, mx_b)
        spare = (sa_ref, pa_ref)
    else:
        last = (sa_ref, pa_ref, mx_a)
        spare = (sb_ref, pb_ref)
    if ctx is not None:
        mx_c = _score_stage(ctx[0], q, spare[0])
    m = _softmax_stage(last[0], last[1], tk, vt_tile(n_main - 1), last[2], m, acc_ref)
    if ctx is not None:
        m = _softmax_stage(spare[0], spare[1], ctx[0].shape[0], ctx[1], mx_c, m, acc_ref)
    return m


def _exp2_as_bf16(s):
    return jnp.exp2(s).astype(BF16)


def _flash_bounded(q, k_ref, vt_ref, ctx, p_ref, acc_ref, *, tk, n_main):
    step = BOUNDED_TILES * tk if n_main % BOUNDED_TILES == 0 else tk

    def group(j):
        off = pl.multiple_of(j * step, step)
        for c in range(0, step, tk):
            p_ref[c:c + tk] = _exp2_as_bf16(_scores_t(k_ref[0, 0, pl.ds(off + c, tk), :], q))
        return jnp.dot(vt_ref[0, 0, :, pl.ds(off, step)], p_ref[0:step], preferred_element_type=F32)

    first = group(0)
    if ctx is not None:
        pc_ref = p_ref.at[BOUNDED_TILES * tk:BOUNDED_TILES * tk + ctx[0].shape[0]]
        pc_ref[...] = _exp2_as_bf16(_scores_t(ctx[0], q))
        first = first + jnp.dot(ctx[1], pc_ref[...], preferred_element_type=F32)
    acc_ref[...] = first

    def body(j, carry):
        acc_ref[...] += group(j)
        return carry

    lax.fori_loop(1, n_main * tk // step, body, 0)


def _flash_kernel(*refs, tq, tk, n_main, has_ctx, has_sink):
    sa_ref, sb_ref, p_ref, acc_ref = refs[-4:]
    o_ref = refs[-5]
    refs = list(refs[:-5])
    sink_ref = refs.pop(0) if has_sink else None
    bound_ref = None if has_sink else refs.pop(0)
    q_ref, k_ref, vt_ref = refs[:3]
    m_cols = GROUP * tq
    q = q_ref[0].reshape(m_cols, HEAD_DIM)
    ctx = (refs[3][0, 0], refs[4][0, 0]) if has_ctx else None

    def k_tile(j):
        return k_ref[0, 0, pl.ds(pl.multiple_of(j * tk, tk), tk), :]

    def vt_tile(j):
        return vt_ref[0, 0, :, pl.ds(pl.multiple_of(j * tk, tk), tk)]

    exact = functools.partial(_flash_exact, q, k_tile, vt_tile, ctx, sa_ref, sb_ref, p_ref, acc_ref,
                              tk=tk, n_main=n_main)
    if has_sink:
        acc = _sink_column(sink_ref, pl.program_id(1), tq, exact(), acc_ref[...])
    else:
        bound = bound_ref[0, 0]
        safe = bound <= SAFE_SHIFT

        @pl.when(safe)
        def _():
            _flash_bounded(q, k_ref, vt_ref, ctx, p_ref, acc_ref, tk=tk, n_main=n_main)

        @pl.when(jnp.logical_not(safe))
        def _():
            exact()

        acc = acc_ref[...]
    o_ref[0] = _attn_finish(acc, tq).astype(o_ref.dtype)


def _kv_specs(n):
    return [pl.BlockSpec((1, 1, n, HEAD_DIM), lambda b, h, i: (b, h, 0, 0)),
            pl.BlockSpec((1, 1, VT_ROWS, n), lambda b, h, i: (b, h, 0, 0))]


def _dense_attention(q, k, vt, kc=None, vct=None, *, bound=None, sink=None):
    bx, _, t, _ = q.shape
    s = k.shape[2]
    tq = min(1024, t)
    tk = min(512, s)
    has_ctx = kc is not None
    has_sink = sink is not None
    in_specs = [pl.BlockSpec((1, GROUP, tq, HEAD_DIM), lambda b, h, i: (b, h, i, 0))] + _kv_specs(s)
    args = [q, k, vt]
    if has_ctx:
        in_specs += _kv_specs(kc.shape[2])
        args += [kc, vct]
    in_specs = [pl.BlockSpec(memory_space=pltpu.SMEM)] + in_specs
    args = [sink if has_sink else bound] + args
    n_main = s // tk
    assert n_main == 1 or n_main % 2 == 0
    n_ctx = kc.shape[2] if has_ctx else 0
    assert n_ctx <= tk
    m_cols = GROUP * tq
    return pl.pallas_call(
        functools.partial(_flash_kernel, tq=tq, tk=tk, n_main=n_main, has_ctx=has_ctx, has_sink=has_sink),
        grid=(bx, N_KV_HEADS, t // tq),
        in_specs=in_specs,
        out_specs=pl.BlockSpec((1, tq, LANES), lambda b, h, i: (b, i, h)),
        out_shape=jax.ShapeDtypeStruct((bx, t, Q_W), BF16),
        scratch_shapes=[pltpu.VMEM((tk, m_cols), F32), pltpu.VMEM((tk, m_cols), F32),
                        pltpu.VMEM((BOUNDED_TILES * tk + n_ctx, m_cols), BF16), pltpu.VMEM((VT_ROWS, m_cols), F32)],
        compiler_params=_params(("arbitrary", "arbitrary", "arbitrary")),
        name="dense_attention",
    )(*args)


BAND_SUB = 256


@functools.lru_cache(maxsize=None)
def _band_bias():
    span = BAND_SUB + 2 * WINDOW
    kr = np.arange(span)[None, :, None]
    qc = (np.arange(GROUP * BAND_SUB) % BAND_SUB)[None, None, :]
    rel = np.arange(3)[:, None, None]
    return np.where(np.abs(kr - qc - rel * WINDOW) <= WINDOW, 0.0, NEG_INF).astype(np.float32)


def _banded_kernel(sink_ref, bound_ref, q_ref, k_ref, vt_ref, kc_ref, vct_ref, bias_ref, o_ref,
                   sl_ref, sc_ref, pl_ref, pc_ref, acc_ref, *, n_sub, s_len):
    i = pl.program_id(2)
    hh = pl.program_id(1)
    sub = BAND_SUB
    m_cols = GROUP * sub
    span = sub + 2 * WINDOW
    n_ctx = kc_ref.shape[2]

    def window(u):
        q = q_ref[0, :, u * sub:(u + 1) * sub, :].reshape(m_cols, HEAD_DIM)
        q0 = (i * n_sub + u) * sub
        start = pl.multiple_of(jnp.clip(q0 - WINDOW, 0, s_len - span), WINDOW)
        return q, start, bias_ref[(q0 - start) // WINDOW]

    shift = jnp.maximum(bound_ref[0, 0], jnp.maximum(sink_ref[hh, 0], sink_ref[hh, 1]) * LOG2E)
    safe = shift <= SAFE_SHIFT

    @pl.when(safe)
    def _():
        col = lax.broadcasted_iota(jnp.int32, (1, m_cols), 1)
        sink = jnp.where(col < sub, sink_ref[hh, 0], sink_ref[hh, 1]) * LOG2E
        den_row = lax.broadcasted_iota(jnp.int32, (VT_ROWS, m_cols), 0) == HEAD_DIM
        sink_den = jnp.where(den_row, jnp.exp2(sink), 0.0)
        for u in range(n_sub):
            q, start, bias = window(u)
            pl_u, pc_u = pl_ref.at[u], pc_ref.at[u]
            pl_u[...] = _exp2_as_bf16(_scores_t(k_ref[0, 0, pl.ds(start, span), :], q) + bias)
            pc_u[...] = _exp2_as_bf16(_scores_t(kc_ref[0, 0], q))
            acc = (jnp.dot(vt_ref[0, 0, :, pl.ds(start, span)], pl_u[...], preferred_element_type=F32)
                   + jnp.dot(vct_ref[0, 0], pc_u[...], preferred_element_type=F32) + sink_den)
            o_ref[0, u * sub:(u + 1) * sub] = _attn_finish(acc, sub).astype(o_ref.dtype)

    @pl.when(jnp.logical_not(safe))
    def _():
        stats = []
        for u in range(n_sub):
            q, start, bias = window(u)
            mx_l = _score_stage(k_ref[0, 0, pl.ds(start, span), :], q, sl_ref.at[u], bias=bias)
            mx_c = _score_stage(kc_ref[0, 0], q, sc_ref.at[u])
            stats.append((start, mx_l, mx_c))
        for u in range(n_sub):
            start, mx_l, mx_c = stats[u]
            acc_u = acc_ref.at[u]
            acc_u[...] = jnp.zeros((VT_ROWS, m_cols), F32)
            m = jnp.full((1, m_cols), NEG_INF, F32)
            m = _softmax_stage(sl_ref.at[u], pl_ref.at[u], span, vt_ref[0, 0, :, pl.ds(start, span)], mx_l, m, acc_u)
            m = _softmax_stage(sc_ref.at[u], pc_ref.at[u], n_ctx, vct_ref[0, 0], mx_c, m, acc_u)
            acc = _sink_column(sink_ref, hh, sub, m, acc_u[...])
            o_ref[0, u * sub:(u + 1) * sub] = _attn_finish(acc, sub).astype(o_ref.dtype)


def _banded_attention(q, k, vt, kc, vct, sink, bound):
    bx, _, t, _ = q.shape
    n_ctx = kc.shape[2]
    n_sub = next(k for k in (8, 4, 2, 1) if t % (k * BAND_SUB) == 0)
    tq = n_sub * BAND_SUB
    m_cols = GROUP * BAND_SUB
    span = BAND_SUB + 2 * WINDOW
    bias = jnp.asarray(_band_bias())
    return pl.pallas_call(
        functools.partial(_banded_kernel, n_sub=n_sub, s_len=t),
        grid=(bx, N_KV_HEADS, t // tq),
        in_specs=[pl.BlockSpec(memory_space=pltpu.SMEM), pl.BlockSpec(memory_space=pltpu.SMEM),
                  pl.BlockSpec((1, GROUP, tq, HEAD_DIM), lambda b, h, i: (b, h, i, 0))]
                 + _kv_specs(t) + _kv_specs(n_ctx) + [_full(bias.shape)],
        out_specs=pl.BlockSpec((1, tq, LANES), lambda b, h, i: (b, i, h)),
        out_shape=jax.ShapeDtypeStruct((bx, t, Q_W), BF16),
        scratch_shapes=[pltpu.VMEM((n_sub, span, m_cols), F32), pltpu.VMEM((n_sub, n_ctx, m_cols), F32),
                        pltpu.VMEM((n_sub, span, m_cols), BF16), pltpu.VMEM((n_sub, n_ctx, m_cols), BF16),
                        pltpu.VMEM((n_sub, VT_ROWS, m_cols), F32)],
        compiler_params=_params(("arbitrary", "arbitrary", "arbitrary")),
        name="banded_attention",
    )(sink, bound, q, k, vt, kc, vct, bias)


@functools.lru_cache(maxsize=None)
def _filter_features(n):
    j = np.arange(2 * n)
    d = np.where(j <= n, j, 2 * n - j)
    d = np.where(j == n, 0, d)
    bands = (FILT_EMB - 1) // 2
    t01 = np.linspace(0.0, 1.0, n)[d]
    w = 2.0 * np.pi * d.astype(np.float64) / n
    f = np.linspace(1e-4, bands - 1, bands)[None, :]
    feats = np.zeros((2 * n, LANES), np.float64)
    feats[:, 0] = t01
    feats[:, 1:1 + bands] = np.cos(f * w[:, None])
    feats[:, 1 + bands:FILT_EMB] = -np.sin(f * w[:, None])
    feats[:, 64] = t01
    feats[:, 65] = (j < n)
    feats[:, 66] = (j != n)
    return feats.astype(np.float32)


def _dot_bf16x3(a, b):
    a_hi = a.astype(BF16)
    a_lo = (a - a_hi.astype(F32)).astype(BF16)
    b_hi = b.astype(BF16)
    b_lo = (b - b_hi.astype(F32)).astype(BF16)
    dot = lambda x, y: jnp.dot(x, y, preferred_element_type=F32)
    return dot(a_hi, b_hi) + dot(a_hi, b_lo) + dot(a_lo, b_hi)


def _filter_kernel(f_ref, w1_ref, b1_ref, w2_ref, b2_ref, w3_ref, b3_ref, w4_ref, fr_ref, dl_ref, k_ref, s_ref):
    f = f_ref[...]
    fr = fr_ref[...]
    mm = lambda a, b: jnp.dot(a, b, preferred_element_type=F32, precision=HI)
    h = jnp.sin(fr * (mm(w1_ref[...], f) + b1_ref[...]))
    h = jnp.sin(fr * (mm(w2_ref[...], h) + b2_ref[...]))
    h = jnp.sin(fr * (mm(w3_ref[...], h) + b3_ref[...]))
    hf = _dot_bf16x3(w4_ref[...], h)
    win = jnp.exp(-dl_ref[...] * f[64:65]) + DECAY_SHIFT
    k = (jnp.where(f[65:66] > 0.5, hf[:CONV_W], hf[CONV_W:]) * win * f[66:67]).T
    k_ref[...] = k

    @pl.when(pl.program_id(0) == 0)
    def _():
        s_ref[...] = jnp.zeros_like(s_ref)

    s_ref[...] += jnp.sum(jnp.abs(k), axis=0, keepdims=True)


def _implicit_filter(n, w1, b1, w2, b2, w3, b3, w4, freq):
    feats_t = jnp.asarray(np.ascontiguousarray(_filter_features(n).T))
    col = lambda a: a.reshape(-1, 1)
    w1t = jnp.pad(w1.T, ((0, 0), (0, LANES - FILT_EMB)))
    deltas = np.abs(np.linspace(math.log(DECAY_TARGET) / SLOW_DECAY_PCT, math.log(DECAY_TARGET) / FAST_DECAY_PCT,
                                CONV_W)).astype(np.float32).reshape(CONV_W, 1)
    tr = min(1024, 2 * n)
    sq = _full((FILT_WIDTH, FILT_WIDTH))
    vec = _full((FILT_WIDTH, 1))
    return pl.pallas_call(
        _filter_kernel,
        grid=(2 * n // tr,),
        in_specs=[pl.BlockSpec((LANES, tr), lambda i: (0, i)), _full((FILT_WIDTH, LANES)), vec, sq, vec, sq, vec,
                  _full((2 * CONV_W, FILT_WIDTH)), vec, _full((CONV_W, 1))],
        out_specs=[pl.BlockSpec((tr, CONV_W), lambda i: (i, 0)), _full((1, CONV_W))],
        out_shape=[jax.ShapeDtypeStruct((2 * n, CONV_W), F32), jax.ShapeDtypeStruct((1, CONV_W), F32)],
        compiler_params=_params(("arbitrary",)),
        name="hyena_filter",
    )(feats_t, w1t, col(b1), w2.T, col(b2), w3.T, col(b3), w4.T, col(freq), jnp.asarray(deltas))


def _twiddle(idx, mod):
    ang = 2.0 * np.pi * (idx % mod) / mod
    return np.cos(ang), -np.sin(ang)


def _real_form(mr, mi):
    return np.concatenate([np.concatenate([mr, -mi], -1), np.concatenate([mi, mr], -1)], -2)


@functools.lru_cache(maxsize=None)
def _dft_tables(n2):
    n1 = DFT_N1
    n = n1 * n2
    h = n2 // 2
    a2 = np.arange(n2)
    fr, fi = _twiddle(np.outer(a2, a2), n2)
    m_data = _real_form(fr[:, :h], fi[:, :h])
    m_filt = np.concatenate([fr, fi], 0)
    k2 = a2[:, None, None]
    k1 = np.arange(n1)[None, :, None]
    c1 = np.arange(n1)[None, None, :]
    g = _real_form(*_twiddle(c1 * (n2 * k1 + k2), n))
    a1 = np.arange(n1)
    f1 = _real_form(*_twiddle(np.outer(a1, a1), n1))
    t2 = a1[:, None, None]
    t1 = np.arange(h)[None, :, None]
    j1 = a2[None, None, :]
    hh = _real_form(*_twiddle(j1 * (n1 * t1 + t2), n))
    cast = lambda m: np.asarray(m, dtype=BF16)
    return cast(m_data), cast(m_filt), cast(g), cast(f1), cast(hh)


DFT_ROW_CHUNK = 2 * SUBLANES


def _dft_rows_kernel(m_ref, *refs):
    o_ref = refs[-1]
    cols = [jnp.concatenate([r[0, :, t, :] for r in refs[:-1]], axis=0) for t in range(DFT_ROW_CHUNK)]
    rhs = jnp.concatenate(cols, axis=1).astype(BF16)
    out = jnp.dot(m_ref[...], rhs, preferred_element_type=F32)
    half = out.shape[0] // 2
    o_ref[0, 0] = out[:half].astype(o_ref.dtype)
    o_ref[0, 1] = out[half:].astype(o_ref.dtype)


def _dft_stage1(mat, views, pairs, rows, n2, c):
    nin = len(views)
    lc = DFT_ROW_CHUNK * c
    in_specs = [_full(mat.shape)] + [
        pl.BlockSpec((1, rows, DFT_ROW_CHUNK, c), (lambda p, j, a=a: (nin * p + a, 0, j, 0))) for a in range(nin)]
    return pl.pallas_call(
        _dft_rows_kernel,
        grid=(pairs, DFT_N1 // DFT_ROW_CHUNK),
        in_specs=in_specs,
        out_specs=pl.BlockSpec((1, 2, n2, lc), lambda p, j: (p, 0, 0, j)),
        out_shape=jax.ShapeDtypeStruct((pairs, 2, n2, DFT_N1 * c), BF16),
        compiler_params=_params(("arbitrary", "arbitrary")),
        name="dft_stage1",
    )(mat, *views)


def _spectrum_kernel(a_ref, g_ref, sc_ref, o_ref, *, kb):
    for j in range(kb):
        rhs = jnp.concatenate([a_ref[0, 0, j], a_ref[0, 1, j]], axis=0)
        x = jnp.dot(g_ref[j], rhs, preferred_element_type=F32)
        o_ref[0, j] = (x[:DFT_N1] * sc_ref[...]).astype(o_ref.dtype)
        o_ref[1, j] = (x[DFT_N1:] * sc_ref[...]).astype(o_ref.dtype)


def _filter_spectrum(a, g, scale, n2, kb):
    c = a.shape[-1]
    return pl.pallas_call(
        functools.partial(_spectrum_kernel, kb=kb),
        grid=(n2 // kb,),
        in_specs=[pl.BlockSpec((1, 2, kb, DFT_N1, c), lambda k: (0, 0, k, 0, 0)),
                  pl.BlockSpec((kb, 2 * DFT_N1, 2 * DFT_N1), lambda k: (k, 0, 0)), _full((1, c))],
        out_specs=pl.BlockSpec((2, kb, DFT_N1, c), lambda k: (0, k, 0, 0)),
        out_shape=jax.ShapeDtypeStruct((2, n2, DFT_N1, c), BF16),
        compiler_params=_params(("arbitrary",)),
        name="filter_spectrum",
    )(a, g, scale)


def _dft_mid_kernel(a_ref, g_ref, kh_ref, f_ref, o_ref, *, kb):
    for j in range(kb):
        rhs = jnp.concatenate([a_ref[0, 0, j], a_ref[0, 1, j]], axis=0)
        x = jnp.dot(g_ref[j], rhs, preferred_element_type=F32)
        xr, xi = x[:DFT_N1], x[DFT_N1:]
        kr, ki = kh_ref[0, j].astype(F32), kh_ref[1, j].astype(F32)
        yr = xr * kr - xi * ki
        yi = xr * ki + xi * kr
        v = jnp.concatenate([yr, -yi], axis=0).astype(BF16)
        b = jnp.dot(f_ref[...], v, preferred_element_type=F32)
        o_ref[0, 0, :, j, :] = b[:DFT_N1]
        o_ref[0, 1, :, j, :] = b[DFT_N1:]


def _dft_mid(a, g, khat, f1, pairs, n2, kb):
    c = a.shape[-1]
    blk = pl.BlockSpec((1, 2, kb, DFT_N1, c), lambda k, p: (p, 0, k, 0, 0))
    return pl.pallas_call(
        functools.partial(_dft_mid_kernel, kb=kb),
        grid=(n2 // kb, pairs),
        in_specs=[blk, pl.BlockSpec((kb, 2 * DFT_N1, 2 * DFT_N1), lambda k, p: (k, 0, 0)),
                  pl.BlockSpec((2, kb, DFT_N1, c), lambda k, p: (0, k, 0, 0)), _full(f1.shape)],
        out_specs=pl.BlockSpec((1, 2, DFT_N1, kb, c), lambda k, p: (p, 0, 0, k, 0)),
        out_shape=jax.ShapeDtypeStruct((pairs, 2, DFT_N1, n2, c), F32),
        compiler_params=_params(("arbitrary", "arbitrary")),
        name="dft_mid",
    )(a, g, khat, f1)


def _dft_last_kernel(b_ref, h_ref, u_ref, x0_ref, bd_ref, o_ref, *, c):
    half = h_ref.shape[1] // 2
    bd = bd_ref[...]
    for t in range(DFT_ROW_CHUNK):
        rhs = jnp.concatenate([b_ref[0, 0, t], b_ref[0, 1, t]], axis=0).astype(BF16)
        v = jnp.dot(h_ref[t], rhs, preferred_element_type=F32)
        o_ref[0, :, t, :] = (v[:half] + u_ref[0, :, t, :] * bd) * x0_ref[0, :, t, :]
        o_ref[1, :, t, :] = (-v[half:] + u_ref[1, :, t, :] * bd) * x0_ref[1, :, t, :]


def _dft_last(bm, hh, u_view, x0_view, bias_d, pairs, n2, c):
    half = n2 // 2
    tc = DFT_ROW_CHUNK
    io = pl.BlockSpec((2, half, tc, c), lambda p, j: (p, 0, j, 0))
    return pl.pallas_call(
        functools.partial(_dft_last_kernel, c=c),
        grid=(pairs, DFT_N1 // tc),
        in_specs=[pl.BlockSpec((1, 2, tc, n2, c), lambda p, j: (p, 0, j, 0, 0)),
                  pl.BlockSpec((tc, n2, 2 * n2), lambda p, j: (j, 0, 0)), io, io, _full((1, c))],
        out_specs=io,
        out_shape=jax.ShapeDtypeStruct((2 * pairs, half, DFT_N1, c), F32),
        compiler_params=_params(("arbitrary", "arbitrary")),
        name="dft_last",
    )(bm, hh, u_view, x0_view, bias_d)


def _long_conv_mixer(u, x0, kfilt, ksum, bias_d):
    b, n, c = u.shape
    n2 = 2 * n // DFT_N1
    half = n2 // 2
    pairs = b // 2
    m_data, m_filt, g, f1, hh = (jnp.asarray(t) for t in _dft_tables(n2))
    kb = min(16, n2)
    scale = 1.0 / (ksum * float(DFT_N1 * n2))
    ka = _dft_stage1(m_filt, [kfilt.reshape(1, n2, DFT_N1, c)], 1, n2, n2, c)
    khat = _filter_spectrum(ka.reshape(1, 2, n2, DFT_N1, c), g, scale, n2, kb)
    u_view = u.reshape(b, half, DFT_N1, c)
    a = _dft_stage1(m_data, [u_view, u_view], pairs, half, n2, c)
    bm = _dft_mid(a.reshape(pairs, 2, n2, DFT_N1, c), g, khat, f1, pairs, n2, kb)
    out = _dft_last(bm, hh, u_view, x0.reshape(b, half, DFT_N1, c),
                    bias_d.reshape(1, c), pairs, n2, c)
    return out.reshape(b, n, c)


@functools.lru_cache(maxsize=None)
def _small_dft_tables(n):
    big = 2 * n
    a = np.arange(big)
    fr, fi = _twiddle(np.outer(a, a), big)
    cast = lambda m: np.asarray(m, dtype=BF16)
    return (cast(np.concatenate([fr, fi], 0)),
            cast(_real_form(fr[:, :n], fi[:, :n])),
            cast(_real_form(fr[:n], fi[:n])))


def _small_conv_kernel(k_ref, ks_ref, u_ref, x0_ref, bd_ref, mf_ref, md_ref, mi_ref, o_ref, *, n):
    big = 2 * n
    kh = jnp.dot(mf_ref[...], k_ref[...].astype(BF16), preferred_element_type=F32) * (1.0 / (ks_ref[...] * big))
    kr, ki = kh[:big], kh[big:]
    rhs = jnp.concatenate([u_ref[0], u_ref[1]], axis=0).astype(BF16)
    x = jnp.dot(md_ref[...], rhs, preferred_element_type=F32)
    xr, xi = x[:big], x[big:]
    v = jnp.concatenate([xr * kr - xi * ki, -(xr * ki + xi * kr)], axis=0).astype(BF16)
    y = jnp.dot(mi_ref[...], v, preferred_element_type=F32)
    bd = bd_ref[...]
    o_ref[0] = ((y[:n] + u_ref[0] * bd) * x0_ref[0]).astype(o_ref.dtype)
    o_ref[1] = ((-y[n:] + u_ref[1] * bd) * x0_ref[1]).astype(o_ref.dtype)


def _small_conv_mixer(u, x0, kfilt, ksum, bias_d):
    b, n, c = u.shape
    mf, md, mi = (jnp.asarray(t) for t in _small_dft_tables(n))
    io = pl.BlockSpec((2, n, c), lambda p: (p, 0, 0))
    return pl.pallas_call(
        functools.partial(_small_conv_kernel, n=n),
        grid=(b // 2,),
        in_specs=[_full((2 * n, c)), _full((1, c)), io, io, _full((1, c)),
                  _full(mf.shape), _full(md.shape), _full(mi.shape)],
        out_specs=io,
        out_shape=jax.ShapeDtypeStruct((b, n, c), BF16),
        compiler_params=_params(("arbitrary",)),
        name="small_conv",
    )(kfilt, ksum, u, x0, bias_d.reshape(1, c), mf, md, mi)


MIX_HALO = 2 * SUBLANES
MXU_DIM = 256
FFN_CHUNK_EDGES = (0, 5 * MXU_DIM, D_FF)


def _gelu_tanh(x):
    return 0.5 * x * (1.0 + jnp.tanh(math.sqrt(2.0 / math.pi) * (x + 0.044715 * (x * x * x))))


def _ffn_kernel(x_ref, xp_ref, xn_ref, oa_ref, oap_ref, oan_ref, oc_ref, ocp_ref, ocn_ref, wa_ref, wc_ref, g1_ref,
                sh_ref, sc_ref, gt_ref, g_ref, wu_ref, cw_ref, cb_ref, wd_ref, o_ref):
    i = pl.program_id(1)
    nt = pl.num_programs(1)
    tm = x_ref.shape[1]
    r = tm + 2 * SUBLANES

    def with_halo(m_ref, p_ref, n_ref):
        return jnp.concatenate([p_ref[0], m_ref[0], n_ref[0]], axis=0)

    mix = (jnp.dot(with_halo(oa_ref, oap_ref, oan_ref).astype(BF16), wa_ref[...], preferred_element_type=F32)
           + jnp.dot(with_halo(oc_ref, ocp_ref, ocn_ref).astype(BF16), wc_ref[...], preferred_element_type=F32))
    ext = with_halo(x_ref, xp_ref, xn_ref) + g1_ref[0] * mix[MIX_HALO - SUBLANES:MIX_HALO - SUBLANES + r]
    xm = ext[SUBLANES:tm + SUBLANES]
    h = _modulated_norm(ext, g_ref[...], sh_ref[0], sc_ref[0])
    row = lax.broadcasted_iota(jnp.int32, (r, 1), 0)
    inside = jnp.logical_and(jnp.logical_or(i > 0, row >= SUBLANES), jnp.logical_or(i < nt - 1, row < tm + SUBLANES))
    h = jnp.where(inside, h, 0.0).astype(BF16)
    hm = h[SUBLANES:tm + SUBLANES]
    acc = jnp.zeros((tm, D_MODEL), F32)
    for c0, c1 in zip(FFN_CHUNK_EDGES[:-1], FFN_CHUNK_EDGES[1:]):
        a = jnp.dot(h, wu_ref[:, c0:c1], preferred_element_type=F32)
        v = jnp.dot(hm, wu_ref[:, D_FF + c0:D_FF + c1], preferred_element_type=F32)
        ap = pltpu.roll(a, 1, 0)[SUBLANES:tm + SUBLANES]
        an = pltpu.roll(a, r - 1, 0)[SUBLANES:tm + SUBLANES]
        cw = cw_ref[:, c0:c1]
        conv = ap * cw[0:1] + a[SUBLANES:tm + SUBLANES] * cw[1:2] + an * cw[2:3] + cb_ref[:, c0:c1]
        act = (_gelu_tanh(conv) * v).astype(BF16)
        acc = acc + jnp.dot(act, wd_ref[c0:c1, :], preferred_element_type=F32)
    o_ref[0] = xm + gt_ref[0] * acc


def _mixer_out_ffn(x, oa, oc, wo_bf, gate1, sh, sc, gate2, gain, wu_bf, conv_w, conv_b, wd_bf):
    bx, t, d = x.shape
    tm = min(512, t)
    vec = pl.BlockSpec((1, 1, d), lambda b, i: (b, 0, 0))
    once = pl.Buffered(1)
    return pl.pallas_call(
        _ffn_kernel,
        grid=(bx, t // tm),
        in_specs=_halo_specs(tm, t, d) + _halo_specs(tm, t, Q_W, MIX_HALO) + _halo_specs(tm, t, CONV_W, MIX_HALO) + [
            pl.BlockSpec((Q_W, d), lambda b, i: (0, 0)), pl.BlockSpec((CONV_W, d), lambda b, i: (1, 0)),
            vec, vec, vec, vec, _full((1, d)),
            pl.BlockSpec((d, 2 * D_FF), lambda b, i: (0, 0), pipeline_mode=once),
            _full((3, D_FF)), _full((1, D_FF)),
            pl.BlockSpec((D_FF, d), lambda b, i: (0, 0), pipeline_mode=once)],
        out_specs=pl.BlockSpec((1, tm, d), lambda b, i: (b, i, 0)),
        out_shape=jax.ShapeDtypeStruct((bx, t, d), F32),
        compiler_params=_params(("arbitrary", "arbitrary")),
        name="mixer_out_ffn",
    )(x, x, x, oa, oa, oa, oc, oc, oc, wo_bf, wo_bf, gate1, sh, sc, gate2, gain, wu_bf, conv_w,
      conv_b.reshape(1, D_FF), wd_bf)


@functools.lru_cache(maxsize=None)
def _rope_tables(t):
    pos = np.arange(t)
    n_freq = HEAD_DIM // 4
    inv = ROPE_THETA ** (-np.arange(n_freq, dtype=np.float64) / n_freq)
    ang_r = (pos // GRID_W)[:, None] * inv
    ang_c = (pos % GRID_W)[:, None] * inv
    zero = np.zeros_like(ang_r)
    cos = np.concatenate([np.cos(ang_r)] * 2 + [np.cos(ang_c)] * 2, 1)
    sa = np.concatenate([-np.sin(ang_r), zero, -np.sin(ang_c), zero], 1)
    sb = np.concatenate([zero, np.sin(ang_r), zero, np.sin(ang_c)], 1)
    tile = lambda m: np.tile(m, (1, LANES // HEAD_DIM)).astype(np.float32)
    return tile(cos), tile(sa), tile(sb)


@functools.lru_cache(maxsize=None)
def _identity_rope_tables(t):
    return np.ones((t, LANES), np.float32), np.zeros((t, LANES), np.float32), np.zeros((t, LANES), np.float32)


@functools.lru_cache(maxsize=None)
def _head_block_diag():
    hid = np.arange(Q_W) // HEAD_DIM
    return np.asarray(hid[:, None] == hid[None, :], dtype=BF16)


def kernel(x, c, ctx, c_ctx, ada_w, ada_b, norm_mix, norm_ffn, mix_w_in, mix_w_out, attn_q_norm, attn_k_norm,
           swa_sink, hy_conv_w, hy_conv_b, hy_w1, hy_b1, hy_w2, hy_b2, hy_w3, hy_b3, hy_w4, hy_freq, hy_bias_d,
           sc_conv_w, ffn_w_up, ffn_conv_w, ffn_conv_b, ffn_w_down):
    b, s, d = x.shape
    s_ctx = ctx.shape[1]
    depth = ada_w.shape[0]
    assert d == D_MODEL and b % 2 == 0 and b + 1 <= MOD_ROWS and s % 1024 == 0 and s_ctx % SUBLANES == 0

    cvec = jnp.concatenate([c, c_ctx[None, :], jnp.zeros((MOD_ROWS - b - 1, d), F32)], axis=0)
    mods = _mods(cvec, ada_w, ada_b)
    rope = [jnp.asarray(t) for t in _rope_tables(s)]
    rope_ctx = [jnp.asarray(t) for t in _identity_rope_tables(s_ctx)]
    bd = jnp.asarray(_head_block_diag())
    xc = ctx
    weights = _weights_bf16([mix_w_in, mix_w_out, ffn_w_up, ffn_w_down])

    for i in range(depth):
        last = i == depth - 1
        j = i // 2
        lat = [mods[i, :b, k * d:(k + 1) * d][:, None, :] for k in range(6)]
        cx = [jnp.broadcast_to(mods[i, b, k * d:(k + 1) * d][None, None, :], (b, 1, d)) for k in range(6)]
        w_in, w_out, w_up, w_down = (w[i] for w in weights)
        g_mix = norm_mix[i].reshape(1, d)
        g_ffn = norm_ffn[i].reshape(1, d)
        qg = jnp.tile(attn_q_norm[i], N_Q_HEADS).reshape(1, Q_W)
        kg = jnp.tile(attn_k_norm[i], N_KV_HEADS).reshape(1, KV_W)

        conv = (hy_conv_w[j], hy_conv_b[j]) if i % 2 == 0 else (sc_conv_w[j],)
        q, k, vt, *mix = _inproj(x, lat[0], lat[1], g_mix, w_in, qg, kg, bd, *rope, *conv)
        qc, kc, vct, *mixc = _inproj(xc, cx[0], cx[1], g_mix, w_in, qg, kg, bd, *rope_ctx, *conv)

        bound = _score_bound(attn_q_norm[i], attn_k_norm[i])
        if i % 2 == 0:
            fargs = (hy_w1[j], hy_b1[j], hy_w2[j], hy_b2[j], hy_w3[j], hy_b3[j], hy_w4[j], hy_freq[j])
            o_attn = _dense_attention(q, k, vt, kc, vct, bound=bound)
            kf, ks = _implicit_filter(s, *fargs)
            o_conv = _long_conv_mixer(*mix, kf, ks, hy_bias_d[j])
            if not last:
                oc_attn = _dense_attention(qc, kc, vct, bound=bound)
                kfc, ksc = _implicit_filter(s_ctx, *fargs)
                oc_conv = _small_conv_mixer(*mixc, kfc, ksc, hy_bias_d[j])
        else:
            sink = swa_sink[j].reshape(N_KV_HEADS, GROUP)
            o_attn = _banded_attention(q, k, vt, kc, vct, sink, bound)
            o_conv = mix[0]
            if not last:
                oc_attn = _dense_attention(qc, kc, vct, sink=sink)
                oc_conv = mixc[0]

        ffn = (g_ffn, w_up, ffn_conv_w[i], ffn_conv_b[i], w_down)
        x = _mixer_out_ffn(x, o_attn, o_conv, w_out, lat[2], lat[3], lat[4], lat[5], *ffn)
        if not last:
            xc = _mixer_out_ffn(xc, oc_attn, oc_conv, w_out, cx[2], cx[3], cx[4], cx[5], *ffn)
    return x
```

```python
import functools
import math

import numpy as np
import jax
import jax.numpy as jnp
from jax import lax
from jax.experimental import pallas as pl
from jax.experimental.pallas import tpu as pltpu

F32 = jnp.float32
BF16 = jnp.bfloat16
HI = lax.Precision.HIGHEST

D_MODEL = 1024
GRID_W = 64
HEAD_DIM = 64
N_Q_HEADS = 8
N_KV_HEADS = 4
GROUP = N_Q_HEADS // N_KV_HEADS
Q_W = N_Q_HEADS * HEAD_DIM
KV_W = N_KV_HEADS * HEAD_DIM
QKV_W = Q_W + 2 * KV_W
CONV_W = D_MODEL // 2
MIX_IN_W = QKV_W + 3 * CONV_W
WINDOW = 128
ROPE_THETA = 10000.0
FILT_EMB = 33
FILT_WIDTH = 64
DECAY_TARGET = 1e-2
FAST_DECAY_PCT = 0.3
SLOW_DECAY_PCT = 1.5
DECAY_SHIFT = 0.05
D_FF = 2816
NEG_INF = -1e30
RMS_EPS = 1e-6
LOG2E = 1.4426950408889634
Q_SCALE = HEAD_DIM ** -0.5 * LOG2E

LANES = 128
SUBLANES = 8
DFT_N1 = 128
VMEM_LIMIT_MB = 56
VT_ROWS = LANES
MOD_ROWS = SUBLANES
PROJ_SUB_ROWS = 128


def _params(sem, vmem_mb=VMEM_LIMIT_MB):
    return pltpu.CompilerParams(dimension_semantics=sem, vmem_limit_bytes=vmem_mb * 1024 * 1024)


def _full(shape):
    nd = len(shape)
    return pl.BlockSpec(shape, lambda *_: (0,) * nd)


def _mods_kernel(c_ref, w_ref, b_ref, o_ref):
    c = c_ref[...]
    s = c / (1.0 + jnp.exp(-c))
    o_ref[0] = _dot_bf16x3(s, w_ref[0]) + b_ref[0]


def _mods(cvec, ada_w, ada_b):
    depth, d, n6 = ada_w.shape
    tn = 1536
    return pl.pallas_call(
        _mods_kernel,
        grid=(depth, n6 // tn),
        in_specs=[_full((MOD_ROWS, d)),
                  pl.BlockSpec((1, d, tn), lambda l, j: (l, 0, j)),
                  pl.BlockSpec((1, 1, tn), lambda l, j: (l, 0, j))],
        out_specs=pl.BlockSpec((1, MOD_ROWS, tn), lambda l, j: (l, 0, j)),
        out_shape=jax.ShapeDtypeStruct((depth, MOD_ROWS, n6), F32),
        compiler_params=_params(("arbitrary", "arbitrary")),
        name="ada_mods",
    )(cvec, ada_w, ada_b.reshape(depth, 1, n6))


def _modulated_norm(x, gain, shift, scale):
    ms = jnp.mean(x * x, axis=-1, keepdims=True)
    return (x * lax.rsqrt(ms + RMS_EPS)) * gain * (1.0 + scale) + shift


WEIGHT_CAST_STEPS = 8


def _cast_kernel(*refs):
    n = len(refs) // 2
    for w_ref, o_ref in zip(refs[:n], refs[n:]):
        o_ref[...] = w_ref[0].astype(o_ref.dtype)


def _weights_bf16(stacked):
    depth = stacked[0].shape[0]
    in_specs, out_specs, out_shape, args = [], [], [], []
    for w in stacked:
        _, rows, cols = w.shape
        tr = rows // WEIGHT_CAST_STEPS
        for layer in range(depth):
            in_specs.append(pl.BlockSpec((1, tr, cols), lambda r, layer=layer: (layer, r, 0)))
            out_specs.append(pl.BlockSpec((tr, cols), lambda r: (r, 0)))
            out_shape.append(jax.ShapeDtypeStruct((rows, cols), BF16))
            args.append(w)
    outs = pl.pallas_call(
        _cast_kernel,
        grid=(WEIGHT_CAST_STEPS,),
        in_specs=in_specs,
        out_specs=out_specs,
        out_shape=out_shape,
        compiler_params=_params(("arbitrary",)),
        name="weights_bf16",
    )(*args)
    return [outs[k * depth:(k + 1) * depth] for k in range(len(stacked))]


def _halo_specs(tm, t, width, rows=SUBLANES):
    nb = tm // rows
    last = t // rows - 1
    return [pl.BlockSpec((1, tm, width), lambda b, i: (b, i, 0)),
            pl.BlockSpec((1, rows, width), lambda b, i: (b, jnp.maximum(i * nb - 1, 0), 0)),
            pl.BlockSpec((1, rows, width), lambda b, i: (b, jnp.minimum((i + 1) * nb, last), 0))]


def _inproj_kernel(*refs, mode):
    (x_ref, xp_ref, xn_ref, sh_ref, sc_ref, g_ref, w_ref, qg_ref, kg_ref, bd_ref,
     cos_ref, sa_ref, sb_ref, cw_ref) = refs[:14]
    if mode == "hyena":
        cb_ref, q_ref, k_ref, vt_ref, u_ref, x0_ref = refs[14:]
    else:
        q_ref, k_ref, vt_ref, oc_ref = refs[14:]
    i = pl.program_id(1)
    nt = pl.num_programs(1)
    tm = x_ref.shape[1]
    sub = min(PROJ_SUB_ROWS, tm)
    gain, shift, scale = g_ref[...], sh_ref[0], sc_ref[0]

    cw = cw_ref[...]

    def conv_stage(ext, rows):
        n = sub + 2 * SUBLANES
        mid = slice(SUBLANES, sub + SUBLANES)
        if mode == "hyena":
            c = (pltpu.roll(ext, 1, 0)[mid] * cw[0:1] + ext[mid] * cw[1:2] + pltpu.roll(ext, n - 1, 0)[mid] * cw[2:3]
                 + cb_ref[...])
            x0_ref[0, rows] = c[:, :CONV_W]
            u_ref[0, rows] = c[:, 2 * CONV_W:] * c[:, CONV_W:2 * CONV_W]
        else:
            pr = ext[:, CONV_W:2 * CONV_W] * ext[:, 2 * CONV_W:]
            conv = pltpu.roll(pr, 1, 0)[mid] * cw[0:1] + pr[mid] * cw[1:2] + pltpu.roll(pr, n - 1, 0)[mid] * cw[2:3]
            oc_ref[0, rows] = (ext[mid, :CONV_W] * conv).astype(oc_ref.dtype)

    def head_norm_rope(t, hgain, bd, cos, sa, sb):
        sq = (t * t).astype(BF16)
        ssq = jnp.concatenate([jnp.dot(sq[:, c:c + MXU_DIM], bd[:MXU_DIM, :MXU_DIM], preferred_element_type=F32)
                               for c in range(0, t.shape[1], MXU_DIM)], axis=1)
        t = t * lax.rsqrt(ssq * (1.0 / HEAD_DIM) + RMS_EPS) * hgain
        outs = []
        for j in range(t.shape[1] // LANES):
            tj = t[:, j * LANES:(j + 1) * LANES]
            outs.append(tj * cos + pltpu.roll(tj, LANES - 16, 1) * sa + pltpu.roll(tj, 16, 1) * sb)
        return outs

    held = None
    for r0 in range(0, tm, sub):
        rows = slice(r0, r0 + sub)
        h = _modulated_norm(x_ref[0, rows], gain, shift, scale)
        z = jnp.dot(h.astype(BF16), w_ref[:, :QKV_W], preferred_element_type=F32)
        first, last = r0 == 0, r0 + sub == tm
        hc = jnp.concatenate(([_modulated_norm(xp_ref[0], gain, shift, scale)] if first else []) + [h]
                             + ([_modulated_norm(xn_ref[0], gain, shift, scale)] if last else []), axis=0)
        zc = jnp.dot(hc.astype(BF16), w_ref[:, QKV_W:], preferred_element_type=F32)
        if first:
            before = jnp.where(i > 0, zc[:SUBLANES], 0.0)
            zc = zc[SUBLANES:]
        if last:
            after = jnp.where(i < nt - 1, zc[sub:], 0.0)
            zc = zc[:sub]
        if held is not None:
            conv_stage(jnp.concatenate([before, held[0], zc[:SUBLANES]], axis=0), held[1])
            before = held[0][-SUBLANES:]
        held = (zc, rows)
        tabs = (cos_ref[rows], sa_ref[rows], sb_ref[rows])
        qs = head_norm_rope(z[:, :Q_W], qg_ref[...], bd_ref[...], *tabs)
        ks = head_norm_rope(z[:, Q_W:Q_W + KV_W], kg_ref[...], bd_ref[:KV_W, :KV_W], *tabs)
        for j, qj in enumerate(qs):
            qj = (qj * Q_SCALE).astype(BF16)
            q_ref[0, 2 * j, rows] = qj[:, :HEAD_DIM]
            q_ref[0, 2 * j + 1, rows] = qj[:, HEAD_DIM:]
        for j, kj in enumerate(ks):
            kj = kj.astype(BF16)
            k_ref[0, 2 * j, rows] = kj[:, :HEAD_DIM]
            k_ref[0, 2 * j + 1, rows] = kj[:, HEAD_DIM:]
        vt = z[:, Q_W + KV_W:QKV_W].T
        ones_row = (lax.broadcasted_iota(jnp.int32, (VT_ROWS - HEAD_DIM, sub), 0) == 0).astype(F32)
        for hh in range(N_KV_HEADS):
            vt_ref[0, hh, :, rows] = jnp.concatenate([vt[hh * HEAD_DIM:(hh + 1) * HEAD_DIM], ones_row],
                                                     axis=0).astype(BF16)

    conv_stage(jnp.concatenate([before, held[0], after], axis=0), held[1])


def _inproj(x, sh, sc, gain, w_bf, qg, kg, bd, cos, sa, sb, conv_w, conv_b=None):
    bx, t, d = x.shape
    tm = min(1024, t)
    mode = "hyena" if conv_b is not None else "short"
    vec = pl.BlockSpec((1, 1, d), lambda b, i: (b, 0, 0))
    tab = pl.BlockSpec((tm, LANES), lambda b, i: (i, 0))
    row = lambda w: pl.BlockSpec((1, tm, w), lambda b, i: (b, i, 0))
    in_specs = _halo_specs(tm, t, d) + [vec, vec, _full((1, d)), _full((d, MIX_IN_W)), _full((1, Q_W)),
                                        _full((1, KV_W)), _full((Q_W, Q_W)), tab, tab, tab, _full(conv_w.shape)]
    args = [x, x, x, sh, sc, gain, w_bf, qg, kg, bd, cos, sa, sb, conv_w]
    out_specs = [pl.BlockSpec((1, N_Q_HEADS, tm, HEAD_DIM), lambda b, i: (b, 0, i, 0)),
                 pl.BlockSpec((1, N_KV_HEADS, tm, HEAD_DIM), lambda b, i: (b, 0, i, 0)),
                 pl.BlockSpec((1, N_KV_HEADS, VT_ROWS, tm), lambda b, i: (b, 0, 0, i))]
    out_shape = [jax.ShapeDtypeStruct((bx, N_Q_HEADS, t, HEAD_DIM), BF16),
                 jax.ShapeDtypeStruct((bx, N_KV_HEADS, t, HEAD_DIM), BF16),
                 jax.ShapeDtypeStruct((bx, N_KV_HEADS, VT_ROWS, t), BF16)]
    if mode == "hyena":
        in_specs.append(_full((1, 3 * CONV_W)))
        args.append(conv_b.reshape(1, 3 * CONV_W))
        out_specs += [row(CONV_W), row(CONV_W)]
        out_shape += [jax.ShapeDtypeStruct((bx, t, CONV_W), F32)] * 2
    else:
        out_specs.append(row(CONV_W))
        out_shape.append(jax.ShapeDtypeStruct((bx, t, CONV_W), BF16))
    return pl.pallas_call(
        functools.partial(_inproj_kernel, mode=mode),
        grid=(bx, t // tm),
        in_specs=in_specs,
        out_specs=out_specs,
        out_shape=out_shape,
        compiler_params=_params(("arbitrary", "arbitrary")),
        name="in_proj",
    )(*args)


def _score_bound(q_gain, k_gain):
    b = HEAD_DIM * Q_SCALE * BOUND_MARGIN * jnp.max(jnp.abs(q_gain)) * jnp.max(jnp.abs(k_gain))
    return b.reshape(1, 1).astype(F32)


def _scores_t(ks, q):
    return lax.dot_general(ks, q, (((1,), (1,)), ((), ())), preferred_element_type=F32)


def _sink_column(sink_ref, hh, tq, m, acc):
    col = lax.broadcasted_iota(jnp.int32, m.shape, 1)
    sink = jnp.where(col < tq, sink_ref[hh, 0], sink_ref[hh, 1]) * LOG2E
    m_new = jnp.maximum(m, sink)
    den_row = lax.broadcasted_iota(jnp.int32, acc.shape, 0) == HEAD_DIM
    return acc * jnp.exp2(m - m_new) + jnp.where(den_row, jnp.exp2(sink - m_new), 0.0)


def _attn_finish(acc, tq):
    o = (acc / acc[HEAD_DIM:HEAD_DIM + 1]).T
    return jnp.concatenate([o[:tq, :HEAD_DIM], o[tq:, :HEAD_DIM]], axis=1)


EXP_CHUNK_ELEMS = 32 * SUBLANES * LANES


def _score_stage(ks, q, s_ref, bias=None):
    s = _scores_t(ks, q)
    if bias is not None:
        s = s + bias
    s_ref[0:s.shape[0]] = s
    return jnp.max(s, axis=0, keepdims=True)


def _softmax_stage(s_ref, p_ref, n, vt, mx, m, acc_ref):
    m_new = jnp.maximum(m, mx)
    rows = EXP_CHUNK_ELEMS // m.shape[1]
    for c in range(0, n, rows):
        p_ref[c:c + rows] = jnp.exp2(s_ref[c:c + rows] - m_new).astype(BF16)
    acc_ref[...] = acc_ref[...] * jnp.exp2(m - m_new) + jnp.dot(vt, p_ref[0:n], preferred_element_type=F32)
    return m_new


BOUNDED_TILES = 8
BOUND_MARGIN = 1.01
SAFE_SHIFT = 40.0


def _flash_exact(q, k_tile, vt_tile, ctx, sa_ref, sb_ref, p_ref, acc_ref, *, tk, n_main):
    pa_ref, pb_ref = p_ref.at[0:tk], p_ref.at[tk:2 * tk]
    acc_ref[...] = jnp.zeros_like(acc_ref)
    m = jnp.full((1, q.shape[0]), NEG_INF, F32)
    mx_a = _score_stage(k_tile(0), q, sa_ref)
    if n_main > 1:
        def body(i, carry):
            m, mx_a = carry
            mx_b = _score_stage(k_tile(2 * i + 1), q, sb_ref)
            m = _softmax_stage(sa_ref, pa_ref, tk, vt_tile(2 * i), mx_a, m, acc_ref)
            mx_a = _score_stage(k_tile(2 * i + 2), q, sa_ref)
            m = _softmax_stage(sb_ref, pb_ref, tk, vt_tile(2 * i + 1), mx_b, m, acc_ref)
            return m, mx_a

        m, mx_a = lax.fori_loop(0, n_main // 2 - 1, body, (m, mx_a))
        mx_b = _score_stage(k_tile(n_main - 1), q, sb_ref)
        m = _softmax_stage(sa_ref, pa_ref, tk, vt_tile(n_main - 2), mx_a, m, acc_ref)
        last = (sb_ref, pb_ref, mx_b)
        spare = (sa_ref, pa_ref)
    else:
        last = (sa_ref, pa_ref, mx_a)
        spare = (sb_ref, pb_ref)
    if ctx is not None:
        mx_c = _score_stage(ctx[0], q, spare[0])
    m = _softmax_stage(last[0], last[1], tk, vt_tile(n_main - 1), last[2], m, acc_ref)
    if ctx is not None:
        m = _softmax_stage(spare[0], spare[1], ctx[0].shape[0], ctx[1], mx_c, m, acc_ref)
    return m


def _exp2_as_bf16(s):
    return jnp.exp2(s).astype(BF16)


def _flash_bounded(q, k_ref, vt_ref, ctx, p_ref, acc_ref, *, tk, n_main):
    step = BOUNDED_TILES * tk if n_main % BOUNDED_TILES == 0 else tk

    def group(j):
        off = pl.multiple_of(j * step, step)
        for c in range(0, step, tk):
            p_ref[c:c + tk] = _exp2_as_bf16(_scores_t(k_ref[0, 0, pl.ds(off + c, tk), :], q))
        return jnp.dot(vt_ref[0, 0, :, pl.ds(off, step)], p_ref[0:step], preferred_element_type=F32)

    first = group(0)
    if ctx is not None:
        pc_ref = p_ref.at[BOUNDED_TILES * tk:BOUNDED_TILES * tk + ctx[0].shape[0]]
        pc_ref[...] = _exp2_as_bf16(_scores_t(ctx[0], q))
        first = first + jnp.dot(ctx[1], pc_ref[...], preferred_element_type=F32)
    acc_ref[...] = first

    def body(j, carry):
        acc_ref[...] += group(j)
        return carry

    lax.fori_loop(1, n_main * tk // step, body, 0)


def _flash_kernel(*refs, tq, tk, n_main, has_ctx, has_sink):
    sa_ref, sb_ref, p_ref, acc_ref = refs[-4:]
    o_ref = refs[-5]
    refs = list(refs[:-5])
    sink_ref = refs.pop(0) if has_sink else None
    bound_ref = None if has_sink else refs.pop(0)
    q_ref, k_ref, vt_ref = refs[:3]
    m_cols = GROUP * tq
    q = q_ref[0].reshape(m_cols, HEAD_DIM)
    ctx = (refs[3][0, 0], refs[4][0, 0]) if has_ctx else None

    def k_tile(j):
        return k_ref[0, 0, pl.ds(pl.multiple_of(j * tk, tk), tk), :]

    def vt_tile(j):
        return vt_ref[0, 0, :, pl.ds(pl.multiple_of(j * tk, tk), tk)]

    exact = functools.partial(_flash_exact, q, k_tile, vt_tile, ctx, sa_ref, sb_ref, p_ref, acc_ref,
                              tk=tk, n_main=n_main)
    if has_sink:
        acc = _sink_column(sink_ref, pl.program_id(1), tq, exact(), acc_ref[...])
    else:
        bound = bound_ref[0, 0]
        safe = bound <= SAFE_SHIFT

        @pl.when(safe)
        def _():
            _flash_bounded(q, k_ref, vt_ref, ctx, p_ref, acc_ref, tk=tk, n_main=n_main)

        @pl.when(jnp.logical_not(safe))
        def _():
            exact()

        acc = acc_ref[...]
    o_ref[0] = _attn_finish(acc, tq).astype(o_ref.dtype)


def _kv_specs(n):
    return [pl.BlockSpec((1, 1, n, HEAD_DIM), lambda b, h, i: (b, h, 0, 0)),
            pl.BlockSpec((1, 1, VT_ROWS, n), lambda b, h, i: (b, h, 0, 0))]


def _dense_attention(q, k, vt, kc=None, vct=None, *, bound=None, sink=None):
    bx, _, t, _ = q.shape
    s = k.shape[2]
    tq = min(1024, t)
    tk = min(512, s)
    has_ctx = kc is not None
    has_sink = sink is not None
    in_specs = [pl.BlockSpec((1, GROUP, tq, HEAD_DIM), lambda b, h, i: (b, h, i, 0))] + _kv_specs(s)
    args = [q, k, vt]
    if has_ctx:
        in_specs += _kv_specs(kc.shape[2])
        args += [kc, vct]
    in_specs = [pl.BlockSpec(memory_space=pltpu.SMEM)] + in_specs
    args = [sink if has_sink else bound] + args
    n_main = s // tk
    assert n_main == 1 or n_main % 2 == 0
    n_ctx = kc.shape[2] if has_ctx else 0
    assert n_ctx <= tk
    m_cols = GROUP * tq
    return pl.pallas_call(
        functools.partial(_flash_kernel, tq=tq, tk=tk, n_main=n_main, has_ctx=has_ctx, has_sink=has_sink),
        grid=(bx, N_KV_HEADS, t // tq),
        in_specs=in_specs,
        out_specs=pl.BlockSpec((1, tq, LANES), lambda b, h, i: (b, i, h)),
        out_shape=jax.ShapeDtypeStruct((bx, t, Q_W), BF16),
        scratch_shapes=[pltpu.VMEM((tk, m_cols), F32), pltpu.VMEM((tk, m_cols), F32),
                        pltpu.VMEM((BOUNDED_TILES * tk + n_ctx, m_cols), BF16), pltpu.VMEM((VT_ROWS, m_cols), F32)],
        compiler_params=_params(("arbitrary", "arbitrary", "arbitrary")),
        name="dense_attention",
    )(*args)


BAND_SUB = 256


@functools.lru_cache(maxsize=None)
def _band_bias(n_ctx):
    span = BAND_SUB + 2 * WINDOW
    kr = np.arange(span)[None, :, None]
    qc = (np.arange(GROUP * BAND_SUB) % BAND_SUB)[None, None, :]
    rel = np.arange(3)[:, None, None]
    band = np.where(np.abs(kr - qc - rel * WINDOW) <= WINDOW, 0.0, NEG_INF)
    return np.concatenate([band, np.zeros((3, n_ctx, GROUP * BAND_SUB))], axis=1).astype(np.float32)


def _banded_kernel(sink_ref, bound_ref, q_ref, k_ref, vt_ref, kc_ref, vct_ref, bias_ref, o_ref,
                   sl_ref, sc_ref, p_ref, acc_ref, *, n_sub, s_len):
    i = pl.program_id(2)
    hh = pl.program_id(1)
    sub = BAND_SUB
    m_cols = GROUP * sub
    span = sub + 2 * WINDOW
    n_ctx = kc_ref.shape[2]

    def window(u):
        q = q_ref[0, :, u * sub:(u + 1) * sub, :].reshape(m_cols, HEAD_DIM)
        q0 = (i * n_sub + u) * sub
        start = pl.multiple_of(jnp.clip(q0 - WINDOW, 0, s_len - span), WINDOW)
        return q, start, bias_ref[(q0 - start) // WINDOW]

    shift = jnp.maximum(bound_ref[0, 0], jnp.maximum(sink_ref[hh, 0], sink_ref[hh, 1]) * LOG2E)
    safe = shift <= SAFE_SHIFT

    @pl.when(safe)
    def _():
        col = lax.broadcasted_iota(jnp.int32, (1, m_cols), 1)
        sink = jnp.where(col < sub, sink_ref[hh, 0], sink_ref[hh, 1]) * LOG2E
        den_row = lax.broadcasted_iota(jnp.int32, (VT_ROWS, m_cols), 0) == HEAD_DIM
        sink_den = jnp.where(den_row, jnp.exp2(sink), 0.0)
        for u in range(n_sub):
            q, start, bias = window(u)
            keys = jnp.concatenate([k_ref[0, 0, pl.ds(start, span), :], kc_ref[0, 0]], axis=0)
            vt = jnp.concatenate([vt_ref[0, 0, :, pl.ds(start, span)], vct_ref[0, 0]], axis=1)
            p_u = p_ref.at[u]
            p_u[...] = _exp2_as_bf16(_scores_t(keys, q) + bias)
            acc = jnp.dot(vt, p_u[...], preferred_element_type=F32) + sink_den
            o_ref[0, u * sub:(u + 1) * sub] = _attn_finish(acc, sub).astype(o_ref.dtype)

    @pl.when(jnp.logical_not(safe))
    def _():
        stats = []
        for u in range(n_sub):
            q, start, bias = window(u)
            mx_l = _score_stage(k_ref[0, 0, pl.ds(start, span), :], q, sl_ref.at[u], bias=bias[:span])
            mx_c = _score_stage(kc_ref[0, 0], q, sc_ref.at[u])
            stats.append((start, mx_l, mx_c))
        for u in range(n_sub):
            start, mx_l, mx_c = stats[u]
            acc_u = acc_ref.at[u]
            acc_u[...] = jnp.zeros((VT_ROWS, m_cols), F32)
            m = jnp.full((1, m_cols), NEG_INF, F32)
            p_l, p_c = p_ref.at[u, 0:span], p_ref.at[u, span:span + n_ctx]
            m = _softmax_stage(sl_ref.at[u], p_l, span, vt_ref[0, 0, :, pl.ds(start, span)], mx_l, m, acc_u)
            m = _softmax_stage(sc_ref.at[u], p_c, n_ctx, vct_ref[0, 0], mx_c, m, acc_u)
            acc = _sink_column(sink_ref, hh, sub, m, acc_u[...])
            o_ref[0, u * sub:(u + 1) * sub] = _attn_finish(acc, sub).astype(o_ref.dtype)


def _banded_attention(q, k, vt, kc, vct, sink, bound):
    bx, _, t, _ = q.shape
    n_ctx = kc.shape[2]
    n_sub = next(k for k in (8, 4, 2, 1) if t % (k * BAND_SUB) == 0)
    tq = n_sub * BAND_SUB
    m_cols = GROUP * BAND_SUB
    span = BAND_SUB + 2 * WINDOW
    bias = jnp.asarray(_band_bias(n_ctx))
    return pl.pallas_call(
        functools.partial(_banded_kernel, n_sub=n_sub, s_len=t),
        grid=(bx, N_KV_HEADS, t // tq),
        in_specs=[pl.BlockSpec(memory_space=pltpu.SMEM), pl.BlockSpec(memory_space=pltpu.SMEM),
                  pl.BlockSpec((1, GROUP, tq, HEAD_DIM), lambda b, h, i: (b, h, i, 0))]
                 + _kv_specs(t) + _kv_specs(n_ctx) + [_full(bias.shape)],
        out_specs=pl.BlockSpec((1, tq, LANES), lambda b, h, i: (b, i, h)),
        out_shape=jax.ShapeDtypeStruct((bx, t, Q_W), BF16),
        scratch_shapes=[pltpu.VMEM((n_sub, span, m_cols), F32), pltpu.VMEM((n_sub, n_ctx, m_cols), F32),
                        pltpu.VMEM((n_sub, span + n_ctx, m_cols), BF16), pltpu.VMEM((n_sub, VT_ROWS, m_cols), F32)],
        compiler_params=_params(("arbitrary", "arbitrary", "arbitrary")),
        name="banded_attention",
    )(sink, bound, q, k, vt, kc, vct, bias)


@functools.lru_cache(maxsize=None)
def _filter_features(n):
    j = np.arange(2 * n)
    d = np.where(j <= n, j, 2 * n - j)
    d = np.where(j == n, 0, d)
    bands = (FILT_EMB - 1) // 2
    t01 = np.linspace(0.0, 1.0, n)[d]
    w = 2.0 * np.pi * d.astype(np.float64) / n
    f = np.linspace(1e-4, bands - 1, bands)[None, :]
    feats = np.zeros((2 * n, LANES), np.float64)
    feats[:, 0] = t01
    feats[:, 1:1 + bands] = np.cos(f * w[:, None])
    feats[:, 1 + bands:FILT_EMB] = -np.sin(f * w[:, None])
    feats[:, 64] = t01
    feats[:, 65] = (j < n)
    feats[:, 66] = (j != n)
    return feats.astype(np.float32)


def _dot_bf16x3(a, b):
    a_hi = a.astype(BF16)
    a_lo = (a - a_hi.astype(F32)).astype(BF16)
    b_hi = b.astype(BF16)
    b_lo = (b - b_hi.astype(F32)).astype(BF16)
    dot = lambda x, y: jnp.dot(x, y, preferred_element_type=F32)
    return dot(a_hi, b_hi) + dot(a_hi, b_lo) + dot(a_lo, b_hi)


def _filter_kernel(f_ref, w1_ref, b1_ref, w2_ref, b2_ref, w3_ref, b3_ref, w4_ref, fr_ref, dl_ref, k_ref, s_ref):
    f = f_ref[...]
    fr = fr_ref[...]
    mm = lambda a, b: jnp.dot(a, b, preferred_element_type=F32, precision=HI)
    h = jnp.sin(fr * (mm(w1_ref[...], f) + b1_ref[...]))
    h = jnp.sin(fr * (mm(w2_ref[...], h) + b2_ref[...]))
    h = jnp.sin(fr * (mm(w3_ref[...], h) + b3_ref[...]))
    hf = _dot_bf16x3(w4_ref[...], h)
    win = jnp.exp(-dl_ref[...] * f[64:65]) + DECAY_SHIFT
    k = (jnp.where(f[65:66] > 0.5, hf[:CONV_W], hf[CONV_W:]) * win * f[66:67]).T
    k_ref[...] = k

    @pl.when(pl.program_id(0) == 0)
    def _():
        s_ref[...] = jnp.zeros_like(s_ref)

    s_ref[...] += jnp.sum(jnp.abs(k), axis=0, keepdims=True)


def _implicit_filter(n, w1, b1, w2, b2, w3, b3, w4, freq):
    feats_t = jnp.asarray(np.ascontiguousarray(_filter_features(n).T))
    col = lambda a: a.reshape(-1, 1)
    w1t = jnp.pad(w1.T, ((0, 0), (0, LANES - FILT_EMB)))
    deltas = np.abs(np.linspace(math.log(DECAY_TARGET) / SLOW_DECAY_PCT, math.log(DECAY_TARGET) / FAST_DECAY_PCT,
                                CONV_W)).astype(np.float32).reshape(CONV_W, 1)
    tr = min(1024, 2 * n)
    sq = _full((FILT_WIDTH, FILT_WIDTH))
    vec = _full((FILT_WIDTH, 1))
    return pl.pallas_call(
        _filter_kernel,
        grid=(2 * n // tr,),
        in_specs=[pl.BlockSpec((LANES, tr), lambda i: (0, i)), _full((FILT_WIDTH, LANES)), vec, sq, vec, sq, vec,
                  _full((2 * CONV_W, FILT_WIDTH)), vec, _full((CONV_W, 1))],
        out_specs=[pl.BlockSpec((tr, CONV_W), lambda i: (i, 0)), _full((1, CONV_W))],
        out_shape=[jax.ShapeDtypeStruct((2 * n, CONV_W), F32), jax.ShapeDtypeStruct((1, CONV_W), F32)],
        compiler_params=_params(("arbitrary",)),
        name="hyena_filter",
    )(feats_t, w1t, col(b1), w2.T, col(b2), w3.T, col(b3), w4.T, col(freq), jnp.asarray(deltas))


def _twiddle(idx, mod):
    ang = 2.0 * np.pi * (idx % mod) / mod
    return np.cos(ang), -np.sin(ang)


def _real_form(mr, mi):
    return np.concatenate([np.concatenate([mr, -mi], -1), np.concatenate([mi, mr], -1)], -2)


@functools.lru_cache(maxsize=None)
def _dft_tables(n2):
    n1 = DFT_N1
    n = n1 * n2
    h = n2 // 2
    a2 = np.arange(n2)
    fr, fi = _twiddle(np.outer(a2, a2), n2)
    m_data = _real_form(fr[:, :h], fi[:, :h])
    m_filt = np.concatenate([fr, fi], 0)
    k2 = a2[:, None, None]
    k1 = np.arange(n1)[None, :, None]
    c1 = np.arange(n1)[None, None, :]
    g = _real_form(*_twiddle(c1 * (n2 * k1 + k2), n))
    a1 = np.arange(n1)
    f1 = _real_form(*_twiddle(np.outer(a1, a1), n1))
    t2 = a1[:, None, None]
    t1 = np.arange(h)[None, :, None]
    j1 = a2[None, None, :]
    hh = _real_form(*_twiddle(j1 * (n1 * t1 + t2), n))
    cast = lambda m: np.asarray(m, dtype=BF16)
    return cast(m_data), cast(m_filt), cast(g), cast(f1), cast(hh)


DFT_ROW_CHUNK = 2 * SUBLANES


def _dft_rows_kernel(m_ref, *refs):
    o_ref = refs[-1]
    cols = [jnp.concatenate([r[0, :, t, :] for r in refs[:-1]], axis=0) for t in range(DFT_ROW_CHUNK)]
    rhs = jnp.concatenate(cols, axis=1).astype(BF16)
    out = jnp.dot(m_ref[...], rhs, preferred_element_type=F32)
    half = out.shape[0] // 2
    o_ref[0, 0] = out[:half].astype(o_ref.dtype)
    o_ref[0, 1] = out[half:].astype(o_ref.dtype)


def _dft_stage1(mat, views, pairs, rows, n2, c):
    nin = len(views)
    lc = DFT_ROW_CHUNK * c
    in_specs = [_full(mat.shape)] + [
        pl.BlockSpec((1, rows, DFT_ROW_CHUNK, c), (lambda p, j, a=a: (nin * p + a, 0, j, 0))) for a in range(nin)]
    return pl.pallas_call(
        _dft_rows_kernel,
        grid=(pairs, DFT_N1 // DFT_ROW_CHUNK),
        in_specs=in_specs,
        out_specs=pl.BlockSpec((1, 2, n2, lc), lambda p, j: (p, 0, 0, j)),
        out_shape=jax.ShapeDtypeStruct((pairs, 2, n2, DFT_N1 * c), BF16),
        compiler_params=_params(("arbitrary", "arbitrary")),
        name="dft_stage1",
    )(mat, *views)


def _spectrum_kernel(a_ref, g_ref, sc_ref, o_ref, *, kb):
    for j in range(kb):
        rhs = jnp.concatenate([a_ref[0, 0, j], a_ref[0, 1, j]], axis=0)
        x = jnp.dot(g_ref[j], rhs, preferred_element_type=F32)
        o_ref[0, j] = (x[:DFT_N1] * sc_ref[...]).astype(o_ref.dtype)
        o_ref[1, j] = (x[DFT_N1:] * sc_ref[...]).astype(o_ref.dtype)


def _filter_spectrum(a, g, scale, n2, kb):
    c = a.shape[-1]
    return pl.pallas_call(
        functools.partial(_spectrum_kernel, kb=kb),
        grid=(n2 // kb,),
        in_specs=[pl.BlockSpec((1, 2, kb, DFT_N1, c), lambda k: (0, 0, k, 0, 0)),
                  pl.BlockSpec((kb, 2 * DFT_N1, 2 * DFT_N1), lambda k: (k, 0, 0)), _full((1, c))],
        out_specs=pl.BlockSpec((2, kb, DFT_N1, c), lambda k: (0, k, 0, 0)),
        out_shape=jax.ShapeDtypeStruct((2, n2, DFT_N1, c), BF16),
        compiler_params=_params(("arbitrary",)),
        name="filter_spectrum",
    )(a, g, scale)


def _dft_mid_kernel(a_ref, g_ref, kh_ref, f_ref, o_ref, *, kb):
    for j in range(kb):
        rhs = jnp.concatenate([a_ref[0, 0, j], a_ref[0, 1, j]], axis=0)
        x = jnp.dot(g_ref[j], rhs, preferred_element_type=F32)
        xr, xi = x[:DFT_N1], x[DFT_N1:]
        kr, ki = kh_ref[0, j].astype(F32), kh_ref[1, j].astype(F32)
        yr = xr * kr - xi * ki
        yi = xr * ki + xi * kr
        v = jnp.concatenate([yr, -yi], axis=0).astype(BF16)
        b = jnp.dot(f_ref[...], v, preferred_element_type=F32)
        o_ref[0, 0, :, j, :] = b[:DFT_N1]
        o_ref[0, 1, :, j, :] = b[DFT_N1:]


def _dft_mid(a, g, khat, f1, pairs, n2, kb):
    c = a.shape[-1]
    blk = pl.BlockSpec((1, 2, kb, DFT_N1, c), lambda k, p: (p, 0, k, 0, 0))
    return pl.pallas_call(
        functools.partial(_dft_mid_kernel, kb=kb),
        grid=(n2 // kb, pairs),
        in_specs=[blk, pl.BlockSpec((kb, 2 * DFT_N1, 2 * DFT_N1), lambda k, p: (k, 0, 0)),
                  pl.BlockSpec((2, kb, DFT_N1, c), lambda k, p: (0, k, 0, 0)), _full(f1.shape)],
        out_specs=pl.BlockSpec((1, 2, DFT_N1, kb, c), lambda k, p: (p, 0, 0, k, 0)),
        out_shape=jax.ShapeDtypeStruct((pairs, 2, DFT_N1, n2, c), F32),
        compiler_params=_params(("arbitrary", "arbitrary")),
        name="dft_mid",
    )(a, g, khat, f1)


def _dft_last_kernel(b_ref, h_ref, u_ref, x0_ref, bd_ref, o_ref, *, c):
    half = h_ref.shape[1] // 2
    bd = bd_ref[...]
    for t in range(DFT_ROW_CHUNK):
        rhs = jnp.concatenate([b_ref[0, 0, t], b_ref[0, 1, t]], axis=0).astype(BF16)
        v = jnp.dot(h_ref[t], rhs, preferred_element_type=F32)
        o_ref[0, :, t, :] = (v[:half] + u_ref[0, :, t, :] * bd) * x0_ref[0, :, t, :]
        o_ref[1, :, t, :] = (-v[half:] + u_ref[1, :, t, :] * bd) * x0_ref[1, :, t, :]


def _dft_last(bm, hh, u_view, x0_view, bias_d, pairs, n2, c):
    half = n2 // 2
    tc = DFT_ROW_CHUNK
    io = pl.BlockSpec((2, half, tc, c), lambda p, j: (p, 0, j, 0))
    return pl.pallas_call(
        functools.partial(_dft_last_kernel, c=c),
        grid=(pairs, DFT_N1 // tc),
        in_specs=[pl.BlockSpec((1, 2, tc, n2, c), lambda p, j: (p, 0, j, 0, 0)),
                  pl.BlockSpec((tc, n2, 2 * n2), lambda p, j: (j, 0, 0)), io, io, _full((1, c))],
        out_specs=io,
        out_shape=jax.ShapeDtypeStruct((2 * pairs, half, DFT_N1, c), F32),
        compiler_params=_params(("arbitrary", "arbitrary")),
        name="dft_last",
    )(bm, hh, u_view, x0_view, bias_d)


def _long_conv_mixer(u, x0, kfilt, ksum, bias_d):
    b, n, c = u.shape
    n2 = 2 * n // DFT_N1
    half = n2 // 2
    pairs = b // 2
    m_data, m_filt, g, f1, hh = (jnp.asarray(t) for t in _dft_tables(n2))
    kb = min(16, n2)
    scale = 1.0 / (ksum * float(DFT_N1 * n2))
    ka = _dft_stage1(m_filt, [kfilt.reshape(1, n2, DFT_N1, c)], 1, n2, n2, c)
    khat = _filter_spectrum(ka.reshape(1, 2, n2, DFT_N1, c), g, scale, n2, kb)
    u_view = u.reshape(b, half, DFT_N1, c)
    a = _dft_stage1(m_data, [u_view, u_view], pairs, half, n2, c)
    bm = _dft_mid(a.reshape(pairs, 2, n2, DFT_N1, c), g, khat, f1, pairs, n2, kb)
    out = _dft_last(bm, hh, u_view, x0.reshape(b, half, DFT_N1, c),
                    bias_d.reshape(1, c), pairs, n2, c)
    return out.reshape(b, n, c)


@functools.lru_cache(maxsize=None)
def _small_dft_tables(n):
    big = 2 * n
    a = np.arange(big)
    fr, fi = _twiddle(np.outer(a, a), big)
    cast = lambda m: np.asarray(m, dtype=BF16)
    return (cast(np.concatenate([fr, fi], 0)),
            cast(_real_form(fr[:, :n], fi[:, :n])),
            cast(_real_form(fr[:n], fi[:n])))


def _small_conv_kernel(k_ref, ks_ref, u_ref, x0_ref, bd_ref, mf_ref, md_ref, mi_ref, o_ref, *, n):
    big = 2 * n
    kh = jnp.dot(mf_ref[...], k_ref[...].astype(BF16), preferred_element_type=F32) * (1.0 / (ks_ref[...] * big))
    kr, ki = kh[:big], kh[big:]
    rhs = jnp.concatenate([u_ref[0], u_ref[1]], axis=0).astype(BF16)
    x = jnp.dot(md_ref[...], rhs, preferred_element_type=F32)
    xr, xi = x[:big], x[big:]
    v = jnp.concatenate([xr * kr - xi * ki, -(xr * ki + xi * kr)], axis=0).astype(BF16)
    y = jnp.dot(mi_ref[...], v, preferred_element_type=F32)
    bd = bd_ref[...]
    o_ref[0] = ((y[:n] + u_ref[0] * bd) * x0_ref[0]).astype(o_ref.dtype)
    o_ref[1] = ((-y[n:] + u_ref[1] * bd) * x0_ref[1]).astype(o_ref.dtype)


def _small_conv_mixer(u, x0, kfilt, ksum, bias_d):
    b, n, c = u.shape
    mf, md, mi = (jnp.asarray(t) for t in _small_dft_tables(n))
    io = pl.BlockSpec((2, n, c), lambda p: (p, 0, 0))
    return pl.pallas_call(
        functools.partial(_small_conv_kernel, n=n),
        grid=(b // 2,),
        in_specs=[_full((2 * n, c)), _full((1, c)), io, io, _full((1, c)),
                  _full(mf.shape), _full(md.shape), _full(mi.shape)],
        out_specs=io,
        out_shape=jax.ShapeDtypeStruct((b, n, c), BF16),
        compiler_params=_params(("arbitrary",)),
        name="small_conv",
    )(kfilt, ksum, u, x0, bias_d.reshape(1, c), mf, md, mi)


MIX_HALO = 2 * SUBLANES
MXU_DIM = 256
FFN_CHUNK_EDGES = (0, 5 * MXU_DIM, D_FF)


def _gelu_tanh(x):
    return 0.5 * x * (1.0 + jnp.tanh(math.sqrt(2.0 / math.pi) * (x + 0.044715 * (x * x * x))))


def _ffn_kernel(x_ref, xp_ref, xn_ref, oa_ref, oap_ref, oan_ref, oc_ref, ocp_ref, ocn_ref, wa_ref, wc_ref, g1_ref,
                sh_ref, sc_ref, gt_ref, g_ref, wu_ref, cw_ref, cb_ref, wd_ref, o_ref):
    i = pl.program_id(1)
    nt = pl.num_programs(1)
    tm = x_ref.shape[1]
    r = tm + 2 * SUBLANES

    def with_halo(m_ref, p_ref, n_ref):
        return jnp.concatenate([p_ref[0], m_ref[0], n_ref[0]], axis=0)

    mix = (jnp.dot(with_halo(oa_ref, oap_ref, oan_ref).astype(BF16), wa_ref[...], preferred_element_type=F32)
           + jnp.dot(with_halo(oc_ref, ocp_ref, ocn_ref).astype(BF16), wc_ref[...], preferred_element_type=F32))
    ext = with_halo(x_ref, xp_ref, xn_ref) + g1_ref[0] * mix[MIX_HALO - SUBLANES:MIX_HALO - SUBLANES + r]
    xm = ext[SUBLANES:tm + SUBLANES]
    h = _modulated_norm(ext, g_ref[...], sh_ref[0], sc_ref[0])
    row = lax.broadcasted_iota(jnp.int32, (r, 1), 0)
    inside = jnp.logical_and(jnp.logical_or(i > 0, row >= SUBLANES), jnp.logical_or(i < nt - 1, row < tm + SUBLANES))
    h = jnp.where(inside, h, 0.0).astype(BF16)
    hm = h[SUBLANES:tm + SUBLANES]
    acc = jnp.zeros((tm, D_MODEL), F32)
    for c0, c1 in zip(FFN_CHUNK_EDGES[:-1], FFN_CHUNK_EDGES[1:]):
        a = jnp.dot(h, wu_ref[:, c0:c1], preferred_element_type=F32)
        v = jnp.dot(hm, wu_ref[:, D_FF + c0:D_FF + c1], preferred_element_type=F32)
        ap = pltpu.roll(a, 1, 0)[SUBLANES:tm + SUBLANES]
        an = pltpu.roll(a, r - 1, 0)[SUBLANES:tm + SUBLANES]
        cw = cw_ref[:, c0:c1]
        conv = ap * cw[0:1] + a[SUBLANES:tm + SUBLANES] * cw[1:2] + an * cw[2:3] + cb_ref[:, c0:c1]
        act = (_gelu_tanh(conv) * v).astype(BF16)
        acc = acc + jnp.dot(act, wd_ref[c0:c1, :], preferred_element_type=F32)
    o_ref[0] = xm + gt_ref[0] * acc


def _mixer_out_ffn(x, oa, oc, wo_bf, gate1, sh, sc, gate2, gain, wu_bf, conv_w, conv_b, wd_bf):
    bx, t, d = x.shape
    tm = min(512, t)
    vec = pl.BlockSpec((1, 1, d), lambda b, i: (b, 0, 0))
    once = pl.Buffered(1)
    return pl.pallas_call(
        _ffn_kernel,
        grid=(bx, t // tm),
        in_specs=_halo_specs(tm, t, d) + _halo_specs(tm, t, Q_W, MIX_HALO) + _halo_specs(tm, t, CONV_W, MIX_HALO) + [
            pl.BlockSpec((Q_W, d), lambda b, i: (0, 0)), pl.BlockSpec((CONV_W, d), lambda b, i: (1, 0)),
            vec, vec, vec, vec, _full((1, d)),
            pl.BlockSpec((d, 2 * D_FF), lambda b, i: (0, 0), pipeline_mode=once),
            _full((3, D_FF)), _full((1, D_FF)),
            pl.BlockSpec((D_FF, d), lambda b, i: (0, 0), pipeline_mode=once)],
        out_specs=pl.BlockSpec((1, tm, d), lambda b, i: (b, i, 0)),
        out_shape=jax.ShapeDtypeStruct((bx, t, d), F32),
        compiler_params=_params(("arbitrary", "arbitrary")),
        name="mixer_out_ffn",
    )(x, x, x, oa, oa, oa, oc, oc, oc, wo_bf, wo_bf, gate1, sh, sc, gate2, gain, wu_bf, conv_w,
      conv_b.reshape(1, D_FF), wd_bf)


@functools.lru_cache(maxsize=None)
def _rope_tables(t):
    pos = np.arange(t)
    n_freq = HEAD_DIM // 4
    inv = ROPE_THETA ** (-np.arange(n_freq, dtype=np.float64) / n_freq)
    ang_r = (pos // GRID_W)[:, None] * inv
    ang_c = (pos % GRID_W)[:, None] * inv
    zero = np.zeros_like(ang_r)
    cos = np.concatenate([np.cos(ang_r)] * 2 + [np.cos(ang_c)] * 2, 1)
    sa = np.concatenate([-np.sin(ang_r), zero, -np.sin(ang_c), zero], 1)
    sb = np.concatenate([zero, np.sin(ang_r), zero, np.sin(ang_c)], 1)
    tile = lambda m: np.tile(m, (1, LANES // HEAD_DIM)).astype(np.float32)
    return tile(cos), tile(sa), tile(sb)


@functools.lru_cache(maxsize=None)
def _identity_rope_tables(t):
    return np.ones((t, LANES), np.float32), np.zeros((t, LANES), np.float32), np.zeros((t, LANES), np.float32)


@functools.lru_cache(maxsize=None)
def _head_block_diag():
    hid = np.arange(Q_W) // HEAD_DIM
    return np.asarray(hid[:, None] == hid[None, :], dtype=BF16)


def kernel(x, c, ctx, c_ctx, ada_w, ada_b, norm_mix, norm_ffn, mix_w_in, mix_w_out, attn_q_norm, attn_k_norm,
           swa_sink, hy_conv_w, hy_conv_b, hy_w1, hy_b1, hy_w2, hy_b2, hy_w3, hy_b3, hy_w4, hy_freq, hy_bias_d,
           sc_conv_w, ffn_w_up, ffn_conv_w, ffn_conv_b, ffn_w_down):
    b, s, d = x.shape
    s_ctx = ctx.shape[1]
    depth = ada_w.shape[0]
    assert d == D_MODEL and b % 2 == 0 and b + 1 <= MOD_ROWS and s % 1024 == 0 and s_ctx % SUBLANES == 0

    cvec = jnp.concatenate([c, c_ctx[None, :], jnp.zeros((MOD_ROWS - b - 1, d), F32)], axis=0)
    mods = _mods(cvec, ada_w, ada_b)
    rope = [jnp.asarray(t) for t in _rope_tables(s)]
    rope_ctx = [jnp.asarray(t) for t in _identity_rope_tables(s_ctx)]
    bd = jnp.asarray(_head_block_diag())
    xc = ctx
    weights = _weights_bf16([mix_w_in, mix_w_out, ffn_w_up, ffn_w_down])

    for i in range(depth):
        last = i == depth - 1
        j = i // 2
        lat = [mods[i, :b, k * d:(k + 1) * d][:, None, :] for k in range(6)]
        cx = [jnp.broadcast_to(mods[i, b, k * d:(k + 1) * d][None, None, :], (b, 1, d)) for k in range(6)]
        w_in, w_out, w_up, w_down = (w[i] for w in weights)
        g_mix = norm_mix[i].reshape(1, d)
        g_ffn = norm_ffn[i].reshape(1, d)
        qg = jnp.tile(attn_q_norm[i], N_Q_HEADS).reshape(1, Q_W)
        kg = jnp.tile(attn_k_norm[i], N_KV_HEADS).reshape(1, KV_W)

        conv = (hy_conv_w[j], hy_conv_b[j]) if i % 2 == 0 else (sc_conv_w[j],)
        q, k, vt, *mix = _inproj(x, lat[0], lat[1], g_mix, w_in, qg, kg, bd, *rope, *conv)
        qc, kc, vct, *mixc = _inproj(xc, cx[0], cx[1], g_mix, w_in, qg, kg, bd, *rope_ctx, *conv)

        bound = _score_bound(attn_q_norm[i], attn_k_norm[i])
        if i % 2 == 0:
            fargs = (hy_w1[j], hy_b1[j], hy_w2[j], hy_b2[j], hy_w3[j], hy_b3[j], hy_w4[j], hy_freq[j])
            o_attn = _dense_attention(q, k, vt, kc, vct, bound=bound)
            kf, ks = _implicit_filter(s, *fargs)
            o_conv = _long_conv_mixer(*mix, kf, ks, hy_bias_d[j])
            if not last:
                oc_attn = _dense_attention(qc, kc, vct, bound=bound)
                kfc, ksc = _implicit_filter(s_ctx, *fargs)
                oc_conv = _small_conv_mixer(*mixc, kfc, ksc, hy_bias_d[j])
        else:
            sink = swa_sink[j].reshape(N_KV_HEADS, GROUP)
            o_attn = _banded_attention(q, k, vt, kc, vct, sink, bound)
            o_conv = mix[0]
            if not last:
                oc_attn = _dense_attention(qc, kc, vct, sink=sink)
                oc_conv = mixc[0]

        ffn = (g_ffn, w_up, ffn_conv_w[i], ffn_conv_b[i], w_down)
        x = _mixer_out_ffn(x, o_attn, o_conv, w_out, lat[2], lat[3], lat[4], lat[5], *ffn)
        if not last:
            xc = _mixer_out_ffn(xc, oc_attn, oc_conv, w_out, cx[2], cx[3], cx[4], cx[5], *ffn)
    return x
```

```python
import functools
import math

import numpy as np
import jax
import jax.numpy as jnp
from jax import lax
from jax.experimental import pallas as pl
from jax.experimental.pallas import tpu as pltpu

F32 = jnp.float32
BF16 = jnp.bfloat16
HI = lax.Precision.HIGHEST

D_MODEL = 1024
GRID_W = 64
HEAD_DIM = 64
N_Q_HEADS = 8
N_KV_HEADS = 4
GROUP = N_Q_HEADS // N_KV_HEADS
Q_W = N_Q_HEADS * HEAD_DIM
KV_W = N_KV_HEADS * HEAD_DIM
QKV_W = Q_W + 2 * KV_W
CONV_W = D_MODEL // 2
MIX_IN_W = QKV_W + 3 * CONV_W
WINDOW = 128
ROPE_THETA = 10000.0
FILT_EMB = 33
FILT_WIDTH = 64
DECAY_TARGET = 1e-2
FAST_DECAY_PCT = 0.3
SLOW_DECAY_PCT = 1.5
DECAY_SHIFT = 0.05
D_FF = 2816
NEG_INF = -1e30
RMS_EPS = 1e-6
LOG2E = 1.4426950408889634
Q_SCALE = HEAD_DIM ** -0.5 * LOG2E

LANES = 128
SUBLANES = 8
DFT_N1 = 128
VMEM_LIMIT_MB = 56
VT_ROWS = LANES
MOD_ROWS = SUBLANES
PROJ_SUB_ROWS = 128


def _params(sem, vmem_mb=VMEM_LIMIT_MB):
    return pltpu.CompilerParams(dimension_semantics=sem, vmem_limit_bytes=vmem_mb * 1024 * 1024)


def _full(shape):
    nd = len(shape)
    return pl.BlockSpec(shape, lambda *_: (0,) * nd)


def _mods_kernel(c_ref, w_ref, b_ref, o_ref):
    c = c_ref[...]
    s = c / (1.0 + jnp.exp(-c))
    o_ref[0] = _dot_bf16x3(s, w_ref[0]) + b_ref[0]


def _mods(cvec, ada_w, ada_b):
    depth, d, n6 = ada_w.shape
    tn = 1536
    return pl.pallas_call(
        _mods_kernel,
        grid=(depth, n6 // tn),
        in_specs=[_full((MOD_ROWS, d)),
                  pl.BlockSpec((1, d, tn), lambda l, j: (l, 0, j)),
                  pl.BlockSpec((1, 1, tn), lambda l, j: (l, 0, j))],
        out_specs=pl.BlockSpec((1, MOD_ROWS, tn), lambda l, j: (l, 0, j)),
        out_shape=jax.ShapeDtypeStruct((depth, MOD_ROWS, n6), F32),
        compiler_params=_params(("arbitrary", "arbitrary")),
        name="ada_mods",
    )(cvec, ada_w, ada_b.reshape(depth, 1, n6))


def _modulated_norm(x, gain, shift, scale):
    ms = jnp.mean(x * x, axis=-1, keepdims=True)
    return (x * lax.rsqrt(ms + RMS_EPS)) * gain * (1.0 + scale) + shift


WEIGHT_CAST_STEPS = 8


def _cast_kernel(*refs):
    n = len(refs) // 2
    for w_ref, o_ref in zip(refs[:n], refs[n:]):
        o_ref[...] = w_ref[0].astype(o_ref.dtype)


def _weights_bf16(stacked):
    depth = stacked[0].shape[0]
    in_specs, out_specs, out_shape, args = [], [], [], []
    for w in stacked:
        _, rows, cols = w.shape
        tr = rows // WEIGHT_CAST_STEPS
        for layer in range(depth):
            in_specs.append(pl.BlockSpec((1, tr, cols), lambda r, layer=layer: (layer, r, 0)))
            out_specs.append(pl.BlockSpec((tr, cols), lambda r: (r, 0)))
            out_shape.append(jax.ShapeDtypeStruct((rows, cols), BF16))
            args.append(w)
    outs = pl.pallas_call(
        _cast_kernel,
        grid=(WEIGHT_CAST_STEPS,),
        in_specs=in_specs,
        out_specs=out_specs,
        out_shape=out_shape,
        compiler_params=_params(("arbitrary",)),
        name="weights_bf16",
    )(*args)
    return [outs[k * depth:(k + 1) * depth] for k in range(len(stacked))]


def _halo_specs(tm, t, width, rows=SUBLANES):
    nb = tm // rows
    last = t // rows - 1
    return [pl.BlockSpec((1, tm, width), lambda b, i: (b, i, 0)),
            pl.BlockSpec((1, rows, width), lambda b, i: (b, jnp.maximum(i * nb - 1, 0), 0)),
            pl.BlockSpec((1, rows, width), lambda b, i: (b, jnp.minimum((i + 1) * nb, last), 0))]


def _inproj_kernel(*refs, mode):
    (x_ref, xp_ref, xn_ref, sh_ref, sc_ref, g_ref, w_ref, qg_ref, kg_ref, bd_ref,
     cos_ref, sa_ref, sb_ref, cw_ref) = refs[:14]
    if mode == "hyena":
        cb_ref, q_ref, k_ref, vt_ref, u_ref, x0_ref = refs[14:]
    else:
        q_ref, k_ref, vt_ref, oc_ref = refs[14:]
    i = pl.program_id(1)
    nt = pl.num_programs(1)
    tm = x_ref.shape[1]
    sub = min(PROJ_SUB_ROWS, tm)
    gain, shift, scale = g_ref[...], sh_ref[0], sc_ref[0]

    cw = cw_ref[...]

    def conv_stage(ext, rows):
        n = sub + 2 * SUBLANES
        mid = slice(SUBLANES, sub + SUBLANES)
        if mode == "hyena":
            c = (pltpu.roll(ext, 1, 0)[mid] * cw[0:1] + ext[mid] * cw[1:2] + pltpu.roll(ext, n - 1, 0)[mid] * cw[2:3]
                 + cb_ref[...])
            x0_ref[0, rows] = c[:, :CONV_W]
            u_ref[0, rows] = c[:, 2 * CONV_W:] * c[:, CONV_W:2 * CONV_W]
        else:
            pr = ext[:, CONV_W:2 * CONV_W] * ext[:, 2 * CONV_W:]
            conv = pltpu.roll(pr, 1, 0)[mid] * cw[0:1] + pr[mid] * cw[1:2] + pltpu.roll(pr, n - 1, 0)[mid] * cw[2:3]
            oc_ref[0, rows] = (ext[mid, :CONV_W] * conv).astype(oc_ref.dtype)

    def head_norm_rope(t, hgain, bd, cos, sa, sb):
        sq = (t * t).astype(BF16)
        ssq = jnp.concatenate([jnp.dot(sq[:, c:c + MXU_DIM], bd[:MXU_DIM, :MXU_DIM], preferred_element_type=F32)
                               for c in range(0, t.shape[1], MXU_DIM)], axis=1)
        t = t * lax.rsqrt(ssq * (1.0 / HEAD_DIM) + RMS_EPS) * hgain
        outs = []
        for j in range(t.shape[1] // LANES):
            tj = t[:, j * LANES:(j + 1) * LANES]
            outs.append(tj * cos + pltpu.roll(tj, LANES - 16, 1) * sa + pltpu.roll(tj, 16, 1) * sb)
        return outs

    held = None
    for r0 in range(0, tm, sub):
        rows = slice(r0, r0 + sub)
        h = _modulated_norm(x_ref[0, rows], gain, shift, scale)
        z = jnp.dot(h.astype(BF16), w_ref[:, :QKV_W], preferred_element_type=F32)
        first, last = r0 == 0, r0 + sub == tm
        hc = jnp.concatenate(([_modulated_norm(xp_ref[0], gain, shift, scale)] if first else []) + [h]
                             + ([_modulated_norm(xn_ref[0], gain, shift, scale)] if last else []), axis=0)
        zc = jnp.dot(hc.astype(BF16), w_ref[:, QKV_W:], preferred_element_type=F32)
        if first:
            before = jnp.where(i > 0, zc[:SUBLANES], 0.0)
            zc = zc[SUBLANES:]
        if last:
            after = jnp.where(i < nt - 1, zc[sub:], 0.0)
            zc = zc[:sub]
        if held is not None:
            conv_stage(jnp.concatenate([before, held[0], zc[:SUBLANES]], axis=0), held[1])
            before = held[0][-SUBLANES:]
        held = (zc, rows)
        tabs = (cos_ref[rows], sa_ref[rows], sb_ref[rows])
        qs = head_norm_rope(z[:, :Q_W], qg_ref[...], bd_ref[...], *tabs)
        ks = head_norm_rope(z[:, Q_W:Q_W + KV_W], kg_ref[...], bd_ref[:KV_W, :KV_W], *tabs)
        for j, qj in enumerate(qs):
            qj = (qj * Q_SCALE).astype(BF16)
            q_ref[0, 2 * j, rows] = qj[:, :HEAD_DIM]
            q_ref[0, 2 * j + 1, rows] = qj[:, HEAD_DIM:]
        for j, kj in enumerate(ks):
            kj = kj.astype(BF16)
            k_ref[0, 2 * j, rows] = kj[:, :HEAD_DIM]
            k_ref[0, 2 * j + 1, rows] = kj[:, HEAD_DIM:]
        vt = z[:, Q_W + KV_W:QKV_W].T
        ones_row = (lax.broadcasted_iota(jnp.int32, (VT_ROWS - HEAD_DIM, sub), 0) == 0).astype(F32)
        for hh in range(N_KV_HEADS):
            vt_ref[0, hh, :, rows] = jnp.concatenate([vt[hh * HEAD_DIM:(hh + 1) * HEAD_DIM], ones_row],
                                                     axis=0).astype(BF16)

    conv_stage(jnp.concatenate([before, held[0], after], axis=0), held[1])


def _inproj(x, sh, sc, gain, w_bf, qg, kg, bd, cos, sa, sb, conv_w, conv_b=None):
    bx, t, d = x.shape
    tm = min(1024, t)
    mode = "hyena" if conv_b is not None else "short"
    vec = pl.BlockSpec((1, 1, d), lambda b, i: (b, 0, 0))
    tab = pl.BlockSpec((tm, LANES), lambda b, i: (i, 0))
    row = lambda w: pl.BlockSpec((1, tm, w), lambda b, i: (b, i, 0))
    in_specs = _halo_specs(tm, t, d) + [vec, vec, _full((1, d)), _full((d, MIX_IN_W)), _full((1, Q_W)),
                                        _full((1, KV_W)), _full((Q_W, Q_W)), tab, tab, tab, _full(conv_w.shape)]
    args = [x, x, x, sh, sc, gain, w_bf, qg, kg, bd, cos, sa, sb, conv_w]
    out_specs = [pl.BlockSpec((1, N_Q_HEADS, tm, HEAD_DIM), lambda b, i: (b, 0, i, 0)),
                 pl.BlockSpec((1, N_KV_HEADS, tm, HEAD_DIM), lambda b, i: (b, 0, i, 0)),
                 pl.BlockSpec((1, N_KV_HEADS, VT_ROWS, tm), lambda b, i: (b, 0, 0, i))]
    out_shape = [jax.ShapeDtypeStruct((bx, N_Q_HEADS, t, HEAD_DIM), BF16),
                 jax.ShapeDtypeStruct((bx, N_KV_HEADS, t, HEAD_DIM), BF16),
                 jax.ShapeDtypeStruct((bx, N_KV_HEADS, VT_ROWS, t), BF16)]
    if mode == "hyena":
        in_specs.append(_full((1, 3 * CONV_W)))
        args.append(conv_b.reshape(1, 3 * CONV_W))
        out_specs += [row(CONV_W), row(CONV_W)]
        out_shape += [jax.ShapeDtypeStruct((bx, t, CONV_W), F32)] * 2
    else:
        out_specs.append(row(CONV_W))
        out_shape.append(jax.ShapeDtypeStruct((bx, t, CONV_W), BF16))
    return pl.pallas_call(
        functools.partial(_inproj_kernel, mode=mode),
        grid=(bx, t // tm),
        in_specs=in_specs,
        out_specs=out_specs,
        out_shape=out_shape,
        compiler_params=_params(("arbitrary", "arbitrary")),
        name="in_proj",
    )(*args)


def _score_bound(q_gain, k_gain):
    b = HEAD_DIM * Q_SCALE * BOUND_MARGIN * jnp.max(jnp.abs(q_gain)) * jnp.max(jnp.abs(k_gain))
    return b.reshape(1, 1).astype(F32)


def _scores_t(ks, q):
    return lax.dot_general(ks, q, (((1,), (1,)), ((), ())), preferred_element_type=F32)


def _sink_column(sink_ref, hh, tq, m, acc):
    col = lax.broadcasted_iota(jnp.int32, m.shape, 1)
    sink = jnp.where(col < tq, sink_ref[hh, 0], sink_ref[hh, 1]) * LOG2E
    m_new = jnp.maximum(m, sink)
    den_row = lax.broadcasted_iota(jnp.int32, acc.shape, 0) == HEAD_DIM
    return acc * jnp.exp2(m - m_new) + jnp.where(den_row, jnp.exp2(sink - m_new), 0.0)


def _attn_finish(acc, tq):
    o = (acc / acc[HEAD_DIM:HEAD_DIM + 1]).T
    return jnp.concatenate([o[:tq, :HEAD_DIM], o[tq:, :HEAD_DIM]], axis=1)


EXP_CHUNK_ELEMS = 32 * SUBLANES * LANES


def _score_stage(ks, q, s_ref, bias=None):
    s = _scores_t(ks, q)
    if bias is not None:
        s = s + bias
    s_ref[0:s.shape[0]] = s
    return jnp.max(s, axis=0, keepdims=True)


def _softmax_stage(s_ref, p_ref, n, vt, mx, m, acc_ref):
    m_new = jnp.maximum(m, mx)
    rows = EXP_CHUNK_ELEMS // m.shape[1]
    for c in range(0, n, rows):
        p_ref[c:c + rows] = jnp.exp2(s_ref[c:c + rows] - m_new).astype(BF16)
    acc_ref[...] = acc_ref[...] * jnp.exp2(m - m_new) + jnp.dot(vt, p_ref[0:n], preferred_element_type=F32)
    return m_new


BOUNDED_TILES = 8
BOUND_MARGIN = 1.01
SAFE_SHIFT = 40.0


def _flash_exact(q, k_tile, vt_tile, ctx, sa_ref, sb_ref, p_ref, acc_ref, *, tk, n_main):
    pa_ref, pb_ref = p_ref.at[0:tk], p_ref.at[tk:2 * tk]
    acc_ref[...] = jnp.zeros_like(acc_ref)
    m = jnp.full((1, q.shape[0]), NEG_INF, F32)
    mx_a = _score_stage(k_tile(0), q, sa_ref)
    if n_main > 1:
        def body(i, carry):
            m, mx_a = carry
            mx_b = _score_stage(k_tile(2 * i + 1), q, sb_ref)
            m = _softmax_stage(sa_ref, pa_ref, tk, vt_tile(2 * i), mx_a, m, acc_ref)
            mx_a = _score_stage(k_tile(2 * i + 2), q, sa_ref)
            m = _softmax_stage(sb_ref, pb_ref, tk, vt_tile(2 * i + 1), mx_b, m, acc_ref)
            return m, mx_a

        m, mx_a = lax.fori_loop(0, n_main // 2 - 1, body, (m, mx_a))
        mx_b = _score_stage(k_tile(n_main - 1), q, sb_ref)
        m = _softmax_stage(sa_ref, pa_ref, tk, vt_tile(n_main - 2), mx_a, m, acc_ref)
        last = (sb_ref, pb_ref, mx_b)
        spare = (sa_ref, pa_ref)
    else:
        last = (sa_ref, pa_ref, mx_a)
        spare = (sb_ref, pb_ref)
    if ctx is not None:
        mx_c = _score_stage(ctx[0], q, spare[0])
    m = _softmax_stage(last[0], last[1], tk, vt_tile(n_main - 1), last[2], m, acc_ref)
    if ctx is not None:
        m = _softmax_stage(spare[0], spare[1], ctx[0].shape[0], ctx[1], mx_c, m, acc_ref)
    return m


def _exp2_as_bf16(s):
    return jnp.exp2(s).astype(BF16)


def _flash_bounded(q, k_ref, vt_ref, ctx, p_ref, acc_ref, *, tk, n_main):
    step = BOUNDED_TILES * tk if n_main % BOUNDED_TILES == 0 else tk

    def group(j, with_ctx=False):
        off = pl.multiple_of(j * step, step)
        for c in range(0, step, tk):
            p_ref[c:c + tk] = _exp2_as_bf16(_scores_t(k_ref[0, 0, pl.ds(off + c, tk), :], q))
        vt, n = vt_ref[0, 0, :, pl.ds(off, step)], step
        if with_ctx:
            n = step + ctx[0].shape[0]
            p_ref[step:n] = _exp2_as_bf16(_scores_t(ctx[0], q))
            vt = jnp.concatenate([vt, ctx[1]], axis=1)
        return jnp.dot(vt, p_ref[0:n], preferred_element_type=F32)

    acc_ref[...] = group(0, with_ctx=ctx is not None)

    def body(j, carry):
        acc_ref[...] += group(j)
        return carry

    lax.fori_loop(1, n_main * tk // step, body, 0)


def _flash_kernel(*refs, tq, tk, n_main, has_ctx, has_sink):
    sa_ref, sb_ref, p_ref, acc_ref = refs[-4:]
    o_ref = refs[-5]
    refs = list(refs[:-5])
    sink_ref = refs.pop(0) if has_sink else None
    bound_ref = None if has_sink else refs.pop(0)
    q_ref, k_ref, vt_ref = refs[:3]
    m_cols = GROUP * tq
    q = q_ref[0].reshape(m_cols, HEAD_DIM)
    ctx = (refs[3][0, 0], refs[4][0, 0]) if has_ctx else None

    def k_tile(j):
        return k_ref[0, 0, pl.ds(pl.multiple_of(j * tk, tk), tk), :]

    def vt_tile(j):
        return vt_ref[0, 0, :, pl.ds(pl.multiple_of(j * tk, tk), tk)]

    exact = functools.partial(_flash_exact, q, k_tile, vt_tile, ctx, sa_ref, sb_ref, p_ref, acc_ref,
                              tk=tk, n_main=n_main)
    if has_sink:
        acc = _sink_column(sink_ref, pl.program_id(1), tq, exact(), acc_ref[...])
    else:
        bound = bound_ref[0, 0]
        safe = bound <= SAFE_SHIFT

        @pl.when(safe)
        def _():
            _flash_bounded(q, k_ref, vt_ref, ctx, p_ref, acc_ref, tk=tk, n_main=n_main)

        @pl.when(jnp.logical_not(safe))
        def _():
            exact()

        acc = acc_ref[...]
    o_ref[0] = _attn_finish(acc, tq).astype(o_ref.dtype)


def _kv_specs(n):
    return [pl.BlockSpec((1, 1, n, HEAD_DIM), lambda b, h, i: (b, h, 0, 0)),
            pl.BlockSpec((1, 1, VT_ROWS, n), lambda b, h, i: (b, h, 0, 0))]


def _dense_attention(q, k, vt, kc=None, vct=None, *, bound=None, sink=None):
    bx, _, t, _ = q.shape
    s = k.shape[2]
    tq = min(1024, t)
    tk = min(512, s)
    has_ctx = kc is not None
    has_sink = sink is not None
    in_specs = [pl.BlockSpec((1, GROUP, tq, HEAD_DIM), lambda b, h, i: (b, h, i, 0))] + _kv_specs(s)
    args = [q, k, vt]
    if has_ctx:
        in_specs += _kv_specs(kc.shape[2])
        args += [kc, vct]
    in_specs = [pl.BlockSpec(memory_space=pltpu.SMEM)] + in_specs
    args = [sink if has_sink else bound] + args
    n_main = s // tk
    assert n_main == 1 or n_main % 2 == 0
    n_ctx = kc.shape[2] if has_ctx else 0
    assert n_ctx <= tk
    m_cols = GROUP * tq
    return pl.pallas_call(
        functools.partial(_flash_kernel, tq=tq, tk=tk, n_main=n_main, has_ctx=has_ctx, has_sink=has_sink),
        grid=(bx, N_KV_HEADS, t // tq),
        in_specs=in_specs,
        out_specs=pl.BlockSpec((1, tq, LANES), lambda b, h, i: (b, i, h)),
        out_shape=jax.ShapeDtypeStruct((bx, t, Q_W), BF16),
        scratch_shapes=[pltpu.VMEM((tk, m_cols), F32), pltpu.VMEM((tk, m_cols), F32),
                        pltpu.VMEM((BOUNDED_TILES * tk + n_ctx, m_cols), BF16), pltpu.VMEM((VT_ROWS, m_cols), F32)],
        compiler_params=_params(("arbitrary", "arbitrary", "arbitrary")),
        name="dense_attention",
    )(*args)


BAND_SUB = 256


@functools.lru_cache(maxsize=None)
def _band_bias(n_ctx):
    span = BAND_SUB + 2 * WINDOW
    kr = np.arange(span)[None, :, None]
    qc = (np.arange(GROUP * BAND_SUB) % BAND_SUB)[None, None, :]
    rel = np.arange(3)[:, None, None]
    band = np.where(np.abs(kr - qc - rel * WINDOW) <= WINDOW, 0.0, NEG_INF)
    return np.concatenate([band, np.zeros((3, n_ctx, GROUP * BAND_SUB))], axis=1).astype(np.float32)


def _banded_kernel(sink_ref, bound_ref, q_ref, k_ref, vt_ref, kc_ref, vct_ref, bias_ref, o_ref,
                   sl_ref, sc_ref, p_ref, acc_ref, *, n_sub, s_len):
    i = pl.program_id(2)
    hh = pl.program_id(1)
    sub = BAND_SUB
    m_cols = GROUP * sub
    span = sub + 2 * WINDOW
    n_ctx = kc_ref.shape[2]

    def window(u):
        q = q_ref[0, :, u * sub:(u + 1) * sub, :].reshape(m_cols, HEAD_DIM)
        q0 = (i * n_sub + u) * sub
        start = pl.multiple_of(jnp.clip(q0 - WINDOW, 0, s_len - span), WINDOW)
        return q, start, bias_ref[(q0 - start) // WINDOW]

    shift = jnp.maximum(bound_ref[0, 0], jnp.maximum(sink_ref[hh, 0], sink_ref[hh, 1]) * LOG2E)
    safe = shift <= SAFE_SHIFT

    @pl.when(safe)
    def _():
        col = lax.broadcasted_iota(jnp.int32, (1, m_cols), 1)
        sink = jnp.where(col < sub, sink_ref[hh, 0], sink_ref[hh, 1]) * LOG2E
        den_row = lax.broadcasted_iota(jnp.int32, (VT_ROWS, m_cols), 0) == HEAD_DIM
        sink_den = jnp.where(den_row, jnp.exp2(sink), 0.0)
        for u in range(n_sub):
            q, start, bias = window(u)
            keys = jnp.concatenate([k_ref[0, 0, pl.ds(start, span), :], kc_ref[0, 0]], axis=0)
            vt = jnp.concatenate([vt_ref[0, 0, :, pl.ds(start, span)], vct_ref[0, 0]], axis=1)
            p_u = p_ref.at[u]
            p_u[...] = _exp2_as_bf16(_scores_t(keys, q) + bias)
            acc = jnp.dot(vt, p_u[...], preferred_element_type=F32) + sink_den
            o_ref[0, u * sub:(u + 1) * sub] = _attn_finish(acc, sub).astype(o_ref.dtype)

    @pl.when(jnp.logical_not(safe))
    def _():
        stats = []
        for u in range(n_sub):
            q, start, bias = window(u)
            mx_l = _score_stage(k_ref[0, 0, pl.ds(start, span), :], q, sl_ref.at[u], bias=bias[:span])
            mx_c = _score_stage(kc_ref[0, 0], q, sc_ref.at[u])
            stats.append((start, mx_l, mx_c))
        for u in range(n_sub):
            start, mx_l, mx_c = stats[u]
            acc_u = acc_ref.at[u]
            acc_u[...] = jnp.zeros((VT_ROWS, m_cols), F32)
            m = jnp.full((1, m_cols), NEG_INF, F32)
            p_l, p_c = p_ref.at[u, 0:span], p_ref.at[u, span:span + n_ctx]
            m = _softmax_stage(sl_ref.at[u], p_l, span, vt_ref[0, 0, :, pl.ds(start, span)], mx_l, m, acc_u)
            m = _softmax_stage(sc_ref.at[u], p_c, n_ctx, vct_ref[0, 0], mx_c, m, acc_u)
            acc = _sink_column(sink_ref, hh, sub, m, acc_u[...])
            o_ref[0, u * sub:(u + 1) * sub] = _attn_finish(acc, sub).astype(o_ref.dtype)


def _banded_attention(q, k, vt, kc, vct, sink, bound):
    bx, _, t, _ = q.shape
    n_ctx = kc.shape[2]
    n_sub = next(k for k in (8, 4, 2, 1) if t % (k * BAND_SUB) == 0)
    tq = n_sub * BAND_SUB
    m_cols = GROUP * BAND_SUB
    span = BAND_SUB + 2 * WINDOW
    bias = jnp.asarray(_band_bias(n_ctx))
    return pl.pallas_call(
        functools.partial(_banded_kernel, n_sub=n_sub, s_len=t),
        grid=(bx, N_KV_HEADS, t // tq),
        in_specs=[pl.BlockSpec(memory_space=pltpu.SMEM), pl.BlockSpec(memory_space=pltpu.SMEM),
                  pl.BlockSpec((1, GROUP, tq, HEAD_DIM), lambda b, h, i: (b, h, i, 0))]
                 + _kv_specs(t) + _kv_specs(n_ctx) + [_full(bias.shape)],
        out_specs=pl.BlockSpec((1, tq, LANES), lambda b, h, i: (b, i, h)),
        out_shape=jax.ShapeDtypeStruct((bx, t, Q_W), BF16),
        scratch_shapes=[pltpu.VMEM((n_sub, span, m_cols), F32), pltpu.VMEM((n_sub, n_ctx, m_cols), F32),
                        pltpu.VMEM((n_sub, span + n_ctx, m_cols), BF16), pltpu.VMEM((n_sub, VT_ROWS, m_cols), F32)],
        compiler_params=_params(("arbitrary", "arbitrary", "arbitrary")),
        name="banded_attention",
    )(sink, bound, q, k, vt, kc, vct, bias)


@functools.lru_cache(maxsize=None)
def _filter_features(n):
    j = np.arange(2 * n)
    d = np.where(j <= n, j, 2 * n - j)
    d = np.where(j == n, 0, d)
    bands = (FILT_EMB - 1) // 2
    t01 = np.linspace(0.0, 1.0, n)[d]
    w = 2.0 * np.pi * d.astype(np.float64) / n
    f = np.linspace(1e-4, bands - 1, bands)[None, :]
    feats = np.zeros((2 * n, LANES), np.float64)
    feats[:, 0] = t01
    feats[:, 1:1 + bands] = np.cos(f * w[:, None])
    feats[:, 1 + bands:FILT_EMB] = -np.sin(f * w[:, None])
    feats[:, 64] = t01
    feats[:, 65] = (j < n)
    feats[:, 66] = (j != n)
    return feats.astype(np.float32)


def _dot_bf16x3(a, b):
    a_hi = a.astype(BF16)
    a_lo = (a - a_hi.astype(F32)).astype(BF16)
    b_hi = b.astype(BF16)
    b_lo = (b - b_hi.astype(F32)).astype(BF16)
    dot = lambda x, y: jnp.dot(x, y, preferred_element_type=F32)
    return dot(a_hi, b_hi) + dot(a_hi, b_lo) + dot(a_lo, b_hi)


def _filter_kernel(f_ref, w1_ref, b1_ref, w2_ref, b2_ref, w3_ref, b3_ref, w4_ref, fr_ref, dl_ref, k_ref, s_ref):
    f = f_ref[...]
    fr = fr_ref[...]
    mm = lambda a, b: jnp.dot(a, b, preferred_element_type=F32, precision=HI)
    h = jnp.sin(fr * (mm(w1_ref[...], f) + b1_ref[...]))
    h = jnp.sin(fr * (mm(w2_ref[...], h) + b2_ref[...]))
    h = jnp.sin(fr * (mm(w3_ref[...], h) + b3_ref[...]))
    hf = _dot_bf16x3(w4_ref[...], h)
    win = jnp.exp(-dl_ref[...] * f[64:65]) + DECAY_SHIFT
    k = (jnp.where(f[65:66] > 0.5, hf[:CONV_W], hf[CONV_W:]) * win * f[66:67]).T
    k_ref[...] = k

    @pl.when(pl.program_id(0) == 0)
    def _():
        s_ref[...] = jnp.zeros_like(s_ref)

    s_ref[...] += jnp.sum(jnp.abs(k), axis=0, keepdims=True)


def _implicit_filter(n, w1, b1, w2, b2, w3, b3, w4, freq):
    feats_t = jnp.asarray(np.ascontiguousarray(_filter_features(n).T))
    col = lambda a: a.reshape(-1, 1)
    w1t = jnp.pad(w1.T, ((0, 0), (0, LANES - FILT_EMB)))
    deltas = np.abs(np.linspace(math.log(DECAY_TARGET) / SLOW_DECAY_PCT, math.log(DECAY_TARGET) / FAST_DECAY_PCT,
                                CONV_W)).astype(np.float32).reshape(CONV_W, 1)
    tr = min(1024, 2 * n)
    sq = _full((FILT_WIDTH, FILT_WIDTH))
    vec = _full((FILT_WIDTH, 1))
    return pl.pallas_call(
        _filter_kernel,
        grid=(2 * n // tr,),
        in_specs=[pl.BlockSpec((LANES, tr), lambda i: (0, i)), _full((FILT_WIDTH, LANES)), vec, sq, vec, sq, vec,
                  _full((2 * CONV_W, FILT_WIDTH)), vec, _full((CONV_W, 1))],
        out_specs=[pl.BlockSpec((tr, CONV_W), lambda i: (i, 0)), _full((1, CONV_W))],
        out_shape=[jax.ShapeDtypeStruct((2 * n, CONV_W), F32), jax.ShapeDtypeStruct((1, CONV_W), F32)],
        compiler_params=_params(("arbitrary",)),
        name="hyena_filter",
    )(feats_t, w1t, col(b1), w2.T, col(b2), w3.T, col(b3), w4.T, col(freq), jnp.asarray(deltas))


def _twiddle(idx, mod):
    ang = 2.0 * np.pi * (idx % mod) / mod
    return np.cos(ang), -np.sin(ang)


def _real_form(mr, mi):
    return np.concatenate([np.concatenate([mr, -mi], -1), np.concatenate([mi, mr], -1)], -2)


@functools.lru_cache(maxsize=None)
def _dft_tables(n2):
    n1 = DFT_N1
    n = n1 * n2
    h = n2 // 2
    a2 = np.arange(n2)
    fr, fi = _twiddle(np.outer(a2, a2), n2)
    m_data = _real_form(fr[:, :h], fi[:, :h])
    m_filt = np.concatenate([fr, fi], 0)
    k2 = a2[:, None, None]
    k1 = np.arange(n1)[None, :, None]
    c1 = np.arange(n1)[None, None, :]
    g = _real_form(*_twiddle(c1 * (n2 * k1 + k2), n))
    a1 = np.arange(n1)
    f1 = _real_form(*_twiddle(np.outer(a1, a1), n1))
    t2 = a1[:, None, None]
    t1 = np.arange(h)[None, :, None]
    j1 = a2[None, None, :]
    hh = _real_form(*_twiddle(j1 * (n1 * t1 + t2), n))
    cast = lambda m: np.asarray(m, dtype=BF16)
    return cast(m_data), cast(m_filt), cast(g), cast(f1), cast(hh)


DFT_ROW_CHUNK = 2 * SUBLANES


def _dft_rows_kernel(m_ref, *refs):
    o_ref = refs[-1]
    cols = [jnp.concatenate([r[0, :, t, :] for r in refs[:-1]], axis=0) for t in range(DFT_ROW_CHUNK)]
    rhs = jnp.concatenate(cols, axis=1).astype(BF16)
    out = jnp.dot(m_ref[...], rhs, preferred_element_type=F32)
    half = out.shape[0] // 2
    o_ref[0, 0] = out[:half].astype(o_ref.dtype)
    o_ref[0, 1] = out[half:].astype(o_ref.dtype)


def _dft_stage1(mat, views, pairs, rows, n2, c):
    nin = len(views)
    lc = DFT_ROW_CHUNK * c
    in_specs = [_full(mat.shape)] + [
        pl.BlockSpec((1, rows, DFT_ROW_CHUNK, c), (lambda p, j, a=a: (nin * p + a, 0, j, 0))) for a in range(nin)]
    return pl.pallas_call(
        _dft_rows_kernel,
        grid=(pairs, DFT_N1 // DFT_ROW_CHUNK),
        in_specs=in_specs,
        out_specs=pl.BlockSpec((1, 2, n2, lc), lambda p, j: (p, 0, 0, j)),
        out_shape=jax.ShapeDtypeStruct((pairs, 2, n2, DFT_N1 * c), BF16),
        compiler_params=_params(("arbitrary", "arbitrary")),
        name="dft_stage1",
    )(mat, *views)


def _spectrum_kernel(a_ref, g_ref, sc_ref, o_ref, *, kb):
    for j in range(kb):
        rhs = jnp.concatenate([a_ref[0, 0, j], a_ref[0, 1, j]], axis=0)
        x = jnp.dot(g_ref[j], rhs, preferred_element_type=F32)
        o_ref[0, j] = (x[:DFT_N1] * sc_ref[...]).astype(o_ref.dtype)
        o_ref[1, j] = (x[DFT_N1:] * sc_ref[...]).astype(o_ref.dtype)


def _filter_spectrum(a, g, scale, n2, kb):
    c = a.shape[-1]
    return pl.pallas_call(
        functools.partial(_spectrum_kernel, kb=kb),
        grid=(n2 // kb,),
        in_specs=[pl.BlockSpec((1, 2, kb, DFT_N1, c), lambda k: (0, 0, k, 0, 0)),
                  pl.BlockSpec((kb, 2 * DFT_N1, 2 * DFT_N1), lambda k: (k, 0, 0)), _full((1, c))],
        out_specs=pl.BlockSpec((2, kb, DFT_N1, c), lambda k: (0, k, 0, 0)),
        out_shape=jax.ShapeDtypeStruct((2, n2, DFT_N1, c), BF16),
        compiler_params=_params(("arbitrary",)),
        name="filter_spectrum",
    )(a, g, scale)


def _dft_mid_kernel(a_ref, g_ref, kh_ref, f_ref, o_ref, *, kb):
    for j in range(kb):
        rhs = jnp.concatenate([a_ref[0, 0, j], a_ref[0, 1, j]], axis=0)
        x = jnp.dot(g_ref[j], rhs, preferred_element_type=F32)
        xr, xi = x[:DFT_N1], x[DFT_N1:]
        kr, ki = kh_ref[0, j].astype(F32), kh_ref[1, j].astype(F32)
        yr = xr * kr - xi * ki
        yi = xr * ki + xi * kr
        v = jnp.concatenate([yr, -yi], axis=0).astype(BF16)
        b = jnp.dot(f_ref[...], v, preferred_element_type=F32)
        o_ref[0, 0, :, j, :] = b[:DFT_N1]
        o_ref[0, 1, :, j, :] = b[DFT_N1:]


def _dft_mid(a, g, khat, f1, pairs, n2, kb):
    c = a.shape[-1]
    blk = pl.BlockSpec((1, 2, kb, DFT_N1, c), lambda k, p: (p, 0, k, 0, 0))
    return pl.pallas_call(
        functools.partial(_dft_mid_kernel, kb=kb),
        grid=(n2 // kb, pairs),
        in_specs=[blk, pl.BlockSpec((kb, 2 * DFT_N1, 2 * DFT_N1), lambda k, p: (k, 0, 0)),
                  pl.BlockSpec((2, kb, DFT_N1, c), lambda k, p: (0, k, 0, 0)), _full(f1.shape)],
        out_specs=pl.BlockSpec((1, 2, DFT_N1, kb, c), lambda k, p: (p, 0, 0, k, 0)),
        out_shape=jax.ShapeDtypeStruct((pairs, 2, DFT_N1, n2, c), F32),
        compiler_params=_params(("arbitrary", "arbitrary")),
        name="dft_mid",
    )(a, g, khat, f1)


def _dft_last_kernel(b_ref, h_ref, u_ref, x0_ref, bd_ref, o_ref, *, c):
    half = h_ref.shape[1] // 2
    bd = bd_ref[...]
    for t in range(DFT_ROW_CHUNK):
        rhs = jnp.concatenate([b_ref[0, 0, t], b_ref[0, 1, t]], axis=0).astype(BF16)
        v = jnp.dot(h_ref[t], rhs, preferred_element_type=F32)
        o_ref[0, :, t, :] = (v[:half] + u_ref[0, :, t, :] * bd) * x0_ref[0, :, t, :]
        o_ref[1, :, t, :] = (-v[half:] + u_ref[1, :, t, :] * bd) * x0_ref[1, :, t, :]


def _dft_last(bm, hh, u_view, x0_view, bias_d, pairs, n2, c):
    half = n2 // 2
    tc = DFT_ROW_CHUNK
    io = pl.BlockSpec((2, half, tc, c), lambda p, j: (p, 0, j, 0))
    return pl.pallas_call(
        functools.partial(_dft_last_kernel, c=c),
        grid=(pairs, DFT_N1 // tc),
        in_specs=[pl.BlockSpec((1, 2, tc, n2, c), lambda p, j: (p, 0, j, 0, 0)),
                  pl.BlockSpec((tc, n2, 2 * n2), lambda p, j: (j, 0, 0)), io, io, _full((1, c))],
        out_specs=io,
        out_shape=jax.ShapeDtypeStruct((2 * pairs, half, DFT_N1, c), F32),
        compiler_params=_params(("arbitrary", "arbitrary")),
        name="dft_last",
    )(bm, hh, u_view, x0_view, bias_d)


def _long_conv_mixer(u, x0, kfilt, ksum, bias_d):
    b, n, c = u.shape
    n2 = 2 * n // DFT_N1
    half = n2 // 2
    pairs = b // 2
    m_data, m_filt, g, f1, hh = (jnp.asarray(t) for t in _dft_tables(n2))
    kb = min(16, n2)
    scale = 1.0 / (ksum * float(DFT_N1 * n2))
    ka = _dft_stage1(m_filt, [kfilt.reshape(1, n2, DFT_N1, c)], 1, n2, n2, c)
    khat = _filter_spectrum(ka.reshape(1, 2, n2, DFT_N1, c), g, scale, n2, kb)
    u_view = u.reshape(b, half, DFT_N1, c)
    a = _dft_stage1(m_data, [u_view, u_view], pairs, half, n2, c)
    bm = _dft_mid(a.reshape(pairs, 2, n2, DFT_N1, c), g, khat, f1, pairs, n2, kb)
    out = _dft_last(bm, hh, u_view, x0.reshape(b, half, DFT_N1, c),
                    bias_d.reshape(1, c), pairs, n2, c)
    return out.reshape(b, n, c)


@functools.lru_cache(maxsize=None)
def _small_dft_tables(n):
    big = 2 * n
    a = np.arange(big)
    fr, fi = _twiddle(np.outer(a, a), big)
    cast = lambda m: np.asarray(m, dtype=BF16)
    return (cast(np.concatenate([fr, fi], 0)),
            cast(_real_form(fr[:, :n], fi[:, :n])),
            cast(_real_form(fr[:n], fi[:n])))


def _small_conv_kernel(k_ref, ks_ref, u_ref, x0_ref, bd_ref, mf_ref, md_ref, mi_ref, o_ref, *, n):
    big = 2 * n
    kh = jnp.dot(mf_ref[...], k_ref[...].astype(BF16), preferred_element_type=F32) * (1.0 / (ks_ref[...] * big))
    kr, ki = kh[:big], kh[big:]
    rhs = jnp.concatenate([u_ref[0], u_ref[1]], axis=0).astype(BF16)
    x = jnp.dot(md_ref[...], rhs, preferred_element_type=F32)
    xr, xi = x[:big], x[big:]
    v = jnp.concatenate([xr * kr - xi * ki, -(xr * ki + xi * kr)], axis=0).astype(BF16)
    y = jnp.dot(mi_ref[...], v, preferred_element_type=F32)
    bd = bd_ref[...]
    o_ref[0] = ((y[:n] + u_ref[0] * bd) * x0_ref[0]).astype(o_ref.dtype)
    o_ref[1] = ((-y[n:] + u_ref[1] * bd) * x0_ref[1]).astype(o_ref.dtype)


def _small_conv_mixer(u, x0, kfilt, ksum, bias_d):
    b, n, c = u.shape
    mf, md, mi = (jnp.asarray(t) for t in _small_dft_tables(n))
    io = pl.BlockSpec((2, n, c), lambda p: (p, 0, 0))
    return pl.pallas_call(
        functools.partial(_small_conv_kernel, n=n),
        grid=(b // 2,),
        in_specs=[_full((2 * n, c)), _full((1, c)), io, io, _full((1, c)),
                  _full(mf.shape), _full(md.shape), _full(mi.shape)],
        out_specs=io,
        out_shape=jax.ShapeDtypeStruct((b, n, c), BF16),
        compiler_params=_params(("arbitrary",)),
        name="small_conv",
    )(kfilt, ksum, u, x0, bias_d.reshape(1, c), mf, md, mi)


MIX_HALO = 2 * SUBLANES
MXU_DIM = 256
FFN_CHUNK_EDGES = (0, 5 * MXU_DIM, D_FF)


def _gelu_tanh(x):
    return 0.5 * x * (1.0 + jnp.tanh(math.sqrt(2.0 / math.pi) * (x + 0.044715 * (x * x * x))))


def _ffn_kernel(x_ref, xp_ref, xn_ref, oa_ref, oap_ref, oan_ref, oc_ref, ocp_ref, ocn_ref, wa_ref, wc_ref, g1_ref,
                sh_ref, sc_ref, gt_ref, g_ref, wu_ref, cw_ref, cb_ref, wd_ref, o_ref):
    i = pl.program_id(1)
    nt = pl.num_programs(1)
    tm = x_ref.shape[1]
    r = tm + 2 * SUBLANES

    def with_halo(m_ref, p_ref, n_ref):
        return jnp.concatenate([p_ref[0], m_ref[0], n_ref[0]], axis=0)

    mix = (jnp.dot(with_halo(oa_ref, oap_ref, oan_ref).astype(BF16), wa_ref[...], preferred_element_type=F32)
           + jnp.dot(with_halo(oc_ref, ocp_ref, ocn_ref).astype(BF16), wc_ref[...], preferred_element_type=F32))
    ext = with_halo(x_ref, xp_ref, xn_ref) + g1_ref[0] * mix[MIX_HALO - SUBLANES:MIX_HALO - SUBLANES + r]
    xm = ext[SUBLANES:tm + SUBLANES]
    h = _modulated_norm(ext, g_ref[...], sh_ref[0], sc_ref[0])
    row = lax.broadcasted_iota(jnp.int32, (r, 1), 0)
    inside = jnp.logical_and(jnp.logical_or(i > 0, row >= SUBLANES), jnp.logical_or(i < nt - 1, row < tm + SUBLANES))
    h = jnp.where(inside, h, 0.0).astype(BF16)
    hm = h[SUBLANES:tm + SUBLANES]
    acc = jnp.zeros((tm, D_MODEL), F32)
    for c0, c1 in zip(FFN_CHUNK_EDGES[:-1], FFN_CHUNK_EDGES[1:]):
        a = jnp.dot(h, wu_ref[:, c0:c1], preferred_element_type=F32)
        v = jnp.dot(hm, wu_ref[:, D_FF + c0:D_FF + c1], preferred_element_type=F32)
        ap = pltpu.roll(a, 1, 0)[SUBLANES:tm + SUBLANES]
        an = pltpu.roll(a, r - 1, 0)[SUBLANES:tm + SUBLANES]
        cw = cw_ref[:, c0:c1]
        conv = ap * cw[0:1] + a[SUBLANES:tm + SUBLANES] * cw[1:2] + an * cw[2:3] + cb_ref[:, c0:c1]
        act = (_gelu_tanh(conv) * v).astype(BF16)
        acc = acc + jnp.dot(act, wd_ref[c0:c1, :], preferred_element_type=F32)
    o_ref[0] = xm + gt_ref[0] * acc


def _mixer_out_ffn(x, oa, oc, wo_bf, gate1, sh, sc, gate2, gain, wu_bf, conv_w, conv_b, wd_bf):
    bx, t, d = x.shape
    tm = min(512, t)
    vec = pl.BlockSpec((1, 1, d), lambda b, i: (b, 0, 0))
    once = pl.Buffered(1)
    return pl.pallas_call(
        _ffn_kernel,
        grid=(bx, t // tm),
        in_specs=_halo_specs(tm, t, d) + _halo_specs(tm, t, Q_W, MIX_HALO) + _halo_specs(tm, t, CONV_W, MIX_HALO) + [
            pl.BlockSpec((Q_W, d), lambda b, i: (0, 0)), pl.BlockSpec((CONV_W, d), lambda b, i: (1, 0)),
            vec, vec, vec, vec, _full((1, d)),
            pl.BlockSpec((d, 2 * D_FF), lambda b, i: (0, 0), pipeline_mode=once),
            _full((3, D_FF)), _full((1, D_FF)),
            pl.BlockSpec((D_FF, d), lambda b, i: (0, 0), pipeline_mode=once)],
        out_specs=pl.BlockSpec((1, tm, d), lambda b, i: (b, i, 0)),
        out_shape=jax.ShapeDtypeStruct((bx, t, d), F32),
        compiler_params=_params(("arbitrary", "arbitrary")),
        name="mixer_out_ffn",
    )(x, x, x, oa, oa, oa, oc, oc, oc, wo_bf, wo_bf, gate1, sh, sc, gate2, gain, wu_bf, conv_w,
      conv_b.reshape(1, D_FF), wd_bf)


@functools.lru_cache(maxsize=None)
def _rope_tables(t):
    pos = np.arange(t)
    n_freq = HEAD_DIM // 4
    inv = ROPE_THETA ** (-np.arange(n_freq, dtype=np.float64) / n_freq)
    ang_r = (pos // GRID_W)[:, None] * inv
    ang_c = (pos % GRID_W)[:, None] * inv
    zero = np.zeros_like(ang_r)
    cos = np.concatenate([np.cos(ang_r)] * 2 + [np.cos(ang_c)] * 2, 1)
    sa = np.concatenate([-np.sin(ang_r), zero, -np.sin(ang_c), zero], 1)
    sb = np.concatenate([zero, np.sin(ang_r), zero, np.sin(ang_c)], 1)
    tile = lambda m: np.tile(m, (1, LANES // HEAD_DIM)).astype(np.float32)
    return tile(cos), tile(sa), tile(sb)


@functools.lru_cache(maxsize=None)
def _identity_rope_tables(t):
    return np.ones((t, LANES), np.float32), np.zeros((t, LANES), np.float32), np.zeros((t, LANES), np.float32)


@functools.lru_cache(maxsize=None)
def _head_block_diag():
    hid = np.arange(Q_W) // HEAD_DIM
    return np.asarray(hid[:, None] == hid[None, :], dtype=BF16)


def kernel(x, c, ctx, c_ctx, ada_w, ada_b, norm_mix, norm_ffn, mix_w_in, mix_w_out, attn_q_norm, attn_k_norm,
           swa_sink, hy_conv_w, hy_conv_b, hy_w1, hy_b1, hy_w2, hy_b2, hy_w3, hy_b3, hy_w4, hy_freq, hy_bias_d,
           sc_conv_w, ffn_w_up, ffn_conv_w, ffn_conv_b, ffn_w_down):
    b, s, d = x.shape
    s_ctx = ctx.shape[1]
    depth = ada_w.shape[0]
    assert d == D_MODEL and b % 2 == 0 and b + 1 <= MOD_ROWS and s % 1024 == 0 and s_ctx % SUBLANES == 0

    cvec = jnp.concatenate([c, c_ctx[None, :], jnp.zeros((MOD_ROWS - b - 1, d), F32)], axis=0)
    mods = _mods(cvec, ada_w, ada_b)
    rope = [jnp.asarray(t) for t in _rope_tables(s)]
    rope_ctx = [jnp.asarray(t) for t in _identity_rope_tables(s_ctx)]
    bd = jnp.asarray(_head_block_diag())
    xc = ctx
    weights = _weights_bf16([mix_w_in, mix_w_out, ffn_w_up, ffn_w_down])

    for i in range(depth):
        last = i == depth - 1
        j = i // 2
        lat = [mods[i, :b, k * d:(k + 1) * d][:, None, :] for k in range(6)]
        cx = [jnp.broadcast_to(mods[i, b, k * d:(k + 1) * d][None, None, :], (b, 1, d)) for k in range(6)]
        w_in, w_out, w_up, w_down = (w[i] for w in weights)
        g_mix = norm_mix[i].reshape(1, d)
        g_ffn = norm_ffn[i].reshape(1, d)
        qg = jnp.tile(attn_q_norm[i], N_Q_HEADS).reshape(1, Q_W)
        kg = jnp.tile(attn_k_norm[i], N_KV_HEADS).reshape(1, KV_W)

        conv = (hy_conv_w[j], hy_conv_b[j]) if i % 2 == 0 else (sc_conv_w[j],)
        q, k, vt, *mix = _inproj(x, lat[0], lat[1], g_mix, w_in, qg, kg, bd, *rope, *conv)
        qc, kc, vct, *mixc = _inproj(xc, cx[0], cx[1], g_mix, w_in, qg, kg, bd, *rope_ctx, *conv)

        bound = _score_bound(attn_q_norm[i], attn_k_norm[i])
        if i % 2 == 0:
            fargs = (hy_w1[j], hy_b1[j], hy_w2[j], hy_b2[j], hy_w3[j], hy_b3[j], hy_w4[j], hy_freq[j])
            o_attn = _dense_attention(q, k, vt, kc, vct, bound=bound)
            kf, ks = _implicit_filter(s, *fargs)
            o_conv = _long_conv_mixer(*mix, kf, ks, hy_bias_d[j])
            if not last:
                oc_attn = _dense_attention(qc, kc, vct, bound=bound)
                kfc, ksc = _implicit_filter(s_ctx, *fargs)
                oc_conv = _small_conv_mixer(*mixc, kfc, ksc, hy_bias_d[j])
        else:
            sink = swa_sink[j].reshape(N_KV_HEADS, GROUP)
            o_attn = _banded_attention(q, k, vt, kc, vct, sink, bound)
            o_conv = mix[0]
            if not last:
                oc_attn = _dense_attention(qc, kc, vct, sink=sink)
                oc_conv = mixc[0]

        ffn = (g_ffn, w_up, ffn_conv_w[i], ffn_conv_b[i], w_down)
        x = _mixer_out_ffn(x, o_attn, o_conv, w_out, lat[2], lat[3], lat[4], lat[5], *ffn)
        if not last:
            xc = _mixer_out_ffn(xc, oc_attn, oc_conv, w_out, cx[2], cx[3], cx[4], cx[5], *ffn)
    return x
```

```python
import functools
import math

import numpy as np
import jax
import jax.numpy as jnp
from jax import lax
from jax.experimental import pallas as pl
from jax.experimental.pallas import tpu as pltpu

F32 = jnp.float32
BF16 = jnp.bfloat16
HI = lax.Precision.HIGHEST

D_MODEL = 1024
GRID_W = 64
HEAD_DIM = 64
N_Q_HEADS = 8
N_KV_HEADS = 4
GROUP = N_Q_HEADS // N_KV_HEADS
Q_W = N_Q_HEADS * HEAD_DIM
KV_W = N_KV_HEADS * HEAD_DIM
QKV_W = Q_W + 2 * KV_W
CONV_W = D_MODEL // 2
MIX_IN_W = QKV_W + 3 * CONV_W
WINDOW = 128
ROPE_THETA = 10000.0
FILT_EMB = 33
FILT_WIDTH = 64
DECAY_TARGET = 1e-2
FAST_DECAY_PCT = 0.3
SLOW_DECAY_PCT = 1.5
DECAY_SHIFT = 0.05
D_FF = 2816
NEG_INF = -1e30
RMS_EPS = 1e-6
LOG2E = 1.4426950408889634
Q_SCALE = HEAD_DIM ** -0.5 * LOG2E

LANES = 128
SUBLANES = 8
DFT_N1 = 128
VMEM_LIMIT_MB = 56
VT_ROWS = LANES
MOD_ROWS = SUBLANES
PROJ_SUB_ROWS = 128


def _params(sem, vmem_mb=VMEM_LIMIT_MB):
    return pltpu.CompilerParams(dimension_semantics=sem, vmem_limit_bytes=vmem_mb * 1024 * 1024)


def _full(shape):
    nd = len(shape)
    return pl.BlockSpec(shape, lambda *_: (0,) * nd)


def _mods_kernel(c_ref, w_ref, b_ref, o_ref):
    c = c_ref[...]
    s = c / (1.0 + jnp.exp(-c))
    o_ref[0] = _dot_bf16x3(s, w_ref[0]) + b_ref[0]


def _mods(cvec, ada_w, ada_b):
    depth, d, n6 = ada_w.shape
    tn = 1536
    return pl.pallas_call(
        _mods_kernel,
        grid=(depth, n6 // tn),
        in_specs=[_full((MOD_ROWS, d)),
                  pl.BlockSpec((1, d, tn), lambda l, j: (l, 0, j)),
                  pl.BlockSpec((1, 1, tn), lambda l, j: (l, 0, j))],
        out_specs=pl.BlockSpec((1, MOD_ROWS, tn), lambda l, j: (l, 0, j)),
        out_shape=jax.ShapeDtypeStruct((depth, MOD_ROWS, n6), F32),
        compiler_params=_params(("arbitrary", "arbitrary")),
        name="ada_mods",
    )(cvec, ada_w, ada_b.reshape(depth, 1, n6))


def _modulated_norm(x, gain, shift, scale):
    ms = jnp.mean(x * x, axis=-1, keepdims=True)
    return (x * lax.rsqrt(ms + RMS_EPS)) * gain * (1.0 + scale) + shift


WEIGHT_CAST_STEPS = 8


def _cast_kernel(*refs):
    n = len(refs) // 2
    for w_ref, o_ref in zip(refs[:n], refs[n:]):
        o_ref[...] = w_ref[0].astype(o_ref.dtype)


def _weights_bf16(stacked):
    depth = stacked[0].shape[0]
    in_specs, out_specs, out_shape, args = [], [], [], []
    for w in stacked:
        _, rows, cols = w.shape
        tr = rows // WEIGHT_CAST_STEPS
        for layer in range(depth):
            in_specs.append(pl.BlockSpec((1, tr, cols), lambda r, layer=layer: (layer, r, 0)))
            out_specs.append(pl.BlockSpec((tr, cols), lambda r: (r, 0)))
            out_shape.append(jax.ShapeDtypeStruct((rows, cols), BF16))
            args.append(w)
    outs = pl.pallas_call(
        _cast_kernel,
        grid=(WEIGHT_CAST_STEPS,),
        in_specs=in_specs,
        out_specs=out_specs,
        out_shape=out_shape,
        compiler_params=_params(("arbitrary",)),
        name="weights_bf16",
    )(*args)
    return [outs[k * depth:(k + 1) * depth] for k in range(len(stacked))]


def _halo_specs(tm, t, width, rows=SUBLANES):
    nb = tm // rows
    last = t // rows - 1
    return [pl.BlockSpec((1, tm, width), lambda b, i: (b, i, 0)),
            pl.BlockSpec((1, rows, width), lambda b, i: (b, jnp.maximum(i * nb - 1, 0), 0)),
            pl.BlockSpec((1, rows, width), lambda b, i: (b, jnp.minimum((i + 1) * nb, last), 0))]


def _inproj_kernel(*refs, mode):
    (x_ref, xp_ref, xn_ref, sh_ref, sc_ref, g_ref, w_ref, qg_ref, kg_ref, bd_ref,
     cos_ref, sa_ref, sb_ref, cw_ref) = refs[:14]
    if mode == "hyena":
        cb_ref, q_ref, k_ref, vt_ref, u_ref, x0_ref = refs[14:]
    else:
        q_ref, k_ref, vt_ref, oc_ref = refs[14:]
    i = pl.program_id(1)
    nt = pl.num_programs(1)
    tm = x_ref.shape[1]
    sub = min(PROJ_SUB_ROWS, tm)
    gain, shift, scale = g_ref[...], sh_ref[0], sc_ref[0]

    cw = cw_ref[...]

    def conv_stage(ext, rows):
        n = sub + 2 * SUBLANES
        mid = slice(SUBLANES, sub + SUBLANES)
        if mode == "hyena":
            c = (pltpu.roll(ext, 1, 0)[mid] * cw[0:1] + ext[mid] * cw[1:2] + pltpu.roll(ext, n - 1, 0)[mid] * cw[2:3]
                 + cb_ref[...])
            x0_ref[0, rows] = c[:, :CONV_W]
            u_ref[0, rows] = c[:, 2 * CONV_W:] * c[:, CONV_W:2 * CONV_W]
        else:
            pr = ext[:, CONV_W:2 * CONV_W] * ext[:, 2 * CONV_W:]
            conv = pltpu.roll(pr, 1, 0)[mid] * cw[0:1] + pr[mid] * cw[1:2] + pltpu.roll(pr, n - 1, 0)[mid] * cw[2:3]
            oc_ref[0, rows] = (ext[mid, :CONV_W] * conv).astype(oc_ref.dtype)

    def head_norm_rope(t, hgain, bd, cos, sa, sb):
        sq = (t * t).astype(BF16)
        ssq = jnp.concatenate([jnp.dot(sq[:, c:c + MXU_DIM], bd[:MXU_DIM, :MXU_DIM], preferred_element_type=F32)
                               for c in range(0, t.shape[1], MXU_DIM)], axis=1)
        t = t * lax.rsqrt(ssq * (1.0 / HEAD_DIM) + RMS_EPS) * hgain
        outs = []
        for j in range(t.shape[1] // LANES):
            tj = t[:, j * LANES:(j + 1) * LANES]
            outs.append(tj * cos + pltpu.roll(tj, LANES - 16, 1) * sa + pltpu.roll(tj, 16, 1) * sb)
        return outs

    held = None
    for r0 in range(0, tm, sub):
        rows = slice(r0, r0 + sub)
        h = _modulated_norm(x_ref[0, rows], gain, shift, scale)
        z = jnp.dot(h.astype(BF16), w_ref[:, :QKV_W], preferred_element_type=F32)
        first, last = r0 == 0, r0 + sub == tm
        hc = jnp.concatenate(([_modulated_norm(xp_ref[0], gain, shift, scale)] if first else []) + [h]
                             + ([_modulated_norm(xn_ref[0], gain, shift, scale)] if last else []), axis=0)
        zc = jnp.dot(hc.astype(BF16), w_ref[:, QKV_W:], preferred_element_type=F32)
        if first:
            before = jnp.where(i > 0, zc[:SUBLANES], 0.0)
            zc = zc[SUBLANES:]
        if last:
            after = jnp.where(i < nt - 1, zc[sub:], 0.0)
            zc = zc[:sub]
        if held is not None:
            conv_stage(jnp.concatenate([before, held[0], zc[:SUBLANES]], axis=0), held[1])
            before = held[0][-SUBLANES:]
        held = (zc, rows)
        tabs = (cos_ref[rows], sa_ref[rows], sb_ref[rows])
        qs = head_norm_rope(z[:, :Q_W], qg_ref[...], bd_ref[...], *tabs)
        ks = head_norm_rope(z[:, Q_W:Q_W + KV_W], kg_ref[...], bd_ref[:KV_W, :KV_W], *tabs)
        for j, qj in enumerate(qs):
            qj = (qj * Q_SCALE).astype(BF16)
            q_ref[0, j, rows] = qj
        for j, kj in enumerate(ks):
            kj = kj.astype(BF16)
            k_ref[0, 2 * j, rows] = kj[:, :HEAD_DIM]
            k_ref[0, 2 * j + 1, rows] = kj[:, HEAD_DIM:]
        vt = z[:, Q_W + KV_W:QKV_W].T
        ones_row = (lax.broadcasted_iota(jnp.int32, (VT_ROWS - HEAD_DIM, sub), 0) == 0).astype(F32)
        for hh in range(N_KV_HEADS):
            vt_ref[0, hh, :, rows] = jnp.concatenate([vt[hh * HEAD_DIM:(hh + 1) * HEAD_DIM], ones_row],
                                                     axis=0).astype(BF16)

    conv_stage(jnp.concatenate([before, held[0], after], axis=0), held[1])


def _inproj(x, sh, sc, gain, w_bf, qg, kg, bd, cos, sa, sb, conv_w, conv_b=None):
    bx, t, d = x.shape
    tm = min(1024, t)
    mode = "hyena" if conv_b is not None else "short"
    vec = pl.BlockSpec((1, 1, d), lambda b, i: (b, 0, 0))
    tab = pl.BlockSpec((tm, LANES), lambda b, i: (i, 0))
    row = lambda w: pl.BlockSpec((1, tm, w), lambda b, i: (b, i, 0))
    in_specs = _halo_specs(tm, t, d) + [vec, vec, _full((1, d)), _full((d, MIX_IN_W)), _full((1, Q_W)),
                                        _full((1, KV_W)), _full((Q_W, Q_W)), tab, tab, tab, _full(conv_w.shape)]
    args = [x, x, x, sh, sc, gain, w_bf, qg, kg, bd, cos, sa, sb, conv_w]
    out_specs = [pl.BlockSpec((1, N_KV_HEADS, tm, LANES), lambda b, i: (b, 0, i, 0)),
                 pl.BlockSpec((1, N_KV_HEADS, tm, HEAD_DIM), lambda b, i: (b, 0, i, 0)),
                 pl.BlockSpec((1, N_KV_HEADS, VT_ROWS, tm), lambda b, i: (b, 0, 0, i))]
    out_shape = [jax.ShapeDtypeStruct((bx, N_KV_HEADS, t, LANES), BF16),
                 jax.ShapeDtypeStruct((bx, N_KV_HEADS, t, HEAD_DIM), BF16),
                 jax.ShapeDtypeStruct((bx, N_KV_HEADS, VT_ROWS, t), BF16)]
    if mode == "hyena":
        in_specs.append(_full((1, 3 * CONV_W)))
        args.append(conv_b.reshape(1, 3 * CONV_W))
        out_specs += [row(CONV_W), row(CONV_W)]
        out_shape += [jax.ShapeDtypeStruct((bx, t, CONV_W), F32)] * 2
    else:
        out_specs.append(row(CONV_W))
        out_shape.append(jax.ShapeDtypeStruct((bx, t, CONV_W), BF16))
    return pl.pallas_call(
        functools.partial(_inproj_kernel, mode=mode),
        grid=(bx, t // tm),
        in_specs=in_specs,
        out_specs=out_specs,
        out_shape=out_shape,
        compiler_params=_params(("arbitrary", "arbitrary")),
        name="in_proj",
    )(*args)


def _score_bound(q_gain, k_gain):
    b = HEAD_DIM * Q_SCALE * BOUND_MARGIN * jnp.max(jnp.abs(q_gain)) * jnp.max(jnp.abs(k_gain))
    return b.reshape(1, 1).astype(F32)


def _group_queries(qb):
    return jnp.concatenate([qb[:, :HEAD_DIM], qb[:, HEAD_DIM:]], axis=0)


def _scores_t(ks, q):
    return lax.dot_general(ks, q, (((1,), (1,)), ((), ())), preferred_element_type=F32)


def _sink_column(sink_ref, hh, tq, m, acc):
    col = lax.broadcasted_iota(jnp.int32, m.shape, 1)
    sink = jnp.where(col < tq, sink_ref[hh, 0], sink_ref[hh, 1]) * LOG2E
    m_new = jnp.maximum(m, sink)
    den_row = lax.broadcasted_iota(jnp.int32, acc.shape, 0) == HEAD_DIM
    return acc * jnp.exp2(m - m_new) + jnp.where(den_row, jnp.exp2(sink - m_new), 0.0)


def _attn_finish(acc, tq):
    o = (acc / acc[HEAD_DIM:HEAD_DIM + 1]).T
    return jnp.concatenate([o[:tq, :HEAD_DIM], o[tq:, :HEAD_DIM]], axis=1)


EXP_CHUNK_ELEMS = 32 * SUBLANES * LANES


def _score_stage(ks, q, s_ref, bias=None):
    s = _scores_t(ks, q)
    if bias is not None:
        s = s + bias
    s_ref[0:s.shape[0]] = s
    return jnp.max(s, axis=0, keepdims=True)


def _softmax_stage(s_ref, p_ref, n, vt, mx, m, acc_ref):
    m_new = jnp.maximum(m, mx)
    rows = EXP_CHUNK_ELEMS // m.shape[1]
    for c in range(0, n, rows):
        p_ref[c:c + rows] = jnp.exp2(s_ref[c:c + rows] - m_new).astype(BF16)
    acc_ref[...] = acc_ref[...] * jnp.exp2(m - m_new) + jnp.dot(vt, p_ref[0:n], preferred_element_type=F32)
    return m_new


BOUNDED_TILES = 8
BOUND_MARGIN = 1.01
SAFE_SHIFT = 40.0


def _flash_exact(q, k_tile, vt_tile, ctx, sa_ref, sb_ref, p_ref, acc_ref, *, tk, n_main):
    pa_ref, pb_ref = p_ref.at[0:tk], p_ref.at[tk:2 * tk]
    acc_ref[...] = jnp.zeros_like(acc_ref)
    m = jnp.full((1, q.shape[0]), NEG_INF, F32)
    mx_a = _score_stage(k_tile(0), q, sa_ref)
    if n_main > 1:
        def body(i, carry):
            m, mx_a = carry
            mx_b = _score_stage(k_tile(2 * i + 1), q, sb_ref)
            m = _softmax_stage(sa_ref, pa_ref, tk, vt_tile(2 * i), mx_a, m, acc_ref)
            mx_a = _score_stage(k_tile(2 * i + 2), q, sa_ref)
            m = _softmax_stage(sb_ref, pb_ref, tk, vt_tile(2 * i + 1), mx_b, m, acc_ref)
            return m, mx_a

        m, mx_a = lax.fori_loop(0, n_main // 2 - 1, body, (m, mx_a))
        mx_b = _score_stage(k_tile(n_main - 1), q, sb_ref)
        m = _softmax_stage(sa_ref, pa_ref, tk, vt_tile(n_main - 2), mx_a, m, acc_ref)
        last = (sb_ref, pb_ref, mx_b)
        spare = (sa_ref, pa_ref)
    else:
        last = (sa_ref, pa_ref, mx_a)
        spare = (sb_ref, pb_ref)
    if ctx is not None:
        mx_c = _score_stage(ctx[0], q, spare[0])
    m = _softmax_stage(last[0], last[1], tk, vt_tile(n_main - 1), last[2], m, acc_ref)
    if ctx is not None:
        m = _softmax_stage(spare[0], spare[1], ctx[0].shape[0], ctx[1], mx_c, m, acc_ref)
    return m


def _exp2_as_bf16(s):
    return jnp.exp2(s).astype(BF16)


def _flash_bounded(q, k_ref, vt_ref, ctx, p_ref, acc_ref, *, tk, n_main):
    step = BOUNDED_TILES * tk if n_main % BOUNDED_TILES == 0 else tk

    def group(j):
        off = pl.multiple_of(j * step, step)
        for c in range(0, step, tk):
            p_ref[c:c + tk] = _exp2_as_bf16(_scores_t(k_ref[0, 0, pl.ds(off + c, tk), :], q))
        return jnp.dot(vt_ref[0, 0, :, pl.ds(off, step)], p_ref[0:step], preferred_element_type=F32)

    first = group(0)
    if ctx is not None:
        pc_ref = p_ref.at[BOUNDED_TILES * tk:BOUNDED_TILES * tk + ctx[0].shape[0]]
        pc_ref[...] = _exp2_as_bf16(_scores_t(ctx[0], q))
        first = first + jnp.dot(ctx[1], pc_ref[...], preferred_element_type=F32)
    acc_ref[...] = first

    def body(j, carry):
        acc_ref[...] += group(j)
        return carry

    lax.fori_loop(1, n_main * tk // step, body, 0)


def _flash_kernel(*refs, tq, tk, n_main, has_ctx, has_sink):
    sa_ref, sb_ref, p_ref, acc_ref = refs[-4:]
    o_ref = refs[-5]
    refs = list(refs[:-5])
    sink_ref = refs.pop(0) if has_sink else None
    bound_ref = None if has_sink else refs.pop(0)
    q_ref, k_ref, vt_ref = refs[:3]
    m_cols = GROUP * tq
    q = _group_queries(q_ref[0, 0])
    ctx = (refs[3][0, 0], refs[4][0, 0]) if has_ctx else None

    def k_tile(j):
        return k_ref[0, 0, pl.ds(pl.multiple_of(j * tk, tk), tk), :]

    def vt_tile(j):
        return vt_ref[0, 0, :, pl.ds(pl.multiple_of(j * tk, tk), tk)]

    exact = functools.partial(_flash_exact, q, k_tile, vt_tile, ctx, sa_ref, sb_ref, p_ref, acc_ref,
                              tk=tk, n_main=n_main)
    if has_sink:
        acc = _sink_column(sink_ref, pl.program_id(1), tq, exact(), acc_ref[...])
    else:
        bound = bound_ref[0, 0]
        safe = bound <= SAFE_SHIFT

        @pl.when(safe)
        def _():
            _flash_bounded(q, k_ref, vt_ref, ctx, p_ref, acc_ref, tk=tk, n_main=n_main)

        @pl.when(jnp.logical_not(safe))
        def _():
            exact()

        acc = acc_ref[...]
    o_ref[0] = _attn_finish(acc, tq).astype(o_ref.dtype)


def _kv_specs(n):
    return [pl.BlockSpec((1, 1, n, HEAD_DIM), lambda b, h, i: (b, h, 0, 0)),
            pl.BlockSpec((1, 1, VT_ROWS, n), lambda b, h, i: (b, h, 0, 0))]


def _dense_attention(q, k, vt, kc=None, vct=None, *, bound=None, sink=None):
    bx, _, t, _ = q.shape
    s = k.shape[2]
    tq = min(1024, t)
    tk = min(512, s)
    has_ctx = kc is not None
    has_sink = sink is not None
    in_specs = [pl.BlockSpec((1, 1, tq, LANES), lambda b, h, i: (b, h, i, 0))] + _kv_specs(s)
    args = [q, k, vt]
    if has_ctx:
        in_specs += _kv_specs(kc.shape[2])
        args += [kc, vct]
    in_specs = [pl.BlockSpec(memory_space=pltpu.SMEM)] + in_specs
    args = [sink if has_sink else bound] + args
    n_main = s // tk
    assert n_main == 1 or n_main % 2 == 0
    n_ctx = kc.shape[2] if has_ctx else 0
    assert n_ctx <= tk
    m_cols = GROUP * tq
    return pl.pallas_call(
        functools.partial(_flash_kernel, tq=tq, tk=tk, n_main=n_main, has_ctx=has_ctx, has_sink=has_sink),
        grid=(bx, N_KV_HEADS, t // tq),
        in_specs=in_specs,
        out_specs=pl.BlockSpec((1, tq, LANES), lambda b, h, i: (b, i, h)),
        out_shape=jax.ShapeDtypeStruct((bx, t, Q_W), BF16),
        scratch_shapes=[pltpu.VMEM((tk, m_cols), F32), pltpu.VMEM((tk, m_cols), F32),
                        pltpu.VMEM((BOUNDED_TILES * tk + n_ctx, m_cols), BF16), pltpu.VMEM((VT_ROWS, m_cols), F32)],
        compiler_params=_params(("arbitrary", "arbitrary", "arbitrary")),
        name="dense_attention",
    )(*args)


BAND_SUB = 256


@functools.lru_cache(maxsize=None)
def _band_bias(n_ctx):
    span = BAND_SUB + 2 * WINDOW
    kr = np.arange(span)[None, :, None]
    qc = (np.arange(GROUP * BAND_SUB) % BAND_SUB)[None, None, :]
    rel = np.arange(3)[:, None, None]
    band = np.where(np.abs(kr - qc - rel * WINDOW) <= WINDOW, 0.0, NEG_INF)
    return np.concatenate([band, np.zeros((3, n_ctx, GROUP * BAND_SUB))], axis=1).astype(np.float32)


def _banded_kernel(sink_ref, bound_ref, q_ref, k_ref, vt_ref, kc_ref, vct_ref, bias_ref, o_ref,
                   sl_ref, sc_ref, p_ref, acc_ref, *, n_sub, s_len):
    i = pl.program_id(2)
    hh = pl.program_id(1)
    sub = BAND_SUB
    m_cols = GROUP * sub
    span = sub + 2 * WINDOW
    n_ctx = kc_ref.shape[2]

    def window(u):
        q = _group_queries(q_ref[0, 0, u * sub:(u + 1) * sub, :])
        q0 = (i * n_sub + u) * sub
        start = pl.multiple_of(jnp.clip(q0 - WINDOW, 0, s_len - span), WINDOW)
        return q, start, bias_ref[(q0 - start) // WINDOW]

    shift = jnp.maximum(bound_ref[0, 0], jnp.maximum(sink_ref[hh, 0], sink_ref[hh, 1]) * LOG2E)
    safe = shift <= SAFE_SHIFT

    @pl.when(safe)
    def _():
        col = lax.broadcasted_iota(jnp.int32, (1, m_cols), 1)
        sink = jnp.where(col < sub, sink_ref[hh, 0], sink_ref[hh, 1]) * LOG2E
        den_row = lax.broadcasted_iota(jnp.int32, (VT_ROWS, m_cols), 0) == HEAD_DIM
        sink_den = jnp.where(den_row, jnp.exp2(sink), 0.0)
        for u in range(n_sub):
            q, start, bias = window(u)
            keys = jnp.concatenate([k_ref[0, 0, pl.ds(start, span), :], kc_ref[0, 0]], axis=0)
            vt = jnp.concatenate([vt_ref[0, 0, :, pl.ds(start, span)], vct_ref[0, 0]], axis=1)
            p_u = p_ref.at[u]
            p_u[...] = _exp2_as_bf16(_scores_t(keys, q) + bias)
            acc = jnp.dot(vt, p_u[...], preferred_element_type=F32) + sink_den
            o_ref[0, u * sub:(u + 1) * sub] = _attn_finish(acc, sub).astype(o_ref.dtype)

    @pl.when(jnp.logical_not(safe))
    def _():
        stats = []
        for u in range(n_sub):
            q, start, bias = window(u)
            mx_l = _score_stage(k_ref[0, 0, pl.ds(start, span), :], q, sl_ref.at[u], bias=bias[:span])
            mx_c = _score_stage(kc_ref[0, 0], q, sc_ref.at[u])
            stats.append((start, mx_l, mx_c))
        for u in range(n_sub):
            start, mx_l, mx_c = stats[u]
            acc_u = acc_ref.at[u]
            acc_u[...] = jnp.zeros((VT_ROWS, m_cols), F32)
            m = jnp.full((1, m_cols), NEG_INF, F32)
            p_l, p_c = p_ref.at[u, 0:span], p_ref.at[u, span:span + n_ctx]
            m = _softmax_stage(sl_ref.at[u], p_l, span, vt_ref[0, 0, :, pl.ds(start, span)], mx_l, m, acc_u)
            m = _softmax_stage(sc_ref.at[u], p_c, n_ctx, vct_ref[0, 0], mx_c, m, acc_u)
            acc = _sink_column(sink_ref, hh, sub, m, acc_u[...])
            o_ref[0, u * sub:(u + 1) * sub] = _attn_finish(acc, sub).astype(o_ref.dtype)


def _banded_attention(q, k, vt, kc, vct, sink, bound):
    bx, _, t, _ = q.shape
    n_ctx = kc.shape[2]
    n_sub = next(k for k in (8, 4, 2, 1) if t % (k * BAND_SUB) == 0)
    tq = n_sub * BAND_SUB
    m_cols = GROUP * BAND_SUB
    span = BAND_SUB + 2 * WINDOW
    bias = jnp.asarray(_band_bias(n_ctx))
    return pl.pallas_call(
        functools.partial(_banded_kernel, n_sub=n_sub, s_len=t),
        grid=(bx, N_KV_HEADS, t // tq),
        in_specs=[pl.BlockSpec(memory_space=pltpu.SMEM), pl.BlockSpec(memory_space=pltpu.SMEM),
                  pl.BlockSpec((1, 1, tq, LANES), lambda b, h, i: (b, h, i, 0))]
                 + _kv_specs(t) + _kv_specs(n_ctx) + [_full(bias.shape)],
        out_specs=pl.BlockSpec((1, tq, LANES), lambda b, h, i: (b, i, h)),
        out_shape=jax.ShapeDtypeStruct((bx, t, Q_W), BF16),
        scratch_shapes=[pltpu.VMEM((n_sub, span, m_cols), F32), pltpu.VMEM((n_sub, n_ctx, m_cols), F32),
                        pltpu.VMEM((n_sub, span + n_ctx, m_cols), BF16), pltpu.VMEM((n_sub, VT_ROWS, m_cols), F32)],
        compiler_params=_params(("arbitrary", "arbitrary", "arbitrary")),
        name="banded_attention",
    )(sink, bound, q, k, vt, kc, vct, bias)


@functools.lru_cache(maxsize=None)
def _filter_features(n):
    j = np.arange(2 * n)
    d = np.where(j <= n, j, 2 * n - j)
    d = np.where(j == n, 0, d)
    bands = (FILT_EMB - 1) // 2
    t01 = np.linspace(0.0, 1.0, n)[d]
    w = 2.0 * np.pi * d.astype(np.float64) / n
    f = np.linspace(1e-4, bands - 1, bands)[None, :]
    feats = np.zeros((2 * n, LANES), np.float64)
    feats[:, 0] = t01
    feats[:, 1:1 + bands] = np.cos(f * w[:, None])
    feats[:, 1 + bands:FILT_EMB] = -np.sin(f * w[:, None])
    feats[:, 64] = t01
    feats[:, 65] = (j < n)
    feats[:, 66] = (j != n)
    return feats.astype(np.float32)


def _dot_bf16x3(a, b):
    a_hi = a.astype(BF16)
    a_lo = (a - a_hi.astype(F32)).astype(BF16)
    b_hi = b.astype(BF16)
    b_lo = (b - b_hi.astype(F32)).astype(BF16)
    dot = lambda x, y: jnp.dot(x, y, preferred_element_type=F32)
    return dot(a_hi, b_hi) + dot(a_hi, b_lo) + dot(a_lo, b_hi)


def _filter_kernel(f_ref, w1_ref, b1_ref, w2_ref, b2_ref, w3_ref, b3_ref, w4_ref, fr_ref, dl_ref, k_ref, s_ref):
    f = f_ref[...]
    fr = fr_ref[...]
    mm = lambda a, b: jnp.dot(a, b, preferred_element_type=F32, precision=HI)
    h = jnp.sin(fr * (mm(w1_ref[...], f) + b1_ref[...]))
    h = jnp.sin(fr * (mm(w2_ref[...], h) + b2_ref[...]))
    h = jnp.sin(fr * (mm(w3_ref[...], h) + b3_ref[...]))
    hf = _dot_bf16x3(w4_ref[...], h)
    win = jnp.exp(-dl_ref[...] * f[64:65]) + DECAY_SHIFT
    k = (jnp.where(f[65:66] > 0.5, hf[:CONV_W], hf[CONV_W:]) * win * f[66:67]).T
    k_ref[...] = k

    @pl.when(pl.program_id(0) == 0)
    def _():
        s_ref[...] = jnp.zeros_like(s_ref)

    s_ref[...] += jnp.sum(jnp.abs(k), axis=0, keepdims=True)


def _implicit_filter(n, w1, b1, w2, b2, w3, b3, w4, freq):
    feats_t = jnp.asarray(np.ascontiguousarray(_filter_features(n).T))
    col = lambda a: a.reshape(-1, 1)
    w1t = jnp.pad(w1.T, ((0, 0), (0, LANES - FILT_EMB)))
    deltas = np.abs(np.linspace(math.log(DECAY_TARGET) / SLOW_DECAY_PCT, math.log(DECAY_TARGET) / FAST_DECAY_PCT,
                                CONV_W)).astype(np.float32).reshape(CONV_W, 1)
    tr = min(1024, 2 * n)
    sq = _full((FILT_WIDTH, FILT_WIDTH))
    vec = _full((FILT_WIDTH, 1))
    return pl.pallas_call(
        _filter_kernel,
        grid=(2 * n // tr,),
        in_specs=[pl.BlockSpec((LANES, tr), lambda i: (0, i)), _full((FILT_WIDTH, LANES)), vec, sq, vec, sq, vec,
                  _full((2 * CONV_W, FILT_WIDTH)), vec, _full((CONV_W, 1))],
        out_specs=[pl.BlockSpec((tr, CONV_W), lambda i: (i, 0)), _full((1, CONV_W))],
        out_shape=[jax.ShapeDtypeStruct((2 * n, CONV_W), F32), jax.ShapeDtypeStruct((1, CONV_W), F32)],
        compiler_params=_params(("arbitrary",)),
        name="hyena_filter",
    )(feats_t, w1t, col(b1), w2.T, col(b2), w3.T, col(b3), w4.T, col(freq), jnp.asarray(deltas))


def _twiddle(idx, mod):
    ang = 2.0 * np.pi * (idx % mod) / mod
    return np.cos(ang), -np.sin(ang)


def _real_form(mr, mi):
    return np.concatenate([np.concatenate([mr, -mi], -1), np.concatenate([mi, mr], -1)], -2)


@functools.lru_cache(maxsize=None)
def _dft_tables(n2):
    n1 = DFT_N1
    n = n1 * n2
    h = n2 // 2
    a2 = np.arange(n2)
    fr, fi = _twiddle(np.outer(a2, a2), n2)
    m_data = _real_form(fr[:, :h], fi[:, :h])
    m_filt = np.concatenate([fr, fi], 0)
    k2 = a2[:, None, None]
    k1 = np.arange(n1)[None, :, None]
    c1 = np.arange(n1)[None, None, :]
    g = _real_form(*_twiddle(c1 * (n2 * k1 + k2), n))
    a1 = np.arange(n1)
    f1 = _real_form(*_twiddle(np.outer(a1, a1), n1))
    t2 = a1[:, None, None]
    t1 = np.arange(h)[None, :, None]
    j1 = a2[None, None, :]
    hh = _real_form(*_twiddle(j1 * (n1 * t1 + t2), n))
    cast = lambda m: np.asarray(m, dtype=BF16)
    return cast(m_data), cast(m_filt), cast(g), cast(f1), cast(hh)


DFT_ROW_CHUNK = 2 * SUBLANES


def _dft_rows_kernel(m_ref, *refs):
    o_ref = refs[-1]
    cols = [jnp.concatenate([r[0, :, t, :] for r in refs[:-1]], axis=0) for t in range(DFT_ROW_CHUNK)]
    rhs = jnp.concatenate(cols, axis=1).astype(BF16)
    out = jnp.dot(m_ref[...], rhs, preferred_element_type=F32)
    half = out.shape[0] // 2
    o_ref[0, 0] = out[:half].astype(o_ref.dtype)
    o_ref[0, 1] = out[half:].astype(o_ref.dtype)


def _dft_stage1(mat, views, pairs, rows, n2, c):
    nin = len(views)
    lc = DFT_ROW_CHUNK * c
    in_specs = [_full(mat.shape)] + [
        pl.BlockSpec((1, rows, DFT_ROW_CHUNK, c), (lambda p, j, a=a: (nin * p + a, 0, j, 0))) for a in range(nin)]
    return pl.pallas_call(
        _dft_rows_kernel,
        grid=(pairs, DFT_N1 // DFT_ROW_CHUNK),
        in_specs=in_specs,
        out_specs=pl.BlockSpec((1, 2, n2, lc), lambda p, j: (p, 0, 0, j)),
        out_shape=jax.ShapeDtypeStruct((pairs, 2, n2, DFT_N1 * c), BF16),
        compiler_params=_params(("arbitrary", "arbitrary")),
        name="dft_stage1",
    )(mat, *views)


def _spectrum_kernel(a_ref, g_ref, sc_ref, o_ref, *, kb):
    for j in range(kb):
        rhs = jnp.concatenate([a_ref[0, 0, j], a_ref[0, 1, j]], axis=0)
        x = jnp.dot(g_ref[j], rhs, preferred_element_type=F32)
        o_ref[0, j] = (x[:DFT_N1] * sc_ref[...]).astype(o_ref.dtype)
        o_ref[1, j] = (x[DFT_N1:] * sc_ref[...]).astype(o_ref.dtype)


def _filter_spectrum(a, g, scale, n2, kb):
    c = a.shape[-1]
    return pl.pallas_call(
        functools.partial(_spectrum_kernel, kb=kb),
        grid=(n2 // kb,),
        in_specs=[pl.BlockSpec((1, 2, kb, DFT_N1, c), lambda k: (0, 0, k, 0, 0)),
                  pl.BlockSpec((kb, 2 * DFT_N1, 2 * DFT_N1), lambda k: (k, 0, 0)), _full((1, c))],
        out_specs=pl.BlockSpec((2, kb, DFT_N1, c), lambda k: (0, k, 0, 0)),
        out_shape=jax.ShapeDtypeStruct((2, n2, DFT_N1, c), BF16),
        compiler_params=_params(("arbitrary",)),
        name="filter_spectrum",
    )(a, g, scale)


def _dft_mid_kernel(a_ref, g_ref, kh_ref, f_ref, o_ref, *, kb):
    for j in range(kb):
        rhs = jnp.concatenate([a_ref[0, 0, j], a_ref[0, 1, j]], axis=0)
        x = jnp.dot(g_ref[j], rhs, preferred_element_type=F32)
        xr, xi = x[:DFT_N1], x[DFT_N1:]
        kr, ki = kh_ref[0, j].astype(F32), kh_ref[1, j].astype(F32)
        yr = xr * kr - xi * ki
        yi = xr * ki + xi * kr
        v = jnp.concatenate([yr, -yi], axis=0).astype(BF16)
        b = jnp.dot(f_ref[...], v, preferred_element_type=F32)
        o_ref[0, 0, :, j, :] = b[:DFT_N1]
        o_ref[0, 1, :, j, :] = b[DFT_N1:]


def _dft_mid(a, g, khat, f1, pairs, n2, kb):
    c = a.shape[-1]
    blk = pl.BlockSpec((1, 2, kb, DFT_N1, c), lambda k, p: (p, 0, k, 0, 0))
    return pl.pallas_call(
        functools.partial(_dft_mid_kernel, kb=kb),
        grid=(n2 // kb, pairs),
        in_specs=[blk, pl.BlockSpec((kb, 2 * DFT_N1, 2 * DFT_N1), lambda k, p: (k, 0, 0)),
                  pl.BlockSpec((2, kb, DFT_N1, c), lambda k, p: (0, k, 0, 0)), _full(f1.shape)],
        out_specs=pl.BlockSpec((1, 2, DFT_N1, kb, c), lambda k, p: (p, 0, 0, k, 0)),
        out_shape=jax.ShapeDtypeStruct((pairs, 2, DFT_N1, n2, c), F32),
        compiler_params=_params(("arbitrary", "arbitrary")),
        name="dft_mid",
    )(a, g, khat, f1)


def _dft_last_kernel(b_ref, h_ref, u_ref, x0_ref, bd_ref, o_ref, *, c):
    half = h_ref.shape[1] // 2
    bd = bd_ref[...]
    for t in range(DFT_ROW_CHUNK):
        rhs = jnp.concatenate([b_ref[0, 0, t], b_ref[0, 1, t]], axis=0).astype(BF16)
        v = jnp.dot(h_ref[t], rhs, preferred_element_type=F32)
        o_ref[0, :, t, :] = (v[:half] + u_ref[0, :, t, :] * bd) * x0_ref[0, :, t, :]
        o_ref[1, :, t, :] = (-v[half:] + u_ref[1, :, t, :] * bd) * x0_ref[1, :, t, :]


def _dft_last(bm, hh, u_view, x0_view, bias_d, pairs, n2, c):
    half = n2 // 2
    tc = DFT_ROW_CHUNK
    io = pl.BlockSpec((2, half, tc, c), lambda p, j: (p, 0, j, 0))
    return pl.pallas_call(
        functools.partial(_dft_last_kernel, c=c),
        grid=(pairs, DFT_N1 // tc),
        in_specs=[pl.BlockSpec((1, 2, tc, n2, c), lambda p, j: (p, 0, j, 0, 0)),
                  pl.BlockSpec((tc, n2, 2 * n2), lambda p, j: (j, 0, 0)), io, io, _full((1, c))],
        out_specs=io,
        out_shape=jax.ShapeDtypeStruct((2 * pairs, half, DFT_N1, c), F32),
        compiler_params=_params(("arbitrary", "arbitrary")),
        name="dft_last",
    )(bm, hh, u_view, x0_view, bias_d)


def _long_conv_mixer(u, x0, kfilt, ksum, bias_d):
    b, n, c = u.shape
    n2 = 2 * n // DFT_N1
    half = n2 // 2
    pairs = b // 2
    m_data, m_filt, g, f1, hh = (jnp.asarray(t) for t in _dft_tables(n2))
    kb = min(16, n2)
    scale = 1.0 / (ksum * float(DFT_N1 * n2))
    ka = _dft_stage1(m_filt, [kfilt.reshape(1, n2, DFT_N1, c)], 1, n2, n2, c)
    khat = _filter_spectrum(ka.reshape(1, 2, n2, DFT_N1, c), g, scale, n2, kb)
    u_view = u.reshape(b, half, DFT_N1, c)
    a = _dft_stage1(m_data, [u_view, u_view], pairs, half, n2, c)
    bm = _dft_mid(a.reshape(pairs, 2, n2, DFT_N1, c), g, khat, f1, pairs, n2, kb)
    out = _dft_last(bm, hh, u_view, x0.reshape(b, half, DFT_N1, c),
                    bias_d.reshape(1, c), pairs, n2, c)
    return out.reshape(b, n, c)


@functools.lru_cache(maxsize=None)
def _small_dft_tables(n):
    big = 2 * n
    a = np.arange(big)
    fr, fi = _twiddle(np.outer(a, a), big)
    cast = lambda m: np.asarray(m, dtype=BF16)
    return (cast(np.concatenate([fr, fi], 0)),
            cast(_real_form(fr[:, :n], fi[:, :n])),
            cast(_real_form(fr[:n], fi[:n])))


def _small_conv_kernel(k_ref, ks_ref, u_ref, x0_ref, bd_ref, mf_ref, md_ref, mi_ref, o_ref, *, n):
    big = 2 * n
    kh = jnp.dot(mf_ref[...], k_ref[...].astype(BF16), preferred_element_type=F32) * (1.0 / (ks_ref[...] * big))
    kr, ki = kh[:big], kh[big:]
    rhs = jnp.concatenate([u_ref[0], u_ref[1]], axis=0).astype(BF16)
    x = jnp.dot(md_ref[...], rhs, preferred_element_type=F32)
    xr, xi = x[:big], x[big:]
    v = jnp.concatenate([xr * kr - xi * ki, -(xr * ki + xi * kr)], axis=0).astype(BF16)
    y = jnp.dot(mi_ref[...], v, preferred_element_type=F32)
    bd = bd_ref[...]
    o_ref[0] = ((y[:n] + u_ref[0] * bd) * x0_ref[0]).astype(o_ref.dtype)
    o_ref[1] = ((-y[n:] + u_ref[1] * bd) * x0_ref[1]).astype(o_ref.dtype)


def _small_conv_mixer(u, x0, kfilt, ksum, bias_d):
    b, n, c = u.shape
    mf, md, mi = (jnp.asarray(t) for t in _small_dft_tables(n))
    io = pl.BlockSpec((2, n, c), lambda p: (p, 0, 0))
    return pl.pallas_call(
        functools.partial(_small_conv_kernel, n=n),
        grid=(b // 2,),
        in_specs=[_full((2 * n, c)), _full((1, c)), io, io, _full((1, c)),
                  _full(mf.shape), _full(md.shape), _full(mi.shape)],
        out_specs=io,
        out_shape=jax.ShapeDtypeStruct((b, n, c), BF16),
        compiler_params=_params(("arbitrary",)),
        name="small_conv",
    )(kfilt, ksum, u, x0, bias_d.reshape(1, c), mf, md, mi)


MIX_HALO = 2 * SUBLANES
MXU_DIM = 256
FFN_CHUNK_EDGES = (0, 5 * MXU_DIM, D_FF)


def _gelu_tanh(x):
    return 0.5 * x * (1.0 + jnp.tanh(math.sqrt(2.0 / math.pi) * (x + 0.044715 * (x * x * x))))


def _ffn_kernel(x_ref, xp_ref, xn_ref, oa_ref, oap_ref, oan_ref, oc_ref, ocp_ref, ocn_ref, wa_ref, wc_ref, g1_ref,
                sh_ref, sc_ref, gt_ref, g_ref, wu_ref, cw_ref, cb_ref, wd_ref, o_ref):
    i = pl.program_id(1)
    nt = pl.num_programs(1)
    tm = x_ref.shape[1]
    r = tm + 2 * SUBLANES

    def with_halo(m_ref, p_ref, n_ref):
        return jnp.concatenate([p_ref[0], m_ref[0], n_ref[0]], axis=0)

    mix = (jnp.dot(with_halo(oa_ref, oap_ref, oan_ref).astype(BF16), wa_ref[...], preferred_element_type=F32)
           + jnp.dot(with_halo(oc_ref, ocp_ref, ocn_ref).astype(BF16), wc_ref[...], preferred_element_type=F32))
    ext = with_halo(x_ref, xp_ref, xn_ref) + g1_ref[0] * mix[MIX_HALO - SUBLANES:MIX_HALO - SUBLANES + r]
    xm = ext[SUBLANES:tm + SUBLANES]
    h = _modulated_norm(ext, g_ref[...], sh_ref[0], sc_ref[0])
    row = lax.broadcasted_iota(jnp.int32, (r, 1), 0)
    inside = jnp.logical_and(jnp.logical_or(i > 0, row >= SUBLANES), jnp.logical_or(i < nt - 1, row < tm + SUBLANES))
    h = jnp.where(inside, h, 0.0).astype(BF16)
    hm = h[SUBLANES:tm + SUBLANES]
    acc = jnp.zeros((tm, D_MODEL), F32)
    for c0, c1 in zip(FFN_CHUNK_EDGES[:-1], FFN_CHUNK_EDGES[1:]):
        a = jnp.dot(h, wu_ref[:, c0:c1], preferred_element_type=F32)
        v = jnp.dot(hm, wu_ref[:, D_FF + c0:D_FF + c1], preferred_element_type=F32)
        ap = pltpu.roll(a, 1, 0)[SUBLANES:tm + SUBLANES]
        an = pltpu.roll(a, r - 1, 0)[SUBLANES:tm + SUBLANES]
        cw = cw_ref[:, c0:c1]
        conv = ap * cw[0:1] + a[SUBLANES:tm + SUBLANES] * cw[1:2] + an * cw[2:3] + cb_ref[:, c0:c1]
        act = (_gelu_tanh(conv) * v).astype(BF16)
        acc = acc + jnp.dot(act, wd_ref[c0:c1, :], preferred_element_type=F32)
    o_ref[0] = xm + gt_ref[0] * acc


def _mixer_out_ffn(x, oa, oc, wo_bf, gate1, sh, sc, gate2, gain, wu_bf, conv_w, conv_b, wd_bf):
    bx, t, d = x.shape
    tm = min(512, t)
    vec = pl.BlockSpec((1, 1, d), lambda b, i: (b, 0, 0))
    once = pl.Buffered(1)
    return pl.pallas_call(
        _ffn_kernel,
        grid=(bx, t // tm),
        in_specs=_halo_specs(tm, t, d) + _halo_specs(tm, t, Q_W, MIX_HALO) + _halo_specs(tm, t, CONV_W, MIX_HALO) + [
            pl.BlockSpec((Q_W, d), lambda b, i: (0, 0)), pl.BlockSpec((CONV_W, d), lambda b, i: (1, 0)),
            vec, vec, vec, vec, _full((1, d)),
            pl.BlockSpec((d, 2 * D_FF), lambda b, i: (0, 0), pipeline_mode=once),
            _full((3, D_FF)), _full((1, D_FF)),
            pl.BlockSpec((D_FF, d), lambda b, i: (0, 0), pipeline_mode=once)],
        out_specs=pl.BlockSpec((1, tm, d), lambda b, i: (b, i, 0)),
        out_shape=jax.ShapeDtypeStruct((bx, t, d), F32),
        compiler_params=_params(("arbitrary", "arbitrary")),
        name="mixer_out_ffn",
    )(x, x, x, oa, oa, oa, oc, oc, oc, wo_bf, wo_bf, gate1, sh, sc, gate2, gain, wu_bf, conv_w,
      conv_b.reshape(1, D_FF), wd_bf)


@functools.lru_cache(maxsize=None)
def _rope_tables(t):
    pos = np.arange(t)
    n_freq = HEAD_DIM // 4
    inv = ROPE_THETA ** (-np.arange(n_freq, dtype=np.float64) / n_freq)
    ang_r = (pos // GRID_W)[:, None] * inv
    ang_c = (pos % GRID_W)[:, None] * inv
    zero = np.zeros_like(ang_r)
    cos = np.concatenate([np.cos(ang_r)] * 2 + [np.cos(ang_c)] * 2, 1)
    sa = np.concatenate([-np.sin(ang_r), zero, -np.sin(ang_c), zero], 1)
    sb = np.concatenate([zero, np.sin(ang_r), zero, np.sin(ang_c)], 1)
    tile = lambda m: np.tile(m, (1, LANES // HEAD_DIM)).astype(np.float32)
    return tile(cos), tile(sa), tile(sb)


@functools.lru_cache(maxsize=None)
def _identity_rope_tables(t):
    return np.ones((t, LANES), np.float32), np.zeros((t, LANES), np.float32), np.zeros((t, LANES), np.float32)


@functools.lru_cache(maxsize=None)
def _head_block_diag():
    hid = np.arange(Q_W) // HEAD_DIM
    return np.asarray(hid[:, None] == hid[None, :], dtype=BF16)


def kernel(x, c, ctx, c_ctx, ada_w, ada_b, norm_mix, norm_ffn, mix_w_in, mix_w_out, attn_q_norm, attn_k_norm,
           swa_sink, hy_conv_w, hy_conv_b, hy_w1, hy_b1, hy_w2, hy_b2, hy_w3, hy_b3, hy_w4, hy_freq, hy_bias_d,
           sc_conv_w, ffn_w_up, ffn_conv_w, ffn_conv_b, ffn_w_down):
    b, s, d = x.shape
    s_ctx = ctx.shape[1]
    depth = ada_w.shape[0]
    assert d == D_MODEL and b % 2 == 0 and b + 1 <= MOD_ROWS and s % 1024 == 0 and s_ctx % SUBLANES == 0

    cvec = jnp.concatenate([c, c_ctx[None, :], jnp.zeros((MOD_ROWS - b - 1, d), F32)], axis=0)
    mods = _mods(cvec, ada_w, ada_b)
    rope = [jnp.asarray(t) for t in _rope_tables(s)]
    rope_ctx = [jnp.asarray(t) for t in _identity_rope_tables(s_ctx)]
    bd = jnp.asarray(_head_block_diag())
    xc = ctx
    weights = _weights_bf16([mix_w_in, mix_w_out, ffn_w_up, ffn_w_down])

    for i in range(depth):
        last = i == depth - 1
        j = i // 2
        lat = [mods[i, :b, k * d:(k + 1) * d][:, None, :] for k in range(6)]
        cx = [jnp.broadcast_to(mods[i, b, k * d:(k + 1) * d][None, None, :], (b, 1, d)) for k in range(6)]
        w_in, w_out, w_up, w_down = (w[i] for w in weights)
        g_mix = norm_mix[i].reshape(1, d)
        g_ffn = norm_ffn[i].reshape(1, d)
        qg = jnp.tile(attn_q_norm[i], N_Q_HEADS).reshape(1, Q_W)
        kg = jnp.tile(attn_k_norm[i], N_KV_HEADS).reshape(1, KV_W)

        conv = (hy_conv_w[j], hy_conv_b[j]) if i % 2 == 0 else (sc_conv_w[j],)
        q, k, vt, *mix = _inproj(x, lat[0], lat[1], g_mix, w_in, qg, kg, bd, *rope, *conv)
        qc, kc, vct, *mixc = _inproj(xc, cx[0], cx[1], g_mix, w_in, qg, kg, bd, *rope_ctx, *conv)

        bound = _score_bound(attn_q_norm[i], attn_k_norm[i])
        if i % 2 == 0:
            fargs = (hy_w1[j], hy_b1[j], hy_w2[j], hy_b2[j], hy_w3[j], hy_b3[j], hy_w4[j], hy_freq[j])
            o_attn = _dense_attention(q, k, vt, kc, vct, bound=bound)
            kf, ks = _implicit_filter(s, *fargs)
            o_conv = _long_conv_mixer(*mix, kf, ks, hy_bias_d[j])
            if not last:
                oc_attn = _dense_attention(qc, kc, vct, bound=bound)
                kfc, ksc = _implicit_filter(s_ctx, *fargs)
                oc_conv = _small_conv_mixer(*mixc, kfc, ksc, hy_bias_d[j])
        else:
            sink = swa_sink[j].reshape(N_KV_HEADS, GROUP)
            o_attn = _banded_attention(q, k, vt, kc, vct, sink, bound)
            o_conv = mix[0]
            if not last:
                oc_attn = _dense_attention(qc, kc, vct, sink=sink)
                oc_conv = mixc[0]

        ffn = (g_ffn, w_up, ffn_conv_w[i], ffn_conv_b[i], w_down)
        x = _mixer_out_ffn(x, o_attn, o_conv, w_out, lat[2], lat[3], lat[4], lat[5], *ffn)
        if not last:
            xc = _mixer_out_ffn(xc, oc_attn, oc_conv, w_out, cx[2], cx[3], cx[4], cx[5], *ffn)
    return x
```
